```python
import jax, jax.numpy as jnp
from jax import lax
import numpy as np

D_MODEL = 1024
BATCH = 8
SEQ = 4096
DEPTH = 2

EPS = 1e-6
CONV_WIDTH = 3
A_HEADS = 8
A_HEAD_DIM = 64
A_WIDTH = A_HEADS * A_HEAD_DIM
POOL_WINDOWS = (2, 4, 8, 16)
B_GROUPS = len(POOL_WINDOWS)
B_GROUP_DIM = 128
B_WIDTH = B_GROUPS * B_GROUP_DIM
EVEN_IN = 3 * A_WIDTH + B_WIDTH
EVEN_MIX = A_WIDTH + B_WIDTH
CHUNK = 128
C_HEADS = 8
C_HEAD_DIM = 128
C_WIDTH = C_HEADS * C_HEAD_DIM
D_FF = 2816

N_EVEN = (DEPTH + 1) // 2
N_ODD = DEPTH // 2

kernel_name = "hybrid_conv_pool_sgu_trunk"


def rms_norm(x, g):
    x32 = x.astype(jnp.float32)
    y = x32 * lax.rsqrt(jnp.mean(x32 * x32, axis=-1, keepdims=True) + EPS)
    return y.astype(x.dtype) * g


def causal_dwconv3(x, w):
    s = x.shape[1]
    xp = jnp.pad(x, ((0, 0), (CONV_WIDTH - 1, 0), (0, 0)))
    y = xp[:, 0:s] * w[0]
    for k in range(1, CONV_WIDTH):
        y = y + xp[:, k:k + s] * w[k]
    return y


def short_gated_conv(gate_b, gate_c, val, conv_w):
    return gate_b * causal_dwconv3(gate_c * val, conv_w)


def multiscale_pool(z, w_pool, pool_scale):
    b, s, _ = z.shape
    z32 = z.astype(jnp.float32).reshape(b, s, B_GROUPS, B_GROUP_DIM)
    cs = jnp.cumsum(z32, axis=1)
    pos = jnp.arange(1, s + 1, dtype=jnp.float32)[None, :, None]
    outs = []
    for g, w in enumerate(POOL_WINDOWS):
        csg = cs[:, :, g]
        shifted = jnp.pad(csg, ((0, 0), (w, 0), (0, 0)))[:, :s]
        count = jnp.minimum(pos, jnp.float32(w))
        outs.append((csg - shifted) / count - z32[:, :, g])
    pooled = jnp.stack(outs, axis=2).astype(z.dtype)
    mixed = jnp.einsum('bsgc,gcd->bsgd', pooled, w_pool)
    return mixed.reshape(b, s, B_WIDTH) * pool_scale


def chunked_spatial_gating(u, v, sgu_norm, w_spatial, b_spatial):
    b, s, _ = u.shape
    n = s // CHUNK
    vn = rms_norm(v, sgu_norm).reshape(b, n, CHUNK, C_HEADS, C_HEAD_DIM)
    mask = jnp.tril(jnp.ones((CHUNK, CHUNK), dtype=w_spatial.dtype))
    ws = w_spatial * mask
    gate = jnp.einsum('hts,bnshc->bnthc', ws, vn) + b_spatial.T[None, None, :, :, None]
    return u * gate.reshape(b, s, C_WIDTH)


def gated_conv_ffn(x, w_gate, w_up, conv_w, conv_b, w_down):
    g = jnp.einsum('bsd,df->bsf', x, w_gate)
    g = causal_dwconv3(g, conv_w) + conv_b
    up = jnp.einsum('bsd,df->bsf', x, w_up)
    return jnp.einsum('bsf,fd->bsd', jax.nn.silu(g) * up, w_down)


def _fwd_setup_inputs(seed: int = 0) -> dict:
    key = jax.random.key(seed)
    ks = jax.random.split(key, 20)
    f32 = jnp.float32
    nrm = lambda k, shape, s: jax.random.normal(k, shape, f32) * s
    return {
        "x": nrm(ks[0], (BATCH, SEQ, D_MODEL), 1.0),
        "norm_mix": 1.0 + nrm(ks[1], (DEPTH, D_MODEL), 0.05),
        "norm_ffn": 1.0 + nrm(ks[2], (DEPTH, D_MODEL), 0.05),
        "final_norm": 1.0 + nrm(ks[3], (D_MODEL,), 0.05),
        "w_in_even": nrm(ks[4], (N_EVEN, D_MODEL, EVEN_IN), D_MODEL ** -0.5),
        "conv_a": nrm(ks[5], (N_EVEN, CONV_WIDTH, A_WIDTH), CONV_WIDTH ** -0.5),
        "w_pool": nrm(ks[6], (N_EVEN, B_GROUPS, B_GROUP_DIM, B_GROUP_DIM), B_GROUP_DIM ** -0.5),
        "pool_scale": 1.0 + nrm(ks[7], (N_EVEN, B_WIDTH), 0.1),
        "w_out_even": nrm(ks[8], (N_EVEN, EVEN_MIX, D_MODEL), EVEN_MIX ** -0.5),
        "w_in_odd": nrm(ks[9], (N_ODD, D_MODEL, 2 * C_WIDTH), D_MODEL ** -0.5),
        "sgu_norm": 1.0 + nrm(ks[10], (N_ODD, C_WIDTH), 0.05),
        "w_spatial": nrm(ks[11], (N_ODD, C_HEADS, CHUNK, CHUNK), CHUNK ** -0.5),
        "b_spatial": 1.0 + nrm(ks[12], (N_ODD, C_HEADS, CHUNK), 0.05),
        "w_out_odd": nrm(ks[13], (N_ODD, C_WIDTH, D_MODEL), C_WIDTH ** -0.5),
        "w_ffn_gate": nrm(ks[14], (DEPTH, D_MODEL, D_FF), D_MODEL ** -0.5),
        "w_ffn_up": nrm(ks[15], (DEPTH, D_MODEL, D_FF), D_MODEL ** -0.5),
        "conv_ffn": nrm(ks[16], (DEPTH, CONV_WIDTH, D_FF), CONV_WIDTH ** -0.5),
        "b_conv_ffn": nrm(ks[17], (DEPTH, D_FF), 0.02),
        "w_ffn_down": nrm(ks[18], (DEPTH, D_FF, D_MODEL), D_FF ** -0.5),
    }


def _fwd_reference(x, norm_mix, norm_ffn, final_norm, w_in_even, conv_a, w_pool, pool_scale,
              w_out_even, w_in_odd, sgu_norm, w_spatial, b_spatial, w_out_odd,
              w_ffn_gate, w_ffn_up, conv_ffn, b_conv_ffn, w_ffn_down):
    h = x
    for layer in range(DEPTH):
        xn = rms_norm(h, norm_mix[layer])
        if layer % 2 == 0:
            i = layer // 2
            proj = jnp.einsum('bsd,de->bse', xn, w_in_even[i])
            a_b = proj[..., 0:A_WIDTH]
            a_c = proj[..., A_WIDTH:2 * A_WIDTH]
            a_v = proj[..., 2 * A_WIDTH:3 * A_WIDTH]
            z_b = proj[..., 3 * A_WIDTH:]
            y_a = short_gated_conv(a_b, a_c, a_v, conv_a[i])
            y_b = multiscale_pool(z_b, w_pool[i], pool_scale[i])
            mix = jnp.concatenate([y_a, y_b], axis=-1)
            h = h + jnp.einsum('bse,ed->bsd', mix, w_out_even[i])
        else:
            i = layer // 2
            proj = jax.nn.gelu(jnp.einsum('bsd,de->bse', xn, w_in_odd[i]), approximate=False)
            u = proj[..., :C_WIDTH]
            v = proj[..., C_WIDTH:]
            mix = chunked_spatial_gating(u, v, sgu_norm[i], w_spatial[i], b_spatial[i])
            h = h + jnp.einsum('bse,ed->bsd', mix, w_out_odd[i])
        hn = rms_norm(h, norm_ffn[layer])
        h = h + gated_conv_ffn(hn, w_ffn_gate[layer], w_ffn_up[layer], conv_ffn[layer],
                               b_conv_ffn[layer], w_ffn_down[layer])
    return rms_norm(h, final_norm)


import jax as _jax
import jax.numpy as _jnp

TWIN_FORMAT = 'train_step'
FWD_PARAMS = ['x', 'norm_mix', 'norm_ffn', 'final_norm', 'w_in_even', 'conv_a', 'w_pool', 'pool_scale', 'w_out_even', 'w_in_odd', 'sgu_norm', 'w_spatial', 'b_spatial', 'w_out_odd', 'w_ffn_gate', 'w_ffn_up', 'conv_ffn', 'b_conv_ffn', 'w_ffn_down']
TWIN_WEIGHTS = ['norm_mix', 'norm_ffn', 'final_norm', 'w_in_even', 'conv_a', 'w_pool', 'pool_scale', 'w_out_even', 'w_in_odd', 'sgu_norm', 'w_spatial', 'b_spatial', 'w_out_odd', 'w_ffn_gate', 'w_ffn_up', 'conv_ffn', 'b_conv_ffn', 'w_ffn_down']
TWIN_DIFF_INPUT = 'x'
TWIN_INPUTS = ['x', 'norm_mix', 'norm_ffn', 'final_norm', 'w_in_even', 'conv_a', 'w_pool', 'pool_scale', 'w_out_even', 'w_in_odd', 'sgu_norm', 'w_spatial', 'b_spatial', 'w_out_odd', 'w_ffn_gate', 'w_ffn_up', 'conv_ffn', 'b_conv_ffn', 'w_ffn_down', 'loss_target', 'm_norm_mix', 'm_norm_ffn', 'm_final_norm', 'm_w_in_even', 'm_conv_a', 'm_w_pool', 'm_pool_scale', 'm_w_out_even', 'm_w_in_odd', 'm_sgu_norm', 'm_w_spatial', 'm_b_spatial', 'm_w_out_odd', 'm_w_ffn_gate', 'm_w_ffn_up', 'm_conv_ffn', 'm_b_conv_ffn', 'm_w_ffn_down', 'v_norm_mix', 'v_norm_ffn', 'v_final_norm', 'v_w_in_even', 'v_conv_a', 'v_w_pool', 'v_pool_scale', 'v_w_out_even', 'v_w_in_odd', 'v_sgu_norm', 'v_w_spatial', 'v_b_spatial', 'v_w_out_odd', 'v_w_ffn_gate', 'v_w_ffn_up', 'v_conv_ffn', 'v_b_conv_ffn', 'v_w_ffn_down']
TWIN_OUTPUTS = ['loss', 'grad_x', 'grad_norm_mix', 'grad_norm_ffn', 'grad_final_norm', 'grad_w_in_even', 'grad_conv_a', 'grad_w_pool', 'grad_pool_scale', 'grad_w_out_even', 'grad_w_in_odd', 'grad_sgu_norm', 'grad_w_spatial', 'grad_b_spatial', 'grad_w_out_odd', 'grad_w_ffn_gate', 'grad_w_ffn_up', 'grad_conv_ffn', 'grad_b_conv_ffn', 'grad_w_ffn_down', 'delta_norm_mix', 'delta_norm_ffn', 'delta_final_norm', 'delta_w_in_even', 'delta_conv_a', 'delta_w_pool', 'delta_pool_scale', 'delta_w_out_even', 'delta_w_in_odd', 'delta_sgu_norm', 'delta_w_spatial', 'delta_b_spatial', 'delta_w_out_odd', 'delta_w_ffn_gate', 'delta_w_ffn_up', 'delta_conv_ffn', 'delta_b_conv_ffn', 'delta_w_ffn_down', 'new_m_norm_mix', 'new_m_norm_ffn', 'new_m_final_norm', 'new_m_w_in_even', 'new_m_conv_a', 'new_m_w_pool', 'new_m_pool_scale', 'new_m_w_out_even', 'new_m_w_in_odd', 'new_m_sgu_norm', 'new_m_w_spatial', 'new_m_b_spatial', 'new_m_w_out_odd', 'new_m_w_ffn_gate', 'new_m_w_ffn_up', 'new_m_conv_ffn', 'new_m_b_conv_ffn', 'new_m_w_ffn_down', 'new_v_norm_mix', 'new_v_norm_ffn', 'new_v_final_norm', 'new_v_w_in_even', 'new_v_conv_a', 'new_v_w_pool', 'new_v_pool_scale', 'new_v_w_out_even', 'new_v_w_in_odd', 'new_v_sgu_norm', 'new_v_w_spatial', 'new_v_b_spatial', 'new_v_w_out_odd', 'new_v_w_ffn_gate', 'new_v_w_ffn_up', 'new_v_conv_ffn', 'new_v_b_conv_ffn', 'new_v_w_ffn_down']
TWIN_LEAF_KINDS = {'loss': 'loss', 'grad_x': 'grad_x', 'grad_norm_mix': 'grad_w', 'grad_norm_ffn': 'grad_w', 'grad_final_norm': 'grad_w', 'grad_w_in_even': 'grad_w', 'grad_conv_a': 'grad_w', 'grad_w_pool': 'grad_w', 'grad_pool_scale': 'grad_w', 'grad_w_out_even': 'grad_w', 'grad_w_in_odd': 'grad_w', 'grad_sgu_norm': 'grad_w', 'grad_w_spatial': 'grad_w', 'grad_b_spatial': 'grad_w', 'grad_w_out_odd': 'grad_w', 'grad_w_ffn_gate': 'grad_w', 'grad_w_ffn_up': 'grad_w', 'grad_conv_ffn': 'grad_w', 'grad_b_conv_ffn': 'grad_w', 'grad_w_ffn_down': 'grad_w', 'delta_norm_mix': 'delta_w', 'delta_norm_ffn': 'delta_w', 'delta_final_norm': 'delta_w', 'delta_w_in_even': 'delta_w', 'delta_conv_a': 'delta_w', 'delta_w_pool': 'delta_w', 'delta_pool_scale': 'delta_w', 'delta_w_out_even': 'delta_w', 'delta_w_in_odd': 'delta_w', 'delta_sgu_norm': 'delta_w', 'delta_w_spatial': 'delta_w', 'delta_b_spatial': 'delta_w', 'delta_w_out_odd': 'delta_w', 'delta_w_ffn_gate': 'delta_w', 'delta_w_ffn_up': 'delta_w', 'delta_conv_ffn': 'delta_w', 'delta_b_conv_ffn': 'delta_w', 'delta_w_ffn_down': 'delta_w', 'new_m_norm_mix': 'new_m', 'new_m_norm_ffn': 'new_m', 'new_m_final_norm': 'new_m', 'new_m_w_in_even': 'new_m', 'new_m_conv_a': 'new_m', 'new_m_w_pool': 'new_m', 'new_m_pool_scale': 'new_m', 'new_m_w_out_even': 'new_m', 'new_m_w_in_odd': 'new_m', 'new_m_sgu_norm': 'new_m', 'new_m_w_spatial': 'new_m', 'new_m_b_spatial': 'new_m', 'new_m_w_out_odd': 'new_m', 'new_m_w_ffn_gate': 'new_m', 'new_m_w_ffn_up': 'new_m', 'new_m_conv_ffn': 'new_m', 'new_m_b_conv_ffn': 'new_m', 'new_m_w_ffn_down': 'new_m', 'new_v_norm_mix': 'new_v', 'new_v_norm_ffn': 'new_v', 'new_v_final_norm': 'new_v', 'new_v_w_in_even': 'new_v', 'new_v_conv_a': 'new_v', 'new_v_w_pool': 'new_v', 'new_v_pool_scale': 'new_v', 'new_v_w_out_even': 'new_v', 'new_v_w_in_odd': 'new_v', 'new_v_sgu_norm': 'new_v', 'new_v_w_spatial': 'new_v', 'new_v_b_spatial': 'new_v', 'new_v_w_out_odd': 'new_v', 'new_v_w_ffn_gate': 'new_v', 'new_v_w_ffn_up': 'new_v', 'new_v_conv_ffn': 'new_v', 'new_v_b_conv_ffn': 'new_v', 'new_v_w_ffn_down': 'new_v'}


def _forward(args):
    return _fwd_reference(*[args[k] for k in FWD_PARAMS])


def _output_shape():
    out = _jax.eval_shape(lambda: _forward(_fwd_setup_inputs(0)))
    return out.shape, out.dtype

N_MICROBATCH = 1
ADAM_LR = 0.001
ADAM_B1 = 0.9
ADAM_B2 = 0.999
ADAM_EPS = 1e-08
ADAM_WD = 0.01
ADAM_STEP = 10
PER_EXAMPLE_BATCH_AXIS = {'x': 0, 'loss_target': 0}
SHARED_INPUTS = []
_WEIGHT_DTYPES = {'norm_mix': _jnp.float32, 'norm_ffn': _jnp.float32, 'final_norm': _jnp.float32, 'w_in_even': _jnp.float32, 'conv_a': _jnp.float32, 'w_pool': _jnp.float32, 'pool_scale': _jnp.float32, 'w_out_even': _jnp.float32, 'w_in_odd': _jnp.float32, 'sgu_norm': _jnp.float32, 'w_spatial': _jnp.float32, 'b_spatial': _jnp.float32, 'w_out_odd': _jnp.float32, 'w_ffn_gate': _jnp.float32, 'w_ffn_up': _jnp.float32, 'conv_ffn': _jnp.float32, 'b_conv_ffn': _jnp.float32, 'w_ffn_down': _jnp.float32}
MOMENT_SCALE = {'norm_mix': 1.689689e-01, 'norm_ffn': 1.060169e-01, 'final_norm': 3.201108e+01, 'w_in_even': 1.532967e-01, 'conv_a': 1.565446e-01, 'w_pool': 1.408487e-01, 'pool_scale': 1.469043e-01, 'w_out_even': 1.489401e-01, 'w_in_odd': 7.522448e-02, 'sgu_norm': 5.100395e-02, 'w_spatial': 5.118431e-02, 'b_spatial': 7.250093e-02, 'w_out_odd': 9.942670e-02, 'w_ffn_gate': 4.432708e-02, 'w_ffn_up': 4.306572e-02, 'conv_ffn': 4.444356e-02, 'b_conv_ffn': 4.309474e-02, 'w_ffn_down': 7.186878e-02}


def _to_microbatches(a, axis):
    t = _jnp.moveaxis(a, axis, 0)
    t = t.reshape((N_MICROBATCH, t.shape[0] // N_MICROBATCH) + t.shape[1:])
    return _jnp.moveaxis(t, 1, axis + 1)


def setup_inputs(seed: int = 0) -> dict:
    inp = _fwd_setup_inputs(seed)
    key = _jax.random.fold_in(_jax.random.key(seed), 7919)
    shape, _ = _output_shape()
    out = dict(inp)
    out["loss_target"] = _jax.random.normal(_jax.random.fold_in(key, 0), shape, _jnp.float32)
    for i, name in enumerate(TWIN_WEIGHTS):
        w = inp[name].astype(_jnp.float32)
        if MOMENT_SCALE is None:
            s = _jnp.sqrt(_jnp.mean(_jnp.square(w)) + 1e-30)
        else:
            s = MOMENT_SCALE[name]
        km, kv = _jax.random.split(_jax.random.fold_in(key, i + 1))
        out[name] = w
        out["m_" + name] = s * _jax.random.normal(km, w.shape, _jnp.float32)
        out["v_" + name] = (s * s) * _jax.random.uniform(kv, w.shape, _jnp.float32, 0.5, 1.5)
    if N_MICROBATCH > 1:
        for name, axis in PER_EXAMPLE_BATCH_AXIS.items():
            out[name] = _to_microbatches(out[name], axis)
    return {'x': out['x'], 'norm_mix': out['norm_mix'], 'norm_ffn': out['norm_ffn'], 'final_norm': out['final_norm'], 'w_in_even': out['w_in_even'], 'conv_a': out['conv_a'], 'w_pool': out['w_pool'], 'pool_scale': out['pool_scale'], 'w_out_even': out['w_out_even'], 'w_in_odd': out['w_in_odd'], 'sgu_norm': out['sgu_norm'], 'w_spatial': out['w_spatial'], 'b_spatial': out['b_spatial'], 'w_out_odd': out['w_out_odd'], 'w_ffn_gate': out['w_ffn_gate'], 'w_ffn_up': out['w_ffn_up'], 'conv_ffn': out['conv_ffn'], 'b_conv_ffn': out['b_conv_ffn'], 'w_ffn_down': out['w_ffn_down'], 'loss_target': out['loss_target'], 'm_norm_mix': out['m_norm_mix'], 'm_norm_ffn': out['m_norm_ffn'], 'm_final_norm': out['m_final_norm'], 'm_w_in_even': out['m_w_in_even'], 'm_conv_a': out['m_conv_a'], 'm_w_pool': out['m_w_pool'], 'm_pool_scale': out['m_pool_scale'], 'm_w_out_even': out['m_w_out_even'], 'm_w_in_odd': out['m_w_in_odd'], 'm_sgu_norm': out['m_sgu_norm'], 'm_w_spatial': out['m_w_spatial'], 'm_b_spatial': out['m_b_spatial'], 'm_w_out_odd': out['m_w_out_odd'], 'm_w_ffn_gate': out['m_w_ffn_gate'], 'm_w_ffn_up': out['m_w_ffn_up'], 'm_conv_ffn': out['m_conv_ffn'], 'm_b_conv_ffn': out['m_b_conv_ffn'], 'm_w_ffn_down': out['m_w_ffn_down'], 'v_norm_mix': out['v_norm_mix'], 'v_norm_ffn': out['v_norm_ffn'], 'v_final_norm': out['v_final_norm'], 'v_w_in_even': out['v_w_in_even'], 'v_conv_a': out['v_conv_a'], 'v_w_pool': out['v_w_pool'], 'v_pool_scale': out['v_pool_scale'], 'v_w_out_even': out['v_w_out_even'], 'v_w_in_odd': out['v_w_in_odd'], 'v_sgu_norm': out['v_sgu_norm'], 'v_w_spatial': out['v_w_spatial'], 'v_b_spatial': out['v_b_spatial'], 'v_w_out_odd': out['v_w_out_odd'], 'v_w_ffn_gate': out['v_w_ffn_gate'], 'v_w_ffn_up': out['v_w_ffn_up'], 'v_conv_ffn': out['v_conv_ffn'], 'v_b_conv_ffn': out['v_b_conv_ffn'], 'v_w_ffn_down': out['v_w_ffn_down']}


def _loss(weights, diff, rest, loss_target):
    with _jax.named_scope("forward"):
        args = {**rest, TWIN_DIFF_INPUT: diff, **{k: w.astype(_WEIGHT_DTYPES[k]) for k, w in weights.items()}}
        y = _forward(args)
    with _jax.named_scope("loss_head"):
        err = _jnp.square(y.astype(_jnp.float32) - loss_target)
        return 0.5 * _jnp.sum(_jnp.mean(err, axis=-1)) if err.ndim else 0.5 * err


def _adamw(w, g, m, v):
    m = ADAM_B1 * m + (1.0 - ADAM_B1) * g
    v = ADAM_B2 * v + (1.0 - ADAM_B2) * _jnp.square(g)
    m_hat = m / (1.0 - ADAM_B1 ** ADAM_STEP)
    v_hat = v / (1.0 - ADAM_B2 ** ADAM_STEP)
    delta = -ADAM_LR * (m_hat / (_jnp.sqrt(v_hat) + ADAM_EPS) + ADAM_WD * w)
    return delta, m, v


def reference(x, norm_mix, norm_ffn, final_norm, w_in_even, conv_a, w_pool, pool_scale, w_out_even, w_in_odd, sgu_norm, w_spatial, b_spatial, w_out_odd, w_ffn_gate, w_ffn_up, conv_ffn, b_conv_ffn, w_ffn_down, loss_target, m_norm_mix, m_norm_ffn, m_final_norm, m_w_in_even, m_conv_a, m_w_pool, m_pool_scale, m_w_out_even, m_w_in_odd, m_sgu_norm, m_w_spatial, m_b_spatial, m_w_out_odd, m_w_ffn_gate, m_w_ffn_up, m_conv_ffn, m_b_conv_ffn, m_w_ffn_down, v_norm_mix, v_norm_ffn, v_final_norm, v_w_in_even, v_conv_a, v_w_pool, v_pool_scale, v_w_out_even, v_w_in_odd, v_sgu_norm, v_w_spatial, v_b_spatial, v_w_out_odd, v_w_ffn_gate, v_w_ffn_up, v_conv_ffn, v_b_conv_ffn, v_w_ffn_down):
    given = dict(x=x, norm_mix=norm_mix, norm_ffn=norm_ffn, final_norm=final_norm, w_in_even=w_in_even, conv_a=conv_a, w_pool=w_pool, pool_scale=pool_scale, w_out_even=w_out_even, w_in_odd=w_in_odd, sgu_norm=sgu_norm, w_spatial=w_spatial, b_spatial=b_spatial, w_out_odd=w_out_odd, w_ffn_gate=w_ffn_gate, w_ffn_up=w_ffn_up, conv_ffn=conv_ffn, b_conv_ffn=b_conv_ffn, w_ffn_down=w_ffn_down, loss_target=loss_target, m_norm_mix=m_norm_mix, m_norm_ffn=m_norm_ffn, m_final_norm=m_final_norm, m_w_in_even=m_w_in_even, m_conv_a=m_conv_a, m_w_pool=m_w_pool, m_pool_scale=m_pool_scale, m_w_out_even=m_w_out_even, m_w_in_odd=m_w_in_odd, m_sgu_norm=m_sgu_norm, m_w_spatial=m_w_spatial, m_b_spatial=m_b_spatial, m_w_out_odd=m_w_out_odd, m_w_ffn_gate=m_w_ffn_gate, m_w_ffn_up=m_w_ffn_up, m_conv_ffn=m_conv_ffn, m_b_conv_ffn=m_b_conv_ffn, m_w_ffn_down=m_w_ffn_down, v_norm_mix=v_norm_mix, v_norm_ffn=v_norm_ffn, v_final_norm=v_final_norm, v_w_in_even=v_w_in_even, v_conv_a=v_conv_a, v_w_pool=v_w_pool, v_pool_scale=v_pool_scale, v_w_out_even=v_w_out_even, v_w_in_odd=v_w_in_odd, v_sgu_norm=v_sgu_norm, v_w_spatial=v_w_spatial, v_b_spatial=v_b_spatial, v_w_out_odd=v_w_out_odd, v_w_ffn_gate=v_w_ffn_gate, v_w_ffn_up=v_w_ffn_up, v_conv_ffn=v_conv_ffn, v_b_conv_ffn=v_b_conv_ffn, v_w_ffn_down=v_w_ffn_down)
    weights = {n: given[n] for n in TWIN_WEIGHTS}
    shared = {n: given[n] for n in SHARED_INPUTS}
    per_example = {n: given[n] for n in ['x']}
    grad_fn = _jax.value_and_grad(_loss, argnums=(0, 1))

    def one_microbatch(ex, loss_target):
        ex = dict(ex)
        diff = ex.pop(TWIN_DIFF_INPUT)
        return grad_fn(weights, diff, {**shared, **ex}, loss_target)

    if N_MICROBATCH == 1:
        loss, (grad_w, grad_x) = one_microbatch(per_example, given["loss_target"])
    else:
        def body(carry, xs):
            loss_sum, grad_sum = carry
            l_k, (gw_k, gx_k) = one_microbatch(xs[0], xs[1])
            with _jax.named_scope("update"):
                return (loss_sum + l_k, _jax.tree.map(_jnp.add, grad_sum, gw_k)), gx_k

        init = (_jnp.zeros((), _jnp.float32), _jax.tree.map(_jnp.zeros_like, weights))
        (loss, grad_w), grad_x = _jax.lax.scan(body, init, (per_example, given["loss_target"]))
    with _jax.named_scope("update"):
        delta_w, new_m, new_v = {}, {}, {}
        for n in TWIN_WEIGHTS:
            delta_w[n], new_m[n], new_v[n] = _adamw(weights[n], grad_w[n], given["m_" + n], given["v_" + n])
    return (loss, grad_x, *[grad_w[n] for n in TWIN_WEIGHTS], *[delta_w[n] for n in TWIN_WEIGHTS],
            *[new_m[n] for n in TWIN_WEIGHTS], *[new_v[n] for n in TWIN_WEIGHTS])
```

```python
import functools

import jax
import jax.numpy as jnp
from jax import lax
from jax.experimental import pallas as pl
from jax.experimental.pallas import tpu as pltpu

F32, BF16 = jnp.float32, jnp.bfloat16
EPS = 1e-6
WINDOWS = (2, 4, 8, 16)
HALO = 16
CHUNK = 128
N_DEV = 8
AXES = ("x", "y", "c")
MESH = pl.DeviceIdType.MESH
VMEM_LIMIT = 56 * 2**20
LANE = 128
ADAM_LR, ADAM_B1, ADAM_B2, ADAM_EPS, ADAM_WD, ADAM_STEP = 0.001, 0.9, 0.999, 1e-08, 0.01, 10
INV_SQRT2 = 0.7071067811865476
INV_SQRT2PI = 0.3989422804014327


def _pcall(body, **kw):
    return pl.pallas_call(body, **kw)


def _params(*sem):
    return pltpu.CompilerParams(dimension_semantics=sem, vmem_limit_bytes=VMEM_LIMIT)


def _whole(shape):
    return pl.BlockSpec(shape, lambda *_: (0,) * len(shape))


def _nn(a, b):
    return jnp.dot(a, b, preferred_element_type=F32)


def _nt(a, b):
    return lax.dot_general(a, b, (((1,), (1,)), ((), ())), preferred_element_type=F32)


def _tn(a, b):
    return lax.dot_general(a, b, (((0,), (0,)), ((), ())), preferred_element_type=F32)


def _rms(x, gain):
    r = lax.rsqrt(jnp.mean(x * x, axis=-1, keepdims=True) + EPS)
    return x * r * gain


def _rms_bwd(dy, x, gain):
    r = lax.rsqrt(jnp.mean(x * x, axis=-1, keepdims=True) + EPS)
    xh = x * r
    dgain = jnp.sum(dy * xh, axis=0, keepdims=True)
    dxh = dy * gain
    dx = r * (dxh - xh * jnp.mean(dxh * xh, axis=-1, keepdims=True))
    return dx, dgain


def _gelu(x):
    return 0.5 * x * (1.0 + lax.erf(x * INV_SQRT2))


def _gelu_grad(x):
    return 0.5 * (1.0 + lax.erf(x * INV_SQRT2)) + x * jnp.exp(-0.5 * x * x) * INV_SQRT2PI


def _acc(ref, val, first):
    @pl.when(first)
    def _():
        ref[...] = val

    @pl.when(jnp.logical_not(first))
    def _():
        ref[...] += val


def _counts(row0, tm, w):
    pos1 = (row0 + lax.broadcasted_iota(jnp.int32, (tm, 1), 0) + 1).astype(F32)
    return jnp.minimum(pos1, float(w))


def _even_fwd(x, gmix, gffn, win, conva, wpool, pscale, wout, tm):
    s, d = x.shape
    e = win.shape[1]
    aw = e // 4

    def body(x_ref, gmix_ref, gffn_ref, win_ref, ca_ref, wp_ref, ps_ref, wout_ref,
             xn_ref, proj_ref, cq_ref, pooled_ref, mix_ref, h_ref, hn_ref, qbuf, zbuf):
        i = pl.program_id(0)

        @pl.when(i == 0)
        def _():
            qbuf[0:HALO, :] = jnp.zeros((HALO, aw), F32)
            zbuf[0:HALO, :] = jnp.zeros((HALO, aw), F32)

        xv = x_ref[...]
        xn = _rms(xv, gmix_ref[...]).astype(BF16)
        xn_ref[...] = xn
        proj = _nn(xn, win_ref[...])
        proj_ref[...] = proj.astype(BF16)
        a_b, a_c, a_v, z = (proj[:, k * aw:(k + 1) * aw] for k in range(4))
        q = a_c * a_v
        qbuf[HALO:HALO + tm, :] = q
        cq = ca_ref[2:3, :] * q + ca_ref[1:2, :] * qbuf[pl.ds(HALO - 1, tm), :] + ca_ref[0:1, :] * qbuf[pl.ds(HALO - 2, tm), :]
        cq_ref[...] = cq.astype(BF16)
        y_a = a_b * cq
        zbuf[HALO:HALO + tm, :] = z
        ys = []
        for g, w in enumerate(WINDOWS):
            cols = slice(g * LANE, (g + 1) * LANE)
            acc = z[:, cols]
            for m in range(1, w):
                acc = acc + zbuf[pl.ds(HALO - m, tm), cols]
            pooled = (acc / _counts(i * tm, tm, w) - z[:, cols]).astype(BF16)
            pooled_ref[:, cols] = pooled
            ys.append(_nn(pooled, wp_ref[g]))
        y_b = jnp.concatenate(ys, axis=1) * ps_ref[...]
        mix = jnp.concatenate([y_a, y_b], axis=1).astype(BF16)
        mix_ref[...] = mix
        h = xv + _nn(mix, wout_ref[...])
        h_ref[...] = h
        hn_ref[...] = _rms(h, gffn_ref[...]).astype(BF16)
        qbuf[0:HALO, :] = qbuf[tm:tm + HALO, :]
        zbuf[0:HALO, :] = zbuf[tm:tm + HALO, :]

    row = lambda c: pl.BlockSpec((tm, c), lambda i: (i, 0))
    return _pcall(
        body, name="even_fwd", grid=(s // tm,),
        in_specs=[row(d), _whole((1, d)), _whole((1, d)), _whole(win.shape), _whole(conva.shape), _whole(wpool.shape),
                  _whole(pscale.shape), _whole(wout.shape)],
        out_specs=[row(d), row(e), row(aw), row(aw), row(d), row(d), row(d)],
        out_shape=[jax.ShapeDtypeStruct((s, d), BF16), jax.ShapeDtypeStruct((s, e), BF16), jax.ShapeDtypeStruct((s, aw), BF16),
                   jax.ShapeDtypeStruct((s, aw), BF16), jax.ShapeDtypeStruct((s, d), BF16), jax.ShapeDtypeStruct((s, d), F32),
                   jax.ShapeDtypeStruct((s, d), BF16)],
        scratch_shapes=[pltpu.VMEM((tm + HALO, aw), F32), pltpu.VMEM((tm + HALO, aw), F32)],
        compiler_params=_params("arbitrary"),
    )(x, gmix, gffn, win, conva, wpool, pscale, wout)


def _ffn_fwd1(hn, wg, wu, cw, cb, tm, tn, name):
    s, d = hn.shape
    f = wg.shape[1]

    def body(hn_ref, wg_ref, wu_ref, cw_ref, cb_ref, g_ref, gc_ref, up_ref, a_ref, gbuf):
        i = pl.program_id(1)

        @pl.when(i == 0)
        def _():
            gbuf[0:HALO, :] = jnp.zeros((HALO, tn), F32)

        hv = hn_ref[...]
        g = _nn(hv, wg_ref[...])
        g_ref[...] = g.astype(BF16)
        gbuf[HALO:HALO + tm, :] = g
        gc = (cw_ref[2:3, :] * g + cw_ref[1:2, :] * gbuf[pl.ds(HALO - 1, tm), :]
              + cw_ref[0:1, :] * gbuf[pl.ds(HALO - 2, tm), :] + cb_ref[...])
        gc_ref[...] = gc.astype(BF16)
        up = _nn(hv, wu_ref[...])
        up_ref[...] = up.astype(BF16)
        a_ref[...] = (gc * jax.nn.sigmoid(gc) * up).astype(BF16)
        gbuf[0:HALO, :] = gbuf[tm:tm + HALO, :]

    tile = pl.BlockSpec((tm, tn), lambda j, i: (i, j))
    wcol = lambda r: pl.BlockSpec((r, tn), lambda j, i: (0, j))
    out = jax.ShapeDtypeStruct((s, f), BF16)
    return _pcall(
        body, name=name, grid=(f // tn, s // tm),
        in_specs=[pl.BlockSpec((tm, d), lambda j, i: (i, 0)), wcol(d), wcol(d), wcol(3), wcol(1)],
        out_specs=[tile, tile, tile, tile], out_shape=[out, out, out, out],
        scratch_shapes=[pltpu.VMEM((tm + HALO, tn), F32)],
        compiler_params=_params("arbitrary", "arbitrary"),
    )(hn, wg, wu, cw, cb)


def _ffn_fwd2(a, wd, h, gain, tm, name):
    s, d = h.shape
    f = a.shape[1]

    def body(a_ref, wd_ref, h_ref, gain_ref, ho_ref, hn_ref):
        ho = h_ref[...] + _nn(a_ref[...], wd_ref[...])
        ho_ref[...] = ho
        hn_ref[...] = _rms(ho, gain_ref[...]).astype(BF16)

    row = lambda c: pl.BlockSpec((tm, c), lambda i: (i, 0))
    return _pcall(
        body, name=name, grid=(s // tm,),
        in_specs=[row(f), _whole(wd.shape), row(d), _whole((1, d))],
        out_specs=[row(d), row(d)],
        out_shape=[jax.ShapeDtypeStruct((s, d), F32), jax.ShapeDtypeStruct((s, d), BF16)],
        compiler_params=_params("arbitrary"),
    )(a, wd, h, gain)


def _odd_fwd(xn, h, win, sgu, ws, bfull, wout, gffn, tm):
    s, d = h.shape
    e = win.shape[1]
    cw = e // 2
    heads = ws.shape[0]

    def body(xn_ref, h_ref, win_ref, sgu_ref, ws_ref, b_ref, wout_ref, gffn_ref,
             pre_ref, gate_ref, mixo_ref, ho_ref, hn_ref, gbuf):
        pre = _nn(xn_ref[...], win_ref[...])
        pre_ref[...] = pre.astype(BF16)
        p = _gelu(pre)
        u, v = p[:, :cw], p[:, cw:]
        vn = _rms(v, sgu_ref[...]).astype(BF16)
        for n in range(tm // CHUNK):
            rows = slice(n * CHUNK, (n + 1) * CHUNK)
            for hd in range(heads):
                cols = slice(hd * CHUNK, (hd + 1) * CHUNK)
                gbuf[rows, cols] = _nn(ws_ref[hd], vn[rows, cols]) + b_ref[:, cols]
        gate = gbuf[...]
        gate_ref[...] = gate.astype(BF16)
        mixo = (u * gate).astype(BF16)
        mixo_ref[...] = mixo
        ho = h_ref[...] + _nn(mixo, wout_ref[...])
        ho_ref[...] = ho
        hn_ref[...] = _rms(ho, gffn_ref[...]).astype(BF16)

    row = lambda c: pl.BlockSpec((tm, c), lambda i: (i, 0))
    return _pcall(
        body, name="odd_fwd", grid=(s // tm,),
        in_specs=[row(d), row(d), _whole(win.shape), _whole(sgu.shape), _whole(ws.shape), _whole(bfull.shape),
                  _whole(wout.shape), _whole((1, d))],
        out_specs=[row(e), row(cw), row(cw), row(d), row(d)],
        out_shape=[jax.ShapeDtypeStruct((s, e), BF16), jax.ShapeDtypeStruct((s, cw), BF16), jax.ShapeDtypeStruct((s, cw), BF16),
                   jax.ShapeDtypeStruct((s, d), F32), jax.ShapeDtypeStruct((s, d), BF16)],
        scratch_shapes=[pltpu.VMEM((tm, cw), F32)],
        compiler_params=_params("arbitrary"),
    )(xn, h, win, sgu, ws, bfull, wout, gffn)


def _loss_bwd(h, gain, target, tm):
    s, d = h.shape

    def body(h_ref, gain_ref, t_ref, dh_ref, dhb_ref, dgain_ref, loss_ref):
        i = pl.program_id(0)
        hv = h_ref[...]
        gain = gain_ref[...]
        err = _rms(hv, gain) - t_ref[...]
        dy = err * (1.0 / d)
        dx, dgain = _rms_bwd(dy, hv, gain)
        dh_ref[...] = dx
        dhb_ref[...] = dx.astype(BF16)
        _acc(dgain_ref, dgain, i == 0)
        _acc(loss_ref, jnp.sum(err * err, axis=0, keepdims=True) * (0.5 / d), i == 0)

    row = pl.BlockSpec((tm, d), lambda i: (i, 0))
    return _pcall(
        body, name="loss_bwd", grid=(s // tm,),
        in_specs=[row, _whole((1, d)), row],
        out_specs=[row, row, _whole((1, d)), _whole((1, d))],
        out_shape=[jax.ShapeDtypeStruct((s, d), F32), jax.ShapeDtypeStruct((s, d), BF16), jax.ShapeDtypeStruct((1, d), F32),
                   jax.ShapeDtypeStruct((1, d), F32)],
        compiler_params=_params("arbitrary"),
    )(h, gain, target)


def _ffn_bwd1(dhb, g, gc, up, wd, cw, tm, tn, name):
    s, d = dhb.shape
    f = g.shape[1]
    ni = s // tm

    def body(dh_ref, g_ref, gc_ref, up_ref, wd_ref, cw_ref, dg_ref, dup_ref, dcw_ref, dcb_ref, ebuf):
        i = pl.program_id(1)

        @pl.when(i == 0)
        def _():
            ebuf[tm:tm + HALO, :] = jnp.zeros((HALO, tn), F32)

        da = _nt(dh_ref[...], wd_ref[...])
        gcv = gc_ref[...].astype(F32)
        sg = jax.nn.sigmoid(gcv)
        dup_ref[...] = (da * gcv * sg).astype(BF16)
        dgc = da * up_ref[...].astype(F32) * (sg * (1.0 + gcv * (1.0 - sg)))
        ebuf[0:tm, :] = dgc
        s1 = ebuf[pl.ds(1, tm), :]
        s2 = ebuf[pl.ds(2, tm), :]
        dg_ref[...] = (cw_ref[2:3, :] * dgc + cw_ref[1:2, :] * s1 + cw_ref[0:1, :] * s2).astype(BF16)
        gv = g_ref[...].astype(F32)
        for k, shifted in enumerate((s2, s1, dgc)):
            _acc(dcw_ref.at[k:k + 1, :], jnp.sum(shifted * gv, axis=0, keepdims=True), i == 0)
        _acc(dcb_ref, jnp.sum(dgc, axis=0, keepdims=True), i == 0)
        ebuf[tm:tm + HALO, :] = ebuf[0:HALO, :]

    tile = pl.BlockSpec((tm, tn), lambda j, i: (ni - 1 - i, j))
    wcol = lambda r: pl.BlockSpec((r, tn), lambda j, i: (0, j))
    out = jax.ShapeDtypeStruct((s, f), BF16)
    return _pcall(
        body, name=name, grid=(f // tn, ni),
        in_specs=[pl.BlockSpec((tm, d), lambda j, i: (ni - 1 - i, 0)), tile, tile, tile,
                  pl.BlockSpec((tn, d), lambda j, i: (j, 0)), wcol(3)],
        out_specs=[tile, tile, wcol(3), wcol(1)],
        out_shape=[out, out, jax.ShapeDtypeStruct((3, f), F32), jax.ShapeDtypeStruct((1, f), F32)],
        scratch_shapes=[pltpu.VMEM((tm + HALO, tn), F32)],
        compiler_params=_params("arbitrary", "arbitrary"),
    )(dhb, g, gc, up, wd, cw)


def _ffn_bwd2(dg, dup, wg, wu, h, gain, dh, tm, name):
    s, d = h.shape
    f = dg.shape[1]

    def body(dg_ref, dup_ref, wg_ref, wu_ref, h_ref, gain_ref, dh_ref, dho_ref, dhb_ref, dgain_ref):
        dhn = _nt(dg_ref[...], wg_ref[...]) + _nt(dup_ref[...], wu_ref[...])
        dx, dgain = _rms_bwd(dhn, h_ref[...], gain_ref[...])
        dho = dh_ref[...] + dx
        dho_ref[...] = dho
        dhb_ref[...] = dho.astype(BF16)
        _acc(dgain_ref, dgain, pl.program_id(0) == 0)

    row = lambda c: pl.BlockSpec((tm, c), lambda i: (i, 0))
    return _pcall(
        body, name=name, grid=(s // tm,),
        in_specs=[row(f), row(f), _whole(wg.shape), _whole(wu.shape), row(d), _whole((1, d)), row(d)],
        out_specs=[row(d), row(d), _whole((1, d))],
        out_shape=[jax.ShapeDtypeStruct((s, d), F32), jax.ShapeDtypeStruct((s, d), BF16), jax.ShapeDtypeStruct((1, d), F32)],
        compiler_params=_params("arbitrary"),
    )(dg, dup, wg, wu, h, gain, dh)


def _odd_bwd(dhb, dh, wout, pre, gate, ws, wst, sgu, win, h, gmix, tm):
    s, d = h.shape
    e = win.shape[1]
    cw = e // 2
    heads = ws.shape[0]
    ni = s // tm

    def body(dhb_ref, dh_ref, wout_ref, pre_ref, gate_ref, ws_ref, wst_ref, sgu_ref, win_ref, h_ref, gmix_ref,
             dpre_ref, dho_ref, dhob_ref, dgain_ref, dsgu_ref, dws_ref, db_ref, vbuf, gacc):
        i = pl.program_id(0)
        first = i == 0
        dmixo = _nt(dhb_ref[...], wout_ref[...])
        pre = pre_ref[...].astype(F32)
        p = _gelu(pre)
        u, v = p[:, :cw], p[:, cw:]
        sgu = sgu_ref[...]
        rv = lax.rsqrt(jnp.mean(v * v, axis=-1, keepdims=True) + EPS)
        vh = v * rv
        vn = (vh * sgu).astype(BF16)
        du = dmixo * gate_ref[...].astype(F32)
        dgate = dmixo * u
        dgate_b = dgate.astype(BF16)
        gsum = dgate[0:CHUNK, :]
        for n in range(1, tm // CHUNK):
            gsum = gsum + dgate[n * CHUNK:(n + 1) * CHUNK, :]
        _acc(gacc, gsum, first)
        for hd in range(heads):
            cols = slice(hd * CHUNK, (hd + 1) * CHUNK)
            dws = None
            for n in range(tm // CHUNK):
                rows = slice(n * CHUNK, (n + 1) * CHUNK)
                vbuf[rows, cols] = _nn(wst_ref[hd], dgate_b[rows, cols])
                part = _nt(dgate_b[rows, cols], vn[rows, cols])
                dws = part if dws is None else dws + part
            _acc(dws_ref.at[hd], dws, first)
        dvn = vbuf[...]
        _acc(dsgu_ref, jnp.sum(dvn * vh, axis=0, keepdims=True), first)
        dvh = dvn * sgu
        dv = rv * (dvh - vh * jnp.mean(dvh * vh, axis=-1, keepdims=True))
        dpre = (jnp.concatenate([du, dv], axis=1) * _gelu_grad(pre)).astype(BF16)
        dpre_ref[...] = dpre
        dx, dgain = _rms_bwd(_nt(dpre, win_ref[...]), h_ref[...], gmix_ref[...])
        dho = dh_ref[...] + dx
        dho_ref[...] = dho
        dhob_ref[...] = dho.astype(BF16)
        _acc(dgain_ref, dgain, first)

        @pl.when(i == ni - 1)
        def _():
            ones = jnp.ones((8, CHUNK), F32)
            for hd in range(heads):
                tot = lax.dot_general(ones, gacc[:, hd * CHUNK:(hd + 1) * CHUNK], (((1,), (1,)), ((), ())),
                                      preferred_element_type=F32, precision=lax.Precision.HIGHEST)
                db_ref[hd:hd + 1, :] = tot[0:1, :]

    row = lambda c: pl.BlockSpec((tm, c), lambda i: (i, 0))
    return _pcall(
        body, name="odd_bwd", grid=(ni,),
        in_specs=[row(d), row(d), _whole(wout.shape), row(e), row(cw), _whole(ws.shape), _whole(wst.shape), _whole(sgu.shape),
                  _whole(win.shape), row(d), _whole((1, d))],
        out_specs=[row(e), row(d), row(d), _whole((1, d)), _whole((1, cw)), _whole(ws.shape), _whole((heads, CHUNK))],
        out_shape=[jax.ShapeDtypeStruct((s, e), BF16), jax.ShapeDtypeStruct((s, d), F32), jax.ShapeDtypeStruct((s, d), BF16),
                   jax.ShapeDtypeStruct((1, d), F32), jax.ShapeDtypeStruct((1, cw), F32), jax.ShapeDtypeStruct(ws.shape, F32),
                   jax.ShapeDtypeStruct((heads, CHUNK), F32)],
        scratch_shapes=[pltpu.VMEM((tm, cw), F32), pltpu.VMEM((CHUNK, cw), F32)],
        compiler_params=_params("arbitrary"),
    )(dhb, dh, wout, pre, gate, ws, wst, sgu, win, h, gmix)


def _even_bwd(dhb, dh, wout, proj, cq, pooled, conva, wpool, wpoolt, pscale, win, x, gmix, tm):
    s, d = x.shape
    e = win.shape[1]
    aw = e // 4
    ni = s // tm

    def body(dhb_ref, dh_ref, wout_ref, proj_ref, cq_ref, pooled_ref, ca_ref, wp_ref, wpt_ref, ps_ref, win_ref, x_ref, gmix_ref,
             dproj_ref, dx_ref, dgain_ref, dca_ref, dwp_ref, dps_ref, cbuf, ebuf):
        i = pl.program_id(0)
        first = i == 0

        @pl.when(first)
        def _():
            cbuf[tm:tm + HALO, :] = jnp.zeros((HALO, aw), F32)
            ebuf[tm:tm + HALO, :] = jnp.zeros((HALO, aw), F32)

        dmix = _nt(dhb_ref[...], wout_ref[...])
        dy_a, dy_b = dmix[:, :aw], dmix[:, aw:]
        proj = proj_ref[...].astype(F32)
        a_b, a_c, a_v = (proj[:, k * aw:(k + 1) * aw] for k in range(3))
        da_b = dy_a * cq_ref[...].astype(F32)
        dcq = dy_a * a_b
        cbuf[0:tm, :] = dcq
        s1 = cbuf[pl.ds(1, tm), :]
        s2 = cbuf[pl.ds(2, tm), :]
        q = a_c * a_v
        for k, shifted in enumerate((s2, s1, dcq)):
            _acc(dca_ref.at[k:k + 1, :], jnp.sum(shifted * q, axis=0, keepdims=True), first)
        dq = ca_ref[2:3, :] * dcq + ca_ref[1:2, :] * s1 + ca_ref[0:1, :] * s2
        da_c = dq * a_v
        da_v = dq * a_c
        dps, dpool = [], []
        for g, w in enumerate(WINDOWS):
            cols = slice(g * LANE, (g + 1) * LANE)
            pooled = pooled_ref[:, cols]
            mixed = _nn(pooled, wp_ref[g])
            dps.append(jnp.sum(dy_b[:, cols] * mixed, axis=0, keepdims=True))
            dmixed = (dy_b[:, cols] * ps_ref[:, cols]).astype(BF16)
            _acc(dwp_ref.at[g], _tn(pooled, dmixed), first)
            dp = _nn(dmixed, wpt_ref[g])
            dpool.append(dp)
            ebuf[0:tm, cols] = dp / _counts((ni - 1 - i) * tm, tm, w)
        _acc(dps_ref, jnp.concatenate(dps, axis=1), first)
        dzs = []
        for g, w in enumerate(WINDOWS):
            cols = slice(g * LANE, (g + 1) * LANE)
            acc = ebuf[pl.ds(0, tm), cols]
            for m in range(1, w):
                acc = acc + ebuf[pl.ds(m, tm), cols]
            dzs.append(acc - dpool[g])
        dproj = jnp.concatenate([da_b, da_c, da_v] + dzs, axis=1).astype(BF16)
        dproj_ref[...] = dproj
        dx, dgain = _rms_bwd(_nt(dproj, win_ref[...]), x_ref[...], gmix_ref[...])
        dx_ref[...] = dh_ref[...] + dx
        _acc(dgain_ref, dgain, first)
        cbuf[tm:tm + HALO, :] = cbuf[0:HALO, :]
        ebuf[tm:tm + HALO, :] = ebuf[0:HALO, :]

    row = lambda c: pl.BlockSpec((tm, c), lambda i: (ni - 1 - i, 0))
    return _pcall(
        body, name="even_bwd", grid=(ni,),
        in_specs=[row(d), row(d), _whole(wout.shape), row(e), row(aw), row(aw), _whole(conva.shape), _whole(wpool.shape),
                  _whole(wpoolt.shape), _whole(pscale.shape), _whole(win.shape), row(d), _whole((1, d))],
        out_specs=[row(e), row(d), _whole((1, d)), _whole(conva.shape), _whole(wpool.shape), _whole(pscale.shape)],
        out_shape=[jax.ShapeDtypeStruct((s, e), BF16), jax.ShapeDtypeStruct((s, d), F32), jax.ShapeDtypeStruct((1, d), F32),
                   jax.ShapeDtypeStruct(conva.shape, F32), jax.ShapeDtypeStruct(wpool.shape, F32),
                   jax.ShapeDtypeStruct(pscale.shape, F32)],
        scratch_shapes=[pltpu.VMEM((tm + HALO, aw), F32), pltpu.VMEM((tm + HALO, aw), F32)],
        compiler_params=_params("arbitrary"),
    )(dhb, dh, wout, proj, cq, pooled, conva, wpool, wpoolt, pscale, win, x, gmix)


def _wgrad(a, b, tk, ts, name):
    s, ka = a.shape
    nb = b.shape[1]

    def body(a_ref, b_ref, o_ref):
        _acc(o_ref, _tn(a_ref[...], b_ref[...]), pl.program_id(1) == 0)

    return _pcall(
        body, name=name, grid=(ka // tk, s // ts),
        in_specs=[pl.BlockSpec((ts, tk), lambda k, t: (t, k)), pl.BlockSpec((ts, nb), lambda k, t: (t, 0))],
        out_specs=pl.BlockSpec((tk, nb), lambda k, t: (k, 0)),
        out_shape=jax.ShapeDtypeStruct((ka, nb), F32),
        compiler_params=_params("arbitrary", "arbitrary"),
    )(a, b)


def _adamw(parts, w, m, v, tr, name, block0=0):
    r, c = w.shape

    def body(p_ref, w_ref, m_ref, v_ref, g_ref, d_ref, mo_ref, vo_ref):
        g = p_ref[0].astype(F32)
        for j in range(1, N_DEV):
            g = g + p_ref[j].astype(F32)
        g_ref[...] = g
        mn = ADAM_B1 * m_ref[...] + (1.0 - ADAM_B1) * g
        vn = ADAM_B2 * v_ref[...] + (1.0 - ADAM_B2) * (g * g)
        mo_ref[...] = mn
        vo_ref[...] = vn
        m_hat = mn / (1.0 - ADAM_B1 ** ADAM_STEP)
        v_hat = vn / (1.0 - ADAM_B2 ** ADAM_STEP)
        d_ref[...] = -ADAM_LR * (m_hat / (jnp.sqrt(v_hat) + ADAM_EPS) + ADAM_WD * w_ref[...])

    row = pl.BlockSpec((tr, c), lambda i: (i, 0))
    out = jax.ShapeDtypeStruct((r, c), F32)
    return _pcall(
        body, name=name, grid=(r // tr,),
        in_specs=[pl.BlockSpec((N_DEV, tr, c), lambda i: (0, i + block0, 0)), row, row, row],
        out_specs=[row, row, row, row], out_shape=[out, out, out, out],
        compiler_params=_params("arbitrary"),
    )(parts, w, m, v)


def _hbm():
    return pl.BlockSpec(memory_space=pltpu.HBM)


def _gather_weights(shards):
    n = len(shards)

    def body(*refs):
        ins, outs = refs[:n], refs[n:2 * n]
        send_sems, recv_sems, local_sems = refs[2 * n:]
        x, y, c = lax.axis_index("x"), lax.axis_index("y"), lax.axis_index("c")
        me, sibling = (x, y, c), (x, y, 1 - c)
        chips = [(1 - x, y), (x, 1 - y), (1 - x, 1 - y)]

        def slot(t, dev):
            return outs[t].at[4 * dev[0] + 2 * dev[1] + dev[2]]

        def copy(t, k, block, to, src=None):
            return pltpu.make_async_remote_copy(
                src_ref=slot(t, block) if src is None else src, dst_ref=slot(t, block),
                send_sem=send_sems.at[7 * t + k], recv_sem=recv_sems.at[7 * t + k], device_id=to, device_id_type=MESH)

        mine, first, passed = [], [], []
        for t in range(n):
            mine.append(pltpu.make_async_copy(ins[t], slot(t, me), local_sems.at[t]))
            mine[t].start()
            first.append([copy(t, 0, me, sibling, src=ins[t])]
                         + [copy(t, 1 + j, me, (*chip, c), src=ins[t]) for j, chip in enumerate(chips)])
            for cp in first[t]:
                cp.start()
        for t in range(n):
            passed.append([copy(t, 4 + j, (*chip, c), sibling) for j, chip in enumerate(chips)])
            for j, chip in enumerate(chips):
                copy(t, 1 + j, (*chip, c), me).wait_recv()
                passed[t][j].start()
        for t in range(n):
            copy(t, 0, sibling, me).wait_recv()
            for j, chip in enumerate(chips):
                copy(t, 4 + j, (*chip, 1 - c), me).wait_recv()
        for t in range(n):
            for cp in first[t] + passed[t]:
                cp.wait_send()
            mine[t].wait()

    return _pcall(
        body, name="gather_weights",
        in_specs=[_hbm()] * n, out_specs=[_hbm()] * n,
        out_shape=[jax.ShapeDtypeStruct((N_DEV,) + a.shape, a.dtype) for a in shards],
        scratch_shapes=[pltpu.SemaphoreType.DMA((7 * n,)), pltpu.SemaphoreType.DMA((7 * n,)), pltpu.SemaphoreType.DMA((n,))],
    )(*shards)


def _exchange_grads(slotted, whole):
    tensors = list(slotted) + list(whole)
    n, ns = len(tensors), len(slotted)

    def body(*refs):
        ins, outs = refs[:n], refs[n:2 * n]
        send_sems, recv_sems, local_sems = refs[2 * n:]
        x, y, c = lax.axis_index("x"), lax.axis_index("y"), lax.axis_index("c")
        me = 4 * x + 2 * y + c
        peers = [(x, y, 1 - c), (1 - x, y, c), (x, 1 - y, c), (1 - x, 1 - y, c),
                 (1 - x, y, 1 - c), (x, 1 - y, 1 - c), (1 - x, 1 - y, 1 - c)]

        def copy(t, k):
            peer = peers[k]
            pid = 4 * peer[0] + 2 * peer[1] + peer[2]
            return pltpu.make_async_remote_copy(
                src_ref=ins[t].at[pid] if t < ns else ins[t], dst_ref=outs[t].at[me],
                send_sem=send_sems.at[7 * t + k], recv_sem=recv_sems.at[7 * t + k], device_id=peer, device_id_type=MESH)

        def arrival(t, k):
            peer = peers[k]
            pid = 4 * peer[0] + 2 * peer[1] + peer[2]
            return pltpu.make_async_remote_copy(
                src_ref=ins[t].at[pid] if t < ns else ins[t], dst_ref=outs[t].at[pid],
                send_sem=send_sems.at[7 * t + k], recv_sem=recv_sems.at[7 * t + k], device_id=peer, device_id_type=MESH)

        mine, sent = [], []
        for t in range(n):
            mine.append(pltpu.make_async_copy(ins[t].at[me] if t < ns else ins[t], outs[t].at[me], local_sems.at[t]))
            mine[t].start()
            for k in range(7):
                cp = copy(t, k)
                cp.start()
                sent.append(cp)
        for t in range(n):
            for k in range(7):
                arrival(t, k).wait_recv()
        for cp in sent:
            cp.wait_send()
        for t in range(n):
            mine[t].wait()

    return _pcall(
        body, name="exchange_grads",
        in_specs=[_hbm()] * n, out_specs=[_hbm()] * n,
        out_shape=[jax.ShapeDtypeStruct(a.shape, a.dtype) for a in slotted]
        + [jax.ShapeDtypeStruct((N_DEV,) + a.shape, a.dtype) for a in whole],
        scratch_shapes=[pltpu.SemaphoreType.DMA((7 * n,)), pltpu.SemaphoreType.DMA((7 * n,)), pltpu.SemaphoreType.DMA((n,))],
    )(*tensors)


SMALL_ROWS, SMALL_COLS = 16, 384
REP_COLS = 1024


def _pad_cols(a, cols):
    return jnp.pad(a, ((0, 0), (0, cols - a.shape[1])))


def _pack_small(conv_a, sgu_norm, conv_ffn):
    rows = [_pad_cols(conv_a, SMALL_COLS), _pad_cols(sgu_norm, SMALL_COLS),
            _pad_cols(conv_ffn.reshape(-1, conv_ffn.shape[-1]), SMALL_COLS)]
    used = sum(r.shape[0] for r in rows)
    return jnp.concatenate(rows + [jnp.zeros((SMALL_ROWS - used, SMALL_COLS), F32)], axis=0)


def _unpack_small(p, ca_w, sg_w, cf_w):
    return p[0:3, 0:ca_w], p[3:4, 0:sg_w], p[4:10, 0:cf_w].reshape(2, 3, cf_w)


def _rep_layout(t):
    return [("norm_mix", 2, 1024), ("norm_ffn", 2, 1024), ("final_norm", 1, 1024), ("pool_scale", 1, 512),
            ("b_spatial", 1, 1024), ("b_conv_ffn", 6, 1024), ("w_pool", 64, 1024), ("w_spatial", 128, 1024)]


REP_ROWS = 208


def _pack_rep(t):
    bc = t["b_conv_ffn"]
    rows = [t["norm_mix"], t["norm_ffn"], t["final_norm"].reshape(1, -1), _pad_cols(t["pool_scale"].reshape(1, -1), REP_COLS),
            t["b_spatial"].reshape(1, -1), _pad_cols(bc, 3 * REP_COLS).reshape(6, REP_COLS),
            t["w_pool"].reshape(-1, REP_COLS), t["w_spatial"].reshape(-1, REP_COLS)]
    used = sum(r.shape[0] for r in rows)
    return jnp.concatenate(rows + [jnp.zeros((REP_ROWS - used, REP_COLS), F32)], axis=0)


def _unpack_rep(p, like):
    f = like["b_conv_ffn"].shape[1]
    return {
        "norm_mix": p[0:2], "norm_ffn": p[2:4], "final_norm": p[4], "pool_scale": p[5:6, 0:like["pool_scale"].shape[1]],
        "b_spatial": p[6].reshape(like["b_spatial"].shape), "b_conv_ffn": p[7:13].reshape(2, 3 * REP_COLS)[:, 0:f],
        "w_pool": p[13:77].reshape(like["w_pool"].shape), "w_spatial": p[77:205].reshape(like["w_spatial"].shape),
    }


def _cols_to_slots(a):
    *lead, r, cc = a.shape
    a = a.reshape(*lead, r, N_DEV, cc // N_DEV)
    return jnp.moveaxis(a, -2, 0)


def _slots_to_cols(a):
    a = jnp.moveaxis(a, 0, -2)
    return a.reshape(*a.shape[:-2], a.shape[-2] * a.shape[-1])


def _rows_to_slots(a):
    *lead, rr, c = a.shape
    a = a.reshape(*lead, N_DEV, rr // N_DEV, c)
    return jnp.moveaxis(a, -3, 0)


def _slots_to_rows(a):
    a = jnp.moveaxis(a, 0, -3)
    return a.reshape(*a.shape[:-3], a.shape[-3] * a.shape[-2], a.shape[-1])


def kernel(x, norm_mix, norm_ffn, final_norm, w_in_even, conv_a, w_pool, pool_scale, w_out_even, w_in_odd, sgu_norm, w_spatial, b_spatial, w_out_odd, w_ffn_gate, w_ffn_up, conv_ffn, b_conv_ffn, w_ffn_down, loss_target, m_norm_mix, m_norm_ffn, m_final_norm, m_w_in_even, m_conv_a, m_w_pool, m_pool_scale, m_w_out_even, m_w_in_odd, m_sgu_norm, m_w_spatial, m_b_spatial, m_w_out_odd, m_w_ffn_gate, m_w_ffn_up, m_conv_ffn, m_b_conv_ffn, m_w_ffn_down, v_norm_mix, v_norm_ffn, v_final_norm, v_w_in_even, v_conv_a, v_w_pool, v_pool_scale, v_w_out_even, v_w_in_odd, v_sgu_norm, v_w_spatial, v_b_spatial, v_w_out_odd, v_w_ffn_gate, v_w_ffn_up, v_conv_ffn, v_b_conv_ffn, v_w_ffn_down):
    s, d = x.shape[1], x.shape[2]
    x2, target = x[0], loss_target[0]
    tm = min(512, s)
    tm_wide = min(2048, s)
    tn = 256
    row = lambda a: a.reshape(1, -1)

    col_pack = jnp.concatenate([w_in_even[0], w_in_odd[0], w_ffn_gate[0], w_ffn_gate[1], w_ffn_up[0], w_ffn_up[1]], axis=1).astype(BF16)
    row_pack = jnp.concatenate([w_out_even[0], w_out_odd[0], w_ffn_down[0], w_ffn_down[1]], axis=0).astype(BF16)
    small_pack = _pack_small(conv_a[0], sgu_norm, conv_ffn)
    gcol, grow, gsmall = _gather_weights([col_pack, row_pack, small_pack])
    ein, fw = w_in_even.shape[2], w_ffn_gate.shape[2]
    o = [0, ein, 2 * ein, 2 * ein + fw, 2 * ein + 2 * fw, 2 * ein + 3 * fw, 2 * ein + 4 * fw]
    win_e, win_o, wg0, wg1, wu0, wu1 = (_slots_to_cols(gcol[:, :, o[k]:o[k + 1]]) for k in range(6))
    ro, rd = w_out_even.shape[1], w_ffn_down.shape[1]
    o = [0, ro, 2 * ro, 2 * ro + rd, 2 * ro + 2 * rd]
    wout_e, wout_o, wd0, wd1 = (_slots_to_rows(grow[:, o[k]:o[k + 1], :]) for k in range(4))
    wg, wu, wd = (wg0, wg1), (wu0, wu1), (wd0, wd1)
    ca_full = jnp.moveaxis(gsmall[:, 0:3, 0:conv_a.shape[2]], 0, 1).reshape(3, -1)
    sgu_full = gsmall[:, 3, 0:sgu_norm.shape[1]].reshape(1, -1)
    cf_full = jnp.moveaxis(gsmall[:, 4:10, 0:conv_ffn.shape[2]].reshape(N_DEV, 2, 3, -1), 0, 2).reshape(2, 3, -1)

    tril = jnp.tril(jnp.ones((CHUNK, CHUNK), F32))
    ws_m = w_spatial[0] * tril
    ws_b = ws_m.astype(BF16)
    wst_b = jnp.swapaxes(ws_m, 1, 2).astype(BF16)
    bfull = jnp.repeat(b_spatial[0].T, CHUNK, axis=1)
    wpool_b = w_pool[0].astype(BF16)
    wpoolt_b = jnp.swapaxes(w_pool[0], 1, 2).astype(BF16)

    xn0, proj0, cq0, pooled0, mix0, h1, hn0 = _even_fwd(x2, norm_mix[0:1], norm_ffn[0:1], win_e, ca_full, wpool_b, pool_scale, wout_e, tm)
    g0, gc0, up0, a0 = _ffn_fwd1(hn0, wg[0], wu[0], cf_full[0], b_conv_ffn[0:1], tm_wide, tn, "ffn_fwd1_l0")
    h2, xn1 = _ffn_fwd2(a0, wd[0], h1, norm_mix[1:2], tm, "ffn_fwd2_l0")
    pre1, gate1, mixo1, h3, hn1 = _odd_fwd(xn1, h2, win_o, sgu_full, ws_b, bfull, wout_o, norm_ffn[1:2], tm)
    g1, gc1, up1, a1 = _ffn_fwd1(hn1, wg[1], wu[1], cf_full[1], b_conv_ffn[1:2], tm_wide, tn, "ffn_fwd1_l1")
    h4, _ = _ffn_fwd2(a1, wd[1], h3, row(final_norm), tm, "ffn_fwd2_l1")

    dh4, dh4b, d_final, lossvec = _loss_bwd(h4, row(final_norm), target, tm)
    loss = lax.psum(jnp.sum(lossvec), AXES)
    dg1, dup1, dcw1, dcb1 = _ffn_bwd1(dh4b, g1, gc1, up1, wd[1], cf_full[1], tm_wide, tn, "ffn_bwd1_l1")
    dh3, dh3b, d_nffn1 = _ffn_bwd2(dg1, dup1, wg[1], wu[1], h3, norm_ffn[1:2], dh4, tm // 2, "ffn_bwd2_l1")
    dpre1, dh2, dh2b, d_nmix1, d_sgu, d_ws, d_b = _odd_bwd(dh3b, dh3, wout_o, pre1, gate1, ws_b, wst_b, sgu_full, win_o, h2, norm_mix[1:2], tm)
    dg0, dup0, dcw0, dcb0 = _ffn_bwd1(dh2b, g0, gc0, up0, wd[0], cf_full[0], tm_wide, tn, "ffn_bwd1_l0")
    dh1, dh1b, d_nffn0 = _ffn_bwd2(dg0, dup0, wg[0], wu[0], h1, norm_ffn[0:1], dh2, tm // 2, "ffn_bwd2_l0")
    dproj0, grad_x, d_nmix0, d_ca, d_wp, d_ps = _even_bwd(dh1b, dh1, wout_e, proj0, cq0, pooled0, ca_full, wpool_b, wpoolt_b, pool_scale,
                                                         win_e, x2, norm_mix[0:1], tm)
    ts = min(1024, s)
    d_win_e = _wgrad(xn0, dproj0, 512, ts, "wgrad_in_even")
    d_win_o = _wgrad(xn1, dpre1, 512, ts, "wgrad_in_odd")
    d_wout_e = _wgrad(mix0, dh1b, 512, ts, "wgrad_out_even")
    d_wout_o = _wgrad(mixo1, dh3b, 512, ts, "wgrad_out_odd")
    d_wg = jnp.stack([_wgrad(hn0, dg0, 512, ts, "wgrad_gate_l0"), _wgrad(hn1, dg1, 512, ts, "wgrad_gate_l1")])
    d_wu = jnp.stack([_wgrad(hn0, dup0, 512, ts, "wgrad_up_l0"), _wgrad(hn1, dup1, 512, ts, "wgrad_up_l1")])
    d_wd = jnp.stack([_wgrad(a0, dh2b, tn, s, "wgrad_down_l0"), _wgrad(a1, dh4b, tn, s, "wgrad_down_l1")])

    d_small = jnp.stack([_pack_small(a, b, c) for a, b, c in zip(
        _cols_to_slots(d_ca), _cols_to_slots(d_sgu), _cols_to_slots(jnp.stack([dcw0, dcw1])))])
    rep = {"norm_mix": jnp.concatenate([d_nmix0, d_nmix1]), "norm_ffn": jnp.concatenate([d_nffn0, d_nffn1]), "final_norm": d_final[0],
           "pool_scale": d_ps, "b_spatial": d_b[None], "b_conv_ffn": jnp.concatenate([dcb0, dcb1]), "w_pool": d_wp[None],
           "w_spatial": (d_ws * tril)[None]}
    slotted = [
        _cols_to_slots(jnp.stack([d_win_e, d_win_o])).astype(BF16),
        _cols_to_slots(d_wg).astype(BF16), _cols_to_slots(d_wu).astype(BF16),
        _rows_to_slots(jnp.stack([d_wout_e, d_wout_o])).astype(BF16),
        _rows_to_slots(d_wd).astype(BF16),
        d_small,
    ]
    r_in, r_g, r_u, r_out, r_d, r_small, r_rep = _exchange_grads(slotted, [_pack_rep(rep)])

    def merged(a):
        return a.reshape(*a.shape[:-3], a.shape[-3] * a.shape[-2], a.shape[-1])

    out = {}
    rows_in = w_in_even.shape[1]
    out["w_in_even"] = _adamw(merged(r_in), w_in_even[0], m_w_in_even[0], v_w_in_even[0], 256, "adamw_in_even")
    out["w_in_odd"] = _adamw(merged(r_in), w_in_odd[0], m_w_in_odd[0], v_w_in_odd[0], 256, "adamw_in_odd", block0=rows_in // 256)
    out["w_out_even"] = _adamw(merged(r_out), w_out_even[0], m_w_out_even[0], v_w_out_even[0], ro, "adamw_out_even")
    out["w_out_odd"] = _adamw(merged(r_out), w_out_odd[0], m_w_out_odd[0], v_w_out_odd[0], ro, "adamw_out_odd", block0=1)
    out["w_ffn_gate"] = _adamw(merged(r_g), merged(w_ffn_gate), merged(m_w_ffn_gate), merged(v_w_ffn_gate), 256, "adamw_gate")
    out["w_ffn_up"] = _adamw(merged(r_u), merged(w_ffn_up), merged(m_w_ffn_up), merged(v_w_ffn_up), 256, "adamw_up")
    out["w_ffn_down"] = _adamw(merged(r_d), merged(w_ffn_down), merged(m_w_ffn_down), merged(v_w_ffn_down), rd, "adamw_down")
    small = _adamw(r_small, _pack_small(conv_a[0], sgu_norm, conv_ffn), _pack_small(m_conv_a[0], m_sgu_norm, m_conv_ffn),
                   _pack_small(v_conv_a[0], v_sgu_norm, v_conv_ffn), SMALL_ROWS, "adamw_small")
    rep_w = {"norm_mix": norm_mix, "norm_ffn": norm_ffn, "final_norm": final_norm, "pool_scale": pool_scale, "b_spatial": b_spatial,
             "b_conv_ffn": b_conv_ffn, "w_pool": w_pool, "w_spatial": w_spatial}
    rep_m = {"norm_mix": m_norm_mix, "norm_ffn": m_norm_ffn, "final_norm": m_final_norm, "pool_scale": m_pool_scale,
             "b_spatial": m_b_spatial, "b_conv_ffn": m_b_conv_ffn, "w_pool": m_w_pool, "w_spatial": m_w_spatial}
    rep_v = {"norm_mix": v_norm_mix, "norm_ffn": v_norm_ffn, "final_norm": v_final_norm, "pool_scale": v_pool_scale,
             "b_spatial": v_b_spatial, "b_conv_ffn": v_b_conv_ffn, "w_pool": v_w_pool, "w_spatial": v_w_spatial}
    reps = _adamw(r_rep, _pack_rep(rep_w), _pack_rep(rep_m), _pack_rep(rep_v), REP_ROWS, "adamw_replicated")

    names = ["norm_mix", "norm_ffn", "final_norm", "w_in_even", "conv_a", "w_pool", "pool_scale", "w_out_even", "w_in_odd", "sgu_norm",
             "w_spatial", "b_spatial", "w_out_odd", "w_ffn_gate", "w_ffn_up", "conv_ffn", "b_conv_ffn", "w_ffn_down"]
    like = {"norm_mix": norm_mix, "norm_ffn": norm_ffn, "final_norm": final_norm, "w_in_even": w_in_even, "conv_a": conv_a,
            "w_pool": w_pool, "pool_scale": pool_scale, "w_out_even": w_out_even, "w_in_odd": w_in_odd, "sgu_norm": sgu_norm,
            "w_spatial": w_spatial, "b_spatial": b_spatial, "w_out_odd": w_out_odd, "w_ffn_gate": w_ffn_gate, "w_ffn_up": w_ffn_up,
            "conv_ffn": conv_ffn, "b_conv_ffn": b_conv_ffn, "w_ffn_down": w_ffn_down}
    groups = []
    for k in range(4):
        ca_k, sg_k, cf_k = _unpack_small(small[k], conv_a.shape[2], sgu_norm.shape[1], conv_ffn.shape[2])
        vals = dict(_unpack_rep(reps[k], like))
        vals.update(conv_a=ca_k, sgu_norm=sg_k, conv_ffn=cf_k)
        for nm in ("w_in_even", "w_in_odd", "w_out_even", "w_out_odd", "w_ffn_gate", "w_ffn_up", "w_ffn_down"):
            vals[nm] = out[nm][k]
        groups.append([vals[nm].reshape(like[nm].shape) for nm in names])
    return (loss, grad_x[None], *groups[0], *groups[1], *groups[2], *groups[3])
```

```python
import functools

import jax
import jax.numpy as jnp
from jax import lax
from jax.experimental import pallas as pl
from jax.experimental.pallas import tpu as pltpu

F32, BF16 = jnp.float32, jnp.bfloat16
EPS = 1e-6
WINDOWS = (2, 4, 8, 16)
HALO = 16
CHUNK = 128
N_DEV = 8
N_CHIP = 4
MESH = pl.DeviceIdType.MESH
VMEM_LIMIT = 56 * 2**20
LANE = 128
ADAM_LR, ADAM_B1, ADAM_B2, ADAM_EPS, ADAM_WD, ADAM_STEP = 0.001, 0.9, 0.999, 1e-08, 0.01, 10
INV_SQRT2 = 0.7071067811865476
INV_SQRT2PI = 0.3989422804014327


def _pcall(body, **kw):
    return pl.pallas_call(body, **kw)


def _params(*sem):
    return pltpu.CompilerParams(dimension_semantics=sem, vmem_limit_bytes=VMEM_LIMIT)


def _whole(shape):
    return pl.BlockSpec(shape, lambda *_: (0,) * len(shape))


def _nn(a, b):
    return jnp.dot(a, b, preferred_element_type=F32)


def _nt(a, b):
    return lax.dot_general(a, b, (((1,), (1,)), ((), ())), preferred_element_type=F32)


def _tn(a, b):
    return lax.dot_general(a, b, (((0,), (0,)), ((), ())), preferred_element_type=F32)


def _rms(x, gain):
    r = lax.rsqrt(jnp.mean(x * x, axis=-1, keepdims=True) + EPS)
    return x * r * gain


def _rms_bwd(dy, x, gain):
    r = lax.rsqrt(jnp.mean(x * x, axis=-1, keepdims=True) + EPS)
    xh = x * r
    dgain = jnp.sum(dy * xh, axis=0, keepdims=True)
    dxh = dy * gain
    dx = r * (dxh - xh * jnp.mean(dxh * xh, axis=-1, keepdims=True))
    return dx, dgain


def _gelu(x):
    return 0.5 * x * (1.0 + lax.erf(x * INV_SQRT2))


def _gelu_grad(x):
    return 0.5 * (1.0 + lax.erf(x * INV_SQRT2)) + x * jnp.exp(-0.5 * x * x) * INV_SQRT2PI


def _acc(ref, val, first):
    @pl.when(first)
    def _():
        ref[...] = val

    @pl.when(jnp.logical_not(first))
    def _():
        ref[...] += val


def _counts(row0, tm, w):
    pos1 = (row0 + lax.broadcasted_iota(jnp.int32, (tm, 1), 0) + 1).astype(F32)
    return jnp.minimum(pos1, float(w))


def _even_fwd(x, gmix, gffn, win, conva, wpool, pscale, wout, tm):
    s, d = x.shape
    e = win.shape[1]
    aw = e // 4

    def body(x_ref, gmix_ref, gffn_ref, win_ref, ca_ref, wp_ref, ps_ref, wout_ref,
             xn_ref, proj_ref, cq_ref, pooled_ref, mix_ref, h_ref, hn_ref, qbuf, zbuf):
        i = pl.program_id(0)

        @pl.when(i == 0)
        def _():
            qbuf[0:HALO, :] = jnp.zeros((HALO, aw), F32)
            zbuf[0:HALO, :] = jnp.zeros((HALO, aw), F32)

        xv = x_ref[...]
        xn = _rms(xv, gmix_ref[...]).astype(BF16)
        xn_ref[...] = xn
        proj = _nn(xn, win_ref[...])
        proj_ref[...] = proj.astype(BF16)
        a_b, a_c, a_v, z = (proj[:, k * aw:(k + 1) * aw] for k in range(4))
        q = a_c * a_v
        qbuf[HALO:HALO + tm, :] = q
        cq = ca_ref[2:3, :] * q + ca_ref[1:2, :] * qbuf[pl.ds(HALO - 1, tm), :] + ca_ref[0:1, :] * qbuf[pl.ds(HALO - 2, tm), :]
        cq_ref[...] = cq.astype(BF16)
        y_a = a_b * cq
        zbuf[HALO:HALO + tm, :] = z
        ys = []
        for g, w in enumerate(WINDOWS):
            cols = slice(g * LANE, (g + 1) * LANE)
            acc = z[:, cols]
            for m in range(1, w):
                acc = acc + zbuf[pl.ds(HALO - m, tm), cols]
            pooled = (acc / _counts(i * tm, tm, w) - z[:, cols]).astype(BF16)
            pooled_ref[:, cols] = pooled
            ys.append(_nn(pooled, wp_ref[g]))
        y_b = jnp.concatenate(ys, axis=1) * ps_ref[...]
        mix = jnp.concatenate([y_a, y_b], axis=1).astype(BF16)
        mix_ref[...] = mix
        h = xv + _nn(mix, wout_ref[...])
        h_ref[...] = h
        hn_ref[...] = _rms(h, gffn_ref[...]).astype(BF16)
        qbuf[0:HALO, :] = qbuf[tm:tm + HALO, :]
        zbuf[0:HALO, :] = zbuf[tm:tm + HALO, :]

    row = lambda c: pl.BlockSpec((tm, c), lambda i: (i, 0))
    return _pcall(
        body, name="even_fwd", grid=(s // tm,),
        in_specs=[row(d), _whole((1, d)), _whole((1, d)), _whole(win.shape), _whole(conva.shape), _whole(wpool.shape),
                  _whole(pscale.shape), _whole(wout.shape)],
        out_specs=[row(d), row(e), row(aw), row(aw), row(d), row(d), row(d)],
        out_shape=[jax.ShapeDtypeStruct((s, d), BF16), jax.ShapeDtypeStruct((s, e), BF16), jax.ShapeDtypeStruct((s, aw), BF16),
                   jax.ShapeDtypeStruct((s, aw), BF16), jax.ShapeDtypeStruct((s, d), BF16), jax.ShapeDtypeStruct((s, d), F32),
                   jax.ShapeDtypeStruct((s, d), BF16)],
        scratch_shapes=[pltpu.VMEM((tm + HALO, aw), F32), pltpu.VMEM((tm + HALO, aw), F32)],
        compiler_params=_params("arbitrary"),
    )(x, gmix, gffn, win, conva, wpool, pscale, wout)


def _ffn_fwd1(hn, wgu, cw, cb, tm, tn, name):
    s, d = hn.shape
    f = wgu.shape[2]

    def body(hn_ref, wg_ref, wu_ref, cw_ref, cb_ref, g_ref, gc_ref, up_ref, a_ref, gbuf):
        i = pl.program_id(1)

        @pl.when(i == 0)
        def _():
            gbuf[0:HALO, :] = jnp.zeros((HALO, tn), F32)

        hv = hn_ref[...]
        g = _nn(hv, wg_ref[...])
        g_ref[...] = g.astype(BF16)
        gbuf[HALO:HALO + tm, :] = g
        gc = (cw_ref[2:3, :] * g + cw_ref[1:2, :] * gbuf[pl.ds(HALO - 1, tm), :]
              + cw_ref[0:1, :] * gbuf[pl.ds(HALO - 2, tm), :] + cb_ref[...])
        gc_ref[...] = gc.astype(BF16)
        up = _nn(hv, wu_ref[...])
        up_ref[...] = up.astype(BF16)
        a_ref[...] = (gc * jax.nn.sigmoid(gc) * up).astype(BF16)
        gbuf[0:HALO, :] = gbuf[tm:tm + HALO, :]

    tile = pl.BlockSpec((tm, tn), lambda j, i: (i, j))
    wcol = lambda r: pl.BlockSpec((r, tn), lambda j, i: (0, j))
    wsel = lambda k: pl.BlockSpec((None, d, tn), lambda j, i: (k, 0, j))
    out = jax.ShapeDtypeStruct((s, f), BF16)
    return _pcall(
        body, name=name, grid=(f // tn, s // tm),
        in_specs=[pl.BlockSpec((tm, d), lambda j, i: (i, 0)), wsel(0), wsel(1), wcol(3), wcol(1)],
        out_specs=[tile, tile, tile, tile], out_shape=[out, out, out, out],
        scratch_shapes=[pltpu.VMEM((tm + HALO, tn), F32)],
        compiler_params=_params("arbitrary", "arbitrary"),
    )(hn, wgu, wgu, cw, cb)


def _ffn_fwd2(a, wd, h, gain, tm, name):
    s, d = h.shape
    f = a.shape[1]

    def body(a_ref, wd_ref, h_ref, gain_ref, ho_ref, hn_ref):
        ho = h_ref[...] + _nn(a_ref[...], wd_ref[...])
        ho_ref[...] = ho
        hn_ref[...] = _rms(ho, gain_ref[...]).astype(BF16)

    row = lambda c: pl.BlockSpec((tm, c), lambda i: (i, 0))
    return _pcall(
        body, name=name, grid=(s // tm,),
        in_specs=[row(f), _whole(wd.shape), row(d), _whole((1, d))],
        out_specs=[row(d), row(d)],
        out_shape=[jax.ShapeDtypeStruct((s, d), F32), jax.ShapeDtypeStruct((s, d), BF16)],
        compiler_params=_params("arbitrary"),
    )(a, wd, h, gain)


def _odd_fwd(xn, h, win, sgu, ws, bfull, wout, gffn, tm):
    s, d = h.shape
    e = win.shape[1]
    cw = e // 2
    heads = ws.shape[0]

    def body(xn_ref, h_ref, win_ref, sgu_ref, ws_ref, b_ref, wout_ref, gffn_ref,
             pre_ref, gate_ref, mixo_ref, ho_ref, hn_ref, gbuf):
        pre = _nn(xn_ref[...], win_ref[...])
        pre_ref[...] = pre.astype(BF16)
        p = _gelu(pre)
        u, v = p[:, :cw], p[:, cw:]
        vn = _rms(v, sgu_ref[...]).astype(BF16)
        for n in range(tm // CHUNK):
            rows = slice(n * CHUNK, (n + 1) * CHUNK)
            for hd in range(heads):
                cols = slice(hd * CHUNK, (hd + 1) * CHUNK)
                gbuf[rows, cols] = _nn(ws_ref[hd], vn[rows, cols]) + b_ref[:, cols]
        gate = gbuf[...]
        gate_ref[...] = gate.astype(BF16)
        mixo = (u * gate).astype(BF16)
        mixo_ref[...] = mixo
        ho = h_ref[...] + _nn(mixo, wout_ref[...])
        ho_ref[...] = ho
        hn_ref[...] = _rms(ho, gffn_ref[...]).astype(BF16)

    row = lambda c: pl.BlockSpec((tm, c), lambda i: (i, 0))
    return _pcall(
        body, name="odd_fwd", grid=(s // tm,),
        in_specs=[row(d), row(d), _whole(win.shape), _whole(sgu.shape), _whole(ws.shape), _whole(bfull.shape),
                  _whole(wout.shape), _whole((1, d))],
        out_specs=[row(e), row(cw), row(cw), row(d), row(d)],
        out_shape=[jax.ShapeDtypeStruct((s, e), BF16), jax.ShapeDtypeStruct((s, cw), BF16), jax.ShapeDtypeStruct((s, cw), BF16),
                   jax.ShapeDtypeStruct((s, d), F32), jax.ShapeDtypeStruct((s, d), BF16)],
        scratch_shapes=[pltpu.VMEM((tm, cw), F32)],
        compiler_params=_params("arbitrary"),
    )(xn, h, win, sgu, ws, bfull, wout, gffn)


def _loss_bwd(h, gain, target, tm):
    s, d = h.shape

    def body(h_ref, gain_ref, t_ref, dh_ref, dhb_ref, dgain_ref, loss_ref):
        i = pl.program_id(0)
        hv = h_ref[...]
        gain = gain_ref[...]
        err = _rms(hv, gain) - t_ref[...]
        dy = err * (1.0 / d)
        dx, dgain = _rms_bwd(dy, hv, gain)
        dh_ref[...] = dx
        dhb_ref[...] = dx.astype(BF16)
        _acc(dgain_ref, dgain, i == 0)
        _acc(loss_ref, jnp.sum(err * err, axis=0, keepdims=True) * (0.5 / d), i == 0)

    row = pl.BlockSpec((tm, d), lambda i: (i, 0))
    return _pcall(
        body, name="loss_bwd", grid=(s // tm,),
        in_specs=[row, _whole((1, d)), row],
        out_specs=[row, row, _whole((1, d)), _whole((1, d))],
        out_shape=[jax.ShapeDtypeStruct((s, d), F32), jax.ShapeDtypeStruct((s, d), BF16), jax.ShapeDtypeStruct((1, d), F32),
                   jax.ShapeDtypeStruct((1, d), F32)],
        compiler_params=_params("arbitrary"),
    )(h, gain, target)


def _ffn_bwd1(dhb, g, gc, up, wd, cw, tm, tn, name):
    s, d = dhb.shape
    f = g.shape[1]
    ni = s // tm

    def body(dh_ref, g_ref, gc_ref, up_ref, wd_ref, cw_ref, dg_ref, dup_ref, dcw_ref, dcb_ref, ebuf):
        i = pl.program_id(1)

        @pl.when(i == 0)
        def _():
            ebuf[tm:tm + HALO, :] = jnp.zeros((HALO, tn), F32)

        da = _nt(dh_ref[...], wd_ref[...])
        gcv = gc_ref[...].astype(F32)
        sg = jax.nn.sigmoid(gcv)
        dup_ref[...] = (da * gcv * sg).astype(BF16)
        dgc = da * up_ref[...].astype(F32) * (sg * (1.0 + gcv * (1.0 - sg)))
        ebuf[0:tm, :] = dgc
        s1 = ebuf[pl.ds(1, tm), :]
        s2 = ebuf[pl.ds(2, tm), :]
        dg_ref[...] = (cw_ref[2:3, :] * dgc + cw_ref[1:2, :] * s1 + cw_ref[0:1, :] * s2).astype(BF16)
        gv = g_ref[...].astype(F32)
        for k, shifted in enumerate((s2, s1, dgc)):
            _acc(dcw_ref.at[k:k + 1, :], jnp.sum(shifted * gv, axis=0, keepdims=True), i == 0)
        _acc(dcb_ref, jnp.sum(dgc, axis=0, keepdims=True), i == 0)
        ebuf[tm:tm + HALO, :] = ebuf[0:HALO, :]

    tile = pl.BlockSpec((tm, tn), lambda j, i: (ni - 1 - i, j))
    wcol = lambda r: pl.BlockSpec((r, tn), lambda j, i: (0, j))
    out = jax.ShapeDtypeStruct((s, f), BF16)
    return _pcall(
        body, name=name, grid=(f // tn, ni),
        in_specs=[pl.BlockSpec((tm, d), lambda j, i: (ni - 1 - i, 0)), tile, tile, tile,
                  pl.BlockSpec((tn, d), lambda j, i: (j, 0)), wcol(3)],
        out_specs=[tile, tile, wcol(3), wcol(1)],
        out_shape=[out, out, jax.ShapeDtypeStruct((3, f), F32), jax.ShapeDtypeStruct((1, f), F32)],
        scratch_shapes=[pltpu.VMEM((tm + HALO, tn), F32)],
        compiler_params=_params("arbitrary", "arbitrary"),
    )(dhb, g, gc, up, wd, cw)


def _ffn_bwd2(dg, dup, wgu, h, gain, dh, tm, name):
    s, d = h.shape
    f = dg.shape[1]

    def body(dg_ref, dup_ref, wg_ref, wu_ref, h_ref, gain_ref, dh_ref, dho_ref, dhb_ref, dgain_ref):
        dhn = _nt(dg_ref[...], wg_ref[...]) + _nt(dup_ref[...], wu_ref[...])
        dx, dgain = _rms_bwd(dhn, h_ref[...], gain_ref[...])
        dho = dh_ref[...] + dx
        dho_ref[...] = dho
        dhb_ref[...] = dho.astype(BF16)
        _acc(dgain_ref, dgain, pl.program_id(0) == 0)

    row = lambda c: pl.BlockSpec((tm, c), lambda i: (i, 0))
    wsel = lambda k: pl.BlockSpec((None, d, f), lambda i: (k, 0, 0))
    return _pcall(
        body, name=name, grid=(s // tm,),
        in_specs=[row(f), row(f), wsel(0), wsel(1), row(d), _whole((1, d)), row(d)],
        out_specs=[row(d), row(d), _whole((1, d))],
        out_shape=[jax.ShapeDtypeStruct((s, d), F32), jax.ShapeDtypeStruct((s, d), BF16), jax.ShapeDtypeStruct((1, d), F32)],
        compiler_params=_params("arbitrary"),
    )(dg, dup, wgu, wgu, h, gain, dh)


def _odd_bwd(dhb, dh, wout, pre, gate, ws, wst, sgu, win, h, gmix, tm):
    s, d = h.shape
    e = win.shape[1]
    cw = e // 2
    heads = ws.shape[0]
    ni = s // tm

    def body(dhb_ref, dh_ref, wout_ref, pre_ref, gate_ref, ws_ref, wst_ref, sgu_ref, win_ref, h_ref, gmix_ref,
             dpre_ref, dho_ref, dhob_ref, dgain_ref, dsgu_ref, dws_ref, db_ref, vbuf, gacc):
        i = pl.program_id(0)
        first = i == 0
        dmixo = _nt(dhb_ref[...], wout_ref[...])
        pre = pre_ref[...].astype(F32)
        p = _gelu(pre)
        u, v = p[:, :cw], p[:, cw:]
        sgu = sgu_ref[...]
        rv = lax.rsqrt(jnp.mean(v * v, axis=-1, keepdims=True) + EPS)
        vh = v * rv
        vn = (vh * sgu).astype(BF16)
        du = dmixo * gate_ref[...].astype(F32)
        dgate = dmixo * u
        dgate_b = dgate.astype(BF16)
        gsum = dgate[0:CHUNK, :]
        for n in range(1, tm // CHUNK):
            gsum = gsum + dgate[n * CHUNK:(n + 1) * CHUNK, :]
        _acc(gacc, gsum, first)
        for hd in range(heads):
            cols = slice(hd * CHUNK, (hd + 1) * CHUNK)
            dws = None
            for n in range(tm // CHUNK):
                rows = slice(n * CHUNK, (n + 1) * CHUNK)
                vbuf[rows, cols] = _nn(wst_ref[hd], dgate_b[rows, cols])
                part = _nt(dgate_b[rows, cols], vn[rows, cols])
                dws = part if dws is None else dws + part
            _acc(dws_ref.at[hd], dws, first)
        dvn = vbuf[...]
        _acc(dsgu_ref, jnp.sum(dvn * vh, axis=0, keepdims=True), first)
        dvh = dvn * sgu
        dv = rv * (dvh - vh * jnp.mean(dvh * vh, axis=-1, keepdims=True))
        dpre = (jnp.concatenate([du, dv], axis=1) * _gelu_grad(pre)).astype(BF16)
        dpre_ref[...] = dpre
        dx, dgain = _rms_bwd(_nt(dpre, win_ref[...]), h_ref[...], gmix_ref[...])
        dho = dh_ref[...] + dx
        dho_ref[...] = dho
        dhob_ref[...] = dho.astype(BF16)
        _acc(dgain_ref, dgain, first)

        @pl.when(i == ni - 1)
        def _():
            ones = jnp.ones((8, CHUNK), F32)
            for hd in range(heads):
                tot = lax.dot_general(ones, gacc[:, hd * CHUNK:(hd + 1) * CHUNK], (((1,), (1,)), ((), ())),
                                      preferred_element_type=F32, precision=lax.Precision.HIGHEST)
                db_ref[hd:hd + 1, :] = tot[0:1, :]

    row = lambda c: pl.BlockSpec((tm, c), lambda i: (i, 0))
    return _pcall(
        body, name="odd_bwd", grid=(ni,),
        in_specs=[row(d), row(d), _whole(wout.shape), row(e), row(cw), _whole(ws.shape), _whole(wst.shape), _whole(sgu.shape),
                  _whole(win.shape), row(d), _whole((1, d))],
        out_specs=[row(e), row(d), row(d), _whole((1, d)), _whole((1, cw)), _whole(ws.shape), _whole((heads, CHUNK))],
        out_shape=[jax.ShapeDtypeStruct((s, e), BF16), jax.ShapeDtypeStruct((s, d), F32), jax.ShapeDtypeStruct((s, d), BF16),
                   jax.ShapeDtypeStruct((1, d), F32), jax.ShapeDtypeStruct((1, cw), F32), jax.ShapeDtypeStruct(ws.shape, F32),
                   jax.ShapeDtypeStruct((heads, CHUNK), F32)],
        scratch_shapes=[pltpu.VMEM((tm, cw), F32), pltpu.VMEM((CHUNK, cw), F32)],
        compiler_params=_params("arbitrary"),
    )(dhb, dh, wout, pre, gate, ws, wst, sgu, win, h, gmix)


def _even_bwd(dhb, dh, wout, proj, cq, pooled, conva, wpool, wpoolt, pscale, win, x, gmix, tm):
    s, d = x.shape
    e = win.shape[1]
    aw = e // 4
    ni = s // tm

    def body(dhb_ref, dh_ref, wout_ref, proj_ref, cq_ref, pooled_ref, ca_ref, wp_ref, wpt_ref, ps_ref, win_ref, x_ref, gmix_ref,
             dproj_ref, dx_ref, dgain_ref, dca_ref, dwp_ref, dps_ref, cbuf, ebuf):
        i = pl.program_id(0)
        first = i == 0

        @pl.when(first)
        def _():
            cbuf[tm:tm + HALO, :] = jnp.zeros((HALO, aw), F32)
            ebuf[tm:tm + HALO, :] = jnp.zeros((HALO, aw), F32)

        dmix = _nt(dhb_ref[...], wout_ref[...])
        dy_a, dy_b = dmix[:, :aw], dmix[:, aw:]
        proj = proj_ref[...].astype(F32)
        a_b, a_c, a_v = (proj[:, k * aw:(k + 1) * aw] for k in range(3))
        da_b = dy_a * cq_ref[...].astype(F32)
        dcq = dy_a * a_b
        cbuf[0:tm, :] = dcq
        s1 = cbuf[pl.ds(1, tm), :]
        s2 = cbuf[pl.ds(2, tm), :]
        q = a_c * a_v
        for k, shifted in enumerate((s2, s1, dcq)):
            _acc(dca_ref.at[k:k + 1, :], jnp.sum(shifted * q, axis=0, keepdims=True), first)
        dq = ca_ref[2:3, :] * dcq + ca_ref[1:2, :] * s1 + ca_ref[0:1, :] * s2
        da_c = dq * a_v
        da_v = dq * a_c
        dps, dpool = [], []
        for g, w in enumerate(WINDOWS):
            cols = slice(g * LANE, (g + 1) * LANE)
            pooled = pooled_ref[:, cols]
            mixed = _nn(pooled, wp_ref[g])
            dps.append(jnp.sum(dy_b[:, cols] * mixed, axis=0, keepdims=True))
            dmixed = (dy_b[:, cols] * ps_ref[:, cols]).astype(BF16)
            _acc(dwp_ref.at[g], _tn(pooled, dmixed), first)
            dp = _nn(dmixed, wpt_ref[g])
            dpool.append(dp)
            ebuf[0:tm, cols] = dp / _counts((ni - 1 - i) * tm, tm, w)
        _acc(dps_ref, jnp.concatenate(dps, axis=1), first)
        dzs = []
        for g, w in enumerate(WINDOWS):
            cols = slice(g * LANE, (g + 1) * LANE)
            acc = ebuf[pl.ds(0, tm), cols]
            for m in range(1, w):
                acc = acc + ebuf[pl.ds(m, tm), cols]
            dzs.append(acc - dpool[g])
        dproj = jnp.concatenate([da_b, da_c, da_v] + dzs, axis=1).astype(BF16)
        dproj_ref[...] = dproj
        dx, dgain = _rms_bwd(_nt(dproj, win_ref[...]), x_ref[...], gmix_ref[...])
        dx_ref[...] = dh_ref[...] + dx
        _acc(dgain_ref, dgain, first)
        cbuf[tm:tm + HALO, :] = cbuf[0:HALO, :]
        ebuf[tm:tm + HALO, :] = ebuf[0:HALO, :]

    row = lambda c: pl.BlockSpec((tm, c), lambda i: (ni - 1 - i, 0))
    return _pcall(
        body, name="even_bwd", grid=(ni,),
        in_specs=[row(d), row(d), _whole(wout.shape), row(e), row(aw), row(aw), _whole(conva.shape), _whole(wpool.shape),
                  _whole(wpoolt.shape), _whole(pscale.shape), _whole(win.shape), row(d), _whole((1, d))],
        out_specs=[row(e), row(d), _whole((1, d)), _whole(conva.shape), _whole(wpool.shape), _whole(pscale.shape)],
        out_shape=[jax.ShapeDtypeStruct((s, e), BF16), jax.ShapeDtypeStruct((s, d), F32), jax.ShapeDtypeStruct((1, d), F32),
                   jax.ShapeDtypeStruct(conva.shape, F32), jax.ShapeDtypeStruct(wpool.shape, F32),
                   jax.ShapeDtypeStruct(pscale.shape, F32)],
        scratch_shapes=[pltpu.VMEM((tm + HALO, aw), F32), pltpu.VMEM((tm + HALO, aw), F32)],
        compiler_params=_params("arbitrary"),
    )(dhb, dh, wout, proj, cq, pooled, conva, wpool, wpoolt, pscale, win, x, gmix)


def _wgrad(a, b, tk, ts, name):
    s, ka = a.shape
    nb = b.shape[1]
    nt = s // ts

    def body(a_ref, b_ref, o_ref, acc):
        t = pl.program_id(1)
        _acc(acc, _tn(a_ref[...], b_ref[...]), t == 0)

        @pl.when(t == nt - 1)
        def _():
            o_ref[...] = acc[...].astype(BF16)

    return _pcall(
        body, name=name, grid=(ka // tk, nt),
        in_specs=[pl.BlockSpec((ts, tk), lambda k, t: (t, k)), pl.BlockSpec((ts, nb), lambda k, t: (t, 0))],
        out_specs=pl.BlockSpec((tk, nb), lambda k, t: (k, 0)),
        out_shape=jax.ShapeDtypeStruct((ka, nb), BF16),
        scratch_shapes=[pltpu.VMEM((tk, nb), F32)],
        compiler_params=_params("arbitrary", "arbitrary"),
    )(a, b)


def _adamw(parts, w, m, v, tr, name, layer=None, into=None):
    r, c = w.shape[-2:]
    n, rp, cp = parts.shape
    tp = tr if rp == r else rp
    assert rp == r or tr == r

    def body(p_ref, w_ref, m_ref, v_ref, *rest):
        g_ref, d_ref, mo_ref, vo_ref = rest[-4:]
        g = p_ref[0, 0:tr, 0:c].astype(F32)
        for j in range(1, n):
            g = g + p_ref[j, 0:tr, 0:c].astype(F32)
        g_ref[...] = g
        mn = ADAM_B1 * m_ref[...] + (1.0 - ADAM_B1) * g
        vn = ADAM_B2 * v_ref[...] + (1.0 - ADAM_B2) * (g * g)
        mo_ref[...] = mn
        vo_ref[...] = vn
        m_hat = mn / (1.0 - ADAM_B1 ** ADAM_STEP)
        v_hat = vn / (1.0 - ADAM_B2 ** ADAM_STEP)
        d_ref[...] = -ADAM_LR * (m_hat / (jnp.sqrt(v_hat) + ADAM_EPS) + ADAM_WD * w_ref[...])

    if layer is None:
        row = pl.BlockSpec((tr, c), lambda i: (i, 0))
    else:
        row = pl.BlockSpec((None, tr, c), lambda i: (layer, i, 0))
    out = jax.ShapeDtypeStruct(w.shape, F32)
    prev = [] if into is None else list(into)
    return _pcall(
        body, name=name, grid=(r // tr,),
        in_specs=[pl.BlockSpec((n, tp, cp), lambda i: (0, i, 0)), row, row, row] + [pl.BlockSpec(memory_space=pl.ANY)] * len(prev),
        out_specs=[row, row, row, row], out_shape=[out, out, out, out],
        input_output_aliases={4 + k: k for k in range(len(prev))},
        compiler_params=_params("arbitrary"),
    )(parts, w, m, v, *prev)


def _pair_add(grad, other, spec, core, name):
    axis, width = spec
    slot = other.shape[1:]

    def body(core_ref, g_ref, o_ref, out_ref):
        out_ref[...] = (g_ref[...].astype(F32) + o_ref[...].astype(F32)).astype(BF16)

    if axis == 0:
        gspec = pl.BlockSpec(slot, lambda q, core_ref: (2 * q + core_ref[0], 0))
    else:
        gspec = pl.BlockSpec(slot, lambda q, core_ref: (0, 2 * q + core_ref[0]))
    per_chip = pl.BlockSpec((None,) + slot, lambda q, core_ref: (q, 0, 0))
    return _pcall(
        body, name=name,
        grid_spec=pltpu.PrefetchScalarGridSpec(num_scalar_prefetch=1, grid=(N_CHIP,), in_specs=[gspec, per_chip], out_specs=per_chip),
        out_shape=jax.ShapeDtypeStruct(other.shape, BF16),
        compiler_params=_params("arbitrary"),
    )(core, grad, other)


def _hbm():
    return pl.BlockSpec(memory_space=pltpu.HBM)


def _window(ref, spec, j):
    axis, width = spec
    start = pl.multiple_of(j * width, width)
    return ref.at[(slice(None),) * axis + (pl.ds(start, width),)]


def _here():
    return lax.axis_index("x"), lax.axis_index("y"), lax.axis_index("c")


def _gather(shards, specs, fulls, name):
    n = len(shards)

    def body(*refs):
        ins, outs = refs[:n], refs[n:2 * n]
        send_sems, recv_sems, local_sems = refs[2 * n:]
        x, y, c = _here()
        me, sibling = (x, y, c), (x, y, 1 - c)
        chips = [(1 - x, y), (x, 1 - y), (1 - x, 1 - y)]

        def slot(t, dev):
            return _window(outs[t], specs[t], 4 * dev[0] + 2 * dev[1] + dev[2])

        def copy(t, k, block, to, src=None):
            return pltpu.make_async_remote_copy(
                src_ref=slot(t, block) if src is None else src, dst_ref=slot(t, block),
                send_sem=send_sems.at[7 * t + k], recv_sem=recv_sems.at[7 * t + k], device_id=to, device_id_type=MESH)

        mine, first, passed = [], [], []
        for t in range(n):
            mine.append(pltpu.make_async_copy(ins[t], slot(t, me), local_sems.at[t]))
            mine[t].start()
            first.append([copy(t, 0, me, sibling, src=ins[t])]
                         + [copy(t, 1 + j, me, (*chip, c), src=ins[t]) for j, chip in enumerate(chips)])
            for cp in first[t]:
                cp.start()
        for t in range(n):
            passed.append([copy(t, 4 + j, (*chip, c), sibling) for j, chip in enumerate(chips)])
            for j, chip in enumerate(chips):
                copy(t, 1 + j, (*chip, c), me).wait_recv()
                passed[t][j].start()
        for t in range(n):
            copy(t, 0, sibling, me).wait_recv()
            for j, chip in enumerate(chips):
                copy(t, 4 + j, (*chip, 1 - c), me).wait_recv()
        for t in range(n):
            for cp in first[t] + passed[t]:
                cp.wait_send()
            mine[t].wait()

    return _pcall(
        body, name=name, in_specs=[_hbm()] * n, out_specs=[_hbm()] * n, out_shape=list(fulls),
        scratch_shapes=[pltpu.SemaphoreType.DMA((7 * n,)), pltpu.SemaphoreType.DMA((7 * n,)), pltpu.SemaphoreType.DMA((n,))],
    )(*shards)


def _pair_exchange(grads, specs, name):
    n = len(grads)

    def slot_shape(a, spec):
        shp = list(a.shape)
        shp[spec[0]] = spec[1]
        return tuple(shp)

    def body(*refs):
        ins, outs = refs[:n], refs[n:2 * n]
        send_sems, recv_sems = refs[2 * n:]
        x, y, c = _here()
        sibling = (x, y, 1 - c)
        for t in range(n):
            for q in range(N_CHIP):
                pltpu.make_async_remote_copy(
                    src_ref=_window(ins[t], specs[t], 2 * q + (1 - c)), dst_ref=outs[t].at[q],
                    send_sem=send_sems.at[t], recv_sem=recv_sems.at[t], device_id=sibling, device_id_type=MESH).start()
        for t in range(n):
            every = pltpu.make_async_remote_copy(src_ref=outs[t], dst_ref=outs[t], send_sem=send_sems.at[t],
                                                 recv_sem=recv_sems.at[t], device_id=sibling, device_id_type=MESH)
            every.wait_send()
            every.wait_recv()

    return _pcall(
        body, name=name, in_specs=[_hbm()] * n, out_specs=[_hbm()] * n,
        out_shape=[jax.ShapeDtypeStruct((N_CHIP,) + slot_shape(a, sp), a.dtype) for a, sp in zip(grads, specs)],
        scratch_shapes=[pltpu.SemaphoreType.DMA((n,)), pltpu.SemaphoreType.DMA((n,))],
    )(*grads)


def _chip_exchange(pairs, slotted, whole, name):
    tensors = list(pairs) + list(slotted) + list(whole)
    n, npair, nslot = len(tensors), len(pairs), len(pairs) + len(slotted)

    def body(*refs):
        ins, outs = refs[:n], refs[n:2 * n]
        send_sems, recv_sems, local_sems = refs[2 * n:]
        x, y, c = _here()
        me, chip = 4 * x + 2 * y + c, 2 * x + y
        chips = [(1 - x, y), (x, 1 - y), (1 - x, 1 - y)]
        peers = [(x, y, 1 - c)] + [(*q, c) for q in chips] + [(*q, 1 - c) for q in chips]

        def index(dev):
            return 4 * dev[0] + 2 * dev[1] + dev[2]

        def copy(t, k, arriving):
            peer = peers[k]
            if t < npair:
                src, mine, theirs = ins[t].at[2 * peer[0] + peer[1]], chip, 2 * peer[0] + peer[1]
            else:
                src, mine, theirs = (ins[t].at[index(peer)] if t < nslot else ins[t]), me, index(peer)
            return pltpu.make_async_remote_copy(
                src_ref=src, dst_ref=outs[t].at[theirs if arriving else mine],
                send_sem=send_sems.at[7 * t + k], recv_sem=recv_sems.at[7 * t + k], device_id=peer, device_id_type=MESH)

        def fan(t):
            return range(1, 4) if t < npair else range(7)

        own, sent = [], []
        for t in range(n):
            if t < npair:
                own.append(pltpu.make_async_copy(ins[t].at[chip], outs[t].at[chip], local_sems.at[t]))
            else:
                own.append(pltpu.make_async_copy(ins[t].at[me] if t < nslot else ins[t], outs[t].at[me], local_sems.at[t]))
            own[t].start()
            for k in fan(t):
                sent.append(copy(t, k, False))
                sent[-1].start()
        for t in range(n):
            for k in fan(t):
                copy(t, k, True).wait_recv()
        for cp in sent:
            cp.wait_send()
        for t in range(n):
            own[t].wait()

    return _pcall(
        body, name=name, in_specs=[_hbm()] * n, out_specs=[_hbm()] * n,
        out_shape=[jax.ShapeDtypeStruct(a.shape, a.dtype) for a in list(pairs) + list(slotted)]
        + [jax.ShapeDtypeStruct((N_DEV,) + a.shape, a.dtype) for a in whole],
        scratch_shapes=[pltpu.SemaphoreType.DMA((7 * n,)), pltpu.SemaphoreType.DMA((7 * n,)), pltpu.SemaphoreType.DMA((n,))],
    )(*tensors)


SMALL_ROWS = 24
REP_COLS = 1024
REP_PARTS = (("norm_mix", 0, 2), ("norm_ffn", 8, 2), ("final_norm", 16, 1), ("pool_scale", 24, 1), ("b_spatial", 32, 1),
             ("b_conv_ffn", 40, 6), ("w_pool", 48, 64), ("w_spatial", 112, 128), ("loss", 240, 1))
REP_ROWS = 248


def _pad_to(a, rows, cols):
    return jnp.pad(a, ((0, rows - a.shape[0]), (0, cols - a.shape[1])))


def _pack_small(conv_a, sgu_norm, conv_ffn, cols):
    return jnp.concatenate([_pad_to(conv_a, 8, cols), _pad_to(sgu_norm, 8, cols),
                            _pad_to(conv_ffn.reshape(-1, conv_ffn.shape[-1]), 8, cols)], axis=0)


def _unpack_small(p, ca_w, sg_w, cf_w):
    return p[0:3, 0:ca_w], p[8:9, 0:sg_w], p[16:22, 0:cf_w].reshape(2, 3, cf_w)


def _pack_rep(t, loss=None):
    bc = t["b_conv_ffn"]
    rows = {"norm_mix": t["norm_mix"], "norm_ffn": t["norm_ffn"], "final_norm": t["final_norm"].reshape(1, -1),
            "pool_scale": t["pool_scale"].reshape(1, -1), "b_spatial": t["b_spatial"].reshape(1, -1),
            "b_conv_ffn": _pad_to(bc, 2, 3 * REP_COLS).reshape(6, REP_COLS), "w_pool": t["w_pool"].reshape(-1, REP_COLS),
            "w_spatial": t["w_spatial"].reshape(-1, REP_COLS), "loss": jnp.zeros((1, REP_COLS), F32) if loss is None else loss}
    return jnp.concatenate([_pad_to(rows[nm], -(-cnt // 8) * 8, REP_COLS) for nm, _, cnt in REP_PARTS], axis=0)


def _unpack_rep(p, like):
    f = like["b_conv_ffn"].shape[1]
    at = {nm: p[r0:r0 + cnt] for nm, r0, cnt in REP_PARTS}
    return {
        "norm_mix": at["norm_mix"], "norm_ffn": at["norm_ffn"], "final_norm": at["final_norm"][0],
        "pool_scale": at["pool_scale"][:, 0:like["pool_scale"].shape[1]], "b_spatial": at["b_spatial"].reshape(like["b_spatial"].shape),
        "b_conv_ffn": at["b_conv_ffn"].reshape(2, 3 * REP_COLS)[:, 0:f], "w_pool": at["w_pool"].reshape(like["w_pool"].shape),
        "w_spatial": at["w_spatial"].reshape(like["w_spatial"].shape),
    }


def _pad_slots(a, width, padded):
    a = a.reshape(*a.shape[:-1], N_DEV, width)
    a = jnp.pad(a, ((0, 0),) * (a.ndim - 1) + ((0, padded - width),))
    return a.reshape(*a.shape[:-2], N_DEV * padded)


def _unpad_slots(a, width, padded):
    a = a.reshape(*a.shape[:-1], N_DEV, padded)[..., 0:width]
    return a.reshape(*a.shape[:-2], N_DEV * width)


def kernel(x, norm_mix, norm_ffn, final_norm, w_in_even, conv_a, w_pool, pool_scale, w_out_even, w_in_odd, sgu_norm, w_spatial, b_spatial, w_out_odd, w_ffn_gate, w_ffn_up, conv_ffn, b_conv_ffn, w_ffn_down, loss_target, m_norm_mix, m_norm_ffn, m_final_norm, m_w_in_even, m_conv_a, m_w_pool, m_pool_scale, m_w_out_even, m_w_in_odd, m_sgu_norm, m_w_spatial, m_b_spatial, m_w_out_odd, m_w_ffn_gate, m_w_ffn_up, m_conv_ffn, m_b_conv_ffn, m_w_ffn_down, v_norm_mix, v_norm_ffn, v_final_norm, v_w_in_even, v_conv_a, v_w_pool, v_pool_scale, v_w_out_even, v_w_in_odd, v_sgu_norm, v_w_spatial, v_b_spatial, v_w_out_odd, v_w_ffn_gate, v_w_ffn_up, v_conv_ffn, v_b_conv_ffn, v_w_ffn_down):
    s, d = x.shape[1], x.shape[2]
    x2, target = x[0], loss_target[0]
    tm = min(512, s)
    tm_wide = min(2048, s)
    tn = 256
    row = lambda a: a.reshape(1, -1)
    ein, ro = w_in_even.shape[2], w_out_even.shape[1]
    fs = w_ffn_gate.shape[2]
    fsp = -(-fs // LANE) * LANE
    fp = N_DEV * fsp
    full = lambda shape, dtype=BF16: jax.ShapeDtypeStruct(shape, dtype)

    gu_s = [jnp.pad(jnp.stack([w_ffn_gate[l], w_ffn_up[l]]), ((0, 0), (0, 0), (0, fsp - fs))).astype(BF16) for l in range(2)]
    wd_s = [jnp.pad(w_ffn_down[l], ((0, fsp - fs), (0, 0))).astype(BF16) for l in range(2)]
    small_s = _pack_small(conv_a[0], sgu_norm, conv_ffn, fsp)[None]
    win_e, wout_e, gsmall = _gather(
        [w_in_even[0].astype(BF16), w_out_even[0].astype(BF16), small_s], [(1, ein), (0, ro), (0, 1)],
        [full((d, N_DEV * ein)), full((N_DEV * ro, d)), full((N_DEV, SMALL_ROWS, fsp), F32)], "gather_mix0")
    wgu0, wd0 = _gather([gu_s[0], wd_s[0]], [(2, fsp), (0, fsp)], [full((2, d, fp)), full((fp, d))], "gather_ffn0")
    win_o, wout_o = _gather([w_in_odd[0].astype(BF16), w_out_odd[0].astype(BF16)], [(1, ein), (0, ro)],
                            [full((d, N_DEV * ein)), full((N_DEV * ro, d))], "gather_mix1")
    wgu1, wd1 = _gather([gu_s[1], wd_s[1]], [(2, fsp), (0, fsp)], [full((2, d, fp)), full((fp, d))], "gather_ffn1")
    wgu, wd = (wgu0, wgu1), (wd0, wd1)
    ca_full = jnp.moveaxis(gsmall[:, 0:3, 0:conv_a.shape[2]], 0, 1).reshape(3, -1)
    sgu_full = gsmall[:, 8, 0:sgu_norm.shape[1]].reshape(1, -1)
    cf_full = jnp.moveaxis(gsmall[:, 16:22, :].reshape(N_DEV, 2, 3, fsp), 0, 2).reshape(2, 3, fp)
    cb_full = _pad_slots(b_conv_ffn, fs, fsp)

    tril = jnp.tril(jnp.ones((CHUNK, CHUNK), F32))
    ws_m = w_spatial[0] * tril
    ws_b = ws_m.astype(BF16)
    wst_b = jnp.swapaxes(ws_m, 1, 2).astype(BF16)
    bfull = jnp.repeat(b_spatial[0].T, CHUNK, axis=1)
    wpool_b = w_pool[0].astype(BF16)
    wpoolt_b = jnp.swapaxes(w_pool[0], 1, 2).astype(BF16)

    xn0, proj0, cq0, pooled0, mix0, h1, hn0 = _even_fwd(x2, norm_mix[0:1], norm_ffn[0:1], win_e, ca_full, wpool_b, pool_scale, wout_e, tm)
    g0, gc0, up0, a0 = _ffn_fwd1(hn0, wgu[0], cf_full[0], cb_full[0:1], tm_wide, tn, "ffn_fwd1_l0")
    h2, xn1 = _ffn_fwd2(a0, wd[0], h1, norm_mix[1:2], tm, "ffn_fwd2_l0")
    pre1, gate1, mixo1, h3, hn1 = _odd_fwd(xn1, h2, win_o, sgu_full, ws_b, bfull, wout_o, norm_ffn[1:2], tm)
    g1, gc1, up1, a1 = _ffn_fwd1(hn1, wgu[1], cf_full[1], cb_full[1:2], tm_wide, tn, "ffn_fwd1_l1")
    h4, _ = _ffn_fwd2(a1, wd[1], h3, row(final_norm), tm, "ffn_fwd2_l1")

    dh4, dh4b, d_final, lossvec = _loss_bwd(h4, row(final_norm), target, tm)
    dg1, dup1, dcw1, dcb1 = _ffn_bwd1(dh4b, g1, gc1, up1, wd[1], cf_full[1], tm_wide, tn, "ffn_bwd1_l1")
    dh3, dh3b, d_nffn1 = _ffn_bwd2(dg1, dup1, wgu[1], h3, norm_ffn[1:2], dh4, tm // 2, "ffn_bwd2_l1")
    dpre1, dh2, dh2b, d_nmix1, d_sgu, d_ws, d_b = _odd_bwd(dh3b, dh3, wout_o, pre1, gate1, ws_b, wst_b, sgu_full, win_o, h2, norm_mix[1:2], tm)
    dg0, dup0, dcw0, dcb0 = _ffn_bwd1(dh2b, g0, gc0, up0, wd[0], cf_full[0], tm_wide, tn, "ffn_bwd1_l0")
    dh1, dh1b, d_nffn0 = _ffn_bwd2(dg0, dup0, wgu[0], h1, norm_ffn[0:1], dh2, tm // 2, "ffn_bwd2_l0")
    dproj0, grad_x, d_nmix0, d_ca, d_wp, d_ps = _even_bwd(dh1b, dh1, wout_e, proj0, cq0, pooled0, ca_full, wpool_b, wpoolt_b, pool_scale,
                                                         win_e, x2, norm_mix[0:1], tm)
    ts = min(1024, s)
    grads = [
        (_wgrad(xn0, dproj0, 512, ts, "wgrad_in_even"), (1, ein)), (_wgrad(mix0, dh1b, 512, ts, "wgrad_out_even"), (0, ro)),
        (_wgrad(xn1, dpre1, 512, ts, "wgrad_in_odd"), (1, ein)), (_wgrad(mixo1, dh3b, 512, ts, "wgrad_out_odd"), (0, ro)),
        (_wgrad(hn0, dg0, 512, ts, "wgrad_gate_l0"), (1, fsp)), (_wgrad(hn0, dup0, 512, ts, "wgrad_up_l0"), (1, fsp)),
        (_wgrad(a0, dh2b, tn, s, "wgrad_down_l0"), (0, fsp)),
        (_wgrad(hn1, dg1, 512, ts, "wgrad_gate_l1"), (1, fsp)), (_wgrad(hn1, dup1, 512, ts, "wgrad_up_l1"), (1, fsp)),
        (_wgrad(a1, dh4b, tn, s, "wgrad_down_l1"), (0, fsp)),
    ]

    core = lax.axis_index("c").astype(jnp.int32).reshape(1)
    others = _pair_exchange([g for g, _ in grads], [sp for _, sp in grads], "pair_exchange")
    pairs = [_pair_add(g, o, sp, core, "pair_add_%d" % k) for k, ((g, sp), o) in enumerate(zip(grads, others))]
    d_small = jnp.stack([_pack_small(a, b, c, fsp) for a, b, c in zip(
        jnp.moveaxis(d_ca.reshape(3, N_DEV, -1), 1, 0), jnp.moveaxis(d_sgu.reshape(1, N_DEV, -1), 1, 0),
        jnp.moveaxis(jnp.stack([dcw0, dcw1]).reshape(2, 3, N_DEV, fsp), 2, 0))])
    rep = {"norm_mix": jnp.concatenate([d_nmix0, d_nmix1]), "norm_ffn": jnp.concatenate([d_nffn0, d_nffn1]), "final_norm": d_final[0],
           "pool_scale": d_ps, "b_spatial": d_b, "b_conv_ffn": _unpad_slots(jnp.concatenate([dcb0, dcb1]), fs, fsp), "w_pool": d_wp,
           "w_spatial": d_ws * tril}
    *sums, r_small, r_rep = _chip_exchange(pairs, [d_small], [_pack_rep(rep, lossvec)], "chip_exchange")
    loss = jnp.sum(r_rep[:, REP_PARTS[-1][1], :])

    out = {}
    out["w_in_even"] = _adamw(sums[0], w_in_even[0], m_w_in_even[0], v_w_in_even[0], 256, "adamw_in_even")
    out["w_out_even"] = _adamw(sums[1], w_out_even[0], m_w_out_even[0], v_w_out_even[0], ro, "adamw_out_even")
    out["w_in_odd"] = _adamw(sums[2], w_in_odd[0], m_w_in_odd[0], v_w_in_odd[0], 256, "adamw_in_odd")
    out["w_out_odd"] = _adamw(sums[3], w_out_odd[0], m_w_out_odd[0], v_w_out_odd[0], ro, "adamw_out_odd")
    for nm, k, tr, w, m, v in (("w_ffn_gate", 4, 256, w_ffn_gate, m_w_ffn_gate, v_w_ffn_gate), ("w_ffn_up", 5, 256, w_ffn_up, m_w_ffn_up, v_w_ffn_up),
                               ("w_ffn_down", 6, fs, w_ffn_down, m_w_ffn_down, v_w_ffn_down)):
        l0 = _adamw(sums[k], w, m, v, tr, "adamw_%s_l0" % nm, layer=0)
        out[nm] = _adamw(sums[k + 3], w, m, v, tr, "adamw_%s_l1" % nm, layer=1, into=l0)
    small = _adamw(r_small, _pack_small(conv_a[0], sgu_norm, conv_ffn, fsp), _pack_small(m_conv_a[0], m_sgu_norm, m_conv_ffn, fsp),
                   _pack_small(v_conv_a[0], v_sgu_norm, v_conv_ffn, fsp), SMALL_ROWS, "adamw_small")
    rep_w = {"norm_mix": norm_mix, "norm_ffn": norm_ffn, "final_norm": final_norm, "pool_scale": pool_scale, "b_spatial": b_spatial,
             "b_conv_ffn": b_conv_ffn, "w_pool": w_pool, "w_spatial": w_spatial}
    rep_m = {"norm_mix": m_norm_mix, "norm_ffn": m_norm_ffn, "final_norm": m_final_norm, "pool_scale": m_pool_scale,
             "b_spatial": m_b_spatial, "b_conv_ffn": m_b_conv_ffn, "w_pool": m_w_pool, "w_spatial": m_w_spatial}
    rep_v = {"norm_mix": v_norm_mix, "norm_ffn": v_norm_ffn, "final_norm": v_final_norm, "pool_scale": v_pool_scale,
             "b_spatial": v_b_spatial, "b_conv_ffn": v_b_conv_ffn, "w_pool": v_w_pool, "w_spatial": v_w_spatial}
    reps = _adamw(r_rep, _pack_rep(rep_w), _pack_rep(rep_m), _pack_rep(rep_v), REP_ROWS, "adamw_replicated")

    names = ["norm_mix", "norm_ffn", "final_norm", "w_in_even", "conv_a", "w_pool", "pool_scale", "w_out_even", "w_in_odd", "sgu_norm",
             "w_spatial", "b_spatial", "w_out_odd", "w_ffn_gate", "w_ffn_up", "conv_ffn", "b_conv_ffn", "w_ffn_down"]
    like = {"norm_mix": norm_mix, "norm_ffn": norm_ffn, "final_norm": final_norm, "w_in_even": w_in_even, "conv_a": conv_a,
            "w_pool": w_pool, "pool_scale": pool_scale, "w_out_even": w_out_even, "w_in_odd": w_in_odd, "sgu_norm": sgu_norm,
            "w_spatial": w_spatial, "b_spatial": b_spatial, "w_out_odd": w_out_odd, "w_ffn_gate": w_ffn_gate, "w_ffn_up": w_ffn_up,
            "conv_ffn": conv_ffn, "b_conv_ffn": b_conv_ffn, "w_ffn_down": w_ffn_down}
    groups = []
    for k in range(4):
        ca_k, sg_k, cf_k = _unpack_small(small[k], conv_a.shape[2], sgu_norm.shape[1], conv_ffn.shape[2])
        vals = dict(_unpack_rep(reps[k], like))
        vals.update(conv_a=ca_k, sgu_norm=sg_k, conv_ffn=cf_k)
        for nm in ("w_in_even", "w_in_odd", "w_out_even", "w_out_odd", "w_ffn_gate", "w_ffn_up", "w_ffn_down"):
            vals[nm] = out[nm][k]
        groups.append([vals[nm].reshape(like[nm].shape) for nm in names])
    return (loss, grad_x[None], *groups[0], *groups[1], *groups[2], *groups[3])
```

```python
import functools

import jax
import jax.numpy as jnp
from jax import lax
from jax.experimental import pallas as pl
from jax.experimental.pallas import tpu as pltpu

F32, BF16 = jnp.float32, jnp.bfloat16
EPS = 1e-6
WINDOWS = (2, 4, 8, 16)
HALO = 16
CHUNK = 128
N_DEV = 8
N_CHIP = 4
MESH = pl.DeviceIdType.MESH
VMEM_LIMIT = 56 * 2**20
LANE = 128
ADAM_LR, ADAM_B1, ADAM_B2, ADAM_EPS, ADAM_WD, ADAM_STEP = 0.001, 0.9, 0.999, 1e-08, 0.01, 10
INV_SQRT2 = 0.7071067811865476
INV_SQRT2PI = 0.3989422804014327


def _pcall(body, comm=None, **kw):
    if comm is None:
        return pl.pallas_call(body, **kw)
    in_specs, out_specs, out_shape = list(kw.pop("in_specs")), kw.pop("out_specs"), kw.pop("out_shape")
    single = not isinstance(out_shape, (list, tuple))
    out_specs, out_shape = ([out_specs], [out_shape]) if single else (list(out_specs), list(out_shape))
    scratch = list(kw.pop("scratch_shapes", []))
    grid = kw.get("grid", ())
    n_in, n_out, n_scr, c_in, c_out = len(in_specs), len(out_specs), len(scratch), len(comm.ins), len(comm.out_shape)

    def hosted(*refs):
        cuts = [0, n_in, n_in + c_in, n_in + c_in + n_out, n_in + c_in + n_out + c_out, n_in + c_in + n_out + c_out + n_scr, len(refs)]
        ins, cins, outs, couts, scr, sems = (refs[a:b] for a, b in zip(cuts[:-1], cuts[1:]))
        first, last = True, True
        for axis, size in enumerate(grid):
            first = jnp.logical_and(first, pl.program_id(axis) == 0)
            last = jnp.logical_and(last, pl.program_id(axis) == size - 1)
        if grid:
            pl.when(first)(lambda: comm.start(cins, couts, sems))
            body(*ins, *outs, *scr)
            pl.when(last)(lambda: comm.finish(cins, couts, sems))
        else:
            comm.start(cins, couts, sems)
            body(*ins, *outs, *scr)
            comm.finish(cins, couts, sems)

    call = pl.pallas_call(hosted, in_specs=in_specs + [_hbm()] * c_in, out_specs=out_specs + [_hbm()] * c_out,
                          out_shape=out_shape + list(comm.out_shape), scratch_shapes=scratch + list(comm.sems), **kw)

    def run(*args):
        res = call(*args, *comm.ins)
        own = res[0] if single else res[:n_out]
        return own, res[n_out:]

    return run


def _params(*sem):
    return pltpu.CompilerParams(dimension_semantics=sem, vmem_limit_bytes=VMEM_LIMIT)


def _whole(shape):
    return pl.BlockSpec(shape, lambda *_: (0,) * len(shape))


def _nn(a, b):
    return jnp.dot(a, b, preferred_element_type=F32)


def _nt(a, b):
    return lax.dot_general(a, b, (((1,), (1,)), ((), ())), preferred_element_type=F32)


def _tn(a, b):
    return lax.dot_general(a, b, (((0,), (0,)), ((), ())), preferred_element_type=F32)


def _rms(x, gain):
    r = lax.rsqrt(jnp.mean(x * x, axis=-1, keepdims=True) + EPS)
    return x * r * gain


def _rms_bwd(dy, x, gain):
    r = lax.rsqrt(jnp.mean(x * x, axis=-1, keepdims=True) + EPS)
    xh = x * r
    dgain = jnp.sum(dy * xh, axis=0, keepdims=True)
    dxh = dy * gain
    dx = r * (dxh - xh * jnp.mean(dxh * xh, axis=-1, keepdims=True))
    return dx, dgain


def _gelu(x):
    return 0.5 * x * (1.0 + lax.erf(x * INV_SQRT2))


def _gelu_grad(x):
    return 0.5 * (1.0 + lax.erf(x * INV_SQRT2)) + x * jnp.exp(-0.5 * x * x) * INV_SQRT2PI


def _acc(ref, val, first):
    @pl.when(first)
    def _():
        ref[...] = val

    @pl.when(jnp.logical_not(first))
    def _():
        ref[...] += val


def _counts(row0, tm, w):
    pos1 = (row0 + lax.broadcasted_iota(jnp.int32, (tm, 1), 0) + 1).astype(F32)
    return jnp.minimum(pos1, float(w))


def _even_fwd(x, gmix, gffn, win, conva, wpool, pscale, wout, tm, comm=None):
    s, d = x.shape
    e = win.shape[1]
    aw = e // 4

    def body(x_ref, gmix_ref, gffn_ref, win_ref, ca_ref, wp_ref, ps_ref, wout_ref,
             xn_ref, proj_ref, cq_ref, pooled_ref, mix_ref, h_ref, hn_ref, qbuf, zbuf):
        i = pl.program_id(0)

        @pl.when(i == 0)
        def _():
            qbuf[0:HALO, :] = jnp.zeros((HALO, aw), F32)
            zbuf[0:HALO, :] = jnp.zeros((HALO, aw), F32)

        xv = x_ref[...]
        xn = _rms(xv, gmix_ref[...]).astype(BF16)
        xn_ref[...] = xn
        proj = _nn(xn, win_ref[...])
        proj_ref[...] = proj.astype(BF16)
        a_b, a_c, a_v, z = (proj[:, k * aw:(k + 1) * aw] for k in range(4))
        q = a_c * a_v
        qbuf[HALO:HALO + tm, :] = q
        cq = ca_ref[2:3, :] * q + ca_ref[1:2, :] * qbuf[pl.ds(HALO - 1, tm), :] + ca_ref[0:1, :] * qbuf[pl.ds(HALO - 2, tm), :]
        cq_ref[...] = cq.astype(BF16)
        y_a = a_b * cq
        zbuf[HALO:HALO + tm, :] = z
        ys = []
        for g, w in enumerate(WINDOWS):
            cols = slice(g * LANE, (g + 1) * LANE)
            acc = z[:, cols]
            for m in range(1, w):
                acc = acc + zbuf[pl.ds(HALO - m, tm), cols]
            pooled = (acc / _counts(i * tm, tm, w) - z[:, cols]).astype(BF16)
            pooled_ref[:, cols] = pooled
            ys.append(_nn(pooled, wp_ref[g]))
        y_b = jnp.concatenate(ys, axis=1) * ps_ref[...]
        mix = jnp.concatenate([y_a, y_b], axis=1).astype(BF16)
        mix_ref[...] = mix
        h = xv + _nn(mix, wout_ref[...])
        h_ref[...] = h
        hn_ref[...] = _rms(h, gffn_ref[...]).astype(BF16)
        qbuf[0:HALO, :] = qbuf[tm:tm + HALO, :]
        zbuf[0:HALO, :] = zbuf[tm:tm + HALO, :]

    row = lambda c: pl.BlockSpec((tm, c), lambda i: (i, 0))
    return _pcall(
        body, comm=comm, name="even_fwd", grid=(s // tm,),
        in_specs=[row(d), _whole((1, d)), _whole((1, d)), _whole(win.shape), _whole(conva.shape), _whole(wpool.shape),
                  _whole(pscale.shape), _whole(wout.shape)],
        out_specs=[row(d), row(e), row(aw), row(aw), row(d), row(d), row(d)],
        out_shape=[jax.ShapeDtypeStruct((s, d), BF16), jax.ShapeDtypeStruct((s, e), BF16), jax.ShapeDtypeStruct((s, aw), BF16),
                   jax.ShapeDtypeStruct((s, aw), BF16), jax.ShapeDtypeStruct((s, d), BF16), jax.ShapeDtypeStruct((s, d), F32),
                   jax.ShapeDtypeStruct((s, d), BF16)],
        scratch_shapes=[pltpu.VMEM((tm + HALO, aw), F32), pltpu.VMEM((tm + HALO, aw), F32)],
        compiler_params=_params("arbitrary"),
    )(x, gmix, gffn, win, conva, wpool, pscale, wout)


def _ffn_fwd1(hn, wg, wu, cw, cb, tm, tn, name, comm=None):
    s, d = hn.shape
    f = wg.shape[1]

    def body(hn_ref, wg_ref, wu_ref, cw_ref, cb_ref, g_ref, gc_ref, up_ref, a_ref, gbuf):
        i = pl.program_id(1)

        @pl.when(i == 0)
        def _():
            gbuf[0:HALO, :] = jnp.zeros((HALO, tn), F32)

        hv = hn_ref[...]
        g = _nn(hv, wg_ref[...])
        g_ref[...] = g.astype(BF16)
        gbuf[HALO:HALO + tm, :] = g
        gc = (cw_ref[2:3, :] * g + cw_ref[1:2, :] * gbuf[pl.ds(HALO - 1, tm), :]
              + cw_ref[0:1, :] * gbuf[pl.ds(HALO - 2, tm), :] + cb_ref[...])
        gc_ref[...] = gc.astype(BF16)
        up = _nn(hv, wu_ref[...])
        up_ref[...] = up.astype(BF16)
        a_ref[...] = (gc * jax.nn.sigmoid(gc) * up).astype(BF16)
        gbuf[0:HALO, :] = gbuf[tm:tm + HALO, :]

    tile = pl.BlockSpec((tm, tn), lambda j, i: (i, j))
    wcol = lambda r: pl.BlockSpec((r, tn), lambda j, i: (0, j))
    out = jax.ShapeDtypeStruct((s, f), BF16)
    return _pcall(
        body, comm=comm, name=name, grid=(f // tn, s // tm),
        in_specs=[pl.BlockSpec((tm, d), lambda j, i: (i, 0)), wcol(d), wcol(d), wcol(3), wcol(1)],
        out_specs=[tile, tile, tile, tile], out_shape=[out, out, out, out],
        scratch_shapes=[pltpu.VMEM((tm + HALO, tn), F32)],
        compiler_params=_params("arbitrary", "arbitrary"),
    )(hn, wg, wu, cw, cb)


def _ffn_fwd2(a, wd, h, gain, tm, name, comm=None):
    s, d = h.shape
    f = a.shape[1]

    def body(a_ref, wd_ref, h_ref, gain_ref, ho_ref, hn_ref):
        ho = h_ref[...] + _nn(a_ref[...], wd_ref[...])
        ho_ref[...] = ho
        hn_ref[...] = _rms(ho, gain_ref[...]).astype(BF16)

    row = lambda c: pl.BlockSpec((tm, c), lambda i: (i, 0))
    return _pcall(
        body, comm=comm, name=name, grid=(s // tm,),
        in_specs=[row(f), _whole(wd.shape), row(d), _whole((1, d))],
        out_specs=[row(d), row(d)],
        out_shape=[jax.ShapeDtypeStruct((s, d), F32), jax.ShapeDtypeStruct((s, d), BF16)],
        compiler_params=_params("arbitrary"),
    )(a, wd, h, gain)


def _odd_fwd(xn, h, win, sgu, ws, bfull, wout, gffn, tm, comm=None):
    s, d = h.shape
    e = win.shape[1]
    cw = e // 2
    heads = ws.shape[0]

    def body(xn_ref, h_ref, win_ref, sgu_ref, ws_ref, b_ref, wout_ref, gffn_ref,
             pre_ref, gate_ref, mixo_ref, ho_ref, hn_ref, gbuf):
        pre = _nn(xn_ref[...], win_ref[...])
        pre_ref[...] = pre.astype(BF16)
        p = _gelu(pre)
        u, v = p[:, :cw], p[:, cw:]
        vn = _rms(v, sgu_ref[...]).astype(BF16)
        for n in range(tm // CHUNK):
            rows = slice(n * CHUNK, (n + 1) * CHUNK)
            for hd in range(heads):
                cols = slice(hd * CHUNK, (hd + 1) * CHUNK)
                gbuf[rows, cols] = _nn(ws_ref[hd], vn[rows, cols]) + b_ref[:, cols]
        gate = gbuf[...]
        gate_ref[...] = gate.astype(BF16)
        mixo = (u * gate).astype(BF16)
        mixo_ref[...] = mixo
        ho = h_ref[...] + _nn(mixo, wout_ref[...])
        ho_ref[...] = ho
        hn_ref[...] = _rms(ho, gffn_ref[...]).astype(BF16)

    row = lambda c: pl.BlockSpec((tm, c), lambda i: (i, 0))
    return _pcall(
        body, comm=comm, name="odd_fwd", grid=(s // tm,),
        in_specs=[row(d), row(d), _whole(win.shape), _whole(sgu.shape), _whole(ws.shape), _whole(bfull.shape),
                  _whole(wout.shape), _whole((1, d))],
        out_specs=[row(e), row(cw), row(cw), row(d), row(d)],
        out_shape=[jax.ShapeDtypeStruct((s, e), BF16), jax.ShapeDtypeStruct((s, cw), BF16), jax.ShapeDtypeStruct((s, cw), BF16),
                   jax.ShapeDtypeStruct((s, d), F32), jax.ShapeDtypeStruct((s, d), BF16)],
        scratch_shapes=[pltpu.VMEM((tm, cw), F32)],
        compiler_params=_params("arbitrary"),
    )(xn, h, win, sgu, ws, bfull, wout, gffn)


def _loss_bwd(h, gain, target, tm):
    s, d = h.shape

    def body(h_ref, gain_ref, t_ref, dh_ref, dhb_ref, dgain_ref, loss_ref):
        i = pl.program_id(0)
        hv = h_ref[...]
        gain = gain_ref[...]
        err = _rms(hv, gain) - t_ref[...]
        dy = err * (1.0 / d)
        dx, dgain = _rms_bwd(dy, hv, gain)
        dh_ref[...] = dx
        dhb_ref[...] = dx.astype(BF16)
        _acc(dgain_ref, dgain, i == 0)
        _acc(loss_ref, jnp.sum(err * err, axis=0, keepdims=True) * (0.5 / d), i == 0)

    row = pl.BlockSpec((tm, d), lambda i: (i, 0))
    return _pcall(
        body, name="loss_bwd", grid=(s // tm,),
        in_specs=[row, _whole((1, d)), row],
        out_specs=[row, row, _whole((1, d)), _whole((1, d))],
        out_shape=[jax.ShapeDtypeStruct((s, d), F32), jax.ShapeDtypeStruct((s, d), BF16), jax.ShapeDtypeStruct((1, d), F32),
                   jax.ShapeDtypeStruct((1, d), F32)],
        compiler_params=_params("arbitrary"),
    )(h, gain, target)


def _ffn_bwd1(dhb, g, gc, up, wd, cw, tm, tn, name, comm=None):
    s, d = dhb.shape
    f = g.shape[1]
    ni = s // tm

    def body(dh_ref, g_ref, gc_ref, up_ref, wd_ref, cw_ref, dg_ref, dup_ref, dcw_ref, dcb_ref, ebuf):
        i = pl.program_id(1)

        @pl.when(i == 0)
        def _():
            ebuf[tm:tm + HALO, :] = jnp.zeros((HALO, tn), F32)

        da = _nt(dh_ref[...], wd_ref[...])
        gcv = gc_ref[...].astype(F32)
        sg = jax.nn.sigmoid(gcv)
        dup_ref[...] = (da * gcv * sg).astype(BF16)
        dgc = da * up_ref[...].astype(F32) * (sg * (1.0 + gcv * (1.0 - sg)))
        ebuf[0:tm, :] = dgc
        s1 = ebuf[pl.ds(1, tm), :]
        s2 = ebuf[pl.ds(2, tm), :]
        dg_ref[...] = (cw_ref[2:3, :] * dgc + cw_ref[1:2, :] * s1 + cw_ref[0:1, :] * s2).astype(BF16)
        gv = g_ref[...].astype(F32)
        for k, shifted in enumerate((s2, s1, dgc)):
            _acc(dcw_ref.at[k:k + 1, :], jnp.sum(shifted * gv, axis=0, keepdims=True), i == 0)
        _acc(dcb_ref, jnp.sum(dgc, axis=0, keepdims=True), i == 0)
        ebuf[tm:tm + HALO, :] = ebuf[0:HALO, :]

    tile = pl.BlockSpec((tm, tn), lambda j, i: (ni - 1 - i, j))
    wcol = lambda r: pl.BlockSpec((r, tn), lambda j, i: (0, j))
    out = jax.ShapeDtypeStruct((s, f), BF16)
    return _pcall(
        body, comm=comm, name=name, grid=(f // tn, ni),
        in_specs=[pl.BlockSpec((tm, d), lambda j, i: (ni - 1 - i, 0)), tile, tile, tile,
                  pl.BlockSpec((tn, d), lambda j, i: (j, 0)), wcol(3)],
        out_specs=[tile, tile, wcol(3), wcol(1)],
        out_shape=[out, out, jax.ShapeDtypeStruct((3, f), F32), jax.ShapeDtypeStruct((1, f), F32)],
        scratch_shapes=[pltpu.VMEM((tm + HALO, tn), F32)],
        compiler_params=_params("arbitrary", "arbitrary"),
    )(dhb, g, gc, up, wd, cw)


def _ffn_bwd2(dg, dup, wg, wu, h, gain, dh, tm, name, comm=None):
    s, d = h.shape
    f = dg.shape[1]

    def body(dg_ref, dup_ref, wg_ref, wu_ref, h_ref, gain_ref, dh_ref, dho_ref, dhb_ref, dgain_ref):
        dhn = _nt(dg_ref[...], wg_ref[...]) + _nt(dup_ref[...], wu_ref[...])
        dx, dgain = _rms_bwd(dhn, h_ref[...], gain_ref[...])
        dho = dh_ref[...] + dx
        dho_ref[...] = dho
        dhb_ref[...] = dho.astype(BF16)
        _acc(dgain_ref, dgain, pl.program_id(0) == 0)

    row = lambda c: pl.BlockSpec((tm, c), lambda i: (i, 0))
    return _pcall(
        body, comm=comm, name=name, grid=(s // tm,),
        in_specs=[row(f), row(f), _whole(wg.shape), _whole(wu.shape), row(d), _whole((1, d)), row(d)],
        out_specs=[row(d), row(d), _whole((1, d))],
        out_shape=[jax.ShapeDtypeStruct((s, d), F32), jax.ShapeDtypeStruct((s, d), BF16), jax.ShapeDtypeStruct((1, d), F32)],
        compiler_params=_params("arbitrary"),
    )(dg, dup, wg, wu, h, gain, dh)


def _odd_bwd(dhb, dh, wout, pre, gate, ws, wst, sgu, win, h, gmix, tm, comm=None):
    s, d = h.shape
    e = win.shape[1]
    cw = e // 2
    heads = ws.shape[0]
    ni = s // tm

    def body(dhb_ref, dh_ref, wout_ref, pre_ref, gate_ref, ws_ref, wst_ref, sgu_ref, win_ref, h_ref, gmix_ref,
             dpre_ref, dho_ref, dhob_ref, dgain_ref, dsgu_ref, dws_ref, db_ref, vbuf, gacc):
        i = pl.program_id(0)
        first = i == 0
        dmixo = _nt(dhb_ref[...], wout_ref[...])
        pre = pre_ref[...].astype(F32)
        p = _gelu(pre)
        u, v = p[:, :cw], p[:, cw:]
        sgu = sgu_ref[...]
        rv = lax.rsqrt(jnp.mean(v * v, axis=-1, keepdims=True) + EPS)
        vh = v * rv
        vn = (vh * sgu).astype(BF16)
        du = dmixo * gate_ref[...].astype(F32)
        dgate = dmixo * u
        dgate_b = dgate.astype(BF16)
        gsum = dgate[0:CHUNK, :]
        for n in range(1, tm // CHUNK):
            gsum = gsum + dgate[n * CHUNK:(n + 1) * CHUNK, :]
        _acc(gacc, gsum, first)
        for hd in range(heads):
            cols = slice(hd * CHUNK, (hd + 1) * CHUNK)
            dws = None
            for n in range(tm // CHUNK):
                rows = slice(n * CHUNK, (n + 1) * CHUNK)
                vbuf[rows, cols] = _nn(wst_ref[hd], dgate_b[rows, cols])
                part = _nt(dgate_b[rows, cols], vn[rows, cols])
                dws = part if dws is None else dws + part
            _acc(dws_ref.at[hd], dws, first)
        dvn = vbuf[...]
        _acc(dsgu_ref, jnp.sum(dvn * vh, axis=0, keepdims=True), first)
        dvh = dvn * sgu
        dv = rv * (dvh - vh * jnp.mean(dvh * vh, axis=-1, keepdims=True))
        dpre = (jnp.concatenate([du, dv], axis=1) * _gelu_grad(pre)).astype(BF16)
        dpre_ref[...] = dpre
        dx, dgain = _rms_bwd(_nt(dpre, win_ref[...]), h_ref[...], gmix_ref[...])
        dho = dh_ref[...] + dx
        dho_ref[...] = dho
        dhob_ref[...] = dho.astype(BF16)
        _acc(dgain_ref, dgain, first)

        @pl.when(i == ni - 1)
        def _():
            ones = jnp.ones((8, CHUNK), F32)
            for hd in range(heads):
                tot = lax.dot_general(ones, gacc[:, hd * CHUNK:(hd + 1) * CHUNK], (((1,), (1,)), ((), ())),
                                      preferred_element_type=F32, precision=lax.Precision.HIGHEST)
                db_ref[hd:hd + 1, :] = tot[0:1, :]

    row = lambda c: pl.BlockSpec((tm, c), lambda i: (i, 0))
    return _pcall(
        body, comm=comm, name="odd_bwd", grid=(ni,),
        in_specs=[row(d), row(d), _whole(wout.shape), row(e), row(cw), _whole(ws.shape), _whole(wst.shape), _whole(sgu.shape),
                  _whole(win.shape), row(d), _whole((1, d))],
        out_specs=[row(e), row(d), row(d), _whole((1, d)), _whole((1, cw)), _whole(ws.shape), _whole((heads, CHUNK))],
        out_shape=[jax.ShapeDtypeStruct((s, e), BF16), jax.ShapeDtypeStruct((s, d), F32), jax.ShapeDtypeStruct((s, d), BF16),
                   jax.ShapeDtypeStruct((1, d), F32), jax.ShapeDtypeStruct((1, cw), F32), jax.ShapeDtypeStruct(ws.shape, F32),
                   jax.ShapeDtypeStruct((heads, CHUNK), F32)],
        scratch_shapes=[pltpu.VMEM((tm, cw), F32), pltpu.VMEM((CHUNK, cw), F32)],
        compiler_params=_params("arbitrary"),
    )(dhb, dh, wout, pre, gate, ws, wst, sgu, win, h, gmix)


def _even_bwd(dhb, dh, wout, proj, cq, pooled, conva, wpool, wpoolt, pscale, win, x, gmix, tm, comm=None):
    s, d = x.shape
    e = win.shape[1]
    aw = e // 4
    ni = s // tm

    def body(dhb_ref, dh_ref, wout_ref, proj_ref, cq_ref, pooled_ref, ca_ref, wp_ref, wpt_ref, ps_ref, win_ref, x_ref, gmix_ref,
             dproj_ref, dx_ref, dgain_ref, dca_ref, dwp_ref, dps_ref, cbuf, ebuf):
        i = pl.program_id(0)
        first = i == 0

        @pl.when(first)
        def _():
            cbuf[tm:tm + HALO, :] = jnp.zeros((HALO, aw), F32)
            ebuf[tm:tm + HALO, :] = jnp.zeros((HALO, aw), F32)

        dmix = _nt(dhb_ref[...], wout_ref[...])
        dy_a, dy_b = dmix[:, :aw], dmix[:, aw:]
        proj = proj_ref[...].astype(F32)
        a_b, a_c, a_v = (proj[:, k * aw:(k + 1) * aw] for k in range(3))
        da_b = dy_a * cq_ref[...].astype(F32)
        dcq = dy_a * a_b
        cbuf[0:tm, :] = dcq
        s1 = cbuf[pl.ds(1, tm), :]
        s2 = cbuf[pl.ds(2, tm), :]
        q = a_c * a_v
        for k, shifted in enumerate((s2, s1, dcq)):
            _acc(dca_ref.at[k:k + 1, :], jnp.sum(shifted * q, axis=0, keepdims=True), first)
        dq = ca_ref[2:3, :] * dcq + ca_ref[1:2, :] * s1 + ca_ref[0:1, :] * s2
        da_c = dq * a_v
        da_v = dq * a_c
        dps, dpool = [], []
        for g, w in enumerate(WINDOWS):
            cols = slice(g * LANE, (g + 1) * LANE)
            pooled = pooled_ref[:, cols]
            mixed = _nn(pooled, wp_ref[g])
            dps.append(jnp.sum(dy_b[:, cols] * mixed, axis=0, keepdims=True))
            dmixed = (dy_b[:, cols] * ps_ref[:, cols]).astype(BF16)
            _acc(dwp_ref.at[g], _tn(pooled, dmixed), first)
            dp = _nn(dmixed, wpt_ref[g])
            dpool.append(dp)
            ebuf[0:tm, cols] = dp / _counts((ni - 1 - i) * tm, tm, w)
        _acc(dps_ref, jnp.concatenate(dps, axis=1), first)
        dzs = []
        for g, w in enumerate(WINDOWS):
            cols = slice(g * LANE, (g + 1) * LANE)
            acc = ebuf[pl.ds(0, tm), cols]
            for m in range(1, w):
                acc = acc + ebuf[pl.ds(m, tm), cols]
            dzs.append(acc - dpool[g])
        dproj = jnp.concatenate([da_b, da_c, da_v] + dzs, axis=1).astype(BF16)
        dproj_ref[...] = dproj
        dx, dgain = _rms_bwd(_nt(dproj, win_ref[...]), x_ref[...], gmix_ref[...])
        dx_ref[...] = dh_ref[...] + dx
        _acc(dgain_ref, dgain, first)
        cbuf[tm:tm + HALO, :] = cbuf[0:HALO, :]
        ebuf[tm:tm + HALO, :] = ebuf[0:HALO, :]

    row = lambda c: pl.BlockSpec((tm, c), lambda i: (ni - 1 - i, 0))
    return _pcall(
        body, comm=comm, name="even_bwd", grid=(ni,),
        in_specs=[row(d), row(d), _whole(wout.shape), row(e), row(aw), row(aw), _whole(conva.shape), _whole(wpool.shape),
                  _whole(wpoolt.shape), _whole(pscale.shape), _whole(win.shape), row(d), _whole((1, d))],
        out_specs=[row(e), row(d), _whole((1, d)), _whole(conva.shape), _whole(wpool.shape), _whole(pscale.shape)],
        out_shape=[jax.ShapeDtypeStruct((s, e), BF16), jax.ShapeDtypeStruct((s, d), F32), jax.ShapeDtypeStruct((1, d), F32),
                   jax.ShapeDtypeStruct(conva.shape, F32), jax.ShapeDtypeStruct(wpool.shape, F32),
                   jax.ShapeDtypeStruct(pscale.shape, F32)],
        scratch_shapes=[pltpu.VMEM((tm + HALO, aw), F32), pltpu.VMEM((tm + HALO, aw), F32)],
        compiler_params=_params("arbitrary"),
    )(dhb, dh, wout, proj, cq, pooled, conva, wpool, wpoolt, pscale, win, x, gmix)


def _wgrad(a, b, tk, ts, name, comm=None):
    s, ka = a.shape
    nb = b.shape[1]
    nt = s // ts

    def body(a_ref, b_ref, o_ref, acc):
        t = pl.program_id(1)
        _acc(acc, _tn(a_ref[...], b_ref[...]), t == 0)

        @pl.when(t == nt - 1)
        def _():
            o_ref[...] = acc[...].astype(BF16)

    return _pcall(
        body, comm=comm, name=name, grid=(ka // tk, nt),
        in_specs=[pl.BlockSpec((ts, tk), lambda k, t: (t, k)), pl.BlockSpec((ts, nb), lambda k, t: (t, 0))],
        out_specs=pl.BlockSpec((tk, nb), lambda k, t: (k, 0)),
        out_shape=jax.ShapeDtypeStruct((ka, nb), BF16),
        scratch_shapes=[pltpu.VMEM((tk, nb), F32)],
        compiler_params=_params("arbitrary", "arbitrary"),
    )(a, b)


def _adamw(parts, w, m, v, tr, name, layer=None, into=None):
    r, c = w.shape[-2:]
    n, rp, cp = parts.shape
    tp = tr if rp == r else rp
    assert rp == r or tr == r

    def body(p_ref, w_ref, m_ref, v_ref, *rest):
        g_ref, d_ref, mo_ref, vo_ref = rest[-4:]
        g = p_ref[0, 0:tr, 0:c].astype(F32)
        for j in range(1, n):
            g = g + p_ref[j, 0:tr, 0:c].astype(F32)
        g_ref[...] = g
        mn = ADAM_B1 * m_ref[...] + (1.0 - ADAM_B1) * g
        vn = ADAM_B2 * v_ref[...] + (1.0 - ADAM_B2) * (g * g)
        mo_ref[...] = mn
        vo_ref[...] = vn
        m_hat = mn / (1.0 - ADAM_B1 ** ADAM_STEP)
        v_hat = vn / (1.0 - ADAM_B2 ** ADAM_STEP)
        d_ref[...] = -ADAM_LR * (m_hat / (jnp.sqrt(v_hat) + ADAM_EPS) + ADAM_WD * w_ref[...])

    if layer is None:
        row = pl.BlockSpec((tr, c), lambda i: (i, 0))
    else:
        row = pl.BlockSpec((None, tr, c), lambda i: (layer, i, 0))
    out = jax.ShapeDtypeStruct(w.shape, F32)
    prev = [] if into is None else list(into)
    return _pcall(
        body, name=name, grid=(r // tr,),
        in_specs=[pl.BlockSpec((n, tp, cp), lambda i: (0, i, 0)), row, row, row] + [pl.BlockSpec(memory_space=pl.ANY)] * len(prev),
        out_specs=[row, row, row, row], out_shape=[out, out, out, out],
        input_output_aliases={4 + k: k for k in range(len(prev))},
        compiler_params=_params("arbitrary"),
    )(parts, w, m, v, *prev)


def _pair_add(grad, other, spec, core, name):
    axis, width = spec
    slot = other.shape[1:]

    def body(core_ref, g_ref, o_ref, out_ref):
        out_ref[...] = (g_ref[...].astype(F32) + o_ref[...].astype(F32)).astype(BF16)

    if axis == 0:
        gspec = pl.BlockSpec(slot, lambda q, core_ref: (2 * q + core_ref[0], 0))
    else:
        gspec = pl.BlockSpec(slot, lambda q, core_ref: (0, 2 * q + core_ref[0]))
    per_chip = pl.BlockSpec((None,) + slot, lambda q, core_ref: (q, 0, 0))
    return _pcall(
        body, name=name,
        grid_spec=pltpu.PrefetchScalarGridSpec(num_scalar_prefetch=1, grid=(N_CHIP,), in_specs=[gspec, per_chip], out_specs=per_chip),
        out_shape=jax.ShapeDtypeStruct(other.shape, BF16),
        compiler_params=_params("arbitrary"),
    )(core, grad, other)


def _hbm():
    return pl.BlockSpec(memory_space=pltpu.HBM)


def _window(ref, spec, j):
    axis, width = spec
    start = pl.multiple_of(j * width, width)
    return ref.at[(slice(None),) * axis + (pl.ds(start, width),)]


def _here():
    return lax.axis_index("x"), lax.axis_index("y"), lax.axis_index("c")


class _Gather:
    def __init__(self, shards, specs, fulls):
        n = len(shards)
        self.ins, self.specs, self.out_shape = list(shards), list(specs), list(fulls)
        self.sems = [pltpu.SemaphoreType.DMA((7 * n,)), pltpu.SemaphoreType.DMA((7 * n,)), pltpu.SemaphoreType.DMA((n,))]

    def _plan(self, ins, outs, sems):
        send_sems, recv_sems, local_sems = sems
        x, y, c = _here()
        me, sibling = (x, y, c), (x, y, 1 - c)
        chips = [(1 - x, y), (x, 1 - y), (1 - x, 1 - y)]

        def slot(t, dev):
            return _window(outs[t], self.specs[t], 4 * dev[0] + 2 * dev[1] + dev[2])

        def copy(t, k, block, to, src=None):
            return pltpu.make_async_remote_copy(
                src_ref=slot(t, block) if src is None else src, dst_ref=slot(t, block),
                send_sem=send_sems.at[7 * t + k], recv_sem=recv_sems.at[7 * t + k], device_id=to, device_id_type=MESH)

        plan = []
        for t in range(len(ins)):
            plan.append(dict(
                mine=pltpu.make_async_copy(ins[t], slot(t, me), local_sems.at[t]),
                first=[copy(t, 0, me, sibling, src=ins[t])] + [copy(t, 1 + j, me, (*q, c), src=ins[t]) for j, q in enumerate(chips)],
                over_ici=[copy(t, 1 + j, (*q, c), me) for j, q in enumerate(chips)],
                passed=[copy(t, 4 + j, (*q, c), sibling) for j, q in enumerate(chips)],
                from_sibling=[copy(t, 0, sibling, me)] + [copy(t, 4 + j, (*q, 1 - c), me) for j, q in enumerate(chips)]))
        return plan

    def start(self, ins, outs, sems):
        for p in self._plan(ins, outs, sems):
            p["mine"].start()
            for cp in p["first"]:
                cp.start()

    def finish(self, ins, outs, sems):
        plan = self._plan(ins, outs, sems)
        for p in plan:
            for arrived, onward in zip(p["over_ici"], p["passed"]):
                arrived.wait_recv()
                onward.start()
        for p in plan:
            for cp in p["from_sibling"]:
                cp.wait_recv()
        for p in plan:
            for cp in p["first"] + p["passed"]:
                cp.wait_send()
            p["mine"].wait()


class _PairExchange:
    def __init__(self, grads, specs):
        n = len(grads)
        self.ins, self.specs = list(grads), list(specs)
        self.out_shape = [jax.ShapeDtypeStruct((N_CHIP,) + a.shape[:sp[0]] + (sp[1],) + a.shape[sp[0] + 1:], a.dtype)
                          for a, sp in zip(grads, specs)]
        self.sems = [pltpu.SemaphoreType.DMA((n,)), pltpu.SemaphoreType.DMA((n,))]

    def start(self, ins, outs, sems):
        send_sems, recv_sems = sems
        x, y, c = _here()
        for t in range(len(ins)):
            for q in range(N_CHIP):
                pltpu.make_async_remote_copy(
                    src_ref=_window(ins[t], self.specs[t], 2 * q + (1 - c)), dst_ref=outs[t].at[q],
                    send_sem=send_sems.at[t], recv_sem=recv_sems.at[t], device_id=(x, y, 1 - c), device_id_type=MESH).start()

    def finish(self, ins, outs, sems):
        send_sems, recv_sems = sems
        x, y, c = _here()
        for t in range(len(ins)):
            every = pltpu.make_async_remote_copy(src_ref=outs[t], dst_ref=outs[t], send_sem=send_sems.at[t],
                                                 recv_sem=recv_sems.at[t], device_id=(x, y, 1 - c), device_id_type=MESH)
            every.wait_send()
            every.wait_recv()


class _ChipExchange:
    def __init__(self, pairs, slotted=(), whole=()):
        self.ins = list(pairs) + list(slotted) + list(whole)
        self.npair, self.nslot = len(pairs), len(pairs) + len(slotted)
        n = len(self.ins)
        self.out_shape = ([jax.ShapeDtypeStruct(a.shape, a.dtype) for a in list(pairs) + list(slotted)]
                          + [jax.ShapeDtypeStruct((N_DEV,) + a.shape, a.dtype) for a in whole])
        self.sems = [pltpu.SemaphoreType.DMA((7 * n,)), pltpu.SemaphoreType.DMA((7 * n,)), pltpu.SemaphoreType.DMA((n,))]

    def _plan(self, ins, outs, sems):
        send_sems, recv_sems, local_sems = sems
        npair, nslot = self.npair, self.nslot
        x, y, c = _here()
        me, chip = 4 * x + 2 * y + c, 2 * x + y
        chips = [(1 - x, y), (x, 1 - y), (1 - x, 1 - y)]
        peers = [(x, y, 1 - c)] + [(*q, c) for q in chips] + [(*q, 1 - c) for q in chips]

        def index(dev):
            return 4 * dev[0] + 2 * dev[1] + dev[2]

        def copy(t, k, arriving):
            peer = peers[k]
            if t < npair:
                src, mine, theirs = ins[t].at[2 * peer[0] + peer[1]], chip, 2 * peer[0] + peer[1]
            else:
                src, mine, theirs = (ins[t].at[index(peer)] if t < nslot else ins[t]), me, index(peer)
            return pltpu.make_async_remote_copy(
                src_ref=src, dst_ref=outs[t].at[theirs if arriving else mine],
                send_sem=send_sems.at[7 * t + k], recv_sem=recv_sems.at[7 * t + k], device_id=peer, device_id_type=MESH)

        own, sent, arriving = [], [], []
        for t in range(len(ins)):
            fan = range(1, 4) if t < npair else range(7)
            if t < npair:
                own.append(pltpu.make_async_copy(ins[t].at[chip], outs[t].at[chip], local_sems.at[t]))
            else:
                own.append(pltpu.make_async_copy(ins[t].at[me] if t < nslot else ins[t], outs[t].at[me], local_sems.at[t]))
            sent += [copy(t, k, False) for k in fan]
            arriving += [copy(t, k, True) for k in fan]
        return own, sent, arriving

    def start(self, ins, outs, sems):
        own, sent, _ = self._plan(ins, outs, sems)
        for cp in own + sent:
            cp.start()

    def finish(self, ins, outs, sems):
        own, sent, arriving = self._plan(ins, outs, sems)
        for cp in arriving:
            cp.wait_recv()
        for cp in sent:
            cp.wait_send()
        for cp in own:
            cp.wait()


class _Jobs:
    def __init__(self, *jobs):
        self.jobs = jobs
        self.ins = [a for j in jobs for a in j.ins]
        self.out_shape = [a for j in jobs for a in j.out_shape]
        self.sems = [a for j in jobs for a in j.sems]

    def _split(self, ins, outs, sems):
        i = o = s = 0
        for j in self.jobs:
            yield j, ins[i:i + len(j.ins)], outs[o:o + len(j.out_shape)], sems[s:s + len(j.sems)]
            i, o, s = i + len(j.ins), o + len(j.out_shape), s + len(j.sems)

    def start(self, ins, outs, sems):
        for j, a, b, c in self._split(ins, outs, sems):
            j.start(a, b, c)

    def finish(self, ins, outs, sems):
        for j, a, b, c in self._split(ins, outs, sems):
            j.finish(a, b, c)

    def results(self, outs):
        return [b for _, _, b, _ in self._split((), outs, ())]


def _alone(job, name):
    return _pcall(lambda: None, comm=job, name=name, in_specs=[], out_specs=[], out_shape=[])()[1]


SMALL_ROWS = 24
REP_COLS = 1024
REP_PARTS = (("norm_mix", 0, 2), ("norm_ffn", 8, 2), ("final_norm", 16, 1), ("pool_scale", 24, 1), ("b_spatial", 32, 1),
             ("b_conv_ffn", 40, 6), ("w_pool", 48, 64), ("w_spatial", 112, 128), ("loss", 240, 1))
REP_ROWS = 248


def _pad_to(a, rows, cols):
    return jnp.pad(a, ((0, rows - a.shape[0]), (0, cols - a.shape[1])))


def _pack_small(conv_a, sgu_norm, conv_ffn, cols):
    return jnp.concatenate([_pad_to(conv_a, 8, cols), _pad_to(sgu_norm, 8, cols),
                            _pad_to(conv_ffn.reshape(-1, conv_ffn.shape[-1]), 8, cols)], axis=0)


def _unpack_small(p, ca_w, sg_w, cf_w):
    return p[0:3, 0:ca_w], p[8:9, 0:sg_w], p[16:22, 0:cf_w].reshape(2, 3, cf_w)


def _pack_rep(t, loss=None):
    bc = t["b_conv_ffn"]
    rows = {"norm_mix": t["norm_mix"], "norm_ffn": t["norm_ffn"], "final_norm": t["final_norm"].reshape(1, -1),
            "pool_scale": t["pool_scale"].reshape(1, -1), "b_spatial": t["b_spatial"].reshape(1, -1),
            "b_conv_ffn": _pad_to(bc, 2, 3 * REP_COLS).reshape(6, REP_COLS), "w_pool": t["w_pool"].reshape(-1, REP_COLS),
            "w_spatial": t["w_spatial"].reshape(-1, REP_COLS), "loss": jnp.zeros((1, REP_COLS), F32) if loss is None else loss}
    return jnp.concatenate([_pad_to(rows[nm], -(-cnt // 8) * 8, REP_COLS) for nm, _, cnt in REP_PARTS], axis=0)


def _unpack_rep(p, like):
    f = like["b_conv_ffn"].shape[1]
    at = {nm: p[r0:r0 + cnt] for nm, r0, cnt in REP_PARTS}
    return {
        "norm_mix": at["norm_mix"], "norm_ffn": at["norm_ffn"], "final_norm": at["final_norm"][0],
        "pool_scale": at["pool_scale"][:, 0:like["pool_scale"].shape[1]], "b_spatial": at["b_spatial"].reshape(like["b_spatial"].shape),
        "b_conv_ffn": at["b_conv_ffn"].reshape(2, 3 * REP_COLS)[:, 0:f], "w_pool": at["w_pool"].reshape(like["w_pool"].shape),
        "w_spatial": at["w_spatial"].reshape(like["w_spatial"].shape),
    }


def _pad_slots(a, width, padded):
    a = a.reshape(*a.shape[:-1], N_DEV, width)
    a = jnp.pad(a, ((0, 0),) * (a.ndim - 1) + ((0, padded - width),))
    return a.reshape(*a.shape[:-2], N_DEV * padded)


def _unpad_slots(a, width, padded):
    a = a.reshape(*a.shape[:-1], N_DEV, padded)[..., 0:width]
    return a.reshape(*a.shape[:-2], N_DEV * width)


def kernel(x, norm_mix, norm_ffn, final_norm, w_in_even, conv_a, w_pool, pool_scale, w_out_even, w_in_odd, sgu_norm, w_spatial, b_spatial, w_out_odd, w_ffn_gate, w_ffn_up, conv_ffn, b_conv_ffn, w_ffn_down, loss_target, m_norm_mix, m_norm_ffn, m_final_norm, m_w_in_even, m_conv_a, m_w_pool, m_pool_scale, m_w_out_even, m_w_in_odd, m_sgu_norm, m_w_spatial, m_b_spatial, m_w_out_odd, m_w_ffn_gate, m_w_ffn_up, m_conv_ffn, m_b_conv_ffn, m_w_ffn_down, v_norm_mix, v_norm_ffn, v_final_norm, v_w_in_even, v_conv_a, v_w_pool, v_pool_scale, v_w_out_even, v_w_in_odd, v_sgu_norm, v_w_spatial, v_b_spatial, v_w_out_odd, v_w_ffn_gate, v_w_ffn_up, v_conv_ffn, v_b_conv_ffn, v_w_ffn_down):
    s, d = x.shape[1], x.shape[2]
    x2, target = x[0], loss_target[0]
    tm = min(512, s)
    tm_wide = min(2048, s)
    tn = 256
    row = lambda a: a.reshape(1, -1)
    ein, ro = w_in_even.shape[2], w_out_even.shape[1]
    fs = w_ffn_gate.shape[2]
    fsp = -(-fs // LANE) * LANE
    fp = N_DEV * fsp
    full = lambda shape, dtype=BF16: jax.ShapeDtypeStruct(shape, dtype)

    wg_s = [jnp.pad(w_ffn_gate[l], ((0, 0), (0, fsp - fs))).astype(BF16) for l in range(2)]
    wu_s = [jnp.pad(w_ffn_up[l], ((0, 0), (0, fsp - fs))).astype(BF16) for l in range(2)]
    wd_s = [jnp.pad(w_ffn_down[l], ((0, fsp - fs), (0, 0))).astype(BF16) for l in range(2)]
    small_s = _pack_small(conv_a[0], sgu_norm, conv_ffn, fsp)[None]
    in_spec, out_spec, gu_spec, down_spec = (1, ein), (0, ro), (1, fsp), (0, fsp)
    full_in, full_out, full_gu, full_down = full((d, N_DEV * ein)), full((N_DEV * ro, d)), full((d, fp)), full((fp, d))
    win_e, wout_e, gsmall = _alone(_Gather([w_in_even[0].astype(BF16), w_out_even[0].astype(BF16), small_s], [in_spec, out_spec, (0, 1)],
                                           [full_in, full_out, full((N_DEV, SMALL_ROWS, fsp), F32)]), "gather_mix0")
    ca_full = jnp.moveaxis(gsmall[:, 0:3, 0:conv_a.shape[2]], 0, 1).reshape(3, -1)
    sgu_full = gsmall[:, 8, 0:sgu_norm.shape[1]].reshape(1, -1)
    cf_full = jnp.moveaxis(gsmall[:, 16:22, :].reshape(N_DEV, 2, 3, fsp), 0, 2).reshape(2, 3, fp)
    cb_full = _pad_slots(b_conv_ffn, fs, fsp)

    tril = jnp.tril(jnp.ones((CHUNK, CHUNK), F32))
    ws_m = w_spatial[0] * tril
    ws_b = ws_m.astype(BF16)
    wst_b = jnp.swapaxes(ws_m, 1, 2).astype(BF16)
    bfull = jnp.repeat(b_spatial[0].T, CHUNK, axis=1)
    wpool_b = w_pool[0].astype(BF16)
    wpoolt_b = jnp.swapaxes(w_pool[0], 1, 2).astype(BF16)

    (xn0, proj0, cq0, pooled0, mix0, h1, hn0), (wg0, wu0) = _even_fwd(
        x2, norm_mix[0:1], norm_ffn[0:1], win_e, ca_full, wpool_b, pool_scale, wout_e, tm,
        comm=_Gather([wg_s[0], wu_s[0]], [gu_spec, gu_spec], [full_gu, full_gu]))
    (g0, gc0, up0, a0), (wd0, win_o, wout_o) = _ffn_fwd1(
        hn0, wg0, wu0, cf_full[0], cb_full[0:1], tm_wide, tn, "ffn_fwd1_l0",
        comm=_Gather([wd_s[0], w_in_odd[0].astype(BF16), w_out_odd[0].astype(BF16)], [down_spec, in_spec, out_spec], [full_down, full_in, full_out]))
    (h2, xn1), (wg1,) = _ffn_fwd2(a0, wd0, h1, norm_mix[1:2], tm, "ffn_fwd2_l0", comm=_Gather([wg_s[1]], [gu_spec], [full_gu]))
    (pre1, gate1, mixo1, h3, hn1), (wu1,) = _odd_fwd(xn1, h2, win_o, sgu_full, ws_b, bfull, wout_o, norm_ffn[1:2], tm,
                                                    comm=_Gather([wu_s[1]], [gu_spec], [full_gu]))
    (g1, gc1, up1, a1), (wd1,) = _ffn_fwd1(hn1, wg1, wu1, cf_full[1], cb_full[1:2], tm_wide, tn, "ffn_fwd1_l1",
                                          comm=_Gather([wd_s[1]], [down_spec], [full_down]))
    h4, _ = _ffn_fwd2(a1, wd1, h3, row(final_norm), tm, "ffn_fwd2_l1")

    core = lax.axis_index("c").astype(jnp.int32).reshape(1)
    ts = min(1024, s)
    dh4, dh4b, d_final, lossvec = _loss_bwd(h4, row(final_norm), target, tm)
    gd1 = _wgrad(a1, dh4b, tn, s, "wgrad_down_l1")
    (dg1, dup1, dcw1, dcb1), (o_d1,) = _ffn_bwd1(dh4b, g1, gc1, up1, wd1, cf_full[1], tm_wide, tn, "ffn_bwd1_l1",
                                                 comm=_PairExchange([gd1], [down_spec]))
    p_d1 = _pair_add(gd1, o_d1, down_spec, core, "pair_add_down_l1")
    gg1 = _wgrad(hn1, dg1, 512, ts, "wgrad_gate_l1")
    gu1, (o_g1,) = _wgrad(hn1, dup1, 512, ts, "wgrad_up_l1", comm=_PairExchange([gg1], [gu_spec]))
    p_g1 = _pair_add(gg1, o_g1, gu_spec, core, "pair_add_gate_l1")
    jobs = _Jobs(_ChipExchange([p_d1]), _PairExchange([gu1], [gu_spec]))
    (dh3, dh3b, d_nffn1), res = _ffn_bwd2(dg1, dup1, wg1, wu1, h3, norm_ffn[1:2], dh4, tm // 2, "ffn_bwd2_l1", comm=jobs)
    (s_d1,), (o_u1,) = jobs.results(res)
    p_u1 = _pair_add(gu1, o_u1, gu_spec, core, "pair_add_up_l1")
    (dpre1, dh2, dh2b, d_nmix1, d_sgu, d_ws, d_b), (s_g1, s_u1) = _odd_bwd(
        dh3b, dh3, wout_o, pre1, gate1, ws_b, wst_b, sgu_full, win_o, h2, norm_mix[1:2], tm, comm=_ChipExchange([p_g1, p_u1]))
    gi1 = _wgrad(xn1, dpre1, 512, ts, "wgrad_in_odd")
    go1, (o_i1,) = _wgrad(mixo1, dh3b, 512, ts, "wgrad_out_odd", comm=_PairExchange([gi1], [in_spec]))
    p_i1 = _pair_add(gi1, o_i1, in_spec, core, "pair_add_in_odd")
    gd0, (o_o1,) = _wgrad(a0, dh2b, tn, s, "wgrad_down_l0", comm=_PairExchange([go1], [out_spec]))
    p_o1 = _pair_add(go1, o_o1, out_spec, core, "pair_add_out_odd")
    jobs = _Jobs(_ChipExchange([p_i1, p_o1]), _PairExchange([gd0], [down_spec]))
    (dg0, dup0, dcw0, dcb0), res = _ffn_bwd1(dh2b, g0, gc0, up0, wd0, cf_full[0], tm_wide, tn, "ffn_bwd1_l0", comm=jobs)
    (s_i1, s_o1), (o_d0,) = jobs.results(res)
    p_d0 = _pair_add(gd0, o_d0, down_spec, core, "pair_add_down_l0")
    gg0, (s_d0,) = _wgrad(hn0, dg0, 512, ts, "wgrad_gate_l0", comm=_ChipExchange([p_d0]))
    gu0, (o_g0,) = _wgrad(hn0, dup0, 512, ts, "wgrad_up_l0", comm=_PairExchange([gg0], [gu_spec]))
    p_g0 = _pair_add(gg0, o_g0, gu_spec, core, "pair_add_gate_l0")
    jobs = _Jobs(_ChipExchange([p_g0]), _PairExchange([gu0], [gu_spec]))
    (dh1, dh1b, d_nffn0), res = _ffn_bwd2(dg0, dup0, wg0, wu0, h1, norm_ffn[0:1], dh2, tm // 2, "ffn_bwd2_l0", comm=jobs)
    (s_g0,), (o_u0,) = jobs.results(res)
    p_u0 = _pair_add(gu0, o_u0, gu_spec, core, "pair_add_up_l0")
    (dproj0, grad_x, d_nmix0, d_ca, d_wp, d_ps), (s_u0,) = _even_bwd(
        dh1b, dh1, wout_e, proj0, cq0, pooled0, ca_full, wpool_b, wpoolt_b, pool_scale, win_e, x2, norm_mix[0:1], tm, comm=_ChipExchange([p_u0]))
    go0 = _wgrad(mix0, dh1b, 512, ts, "wgrad_out_even")
    gi0, (o_o0,) = _wgrad(xn0, dproj0, 512, ts, "wgrad_in_even", comm=_PairExchange([go0], [out_spec]))
    p_o0 = _pair_add(go0, o_o0, out_spec, core, "pair_add_out_even")
    (o_i0,) = _alone(_PairExchange([gi0], [in_spec]), "pair_exchange_in_even")
    p_i0 = _pair_add(gi0, o_i0, in_spec, core, "pair_add_in_even")
    d_small = jnp.stack([_pack_small(a, b, c, fsp) for a, b, c in zip(
        jnp.moveaxis(d_ca.reshape(3, N_DEV, -1), 1, 0), jnp.moveaxis(d_sgu.reshape(1, N_DEV, -1), 1, 0),
        jnp.moveaxis(jnp.stack([dcw0, dcw1]).reshape(2, 3, N_DEV, fsp), 2, 0))])
    rep = {"norm_mix": jnp.concatenate([d_nmix0, d_nmix1]), "norm_ffn": jnp.concatenate([d_nffn0, d_nffn1]), "final_norm": d_final[0],
           "pool_scale": d_ps, "b_spatial": d_b, "b_conv_ffn": _unpad_slots(jnp.concatenate([dcb0, dcb1]), fs, fsp), "w_pool": d_wp,
           "w_spatial": d_ws * tril}
    s_i0, s_o0, r_small, r_rep = _alone(_ChipExchange([p_i0, p_o0], [d_small], [_pack_rep(rep, lossvec)]), "chip_exchange_last")
    loss = jnp.sum(r_rep[:, REP_PARTS[-1][1], :])

    out = {}
    out["w_in_even"] = _adamw(s_i0, w_in_even[0], m_w_in_even[0], v_w_in_even[0], 256, "adamw_in_even")
    out["w_out_even"] = _adamw(s_o0, w_out_even[0], m_w_out_even[0], v_w_out_even[0], ro, "adamw_out_even")
    out["w_in_odd"] = _adamw(s_i1, w_in_odd[0], m_w_in_odd[0], v_w_in_odd[0], 256, "adamw_in_odd")
    out["w_out_odd"] = _adamw(s_o1, w_out_odd[0], m_w_out_odd[0], v_w_out_odd[0], ro, "adamw_out_odd")
    for nm, s1, s0, tr, w, m, v in (("w_ffn_gate", s_g1, s_g0, 256, w_ffn_gate, m_w_ffn_gate, v_w_ffn_gate),
                                    ("w_ffn_up", s_u1, s_u0, 256, w_ffn_up, m_w_ffn_up, v_w_ffn_up),
                                    ("w_ffn_down", s_d1, s_d0, fs, w_ffn_down, m_w_ffn_down, v_w_ffn_down)):
        l1 = _adamw(s1, w, m, v, tr, "adamw_%s_l1" % nm, layer=1)
        out[nm] = _adamw(s0, w, m, v, tr, "adamw_%s_l0" % nm, layer=0, into=l1)
    small = _adamw(r_small, _pack_small(conv_a[0], sgu_norm, conv_ffn, fsp), _pack_small(m_conv_a[0], m_sgu_norm, m_conv_ffn, fsp),
                   _pack_small(v_conv_a[0], v_sgu_norm, v_conv_ffn, fsp), SMALL_ROWS, "adamw_small")
    rep_w = {"norm_mix": norm_mix, "norm_ffn": norm_ffn, "final_norm": final_norm, "pool_scale": pool_scale, "b_spatial": b_spatial,
             "b_conv_ffn": b_conv_ffn, "w_pool": w_pool, "w_spatial": w_spatial}
    rep_m = {"norm_mix": m_norm_mix, "norm_ffn": m_norm_ffn, "final_norm": m_final_norm, "pool_scale": m_pool_scale,
             "b_spatial": m_b_spatial, "b_conv_ffn": m_b_conv_ffn, "w_pool": m_w_pool, "w_spatial": m_w_spatial}
    rep_v = {"norm_mix": v_norm_mix, "norm_ffn": v_norm_ffn, "final_norm": v_final_norm, "pool_scale": v_pool_scale,
             "b_spatial": v_b_spatial, "b_conv_ffn": v_b_conv_ffn, "w_pool": v_w_pool, "w_spatial": v_w_spatial}
    reps = _adamw(r_rep, _pack_rep(rep_w), _pack_rep(rep_m), _pack_rep(rep_v), REP_ROWS, "adamw_replicated")

    names = ["norm_mix", "norm_ffn", "final_norm", "w_in_even", "conv_a", "w_pool", "pool_scale", "w_out_even", "w_in_odd", "sgu_norm",
             "w_spatial", "b_spatial", "w_out_odd", "w_ffn_gate", "w_ffn_up", "conv_ffn", "b_conv_ffn", "w_ffn_down"]
    like = {"norm_mix": norm_mix, "norm_ffn": norm_ffn, "final_norm": final_norm, "w_in_even": w_in_even, "conv_a": conv_a,
            "w_pool": w_pool, "pool_scale": pool_scale, "w_out_even": w_out_even, "w_in_odd": w_in_odd, "sgu_norm": sgu_norm,
            "w_spatial": w_spatial, "b_spatial": b_spatial, "w_out_odd": w_out_odd, "w_ffn_gate": w_ffn_gate, "w_ffn_up": w_ffn_up,
            "conv_ffn": conv_ffn, "b_conv_ffn": b_conv_ffn, "w_ffn_down": w_ffn_down}
    groups = []
    for k in range(4):
        ca_k, sg_k, cf_k = _unpack_small(small[k], conv_a.shape[2], sgu_norm.shape[1], conv_ffn.shape[2])
        vals = dict(_unpack_rep(reps[k], like))
        vals.update(conv_a=ca_k, sgu_norm=sg_k, conv_ffn=cf_k)
        for nm in ("w_in_even", "w_in_odd", "w_out_even", "w_out_odd", "w_ffn_gate", "w_ffn_up", "w_ffn_down"):
            vals[nm] = out[nm][k]
        groups.append([vals[nm].reshape(like[nm].shape) for nm in names])
    return (loss, grad_x[None], *groups[0], *groups[1], *groups[2], *groups[3])
```

```python
import functools

import jax
import jax.numpy as jnp
from jax import lax
from jax.experimental import pallas as pl
from jax.experimental.pallas import tpu as pltpu

F32, BF16 = jnp.float32, jnp.bfloat16
EPS = 1e-6
WINDOWS = (2, 4, 8, 16)
HALO = 16
CHUNK = 128
N_DEV = 8
N_CHIP = 4
MESH = pl.DeviceIdType.MESH
VMEM_LIMIT = 56 * 2**20
LANE = 128
ADAM_LR, ADAM_B1, ADAM_B2, ADAM_EPS, ADAM_WD, ADAM_STEP = 0.001, 0.9, 0.999, 1e-08, 0.01, 10
INV_SQRT2 = 0.7071067811865476
INV_SQRT2PI = 0.3989422804014327


def _pcall(body, comm=None, **kw):
    if comm is None:
        return pl.pallas_call(body, **kw)
    in_specs, out_specs, out_shape = list(kw.pop("in_specs")), kw.pop("out_specs"), kw.pop("out_shape")
    single = not isinstance(out_shape, (list, tuple))
    out_specs, out_shape = ([out_specs], [out_shape]) if single else (list(out_specs), list(out_shape))
    scratch = list(kw.pop("scratch_shapes", []))
    grid = kw.get("grid", ())
    n_in, n_out, n_scr, c_in, c_out = len(in_specs), len(out_specs), len(scratch), len(comm.ins), len(comm.out_shape)

    def hosted(*refs):
        cuts = [0, n_in, n_in + c_in, n_in + c_in + n_out, n_in + c_in + n_out + c_out, n_in + c_in + n_out + c_out + n_scr, len(refs)]
        ins, cins, outs, couts, scr, sems = (refs[a:b] for a, b in zip(cuts[:-1], cuts[1:]))
        first, last = True, True
        for axis, size in enumerate(grid):
            first = jnp.logical_and(first, pl.program_id(axis) == 0)
            last = jnp.logical_and(last, pl.program_id(axis) == size - 1)
        if grid:
            pl.when(first)(lambda: comm.start(cins, couts, sems))
            body(*ins, *outs, *scr)
            pl.when(last)(lambda: comm.finish(cins, couts, sems))
        else:
            comm.start(cins, couts, sems)
            body(*ins, *outs, *scr)
            comm.finish(cins, couts, sems)

    call = pl.pallas_call(hosted, in_specs=in_specs + [_hbm()] * c_in, out_specs=out_specs + [_hbm()] * c_out,
                          out_shape=out_shape + list(comm.out_shape), scratch_shapes=scratch + list(comm.sems), **kw)

    def run(*args):
        res = call(*args, *comm.ins)
        own = res[0] if single else res[:n_out]
        return own, res[n_out:]

    return run


def _params(*sem):
    return pltpu.CompilerParams(dimension_semantics=sem, vmem_limit_bytes=VMEM_LIMIT)


def _whole(shape):
    return pl.BlockSpec(shape, lambda *_: (0,) * len(shape))


def _nn(a, b):
    return jnp.dot(a, b, preferred_element_type=F32)


def _nt(a, b):
    return lax.dot_general(a, b, (((1,), (1,)), ((), ())), preferred_element_type=F32)


def _tn(a, b):
    return lax.dot_general(a, b, (((0,), (0,)), ((), ())), preferred_element_type=F32)


def _rms(x, gain):
    r = lax.rsqrt(jnp.mean(x * x, axis=-1, keepdims=True) + EPS)
    return x * r * gain


def _rms_bwd(dy, x, gain):
    r = lax.rsqrt(jnp.mean(x * x, axis=-1, keepdims=True) + EPS)
    xh = x * r
    dgain = jnp.sum(dy * xh, axis=0, keepdims=True)
    dxh = dy * gain
    dx = r * (dxh - xh * jnp.mean(dxh * xh, axis=-1, keepdims=True))
    return dx, dgain


def _gelu(x):
    return 0.5 * x * (1.0 + lax.erf(x * INV_SQRT2))


def _gelu_grad(x):
    return 0.5 * (1.0 + lax.erf(x * INV_SQRT2)) + x * jnp.exp(-0.5 * x * x) * INV_SQRT2PI


def _acc(ref, val, first):
    @pl.when(first)
    def _():
        ref[...] = val

    @pl.when(jnp.logical_not(first))
    def _():
        ref[...] += val


def _counts(row0, tm, w):
    pos1 = (row0 + lax.broadcasted_iota(jnp.int32, (tm, 1), 0) + 1).astype(F32)
    return jnp.minimum(pos1, float(w))


def _even_fwd(x, gmix, gffn, win, conva, wpool, pscale, wout, tm, comm=None):
    s, d = x.shape
    e = win.shape[1]
    aw = e // 4

    def body(x_ref, gmix_ref, gffn_ref, win_ref, ca_ref, wp_ref, ps_ref, wout_ref,
             xn_ref, proj_ref, cq_ref, pooled_ref, mix_ref, h_ref, hn_ref, qbuf, zbuf):
        i = pl.program_id(0)

        @pl.when(i == 0)
        def _():
            qbuf[0:HALO, :] = jnp.zeros((HALO, aw), F32)
            zbuf[0:HALO, :] = jnp.zeros((HALO, aw), F32)

        xv = x_ref[...]
        xn = _rms(xv, gmix_ref[...]).astype(BF16)
        xn_ref[...] = xn
        proj = _nn(xn, win_ref[...])
        proj_ref[...] = proj.astype(BF16)
        a_b, a_c, a_v, z = (proj[:, k * aw:(k + 1) * aw] for k in range(4))
        q = a_c * a_v
        qbuf[HALO:HALO + tm, :] = q
        cq = ca_ref[2:3, :] * q + ca_ref[1:2, :] * qbuf[pl.ds(HALO - 1, tm), :] + ca_ref[0:1, :] * qbuf[pl.ds(HALO - 2, tm), :]
        cq_ref[...] = cq.astype(BF16)
        y_a = a_b * cq
        zbuf[HALO:HALO + tm, :] = z
        ys = []
        for g, w in enumerate(WINDOWS):
            cols = slice(g * LANE, (g + 1) * LANE)
            acc = z[:, cols]
            for m in range(1, w):
                acc = acc + zbuf[pl.ds(HALO - m, tm), cols]
            pooled = (acc / _counts(i * tm, tm, w) - z[:, cols]).astype(BF16)
            pooled_ref[:, cols] = pooled
            ys.append(_nn(pooled, wp_ref[g]))
        y_b = jnp.concatenate(ys, axis=1) * ps_ref[...]
        mix = jnp.concatenate([y_a, y_b], axis=1).astype(BF16)
        mix_ref[...] = mix
        h = xv + _nn(mix, wout_ref[...])
        h_ref[...] = h
        hn_ref[...] = _rms(h, gffn_ref[...]).astype(BF16)
        qbuf[0:HALO, :] = qbuf[tm:tm + HALO, :]
        zbuf[0:HALO, :] = zbuf[tm:tm + HALO, :]

    row = lambda c: pl.BlockSpec((tm, c), lambda i: (i, 0))
    return _pcall(
        body, comm=comm, name="even_fwd", grid=(s // tm,),
        in_specs=[row(d), _whole((1, d)), _whole((1, d)), _whole(win.shape), _whole(conva.shape), _whole(wpool.shape),
                  _whole(pscale.shape), _whole(wout.shape)],
        out_specs=[row(d), row(e), row(aw), row(aw), row(d), row(d), row(d)],
        out_shape=[jax.ShapeDtypeStruct((s, d), BF16), jax.ShapeDtypeStruct((s, e), BF16), jax.ShapeDtypeStruct((s, aw), BF16),
                   jax.ShapeDtypeStruct((s, aw), BF16), jax.ShapeDtypeStruct((s, d), BF16), jax.ShapeDtypeStruct((s, d), F32),
                   jax.ShapeDtypeStruct((s, d), BF16)],
        scratch_shapes=[pltpu.VMEM((tm + HALO, aw), F32), pltpu.VMEM((tm + HALO, aw), F32)],
        compiler_params=_params("arbitrary"),
    )(x, gmix, gffn, win, conva, wpool, pscale, wout)


def _ffn_gate(hn, wg, cw, cb, tm, tn, name, comm=None):
    s, d = hn.shape
    f = wg.shape[1]

    def body(hn_ref, wg_ref, cw_ref, cb_ref, g_ref, gc_ref, gbuf):
        i = pl.program_id(1)

        @pl.when(i == 0)
        def _():
            gbuf[0:HALO, :] = jnp.zeros((HALO, tn), F32)

        g = _nn(hn_ref[...], wg_ref[...])
        g_ref[...] = g.astype(BF16)
        gbuf[HALO:HALO + tm, :] = g
        gc = (cw_ref[2:3, :] * g + cw_ref[1:2, :] * gbuf[pl.ds(HALO - 1, tm), :]
              + cw_ref[0:1, :] * gbuf[pl.ds(HALO - 2, tm), :] + cb_ref[...])
        gc_ref[...] = gc.astype(BF16)
        gbuf[0:HALO, :] = gbuf[tm:tm + HALO, :]

    tile = pl.BlockSpec((tm, tn), lambda j, i: (i, j))
    wcol = lambda r: pl.BlockSpec((r, tn), lambda j, i: (0, j))
    out = jax.ShapeDtypeStruct((s, f), BF16)
    return _pcall(
        body, comm=comm, name=name, grid=(f // tn, s // tm),
        in_specs=[pl.BlockSpec((tm, d), lambda j, i: (i, 0)), wcol(d), wcol(3), wcol(1)],
        out_specs=[tile, tile], out_shape=[out, out],
        scratch_shapes=[pltpu.VMEM((tm + HALO, tn), F32)],
        compiler_params=_params("arbitrary", "arbitrary"),
    )(hn, wg, cw, cb)


def _ffn_up(hn, wu, gc, tm, tn, name, comm=None):
    s, d = hn.shape
    f = wu.shape[1]

    def body(hn_ref, wu_ref, gc_ref, up_ref, a_ref):
        up = _nn(hn_ref[...], wu_ref[...])
        up_ref[...] = up.astype(BF16)
        gc = gc_ref[...].astype(F32)
        a_ref[...] = (gc * jax.nn.sigmoid(gc) * up).astype(BF16)

    tile = pl.BlockSpec((tm, tn), lambda j, i: (i, j))
    out = jax.ShapeDtypeStruct((s, f), BF16)
    return _pcall(
        body, comm=comm, name=name, grid=(f // tn, s // tm),
        in_specs=[pl.BlockSpec((tm, d), lambda j, i: (i, 0)), pl.BlockSpec((d, tn), lambda j, i: (0, j)), tile],
        out_specs=[tile, tile], out_shape=[out, out],
        compiler_params=_params("arbitrary", "arbitrary"),
    )(hn, wu, gc)


def _ffn_fwd2(a, wd, h, gain, tm, name, comm=None):
    s, d = h.shape
    f = a.shape[1]

    def body(a_ref, wd_ref, h_ref, gain_ref, ho_ref, hn_ref):
        ho = h_ref[...] + _nn(a_ref[...], wd_ref[...])
        ho_ref[...] = ho
        hn_ref[...] = _rms(ho, gain_ref[...]).astype(BF16)

    row = lambda c: pl.BlockSpec((tm, c), lambda i: (i, 0))
    return _pcall(
        body, comm=comm, name=name, grid=(s // tm,),
        in_specs=[row(f), _whole(wd.shape), row(d), _whole((1, d))],
        out_specs=[row(d), row(d)],
        out_shape=[jax.ShapeDtypeStruct((s, d), F32), jax.ShapeDtypeStruct((s, d), BF16)],
        compiler_params=_params("arbitrary"),
    )(a, wd, h, gain)


def _odd_fwd(xn, h, win, sgu, ws, bfull, wout, gffn, tm, comm=None):
    s, d = h.shape
    e = win.shape[1]
    cw = e // 2
    heads = ws.shape[0]

    def body(xn_ref, h_ref, win_ref, sgu_ref, ws_ref, b_ref, wout_ref, gffn_ref,
             pre_ref, gate_ref, mixo_ref, ho_ref, hn_ref, gbuf):
        pre = _nn(xn_ref[...], win_ref[...])
        pre_ref[...] = pre.astype(BF16)
        p = _gelu(pre)
        u, v = p[:, :cw], p[:, cw:]
        vn = _rms(v, sgu_ref[...]).astype(BF16)
        for n in range(tm // CHUNK):
            rows = slice(n * CHUNK, (n + 1) * CHUNK)
            for hd in range(heads):
                cols = slice(hd * CHUNK, (hd + 1) * CHUNK)
                gbuf[rows, cols] = _nn(ws_ref[hd], vn[rows, cols]) + b_ref[:, cols]
        gate = gbuf[...]
        gate_ref[...] = gate.astype(BF16)
        mixo = (u * gate).astype(BF16)
        mixo_ref[...] = mixo
        ho = h_ref[...] + _nn(mixo, wout_ref[...])
        ho_ref[...] = ho
        hn_ref[...] = _rms(ho, gffn_ref[...]).astype(BF16)

    row = lambda c: pl.BlockSpec((tm, c), lambda i: (i, 0))
    return _pcall(
        body, comm=comm, name="odd_fwd", grid=(s // tm,),
        in_specs=[row(d), row(d), _whole(win.shape), _whole(sgu.shape), _whole(ws.shape), _whole(bfull.shape),
                  _whole(wout.shape), _whole((1, d))],
        out_specs=[row(e), row(cw), row(cw), row(d), row(d)],
        out_shape=[jax.ShapeDtypeStruct((s, e), BF16), jax.ShapeDtypeStruct((s, cw), BF16), jax.ShapeDtypeStruct((s, cw), BF16),
                   jax.ShapeDtypeStruct((s, d), F32), jax.ShapeDtypeStruct((s, d), BF16)],
        scratch_shapes=[pltpu.VMEM((tm, cw), F32)],
        compiler_params=_params("arbitrary"),
    )(xn, h, win, sgu, ws, bfull, wout, gffn)


def _loss_bwd(h, gain, target, tm):
    s, d = h.shape

    def body(h_ref, gain_ref, t_ref, dh_ref, dhb_ref, dgain_ref, loss_ref):
        i = pl.program_id(0)
        hv = h_ref[...]
        gain = gain_ref[...]
        err = _rms(hv, gain) - t_ref[...]
        dy = err * (1.0 / d)
        dx, dgain = _rms_bwd(dy, hv, gain)
        dh_ref[...] = dx
        dhb_ref[...] = dx.astype(BF16)
        _acc(dgain_ref, dgain, i == 0)
        _acc(loss_ref, jnp.sum(err * err, axis=0, keepdims=True) * (0.5 / d), i == 0)

    row = pl.BlockSpec((tm, d), lambda i: (i, 0))
    return _pcall(
        body, name="loss_bwd", grid=(s // tm,),
        in_specs=[row, _whole((1, d)), row],
        out_specs=[row, row, _whole((1, d)), _whole((1, d))],
        out_shape=[jax.ShapeDtypeStruct((s, d), F32), jax.ShapeDtypeStruct((s, d), BF16), jax.ShapeDtypeStruct((1, d), F32),
                   jax.ShapeDtypeStruct((1, d), F32)],
        compiler_params=_params("arbitrary"),
    )(h, gain, target)


def _ffn_bwd1(dhb, g, gc, up, wd, cw, tm, tn, name, comm=None):
    s, d = dhb.shape
    f = g.shape[1]
    ni = s // tm

    def body(dh_ref, g_ref, gc_ref, up_ref, wd_ref, cw_ref, dg_ref, dup_ref, dcw_ref, dcb_ref, ebuf):
        i = pl.program_id(1)

        @pl.when(i == 0)
        def _():
            ebuf[tm:tm + HALO, :] = jnp.zeros((HALO, tn), F32)

        da = _nt(dh_ref[...], wd_ref[...])
        gcv = gc_ref[...].astype(F32)
        sg = jax.nn.sigmoid(gcv)
        dup_ref[...] = (da * gcv * sg).astype(BF16)
        dgc = da * up_ref[...].astype(F32) * (sg * (1.0 + gcv * (1.0 - sg)))
        ebuf[0:tm, :] = dgc
        s1 = ebuf[pl.ds(1, tm), :]
        s2 = ebuf[pl.ds(2, tm), :]
        dg_ref[...] = (cw_ref[2:3, :] * dgc + cw_ref[1:2, :] * s1 + cw_ref[0:1, :] * s2).astype(BF16)
        gv = g_ref[...].astype(F32)
        for k, shifted in enumerate((s2, s1, dgc)):
            _acc(dcw_ref.at[k:k + 1, :], jnp.sum(shifted * gv, axis=0, keepdims=True), i == 0)
        _acc(dcb_ref, jnp.sum(dgc, axis=0, keepdims=True), i == 0)
        ebuf[tm:tm + HALO, :] = ebuf[0:HALO, :]

    tile = pl.BlockSpec((tm, tn), lambda j, i: (ni - 1 - i, j))
    wcol = lambda r: pl.BlockSpec((r, tn), lambda j, i: (0, j))
    out = jax.ShapeDtypeStruct((s, f), BF16)
    return _pcall(
        body, comm=comm, name=name, grid=(f // tn, ni),
        in_specs=[pl.BlockSpec((tm, d), lambda j, i: (ni - 1 - i, 0)), tile, tile, tile,
                  pl.BlockSpec((tn, d), lambda j, i: (j, 0)), wcol(3)],
        out_specs=[tile, tile, wcol(3), wcol(1)],
        out_shape=[out, out, jax.ShapeDtypeStruct((3, f), F32), jax.ShapeDtypeStruct((1, f), F32)],
        scratch_shapes=[pltpu.VMEM((tm + HALO, tn), F32)],
        compiler_params=_params("arbitrary", "arbitrary"),
    )(dhb, g, gc, up, wd, cw)


def _ffn_bwd2(dg, dup, wg, wu, h, gain, dh, tm, name, comm=None):
    s, d = h.shape
    f = dg.shape[1]

    def body(dg_ref, dup_ref, wg_ref, wu_ref, h_ref, gain_ref, dh_ref, dho_ref, dhb_ref, dgain_ref):
        dhn = _nt(dg_ref[...], wg_ref[...]) + _nt(dup_ref[...], wu_ref[...])
        dx, dgain = _rms_bwd(dhn, h_ref[...], gain_ref[...])
        dho = dh_ref[...] + dx
        dho_ref[...] = dho
        dhb_ref[...] = dho.astype(BF16)
        _acc(dgain_ref, dgain, pl.program_id(0) == 0)

    row = lambda c: pl.BlockSpec((tm, c), lambda i: (i, 0))
    return _pcall(
        body, comm=comm, name=name, grid=(s // tm,),
        in_specs=[row(f), row(f), _whole(wg.shape), _whole(wu.shape), row(d), _whole((1, d)), row(d)],
        out_specs=[row(d), row(d), _whole((1, d))],
        out_shape=[jax.ShapeDtypeStruct((s, d), F32), jax.ShapeDtypeStruct((s, d), BF16), jax.ShapeDtypeStruct((1, d), F32)],
        compiler_params=_params("arbitrary"),
    )(dg, dup, wg, wu, h, gain, dh)


def _odd_bwd(dhb, dh, wout, pre, gate, ws, wst, sgu, win, h, gmix, tm, comm=None):
    s, d = h.shape
    e = win.shape[1]
    cw = e // 2
    heads = ws.shape[0]
    ni = s // tm

    def body(dhb_ref, dh_ref, wout_ref, pre_ref, gate_ref, ws_ref, wst_ref, sgu_ref, win_ref, h_ref, gmix_ref,
             dpre_ref, dho_ref, dhob_ref, dgain_ref, dsgu_ref, dws_ref, db_ref, vbuf, gacc):
        i = pl.program_id(0)
        first = i == 0
        dmixo = _nt(dhb_ref[...], wout_ref[...])
        pre = pre_ref[...].astype(F32)
        p = _gelu(pre)
        u, v = p[:, :cw], p[:, cw:]
        sgu = sgu_ref[...]
        rv = lax.rsqrt(jnp.mean(v * v, axis=-1, keepdims=True) + EPS)
        vh = v * rv
        vn = (vh * sgu).astype(BF16)
        du = dmixo * gate_ref[...].astype(F32)
        dgate = dmixo * u
        dgate_b = dgate.astype(BF16)
        gsum = dgate[0:CHUNK, :]
        for n in range(1, tm // CHUNK):
            gsum = gsum + dgate[n * CHUNK:(n + 1) * CHUNK, :]
        _acc(gacc, gsum, first)
        for hd in range(heads):
            cols = slice(hd * CHUNK, (hd + 1) * CHUNK)
            dws = None
            for n in range(tm // CHUNK):
                rows = slice(n * CHUNK, (n + 1) * CHUNK)
                vbuf[rows, cols] = _nn(wst_ref[hd], dgate_b[rows, cols])
                part = _nt(dgate_b[rows, cols], vn[rows, cols])
                dws = part if dws is None else dws + part
            _acc(dws_ref.at[hd], dws, first)
        dvn = vbuf[...]
        _acc(dsgu_ref, jnp.sum(dvn * vh, axis=0, keepdims=True), first)
        dvh = dvn * sgu
        dv = rv * (dvh - vh * jnp.mean(dvh * vh, axis=-1, keepdims=True))
        dpre = (jnp.concatenate([du, dv], axis=1) * _gelu_grad(pre)).astype(BF16)
        dpre_ref[...] = dpre
        dx, dgain = _rms_bwd(_nt(dpre, win_ref[...]), h_ref[...], gmix_ref[...])
        dho = dh_ref[...] + dx
        dho_ref[...] = dho
        dhob_ref[...] = dho.astype(BF16)
        _acc(dgain_ref, dgain, first)

        @pl.when(i == ni - 1)
        def _():
            ones = jnp.ones((8, CHUNK), F32)
            for hd in range(heads):
                tot = lax.dot_general(ones, gacc[:, hd * CHUNK:(hd + 1) * CHUNK], (((1,), (1,)), ((), ())),
                                      preferred_element_type=F32, precision=lax.Precision.HIGHEST)
                db_ref[hd:hd + 1, :] = tot[0:1, :]

    row = lambda c: pl.BlockSpec((tm, c), lambda i: (i, 0))
    return _pcall(
        body, comm=comm, name="odd_bwd", grid=(ni,),
        in_specs=[row(d), row(d), _whole(wout.shape), row(e), row(cw), _whole(ws.shape), _whole(wst.shape), _whole(sgu.shape),
                  _whole(win.shape), row(d), _whole((1, d))],
        out_specs=[row(e), row(d), row(d), _whole((1, d)), _whole((1, cw)), _whole(ws.shape), _whole((heads, CHUNK))],
        out_shape=[jax.ShapeDtypeStruct((s, e), BF16), jax.ShapeDtypeStruct((s, d), F32), jax.ShapeDtypeStruct((s, d), BF16),
                   jax.ShapeDtypeStruct((1, d), F32), jax.ShapeDtypeStruct((1, cw), F32), jax.ShapeDtypeStruct(ws.shape, F32),
                   jax.ShapeDtypeStruct((heads, CHUNK), F32)],
        scratch_shapes=[pltpu.VMEM((tm, cw), F32), pltpu.VMEM((CHUNK, cw), F32)],
        compiler_params=_params("arbitrary"),
    )(dhb, dh, wout, pre, gate, ws, wst, sgu, win, h, gmix)


def _even_bwd(dhb, dh, wout, proj, cq, pooled, conva, wpool, wpoolt, pscale, win, x, gmix, tm, comm=None):
    s, d = x.shape
    e = win.shape[1]
    aw = e // 4
    ni = s // tm

    def body(dhb_ref, dh_ref, wout_ref, proj_ref, cq_ref, pooled_ref, ca_ref, wp_ref, wpt_ref, ps_ref, win_ref, x_ref, gmix_ref,
             dproj_ref, dx_ref, dgain_ref, dca_ref, dwp_ref, dps_ref, cbuf, ebuf):
        i = pl.program_id(0)
        first = i == 0

        @pl.when(first)
        def _():
            cbuf[tm:tm + HALO, :] = jnp.zeros((HALO, aw), F32)
            ebuf[tm:tm + HALO, :] = jnp.zeros((HALO, aw), F32)

        dmix = _nt(dhb_ref[...], wout_ref[...])
        dy_a, dy_b = dmix[:, :aw], dmix[:, aw:]
        proj = proj_ref[...].astype(F32)
        a_b, a_c, a_v = (proj[:, k * aw:(k + 1) * aw] for k in range(3))
        da_b = dy_a * cq_ref[...].astype(F32)
        dcq = dy_a * a_b
        cbuf[0:tm, :] = dcq
        s1 = cbuf[pl.ds(1, tm), :]
        s2 = cbuf[pl.ds(2, tm), :]
        q = a_c * a_v
        for k, shifted in enumerate((s2, s1, dcq)):
            _acc(dca_ref.at[k:k + 1, :], jnp.sum(shifted * q, axis=0, keepdims=True), first)
        dq = ca_ref[2:3, :] * dcq + ca_ref[1:2, :] * s1 + ca_ref[0:1, :] * s2
        da_c = dq * a_v
        da_v = dq * a_c
        dps, dpool = [], []
        for g, w in enumerate(WINDOWS):
            cols = slice(g * LANE, (g + 1) * LANE)
            pooled = pooled_ref[:, cols]
            mixed = _nn(pooled, wp_ref[g])
            dps.append(jnp.sum(dy_b[:, cols] * mixed, axis=0, keepdims=True))
            dmixed = (dy_b[:, cols] * ps_ref[:, cols]).astype(BF16)
            _acc(dwp_ref.at[g], _tn(pooled, dmixed), first)
            dp = _nn(dmixed, wpt_ref[g])
            dpool.append(dp)
            ebuf[0:tm, cols] = dp / _counts((ni - 1 - i) * tm, tm, w)
        _acc(dps_ref, jnp.concatenate(dps, axis=1), first)
        dzs = []
        for g, w in enumerate(WINDOWS):
            cols = slice(g * LANE, (g + 1) * LANE)
            acc = ebuf[pl.ds(0, tm), cols]
            for m in range(1, w):
                acc = acc + ebuf[pl.ds(m, tm), cols]
            dzs.append(acc - dpool[g])
        dproj = jnp.concatenate([da_b, da_c, da_v] + dzs, axis=1).astype(BF16)
        dproj_ref[...] = dproj
        dx, dgain = _rms_bwd(_nt(dproj, win_ref[...]), x_ref[...], gmix_ref[...])
        dx_ref[...] = dh_ref[...] + dx
        _acc(dgain_ref, dgain, first)
        cbuf[tm:tm + HALO, :] = cbuf[0:HALO, :]
        ebuf[tm:tm + HALO, :] = ebuf[0:HALO, :]

    row = lambda c: pl.BlockSpec((tm, c), lambda i: (ni - 1 - i, 0))
    return _pcall(
        body, comm=comm, name="even_bwd", grid=(ni,),
        in_specs=[row(d), row(d), _whole(wout.shape), row(e), row(aw), row(aw), _whole(conva.shape), _whole(wpool.shape),
                  _whole(wpoolt.shape), _whole(pscale.shape), _whole(win.shape), row(d), _whole((1, d))],
        out_specs=[row(e), row(d), _whole((1, d)), _whole(conva.shape), _whole(wpool.shape), _whole(pscale.shape)],
        out_shape=[jax.ShapeDtypeStruct((s, e), BF16), jax.ShapeDtypeStruct((s, d), F32), jax.ShapeDtypeStruct((1, d), F32),
                   jax.ShapeDtypeStruct(conva.shape, F32), jax.ShapeDtypeStruct(wpool.shape, F32),
                   jax.ShapeDtypeStruct(pscale.shape, F32)],
        scratch_shapes=[pltpu.VMEM((tm + HALO, aw), F32), pltpu.VMEM((tm + HALO, aw), F32)],
        compiler_params=_params("arbitrary"),
    )(dhb, dh, wout, proj, cq, pooled, conva, wpool, wpoolt, pscale, win, x, gmix)


def _wgrad(a, b, tk, ts, name, comm=None):
    s, ka = a.shape
    nb = b.shape[1]
    nt = s // ts

    def body(a_ref, b_ref, o_ref, acc):
        t = pl.program_id(1)
        _acc(acc, _tn(a_ref[...], b_ref[...]), t == 0)

        @pl.when(t == nt - 1)
        def _():
            o_ref[...] = acc[...].astype(BF16)

    return _pcall(
        body, comm=comm, name=name, grid=(ka // tk, nt),
        in_specs=[pl.BlockSpec((ts, tk), lambda k, t: (t, k)), pl.BlockSpec((ts, nb), lambda k, t: (t, 0))],
        out_specs=pl.BlockSpec((tk, nb), lambda k, t: (k, 0)),
        out_shape=jax.ShapeDtypeStruct((ka, nb), BF16),
        scratch_shapes=[pltpu.VMEM((tk, nb), F32)],
        compiler_params=_params("arbitrary", "arbitrary"),
    )(a, b)


def _adamw(parts, w, m, v, tr, name, layer=None, into=None):
    r, c = w.shape[-2:]
    n, rp, cp = parts.shape
    tp = tr if rp == r else rp
    assert rp == r or tr == r

    def body(p_ref, w_ref, m_ref, v_ref, *rest):
        g_ref, d_ref, mo_ref, vo_ref = rest[-4:]
        g = p_ref[0, 0:tr, 0:c].astype(F32)
        for j in range(1, n):
            g = g + p_ref[j, 0:tr, 0:c].astype(F32)
        g_ref[...] = g
        mn = ADAM_B1 * m_ref[...] + (1.0 - ADAM_B1) * g
        vn = ADAM_B2 * v_ref[...] + (1.0 - ADAM_B2) * (g * g)
        mo_ref[...] = mn
        vo_ref[...] = vn
        m_hat = mn / (1.0 - ADAM_B1 ** ADAM_STEP)
        v_hat = vn / (1.0 - ADAM_B2 ** ADAM_STEP)
        d_ref[...] = -ADAM_LR * (m_hat / (jnp.sqrt(v_hat) + ADAM_EPS) + ADAM_WD * w_ref[...])

    if layer is None:
        row = pl.BlockSpec((tr, c), lambda i: (i, 0))
    else:
        row = pl.BlockSpec((None, tr, c), lambda i: (layer, i, 0))
    out = jax.ShapeDtypeStruct(w.shape, F32)
    prev = [] if into is None else list(into)
    return _pcall(
        body, name=name, grid=(r // tr,),
        in_specs=[pl.BlockSpec((n, tp, cp), lambda i: (0, i, 0)), row, row, row] + [pl.BlockSpec(memory_space=pl.ANY)] * len(prev),
        out_specs=[row, row, row, row], out_shape=[out, out, out, out],
        input_output_aliases={4 + k: k for k in range(len(prev))},
        compiler_params=_params("arbitrary"),
    )(parts, w, m, v, *prev)


def _pair_add(grad, other, spec, core, name):
    axis, width = spec
    slot = other.shape[1:]

    def body(core_ref, g_ref, o_ref, out_ref):
        out_ref[...] = (g_ref[...].astype(F32) + o_ref[...].astype(F32)).astype(BF16)

    if axis == 0:
        gspec = pl.BlockSpec(slot, lambda q, core_ref: (2 * q + core_ref[0], 0))
    else:
        gspec = pl.BlockSpec(slot, lambda q, core_ref: (0, 2 * q + core_ref[0]))
    per_chip = pl.BlockSpec((None,) + slot, lambda q, core_ref: (q, 0, 0))
    return _pcall(
        body, name=name,
        grid_spec=pltpu.PrefetchScalarGridSpec(num_scalar_prefetch=1, grid=(N_CHIP,), in_specs=[gspec, per_chip], out_specs=per_chip),
        out_shape=jax.ShapeDtypeStruct(other.shape, BF16),
        compiler_params=_params("arbitrary"),
    )(core, grad, other)


def _hbm():
    return pl.BlockSpec(memory_space=pltpu.HBM)


def _window(ref, spec, j):
    axis, width = spec
    start = pl.multiple_of(j * width, width)
    return ref.at[(slice(None),) * axis + (pl.ds(start, width),)]


def _here():
    return lax.axis_index("x"), lax.axis_index("y"), lax.axis_index("c")


class _Gather:
    def __init__(self, shards, specs, fulls):
        n = len(shards)
        self.ins, self.specs, self.out_shape = list(shards), list(specs), list(fulls)
        self.sems = [pltpu.SemaphoreType.DMA((7 * n,)), pltpu.SemaphoreType.DMA((7 * n,)), pltpu.SemaphoreType.DMA((n,))]

    def _plan(self, ins, outs, sems):
        send_sems, recv_sems, local_sems = sems
        x, y, c = _here()
        me, sibling = (x, y, c), (x, y, 1 - c)
        chips = [(1 - x, y), (x, 1 - y), (1 - x, 1 - y)]

        def slot(t, dev):
            return _window(outs[t], self.specs[t], 4 * dev[0] + 2 * dev[1] + dev[2])

        def copy(t, k, block, to, src=None):
            return pltpu.make_async_remote_copy(
                src_ref=slot(t, block) if src is None else src, dst_ref=slot(t, block),
                send_sem=send_sems.at[7 * t + k], recv_sem=recv_sems.at[7 * t + k], device_id=to, device_id_type=MESH)

        plan = []
        for t in range(len(ins)):
            plan.append(dict(
                mine=pltpu.make_async_copy(ins[t], slot(t, me), local_sems.at[t]),
                first=[copy(t, 0, me, sibling, src=ins[t])] + [copy(t, 1 + j, me, (*q, c), src=ins[t]) for j, q in enumerate(chips)],
                over_ici=[copy(t, 1 + j, (*q, c), me) for j, q in enumerate(chips)],
                passed=[copy(t, 4 + j, (*q, c), sibling) for j, q in enumerate(chips)],
                from_sibling=[copy(t, 0, sibling, me)] + [copy(t, 4 + j, (*q, 1 - c), me) for j, q in enumerate(chips)]))
        return plan

    def start(self, ins, outs, sems):
        for p in self._plan(ins, outs, sems):
            p["mine"].start()
            for cp in p["first"]:
                cp.start()

    def finish(self, ins, outs, sems):
        plan = self._plan(ins, outs, sems)
        for p in plan:
            for arrived, onward in zip(p["over_ici"], p["passed"]):
                arrived.wait_recv()
                onward.start()
        for p in plan:
            for cp in p["from_sibling"]:
                cp.wait_recv()
        for p in plan:
            for cp in p["first"] + p["passed"]:
                cp.wait_send()
            p["mine"].wait()


class _PairExchange:
    def __init__(self, grads, specs):
        n = len(grads)
        self.ins, self.specs = list(grads), list(specs)
        self.out_shape = [jax.ShapeDtypeStruct((N_CHIP,) + a.shape[:sp[0]] + (sp[1],) + a.shape[sp[0] + 1:], a.dtype)
                          for a, sp in zip(grads, specs)]
        self.sems = [pltpu.SemaphoreType.DMA((n,)), pltpu.SemaphoreType.DMA((n,))]

    def start(self, ins, outs, sems):
        send_sems, recv_sems = sems
        x, y, c = _here()
        for t in range(len(ins)):
            for q in range(N_CHIP):
                pltpu.make_async_remote_copy(
                    src_ref=_window(ins[t], self.specs[t], 2 * q + (1 - c)), dst_ref=outs[t].at[q],
                    send_sem=send_sems.at[t], recv_sem=recv_sems.at[t], device_id=(x, y, 1 - c), device_id_type=MESH).start()

    def finish(self, ins, outs, sems):
        send_sems, recv_sems = sems
        x, y, c = _here()
        for t in range(len(ins)):
            every = pltpu.make_async_remote_copy(src_ref=outs[t], dst_ref=outs[t], send_sem=send_sems.at[t],
                                                 recv_sem=recv_sems.at[t], device_id=(x, y, 1 - c), device_id_type=MESH)
            every.wait_send()
            every.wait_recv()


class _ChipExchange:
    def __init__(self, pairs, slotted=(), whole=()):
        self.ins = list(pairs) + list(slotted) + list(whole)
        self.npair, self.nslot = len(pairs), len(pairs) + len(slotted)
        n = len(self.ins)
        self.out_shape = ([jax.ShapeDtypeStruct(a.shape, a.dtype) for a in list(pairs) + list(slotted)]
                          + [jax.ShapeDtypeStruct((N_DEV,) + a.shape, a.dtype) for a in whole])
        self.sems = [pltpu.SemaphoreType.DMA((7 * n,)), pltpu.SemaphoreType.DMA((7 * n,)), pltpu.SemaphoreType.DMA((n,))]

    def _plan(self, ins, outs, sems):
        send_sems, recv_sems, local_sems = sems
        npair, nslot = self.npair, self.nslot
        x, y, c = _here()
        me, chip = 4 * x + 2 * y + c, 2 * x + y
        chips = [(1 - x, y), (x, 1 - y), (1 - x, 1 - y)]
        peers = [(x, y, 1 - c)] + [(*q, c) for q in chips] + [(*q, 1 - c) for q in chips]

        def index(dev):
            return 4 * dev[0] + 2 * dev[1] + dev[2]

        def copy(t, k, arriving):
            peer = peers[k]
            if t < npair:
                src, mine, theirs = ins[t].at[2 * peer[0] + peer[1]], chip, 2 * peer[0] + peer[1]
            else:
                src, mine, theirs = (ins[t].at[index(peer)] if t < nslot else ins[t]), me, index(peer)
            return pltpu.make_async_remote_copy(
                src_ref=src, dst_ref=outs[t].at[theirs if arriving else mine],
                send_sem=send_sems.at[7 * t + k], recv_sem=recv_sems.at[7 * t + k], device_id=peer, device_id_type=MESH)

        own, sent, arriving = [], [], []
        for t in range(len(ins)):
            fan = range(1, 4) if t < npair else range(7)
            if t < npair:
                own.append(pltpu.make_async_copy(ins[t].at[chip], outs[t].at[chip], local_sems.at[t]))
            else:
                own.append(pltpu.make_async_copy(ins[t].at[me] if t < nslot else ins[t], outs[t].at[me], local_sems.at[t]))
            sent += [copy(t, k, False) for k in fan]
            arriving += [copy(t, k, True) for k in fan]
        return own, sent, arriving

    def start(self, ins, outs, sems):
        own, sent, _ = self._plan(ins, outs, sems)
        for cp in own + sent:
            cp.start()

    def finish(self, ins, outs, sems):
        own, sent, arriving = self._plan(ins, outs, sems)
        for cp in arriving:
            cp.wait_recv()
        for cp in sent:
            cp.wait_send()
        for cp in own:
            cp.wait()


class _Jobs:
    def __init__(self, *jobs):
        self.jobs = jobs
        self.ins = [a for j in jobs for a in j.ins]
        self.out_shape = [a for j in jobs for a in j.out_shape]
        self.sems = [a for j in jobs for a in j.sems]

    def _split(self, ins, outs, sems):
        i = o = s = 0
        for j in self.jobs:
            yield j, ins[i:i + len(j.ins)], outs[o:o + len(j.out_shape)], sems[s:s + len(j.sems)]
            i, o, s = i + len(j.ins), o + len(j.out_shape), s + len(j.sems)

    def start(self, ins, outs, sems):
        for j, a, b, c in self._split(ins, outs, sems):
            j.start(a, b, c)

    def finish(self, ins, outs, sems):
        for j, a, b, c in self._split(ins, outs, sems):
            j.finish(a, b, c)

    def results(self, outs):
        return [b for _, _, b, _ in self._split((), outs, ())]


def _alone(job, name):
    return _pcall(lambda: None, comm=job, name=name, in_specs=[], out_specs=[], out_shape=[])()[1]


SMALL_ROWS = 24
REP_COLS = 1024


def _pad_to(a, rows, cols):
    return jnp.pad(a, ((0, rows - a.shape[0]), (0, cols - a.shape[1])))


def _pack_small(conv_a, sgu_norm, conv_ffn, cols):
    return jnp.concatenate([_pad_to(conv_a, 8, cols), _pad_to(sgu_norm, 8, cols),
                            _pad_to(conv_ffn.reshape(-1, conv_ffn.shape[-1]), 8, cols)], axis=0)


def _unpack_small(p, ca_w, sg_w, cf_w):
    return p[0:3, 0:ca_w], p[8:9, 0:sg_w], p[16:22, 0:cf_w].reshape(2, 3, cf_w)


def _tile_rows(rows):
    return -(-rows // 8) * 8


def _pack_rows(parts):
    return jnp.concatenate([_pad_to(a, _tile_rows(a.shape[0]), REP_COLS) for a in parts], axis=0)


def _unpack_rows(p, shapes):
    out, r0 = [], 0
    for r, c in shapes:
        out.append(p[r0:r0 + r, 0:c])
        r0 += _tile_rows(r)
    return out


def _rep_late(norm_mix0, norm_ffn0, pool_scale, b_conv0, w_pool):
    return _pack_rows([norm_mix0, norm_ffn0, pool_scale.reshape(1, -1), _pad_to(b_conv0, 1, 3 * REP_COLS).reshape(3, REP_COLS),
                       w_pool.reshape(-1, REP_COLS)])


def _rep_early(norm_mix1, norm_ffn1, final_norm, b_spatial, b_conv1, w_spatial, loss):
    return _pack_rows([norm_mix1, norm_ffn1, final_norm.reshape(1, -1), b_spatial.reshape(1, -1),
                       _pad_to(b_conv1, 1, 3 * REP_COLS).reshape(3, REP_COLS), w_spatial.reshape(-1, REP_COLS), loss])


def _unpack_rep(late, early, like):
    f = like["b_conv_ffn"].shape[1]
    nm0, nf0, ps, bc0, wp = _unpack_rows(late, [(1, REP_COLS), (1, REP_COLS), (1, like["pool_scale"].shape[1]), (3, REP_COLS),
                                                (like["w_pool"].size // REP_COLS, REP_COLS)])
    nm1, nf1, fin, bs, bc1, wsp, _ = _unpack_rows(early, [(1, REP_COLS)] * 4 + [(3, REP_COLS), (like["w_spatial"].size // REP_COLS, REP_COLS),
                                                          (1, REP_COLS)])
    return {
        "norm_mix": jnp.concatenate([nm0, nm1]), "norm_ffn": jnp.concatenate([nf0, nf1]), "final_norm": fin[0], "pool_scale": ps,
        "b_spatial": bs.reshape(like["b_spatial"].shape),
        "b_conv_ffn": jnp.concatenate([bc0.reshape(1, -1), bc1.reshape(1, -1)])[:, 0:f],
        "w_pool": wp.reshape(like["w_pool"].shape), "w_spatial": wsp.reshape(like["w_spatial"].shape),
    }


def _pad_slots(a, width, padded):
    a = a.reshape(*a.shape[:-1], N_DEV, width)
    a = jnp.pad(a, ((0, 0),) * (a.ndim - 1) + ((0, padded - width),))
    return a.reshape(*a.shape[:-2], N_DEV * padded)


def _unpad_slots(a, width, padded):
    a = a.reshape(*a.shape[:-1], N_DEV, padded)[..., 0:width]
    return a.reshape(*a.shape[:-2], N_DEV * width)


def kernel(x, norm_mix, norm_ffn, final_norm, w_in_even, conv_a, w_pool, pool_scale, w_out_even, w_in_odd, sgu_norm, w_spatial, b_spatial, w_out_odd, w_ffn_gate, w_ffn_up, conv_ffn, b_conv_ffn, w_ffn_down, loss_target, m_norm_mix, m_norm_ffn, m_final_norm, m_w_in_even, m_conv_a, m_w_pool, m_pool_scale, m_w_out_even, m_w_in_odd, m_sgu_norm, m_w_spatial, m_b_spatial, m_w_out_odd, m_w_ffn_gate, m_w_ffn_up, m_conv_ffn, m_b_conv_ffn, m_w_ffn_down, v_norm_mix, v_norm_ffn, v_final_norm, v_w_in_even, v_conv_a, v_w_pool, v_pool_scale, v_w_out_even, v_w_in_odd, v_sgu_norm, v_w_spatial, v_b_spatial, v_w_out_odd, v_w_ffn_gate, v_w_ffn_up, v_conv_ffn, v_b_conv_ffn, v_w_ffn_down):
    s, d = x.shape[1], x.shape[2]
    x2, target = x[0], loss_target[0]
    tm = min(512, s)
    tm_wide = min(2048, s)
    tn = 256
    row = lambda a: a.reshape(1, -1)
    ein, ro = w_in_even.shape[2], w_out_even.shape[1]
    fs = w_ffn_gate.shape[2]
    fsp = -(-fs // LANE) * LANE
    fp = N_DEV * fsp
    full = lambda shape, dtype=BF16: jax.ShapeDtypeStruct(shape, dtype)

    wg_s = [jnp.pad(w_ffn_gate[l], ((0, 0), (0, fsp - fs))).astype(BF16) for l in range(2)]
    wu_s = [jnp.pad(w_ffn_up[l], ((0, 0), (0, fsp - fs))).astype(BF16) for l in range(2)]
    wd_s = [jnp.pad(w_ffn_down[l], ((0, fsp - fs), (0, 0))).astype(BF16) for l in range(2)]
    small_s = _pack_small(conv_a[0], sgu_norm, conv_ffn, fsp)[None]
    in_spec, out_spec, gu_spec, down_spec = (1, ein), (0, ro), (1, fsp), (0, fsp)
    full_in, full_out, full_gu, full_down = full((d, N_DEV * ein)), full((N_DEV * ro, d)), full((d, fp)), full((fp, d))
    win_e, wout_e, gsmall = _alone(_Gather([w_in_even[0].astype(BF16), w_out_even[0].astype(BF16), small_s], [in_spec, out_spec, (0, 1)],
                                           [full_in, full_out, full((N_DEV, SMALL_ROWS, fsp), F32)]), "gather_mix0")
    ca_full = jnp.moveaxis(gsmall[:, 0:3, 0:conv_a.shape[2]], 0, 1).reshape(3, -1)
    sgu_full = gsmall[:, 8, 0:sgu_norm.shape[1]].reshape(1, -1)
    cf_full = jnp.moveaxis(gsmall[:, 16:22, :].reshape(N_DEV, 2, 3, fsp), 0, 2).reshape(2, 3, fp)
    cb_full = _pad_slots(b_conv_ffn, fs, fsp)

    tril = jnp.tril(jnp.ones((CHUNK, CHUNK), F32))
    ws_m = w_spatial[0] * tril
    ws_b = ws_m.astype(BF16)
    wst_b = jnp.swapaxes(ws_m, 1, 2).astype(BF16)
    bfull = jnp.repeat(b_spatial[0].T, CHUNK, axis=1)
    wpool_b = w_pool[0].astype(BF16)
    wpoolt_b = jnp.swapaxes(w_pool[0], 1, 2).astype(BF16)

    (xn0, proj0, cq0, pooled0, mix0, h1, hn0), (wg0,) = _even_fwd(
        x2, norm_mix[0:1], norm_ffn[0:1], win_e, ca_full, wpool_b, pool_scale, wout_e, tm,
        comm=_Gather([wg_s[0]], [gu_spec], [full_gu]))
    (g0, gc0), (wu0,) = _ffn_gate(hn0, wg0, cf_full[0], cb_full[0:1], tm_wide, tn, "ffn_gate_l0", comm=_Gather([wu_s[0]], [gu_spec], [full_gu]))
    (up0, a0), (wd0,) = _ffn_up(hn0, wu0, gc0, tm_wide, tn, "ffn_up_l0", comm=_Gather([wd_s[0]], [down_spec], [full_down]))
    (h2, xn1), (win_o, wout_o) = _ffn_fwd2(a0, wd0, h1, norm_mix[1:2], tm, "ffn_fwd2_l0", comm=_Gather(
        [w_in_odd[0].astype(BF16), w_out_odd[0].astype(BF16)], [in_spec, out_spec], [full_in, full_out]))
    (pre1, gate1, mixo1, h3, hn1), (wg1,) = _odd_fwd(xn1, h2, win_o, sgu_full, ws_b, bfull, wout_o, norm_ffn[1:2], tm,
                                                    comm=_Gather([wg_s[1]], [gu_spec], [full_gu]))
    (g1, gc1), (wu1,) = _ffn_gate(hn1, wg1, cf_full[1], cb_full[1:2], tm_wide, tn, "ffn_gate_l1", comm=_Gather([wu_s[1]], [gu_spec], [full_gu]))
    (up1, a1), (wd1,) = _ffn_up(hn1, wu1, gc1, tm_wide, tn, "ffn_up_l1", comm=_Gather([wd_s[1]], [down_spec], [full_down]))
    h4, _ = _ffn_fwd2(a1, wd1, h3, row(final_norm), tm, "ffn_fwd2_l1")

    core = lax.axis_index("c").astype(jnp.int32).reshape(1)
    ts = min(1024, s)
    dh4, dh4b, d_final, lossvec = _loss_bwd(h4, row(final_norm), target, tm)
    gd1 = _wgrad(a1, dh4b, tn, s, "wgrad_down_l1")
    (dg1, dup1, dcw1, dcb1), (o_d1,) = _ffn_bwd1(dh4b, g1, gc1, up1, wd1, cf_full[1], tm_wide, tn, "ffn_bwd1_l1",
                                                 comm=_PairExchange([gd1], [down_spec]))
    p_d1 = _pair_add(gd1, o_d1, down_spec, core, "pair_add_down_l1")
    gg1 = _wgrad(hn1, dg1, 512, ts, "wgrad_gate_l1")
    gu1, (o_g1,) = _wgrad(hn1, dup1, 512, ts, "wgrad_up_l1", comm=_PairExchange([gg1], [gu_spec]))
    p_g1 = _pair_add(gg1, o_g1, gu_spec, core, "pair_add_gate_l1")
    jobs = _Jobs(_ChipExchange([p_d1]), _PairExchange([gu1], [gu_spec]))
    (dh3, dh3b, d_nffn1), res = _ffn_bwd2(dg1, dup1, wg1, wu1, h3, norm_ffn[1:2], dh4, tm // 2, "ffn_bwd2_l1", comm=jobs)
    (s_d1,), (o_u1,) = jobs.results(res)
    p_u1 = _pair_add(gu1, o_u1, gu_spec, core, "pair_add_up_l1")
    (dpre1, dh2, dh2b, d_nmix1, d_sgu, d_ws, d_b), (s_g1, s_u1) = _odd_bwd(
        dh3b, dh3, wout_o, pre1, gate1, ws_b, wst_b, sgu_full, win_o, h2, norm_mix[1:2], tm, comm=_ChipExchange([p_g1, p_u1]))
    gi1 = _wgrad(xn1, dpre1, 512, ts, "wgrad_in_odd")
    go1, (o_i1,) = _wgrad(mixo1, dh3b, 512, ts, "wgrad_out_odd", comm=_PairExchange([gi1], [in_spec]))
    p_i1 = _pair_add(gi1, o_i1, in_spec, core, "pair_add_in_odd")
    gd0, (o_o1,) = _wgrad(a0, dh2b, tn, s, "wgrad_down_l0", comm=_PairExchange([go1], [out_spec]))
    p_o1 = _pair_add(go1, o_o1, out_spec, core, "pair_add_out_odd")
    d_early = _rep_early(d_nmix1, d_nffn1, d_final, d_b, _unpad_slots(dcb1, fs, fsp), d_ws * tril, lossvec)
    jobs = _Jobs(_ChipExchange([p_i1, p_o1], [], [d_early]), _PairExchange([gd0], [down_spec]))
    (dg0, dup0, dcw0, dcb0), res = _ffn_bwd1(dh2b, g0, gc0, up0, wd0, cf_full[0], tm_wide, tn, "ffn_bwd1_l0", comm=jobs)
    (s_i1, s_o1, r_early), (o_d0,) = jobs.results(res)
    p_d0 = _pair_add(gd0, o_d0, down_spec, core, "pair_add_down_l0")
    gg0, (s_d0,) = _wgrad(hn0, dg0, 512, ts, "wgrad_gate_l0", comm=_ChipExchange([p_d0]))
    gu0, (o_g0,) = _wgrad(hn0, dup0, 512, ts, "wgrad_up_l0", comm=_PairExchange([gg0], [gu_spec]))
    p_g0 = _pair_add(gg0, o_g0, gu_spec, core, "pair_add_gate_l0")
    jobs = _Jobs(_ChipExchange([p_g0]), _PairExchange([gu0], [gu_spec]))
    (dh1, dh1b, d_nffn0), res = _ffn_bwd2(dg0, dup0, wg0, wu0, h1, norm_ffn[0:1], dh2, tm // 2, "ffn_bwd2_l0", comm=jobs)
    (s_g0,), (o_u0,) = jobs.results(res)
    p_u0 = _pair_add(gu0, o_u0, gu_spec, core, "pair_add_up_l0")
    (dproj0, grad_x, d_nmix0, d_ca, d_wp, d_ps), (s_u0,) = _even_bwd(
        dh1b, dh1, wout_e, proj0, cq0, pooled0, ca_full, wpool_b, wpoolt_b, pool_scale, win_e, x2, norm_mix[0:1], tm, comm=_ChipExchange([p_u0]))
    go0 = _wgrad(mix0, dh1b, 512, ts, "wgrad_out_even")
    gi0, (o_o0,) = _wgrad(xn0, dproj0, 512, ts, "wgrad_in_even", comm=_PairExchange([go0], [out_spec]))
    p_o0 = _pair_add(go0, o_o0, out_spec, core, "pair_add_out_even")
    (o_i0,) = _alone(_PairExchange([gi0], [in_spec]), "pair_exchange_in_even")
    p_i0 = _pair_add(gi0, o_i0, in_spec, core, "pair_add_in_even")
    d_small = jnp.stack([_pack_small(a, b, c, fsp) for a, b, c in zip(
        jnp.moveaxis(d_ca.reshape(3, N_DEV, -1), 1, 0), jnp.moveaxis(d_sgu.reshape(1, N_DEV, -1), 1, 0),
        jnp.moveaxis(jnp.stack([dcw0, dcw1]).reshape(2, 3, N_DEV, fsp), 2, 0))])
    d_late = _rep_late(d_nmix0, d_nffn0, d_ps, _unpad_slots(dcb0, fs, fsp), d_wp)
    s_i0, s_o0, r_small, r_late = _alone(_ChipExchange([p_i0, p_o0], [d_small], [d_late]), "chip_exchange_last")
    loss = jnp.sum(r_early[:, r_early.shape[1] - 8, :])

    out = {}
    out["w_in_even"] = _adamw(s_i0, w_in_even[0], m_w_in_even[0], v_w_in_even[0], 256, "adamw_in_even")
    out["w_out_even"] = _adamw(s_o0, w_out_even[0], m_w_out_even[0], v_w_out_even[0], ro, "adamw_out_even")
    out["w_in_odd"] = _adamw(s_i1, w_in_odd[0], m_w_in_odd[0], v_w_in_odd[0], 256, "adamw_in_odd")
    out["w_out_odd"] = _adamw(s_o1, w_out_odd[0], m_w_out_odd[0], v_w_out_odd[0], ro, "adamw_out_odd")
    for nm, s1, s0, tr, w, m, v in (("w_ffn_gate", s_g1, s_g0, 256, w_ffn_gate, m_w_ffn_gate, v_w_ffn_gate),
                                    ("w_ffn_up", s_u1, s_u0, 256, w_ffn_up, m_w_ffn_up, v_w_ffn_up),
                                    ("w_ffn_down", s_d1, s_d0, fs, w_ffn_down, m_w_ffn_down, v_w_ffn_down)):
        l1 = _adamw(s1, w, m, v, tr, "adamw_%s_l1" % nm, layer=1)
        out[nm] = _adamw(s0, w, m, v, tr, "adamw_%s_l0" % nm, layer=0, into=l1)
    small = _adamw(r_small, _pack_small(conv_a[0], sgu_norm, conv_ffn, fsp), _pack_small(m_conv_a[0], m_sgu_norm, m_conv_ffn, fsp),
                   _pack_small(v_conv_a[0], v_sgu_norm, v_conv_ffn, fsp), SMALL_ROWS, "adamw_small")
    no_loss = jnp.zeros((1, REP_COLS), F32)
    early = _adamw(r_early, *[_rep_early(nm[1:2], nf[1:2], fn, bs, bc[1:2], wsp, no_loss) for nm, nf, fn, bs, bc, wsp in (
        (norm_mix, norm_ffn, final_norm, b_spatial, b_conv_ffn, w_spatial), (m_norm_mix, m_norm_ffn, m_final_norm, m_b_spatial, m_b_conv_ffn, m_w_spatial),
        (v_norm_mix, v_norm_ffn, v_final_norm, v_b_spatial, v_b_conv_ffn, v_w_spatial))], r_early.shape[1], "adamw_replicated_early")
    late = _adamw(r_late, *[_rep_late(nm[0:1], nf[0:1], ps, bc[0:1], wp) for nm, nf, ps, bc, wp in (
        (norm_mix, norm_ffn, pool_scale, b_conv_ffn, w_pool), (m_norm_mix, m_norm_ffn, m_pool_scale, m_b_conv_ffn, m_w_pool),
        (v_norm_mix, v_norm_ffn, v_pool_scale, v_b_conv_ffn, v_w_pool))], r_late.shape[1], "adamw_replicated_late")

    names = ["norm_mix", "norm_ffn", "final_norm", "w_in_even", "conv_a", "w_pool", "pool_scale", "w_out_even", "w_in_odd", "sgu_norm",
             "w_spatial", "b_spatial", "w_out_odd", "w_ffn_gate", "w_ffn_up", "conv_ffn", "b_conv_ffn", "w_ffn_down"]
    like = {"norm_mix": norm_mix, "norm_ffn": norm_ffn, "final_norm": final_norm, "w_in_even": w_in_even, "conv_a": conv_a,
            "w_pool": w_pool, "pool_scale": pool_scale, "w_out_even": w_out_even, "w_in_odd": w_in_odd, "sgu_norm": sgu_norm,
            "w_spatial": w_spatial, "b_spatial": b_spatial, "w_out_odd": w_out_odd, "w_ffn_gate": w_ffn_gate, "w_ffn_up": w_ffn_up,
            "conv_ffn": conv_ffn, "b_conv_ffn": b_conv_ffn, "w_ffn_down": w_ffn_down}
    groups = []
    for k in range(4):
        ca_k, sg_k, cf_k = _unpack_small(small[k], conv_a.shape[2], sgu_norm.shape[1], conv_ffn.shape[2])
        vals = dict(_unpack_rep(late[k], early[k], like))
        vals.update(conv_a=ca_k, sgu_norm=sg_k, conv_ffn=cf_k)
        for nm in ("w_in_even", "w_in_odd", "w_out_even", "w_out_odd", "w_ffn_gate", "w_ffn_up", "w_ffn_down"):
            vals[nm] = out[nm][k]
        groups.append([vals[nm].reshape(like[nm].shape) for nm in names])
    return (loss, grad_x[None], *groups[0], *groups[1], *groups[2], *groups[3])
```

```python
import functools

import jax
import jax.numpy as jnp
from jax import lax
from jax.experimental import pallas as pl
from jax.experimental.pallas import tpu as pltpu

F32, BF16 = jnp.float32, jnp.bfloat16
EPS = 1e-6
WINDOWS = (2, 4, 8, 16)
HALO = 16
CHUNK = 128
N_DEV = 8
N_CHIP = 4
MESH = pl.DeviceIdType.MESH
VMEM_LIMIT = 56 * 2**20
LANE = 128
ADAM_LR, ADAM_B1, ADAM_B2, ADAM_EPS, ADAM_WD, ADAM_STEP = 0.001, 0.9, 0.999, 1e-08, 0.01, 10
INV_SQRT2 = 0.7071067811865476
INV_SQRT2PI = 0.3989422804014327


def _pcall(body, comm=None, **kw):
    if comm is None:
        return pl.pallas_call(body, **kw)
    in_specs, out_specs, out_shape = list(kw.pop("in_specs")), kw.pop("out_specs"), kw.pop("out_shape")
    single = not isinstance(out_shape, (list, tuple))
    out_specs, out_shape = ([out_specs], [out_shape]) if single else (list(out_specs), list(out_shape))
    scratch = list(kw.pop("scratch_shapes", []))
    grid = kw.get("grid", ())
    n_in, n_out, n_scr, c_in, c_out = len(in_specs), len(out_specs), len(scratch), len(comm.ins), len(comm.out_shape)

    def hosted(*refs):
        cuts = [0, n_in, n_in + c_in, n_in + c_in + n_out, n_in + c_in + n_out + c_out, n_in + c_in + n_out + c_out + n_scr, len(refs)]
        ins, cins, outs, couts, scr, sems = (refs[a:b] for a, b in zip(cuts[:-1], cuts[1:]))
        first, last = True, True
        for axis, size in enumerate(grid):
            first = jnp.logical_and(first, pl.program_id(axis) == 0)
            last = jnp.logical_and(last, pl.program_id(axis) == size - 1)
        if grid:
            pl.when(first)(lambda: comm.start(cins, couts, sems))
            body(*ins, *outs, *scr)
            pl.when(last)(lambda: comm.finish(cins, couts, sems))
        else:
            comm.start(cins, couts, sems)
            body(*ins, *outs, *scr)
            comm.finish(cins, couts, sems)

    call = pl.pallas_call(hosted, in_specs=in_specs + [_hbm()] * c_in, out_specs=out_specs + [_hbm()] * c_out,
                          out_shape=out_shape + list(comm.out_shape), scratch_shapes=scratch + list(comm.sems), **kw)

    def run(*args):
        res = call(*args, *comm.ins)
        own = res[0] if single else res[:n_out]
        return own, res[n_out:]

    return run


def _params(*sem):
    return pltpu.CompilerParams(dimension_semantics=sem, vmem_limit_bytes=VMEM_LIMIT)


def _whole(shape):
    return pl.BlockSpec(shape, lambda *_: (0,) * len(shape))


def _resident(shape):
    return pl.BlockSpec(shape, lambda *_: (0,) * len(shape), pipeline_mode=pl.Buffered(1))


def _rows(i, tm):
    return pl.ds(pl.multiple_of(i * tm, tm), tm)


def _nn(a, b):
    return jnp.dot(a, b, preferred_element_type=F32)


def _nt(a, b):
    return lax.dot_general(a, b, (((1,), (1,)), ((), ())), preferred_element_type=F32)


def _tn(a, b):
    return lax.dot_general(a, b, (((0,), (0,)), ((), ())), preferred_element_type=F32)


def _rms(x, gain):
    r = lax.rsqrt(jnp.mean(x * x, axis=-1, keepdims=True) + EPS)
    return x * r * gain


def _rms_bwd(dy, x, gain):
    r = lax.rsqrt(jnp.mean(x * x, axis=-1, keepdims=True) + EPS)
    xh = x * r
    dgain = jnp.sum(dy * xh, axis=0, keepdims=True)
    dxh = dy * gain
    dx = r * (dxh - xh * jnp.mean(dxh * xh, axis=-1, keepdims=True))
    return dx, dgain


def _gelu(x):
    return 0.5 * x * (1.0 + lax.erf(x * INV_SQRT2))


def _gelu_grad(x):
    return 0.5 * (1.0 + lax.erf(x * INV_SQRT2)) + x * jnp.exp(-0.5 * x * x) * INV_SQRT2PI


def _acc(ref, val, first):
    @pl.when(first)
    def _():
        ref[...] = val

    @pl.when(jnp.logical_not(first))
    def _():
        ref[...] += val


def _counts(row0, tm, w):
    pos1 = (row0 + lax.broadcasted_iota(jnp.int32, (tm, 1), 0) + 1).astype(F32)
    return jnp.minimum(pos1, float(w))


def _even_fwd(x, gmix, gffn, win, conva, wpool, pscale, wout, tm, comm=None):
    s, d = x.shape
    e = win.shape[1]
    aw = e // 4

    def body(x_ref, gmix_ref, gffn_ref, win_ref, ca_ref, wp_ref, ps_ref, wout_ref,
             xn_ref, proj_ref, cq_ref, pooled_ref, mix_ref, h_ref, hn_ref, qbuf, zbuf):
        i = pl.program_id(0)

        @pl.when(i == 0)
        def _():
            qbuf[0:HALO, :] = jnp.zeros((HALO, aw), F32)
            zbuf[0:HALO, :] = jnp.zeros((HALO, aw), F32)

        xv = x_ref[...]
        xn = _rms(xv, gmix_ref[...]).astype(BF16)
        xn_ref[...] = xn
        proj = _nn(xn, win_ref[...])
        proj_ref[...] = proj.astype(BF16)
        a_b, a_c, a_v, z = (proj[:, k * aw:(k + 1) * aw] for k in range(4))
        q = a_c * a_v
        qbuf[HALO:HALO + tm, :] = q
        cq = ca_ref[2:3, :] * q + ca_ref[1:2, :] * qbuf[pl.ds(HALO - 1, tm), :] + ca_ref[0:1, :] * qbuf[pl.ds(HALO - 2, tm), :]
        cq_ref[...] = cq.astype(BF16)
        y_a = a_b * cq
        zbuf[HALO:HALO + tm, :] = z
        ys = []
        for g, w in enumerate(WINDOWS):
            cols = slice(g * LANE, (g + 1) * LANE)
            acc = z[:, cols]
            for m in range(1, w):
                acc = acc + zbuf[pl.ds(HALO - m, tm), cols]
            pooled = (acc / _counts(i * tm, tm, w) - z[:, cols]).astype(BF16)
            pooled_ref[:, cols] = pooled
            ys.append(_nn(pooled, wp_ref[g]))
        y_b = jnp.concatenate(ys, axis=1) * ps_ref[...]
        mix = jnp.concatenate([y_a, y_b], axis=1).astype(BF16)
        mix_ref[...] = mix
        h = xv + _nn(mix, wout_ref[...])
        h_ref[...] = h
        hn_ref[...] = _rms(h, gffn_ref[...]).astype(BF16)
        qbuf[0:HALO, :] = qbuf[tm:tm + HALO, :]
        zbuf[0:HALO, :] = zbuf[tm:tm + HALO, :]

    row = lambda c: pl.BlockSpec((tm, c), lambda i: (i, 0))
    return _pcall(
        body, comm=comm, name="even_fwd", grid=(s // tm,),
        in_specs=[row(d), _whole((1, d)), _whole((1, d)), _whole(win.shape), _whole(conva.shape), _whole(wpool.shape),
                  _whole(pscale.shape), _whole(wout.shape)],
        out_specs=[row(d), row(e), row(aw), row(aw), row(d), row(d), row(d)],
        out_shape=[jax.ShapeDtypeStruct((s, d), BF16), jax.ShapeDtypeStruct((s, e), BF16), jax.ShapeDtypeStruct((s, aw), BF16),
                   jax.ShapeDtypeStruct((s, aw), BF16), jax.ShapeDtypeStruct((s, d), BF16), jax.ShapeDtypeStruct((s, d), F32),
                   jax.ShapeDtypeStruct((s, d), BF16)],
        scratch_shapes=[pltpu.VMEM((tm + HALO, aw), F32), pltpu.VMEM((tm + HALO, aw), F32)],
        compiler_params=_params("arbitrary"),
    )(x, gmix, gffn, win, conva, wpool, pscale, wout)


def _ffn_gate(hn, wg, cw, cb, tm, tn, name, comm=None):
    s, d = hn.shape
    f = wg.shape[1]

    def body(hn_ref, wg_ref, cw_ref, cb_ref, g_ref, gc_ref, gbuf):
        i = pl.program_id(1)

        @pl.when(i == 0)
        def _():
            gbuf[0:HALO, :] = jnp.zeros((HALO, tn), F32)

        g = _nn(hn_ref[_rows(i, tm), :], wg_ref[...])
        g_ref[...] = g.astype(BF16)
        gbuf[HALO:HALO + tm, :] = g
        gc = (cw_ref[2:3, :] * g + cw_ref[1:2, :] * gbuf[pl.ds(HALO - 1, tm), :]
              + cw_ref[0:1, :] * gbuf[pl.ds(HALO - 2, tm), :] + cb_ref[...])
        gc_ref[...] = gc.astype(BF16)
        gbuf[0:HALO, :] = gbuf[tm:tm + HALO, :]

    tile = pl.BlockSpec((tm, tn), lambda j, i: (i, j))
    wcol = lambda r: pl.BlockSpec((r, tn), lambda j, i: (0, j))
    out = jax.ShapeDtypeStruct((s, f), BF16)
    return _pcall(
        body, comm=comm, name=name, grid=(f // tn, s // tm),
        in_specs=[_resident((s, d)), wcol(d), wcol(3), wcol(1)],
        out_specs=[tile, tile], out_shape=[out, out],
        scratch_shapes=[pltpu.VMEM((tm + HALO, tn), F32)],
        compiler_params=_params("arbitrary", "arbitrary"),
    )(hn, wg, cw, cb)


def _ffn_up(hn, wu, gc, tm, tn, name, comm=None):
    s, d = hn.shape
    f = wu.shape[1]

    def body(hn_ref, wu_ref, gc_ref, up_ref, a_ref):
        up = _nn(hn_ref[_rows(pl.program_id(1), tm), :], wu_ref[...])
        up_ref[...] = up.astype(BF16)
        gc = gc_ref[...].astype(F32)
        a_ref[...] = (gc * jax.nn.sigmoid(gc) * up).astype(BF16)

    tile = pl.BlockSpec((tm, tn), lambda j, i: (i, j))
    out = jax.ShapeDtypeStruct((s, f), BF16)
    return _pcall(
        body, comm=comm, name=name, grid=(f // tn, s // tm),
        in_specs=[_resident((s, d)), pl.BlockSpec((d, tn), lambda j, i: (0, j)), tile],
        out_specs=[tile, tile], out_shape=[out, out],
        compiler_params=_params("arbitrary", "arbitrary"),
    )(hn, wu, gc)


def _ffn_fwd2(a, wd, h, gain, tm, name, comm=None):
    s, d = h.shape
    f = a.shape[1]

    def body(a_ref, wd_ref, h_ref, gain_ref, ho_ref, hn_ref):
        ho = h_ref[...] + _nn(a_ref[...], wd_ref[...])
        ho_ref[...] = ho
        hn_ref[...] = _rms(ho, gain_ref[...]).astype(BF16)

    row = lambda c: pl.BlockSpec((tm, c), lambda i: (i, 0))
    return _pcall(
        body, comm=comm, name=name, grid=(s // tm,),
        in_specs=[row(f), _whole(wd.shape), row(d), _whole((1, d))],
        out_specs=[row(d), row(d)],
        out_shape=[jax.ShapeDtypeStruct((s, d), F32), jax.ShapeDtypeStruct((s, d), BF16)],
        compiler_params=_params("arbitrary"),
    )(a, wd, h, gain)


def _odd_fwd(xn, h, win, sgu, ws, bfull, wout, gffn, tm, comm=None):
    s, d = h.shape
    e = win.shape[1]
    cw = e // 2
    heads = ws.shape[0]

    def body(xn_ref, h_ref, win_ref, sgu_ref, ws_ref, b_ref, wout_ref, gffn_ref,
             pre_ref, gate_ref, mixo_ref, ho_ref, hn_ref, gbuf):
        pre = _nn(xn_ref[...], win_ref[...])
        pre_ref[...] = pre.astype(BF16)
        p = _gelu(pre)
        u, v = p[:, :cw], p[:, cw:]
        vn = _rms(v, sgu_ref[...]).astype(BF16)
        for n in range(tm // CHUNK):
            rows = slice(n * CHUNK, (n + 1) * CHUNK)
            for hd in range(heads):
                cols = slice(hd * CHUNK, (hd + 1) * CHUNK)
                gbuf[rows, cols] = _nn(ws_ref[hd], vn[rows, cols]) + b_ref[:, cols]
        gate = gbuf[...]
        gate_ref[...] = gate.astype(BF16)
        mixo = (u * gate).astype(BF16)
        mixo_ref[...] = mixo
        ho = h_ref[...] + _nn(mixo, wout_ref[...])
        ho_ref[...] = ho
        hn_ref[...] = _rms(ho, gffn_ref[...]).astype(BF16)

    row = lambda c: pl.BlockSpec((tm, c), lambda i: (i, 0))
    return _pcall(
        body, comm=comm, name="odd_fwd", grid=(s // tm,),
        in_specs=[row(d), row(d), _whole(win.shape), _whole(sgu.shape), _whole(ws.shape), _whole(bfull.shape),
                  _whole(wout.shape), _whole((1, d))],
        out_specs=[row(e), row(cw), row(cw), row(d), row(d)],
        out_shape=[jax.ShapeDtypeStruct((s, e), BF16), jax.ShapeDtypeStruct((s, cw), BF16), jax.ShapeDtypeStruct((s, cw), BF16),
                   jax.ShapeDtypeStruct((s, d), F32), jax.ShapeDtypeStruct((s, d), BF16)],
        scratch_shapes=[pltpu.VMEM((tm, cw), F32)],
        compiler_params=_params("arbitrary"),
    )(xn, h, win, sgu, ws, bfull, wout, gffn)


def _loss_bwd(h, gain, target, tm):
    s, d = h.shape

    def body(h_ref, gain_ref, t_ref, dh_ref, dhb_ref, dgain_ref, loss_ref):
        i = pl.program_id(0)
        hv = h_ref[...]
        gain = gain_ref[...]
        err = _rms(hv, gain) - t_ref[...]
        dy = err * (1.0 / d)
        dx, dgain = _rms_bwd(dy, hv, gain)
        dh_ref[...] = dx
        dhb_ref[...] = dx.astype(BF16)
        _acc(dgain_ref, dgain, i == 0)
        _acc(loss_ref, jnp.sum(err * err, axis=0, keepdims=True) * (0.5 / d), i == 0)

    row = pl.BlockSpec((tm, d), lambda i: (i, 0))
    return _pcall(
        body, name="loss_bwd", grid=(s // tm,),
        in_specs=[row, _whole((1, d)), row],
        out_specs=[row, row, _whole((1, d)), _whole((1, d))],
        out_shape=[jax.ShapeDtypeStruct((s, d), F32), jax.ShapeDtypeStruct((s, d), BF16), jax.ShapeDtypeStruct((1, d), F32),
                   jax.ShapeDtypeStruct((1, d), F32)],
        compiler_params=_params("arbitrary"),
    )(h, gain, target)


def _ffn_bwd1(dhb, g, gc, up, wd, cw, tm, tn, name, comm=None):
    s, d = dhb.shape
    f = g.shape[1]
    ni = s // tm

    def body(dh_ref, g_ref, gc_ref, up_ref, wd_ref, cw_ref, dg_ref, dup_ref, dcw_ref, dcb_ref, ebuf):
        i = pl.program_id(1)

        @pl.when(i == 0)
        def _():
            ebuf[tm:tm + HALO, :] = jnp.zeros((HALO, tn), F32)

        da = _nt(dh_ref[_rows(ni - 1 - i, tm), :], wd_ref[...])
        gcv = gc_ref[...].astype(F32)
        sg = jax.nn.sigmoid(gcv)
        dup_ref[...] = (da * gcv * sg).astype(BF16)
        dgc = da * up_ref[...].astype(F32) * (sg * (1.0 + gcv * (1.0 - sg)))
        ebuf[0:tm, :] = dgc
        s1 = ebuf[pl.ds(1, tm), :]
        s2 = ebuf[pl.ds(2, tm), :]
        dg_ref[...] = (cw_ref[2:3, :] * dgc + cw_ref[1:2, :] * s1 + cw_ref[0:1, :] * s2).astype(BF16)
        gv = g_ref[...].astype(F32)
        for k, shifted in enumerate((s2, s1, dgc)):
            _acc(dcw_ref.at[k:k + 1, :], jnp.sum(shifted * gv, axis=0, keepdims=True), i == 0)
        _acc(dcb_ref, jnp.sum(dgc, axis=0, keepdims=True), i == 0)
        ebuf[tm:tm + HALO, :] = ebuf[0:HALO, :]

    tile = pl.BlockSpec((tm, tn), lambda j, i: (ni - 1 - i, j))
    wcol = lambda r: pl.BlockSpec((r, tn), lambda j, i: (0, j))
    out = jax.ShapeDtypeStruct((s, f), BF16)
    return _pcall(
        body, comm=comm, name=name, grid=(f // tn, ni),
        in_specs=[_resident((s, d)), tile, tile, tile,
                  pl.BlockSpec((tn, d), lambda j, i: (j, 0)), wcol(3)],
        out_specs=[tile, tile, wcol(3), wcol(1)],
        out_shape=[out, out, jax.ShapeDtypeStruct((3, f), F32), jax.ShapeDtypeStruct((1, f), F32)],
        scratch_shapes=[pltpu.VMEM((tm + HALO, tn), F32)],
        compiler_params=_params("arbitrary", "arbitrary"),
    )(dhb, g, gc, up, wd, cw)


def _ffn_bwd2(dg, dup, wg, wu, h, gain, dh, tm, name, comm=None):
    s, d = h.shape
    f = dg.shape[1]

    def body(dg_ref, dup_ref, wg_ref, wu_ref, h_ref, gain_ref, dh_ref, dho_ref, dhb_ref, dgain_ref):
        dhn = _nt(dg_ref[...], wg_ref[...]) + _nt(dup_ref[...], wu_ref[...])
        dx, dgain = _rms_bwd(dhn, h_ref[...], gain_ref[...])
        dho = dh_ref[...] + dx
        dho_ref[...] = dho
        dhb_ref[...] = dho.astype(BF16)
        _acc(dgain_ref, dgain, pl.program_id(0) == 0)

    row = lambda c: pl.BlockSpec((tm, c), lambda i: (i, 0))
    return _pcall(
        body, comm=comm, name=name, grid=(s // tm,),
        in_specs=[row(f), row(f), _whole(wg.shape), _whole(wu.shape), row(d), _whole((1, d)), row(d)],
        out_specs=[row(d), row(d), _whole((1, d))],
        out_shape=[jax.ShapeDtypeStruct((s, d), F32), jax.ShapeDtypeStruct((s, d), BF16), jax.ShapeDtypeStruct((1, d), F32)],
        compiler_params=_params("arbitrary"),
    )(dg, dup, wg, wu, h, gain, dh)


def _odd_bwd(dhb, dh, wout, pre, gate, ws, wst, sgu, win, h, gmix, tm, comm=None):
    s, d = h.shape
    e = win.shape[1]
    cw = e // 2
    heads = ws.shape[0]
    ni = s // tm

    def body(dhb_ref, dh_ref, wout_ref, pre_ref, gate_ref, ws_ref, wst_ref, sgu_ref, win_ref, h_ref, gmix_ref,
             dpre_ref, dho_ref, dhob_ref, dgain_ref, dsgu_ref, dws_ref, db_ref, vbuf, gacc):
        i = pl.program_id(0)
        first = i == 0
        dmixo = _nt(dhb_ref[...], wout_ref[...])
        pre = pre_ref[...].astype(F32)
        p = _gelu(pre)
        u, v = p[:, :cw], p[:, cw:]
        sgu = sgu_ref[...]
        rv = lax.rsqrt(jnp.mean(v * v, axis=-1, keepdims=True) + EPS)
        vh = v * rv
        vn = (vh * sgu).astype(BF16)
        du = dmixo * gate_ref[...].astype(F32)
        dgate = dmixo * u
        dgate_b = dgate.astype(BF16)
        gsum = dgate[0:CHUNK, :]
        for n in range(1, tm // CHUNK):
            gsum = gsum + dgate[n * CHUNK:(n + 1) * CHUNK, :]
        _acc(gacc, gsum, first)
        for hd in range(heads):
            cols = slice(hd * CHUNK, (hd + 1) * CHUNK)
            dws = None
            for n in range(tm // CHUNK):
                rows = slice(n * CHUNK, (n + 1) * CHUNK)
                vbuf[rows, cols] = _nn(wst_ref[hd], dgate_b[rows, cols])
                part = _nt(dgate_b[rows, cols], vn[rows, cols])
                dws = part if dws is None else dws + part
            _acc(dws_ref.at[hd], dws, first)
        dvn = vbuf[...]
        _acc(dsgu_ref, jnp.sum(dvn * vh, axis=0, keepdims=True), first)
        dvh = dvn * sgu
        dv = rv * (dvh - vh * jnp.mean(dvh * vh, axis=-1, keepdims=True))
        dpre = (jnp.concatenate([du, dv], axis=1) * _gelu_grad(pre)).astype(BF16)
        dpre_ref[...] = dpre
        dx, dgain = _rms_bwd(_nt(dpre, win_ref[...]), h_ref[...], gmix_ref[...])
        dho = dh_ref[...] + dx
        dho_ref[...] = dho
        dhob_ref[...] = dho.astype(BF16)
        _acc(dgain_ref, dgain, first)

        @pl.when(i == ni - 1)
        def _():
            ones = jnp.ones((8, CHUNK), F32)
            for hd in range(heads):
                tot = lax.dot_general(ones, gacc[:, hd * CHUNK:(hd + 1) * CHUNK], (((1,), (1,)), ((), ())),
                                      preferred_element_type=F32, precision=lax.Precision.HIGHEST)
                db_ref[hd:hd + 1, :] = tot[0:1, :]

    row = lambda c: pl.BlockSpec((tm, c), lambda i: (i, 0))
    return _pcall(
        body, comm=comm, name="odd_bwd", grid=(ni,),
        in_specs=[row(d), row(d), _whole(wout.shape), row(e), row(cw), _whole(ws.shape), _whole(wst.shape), _whole(sgu.shape),
                  _whole(win.shape), row(d), _whole((1, d))],
        out_specs=[row(e), row(d), row(d), _whole((1, d)), _whole((1, cw)), _whole(ws.shape), _whole((heads, CHUNK))],
        out_shape=[jax.ShapeDtypeStruct((s, e), BF16), jax.ShapeDtypeStruct((s, d), F32), jax.ShapeDtypeStruct((s, d), BF16),
                   jax.ShapeDtypeStruct((1, d), F32), jax.ShapeDtypeStruct((1, cw), F32), jax.ShapeDtypeStruct(ws.shape, F32),
                   jax.ShapeDtypeStruct((heads, CHUNK), F32)],
        scratch_shapes=[pltpu.VMEM((tm, cw), F32), pltpu.VMEM((CHUNK, cw), F32)],
        compiler_params=_params("arbitrary"),
    )(dhb, dh, wout, pre, gate, ws, wst, sgu, win, h, gmix)


def _even_bwd(dhb, dh, wout, proj, cq, pooled, conva, wpool, wpoolt, pscale, win, x, gmix, tm, comm=None):
    s, d = x.shape
    e = win.shape[1]
    aw = e // 4
    ni = s // tm

    def body(dhb_ref, dh_ref, wout_ref, proj_ref, cq_ref, pooled_ref, ca_ref, wp_ref, wpt_ref, ps_ref, win_ref, x_ref, gmix_ref,
             dproj_ref, dx_ref, dgain_ref, dca_ref, dwp_ref, dps_ref, cbuf, ebuf):
        i = pl.program_id(0)
        first = i == 0

        @pl.when(first)
        def _():
            cbuf[tm:tm + HALO, :] = jnp.zeros((HALO, aw), F32)
            ebuf[tm:tm + HALO, :] = jnp.zeros((HALO, aw), F32)

        dmix = _nt(dhb_ref[...], wout_ref[...])
        dy_a, dy_b = dmix[:, :aw], dmix[:, aw:]
        proj = proj_ref[...].astype(F32)
        a_b, a_c, a_v = (proj[:, k * aw:(k + 1) * aw] for k in range(3))
        da_b = dy_a * cq_ref[...].astype(F32)
        dcq = dy_a * a_b
        cbuf[0:tm, :] = dcq
        s1 = cbuf[pl.ds(1, tm), :]
        s2 = cbuf[pl.ds(2, tm), :]
        q = a_c * a_v
        for k, shifted in enumerate((s2, s1, dcq)):
            _acc(dca_ref.at[k:k + 1, :], jnp.sum(shifted * q, axis=0, keepdims=True), first)
        dq = ca_ref[2:3, :] * dcq + ca_ref[1:2, :] * s1 + ca_ref[0:1, :] * s2
        da_c = dq * a_v
        da_v = dq * a_c
        dps, dpool = [], []
        for g, w in enumerate(WINDOWS):
            cols = slice(g * LANE, (g + 1) * LANE)
            pooled = pooled_ref[:, cols]
            mixed = _nn(pooled, wp_ref[g])
            dps.append(jnp.sum(dy_b[:, cols] * mixed, axis=0, keepdims=True))
            dmixed = (dy_b[:, cols] * ps_ref[:, cols]).astype(BF16)
            _acc(dwp_ref.at[g], _tn(pooled, dmixed), first)
            dp = _nn(dmixed, wpt_ref[g])
            dpool.append(dp)
            ebuf[0:tm, cols] = dp / _counts((ni - 1 - i) * tm, tm, w)
        _acc(dps_ref, jnp.concatenate(dps, axis=1), first)
        dzs = []
        for g, w in enumerate(WINDOWS):
            cols = slice(g * LANE, (g + 1) * LANE)
            acc = ebuf[pl.ds(0, tm), cols]
            for m in range(1, w):
                acc = acc + ebuf[pl.ds(m, tm), cols]
            dzs.append(acc - dpool[g])
        dproj = jnp.concatenate([da_b, da_c, da_v] + dzs, axis=1).astype(BF16)
        dproj_ref[...] = dproj
        dx, dgain = _rms_bwd(_nt(dproj, win_ref[...]), x_ref[...], gmix_ref[...])
        dx_ref[...] = dh_ref[...] + dx
        _acc(dgain_ref, dgain, first)
        cbuf[tm:tm + HALO, :] = cbuf[0:HALO, :]
        ebuf[tm:tm + HALO, :] = ebuf[0:HALO, :]

    row = lambda c: pl.BlockSpec((tm, c), lambda i: (ni - 1 - i, 0))
    return _pcall(
        body, comm=comm, name="even_bwd", grid=(ni,),
        in_specs=[row(d), row(d), _whole(wout.shape), row(e), row(aw), row(aw), _whole(conva.shape), _whole(wpool.shape),
                  _whole(wpoolt.shape), _whole(pscale.shape), _whole(win.shape), row(d), _whole((1, d))],
        out_specs=[row(e), row(d), _whole((1, d)), _whole(conva.shape), _whole(wpool.shape), _whole(pscale.shape)],
        out_shape=[jax.ShapeDtypeStruct((s, e), BF16), jax.ShapeDtypeStruct((s, d), F32), jax.ShapeDtypeStruct((1, d), F32),
                   jax.ShapeDtypeStruct(conva.shape, F32), jax.ShapeDtypeStruct(wpool.shape, F32),
                   jax.ShapeDtypeStruct(pscale.shape, F32)],
        scratch_shapes=[pltpu.VMEM((tm + HALO, aw), F32), pltpu.VMEM((tm + HALO, aw), F32)],
        compiler_params=_params("arbitrary"),
    )(dhb, dh, wout, proj, cq, pooled, conva, wpool, wpoolt, pscale, win, x, gmix)


def _wgrad(a, b, tk, ts, name, comm=None):
    s, ka = a.shape
    nb = b.shape[1]
    nt = s // ts

    def body(a_ref, b_ref, o_ref, acc):
        t = pl.program_id(1)
        _acc(acc, _tn(a_ref[...], b_ref[...]), t == 0)

        @pl.when(t == nt - 1)
        def _():
            o_ref[...] = acc[...].astype(BF16)

    return _pcall(
        body, comm=comm, name=name, grid=(ka // tk, nt),
        in_specs=[pl.BlockSpec((ts, tk), lambda k, t: (t, k)), pl.BlockSpec((ts, nb), lambda k, t: (t, 0))],
        out_specs=pl.BlockSpec((tk, nb), lambda k, t: (k, 0)),
        out_shape=jax.ShapeDtypeStruct((ka, nb), BF16),
        scratch_shapes=[pltpu.VMEM((tk, nb), F32)],
        compiler_params=_params("arbitrary", "arbitrary"),
    )(a, b)


def _adamw(parts, w, m, v, tr, name, layer=None, into=None):
    r, c = w.shape[-2:]
    n, rp, cp = parts.shape
    tp = tr if rp == r else rp
    assert rp == r or tr == r

    def body(p_ref, w_ref, m_ref, v_ref, *rest):
        g_ref, d_ref, mo_ref, vo_ref = rest[-4:]
        g = p_ref[0, 0:tr, 0:c].astype(F32)
        for j in range(1, n):
            g = g + p_ref[j, 0:tr, 0:c].astype(F32)
        g_ref[...] = g
        mn = ADAM_B1 * m_ref[...] + (1.0 - ADAM_B1) * g
        vn = ADAM_B2 * v_ref[...] + (1.0 - ADAM_B2) * (g * g)
        mo_ref[...] = mn
        vo_ref[...] = vn
        m_hat = mn / (1.0 - ADAM_B1 ** ADAM_STEP)
        v_hat = vn / (1.0 - ADAM_B2 ** ADAM_STEP)
        d_ref[...] = -ADAM_LR * (m_hat / (jnp.sqrt(v_hat) + ADAM_EPS) + ADAM_WD * w_ref[...])

    if layer is None:
        row = pl.BlockSpec((tr, c), lambda i: (i, 0))
    else:
        row = pl.BlockSpec((None, tr, c), lambda i: (layer, i, 0))
    out = jax.ShapeDtypeStruct(w.shape, F32)
    prev = [] if into is None else list(into)
    return _pcall(
        body, name=name, grid=(r // tr,),
        in_specs=[pl.BlockSpec((n, tp, cp), lambda i: (0, i, 0)), row, row, row] + [pl.BlockSpec(memory_space=pl.ANY)] * len(prev),
        out_specs=[row, row, row, row], out_shape=[out, out, out, out],
        input_output_aliases={4 + k: k for k in range(len(prev))},
        compiler_params=_params("arbitrary"),
    )(parts, w, m, v, *prev)


def _pair_add(grad, other, spec, core, name):
    axis, width = spec
    slot = other.shape[1:]

    def body(core_ref, g_ref, o_ref, out_ref):
        out_ref[...] = (g_ref[...].astype(F32) + o_ref[...].astype(F32)).astype(BF16)

    if axis == 0:
        gspec = pl.BlockSpec(slot, lambda q, core_ref: (2 * q + core_ref[0], 0))
    else:
        gspec = pl.BlockSpec(slot, lambda q, core_ref: (0, 2 * q + core_ref[0]))
    per_chip = pl.BlockSpec((None,) + slot, lambda q, core_ref: (q, 0, 0))
    return _pcall(
        body, name=name,
        grid_spec=pltpu.PrefetchScalarGridSpec(num_scalar_prefetch=1, grid=(N_CHIP,), in_specs=[gspec, per_chip], out_specs=per_chip),
        out_shape=jax.ShapeDtypeStruct(other.shape, BF16),
        compiler_params=_params("arbitrary"),
    )(core, grad, other)


def _hbm():
    return pl.BlockSpec(memory_space=pltpu.HBM)


def _window(ref, spec, j):
    axis, width = spec
    start = pl.multiple_of(j * width, width)
    return ref.at[(slice(None),) * axis + (pl.ds(start, width),)]


def _here():
    return lax.axis_index("x"), lax.axis_index("y"), lax.axis_index("c")


class _Gather:
    def __init__(self, shards, specs, fulls):
        n = len(shards)
        self.ins, self.specs, self.out_shape = list(shards), list(specs), list(fulls)
        self.sems = [pltpu.SemaphoreType.DMA((7 * n,)), pltpu.SemaphoreType.DMA((7 * n,)), pltpu.SemaphoreType.DMA((n,))]

    def _plan(self, ins, outs, sems):
        send_sems, recv_sems, local_sems = sems
        x, y, c = _here()
        me, sibling = (x, y, c), (x, y, 1 - c)
        chips = [(1 - x, y), (x, 1 - y), (1 - x, 1 - y)]

        def slot(t, dev):
            return _window(outs[t], self.specs[t], 4 * dev[0] + 2 * dev[1] + dev[2])

        def copy(t, k, block, to, src=None):
            return pltpu.make_async_remote_copy(
                src_ref=slot(t, block) if src is None else src, dst_ref=slot(t, block),
                send_sem=send_sems.at[7 * t + k], recv_sem=recv_sems.at[7 * t + k], device_id=to, device_id_type=MESH)

        plan = []
        for t in range(len(ins)):
            plan.append(dict(
                mine=pltpu.make_async_copy(ins[t], slot(t, me), local_sems.at[t]),
                first=[copy(t, 0, me, sibling, src=ins[t])] + [copy(t, 1 + j, me, (*q, c), src=ins[t]) for j, q in enumerate(chips)],
                over_ici=[copy(t, 1 + j, (*q, c), me) for j, q in enumerate(chips)],
                passed=[copy(t, 4 + j, (*q, c), sibling) for j, q in enumerate(chips)],
                from_sibling=[copy(t, 0, sibling, me)] + [copy(t, 4 + j, (*q, 1 - c), me) for j, q in enumerate(chips)]))
        return plan

    def start(self, ins, outs, sems):
        for p in self._plan(ins, outs, sems):
            p["mine"].start()
            for cp in p["first"]:
                cp.start()

    def finish(self, ins, outs, sems):
        plan = self._plan(ins, outs, sems)
        for p in plan:
            for arrived, onward in zip(p["over_ici"], p["passed"]):
                arrived.wait_recv()
                onward.start()
        for p in plan:
            for cp in p["from_sibling"]:
                cp.wait_recv()
        for p in plan:
            for cp in p["first"] + p["passed"]:
                cp.wait_send()
            p["mine"].wait()


class _PairExchange:
    def __init__(self, grads, specs):
        n = len(grads)
        self.ins, self.specs = list(grads), list(specs)
        self.out_shape = [jax.ShapeDtypeStruct((N_CHIP,) + a.shape[:sp[0]] + (sp[1],) + a.shape[sp[0] + 1:], a.dtype)
                          for a, sp in zip(grads, specs)]
        self.sems = [pltpu.SemaphoreType.DMA((n,)), pltpu.SemaphoreType.DMA((n,))]

    def start(self, ins, outs, sems):
        send_sems, recv_sems = sems
        x, y, c = _here()
        for t in range(len(ins)):
            for q in range(N_CHIP):
                pltpu.make_async_remote_copy(
                    src_ref=_window(ins[t], self.specs[t], 2 * q + (1 - c)), dst_ref=outs[t].at[q],
                    send_sem=send_sems.at[t], recv_sem=recv_sems.at[t], device_id=(x, y, 1 - c), device_id_type=MESH).start()

    def finish(self, ins, outs, sems):
        send_sems, recv_sems = sems
        x, y, c = _here()
        for t in range(len(ins)):
            every = pltpu.make_async_remote_copy(src_ref=outs[t], dst_ref=outs[t], send_sem=send_sems.at[t],
                                                 recv_sem=recv_sems.at[t], device_id=(x, y, 1 - c), device_id_type=MESH)
            every.wait_send()
            every.wait_recv()


class _ChipExchange:
    def __init__(self, pairs, slotted=(), whole=()):
        self.ins = list(pairs) + list(slotted) + list(whole)
        self.npair, self.nslot = len(pairs), len(pairs) + len(slotted)
        n = len(self.ins)
        self.out_shape = ([jax.ShapeDtypeStruct(a.shape, a.dtype) for a in list(pairs) + list(slotted)]
                          + [jax.ShapeDtypeStruct((N_DEV,) + a.shape, a.dtype) for a in whole])
        self.sems = [pltpu.SemaphoreType.DMA((7 * n,)), pltpu.SemaphoreType.DMA((7 * n,)), pltpu.SemaphoreType.DMA((n,))]

    def _plan(self, ins, outs, sems):
        send_sems, recv_sems, local_sems = sems
        npair, nslot = self.npair, self.nslot
        x, y, c = _here()
        me, chip = 4 * x + 2 * y + c, 2 * x + y
        chips = [(1 - x, y), (x, 1 - y), (1 - x, 1 - y)]
        peers = [(x, y, 1 - c)] + [(*q, c) for q in chips] + [(*q, 1 - c) for q in chips]

        def index(dev):
            return 4 * dev[0] + 2 * dev[1] + dev[2]

        def copy(t, k, arriving):
            peer = peers[k]
            if t < npair:
                src, mine, theirs = ins[t].at[2 * peer[0] + peer[1]], chip, 2 * peer[0] + peer[1]
            else:
                src, mine, theirs = (ins[t].at[index(peer)] if t < nslot else ins[t]), me, index(peer)
            return pltpu.make_async_remote_copy(
                src_ref=src, dst_ref=outs[t].at[theirs if arriving else mine],
                send_sem=send_sems.at[7 * t + k], recv_sem=recv_sems.at[7 * t + k], device_id=peer, device_id_type=MESH)

        own, sent, arriving = [], [], []
        for t in range(len(ins)):
            fan = range(1, 4) if t < npair else range(7)
            if t < npair:
                own.append(pltpu.make_async_copy(ins[t].at[chip], outs[t].at[chip], local_sems.at[t]))
            else:
                own.append(pltpu.make_async_copy(ins[t].at[me] if t < nslot else ins[t], outs[t].at[me], local_sems.at[t]))
            sent += [copy(t, k, False) for k in fan]
            arriving += [copy(t, k, True) for k in fan]
        return own, sent, arriving

    def start(self, ins, outs, sems):
        own, sent, _ = self._plan(ins, outs, sems)
        for cp in own + sent:
            cp.start()

    def finish(self, ins, outs, sems):
        own, sent, arriving = self._plan(ins, outs, sems)
        for cp in arriving:
            cp.wait_recv()
        for cp in sent:
            cp.wait_send()
        for cp in own:
            cp.wait()


class _Jobs:
    def __init__(self, *jobs):
        self.jobs = jobs
        self.ins = [a for j in jobs for a in j.ins]
        self.out_shape = [a for j in jobs for a in j.out_shape]
        self.sems = [a for j in jobs for a in j.sems]

    def _split(self, ins, outs, sems):
        i = o = s = 0
        for j in self.jobs:
            yield j, ins[i:i + len(j.ins)], outs[o:o + len(j.out_shape)], sems[s:s + len(j.sems)]
            i, o, s = i + len(j.ins), o + len(j.out_shape), s + len(j.sems)

    def start(self, ins, outs, sems):
        for j, a, b, c in self._split(ins, outs, sems):
            j.start(a, b, c)

    def finish(self, ins, outs, sems):
        for j, a, b, c in self._split(ins, outs, sems):
            j.finish(a, b, c)

    def results(self, outs):
        return [b for _, _, b, _ in self._split((), outs, ())]


def _alone(job, name):
    return _pcall(lambda: None, comm=job, name=name, in_specs=[], out_specs=[], out_shape=[])()[1]


SMALL_ROWS = 24
REP_COLS = 1024


def _pad_to(a, rows, cols):
    return jnp.pad(a, ((0, rows - a.shape[0]), (0, cols - a.shape[1])))


def _pack_small(conv_a, sgu_norm, conv_ffn, cols):
    return jnp.concatenate([_pad_to(conv_a, 8, cols), _pad_to(sgu_norm, 8, cols),
                            _pad_to(conv_ffn.reshape(-1, conv_ffn.shape[-1]), 8, cols)], axis=0)


def _unpack_small(p, ca_w, sg_w, cf_w):
    return p[0:3, 0:ca_w], p[8:9, 0:sg_w], p[16:22, 0:cf_w].reshape(2, 3, cf_w)


def _tile_rows(rows):
    return -(-rows // 8) * 8


def _pack_rows(parts):
    return jnp.concatenate([_pad_to(a, _tile_rows(a.shape[0]), REP_COLS) for a in parts], axis=0)


def _unpack_rows(p, shapes):
    out, r0 = [], 0
    for r, c in shapes:
        out.append(p[r0:r0 + r, 0:c])
        r0 += _tile_rows(r)
    return out


def _rep_late(norm_mix0, norm_ffn0, pool_scale, b_conv0, w_pool):
    return _pack_rows([norm_mix0, norm_ffn0, pool_scale.reshape(1, -1), _pad_to(b_conv0, 1, 3 * REP_COLS).reshape(3, REP_COLS),
                       w_pool.reshape(-1, REP_COLS)])


def _rep_early(norm_mix1, norm_ffn1, final_norm, b_spatial, b_conv1, w_spatial, loss):
    return _pack_rows([norm_mix1, norm_ffn1, final_norm.reshape(1, -1), b_spatial.reshape(1, -1),
                       _pad_to(b_conv1, 1, 3 * REP_COLS).reshape(3, REP_COLS), w_spatial.reshape(-1, REP_COLS), loss])


def _unpack_rep(late, early, like):
    f = like["b_conv_ffn"].shape[1]
    nm0, nf0, ps, bc0, wp = _unpack_rows(late, [(1, REP_COLS), (1, REP_COLS), (1, like["pool_scale"].shape[1]), (3, REP_COLS),
                                                (like["w_pool"].size // REP_COLS, REP_COLS)])
    nm1, nf1, fin, bs, bc1, wsp, _ = _unpack_rows(early, [(1, REP_COLS)] * 4 + [(3, REP_COLS), (like["w_spatial"].size // REP_COLS, REP_COLS),
                                                          (1, REP_COLS)])
    return {
        "norm_mix": jnp.concatenate([nm0, nm1]), "norm_ffn": jnp.concatenate([nf0, nf1]), "final_norm": fin[0], "pool_scale": ps,
        "b_spatial": bs.reshape(like["b_spatial"].shape),
        "b_conv_ffn": jnp.concatenate([bc0.reshape(1, -1), bc1.reshape(1, -1)])[:, 0:f],
        "w_pool": wp.reshape(like["w_pool"].shape), "w_spatial": wsp.reshape(like["w_spatial"].shape),
    }


def _pad_slots(a, width, padded):
    a = a.reshape(*a.shape[:-1], N_DEV, width)
    a = jnp.pad(a, ((0, 0),) * (a.ndim - 1) + ((0, padded - width),))
    return a.reshape(*a.shape[:-2], N_DEV * padded)


def _unpad_slots(a, width, padded):
    a = a.reshape(*a.shape[:-1], N_DEV, padded)[..., 0:width]
    return a.reshape(*a.shape[:-2], N_DEV * width)


def kernel(x, norm_mix, norm_ffn, final_norm, w_in_even, conv_a, w_pool, pool_scale, w_out_even, w_in_odd, sgu_norm, w_spatial, b_spatial, w_out_odd, w_ffn_gate, w_ffn_up, conv_ffn, b_conv_ffn, w_ffn_down, loss_target, m_norm_mix, m_norm_ffn, m_final_norm, m_w_in_even, m_conv_a, m_w_pool, m_pool_scale, m_w_out_even, m_w_in_odd, m_sgu_norm, m_w_spatial, m_b_spatial, m_w_out_odd, m_w_ffn_gate, m_w_ffn_up, m_conv_ffn, m_b_conv_ffn, m_w_ffn_down, v_norm_mix, v_norm_ffn, v_final_norm, v_w_in_even, v_conv_a, v_w_pool, v_pool_scale, v_w_out_even, v_w_in_odd, v_sgu_norm, v_w_spatial, v_b_spatial, v_w_out_odd, v_w_ffn_gate, v_w_ffn_up, v_conv_ffn, v_b_conv_ffn, v_w_ffn_down):
    s, d = x.shape[1], x.shape[2]
    x2, target = x[0], loss_target[0]
    tm = min(512, s)
    tm_wide = min(2048, s)
    tn = 256
    row = lambda a: a.reshape(1, -1)
    ein, ro = w_in_even.shape[2], w_out_even.shape[1]
    fs = w_ffn_gate.shape[2]
    fsp = -(-fs // LANE) * LANE
    fp = N_DEV * fsp
    full = lambda shape, dtype=BF16: jax.ShapeDtypeStruct(shape, dtype)

    wg_s = [jnp.pad(w_ffn_gate[l], ((0, 0), (0, fsp - fs))).astype(BF16) for l in range(2)]
    wu_s = [jnp.pad(w_ffn_up[l], ((0, 0), (0, fsp - fs))).astype(BF16) for l in range(2)]
    wd_s = [jnp.pad(w_ffn_down[l], ((0, fsp - fs), (0, 0))).astype(BF16) for l in range(2)]
    small_s = _pack_small(conv_a[0], sgu_norm, conv_ffn, fsp)[None]
    in_spec, out_spec, gu_spec, down_spec = (1, ein), (0, ro), (1, fsp), (0, fsp)
    full_in, full_out, full_gu, full_down = full((d, N_DEV * ein)), full((N_DEV * ro, d)), full((d, fp)), full((fp, d))
    win_e, wout_e, gsmall = _alone(_Gather([w_in_even[0].astype(BF16), w_out_even[0].astype(BF16), small_s], [in_spec, out_spec, (0, 1)],
                                           [full_in, full_out, full((N_DEV, SMALL_ROWS, fsp), F32)]), "gather_mix0")
    ca_full = jnp.moveaxis(gsmall[:, 0:3, 0:conv_a.shape[2]], 0, 1).reshape(3, -1)
    sgu_full = gsmall[:, 8, 0:sgu_norm.shape[1]].reshape(1, -1)
    cf_full = jnp.moveaxis(gsmall[:, 16:22, :].reshape(N_DEV, 2, 3, fsp), 0, 2).reshape(2, 3, fp)
    cb_full = _pad_slots(b_conv_ffn, fs, fsp)

    tril = jnp.tril(jnp.ones((CHUNK, CHUNK), F32))
    ws_m = w_spatial[0] * tril
    ws_b = ws_m.astype(BF16)
    wst_b = jnp.swapaxes(ws_m, 1, 2).astype(BF16)
    bfull = jnp.repeat(b_spatial[0].T, CHUNK, axis=1)
    wpool_b = w_pool[0].astype(BF16)
    wpoolt_b = jnp.swapaxes(w_pool[0], 1, 2).astype(BF16)

    (xn0, proj0, cq0, pooled0, mix0, h1, hn0), (wg0,) = _even_fwd(
        x2, norm_mix[0:1], norm_ffn[0:1], win_e, ca_full, wpool_b, pool_scale, wout_e, tm,
        comm=_Gather([wg_s[0]], [gu_spec], [full_gu]))
    (g0, gc0), (wu0,) = _ffn_gate(hn0, wg0, cf_full[0], cb_full[0:1], tm_wide, tn, "ffn_gate_l0", comm=_Gather([wu_s[0]], [gu_spec], [full_gu]))
    (up0, a0), (wd0,) = _ffn_up(hn0, wu0, gc0, tm_wide, tn, "ffn_up_l0", comm=_Gather([wd_s[0]], [down_spec], [full_down]))
    (h2, xn1), (win_o, wout_o) = _ffn_fwd2(a0, wd0, h1, norm_mix[1:2], tm, "ffn_fwd2_l0", comm=_Gather(
        [w_in_odd[0].astype(BF16), w_out_odd[0].astype(BF16)], [in_spec, out_spec], [full_in, full_out]))
    (pre1, gate1, mixo1, h3, hn1), (wg1,) = _odd_fwd(xn1, h2, win_o, sgu_full, ws_b, bfull, wout_o, norm_ffn[1:2], tm,
                                                    comm=_Gather([wg_s[1]], [gu_spec], [full_gu]))
    (g1, gc1), (wu1,) = _ffn_gate(hn1, wg1, cf_full[1], cb_full[1:2], tm_wide, tn, "ffn_gate_l1", comm=_Gather([wu_s[1]], [gu_spec], [full_gu]))
    (up1, a1), (wd1,) = _ffn_up(hn1, wu1, gc1, tm_wide, tn, "ffn_up_l1", comm=_Gather([wd_s[1]], [down_spec], [full_down]))
    h4, _ = _ffn_fwd2(a1, wd1, h3, row(final_norm), tm, "ffn_fwd2_l1")

    core = lax.axis_index("c").astype(jnp.int32).reshape(1)
    ts = min(1024, s)
    dh4, dh4b, d_final, lossvec = _loss_bwd(h4, row(final_norm), target, tm)
    gd1 = _wgrad(a1, dh4b, tn, s, "wgrad_down_l1")
    (dg1, dup1, dcw1, dcb1), (o_d1,) = _ffn_bwd1(dh4b, g1, gc1, up1, wd1, cf_full[1], tm_wide, tn, "ffn_bwd1_l1",
                                                 comm=_PairExchange([gd1], [down_spec]))
    p_d1 = _pair_add(gd1, o_d1, down_spec, core, "pair_add_down_l1")
    gg1 = _wgrad(hn1, dg1, 512, ts, "wgrad_gate_l1")
    gu1, (o_g1,) = _wgrad(hn1, dup1, 512, ts, "wgrad_up_l1", comm=_PairExchange([gg1], [gu_spec]))
    p_g1 = _pair_add(gg1, o_g1, gu_spec, core, "pair_add_gate_l1")
    jobs = _Jobs(_ChipExchange([p_d1]), _PairExchange([gu1], [gu_spec]))
    (dh3, dh3b, d_nffn1), res = _ffn_bwd2(dg1, dup1, wg1, wu1, h3, norm_ffn[1:2], dh4, tm // 2, "ffn_bwd2_l1", comm=jobs)
    (s_d1,), (o_u1,) = jobs.results(res)
    p_u1 = _pair_add(gu1, o_u1, gu_spec, core, "pair_add_up_l1")
    (dpre1, dh2, dh2b, d_nmix1, d_sgu, d_ws, d_b), (s_g1, s_u1) = _odd_bwd(
        dh3b, dh3, wout_o, pre1, gate1, ws_b, wst_b, sgu_full, win_o, h2, norm_mix[1:2], tm, comm=_ChipExchange([p_g1, p_u1]))
    gi1 = _wgrad(xn1, dpre1, 512, ts, "wgrad_in_odd")
    go1, (o_i1,) = _wgrad(mixo1, dh3b, 512, ts, "wgrad_out_odd", comm=_PairExchange([gi1], [in_spec]))
    p_i1 = _pair_add(gi1, o_i1, in_spec, core, "pair_add_in_odd")
    gd0, (o_o1,) = _wgrad(a0, dh2b, tn, s, "wgrad_down_l0", comm=_PairExchange([go1], [out_spec]))
    p_o1 = _pair_add(go1, o_o1, out_spec, core, "pair_add_out_odd")
    d_early = _rep_early(d_nmix1, d_nffn1, d_final, d_b, _unpad_slots(dcb1, fs, fsp), d_ws * tril, lossvec)
    jobs = _Jobs(_ChipExchange([p_i1, p_o1], [], [d_early]), _PairExchange([gd0], [down_spec]))
    (dg0, dup0, dcw0, dcb0), res = _ffn_bwd1(dh2b, g0, gc0, up0, wd0, cf_full[0], tm_wide, tn, "ffn_bwd1_l0", comm=jobs)
    (s_i1, s_o1, r_early), (o_d0,) = jobs.results(res)
    p_d0 = _pair_add(gd0, o_d0, down_spec, core, "pair_add_down_l0")
    gg0, (s_d0,) = _wgrad(hn0, dg0, 512, ts, "wgrad_gate_l0", comm=_ChipExchange([p_d0]))
    gu0, (o_g0,) = _wgrad(hn0, dup0, 512, ts, "wgrad_up_l0", comm=_PairExchange([gg0], [gu_spec]))
    p_g0 = _pair_add(gg0, o_g0, gu_spec, core, "pair_add_gate_l0")
    jobs = _Jobs(_ChipExchange([p_g0]), _PairExchange([gu0], [gu_spec]))
    (dh1, dh1b, d_nffn0), res = _ffn_bwd2(dg0, dup0, wg0, wu0, h1, norm_ffn[0:1], dh2, tm // 2, "ffn_bwd2_l0", comm=jobs)
    (s_g0,), (o_u0,) = jobs.results(res)
    p_u0 = _pair_add(gu0, o_u0, gu_spec, core, "pair_add_up_l0")
    (dproj0, grad_x, d_nmix0, d_ca, d_wp, d_ps), (s_u0,) = _even_bwd(
        dh1b, dh1, wout_e, proj0, cq0, pooled0, ca_full, wpool_b, wpoolt_b, pool_scale, win_e, x2, norm_mix[0:1], tm, comm=_ChipExchange([p_u0]))
    go0 = _wgrad(mix0, dh1b, 512, ts, "wgrad_out_even")
    gi0, (o_o0,) = _wgrad(xn0, dproj0, 512, ts, "wgrad_in_even", comm=_PairExchange([go0], [out_spec]))
    p_o0 = _pair_add(go0, o_o0, out_spec, core, "pair_add_out_even")
    (o_i0,) = _alone(_PairExchange([gi0], [in_spec]), "pair_exchange_in_even")
    p_i0 = _pair_add(gi0, o_i0, in_spec, core, "pair_add_in_even")
    d_small = jnp.stack([_pack_small(a, b, c, fsp) for a, b, c in zip(
        jnp.moveaxis(d_ca.reshape(3, N_DEV, -1), 1, 0), jnp.moveaxis(d_sgu.reshape(1, N_DEV, -1), 1, 0),
        jnp.moveaxis(jnp.stack([dcw0, dcw1]).reshape(2, 3, N_DEV, fsp), 2, 0))])
    d_late = _rep_late(d_nmix0, d_nffn0, d_ps, _unpad_slots(dcb0, fs, fsp), d_wp)
    s_i0, s_o0, r_small, r_late = _alone(_ChipExchange([p_i0, p_o0], [d_small], [d_late]), "chip_exchange_last")
    loss = jnp.sum(r_early[:, r_early.shape[1] - 8, :])

    out = {}
    out["w_in_even"] = _adamw(s_i0, w_in_even[0], m_w_in_even[0], v_w_in_even[0], 256, "adamw_in_even")
    out["w_out_even"] = _adamw(s_o0, w_out_even[0], m_w_out_even[0], v_w_out_even[0], ro, "adamw_out_even")
    out["w_in_odd"] = _adamw(s_i1, w_in_odd[0], m_w_in_odd[0], v_w_in_odd[0], 256, "adamw_in_odd")
    out["w_out_odd"] = _adamw(s_o1, w_out_odd[0], m_w_out_odd[0], v_w_out_odd[0], ro, "adamw_out_odd")
    for nm, s1, s0, tr, w, m, v in (("w_ffn_gate", s_g1, s_g0, 256, w_ffn_gate, m_w_ffn_gate, v_w_ffn_gate),
                                    ("w_ffn_up", s_u1, s_u0, 256, w_ffn_up, m_w_ffn_up, v_w_ffn_up),
                                    ("w_ffn_down", s_d1, s_d0, fs, w_ffn_down, m_w_ffn_down, v_w_ffn_down)):
        l1 = _adamw(s1, w, m, v, tr, "adamw_%s_l1" % nm, layer=1)
        out[nm] = _adamw(s0, w, m, v, tr, "adamw_%s_l0" % nm, layer=0, into=l1)
    small = _adamw(r_small, _pack_small(conv_a[0], sgu_norm, conv_ffn, fsp), _pack_small(m_conv_a[0], m_sgu_norm, m_conv_ffn, fsp),
                   _pack_small(v_conv_a[0], v_sgu_norm, v_conv_ffn, fsp), SMALL_ROWS, "adamw_small")
    no_loss = jnp.zeros((1, REP_COLS), F32)
    early = _adamw(r_early, *[_rep_early(nm[1:2], nf[1:2], fn, bs, bc[1:2], wsp, no_loss) for nm, nf, fn, bs, bc, wsp in (
        (norm_mix, norm_ffn, final_norm, b_spatial, b_conv_ffn, w_spatial), (m_norm_mix, m_norm_ffn, m_final_norm, m_b_spatial, m_b_conv_ffn, m_w_spatial),
        (v_norm_mix, v_norm_ffn, v_final_norm, v_b_spatial, v_b_conv_ffn, v_w_spatial))], r_early.shape[1], "adamw_replicated_early")
    late = _adamw(r_late, *[_rep_late(nm[0:1], nf[0:1], ps, bc[0:1], wp) for nm, nf, ps, bc, wp in (
        (norm_mix, norm_ffn, pool_scale, b_conv_ffn, w_pool), (m_norm_mix, m_norm_ffn, m_pool_scale, m_b_conv_ffn, m_w_pool),
        (v_norm_mix, v_norm_ffn, v_pool_scale, v_b_conv_ffn, v_w_pool))], r_late.shape[1], "adamw_replicated_late")

    names = ["norm_mix", "norm_ffn", "final_norm", "w_in_even", "conv_a", "w_pool", "pool_scale", "w_out_even", "w_in_odd", "sgu_norm",
             "w_spatial", "b_spatial", "w_out_odd", "w_ffn_gate", "w_ffn_up", "conv_ffn", "b_conv_ffn", "w_ffn_down"]
    like = {"norm_mix": norm_mix, "norm_ffn": norm_ffn, "final_norm": final_norm, "w_in_even": w_in_even, "conv_a": conv_a,
            "w_pool": w_pool, "pool_scale": pool_scale, "w_out_even": w_out_even, "w_in_odd": w_in_odd, "sgu_norm": sgu_norm,
            "w_spatial": w_spatial, "b_spatial": b_spatial, "w_out_odd": w_out_odd, "w_ffn_gate": w_ffn_gate, "w_ffn_up": w_ffn_up,
            "conv_ffn": conv_ffn, "b_conv_ffn": b_conv_ffn, "w_ffn_down": w_ffn_down}
    groups = []
    for k in range(4):
        ca_k, sg_k, cf_k = _unpack_small(small[k], conv_a.shape[2], sgu_norm.shape[1], conv_ffn.shape[2])
        vals = dict(_unpack_rep(late[k], early[k], like))
        vals.update(conv_a=ca_k, sgu_norm=sg_k, conv_ffn=cf_k)
        for nm in ("w_in_even", "w_in_odd", "w_out_even", "w_out_odd", "w_ffn_gate", "w_ffn_up", "w_ffn_down"):
            vals[nm] = out[nm][k]
        groups.append([vals[nm].reshape(like[nm].shape) for nm in names])
    return (loss, grad_x[None], *groups[0], *groups[1], *groups[2], *groups[3])
```

```python
import functools

import jax
import jax.numpy as jnp
from jax import lax
from jax.experimental import pallas as pl
from jax.experimental.pallas import tpu as pltpu

F32, BF16 = jnp.float32, jnp.bfloat16
EPS = 1e-6
WINDOWS = (2, 4, 8, 16)
HALO = 16
CHUNK = 128
N_DEV = 8
N_CHIP = 4
MESH = pl.DeviceIdType.MESH
VMEM_LIMIT = 56 * 2**20
LANE = 128
ADAM_LR, ADAM_B1, ADAM_B2, ADAM_EPS, ADAM_WD, ADAM_STEP = 0.001, 0.9, 0.999, 1e-08, 0.01, 10
INV_SQRT2 = 0.7071067811865476
INV_SQRT2PI = 0.3989422804014327


def _pcall(body, comm=None, **kw):
    if comm is None:
        return pl.pallas_call(body, **kw)
    in_specs, out_specs, out_shape = list(kw.pop("in_specs")), kw.pop("out_specs"), kw.pop("out_shape")
    single = not isinstance(out_shape, (list, tuple))
    out_specs, out_shape = ([out_specs], [out_shape]) if single else (list(out_specs), list(out_shape))
    scratch = list(kw.pop("scratch_shapes", []))
    grid = kw.get("grid", ())
    n_in, n_out, n_scr, c_in, c_out = len(in_specs), len(out_specs), len(scratch), len(comm.ins), len(comm.out_shape)

    def hosted(*refs):
        cuts = [0, n_in, n_in + c_in, n_in + c_in + n_out, n_in + c_in + n_out + c_out, n_in + c_in + n_out + c_out + n_scr, len(refs)]
        ins, cins, outs, couts, scr, sems = (refs[a:b] for a, b in zip(cuts[:-1], cuts[1:]))
        first, last = True, True
        for axis, size in enumerate(grid):
            first = jnp.logical_and(first, pl.program_id(axis) == 0)
            last = jnp.logical_and(last, pl.program_id(axis) == size - 1)
        if grid:
            pl.when(first)(lambda: comm.start(cins, couts, sems))
            body(*ins, *outs, *scr)
            pl.when(last)(lambda: comm.finish(cins, couts, sems))
        else:
            comm.start(cins, couts, sems)
            body(*ins, *outs, *scr)
            comm.finish(cins, couts, sems)

    call = pl.pallas_call(hosted, in_specs=in_specs + [_hbm()] * c_in, out_specs=out_specs + [_hbm()] * c_out,
                          out_shape=out_shape + list(comm.out_shape), scratch_shapes=scratch + list(comm.sems), **kw)

    def run(*args):
        res = call(*args, *comm.ins)
        own = res[0] if single else res[:n_out]
        return own, res[n_out:]

    return run


def _params(*sem):
    return pltpu.CompilerParams(dimension_semantics=sem, vmem_limit_bytes=VMEM_LIMIT)


def _whole(shape):
    return pl.BlockSpec(shape, lambda *_: (0,) * len(shape))


def _resident(shape):
    return pl.BlockSpec(shape, lambda *_: (0,) * len(shape), pipeline_mode=pl.Buffered(1))


def _rows(i, tm):
    return pl.ds(pl.multiple_of(i * tm, tm), tm)


def _nn(a, b):
    return jnp.dot(a, b, preferred_element_type=F32)


def _nt(a, b):
    return lax.dot_general(a, b, (((1,), (1,)), ((), ())), preferred_element_type=F32)


def _tn(a, b):
    return lax.dot_general(a, b, (((0,), (0,)), ((), ())), preferred_element_type=F32)


def _rms(x, gain):
    r = lax.rsqrt(jnp.mean(x * x, axis=-1, keepdims=True) + EPS)
    return x * r * gain


def _rms_bwd(dy, x, gain):
    r = lax.rsqrt(jnp.mean(x * x, axis=-1, keepdims=True) + EPS)
    xh = x * r
    dgain = jnp.sum(dy * xh, axis=0, keepdims=True)
    dxh = dy * gain
    dx = r * (dxh - xh * jnp.mean(dxh * xh, axis=-1, keepdims=True))
    return dx, dgain


def _gelu(x):
    return 0.5 * x * (1.0 + lax.erf(x * INV_SQRT2))


def _gelu_grad(x):
    return 0.5 * (1.0 + lax.erf(x * INV_SQRT2)) + x * jnp.exp(-0.5 * x * x) * INV_SQRT2PI


def _acc(ref, val, first):
    @pl.when(first)
    def _():
        ref[...] = val

    @pl.when(jnp.logical_not(first))
    def _():
        ref[...] += val


def _shift(ext, k, back):
    n = ext.shape[0]
    return pltpu.roll(ext, k if back else n - k, axis=0)


def _window_sum(ext, w, back):
    total, step = ext, 1
    while step < w:
        total = total + _shift(total, step, back)
        step *= 2
    return total


def _counts(row0, tm, w):
    pos1 = (row0 + lax.broadcasted_iota(jnp.int32, (tm, 1), 0) + 1).astype(F32)
    return jnp.minimum(pos1, float(w))


def _even_fwd(x, gmix, gffn, win, conva, wpool, pscale, wout, tm, comm=None):
    s, d = x.shape
    e = win.shape[1]
    aw = e // 4

    def body(x_ref, gmix_ref, gffn_ref, win_ref, ca_ref, wp_ref, ps_ref, wout_ref,
             xn_ref, proj_ref, cq_ref, pooled_ref, mix_ref, h_ref, hn_ref, qbuf, zbuf):
        i = pl.program_id(0)

        @pl.when(i == 0)
        def _():
            qbuf[0:HALO, :] = jnp.zeros((HALO, aw), F32)
            zbuf[0:HALO, :] = jnp.zeros((HALO, aw), F32)

        xv = x_ref[...]
        xn = _rms(xv, gmix_ref[...]).astype(BF16)
        xn_ref[...] = xn
        proj = _nn(xn, win_ref[...])
        proj_ref[...] = proj.astype(BF16)
        a_b, a_c, a_v, z = (proj[:, k * aw:(k + 1) * aw] for k in range(4))
        q = a_c * a_v
        qbuf[HALO:HALO + tm, :] = q
        qext = qbuf[...]
        cur = slice(HALO, HALO + tm)
        cq = ca_ref[2:3, :] * q + ca_ref[1:2, :] * _shift(qext, 1, True)[cur, :] + ca_ref[0:1, :] * _shift(qext, 2, True)[cur, :]
        cq_ref[...] = cq.astype(BF16)
        y_a = a_b * cq
        zbuf[HALO:HALO + tm, :] = z
        zext = zbuf[...]
        ys = []
        for g, w in enumerate(WINDOWS):
            cols = slice(g * LANE, (g + 1) * LANE)
            acc = _window_sum(zext[:, cols], w, True)[cur, :]
            pooled = (acc / _counts(i * tm, tm, w) - z[:, cols]).astype(BF16)
            pooled_ref[:, cols] = pooled
            ys.append(_nn(pooled, wp_ref[g]))
        y_b = jnp.concatenate(ys, axis=1) * ps_ref[...]
        mix = jnp.concatenate([y_a, y_b], axis=1).astype(BF16)
        mix_ref[...] = mix
        h = xv + _nn(mix, wout_ref[...])
        h_ref[...] = h
        hn_ref[...] = _rms(h, gffn_ref[...]).astype(BF16)
        qbuf[0:HALO, :] = qbuf[tm:tm + HALO, :]
        zbuf[0:HALO, :] = zbuf[tm:tm + HALO, :]

    row = lambda c: pl.BlockSpec((tm, c), lambda i: (i, 0))
    return _pcall(
        body, comm=comm, name="even_fwd", grid=(s // tm,),
        in_specs=[row(d), _whole((1, d)), _whole((1, d)), _whole(win.shape), _whole(conva.shape), _whole(wpool.shape),
                  _whole(pscale.shape), _whole(wout.shape)],
        out_specs=[row(d), row(e), row(aw), row(aw), row(d), row(d), row(d)],
        out_shape=[jax.ShapeDtypeStruct((s, d), BF16), jax.ShapeDtypeStruct((s, e), BF16), jax.ShapeDtypeStruct((s, aw), BF16),
                   jax.ShapeDtypeStruct((s, aw), BF16), jax.ShapeDtypeStruct((s, d), BF16), jax.ShapeDtypeStruct((s, d), F32),
                   jax.ShapeDtypeStruct((s, d), BF16)],
        scratch_shapes=[pltpu.VMEM((tm + HALO, aw), F32), pltpu.VMEM((tm + HALO, aw), F32)],
        compiler_params=_params("arbitrary"),
    )(x, gmix, gffn, win, conva, wpool, pscale, wout)


def _ffn_gate(hn, wg, cw, cb, tm, tn, name, comm=None):
    s, d = hn.shape
    f = wg.shape[1]

    def body(hn_ref, wg_ref, cw_ref, cb_ref, g_ref, gc_ref, gbuf):
        i = pl.program_id(1)

        @pl.when(i == 0)
        def _():
            gbuf[0:HALO, :] = jnp.zeros((HALO, tn), F32)

        g = _nn(hn_ref[_rows(i, tm), :], wg_ref[...])
        g_ref[...] = g.astype(BF16)
        gbuf[HALO:HALO + tm, :] = g
        ext = gbuf[...]
        gc = (cw_ref[2:3, :] * g + cw_ref[1:2, :] * pltpu.roll(ext, 1, axis=0)[HALO:HALO + tm, :]
              + cw_ref[0:1, :] * pltpu.roll(ext, 2, axis=0)[HALO:HALO + tm, :] + cb_ref[...])
        gc_ref[...] = gc.astype(BF16)
        gbuf[0:HALO, :] = gbuf[tm:tm + HALO, :]

    tile = pl.BlockSpec((tm, tn), lambda j, i: (i, j))
    wcol = lambda r: pl.BlockSpec((r, tn), lambda j, i: (0, j))
    out = jax.ShapeDtypeStruct((s, f), BF16)
    return _pcall(
        body, comm=comm, name=name, grid=(f // tn, s // tm),
        in_specs=[_resident((s, d)), wcol(d), wcol(3), wcol(1)],
        out_specs=[tile, tile], out_shape=[out, out],
        scratch_shapes=[pltpu.VMEM((tm + HALO, tn), F32)],
        compiler_params=_params("arbitrary", "arbitrary"),
    )(hn, wg, cw, cb)


def _ffn_up(hn, wu, gc, tm, tn, name, comm=None):
    s, d = hn.shape
    f = wu.shape[1]

    def body(hn_ref, wu_ref, gc_ref, up_ref, a_ref):
        up = _nn(hn_ref[_rows(pl.program_id(1), tm), :], wu_ref[...])
        up_ref[...] = up.astype(BF16)
        gc = gc_ref[...].astype(F32)
        a_ref[...] = (gc * jax.nn.sigmoid(gc) * up).astype(BF16)

    tile = pl.BlockSpec((tm, tn), lambda j, i: (i, j))
    out = jax.ShapeDtypeStruct((s, f), BF16)
    return _pcall(
        body, comm=comm, name=name, grid=(f // tn, s // tm),
        in_specs=[_resident((s, d)), pl.BlockSpec((d, tn), lambda j, i: (0, j)), tile],
        out_specs=[tile, tile], out_shape=[out, out],
        compiler_params=_params("arbitrary", "arbitrary"),
    )(hn, wu, gc)


def _ffn_fwd2(a, wd, h, gain, tm, name, comm=None):
    s, d = h.shape
    f = a.shape[1]

    def body(a_ref, wd_ref, h_ref, gain_ref, ho_ref, hn_ref):
        ho = h_ref[...] + _nn(a_ref[...], wd_ref[...])
        ho_ref[...] = ho
        hn_ref[...] = _rms(ho, gain_ref[...]).astype(BF16)

    row = lambda c: pl.BlockSpec((tm, c), lambda i: (i, 0))
    return _pcall(
        body, comm=comm, name=name, grid=(s // tm,),
        in_specs=[row(f), _whole(wd.shape), row(d), _whole((1, d))],
        out_specs=[row(d), row(d)],
        out_shape=[jax.ShapeDtypeStruct((s, d), F32), jax.ShapeDtypeStruct((s, d), BF16)],
        compiler_params=_params("arbitrary"),
    )(a, wd, h, gain)


def _odd_fwd(xn, h, win, sgu, ws, bfull, wout, gffn, tm, comm=None):
    s, d = h.shape
    e = win.shape[1]
    cw = e // 2
    heads = ws.shape[0]

    def body(xn_ref, h_ref, win_ref, sgu_ref, ws_ref, b_ref, wout_ref, gffn_ref,
             pre_ref, gate_ref, mixo_ref, ho_ref, hn_ref, gbuf):
        pre = _nn(xn_ref[...], win_ref[...])
        pre_ref[...] = pre.astype(BF16)
        p = _gelu(pre)
        u, v = p[:, :cw], p[:, cw:]
        vn = _rms(v, sgu_ref[...]).astype(BF16)
        for n in range(tm // CHUNK):
            rows = slice(n * CHUNK, (n + 1) * CHUNK)
            for hd in range(heads):
                cols = slice(hd * CHUNK, (hd + 1) * CHUNK)
                gbuf[rows, cols] = _nn(ws_ref[hd], vn[rows, cols]) + b_ref[:, cols]
        gate = gbuf[...]
        gate_ref[...] = gate.astype(BF16)
        mixo = (u * gate).astype(BF16)
        mixo_ref[...] = mixo
        ho = h_ref[...] + _nn(mixo, wout_ref[...])
        ho_ref[...] = ho
        hn_ref[...] = _rms(ho, gffn_ref[...]).astype(BF16)

    row = lambda c: pl.BlockSpec((tm, c), lambda i: (i, 0))
    return _pcall(
        body, comm=comm, name="odd_fwd", grid=(s // tm,),
        in_specs=[row(d), row(d), _whole(win.shape), _whole(sgu.shape), _whole(ws.shape), _whole(bfull.shape),
                  _whole(wout.shape), _whole((1, d))],
        out_specs=[row(e), row(cw), row(cw), row(d), row(d)],
        out_shape=[jax.ShapeDtypeStruct((s, e), BF16), jax.ShapeDtypeStruct((s, cw), BF16), jax.ShapeDtypeStruct((s, cw), BF16),
                   jax.ShapeDtypeStruct((s, d), F32), jax.ShapeDtypeStruct((s, d), BF16)],
        scratch_shapes=[pltpu.VMEM((tm, cw), F32)],
        compiler_params=_params("arbitrary"),
    )(xn, h, win, sgu, ws, bfull, wout, gffn)


def _loss_bwd(h, gain, target, tm):
    s, d = h.shape

    def body(h_ref, gain_ref, t_ref, dh_ref, dhb_ref, dgain_ref, loss_ref):
        i = pl.program_id(0)
        hv = h_ref[...]
        gain = gain_ref[...]
        err = _rms(hv, gain) - t_ref[...]
        dy = err * (1.0 / d)
        dx, dgain = _rms_bwd(dy, hv, gain)
        dh_ref[...] = dx
        dhb_ref[...] = dx.astype(BF16)
        _acc(dgain_ref, dgain, i == 0)
        _acc(loss_ref, jnp.sum(err * err, axis=0, keepdims=True) * (0.5 / d), i == 0)

    row = pl.BlockSpec((tm, d), lambda i: (i, 0))
    return _pcall(
        body, name="loss_bwd", grid=(s // tm,),
        in_specs=[row, _whole((1, d)), row],
        out_specs=[row, row, _whole((1, d)), _whole((1, d))],
        out_shape=[jax.ShapeDtypeStruct((s, d), F32), jax.ShapeDtypeStruct((s, d), BF16), jax.ShapeDtypeStruct((1, d), F32),
                   jax.ShapeDtypeStruct((1, d), F32)],
        compiler_params=_params("arbitrary"),
    )(h, gain, target)


def _ffn_bwd1(dhb, g, gc, up, wd, cw, tm, tn, name, comm=None):
    s, d = dhb.shape
    f = g.shape[1]
    ni = s // tm

    def body(dh_ref, g_ref, gc_ref, up_ref, wd_ref, cw_ref, dg_ref, dup_ref, dcw_ref, dcb_ref, ebuf):
        i = pl.program_id(1)

        @pl.when(i == 0)
        def _():
            ebuf[tm:tm + HALO, :] = jnp.zeros((HALO, tn), F32)

        da = _nt(dh_ref[_rows(ni - 1 - i, tm), :], wd_ref[...])
        gcv = gc_ref[...].astype(F32)
        sg = jax.nn.sigmoid(gcv)
        silu = gcv * sg
        dup_ref[...] = (da * silu).astype(BF16)
        dgc = da * up_ref[...].astype(F32) * (sg * (1.0 - silu) + silu)
        ebuf[0:tm, :] = dgc
        ext = ebuf[...]
        s1 = pltpu.roll(ext, tm + HALO - 1, axis=0)[0:tm, :]
        s2 = pltpu.roll(ext, tm + HALO - 2, axis=0)[0:tm, :]
        dg_ref[...] = (cw_ref[2:3, :] * dgc + cw_ref[1:2, :] * s1 + cw_ref[0:1, :] * s2).astype(BF16)
        gv = g_ref[...].astype(F32)
        for k, shifted in enumerate((s2, s1, dgc)):
            _acc(dcw_ref.at[k:k + 1, :], jnp.sum(shifted * gv, axis=0, keepdims=True), i == 0)
        _acc(dcb_ref, jnp.sum(dgc, axis=0, keepdims=True), i == 0)
        ebuf[tm:tm + HALO, :] = ebuf[0:HALO, :]

    tile = pl.BlockSpec((tm, tn), lambda j, i: (ni - 1 - i, j))
    wcol = lambda r: pl.BlockSpec((r, tn), lambda j, i: (0, j))
    out = jax.ShapeDtypeStruct((s, f), BF16)
    return _pcall(
        body, comm=comm, name=name, grid=(f // tn, ni),
        in_specs=[_resident((s, d)), tile, tile, tile,
                  pl.BlockSpec((tn, d), lambda j, i: (j, 0)), wcol(3)],
        out_specs=[tile, tile, wcol(3), wcol(1)],
        out_shape=[out, out, jax.ShapeDtypeStruct((3, f), F32), jax.ShapeDtypeStruct((1, f), F32)],
        scratch_shapes=[pltpu.VMEM((tm + HALO, tn), F32)],
        compiler_params=_params("arbitrary", "arbitrary"),
    )(dhb, g, gc, up, wd, cw)


def _ffn_bwd2(dg, dup, wg, wu, h, gain, dh, tm, name, comm=None):
    s, d = h.shape
    f = dg.shape[1]

    def body(dg_ref, dup_ref, wg_ref, wu_ref, h_ref, gain_ref, dh_ref, dho_ref, dhb_ref, dgain_ref):
        dhn = _nt(dg_ref[...], wg_ref[...]) + _nt(dup_ref[...], wu_ref[...])
        dx, dgain = _rms_bwd(dhn, h_ref[...], gain_ref[...])
        dho = dh_ref[...] + dx
        dho_ref[...] = dho
        dhb_ref[...] = dho.astype(BF16)
        _acc(dgain_ref, dgain, pl.program_id(0) == 0)

    row = lambda c: pl.BlockSpec((tm, c), lambda i: (i, 0))
    return _pcall(
        body, comm=comm, name=name, grid=(s // tm,),
        in_specs=[row(f), row(f), _whole(wg.shape), _whole(wu.shape), row(d), _whole((1, d)), row(d)],
        out_specs=[row(d), row(d), _whole((1, d))],
        out_shape=[jax.ShapeDtypeStruct((s, d), F32), jax.ShapeDtypeStruct((s, d), BF16), jax.ShapeDtypeStruct((1, d), F32)],
        compiler_params=_params("arbitrary"),
    )(dg, dup, wg, wu, h, gain, dh)


def _odd_bwd(dhb, dh, wout, pre, gate, ws, wst, sgu, win, h, gmix, tm, comm=None):
    s, d = h.shape
    e = win.shape[1]
    cw = e // 2
    heads = ws.shape[0]
    ni = s // tm

    def body(dhb_ref, dh_ref, wout_ref, pre_ref, gate_ref, ws_ref, wst_ref, sgu_ref, win_ref, h_ref, gmix_ref,
             dpre_ref, dho_ref, dhob_ref, dgain_ref, dsgu_ref, dws_ref, db_ref, vbuf, gacc):
        i = pl.program_id(0)
        first = i == 0
        dmixo = _nt(dhb_ref[...], wout_ref[...])
        pre = pre_ref[...].astype(F32)
        p = _gelu(pre)
        u, v = p[:, :cw], p[:, cw:]
        sgu = sgu_ref[...]
        rv = lax.rsqrt(jnp.mean(v * v, axis=-1, keepdims=True) + EPS)
        vh = v * rv
        vn = (vh * sgu).astype(BF16)
        du = dmixo * gate_ref[...].astype(F32)
        dgate = dmixo * u
        dgate_b = dgate.astype(BF16)
        gsum = dgate[0:CHUNK, :]
        for n in range(1, tm // CHUNK):
            gsum = gsum + dgate[n * CHUNK:(n + 1) * CHUNK, :]
        _acc(gacc, gsum, first)
        for hd in range(heads):
            cols = slice(hd * CHUNK, (hd + 1) * CHUNK)
            dws = None
            for n in range(tm // CHUNK):
                rows = slice(n * CHUNK, (n + 1) * CHUNK)
                vbuf[rows, cols] = _nn(wst_ref[hd], dgate_b[rows, cols])
                part = _nt(dgate_b[rows, cols], vn[rows, cols])
                dws = part if dws is None else dws + part
            _acc(dws_ref.at[hd], dws, first)
        dvn = vbuf[...]
        _acc(dsgu_ref, jnp.sum(dvn * vh, axis=0, keepdims=True), first)
        dvh = dvn * sgu
        dv = rv * (dvh - vh * jnp.mean(dvh * vh, axis=-1, keepdims=True))
        dpre = (jnp.concatenate([du, dv], axis=1) * _gelu_grad(pre)).astype(BF16)
        dpre_ref[...] = dpre
        dx, dgain = _rms_bwd(_nt(dpre, win_ref[...]), h_ref[...], gmix_ref[...])
        dho = dh_ref[...] + dx
        dho_ref[...] = dho
        dhob_ref[...] = dho.astype(BF16)
        _acc(dgain_ref, dgain, first)

        @pl.when(i == ni - 1)
        def _():
            ones = jnp.ones((8, CHUNK), F32)
            for hd in range(heads):
                tot = lax.dot_general(ones, gacc[:, hd * CHUNK:(hd + 1) * CHUNK], (((1,), (1,)), ((), ())),
                                      preferred_element_type=F32, precision=lax.Precision.HIGHEST)
                db_ref[hd:hd + 1, :] = tot[0:1, :]

    row = lambda c: pl.BlockSpec((tm, c), lambda i: (i, 0))
    return _pcall(
        body, comm=comm, name="odd_bwd", grid=(ni,),
        in_specs=[row(d), row(d), _whole(wout.shape), row(e), row(cw), _whole(ws.shape), _whole(wst.shape), _whole(sgu.shape),
                  _whole(win.shape), row(d), _whole((1, d))],
        out_specs=[row(e), row(d), row(d), _whole((1, d)), _whole((1, cw)), _whole(ws.shape), _whole((heads, CHUNK))],
        out_shape=[jax.ShapeDtypeStruct((s, e), BF16), jax.ShapeDtypeStruct((s, d), F32), jax.ShapeDtypeStruct((s, d), BF16),
                   jax.ShapeDtypeStruct((1, d), F32), jax.ShapeDtypeStruct((1, cw), F32), jax.ShapeDtypeStruct(ws.shape, F32),
                   jax.ShapeDtypeStruct((heads, CHUNK), F32)],
        scratch_shapes=[pltpu.VMEM((tm, cw), F32), pltpu.VMEM((CHUNK, cw), F32)],
        compiler_params=_params("arbitrary"),
    )(dhb, dh, wout, pre, gate, ws, wst, sgu, win, h, gmix)


def _even_bwd(dhb, dh, wout, proj, cq, pooled, conva, wpool, wpoolt, pscale, win, x, gmix, tm, comm=None):
    s, d = x.shape
    e = win.shape[1]
    aw = e // 4
    ni = s // tm

    def body(dhb_ref, dh_ref, wout_ref, proj_ref, cq_ref, pooled_ref, ca_ref, wp_ref, wpt_ref, ps_ref, win_ref, x_ref, gmix_ref,
             dproj_ref, dx_ref, dgain_ref, dca_ref, dwp_ref, dps_ref, cbuf, ebuf):
        i = pl.program_id(0)
        first = i == 0

        @pl.when(first)
        def _():
            cbuf[tm:tm + HALO, :] = jnp.zeros((HALO, aw), F32)
            ebuf[tm:tm + HALO, :] = jnp.zeros((HALO, aw), F32)

        dmix = _nt(dhb_ref[...], wout_ref[...])
        dy_a, dy_b = dmix[:, :aw], dmix[:, aw:]
        proj = proj_ref[...].astype(F32)
        a_b, a_c, a_v = (proj[:, k * aw:(k + 1) * aw] for k in range(3))
        da_b = dy_a * cq_ref[...].astype(F32)
        dcq = dy_a * a_b
        cbuf[0:tm, :] = dcq
        cext = cbuf[...]
        s1 = _shift(cext, 1, False)[0:tm, :]
        s2 = _shift(cext, 2, False)[0:tm, :]
        q = a_c * a_v
        for k, shifted in enumerate((s2, s1, dcq)):
            _acc(dca_ref.at[k:k + 1, :], jnp.sum(shifted * q, axis=0, keepdims=True), first)
        dq = ca_ref[2:3, :] * dcq + ca_ref[1:2, :] * s1 + ca_ref[0:1, :] * s2
        da_c = dq * a_v
        da_v = dq * a_c
        dps, dpool = [], []
        for g, w in enumerate(WINDOWS):
            cols = slice(g * LANE, (g + 1) * LANE)
            pooled = pooled_ref[:, cols]
            mixed = _nn(pooled, wp_ref[g])
            dps.append(jnp.sum(dy_b[:, cols] * mixed, axis=0, keepdims=True))
            dmixed = (dy_b[:, cols] * ps_ref[:, cols]).astype(BF16)
            _acc(dwp_ref.at[g], _tn(pooled, dmixed), first)
            dp = _nn(dmixed, wpt_ref[g])
            dpool.append(dp)
            ebuf[0:tm, cols] = dp / _counts((ni - 1 - i) * tm, tm, w)
        _acc(dps_ref, jnp.concatenate(dps, axis=1), first)
        eext = ebuf[...]
        dzs = []
        for g, w in enumerate(WINDOWS):
            cols = slice(g * LANE, (g + 1) * LANE)
            dzs.append(_window_sum(eext[:, cols], w, False)[0:tm, :] - dpool[g])
        dproj = jnp.concatenate([da_b, da_c, da_v] + dzs, axis=1).astype(BF16)
        dproj_ref[...] = dproj
        dx, dgain = _rms_bwd(_nt(dproj, win_ref[...]), x_ref[...], gmix_ref[...])
        dx_ref[...] = dh_ref[...] + dx
        _acc(dgain_ref, dgain, first)
        cbuf[tm:tm + HALO, :] = cbuf[0:HALO, :]
        ebuf[tm:tm + HALO, :] = ebuf[0:HALO, :]

    row = lambda c: pl.BlockSpec((tm, c), lambda i: (ni - 1 - i, 0))
    return _pcall(
        body, comm=comm, name="even_bwd", grid=(ni,),
        in_specs=[row(d), row(d), _whole(wout.shape), row(e), row(aw), row(aw), _whole(conva.shape), _whole(wpool.shape),
                  _whole(wpoolt.shape), _whole(pscale.shape), _whole(win.shape), row(d), _whole((1, d))],
        out_specs=[row(e), row(d), _whole((1, d)), _whole(conva.shape), _whole(wpool.shape), _whole(pscale.shape)],
        out_shape=[jax.ShapeDtypeStruct((s, e), BF16), jax.ShapeDtypeStruct((s, d), F32), jax.ShapeDtypeStruct((1, d), F32),
                   jax.ShapeDtypeStruct(conva.shape, F32), jax.ShapeDtypeStruct(wpool.shape, F32),
                   jax.ShapeDtypeStruct(pscale.shape, F32)],
        scratch_shapes=[pltpu.VMEM((tm + HALO, aw), F32), pltpu.VMEM((tm + HALO, aw), F32)],
        compiler_params=_params("arbitrary"),
    )(dhb, dh, wout, proj, cq, pooled, conva, wpool, wpoolt, pscale, win, x, gmix)


def _wgrad(a, b, tk, ts, name, comm=None):
    s, ka = a.shape
    nb = b.shape[1]
    nt = s // ts

    def body(a_ref, b_ref, o_ref, acc):
        t = pl.program_id(1)
        _acc(acc, _tn(a_ref[...], b_ref[...]), t == 0)

        @pl.when(t == nt - 1)
        def _():
            o_ref[...] = acc[...].astype(BF16)

    return _pcall(
        body, comm=comm, name=name, grid=(ka // tk, nt),
        in_specs=[pl.BlockSpec((ts, tk), lambda k, t: (t, k)), pl.BlockSpec((ts, nb), lambda k, t: (t, 0))],
        out_specs=pl.BlockSpec((tk, nb), lambda k, t: (k, 0)),
        out_shape=jax.ShapeDtypeStruct((ka, nb), BF16),
        scratch_shapes=[pltpu.VMEM((tk, nb), F32)],
        compiler_params=_params("arbitrary", "arbitrary"),
    )(a, b)


def _adamw(parts, w, m, v, tr, name, layer=None, into=None):
    r, c = w.shape[-2:]
    n, rp, cp = parts.shape
    tp = tr if rp == r else rp
    assert rp == r or tr == r

    def body(p_ref, w_ref, m_ref, v_ref, *rest):
        g_ref, d_ref, mo_ref, vo_ref = rest[-4:]
        g = p_ref[0, 0:tr, 0:c].astype(F32)
        for j in range(1, n):
            g = g + p_ref[j, 0:tr, 0:c].astype(F32)
        g_ref[...] = g
        mn = ADAM_B1 * m_ref[...] + (1.0 - ADAM_B1) * g
        vn = ADAM_B2 * v_ref[...] + (1.0 - ADAM_B2) * (g * g)
        mo_ref[...] = mn
        vo_ref[...] = vn
        m_hat = mn / (1.0 - ADAM_B1 ** ADAM_STEP)
        v_hat = vn / (1.0 - ADAM_B2 ** ADAM_STEP)
        d_ref[...] = -ADAM_LR * (m_hat / (jnp.sqrt(v_hat) + ADAM_EPS) + ADAM_WD * w_ref[...])

    if layer is None:
        row = pl.BlockSpec((tr, c), lambda i: (i, 0))
    else:
        row = pl.BlockSpec((None, tr, c), lambda i: (layer, i, 0))
    out = jax.ShapeDtypeStruct(w.shape, F32)
    prev = [] if into is None else list(into)
    return _pcall(
        body, name=name, grid=(r // tr,),
        in_specs=[pl.BlockSpec((n, tp, cp), lambda i: (0, i, 0)), row, row, row] + [pl.BlockSpec(memory_space=pl.ANY)] * len(prev),
        out_specs=[row, row, row, row], out_shape=[out, out, out, out],
        input_output_aliases={4 + k: k for k in range(len(prev))},
        compiler_params=_params("arbitrary"),
    )(parts, w, m, v, *prev)


def _pair_add(grad, other, spec, core, name):
    axis, width = spec
    slot = other.shape[1:]

    def body(core_ref, g_ref, o_ref, out_ref):
        out_ref[...] = (g_ref[...].astype(F32) + o_ref[...].astype(F32)).astype(BF16)

    if axis == 0:
        gspec = pl.BlockSpec(slot, lambda q, core_ref: (2 * q + core_ref[0], 0))
    else:
        gspec = pl.BlockSpec(slot, lambda q, core_ref: (0, 2 * q + core_ref[0]))
    per_chip = pl.BlockSpec((None,) + slot, lambda q, core_ref: (q, 0, 0))
    return _pcall(
        body, name=name,
        grid_spec=pltpu.PrefetchScalarGridSpec(num_scalar_prefetch=1, grid=(N_CHIP,), in_specs=[gspec, per_chip], out_specs=per_chip),
        out_shape=jax.ShapeDtypeStruct(other.shape, BF16),
        compiler_params=_params("arbitrary"),
    )(core, grad, other)


def _hbm():
    return pl.BlockSpec(memory_space=pltpu.HBM)


def _window(ref, spec, j):
    axis, width = spec
    start = pl.multiple_of(j * width, width)
    return ref.at[(slice(None),) * axis + (pl.ds(start, width),)]


def _here():
    return lax.axis_index("x"), lax.axis_index("y"), lax.axis_index("c")


class _Gather:
    def __init__(self, shards, specs, fulls):
        n = len(shards)
        self.ins, self.specs, self.out_shape = list(shards), list(specs), list(fulls)
        self.sems = [pltpu.SemaphoreType.DMA((7 * n,)), pltpu.SemaphoreType.DMA((7 * n,)), pltpu.SemaphoreType.DMA((n,))]

    def _plan(self, ins, outs, sems):
        send_sems, recv_sems, local_sems = sems
        x, y, c = _here()
        me, sibling = (x, y, c), (x, y, 1 - c)
        chips = [(1 - x, y), (x, 1 - y), (1 - x, 1 - y)]

        def slot(t, dev):
            return _window(outs[t], self.specs[t], 4 * dev[0] + 2 * dev[1] + dev[2])

        def copy(t, k, block, to, src=None):
            return pltpu.make_async_remote_copy(
                src_ref=slot(t, block) if src is None else src, dst_ref=slot(t, block),
                send_sem=send_sems.at[7 * t + k], recv_sem=recv_sems.at[7 * t + k], device_id=to, device_id_type=MESH)

        plan = []
        for t in range(len(ins)):
            plan.append(dict(
                mine=pltpu.make_async_copy(ins[t], slot(t, me), local_sems.at[t]),
                first=[copy(t, 0, me, sibling, src=ins[t])] + [copy(t, 1 + j, me, (*q, c), src=ins[t]) for j, q in enumerate(chips)],
                over_ici=[copy(t, 1 + j, (*q, c), me) for j, q in enumerate(chips)],
                passed=[copy(t, 4 + j, (*q, c), sibling) for j, q in enumerate(chips)],
                from_sibling=[copy(t, 0, sibling, me)] + [copy(t, 4 + j, (*q, 1 - c), me) for j, q in enumerate(chips)]))
        return plan

    def start(self, ins, outs, sems):
        for p in self._plan(ins, outs, sems):
            p["mine"].start()
            for cp in p["first"]:
                cp.start()

    def finish(self, ins, outs, sems):
        plan = self._plan(ins, outs, sems)
        for p in plan:
            for arrived, onward in zip(p["over_ici"], p["passed"]):
                arrived.wait_recv()
                onward.start()
        for p in plan:
            for cp in p["from_sibling"]:
                cp.wait_recv()
        for p in plan:
            for cp in p["first"] + p["passed"]:
                cp.wait_send()
            p["mine"].wait()


class _PairExchange:
    def __init__(self, grads, specs):
        n = len(grads)
        self.ins, self.specs = list(grads), list(specs)
        self.out_shape = [jax.ShapeDtypeStruct((N_CHIP,) + a.shape[:sp[0]] + (sp[1],) + a.shape[sp[0] + 1:], a.dtype)
                          for a, sp in zip(grads, specs)]
        self.sems = [pltpu.SemaphoreType.DMA((n,)), pltpu.SemaphoreType.DMA((n,))]

    def start(self, ins, outs, sems):
        send_sems, recv_sems = sems
        x, y, c = _here()
        for t in range(len(ins)):
            for q in range(N_CHIP):
                pltpu.make_async_remote_copy(
                    src_ref=_window(ins[t], self.specs[t], 2 * q + (1 - c)), dst_ref=outs[t].at[q],
                    send_sem=send_sems.at[t], recv_sem=recv_sems.at[t], device_id=(x, y, 1 - c), device_id_type=MESH).start()

    def finish(self, ins, outs, sems):
        send_sems, recv_sems = sems
        x, y, c = _here()
        for t in range(len(ins)):
            every = pltpu.make_async_remote_copy(src_ref=outs[t], dst_ref=outs[t], send_sem=send_sems.at[t],
                                                 recv_sem=recv_sems.at[t], device_id=(x, y, 1 - c), device_id_type=MESH)
            every.wait_send()
            every.wait_recv()


class _ChipExchange:
    def __init__(self, pairs, slotted=(), whole=()):
        self.ins = list(pairs) + list(slotted) + list(whole)
        self.npair, self.nslot = len(pairs), len(pairs) + len(slotted)
        n = len(self.ins)
        self.out_shape = ([jax.ShapeDtypeStruct(a.shape, a.dtype) for a in list(pairs) + list(slotted)]
                          + [jax.ShapeDtypeStruct((N_DEV,) + a.shape, a.dtype) for a in whole])
        self.sems = [pltpu.SemaphoreType.DMA((7 * n,)), pltpu.SemaphoreType.DMA((7 * n,)), pltpu.SemaphoreType.DMA((n,))]

    def _plan(self, ins, outs, sems):
        send_sems, recv_sems, local_sems = sems
        npair, nslot = self.npair, self.nslot
        x, y, c = _here()
        me, chip = 4 * x + 2 * y + c, 2 * x + y
        chips = [(1 - x, y), (x, 1 - y), (1 - x, 1 - y)]
        peers = [(x, y, 1 - c)] + [(*q, c) for q in chips] + [(*q, 1 - c) for q in chips]

        def index(dev):
            return 4 * dev[0] + 2 * dev[1] + dev[2]

        def copy(t, k, arriving):
            peer = peers[k]
            if t < npair:
                src, mine, theirs = ins[t].at[2 * peer[0] + peer[1]], chip, 2 * peer[0] + peer[1]
            else:
                src, mine, theirs = (ins[t].at[index(peer)] if t < nslot else ins[t]), me, index(peer)
            return pltpu.make_async_remote_copy(
                src_ref=src, dst_ref=outs[t].at[theirs if arriving else mine],
                send_sem=send_sems.at[7 * t + k], recv_sem=recv_sems.at[7 * t + k], device_id=peer, device_id_type=MESH)

        own, sent, arriving = [], [], []
        for t in range(len(ins)):
            fan = range(1, 4) if t < npair else range(7)
            if t < npair:
                own.append(pltpu.make_async_copy(ins[t].at[chip], outs[t].at[chip], local_sems.at[t]))
            else:
                own.append(pltpu.make_async_copy(ins[t].at[me] if t < nslot else ins[t], outs[t].at[me], local_sems.at[t]))
            sent += [copy(t, k, False) for k in fan]
            arriving += [copy(t, k, True) for k in fan]
        return own, sent, arriving

    def start(self, ins, outs, sems):
        own, sent, _ = self._plan(ins, outs, sems)
        for cp in own + sent:
            cp.start()

    def finish(self, ins, outs, sems):
        own, sent, arriving = self._plan(ins, outs, sems)
        for cp in arriving:
            cp.wait_recv()
        for cp in sent:
            cp.wait_send()
        for cp in own:
            cp.wait()


class _Jobs:
    def __init__(self, *jobs):
        self.jobs = jobs
        self.ins = [a for j in jobs for a in j.ins]
        self.out_shape = [a for j in jobs for a in j.out_shape]
        self.sems = [a for j in jobs for a in j.sems]

    def _split(self, ins, outs, sems):
        i = o = s = 0
        for j in self.jobs:
            yield j, ins[i:i + len(j.ins)], outs[o:o + len(j.out_shape)], sems[s:s + len(j.sems)]
            i, o, s = i + len(j.ins), o + len(j.out_shape), s + len(j.sems)

    def start(self, ins, outs, sems):
        for j, a, b, c in self._split(ins, outs, sems):
            j.start(a, b, c)

    def finish(self, ins, outs, sems):
        for j, a, b, c in self._split(ins, outs, sems):
            j.finish(a, b, c)

    def results(self, outs):
        return [b for _, _, b, _ in self._split((), outs, ())]


def _alone(job, name):
    return _pcall(lambda: None, comm=job, name=name, in_specs=[], out_specs=[], out_shape=[])()[1]


SMALL_ROWS = 24
REP_COLS = 1024


def _pad_to(a, rows, cols):
    return jnp.pad(a, ((0, rows - a.shape[0]), (0, cols - a.shape[1])))


def _pack_small(conv_a, sgu_norm, conv_ffn, cols):
    return jnp.concatenate([_pad_to(conv_a, 8, cols), _pad_to(sgu_norm, 8, cols),
                            _pad_to(conv_ffn.reshape(-1, conv_ffn.shape[-1]), 8, cols)], axis=0)


def _unpack_small(p, ca_w, sg_w, cf_w):
    return p[0:3, 0:ca_w], p[8:9, 0:sg_w], p[16:22, 0:cf_w].reshape(2, 3, cf_w)


def _tile_rows(rows):
    return -(-rows // 8) * 8


def _pack_rows(parts):
    return jnp.concatenate([_pad_to(a, _tile_rows(a.shape[0]), REP_COLS) for a in parts], axis=0)


def _unpack_rows(p, shapes):
    out, r0 = [], 0
    for r, c in shapes:
        out.append(p[r0:r0 + r, 0:c])
        r0 += _tile_rows(r)
    return out


def _rep_late(norm_mix0, norm_ffn0, pool_scale, b_conv0, w_pool):
    return _pack_rows([norm_mix0, norm_ffn0, pool_scale.reshape(1, -1), _pad_to(b_conv0, 1, 3 * REP_COLS).reshape(3, REP_COLS),
                       w_pool.reshape(-1, REP_COLS)])


def _rep_early(norm_mix1, norm_ffn1, final_norm, b_spatial, b_conv1, w_spatial, loss):
    return _pack_rows([norm_mix1, norm_ffn1, final_norm.reshape(1, -1), b_spatial.reshape(1, -1),
                       _pad_to(b_conv1, 1, 3 * REP_COLS).reshape(3, REP_COLS), w_spatial.reshape(-1, REP_COLS), loss])


def _unpack_rep(late, early, like):
    f = like["b_conv_ffn"].shape[1]
    nm0, nf0, ps, bc0, wp = _unpack_rows(late, [(1, REP_COLS), (1, REP_COLS), (1, like["pool_scale"].shape[1]), (3, REP_COLS),
                                                (like["w_pool"].size // REP_COLS, REP_COLS)])
    nm1, nf1, fin, bs, bc1, wsp, _ = _unpack_rows(early, [(1, REP_COLS)] * 4 + [(3, REP_COLS), (like["w_spatial"].size // REP_COLS, REP_COLS),
                                                          (1, REP_COLS)])
    return {
        "norm_mix": jnp.concatenate([nm0, nm1]), "norm_ffn": jnp.concatenate([nf0, nf1]), "final_norm": fin[0], "pool_scale": ps,
        "b_spatial": bs.reshape(like["b_spatial"].shape),
        "b_conv_ffn": jnp.concatenate([bc0.reshape(1, -1), bc1.reshape(1, -1)])[:, 0:f],
        "w_pool": wp.reshape(like["w_pool"].shape), "w_spatial": wsp.reshape(like["w_spatial"].shape),
    }


def _pad_slots(a, width, padded):
    a = a.reshape(*a.shape[:-1], N_DEV, width)
    a = jnp.pad(a, ((0, 0),) * (a.ndim - 1) + ((0, padded - width),))
    return a.reshape(*a.shape[:-2], N_DEV * padded)


def _unpad_slots(a, width, padded):
    a = a.reshape(*a.shape[:-1], N_DEV, padded)[..., 0:width]
    return a.reshape(*a.shape[:-2], N_DEV * width)


def kernel(x, norm_mix, norm_ffn, final_norm, w_in_even, conv_a, w_pool, pool_scale, w_out_even, w_in_odd, sgu_norm, w_spatial, b_spatial, w_out_odd, w_ffn_gate, w_ffn_up, conv_ffn, b_conv_ffn, w_ffn_down, loss_target, m_norm_mix, m_norm_ffn, m_final_norm, m_w_in_even, m_conv_a, m_w_pool, m_pool_scale, m_w_out_even, m_w_in_odd, m_sgu_norm, m_w_spatial, m_b_spatial, m_w_out_odd, m_w_ffn_gate, m_w_ffn_up, m_conv_ffn, m_b_conv_ffn, m_w_ffn_down, v_norm_mix, v_norm_ffn, v_final_norm, v_w_in_even, v_conv_a, v_w_pool, v_pool_scale, v_w_out_even, v_w_in_odd, v_sgu_norm, v_w_spatial, v_b_spatial, v_w_out_odd, v_w_ffn_gate, v_w_ffn_up, v_conv_ffn, v_b_conv_ffn, v_w_ffn_down):
    s, d = x.shape[1], x.shape[2]
    x2, target = x[0], loss_target[0]
    tm = min(512, s)
    tm_wide = min(2048, s)
    tn = 256
    row = lambda a: a.reshape(1, -1)
    ein, ro = w_in_even.shape[2], w_out_even.shape[1]
    fs = w_ffn_gate.shape[2]
    fsp = -(-fs // LANE) * LANE
    fp = N_DEV * fsp
    full = lambda shape, dtype=BF16: jax.ShapeDtypeStruct(shape, dtype)

    wg_s = [jnp.pad(w_ffn_gate[l], ((0, 0), (0, fsp - fs))).astype(BF16) for l in range(2)]
    wu_s = [jnp.pad(w_ffn_up[l], ((0, 0), (0, fsp - fs))).astype(BF16) for l in range(2)]
    wd_s = [jnp.pad(w_ffn_down[l], ((0, fsp - fs), (0, 0))).astype(BF16) for l in range(2)]
    small_s = _pack_small(conv_a[0], sgu_norm, conv_ffn, fsp)[None]
    in_spec, out_spec, gu_spec, down_spec = (1, ein), (0, ro), (1, fsp), (0, fsp)
    full_in, full_out, full_gu, full_down = full((d, N_DEV * ein)), full((N_DEV * ro, d)), full((d, fp)), full((fp, d))
    win_e, wout_e, gsmall = _alone(_Gather([w_in_even[0].astype(BF16), w_out_even[0].astype(BF16), small_s], [in_spec, out_spec, (0, 1)],
                                           [full_in, full_out, full((N_DEV, SMALL_ROWS, fsp), F32)]), "gather_mix0")
    ca_full = jnp.moveaxis(gsmall[:, 0:3, 0:conv_a.shape[2]], 0, 1).reshape(3, -1)
    sgu_full = gsmall[:, 8, 0:sgu_norm.shape[1]].reshape(1, -1)
    cf_full = jnp.moveaxis(gsmall[:, 16:22, :].reshape(N_DEV, 2, 3, fsp), 0, 2).reshape(2, 3, fp)
    cb_full = _pad_slots(b_conv_ffn, fs, fsp)

    tril = jnp.tril(jnp.ones((CHUNK, CHUNK), F32))
    ws_m = w_spatial[0] * tril
    ws_b = ws_m.astype(BF16)
    wst_b = jnp.swapaxes(ws_m, 1, 2).astype(BF16)
    bfull = jnp.repeat(b_spatial[0].T, CHUNK, axis=1)
    wpool_b = w_pool[0].astype(BF16)
    wpoolt_b = jnp.swapaxes(w_pool[0], 1, 2).astype(BF16)

    (xn0, proj0, cq0, pooled0, mix0, h1, hn0), (wg0,) = _even_fwd(
        x2, norm_mix[0:1], norm_ffn[0:1], win_e, ca_full, wpool_b, pool_scale, wout_e, tm,
        comm=_Gather([wg_s[0]], [gu_spec], [full_gu]))
    (g0, gc0), (wu0,) = _ffn_gate(hn0, wg0, cf_full[0], cb_full[0:1], tm_wide, tn, "ffn_gate_l0", comm=_Gather([wu_s[0]], [gu_spec], [full_gu]))
    (up0, a0), (wd0,) = _ffn_up(hn0, wu0, gc0, tm_wide, tn, "ffn_up_l0", comm=_Gather([wd_s[0]], [down_spec], [full_down]))
    (h2, xn1), (win_o, wout_o) = _ffn_fwd2(a0, wd0, h1, norm_mix[1:2], tm, "ffn_fwd2_l0", comm=_Gather(
        [w_in_odd[0].astype(BF16), w_out_odd[0].astype(BF16)], [in_spec, out_spec], [full_in, full_out]))
    (pre1, gate1, mixo1, h3, hn1), (wg1,) = _odd_fwd(xn1, h2, win_o, sgu_full, ws_b, bfull, wout_o, norm_ffn[1:2], tm,
                                                    comm=_Gather([wg_s[1]], [gu_spec], [full_gu]))
    (g1, gc1), (wu1,) = _ffn_gate(hn1, wg1, cf_full[1], cb_full[1:2], tm_wide, tn, "ffn_gate_l1", comm=_Gather([wu_s[1]], [gu_spec], [full_gu]))
    (up1, a1), (wd1,) = _ffn_up(hn1, wu1, gc1, tm_wide, tn, "ffn_up_l1", comm=_Gather([wd_s[1]], [down_spec], [full_down]))
    h4, _ = _ffn_fwd2(a1, wd1, h3, row(final_norm), tm, "ffn_fwd2_l1")

    core = lax.axis_index("c").astype(jnp.int32).reshape(1)
    ts = min(1024, s)
    dh4, dh4b, d_final, lossvec = _loss_bwd(h4, row(final_norm), target, tm)
    gd1 = _wgrad(a1, dh4b, tn, s, "wgrad_down_l1")
    (dg1, dup1, dcw1, dcb1), (o_d1,) = _ffn_bwd1(dh4b, g1, gc1, up1, wd1, cf_full[1], tm_wide, tn, "ffn_bwd1_l1",
                                                 comm=_PairExchange([gd1], [down_spec]))
    p_d1 = _pair_add(gd1, o_d1, down_spec, core, "pair_add_down_l1")
    gg1 = _wgrad(hn1, dg1, 512, ts, "wgrad_gate_l1")
    gu1, (o_g1,) = _wgrad(hn1, dup1, 512, ts, "wgrad_up_l1", comm=_PairExchange([gg1], [gu_spec]))
    p_g1 = _pair_add(gg1, o_g1, gu_spec, core, "pair_add_gate_l1")
    jobs = _Jobs(_ChipExchange([p_d1]), _PairExchange([gu1], [gu_spec]))
    (dh3, dh3b, d_nffn1), res = _ffn_bwd2(dg1, dup1, wg1, wu1, h3, norm_ffn[1:2], dh4, tm // 2, "ffn_bwd2_l1", comm=jobs)
    (s_d1,), (o_u1,) = jobs.results(res)
    p_u1 = _pair_add(gu1, o_u1, gu_spec, core, "pair_add_up_l1")
    (dpre1, dh2, dh2b, d_nmix1, d_sgu, d_ws, d_b), (s_g1, s_u1) = _odd_bwd(
        dh3b, dh3, wout_o, pre1, gate1, ws_b, wst_b, sgu_full, win_o, h2, norm_mix[1:2], tm, comm=_ChipExchange([p_g1, p_u1]))
    gi1 = _wgrad(xn1, dpre1, 512, ts, "wgrad_in_odd")
    go1, (o_i1,) = _wgrad(mixo1, dh3b, 512, ts, "wgrad_out_odd", comm=_PairExchange([gi1], [in_spec]))
    p_i1 = _pair_add(gi1, o_i1, in_spec, core, "pair_add_in_odd")
    gd0, (o_o1,) = _wgrad(a0, dh2b, tn, s, "wgrad_down_l0", comm=_PairExchange([go1], [out_spec]))
    p_o1 = _pair_add(go1, o_o1, out_spec, core, "pair_add_out_odd")
    d_early = _rep_early(d_nmix1, d_nffn1, d_final, d_b, _unpad_slots(dcb1, fs, fsp), d_ws * tril, lossvec)
    jobs = _Jobs(_ChipExchange([p_i1, p_o1], [], [d_early]), _PairExchange([gd0], [down_spec]))
    (dg0, dup0, dcw0, dcb0), res = _ffn_bwd1(dh2b, g0, gc0, up0, wd0, cf_full[0], tm_wide, tn, "ffn_bwd1_l0", comm=jobs)
    (s_i1, s_o1, r_early), (o_d0,) = jobs.results(res)
    p_d0 = _pair_add(gd0, o_d0, down_spec, core, "pair_add_down_l0")
    gg0, (s_d0,) = _wgrad(hn0, dg0, 512, ts, "wgrad_gate_l0", comm=_ChipExchange([p_d0]))
    gu0, (o_g0,) = _wgrad(hn0, dup0, 512, ts, "wgrad_up_l0", comm=_PairExchange([gg0], [gu_spec]))
    p_g0 = _pair_add(gg0, o_g0, gu_spec, core, "pair_add_gate_l0")
    jobs = _Jobs(_ChipExchange([p_g0]), _PairExchange([gu0], [gu_spec]))
    (dh1, dh1b, d_nffn0), res = _ffn_bwd2(dg0, dup0, wg0, wu0, h1, norm_ffn[0:1], dh2, tm // 2, "ffn_bwd2_l0", comm=jobs)
    (s_g0,), (o_u0,) = jobs.results(res)
    p_u0 = _pair_add(gu0, o_u0, gu_spec, core, "pair_add_up_l0")
    (dproj0, grad_x, d_nmix0, d_ca, d_wp, d_ps), (s_u0,) = _even_bwd(
        dh1b, dh1, wout_e, proj0, cq0, pooled0, ca_full, wpool_b, wpoolt_b, pool_scale, win_e, x2, norm_mix[0:1], tm, comm=_ChipExchange([p_u0]))
    go0 = _wgrad(mix0, dh1b, 512, ts, "wgrad_out_even")
    gi0, (o_o0,) = _wgrad(xn0, dproj0, 512, ts, "wgrad_in_even", comm=_PairExchange([go0], [out_spec]))
    p_o0 = _pair_add(go0, o_o0, out_spec, core, "pair_add_out_even")
    (o_i0,) = _alone(_PairExchange([gi0], [in_spec]), "pair_exchange_in_even")
    p_i0 = _pair_add(gi0, o_i0, in_spec, core, "pair_add_in_even")
    d_small = jnp.stack([_pack_small(a, b, c, fsp) for a, b, c in zip(
        jnp.moveaxis(d_ca.reshape(3, N_DEV, -1), 1, 0), jnp.moveaxis(d_sgu.reshape(1, N_DEV, -1), 1, 0),
        jnp.moveaxis(jnp.stack([dcw0, dcw1]).reshape(2, 3, N_DEV, fsp), 2, 0))])
    d_late = _rep_late(d_nmix0, d_nffn0, d_ps, _unpad_slots(dcb0, fs, fsp), d_wp)
    s_i0, s_o0, r_small, r_late = _alone(_ChipExchange([p_i0, p_o0], [d_small], [d_late]), "chip_exchange_last")
    loss = jnp.sum(r_early[:, r_early.shape[1] - 8, :])

    out = {}
    out["w_in_even"] = _adamw(s_i0, w_in_even[0], m_w_in_even[0], v_w_in_even[0], 256, "adamw_in_even")
    out["w_out_even"] = _adamw(s_o0, w_out_even[0], m_w_out_even[0], v_w_out_even[0], ro, "adamw_out_even")
    out["w_in_odd"] = _adamw(s_i1, w_in_odd[0], m_w_in_odd[0], v_w_in_odd[0], 256, "adamw_in_odd")
    out["w_out_odd"] = _adamw(s_o1, w_out_odd[0], m_w_out_odd[0], v_w_out_odd[0], ro, "adamw_out_odd")
    for nm, s1, s0, tr, w, m, v in (("w_ffn_gate", s_g1, s_g0, 256, w_ffn_gate, m_w_ffn_gate, v_w_ffn_gate),
                                    ("w_ffn_up", s_u1, s_u0, 256, w_ffn_up, m_w_ffn_up, v_w_ffn_up),
                                    ("w_ffn_down", s_d1, s_d0, fs, w_ffn_down, m_w_ffn_down, v_w_ffn_down)):
        l1 = _adamw(s1, w, m, v, tr, "adamw_%s_l1" % nm, layer=1)
        out[nm] = _adamw(s0, w, m, v, tr, "adamw_%s_l0" % nm, layer=0, into=l1)
    small = _adamw(r_small, _pack_small(conv_a[0], sgu_norm, conv_ffn, fsp), _pack_small(m_conv_a[0], m_sgu_norm, m_conv_ffn, fsp),
                   _pack_small(v_conv_a[0], v_sgu_norm, v_conv_ffn, fsp), SMALL_ROWS, "adamw_small")
    no_loss = jnp.zeros((1, REP_COLS), F32)
    early = _adamw(r_early, *[_rep_early(nm[1:2], nf[1:2], fn, bs, bc[1:2], wsp, no_loss) for nm, nf, fn, bs, bc, wsp in (
        (norm_mix, norm_ffn, final_norm, b_spatial, b_conv_ffn, w_spatial), (m_norm_mix, m_norm_ffn, m_final_norm, m_b_spatial, m_b_conv_ffn, m_w_spatial),
        (v_norm_mix, v_norm_ffn, v_final_norm, v_b_spatial, v_b_conv_ffn, v_w_spatial))], r_early.shape[1], "adamw_replicated_early")
    late = _adamw(r_late, *[_rep_late(nm[0:1], nf[0:1], ps, bc[0:1], wp) for nm, nf, ps, bc, wp in (
        (norm_mix, norm_ffn, pool_scale, b_conv_ffn, w_pool), (m_norm_mix, m_norm_ffn, m_pool_scale, m_b_conv_ffn, m_w_pool),
        (v_norm_mix, v_norm_ffn, v_pool_scale, v_b_conv_ffn, v_w_pool))], r_late.shape[1], "adamw_replicated_late")

    names = ["norm_mix", "norm_ffn", "final_norm", "w_in_even", "conv_a", "w_pool", "pool_scale", "w_out_even", "w_in_odd", "sgu_norm",
             "w_spatial", "b_spatial", "w_out_odd", "w_ffn_gate", "w_ffn_up", "conv_ffn", "b_conv_ffn", "w_ffn_down"]
    like = {"norm_mix": norm_mix, "norm_ffn": norm_ffn, "final_norm": final_norm, "w_in_even": w_in_even, "conv_a": conv_a,
            "w_pool": w_pool, "pool_scale": pool_scale, "w_out_even": w_out_even, "w_in_odd": w_in_odd, "sgu_norm": sgu_norm,
            "w_spatial": w_spatial, "b_spatial": b_spatial, "w_out_odd": w_out_odd, "w_ffn_gate": w_ffn_gate, "w_ffn_up": w_ffn_up,
            "conv_ffn": conv_ffn, "b_conv_ffn": b_conv_ffn, "w_ffn_down": w_ffn_down}
    groups = []
    for k in range(4):
        ca_k, sg_k, cf_k = _unpack_small(small[k], conv_a.shape[2], sgu_norm.shape[1], conv_ffn.shape[2])
        vals = dict(_unpack_rep(late[k], early[k], like))
        vals.update(conv_a=ca_k, sgu_norm=sg_k, conv_ffn=cf_k)
        for nm in ("w_in_even", "w_in_odd", "w_out_even", "w_out_odd", "w_ffn_gate", "w_ffn_up", "w_ffn_down"):
            vals[nm] = out[nm][k]
        groups.append([vals[nm].reshape(like[nm].shape) for nm in names])
    return (loss, grad_x[None], *groups[0], *groups[1], *groups[2], *groups[3])
```

```python
import functools

import jax
import jax.numpy as jnp
from jax import lax
from jax.experimental import pallas as pl
from jax.experimental.pallas import tpu as pltpu

F32, BF16 = jnp.float32, jnp.bfloat16
EPS = 1e-6
WINDOWS = (2, 4, 8, 16)
HALO = 16
CHUNK = 128
N_DEV = 8
N_CHIP = 4
MESH = pl.DeviceIdType.MESH
VMEM_LIMIT = 56 * 2**20
LANE = 128
ADAM_LR, ADAM_B1, ADAM_B2, ADAM_EPS, ADAM_WD, ADAM_STEP = 0.001, 0.9, 0.999, 1e-08, 0.01, 10
INV_SQRT2 = 0.7071067811865476
INV_SQRT2PI = 0.3989422804014327


def _pcall(body, comm=None, **kw):
    if comm is None:
        return pl.pallas_call(body, **kw)
    in_specs, out_specs, out_shape = list(kw.pop("in_specs")), kw.pop("out_specs"), kw.pop("out_shape")
    single = not isinstance(out_shape, (list, tuple))
    out_specs, out_shape = ([out_specs], [out_shape]) if single else (list(out_specs), list(out_shape))
    scratch = list(kw.pop("scratch_shapes", []))
    grid = kw.get("grid", ())
    n_in, n_out, n_scr, c_in, c_out = len(in_specs), len(out_specs), len(scratch), len(comm.ins), len(comm.out_shape)

    def hosted(*refs):
        cuts = [0, n_in, n_in + c_in, n_in + c_in + n_out, n_in + c_in + n_out + c_out, n_in + c_in + n_out + c_out + n_scr, len(refs)]
        ins, cins, outs, couts, scr, sems = (refs[a:b] for a, b in zip(cuts[:-1], cuts[1:]))
        first, last = True, True
        for axis, size in enumerate(grid):
            first = jnp.logical_and(first, pl.program_id(axis) == 0)
            last = jnp.logical_and(last, pl.program_id(axis) == size - 1)
        if grid:
            pl.when(first)(lambda: comm.start(cins, couts, sems))
            body(*ins, *outs, *scr)
            pl.when(last)(lambda: comm.finish(cins, couts, sems))
        else:
            comm.start(cins, couts, sems)
            body(*ins, *outs, *scr)
            comm.finish(cins, couts, sems)

    call = pl.pallas_call(hosted, in_specs=in_specs + [_hbm()] * c_in, out_specs=out_specs + [_hbm()] * c_out,
                          out_shape=out_shape + list(comm.out_shape), scratch_shapes=scratch + list(comm.sems), **kw)

    def run(*args):
        res = call(*args, *comm.ins)
        own = res[0] if single else res[:n_out]
        return own, res[n_out:]

    return run


def _params(*sem):
    return pltpu.CompilerParams(dimension_semantics=sem, vmem_limit_bytes=VMEM_LIMIT)


def _whole(shape):
    return pl.BlockSpec(shape, lambda *_: (0,) * len(shape))


def _resident(shape):
    return pl.BlockSpec(shape, lambda *_: (0,) * len(shape), pipeline_mode=pl.Buffered(1))


def _rows(i, tm):
    return pl.ds(pl.multiple_of(i * tm, tm), tm)


def _nn(a, b):
    return jnp.dot(a, b, preferred_element_type=F32)


def _nt(a, b):
    return lax.dot_general(a, b, (((1,), (1,)), ((), ())), preferred_element_type=F32)


def _tn(a, b):
    return lax.dot_general(a, b, (((0,), (0,)), ((), ())), preferred_element_type=F32)


def _rms(x, gain):
    r = lax.rsqrt(jnp.mean(x * x, axis=-1, keepdims=True) + EPS)
    return x * r * gain


def _rms_bwd(dy, x, gain):
    r = lax.rsqrt(jnp.mean(x * x, axis=-1, keepdims=True) + EPS)
    xh = x * r
    dgain = jnp.sum(dy * xh, axis=0, keepdims=True)
    dxh = dy * gain
    dx = r * (dxh - xh * jnp.mean(dxh * xh, axis=-1, keepdims=True))
    return dx, dgain


def _gelu(x):
    return 0.5 * x * (1.0 + lax.erf(x * INV_SQRT2))


def _gelu_grad(x):
    return 0.5 * (1.0 + lax.erf(x * INV_SQRT2)) + x * jnp.exp(-0.5 * x * x) * INV_SQRT2PI


def _acc(ref, val, first):
    @pl.when(first)
    def _():
        ref[...] = val

    @pl.when(jnp.logical_not(first))
    def _():
        ref[...] += val


def _shift(ext, k, back):
    n = ext.shape[0]
    return pltpu.roll(ext, k if back else n - k, axis=0)


def _window_sum(ext, w, back):
    total, step = ext, 1
    while step < w:
        total = total + _shift(total, step, back)
        step *= 2
    return total


def _counts(row0, tm, w):
    pos1 = (row0 + lax.broadcasted_iota(jnp.int32, (tm, 1), 0) + 1).astype(F32)
    return jnp.minimum(pos1, float(w))


def _even_fwd(x, gmix, gffn, win, conva, wpool, pscale, wout, tm, comm=None):
    s, d = x.shape
    e = win.shape[1]
    aw = e // 4

    def body(x_ref, gmix_ref, gffn_ref, win_ref, ca_ref, wp_ref, ps_ref, wout_ref,
             xn_ref, proj_ref, cq_ref, pooled_ref, mix_ref, h_ref, hn_ref, qbuf, zbuf):
        i = pl.program_id(0)

        @pl.when(i == 0)
        def _():
            qbuf[0:HALO, :] = jnp.zeros((HALO, aw), F32)
            zbuf[0:HALO, :] = jnp.zeros((HALO, aw), F32)

        xv = x_ref[...]
        xn = _rms(xv, gmix_ref[...]).astype(BF16)
        xn_ref[...] = xn
        proj = _nn(xn, win_ref[...])
        proj_ref[...] = proj.astype(BF16)
        a_b, a_c, a_v, z = (proj[:, k * aw:(k + 1) * aw] for k in range(4))
        q = a_c * a_v
        qbuf[HALO:HALO + tm, :] = q
        qext = qbuf[...]
        cur = slice(HALO, HALO + tm)
        cq = ca_ref[2:3, :] * q + ca_ref[1:2, :] * _shift(qext, 1, True)[cur, :] + ca_ref[0:1, :] * _shift(qext, 2, True)[cur, :]
        cq_ref[...] = cq.astype(BF16)
        y_a = a_b * cq
        zbuf[HALO:HALO + tm, :] = z
        zext = zbuf[...]
        ys = []
        for g, w in enumerate(WINDOWS):
            cols = slice(g * LANE, (g + 1) * LANE)
            acc = _window_sum(zext[:, cols], w, True)[cur, :]
            pooled = (acc / _counts(i * tm, tm, w) - z[:, cols]).astype(BF16)
            pooled_ref[:, cols] = pooled
            ys.append(_nn(pooled, wp_ref[g]))
        y_b = jnp.concatenate(ys, axis=1) * ps_ref[...]
        mix = jnp.concatenate([y_a, y_b], axis=1).astype(BF16)
        mix_ref[...] = mix
        h = xv + _nn(mix, wout_ref[...])
        h_ref[...] = h
        hn_ref[...] = _rms(h, gffn_ref[...]).astype(BF16)
        qbuf[0:HALO, :] = qbuf[tm:tm + HALO, :]
        zbuf[0:HALO, :] = zbuf[tm:tm + HALO, :]

    row = lambda c: pl.BlockSpec((tm, c), lambda i: (i, 0))
    return _pcall(
        body, comm=comm, name="even_fwd", grid=(s // tm,),
        in_specs=[row(d), _whole((1, d)), _whole((1, d)), _whole(win.shape), _whole(conva.shape), _whole(wpool.shape),
                  _whole(pscale.shape), _whole(wout.shape)],
        out_specs=[row(d), row(e), row(aw), row(aw), row(d), row(d), row(d)],
        out_shape=[jax.ShapeDtypeStruct((s, d), BF16), jax.ShapeDtypeStruct((s, e), BF16), jax.ShapeDtypeStruct((s, aw), BF16),
                   jax.ShapeDtypeStruct((s, aw), BF16), jax.ShapeDtypeStruct((s, d), BF16), jax.ShapeDtypeStruct((s, d), F32),
                   jax.ShapeDtypeStruct((s, d), BF16)],
        scratch_shapes=[pltpu.VMEM((tm + HALO, aw), F32), pltpu.VMEM((tm + HALO, aw), F32)],
        compiler_params=_params("arbitrary"),
    )(x, gmix, gffn, win, conva, wpool, pscale, wout)


def _ffn_gate(hn, wgt, cw, cb, tm, tn, name, comm=None):
    s, d = hn.shape
    f = wgt.shape[0]

    def body(hn_ref, wg_ref, cw_ref, cb_ref, g_ref, gc_ref, gbuf):
        i = pl.program_id(1)

        @pl.when(i == 0)
        def _():
            gbuf[0:HALO, :] = jnp.zeros((HALO, tn), F32)

        g = _nt(hn_ref[_rows(i, tm), :], wg_ref[...])
        g_ref[...] = g.astype(BF16)
        gbuf[HALO:HALO + tm, :] = g
        ext = gbuf[...]
        gc = (cw_ref[2:3, :] * g + cw_ref[1:2, :] * pltpu.roll(ext, 1, axis=0)[HALO:HALO + tm, :]
              + cw_ref[0:1, :] * pltpu.roll(ext, 2, axis=0)[HALO:HALO + tm, :] + cb_ref[...])
        gc_ref[...] = gc.astype(BF16)
        gbuf[0:HALO, :] = gbuf[tm:tm + HALO, :]

    tile = pl.BlockSpec((tm, tn), lambda j, i: (i, j))
    wcol = lambda r: pl.BlockSpec((r, tn), lambda j, i: (0, j))
    out = jax.ShapeDtypeStruct((s, f), BF16)
    return _pcall(
        body, comm=comm, name=name, grid=(f // tn, s // tm),
        in_specs=[_resident((s, d)), pl.BlockSpec((tn, d), lambda j, i: (j, 0)), wcol(3), wcol(1)],
        out_specs=[tile, tile], out_shape=[out, out],
        scratch_shapes=[pltpu.VMEM((tm + HALO, tn), F32)],
        compiler_params=_params("arbitrary", "arbitrary"),
    )(hn, wgt, cw, cb)


def _ffn_up(hn, wut, gc, tm, tn, name, comm=None):
    s, d = hn.shape
    f = wut.shape[0]

    def body(hn_ref, wu_ref, gc_ref, up_ref, a_ref):
        up = _nt(hn_ref[_rows(pl.program_id(1), tm), :], wu_ref[...])
        up_ref[...] = up.astype(BF16)
        gc = gc_ref[...].astype(F32)
        a_ref[...] = (gc * jax.nn.sigmoid(gc) * up).astype(BF16)

    tile = pl.BlockSpec((tm, tn), lambda j, i: (i, j))
    out = jax.ShapeDtypeStruct((s, f), BF16)
    return _pcall(
        body, comm=comm, name=name, grid=(f // tn, s // tm),
        in_specs=[_resident((s, d)), pl.BlockSpec((tn, d), lambda j, i: (j, 0)), tile],
        out_specs=[tile, tile], out_shape=[out, out],
        compiler_params=_params("arbitrary", "arbitrary"),
    )(hn, wut, gc)


def _ffn_fwd2(a, wd, h, gain, tm, name, comm=None):
    s, d = h.shape
    f = a.shape[1]

    def body(a_ref, wd_ref, h_ref, gain_ref, ho_ref, hn_ref):
        ho = h_ref[...] + _nn(a_ref[...], wd_ref[...])
        ho_ref[...] = ho
        hn_ref[...] = _rms(ho, gain_ref[...]).astype(BF16)

    row = lambda c: pl.BlockSpec((tm, c), lambda i: (i, 0))
    return _pcall(
        body, comm=comm, name=name, grid=(s // tm,),
        in_specs=[row(f), _whole(wd.shape), row(d), _whole((1, d))],
        out_specs=[row(d), row(d)],
        out_shape=[jax.ShapeDtypeStruct((s, d), F32), jax.ShapeDtypeStruct((s, d), BF16)],
        compiler_params=_params("arbitrary"),
    )(a, wd, h, gain)


def _odd_fwd(xn, h, win, sgu, ws, bfull, wout, gffn, tm, comm=None):
    s, d = h.shape
    e = win.shape[1]
    cw = e // 2
    heads = ws.shape[0]

    def body(xn_ref, h_ref, win_ref, sgu_ref, ws_ref, b_ref, wout_ref, gffn_ref,
             pre_ref, gate_ref, mixo_ref, ho_ref, hn_ref, gbuf):
        pre = _nn(xn_ref[...], win_ref[...])
        pre_ref[...] = pre.astype(BF16)
        p = _gelu(pre)
        u, v = p[:, :cw], p[:, cw:]
        vn = _rms(v, sgu_ref[...]).astype(BF16)
        for n in range(tm // CHUNK):
            rows = slice(n * CHUNK, (n + 1) * CHUNK)
            for hd in range(heads):
                cols = slice(hd * CHUNK, (hd + 1) * CHUNK)
                gbuf[rows, cols] = _nn(ws_ref[hd], vn[rows, cols]) + b_ref[:, cols]
        gate = gbuf[...]
        gate_ref[...] = gate.astype(BF16)
        mixo = (u * gate).astype(BF16)
        mixo_ref[...] = mixo
        ho = h_ref[...] + _nn(mixo, wout_ref[...])
        ho_ref[...] = ho
        hn_ref[...] = _rms(ho, gffn_ref[...]).astype(BF16)

    row = lambda c: pl.BlockSpec((tm, c), lambda i: (i, 0))
    return _pcall(
        body, comm=comm, name="odd_fwd", grid=(s // tm,),
        in_specs=[row(d), row(d), _whole(win.shape), _whole(sgu.shape), _whole(ws.shape), _whole(bfull.shape),
                  _whole(wout.shape), _whole((1, d))],
        out_specs=[row(e), row(cw), row(cw), row(d), row(d)],
        out_shape=[jax.ShapeDtypeStruct((s, e), BF16), jax.ShapeDtypeStruct((s, cw), BF16), jax.ShapeDtypeStruct((s, cw), BF16),
                   jax.ShapeDtypeStruct((s, d), F32), jax.ShapeDtypeStruct((s, d), BF16)],
        scratch_shapes=[pltpu.VMEM((tm, cw), F32)],
        compiler_params=_params("arbitrary"),
    )(xn, h, win, sgu, ws, bfull, wout, gffn)


def _loss_bwd(h, gain, target, tm):
    s, d = h.shape

    def body(h_ref, gain_ref, t_ref, dh_ref, dhb_ref, dgain_ref, loss_ref):
        i = pl.program_id(0)
        hv = h_ref[...]
        gain = gain_ref[...]
        err = _rms(hv, gain) - t_ref[...]
        dy = err * (1.0 / d)
        dx, dgain = _rms_bwd(dy, hv, gain)
        dh_ref[...] = dx
        dhb_ref[...] = dx.astype(BF16)
        _acc(dgain_ref, dgain, i == 0)
        _acc(loss_ref, jnp.sum(err * err, axis=0, keepdims=True) * (0.5 / d), i == 0)

    row = pl.BlockSpec((tm, d), lambda i: (i, 0))
    return _pcall(
        body, name="loss_bwd", grid=(s // tm,),
        in_specs=[row, _whole((1, d)), row],
        out_specs=[row, row, _whole((1, d)), _whole((1, d))],
        out_shape=[jax.ShapeDtypeStruct((s, d), F32), jax.ShapeDtypeStruct((s, d), BF16), jax.ShapeDtypeStruct((1, d), F32),
                   jax.ShapeDtypeStruct((1, d), F32)],
        compiler_params=_params("arbitrary"),
    )(h, gain, target)


def _ffn_bwd1(dhb, g, gc, up, wd, cw, tm, tn, name, comm=None):
    s, d = dhb.shape
    f = g.shape[1]
    ni = s // tm

    def body(dh_ref, g_ref, gc_ref, up_ref, wd_ref, cw_ref, dg_ref, dup_ref, dcw_ref, dcb_ref, ebuf):
        i = pl.program_id(1)

        @pl.when(i == 0)
        def _():
            ebuf[tm:tm + HALO, :] = jnp.zeros((HALO, tn), F32)

        da = _nt(dh_ref[_rows(ni - 1 - i, tm), :], wd_ref[...])
        gcv = gc_ref[...].astype(F32)
        sg = jax.nn.sigmoid(gcv)
        silu = gcv * sg
        dup_ref[...] = (da * silu).astype(BF16)
        dgc = da * up_ref[...].astype(F32) * (sg * (1.0 - silu) + silu)
        ebuf[0:tm, :] = dgc
        ext = ebuf[...]
        s1 = pltpu.roll(ext, tm + HALO - 1, axis=0)[0:tm, :]
        s2 = pltpu.roll(ext, tm + HALO - 2, axis=0)[0:tm, :]
        dg_ref[...] = (cw_ref[2:3, :] * dgc + cw_ref[1:2, :] * s1 + cw_ref[0:1, :] * s2).astype(BF16)
        gv = g_ref[...].astype(F32)
        for k, shifted in enumerate((s2, s1, dgc)):
            _acc(dcw_ref.at[k:k + 1, :], jnp.sum(shifted * gv, axis=0, keepdims=True), i == 0)
        _acc(dcb_ref, jnp.sum(dgc, axis=0, keepdims=True), i == 0)
        ebuf[tm:tm + HALO, :] = ebuf[0:HALO, :]

    tile = pl.BlockSpec((tm, tn), lambda j, i: (ni - 1 - i, j))
    wcol = lambda r: pl.BlockSpec((r, tn), lambda j, i: (0, j))
    out = jax.ShapeDtypeStruct((s, f), BF16)
    return _pcall(
        body, comm=comm, name=name, grid=(f // tn, ni),
        in_specs=[_resident((s, d)), tile, tile, tile,
                  pl.BlockSpec((tn, d), lambda j, i: (j, 0)), wcol(3)],
        out_specs=[tile, tile, wcol(3), wcol(1)],
        out_shape=[out, out, jax.ShapeDtypeStruct((3, f), F32), jax.ShapeDtypeStruct((1, f), F32)],
        scratch_shapes=[pltpu.VMEM((tm + HALO, tn), F32)],
        compiler_params=_params("arbitrary", "arbitrary"),
    )(dhb, g, gc, up, wd, cw)


def _ffn_bwd2(dg, dup, wg, wu, h, gain, dh, tm, name, comm=None):
    s, d = h.shape
    f = dg.shape[1]

    def body(dg_ref, dup_ref, wg_ref, wu_ref, h_ref, gain_ref, dh_ref, dho_ref, dhb_ref, dgain_ref):
        dhn = _nn(dg_ref[...], wg_ref[...]) + _nn(dup_ref[...], wu_ref[...])
        dx, dgain = _rms_bwd(dhn, h_ref[...], gain_ref[...])
        dho = dh_ref[...] + dx
        dho_ref[...] = dho
        dhb_ref[...] = dho.astype(BF16)
        _acc(dgain_ref, dgain, pl.program_id(0) == 0)

    row = lambda c: pl.BlockSpec((tm, c), lambda i: (i, 0))
    return _pcall(
        body, comm=comm, name=name, grid=(s // tm,),
        in_specs=[row(f), row(f), _whole(wg.shape), _whole(wu.shape), row(d), _whole((1, d)), row(d)],
        out_specs=[row(d), row(d), _whole((1, d))],
        out_shape=[jax.ShapeDtypeStruct((s, d), F32), jax.ShapeDtypeStruct((s, d), BF16), jax.ShapeDtypeStruct((1, d), F32)],
        compiler_params=_params("arbitrary"),
    )(dg, dup, wg, wu, h, gain, dh)


def _odd_bwd(dhb, dh, wout, pre, gate, ws, wst, sgu, win, h, gmix, tm, comm=None):
    s, d = h.shape
    e = win.shape[1]
    cw = e // 2
    heads = ws.shape[0]
    ni = s // tm

    def body(dhb_ref, dh_ref, wout_ref, pre_ref, gate_ref, ws_ref, wst_ref, sgu_ref, win_ref, h_ref, gmix_ref,
             dpre_ref, dho_ref, dhob_ref, dgain_ref, dsgu_ref, dws_ref, db_ref, vbuf, gacc):
        i = pl.program_id(0)
        first = i == 0
        dmixo = _nt(dhb_ref[...], wout_ref[...])
        pre = pre_ref[...].astype(F32)
        p = _gelu(pre)
        u, v = p[:, :cw], p[:, cw:]
        sgu = sgu_ref[...]
        rv = lax.rsqrt(jnp.mean(v * v, axis=-1, keepdims=True) + EPS)
        vh = v * rv
        vn = (vh * sgu).astype(BF16)
        du = dmixo * gate_ref[...].astype(F32)
        dgate = dmixo * u
        dgate_b = dgate.astype(BF16)
        gsum = dgate[0:CHUNK, :]
        for n in range(1, tm // CHUNK):
            gsum = gsum + dgate[n * CHUNK:(n + 1) * CHUNK, :]
        _acc(gacc, gsum, first)
        for hd in range(heads):
            cols = slice(hd * CHUNK, (hd + 1) * CHUNK)
            dws = None
            for n in range(tm // CHUNK):
                rows = slice(n * CHUNK, (n + 1) * CHUNK)
                vbuf[rows, cols] = _nn(wst_ref[hd], dgate_b[rows, cols])
                part = _nt(dgate_b[rows, cols], vn[rows, cols])
                dws = part if dws is None else dws + part
            _acc(dws_ref.at[hd], dws, first)
        dvn = vbuf[...]
        _acc(dsgu_ref, jnp.sum(dvn * vh, axis=0, keepdims=True), first)
        dvh = dvn * sgu
        dv = rv * (dvh - vh * jnp.mean(dvh * vh, axis=-1, keepdims=True))
        dpre = (jnp.concatenate([du, dv], axis=1) * _gelu_grad(pre)).astype(BF16)
        dpre_ref[...] = dpre
        dx, dgain = _rms_bwd(_nt(dpre, win_ref[...]), h_ref[...], gmix_ref[...])
        dho = dh_ref[...] + dx
        dho_ref[...] = dho
        dhob_ref[...] = dho.astype(BF16)
        _acc(dgain_ref, dgain, first)

        @pl.when(i == ni - 1)
        def _():
            ones = jnp.ones((8, CHUNK), F32)
            for hd in range(heads):
                tot = lax.dot_general(ones, gacc[:, hd * CHUNK:(hd + 1) * CHUNK], (((1,), (1,)), ((), ())),
                                      preferred_element_type=F32, precision=lax.Precision.HIGHEST)
                db_ref[hd:hd + 1, :] = tot[0:1, :]

    row = lambda c: pl.BlockSpec((tm, c), lambda i: (i, 0))
    return _pcall(
        body, comm=comm, name="odd_bwd", grid=(ni,),
        in_specs=[row(d), row(d), _whole(wout.shape), row(e), row(cw), _whole(ws.shape), _whole(wst.shape), _whole(sgu.shape),
                  _whole(win.shape), row(d), _whole((1, d))],
        out_specs=[row(e), row(d), row(d), _whole((1, d)), _whole((1, cw)), _whole(ws.shape), _whole((heads, CHUNK))],
        out_shape=[jax.ShapeDtypeStruct((s, e), BF16), jax.ShapeDtypeStruct((s, d), F32), jax.ShapeDtypeStruct((s, d), BF16),
                   jax.ShapeDtypeStruct((1, d), F32), jax.ShapeDtypeStruct((1, cw), F32), jax.ShapeDtypeStruct(ws.shape, F32),
                   jax.ShapeDtypeStruct((heads, CHUNK), F32)],
        scratch_shapes=[pltpu.VMEM((tm, cw), F32), pltpu.VMEM((CHUNK, cw), F32)],
        compiler_params=_params("arbitrary"),
    )(dhb, dh, wout, pre, gate, ws, wst, sgu, win, h, gmix)


def _even_bwd(dhb, dh, wout, proj, cq, pooled, conva, wpool, wpoolt, pscale, win, x, gmix, tm, comm=None):
    s, d = x.shape
    e = win.shape[1]
    aw = e // 4
    ni = s // tm

    def body(dhb_ref, dh_ref, wout_ref, proj_ref, cq_ref, pooled_ref, ca_ref, wp_ref, wpt_ref, ps_ref, win_ref, x_ref, gmix_ref,
             dproj_ref, dx_ref, dgain_ref, dca_ref, dwp_ref, dps_ref, cbuf, ebuf):
        i = pl.program_id(0)
        first = i == 0

        @pl.when(first)
        def _():
            cbuf[tm:tm + HALO, :] = jnp.zeros((HALO, aw), F32)
            ebuf[tm:tm + HALO, :] = jnp.zeros((HALO, aw), F32)

        dmix = _nt(dhb_ref[...], wout_ref[...])
        dy_a, dy_b = dmix[:, :aw], dmix[:, aw:]
        proj = proj_ref[...].astype(F32)
        a_b, a_c, a_v = (proj[:, k * aw:(k + 1) * aw] for k in range(3))
        da_b = dy_a * cq_ref[...].astype(F32)
        dcq = dy_a * a_b
        cbuf[0:tm, :] = dcq
        cext = cbuf[...]
        s1 = _shift(cext, 1, False)[0:tm, :]
        s2 = _shift(cext, 2, False)[0:tm, :]
        q = a_c * a_v
        for k, shifted in enumerate((s2, s1, dcq)):
            _acc(dca_ref.at[k:k + 1, :], jnp.sum(shifted * q, axis=0, keepdims=True), first)
        dq = ca_ref[2:3, :] * dcq + ca_ref[1:2, :] * s1 + ca_ref[0:1, :] * s2
        da_c = dq * a_v
        da_v = dq * a_c
        dps, dpool = [], []
        for g, w in enumerate(WINDOWS):
            cols = slice(g * LANE, (g + 1) * LANE)
            pooled = pooled_ref[:, cols]
            mixed = _nn(pooled, wp_ref[g])
            dps.append(jnp.sum(dy_b[:, cols] * mixed, axis=0, keepdims=True))
            dmixed = (dy_b[:, cols] * ps_ref[:, cols]).astype(BF16)
            _acc(dwp_ref.at[g], _tn(pooled, dmixed), first)
            dp = _nn(dmixed, wpt_ref[g])
            dpool.append(dp)
            ebuf[0:tm, cols] = dp / _counts((ni - 1 - i) * tm, tm, w)
        _acc(dps_ref, jnp.concatenate(dps, axis=1), first)
        eext = ebuf[...]
        dzs = []
        for g, w in enumerate(WINDOWS):
            cols = slice(g * LANE, (g + 1) * LANE)
            dzs.append(_window_sum(eext[:, cols], w, False)[0:tm, :] - dpool[g])
        dproj = jnp.concatenate([da_b, da_c, da_v] + dzs, axis=1).astype(BF16)
        dproj_ref[...] = dproj
        dx, dgain = _rms_bwd(_nt(dproj, win_ref[...]), x_ref[...], gmix_ref[...])
        dx_ref[...] = dh_ref[...] + dx
        _acc(dgain_ref, dgain, first)
        cbuf[tm:tm + HALO, :] = cbuf[0:HALO, :]
        ebuf[tm:tm + HALO, :] = ebuf[0:HALO, :]

    row = lambda c: pl.BlockSpec((tm, c), lambda i: (ni - 1 - i, 0))
    return _pcall(
        body, comm=comm, name="even_bwd", grid=(ni,),
        in_specs=[row(d), row(d), _whole(wout.shape), row(e), row(aw), row(aw), _whole(conva.shape), _whole(wpool.shape),
                  _whole(wpoolt.shape), _whole(pscale.shape), _whole(win.shape), row(d), _whole((1, d))],
        out_specs=[row(e), row(d), _whole((1, d)), _whole(conva.shape), _whole(wpool.shape), _whole(pscale.shape)],
        out_shape=[jax.ShapeDtypeStruct((s, e), BF16), jax.ShapeDtypeStruct((s, d), F32), jax.ShapeDtypeStruct((1, d), F32),
                   jax.ShapeDtypeStruct(conva.shape, F32), jax.ShapeDtypeStruct(wpool.shape, F32),
                   jax.ShapeDtypeStruct(pscale.shape, F32)],
        scratch_shapes=[pltpu.VMEM((tm + HALO, aw), F32), pltpu.VMEM((tm + HALO, aw), F32)],
        compiler_params=_params("arbitrary"),
    )(dhb, dh, wout, proj, cq, pooled, conva, wpool, wpoolt, pscale, win, x, gmix)


def _wgrad(a, b, tk, ts, name, comm=None):
    s, ka = a.shape
    nb = b.shape[1]
    nt = s // ts

    def body(a_ref, b_ref, o_ref, acc):
        t = pl.program_id(1)
        _acc(acc, _tn(a_ref[...], b_ref[...]), t == 0)

        @pl.when(t == nt - 1)
        def _():
            o_ref[...] = acc[...].astype(BF16)

    return _pcall(
        body, comm=comm, name=name, grid=(ka // tk, nt),
        in_specs=[pl.BlockSpec((ts, tk), lambda k, t: (t, k)), pl.BlockSpec((ts, nb), lambda k, t: (t, 0))],
        out_specs=pl.BlockSpec((tk, nb), lambda k, t: (k, 0)),
        out_shape=jax.ShapeDtypeStruct((ka, nb), BF16),
        scratch_shapes=[pltpu.VMEM((tk, nb), F32)],
        compiler_params=_params("arbitrary", "arbitrary"),
    )(a, b)


def _adamw(parts, w, m, v, tr, name, layer=None, into=None):
    r, c = w.shape[-2:]
    n, rp, cp = parts.shape
    tp = tr if rp == r else rp
    assert rp == r or tr == r

    def body(p_ref, w_ref, m_ref, v_ref, *rest):
        g_ref, d_ref, mo_ref, vo_ref = rest[-4:]
        g = p_ref[0, 0:tr, 0:c].astype(F32)
        for j in range(1, n):
            g = g + p_ref[j, 0:tr, 0:c].astype(F32)
        g_ref[...] = g
        mn = ADAM_B1 * m_ref[...] + (1.0 - ADAM_B1) * g
        vn = ADAM_B2 * v_ref[...] + (1.0 - ADAM_B2) * (g * g)
        mo_ref[...] = mn
        vo_ref[...] = vn
        m_hat = mn / (1.0 - ADAM_B1 ** ADAM_STEP)
        v_hat = vn / (1.0 - ADAM_B2 ** ADAM_STEP)
        d_ref[...] = -ADAM_LR * (m_hat / (jnp.sqrt(v_hat) + ADAM_EPS) + ADAM_WD * w_ref[...])

    if layer is None:
        row = pl.BlockSpec((tr, c), lambda i: (i, 0))
    else:
        row = pl.BlockSpec((None, tr, c), lambda i: (layer, i, 0))
    out = jax.ShapeDtypeStruct(w.shape, F32)
    prev = [] if into is None else list(into)
    return _pcall(
        body, name=name, grid=(r // tr,),
        in_specs=[pl.BlockSpec((n, tp, cp), lambda i: (0, i, 0)), row, row, row] + [pl.BlockSpec(memory_space=pl.ANY)] * len(prev),
        out_specs=[row, row, row, row], out_shape=[out, out, out, out],
        input_output_aliases={4 + k: k for k in range(len(prev))},
        compiler_params=_params("arbitrary"),
    )(parts, w, m, v, *prev)


def _pair_add(grad, other, spec, core, name):
    axis, width = spec
    slot = other.shape[1:]

    def body(core_ref, g_ref, o_ref, out_ref):
        out_ref[...] = (g_ref[...].astype(F32) + o_ref[...].astype(F32)).astype(BF16)

    if axis == 0:
        gspec = pl.BlockSpec(slot, lambda q, core_ref: (2 * q + core_ref[0], 0))
    else:
        gspec = pl.BlockSpec(slot, lambda q, core_ref: (0, 2 * q + core_ref[0]))
    per_chip = pl.BlockSpec((None,) + slot, lambda q, core_ref: (q, 0, 0))
    return _pcall(
        body, name=name,
        grid_spec=pltpu.PrefetchScalarGridSpec(num_scalar_prefetch=1, grid=(N_CHIP,), in_specs=[gspec, per_chip], out_specs=per_chip),
        out_shape=jax.ShapeDtypeStruct(other.shape, BF16),
        compiler_params=_params("arbitrary"),
    )(core, grad, other)


def _hbm():
    return pl.BlockSpec(memory_space=pltpu.HBM)


def _window(ref, spec, j):
    axis, width = spec
    start = pl.multiple_of(j * width, width)
    return ref.at[(slice(None),) * axis + (pl.ds(start, width),)]


def _here():
    return lax.axis_index("x"), lax.axis_index("y"), lax.axis_index("c")


class _Gather:
    def __init__(self, shards, specs, fulls):
        n = len(shards)
        self.ins, self.specs, self.out_shape = list(shards), list(specs), list(fulls)
        self.sems = [pltpu.SemaphoreType.DMA((7 * n,)), pltpu.SemaphoreType.DMA((7 * n,)), pltpu.SemaphoreType.DMA((n,))]

    def _plan(self, ins, outs, sems):
        send_sems, recv_sems, local_sems = sems
        x, y, c = _here()
        me, sibling = (x, y, c), (x, y, 1 - c)
        chips = [(1 - x, y), (x, 1 - y), (1 - x, 1 - y)]

        def slot(t, dev):
            return _window(outs[t], self.specs[t], 4 * dev[0] + 2 * dev[1] + dev[2])

        def copy(t, k, block, to, src=None):
            return pltpu.make_async_remote_copy(
                src_ref=slot(t, block) if src is None else src, dst_ref=slot(t, block),
                send_sem=send_sems.at[7 * t + k], recv_sem=recv_sems.at[7 * t + k], device_id=to, device_id_type=MESH)

        plan = []
        for t in range(len(ins)):
            plan.append(dict(
                mine=pltpu.make_async_copy(ins[t], slot(t, me), local_sems.at[t]),
                first=[copy(t, 0, me, sibling, src=ins[t])] + [copy(t, 1 + j, me, (*q, c), src=ins[t]) for j, q in enumerate(chips)],
                over_ici=[copy(t, 1 + j, (*q, c), me) for j, q in enumerate(chips)],
                passed=[copy(t, 4 + j, (*q, c), sibling) for j, q in enumerate(chips)],
                from_sibling=[copy(t, 0, sibling, me)] + [copy(t, 4 + j, (*q, 1 - c), me) for j, q in enumerate(chips)]))
        return plan

    def start(self, ins, outs, sems):
        for p in self._plan(ins, outs, sems):
            p["mine"].start()
            for cp in p["first"]:
                cp.start()

    def finish(self, ins, outs, sems):
        plan = self._plan(ins, outs, sems)
        for p in plan:
            for arrived, onward in zip(p["over_ici"], p["passed"]):
                arrived.wait_recv()
                onward.start()
        for p in plan:
            for cp in p["from_sibling"]:
                cp.wait_recv()
        for p in plan:
            for cp in p["first"] + p["passed"]:
                cp.wait_send()
            p["mine"].wait()


class _PairExchange:
    def __init__(self, grads, specs):
        n = len(grads)
        self.ins, self.specs = list(grads), list(specs)
        self.out_shape = [jax.ShapeDtypeStruct((N_CHIP,) + a.shape[:sp[0]] + (sp[1],) + a.shape[sp[0] + 1:], a.dtype)
                          for a, sp in zip(grads, specs)]
        self.sems = [pltpu.SemaphoreType.DMA((n,)), pltpu.SemaphoreType.DMA((n,))]

    def start(self, ins, outs, sems):
        send_sems, recv_sems = sems
        x, y, c = _here()
        for t in range(len(ins)):
            for q in range(N_CHIP):
                pltpu.make_async_remote_copy(
                    src_ref=_window(ins[t], self.specs[t], 2 * q + (1 - c)), dst_ref=outs[t].at[q],
                    send_sem=send_sems.at[t], recv_sem=recv_sems.at[t], device_id=(x, y, 1 - c), device_id_type=MESH).start()

    def finish(self, ins, outs, sems):
        send_sems, recv_sems = sems
        x, y, c = _here()
        for t in range(len(ins)):
            every = pltpu.make_async_remote_copy(src_ref=outs[t], dst_ref=outs[t], send_sem=send_sems.at[t],
                                                 recv_sem=recv_sems.at[t], device_id=(x, y, 1 - c), device_id_type=MESH)
            every.wait_send()
            every.wait_recv()


class _ChipExchange:
    def __init__(self, pairs, slotted=(), whole=()):
        self.ins = list(pairs) + list(slotted) + list(whole)
        self.npair, self.nslot = len(pairs), len(pairs) + len(slotted)
        n = len(self.ins)
        self.out_shape = ([jax.ShapeDtypeStruct(a.shape, a.dtype) for a in list(pairs) + list(slotted)]
                          + [jax.ShapeDtypeStruct((N_DEV,) + a.shape, a.dtype) for a in whole])
        self.sems = [pltpu.SemaphoreType.DMA((7 * n,)), pltpu.SemaphoreType.DMA((7 * n,)), pltpu.SemaphoreType.DMA((n,))]

    def _plan(self, ins, outs, sems):
        send_sems, recv_sems, local_sems = sems
        npair, nslot = self.npair, self.nslot
        x, y, c = _here()
        me, chip = 4 * x + 2 * y + c, 2 * x + y
        chips = [(1 - x, y), (x, 1 - y), (1 - x, 1 - y)]
        peers = [(x, y, 1 - c)] + [(*q, c) for q in chips] + [(*q, 1 - c) for q in chips]

        def index(dev):
            return 4 * dev[0] + 2 * dev[1] + dev[2]

        def copy(t, k, arriving):
            peer = peers[k]
            if t < npair:
                src, mine, theirs = ins[t].at[2 * peer[0] + peer[1]], chip, 2 * peer[0] + peer[1]
            else:
                src, mine, theirs = (ins[t].at[index(peer)] if t < nslot else ins[t]), me, index(peer)
            return pltpu.make_async_remote_copy(
                src_ref=src, dst_ref=outs[t].at[theirs if arriving else mine],
                send_sem=send_sems.at[7 * t + k], recv_sem=recv_sems.at[7 * t + k], device_id=peer, device_id_type=MESH)

        own, sent, arriving = [], [], []
        for t in range(len(ins)):
            fan = range(1, 4) if t < npair else range(7)
            if t < npair:
                own.append(pltpu.make_async_copy(ins[t].at[chip], outs[t].at[chip], local_sems.at[t]))
            else:
                own.append(pltpu.make_async_copy(ins[t].at[me] if t < nslot else ins[t], outs[t].at[me], local_sems.at[t]))
            sent += [copy(t, k, False) for k in fan]
            arriving += [copy(t, k, True) for k in fan]
        return own, sent, arriving

    def start(self, ins, outs, sems):
        own, sent, _ = self._plan(ins, outs, sems)
        for cp in own + sent:
            cp.start()

    def finish(self, ins, outs, sems):
        own, sent, arriving = self._plan(ins, outs, sems)
        for cp in arriving:
            cp.wait_recv()
        for cp in sent:
            cp.wait_send()
        for cp in own:
            cp.wait()


class _Jobs:
    def __init__(self, *jobs):
        self.jobs = jobs
        self.ins = [a for j in jobs for a in j.ins]
        self.out_shape = [a for j in jobs for a in j.out_shape]
        self.sems = [a for j in jobs for a in j.sems]

    def _split(self, ins, outs, sems):
        i = o = s = 0
        for j in self.jobs:
            yield j, ins[i:i + len(j.ins)], outs[o:o + len(j.out_shape)], sems[s:s + len(j.sems)]
            i, o, s = i + len(j.ins), o + len(j.out_shape), s + len(j.sems)

    def start(self, ins, outs, sems):
        for j, a, b, c in self._split(ins, outs, sems):
            j.start(a, b, c)

    def finish(self, ins, outs, sems):
        for j, a, b, c in self._split(ins, outs, sems):
            j.finish(a, b, c)

    def results(self, outs):
        return [b for _, _, b, _ in self._split((), outs, ())]


def _alone(job, name):
    return _pcall(lambda: None, comm=job, name=name, in_specs=[], out_specs=[], out_shape=[])()[1]


SMALL_ROWS = 24
REP_COLS = 1024


def _pad_to(a, rows, cols):
    return jnp.pad(a, ((0, rows - a.shape[0]), (0, cols - a.shape[1])))


def _pack_small(conv_a, sgu_norm, conv_ffn, cols):
    return jnp.concatenate([_pad_to(conv_a, 8, cols), _pad_to(sgu_norm, 8, cols),
                            _pad_to(conv_ffn.reshape(-1, conv_ffn.shape[-1]), 8, cols)], axis=0)


def _unpack_small(p, ca_w, sg_w, cf_w):
    return p[0:3, 0:ca_w], p[8:9, 0:sg_w], p[16:22, 0:cf_w].reshape(2, 3, cf_w)


def _tile_rows(rows):
    return -(-rows // 8) * 8


def _pack_rows(parts):
    return jnp.concatenate([_pad_to(a, _tile_rows(a.shape[0]), REP_COLS) for a in parts], axis=0)


def _unpack_rows(p, shapes):
    out, r0 = [], 0
    for r, c in shapes:
        out.append(p[r0:r0 + r, 0:c])
        r0 += _tile_rows(r)
    return out


def _rep_late(norm_mix0, norm_ffn0, pool_scale, b_conv0, w_pool):
    return _pack_rows([norm_mix0, norm_ffn0, pool_scale.reshape(1, -1), _pad_to(b_conv0, 1, 3 * REP_COLS).reshape(3, REP_COLS),
                       w_pool.reshape(-1, REP_COLS)])


def _rep_early(norm_mix1, norm_ffn1, final_norm, b_spatial, b_conv1, w_spatial, loss):
    return _pack_rows([norm_mix1, norm_ffn1, final_norm.reshape(1, -1), b_spatial.reshape(1, -1),
                       _pad_to(b_conv1, 1, 3 * REP_COLS).reshape(3, REP_COLS), w_spatial.reshape(-1, REP_COLS), loss])


def _unpack_rep(late, early, like):
    f = like["b_conv_ffn"].shape[1]
    nm0, nf0, ps, bc0, wp = _unpack_rows(late, [(1, REP_COLS), (1, REP_COLS), (1, like["pool_scale"].shape[1]), (3, REP_COLS),
                                                (like["w_pool"].size // REP_COLS, REP_COLS)])
    nm1, nf1, fin, bs, bc1, wsp, _ = _unpack_rows(early, [(1, REP_COLS)] * 4 + [(3, REP_COLS), (like["w_spatial"].size // REP_COLS, REP_COLS),
                                                          (1, REP_COLS)])
    return {
        "norm_mix": jnp.concatenate([nm0, nm1]), "norm_ffn": jnp.concatenate([nf0, nf1]), "final_norm": fin[0], "pool_scale": ps,
        "b_spatial": bs.reshape(like["b_spatial"].shape),
        "b_conv_ffn": jnp.concatenate([bc0.reshape(1, -1), bc1.reshape(1, -1)])[:, 0:f],
        "w_pool": wp.reshape(like["w_pool"].shape), "w_spatial": wsp.reshape(like["w_spatial"].shape),
    }


def _pad_slots(a, width, padded):
    a = a.reshape(*a.shape[:-1], N_DEV, width)
    a = jnp.pad(a, ((0, 0),) * (a.ndim - 1) + ((0, padded - width),))
    return a.reshape(*a.shape[:-2], N_DEV * padded)


def _unpad_slots(a, width, padded):
    a = a.reshape(*a.shape[:-1], N_DEV, padded)[..., 0:width]
    return a.reshape(*a.shape[:-2], N_DEV * width)


def kernel(x, norm_mix, norm_ffn, final_norm, w_in_even, conv_a, w_pool, pool_scale, w_out_even, w_in_odd, sgu_norm, w_spatial, b_spatial, w_out_odd, w_ffn_gate, w_ffn_up, conv_ffn, b_conv_ffn, w_ffn_down, loss_target, m_norm_mix, m_norm_ffn, m_final_norm, m_w_in_even, m_conv_a, m_w_pool, m_pool_scale, m_w_out_even, m_w_in_odd, m_sgu_norm, m_w_spatial, m_b_spatial, m_w_out_odd, m_w_ffn_gate, m_w_ffn_up, m_conv_ffn, m_b_conv_ffn, m_w_ffn_down, v_norm_mix, v_norm_ffn, v_final_norm, v_w_in_even, v_conv_a, v_w_pool, v_pool_scale, v_w_out_even, v_w_in_odd, v_sgu_norm, v_w_spatial, v_b_spatial, v_w_out_odd, v_w_ffn_gate, v_w_ffn_up, v_conv_ffn, v_b_conv_ffn, v_w_ffn_down):
    s, d = x.shape[1], x.shape[2]
    x2, target = x[0], loss_target[0]
    tm = min(512, s)
    tm_wide = min(2048, s)
    tn = 256
    row = lambda a: a.reshape(1, -1)
    ein, ro = w_in_even.shape[2], w_out_even.shape[1]
    fs = w_ffn_gate.shape[2]
    fsp = -(-fs // LANE) * LANE
    fp = N_DEV * fsp
    full = lambda shape, dtype=BF16: jax.ShapeDtypeStruct(shape, dtype)

    wg_s = [jnp.pad(w_ffn_gate[l].T, ((0, fsp - fs), (0, 0))).astype(BF16) for l in range(2)]
    wu_s = [jnp.pad(w_ffn_up[l].T, ((0, fsp - fs), (0, 0))).astype(BF16) for l in range(2)]
    wd_s = [jnp.pad(w_ffn_down[l], ((0, fsp - fs), (0, 0))).astype(BF16) for l in range(2)]
    small_s = _pack_small(conv_a[0], sgu_norm, conv_ffn, fsp)[None]
    in_spec, out_spec, gu_spec, down_spec = (1, ein), (0, ro), (0, fsp), (0, fsp)
    full_in, full_out, full_gu, full_down = full((d, N_DEV * ein)), full((N_DEV * ro, d)), full((fp, d)), full((fp, d))
    win_e, wout_e, gsmall = _alone(_Gather([w_in_even[0].astype(BF16), w_out_even[0].astype(BF16), small_s], [in_spec, out_spec, (0, 1)],
                                           [full_in, full_out, full((N_DEV, SMALL_ROWS, fsp), F32)]), "gather_mix0")
    ca_full = jnp.moveaxis(gsmall[:, 0:3, 0:conv_a.shape[2]], 0, 1).reshape(3, -1)
    sgu_full = gsmall[:, 8, 0:sgu_norm.shape[1]].reshape(1, -1)
    cf_full = jnp.moveaxis(gsmall[:, 16:22, :].reshape(N_DEV, 2, 3, fsp), 0, 2).reshape(2, 3, fp)
    cb_full = _pad_slots(b_conv_ffn, fs, fsp)

    tril = jnp.tril(jnp.ones((CHUNK, CHUNK), F32))
    ws_m = w_spatial[0] * tril
    ws_b = ws_m.astype(BF16)
    wst_b = jnp.swapaxes(ws_m, 1, 2).astype(BF16)
    bfull = jnp.repeat(b_spatial[0].T, CHUNK, axis=1)
    wpool_b = w_pool[0].astype(BF16)
    wpoolt_b = jnp.swapaxes(w_pool[0], 1, 2).astype(BF16)

    (xn0, proj0, cq0, pooled0, mix0, h1, hn0), (wg0,) = _even_fwd(
        x2, norm_mix[0:1], norm_ffn[0:1], win_e, ca_full, wpool_b, pool_scale, wout_e, tm,
        comm=_Gather([wg_s[0]], [gu_spec], [full_gu]))
    (g0, gc0), (wu0,) = _ffn_gate(hn0, wg0, cf_full[0], cb_full[0:1], tm_wide, tn, "ffn_gate_l0", comm=_Gather([wu_s[0]], [gu_spec], [full_gu]))
    (up0, a0), (wd0,) = _ffn_up(hn0, wu0, gc0, tm_wide, tn, "ffn_up_l0", comm=_Gather([wd_s[0]], [down_spec], [full_down]))
    (h2, xn1), (win_o, wout_o) = _ffn_fwd2(a0, wd0, h1, norm_mix[1:2], tm, "ffn_fwd2_l0", comm=_Gather(
        [w_in_odd[0].astype(BF16), w_out_odd[0].astype(BF16)], [in_spec, out_spec], [full_in, full_out]))
    (pre1, gate1, mixo1, h3, hn1), (wg1,) = _odd_fwd(xn1, h2, win_o, sgu_full, ws_b, bfull, wout_o, norm_ffn[1:2], tm,
                                                    comm=_Gather([wg_s[1]], [gu_spec], [full_gu]))
    (g1, gc1), (wu1,) = _ffn_gate(hn1, wg1, cf_full[1], cb_full[1:2], tm_wide, tn, "ffn_gate_l1", comm=_Gather([wu_s[1]], [gu_spec], [full_gu]))
    (up1, a1), (wd1,) = _ffn_up(hn1, wu1, gc1, tm_wide, tn, "ffn_up_l1", comm=_Gather([wd_s[1]], [down_spec], [full_down]))
    h4, _ = _ffn_fwd2(a1, wd1, h3, row(final_norm), tm, "ffn_fwd2_l1")

    core = lax.axis_index("c").astype(jnp.int32).reshape(1)
    ts = min(1024, s)
    dh4, dh4b, d_final, lossvec = _loss_bwd(h4, row(final_norm), target, tm)
    gd1 = _wgrad(a1, dh4b, tn, s, "wgrad_down_l1")
    (dg1, dup1, dcw1, dcb1), (o_d1,) = _ffn_bwd1(dh4b, g1, gc1, up1, wd1, cf_full[1], tm_wide, tn, "ffn_bwd1_l1",
                                                 comm=_PairExchange([gd1], [down_spec]))
    p_d1 = _pair_add(gd1, o_d1, down_spec, core, "pair_add_down_l1")
    gg1 = _wgrad(dg1, hn1, tn, s, "wgrad_gate_l1")
    gu1, (o_g1,) = _wgrad(dup1, hn1, tn, s, "wgrad_up_l1", comm=_PairExchange([gg1], [gu_spec]))
    p_g1 = _pair_add(gg1, o_g1, gu_spec, core, "pair_add_gate_l1")
    jobs = _Jobs(_ChipExchange([p_d1]), _PairExchange([gu1], [gu_spec]))
    (dh3, dh3b, d_nffn1), res = _ffn_bwd2(dg1, dup1, wg1, wu1, h3, norm_ffn[1:2], dh4, tm // 2, "ffn_bwd2_l1", comm=jobs)
    (s_d1,), (o_u1,) = jobs.results(res)
    p_u1 = _pair_add(gu1, o_u1, gu_spec, core, "pair_add_up_l1")
    (dpre1, dh2, dh2b, d_nmix1, d_sgu, d_ws, d_b), (s_g1, s_u1) = _odd_bwd(
        dh3b, dh3, wout_o, pre1, gate1, ws_b, wst_b, sgu_full, win_o, h2, norm_mix[1:2], tm, comm=_ChipExchange([p_g1, p_u1]))
    gi1 = _wgrad(xn1, dpre1, 512, ts, "wgrad_in_odd")
    go1, (o_i1,) = _wgrad(mixo1, dh3b, 512, ts, "wgrad_out_odd", comm=_PairExchange([gi1], [in_spec]))
    p_i1 = _pair_add(gi1, o_i1, in_spec, core, "pair_add_in_odd")
    gd0, (o_o1,) = _wgrad(a0, dh2b, tn, s, "wgrad_down_l0", comm=_PairExchange([go1], [out_spec]))
    p_o1 = _pair_add(go1, o_o1, out_spec, core, "pair_add_out_odd")
    d_early = _rep_early(d_nmix1, d_nffn1, d_final, d_b, _unpad_slots(dcb1, fs, fsp), d_ws * tril, lossvec)
    jobs = _Jobs(_ChipExchange([p_i1, p_o1], [], [d_early]), _PairExchange([gd0], [down_spec]))
    (dg0, dup0, dcw0, dcb0), res = _ffn_bwd1(dh2b, g0, gc0, up0, wd0, cf_full[0], tm_wide, tn, "ffn_bwd1_l0", comm=jobs)
    (s_i1, s_o1, r_early), (o_d0,) = jobs.results(res)
    p_d0 = _pair_add(gd0, o_d0, down_spec, core, "pair_add_down_l0")
    gg0, (s_d0,) = _wgrad(dg0, hn0, tn, s, "wgrad_gate_l0", comm=_ChipExchange([p_d0]))
    gu0, (o_g0,) = _wgrad(dup0, hn0, tn, s, "wgrad_up_l0", comm=_PairExchange([gg0], [gu_spec]))
    p_g0 = _pair_add(gg0, o_g0, gu_spec, core, "pair_add_gate_l0")
    jobs = _Jobs(_ChipExchange([p_g0]), _PairExchange([gu0], [gu_spec]))
    (dh1, dh1b, d_nffn0), res = _ffn_bwd2(dg0, dup0, wg0, wu0, h1, norm_ffn[0:1], dh2, tm // 2, "ffn_bwd2_l0", comm=jobs)
    (s_g0,), (o_u0,) = jobs.results(res)
    p_u0 = _pair_add(gu0, o_u0, gu_spec, core, "pair_add_up_l0")
    (dproj0, grad_x, d_nmix0, d_ca, d_wp, d_ps), (s_u0,) = _even_bwd(
        dh1b, dh1, wout_e, proj0, cq0, pooled0, ca_full, wpool_b, wpoolt_b, pool_scale, win_e, x2, norm_mix[0:1], tm, comm=_ChipExchange([p_u0]))
    go0 = _wgrad(mix0, dh1b, 512, ts, "wgrad_out_even")
    gi0, (o_o0,) = _wgrad(xn0, dproj0, 512, ts, "wgrad_in_even", comm=_PairExchange([go0], [out_spec]))
    p_o0 = _pair_add(go0, o_o0, out_spec, core, "pair_add_out_even")
    (o_i0,) = _alone(_PairExchange([gi0], [in_spec]), "pair_exchange_in_even")
    p_i0 = _pair_add(gi0, o_i0, in_spec, core, "pair_add_in_even")
    d_small = jnp.stack([_pack_small(a, b, c, fsp) for a, b, c in zip(
        jnp.moveaxis(d_ca.reshape(3, N_DEV, -1), 1, 0), jnp.moveaxis(d_sgu.reshape(1, N_DEV, -1), 1, 0),
        jnp.moveaxis(jnp.stack([dcw0, dcw1]).reshape(2, 3, N_DEV, fsp), 2, 0))])
    d_late = _rep_late(d_nmix0, d_nffn0, d_ps, _unpad_slots(dcb0, fs, fsp), d_wp)
    s_i0, s_o0, r_small, r_late = _alone(_ChipExchange([p_i0, p_o0], [d_small], [d_late]), "chip_exchange_last")
    loss = jnp.sum(r_early[:, r_early.shape[1] - 8, :])

    out = {}
    out["w_in_even"] = _adamw(s_i0, w_in_even[0], m_w_in_even[0], v_w_in_even[0], 256, "adamw_in_even")
    out["w_out_even"] = _adamw(s_o0, w_out_even[0], m_w_out_even[0], v_w_out_even[0], ro, "adamw_out_even")
    out["w_in_odd"] = _adamw(s_i1, w_in_odd[0], m_w_in_odd[0], v_w_in_odd[0], 256, "adamw_in_odd")
    out["w_out_odd"] = _adamw(s_o1, w_out_odd[0], m_w_out_odd[0], v_w_out_odd[0], ro, "adamw_out_odd")
    tp = lambda a: jnp.swapaxes(a, 1, 2)
    for nm, s1, s0, w, m, v, back in (("w_ffn_gate", s_g1, s_g0, tp(w_ffn_gate), tp(m_w_ffn_gate), tp(v_w_ffn_gate), tp),
                                      ("w_ffn_up", s_u1, s_u0, tp(w_ffn_up), tp(m_w_ffn_up), tp(v_w_ffn_up), tp),
                                      ("w_ffn_down", s_d1, s_d0, w_ffn_down, m_w_ffn_down, v_w_ffn_down, lambda a: a)):
        l1 = _adamw(s1, w, m, v, fs, "adamw_%s_l1" % nm, layer=1)
        out[nm] = [back(a) for a in _adamw(s0, w, m, v, fs, "adamw_%s_l0" % nm, layer=0, into=l1)]
    small = _adamw(r_small, _pack_small(conv_a[0], sgu_norm, conv_ffn, fsp), _pack_small(m_conv_a[0], m_sgu_norm, m_conv_ffn, fsp),
                   _pack_small(v_conv_a[0], v_sgu_norm, v_conv_ffn, fsp), SMALL_ROWS, "adamw_small")
    no_loss = jnp.zeros((1, REP_COLS), F32)
    early = _adamw(r_early, *[_rep_early(nm[1:2], nf[1:2], fn, bs, bc[1:2], wsp, no_loss) for nm, nf, fn, bs, bc, wsp in (
        (norm_mix, norm_ffn, final_norm, b_spatial, b_conv_ffn, w_spatial), (m_norm_mix, m_norm_ffn, m_final_norm, m_b_spatial, m_b_conv_ffn, m_w_spatial),
        (v_norm_mix, v_norm_ffn, v_final_norm, v_b_spatial, v_b_conv_ffn, v_w_spatial))], r_early.shape[1], "adamw_replicated_early")
    late = _adamw(r_late, *[_rep_late(nm[0:1], nf[0:1], ps, bc[0:1], wp) for nm, nf, ps, bc, wp in (
        (norm_mix, norm_ffn, pool_scale, b_conv_ffn, w_pool), (m_norm_mix, m_norm_ffn, m_pool_scale, m_b_conv_ffn, m_w_pool),
        (v_norm_mix, v_norm_ffn, v_pool_scale, v_b_conv_ffn, v_w_pool))], r_late.shape[1], "adamw_replicated_late")

    names = ["norm_mix", "norm_ffn", "final_norm", "w_in_even", "conv_a", "w_pool", "pool_scale", "w_out_even", "w_in_odd", "sgu_norm",
             "w_spatial", "b_spatial", "w_out_odd", "w_ffn_gate", "w_ffn_up", "conv_ffn", "b_conv_ffn", "w_ffn_down"]
    like = {"norm_mix": norm_mix, "norm_ffn": norm_ffn, "final_norm": final_norm, "w_in_even": w_in_even, "conv_a": conv_a,
            "w_pool": w_pool, "pool_scale": pool_scale, "w_out_even": w_out_even, "w_in_odd": w_in_odd, "sgu_norm": sgu_norm,
            "w_spatial": w_spatial, "b_spatial": b_spatial, "w_out_odd": w_out_odd, "w_ffn_gate": w_ffn_gate, "w_ffn_up": w_ffn_up,
            "conv_ffn": conv_ffn, "b_conv_ffn": b_conv_ffn, "w_ffn_down": w_ffn_down}
    groups = []
    for k in range(4):
        ca_k, sg_k, cf_k = _unpack_small(small[k], conv_a.shape[2], sgu_norm.shape[1], conv_ffn.shape[2])
        vals = dict(_unpack_rep(late[k], early[k], like))
        vals.update(conv_a=ca_k, sgu_norm=sg_k, conv_ffn=cf_k)
        for nm in ("w_in_even", "w_in_odd", "w_out_even", "w_out_odd", "w_ffn_gate", "w_ffn_up", "w_ffn_down"):
            vals[nm] = out[nm][k]
        groups.append([vals[nm].reshape(like[nm].shape) for nm in names])
    return (loss, grad_x[None], *groups[0], *groups[1], *groups[2], *groups[3])
```

```python
import functools

import jax
import jax.numpy as jnp
from jax import lax
from jax.experimental import pallas as pl
from jax.experimental.pallas import tpu as pltpu

F32, BF16 = jnp.float32, jnp.bfloat16
EPS = 1e-6
WINDOWS = (2, 4, 8, 16)
HALO = 16
CHUNK = 128
N_DEV = 8
N_CHIP = 4
MESH = pl.DeviceIdType.MESH
VMEM_LIMIT = 56 * 2**20
LANE = 128
ADAM_LR, ADAM_B1, ADAM_B2, ADAM_EPS, ADAM_WD, ADAM_STEP = 0.001, 0.9, 0.999, 1e-08, 0.01, 10
LATE_NUM, LATE_DEN = 3, 4
INV_SQRT2 = 0.7071067811865476
INV_SQRT2PI = 0.3989422804014327


def _pcall(body, comm=None, **kw):
    if comm is None:
        return pl.pallas_call(body, **kw)
    in_specs, out_specs, out_shape = list(kw.pop("in_specs")), kw.pop("out_specs"), kw.pop("out_shape")
    single = not isinstance(out_shape, (list, tuple))
    out_specs, out_shape = ([out_specs], [out_shape]) if single else (list(out_specs), list(out_shape))
    scratch = list(kw.pop("scratch_shapes", []))
    grid = kw.get("grid", ())
    n_in, n_out, n_scr, c_in, c_out = len(in_specs), len(out_specs), len(scratch), len(comm.ins), len(comm.out_shape)

    def hosted(*refs):
        cuts = [0, n_in, n_in + c_in, n_in + c_in + n_out, n_in + c_in + n_out + c_out, n_in + c_in + n_out + c_out + n_scr, len(refs)]
        ins, cins, outs, couts, scr, sems = (refs[a:b] for a, b in zip(cuts[:-1], cuts[1:]))
        if grid:
            step, steps = 0, 1
            for axis, size in enumerate(grid):
                step, steps = step * size + pl.program_id(axis), steps * size
            pl.when(step == 0)(lambda: comm.start(cins, couts, sems))
            pl.when(step == (steps * LATE_NUM) // LATE_DEN)(lambda: comm.middle(cins, couts, sems))
            body(*ins, *outs, *scr)
            pl.when(step == steps - 1)(lambda: comm.finish(cins, couts, sems))
        else:
            comm.start(cins, couts, sems)
            comm.middle(cins, couts, sems)
            body(*ins, *outs, *scr)
            comm.finish(cins, couts, sems)

    call = pl.pallas_call(hosted, in_specs=in_specs + [_hbm()] * c_in, out_specs=out_specs + [_hbm()] * c_out,
                          out_shape=out_shape + list(comm.out_shape), scratch_shapes=scratch + list(comm.sems), **kw)

    def run(*args):
        res = call(*args, *comm.ins)
        own = res[0] if single else res[:n_out]
        return own, res[n_out:]

    return run


def _params(*sem):
    return pltpu.CompilerParams(dimension_semantics=sem, vmem_limit_bytes=VMEM_LIMIT)


def _whole(shape):
    return pl.BlockSpec(shape, lambda *_: (0,) * len(shape))


def _resident(shape):
    return pl.BlockSpec(shape, lambda *_: (0,) * len(shape), pipeline_mode=pl.Buffered(1))


def _rows(i, tm):
    return pl.ds(pl.multiple_of(i * tm, tm), tm)


def _nn(a, b):
    return jnp.dot(a, b, preferred_element_type=F32)


def _nt(a, b):
    return lax.dot_general(a, b, (((1,), (1,)), ((), ())), preferred_element_type=F32)


def _tn(a, b):
    return lax.dot_general(a, b, (((0,), (0,)), ((), ())), preferred_element_type=F32)


def _rms(x, gain):
    r = lax.rsqrt(jnp.mean(x * x, axis=-1, keepdims=True) + EPS)
    return x * r * gain


def _rms_bwd(dy, x, gain):
    r = lax.rsqrt(jnp.mean(x * x, axis=-1, keepdims=True) + EPS)
    xh = x * r
    dgain = jnp.sum(dy * xh, axis=0, keepdims=True)
    dxh = dy * gain
    dx = r * (dxh - xh * jnp.mean(dxh * xh, axis=-1, keepdims=True))
    return dx, dgain


def _gelu(x):
    return 0.5 * x * (1.0 + lax.erf(x * INV_SQRT2))


def _gelu_grad(x):
    return 0.5 * (1.0 + lax.erf(x * INV_SQRT2)) + x * jnp.exp(-0.5 * x * x) * INV_SQRT2PI


def _acc(ref, val, first):
    @pl.when(first)
    def _():
        ref[...] = val

    @pl.when(jnp.logical_not(first))
    def _():
        ref[...] += val


def _shift(ext, k, back):
    n = ext.shape[0]
    return pltpu.roll(ext, k if back else n - k, axis=0)


def _window_sum(ext, w, back):
    total, step = ext, 1
    while step < w:
        total = total + _shift(total, step, back)
        step *= 2
    return total


def _counts(row0, tm, w):
    pos1 = (row0 + lax.broadcasted_iota(jnp.int32, (tm, 1), 0) + 1).astype(F32)
    return jnp.minimum(pos1, float(w))


def _even_fwd(x, gmix, gffn, win, conva, wpool, pscale, wout, tm, comm=None):
    s, d = x.shape
    e = win.shape[1]
    aw = e // 4

    def body(x_ref, gmix_ref, gffn_ref, win_ref, ca_ref, wp_ref, ps_ref, wout_ref,
             xn_ref, proj_ref, cq_ref, pooled_ref, mix_ref, h_ref, hn_ref, qbuf, zbuf):
        i = pl.program_id(0)

        @pl.when(i == 0)
        def _():
            qbuf[0:HALO, :] = jnp.zeros((HALO, aw), F32)
            zbuf[0:HALO, :] = jnp.zeros((HALO, aw), F32)

        xv = x_ref[...]
        xn = _rms(xv, gmix_ref[...]).astype(BF16)
        xn_ref[...] = xn
        proj = _nn(xn, win_ref[...])
        proj_ref[...] = proj.astype(BF16)
        a_b, a_c, a_v, z = (proj[:, k * aw:(k + 1) * aw] for k in range(4))
        q = a_c * a_v
        qbuf[HALO:HALO + tm, :] = q
        qext = qbuf[...]
        cur = slice(HALO, HALO + tm)
        cq = ca_ref[2:3, :] * q + ca_ref[1:2, :] * _shift(qext, 1, True)[cur, :] + ca_ref[0:1, :] * _shift(qext, 2, True)[cur, :]
        cq_ref[...] = cq.astype(BF16)
        y_a = a_b * cq
        zbuf[HALO:HALO + tm, :] = z
        zext = zbuf[...]
        ys = []
        for g, w in enumerate(WINDOWS):
            cols = slice(g * LANE, (g + 1) * LANE)
            acc = _window_sum(zext[:, cols], w, True)[cur, :]
            pooled = (acc / _counts(i * tm, tm, w) - z[:, cols]).astype(BF16)
            pooled_ref[:, cols] = pooled
            ys.append(_nn(pooled, wp_ref[g]))
        y_b = jnp.concatenate(ys, axis=1) * ps_ref[...]
        mix = jnp.concatenate([y_a, y_b], axis=1).astype(BF16)
        mix_ref[...] = mix
        h = xv + _nn(mix, wout_ref[...])
        h_ref[...] = h
        hn_ref[...] = _rms(h, gffn_ref[...]).astype(BF16)
        qbuf[0:HALO, :] = qbuf[tm:tm + HALO, :]
        zbuf[0:HALO, :] = zbuf[tm:tm + HALO, :]

    row = lambda c: pl.BlockSpec((tm, c), lambda i: (i, 0))
    return _pcall(
        body, comm=comm, name="even_fwd", grid=(s // tm,),
        in_specs=[row(d), _whole((1, d)), _whole((1, d)), _whole(win.shape), _whole(conva.shape), _whole(wpool.shape),
                  _whole(pscale.shape), _whole(wout.shape)],
        out_specs=[row(d), row(e), row(aw), row(aw), row(d), row(d), row(d)],
        out_shape=[jax.ShapeDtypeStruct((s, d), BF16), jax.ShapeDtypeStruct((s, e), BF16), jax.ShapeDtypeStruct((s, aw), BF16),
                   jax.ShapeDtypeStruct((s, aw), BF16), jax.ShapeDtypeStruct((s, d), BF16), jax.ShapeDtypeStruct((s, d), F32),
                   jax.ShapeDtypeStruct((s, d), BF16)],
        scratch_shapes=[pltpu.VMEM((tm + HALO, aw), F32), pltpu.VMEM((tm + HALO, aw), F32)],
        compiler_params=_params("arbitrary"),
    )(x, gmix, gffn, win, conva, wpool, pscale, wout)


def _ffn_gate(hn, wgt, cw, cb, tm, tn, name, comm=None):
    s, d = hn.shape
    f = wgt.shape[0]

    def body(hn_ref, wg_ref, cw_ref, cb_ref, g_ref, gc_ref, gbuf):
        i = pl.program_id(1)

        @pl.when(i == 0)
        def _():
            gbuf[0:HALO, :] = jnp.zeros((HALO, tn), F32)

        g = _nt(hn_ref[_rows(i, tm), :], wg_ref[...])
        g_ref[...] = g.astype(BF16)
        gbuf[HALO:HALO + tm, :] = g
        ext = gbuf[...]
        gc = (cw_ref[2:3, :] * g + cw_ref[1:2, :] * pltpu.roll(ext, 1, axis=0)[HALO:HALO + tm, :]
              + cw_ref[0:1, :] * pltpu.roll(ext, 2, axis=0)[HALO:HALO + tm, :] + cb_ref[...])
        gc_ref[...] = gc.astype(BF16)
        gbuf[0:HALO, :] = gbuf[tm:tm + HALO, :]

    tile = pl.BlockSpec((tm, tn), lambda j, i: (i, j))
    wcol = lambda r: pl.BlockSpec((r, tn), lambda j, i: (0, j))
    out = jax.ShapeDtypeStruct((s, f), BF16)
    return _pcall(
        body, comm=comm, name=name, grid=(f // tn, s // tm),
        in_specs=[_resident((s, d)), pl.BlockSpec((tn, d), lambda j, i: (j, 0)), wcol(3), wcol(1)],
        out_specs=[tile, tile], out_shape=[out, out],
        scratch_shapes=[pltpu.VMEM((tm + HALO, tn), F32)],
        compiler_params=_params("arbitrary", "arbitrary"),
    )(hn, wgt, cw, cb)


def _ffn_up(hn, wut, gc, tm, tn, name, comm=None):
    s, d = hn.shape
    f = wut.shape[0]

    def body(hn_ref, wu_ref, gc_ref, up_ref, a_ref):
        up = _nt(hn_ref[_rows(pl.program_id(1), tm), :], wu_ref[...])
        up_ref[...] = up.astype(BF16)
        gc = gc_ref[...].astype(F32)
        a_ref[...] = (gc * jax.nn.sigmoid(gc) * up).astype(BF16)

    tile = pl.BlockSpec((tm, tn), lambda j, i: (i, j))
    out = jax.ShapeDtypeStruct((s, f), BF16)
    return _pcall(
        body, comm=comm, name=name, grid=(f // tn, s // tm),
        in_specs=[_resident((s, d)), pl.BlockSpec((tn, d), lambda j, i: (j, 0)), tile],
        out_specs=[tile, tile], out_shape=[out, out],
        compiler_params=_params("arbitrary", "arbitrary"),
    )(hn, wut, gc)


def _ffn_fwd2(a, wd, h, gain, tm, name, comm=None):
    s, d = h.shape
    f = a.shape[1]

    def body(a_ref, wd_ref, h_ref, gain_ref, ho_ref, hn_ref):
        ho = h_ref[...] + _nn(a_ref[...], wd_ref[...])
        ho_ref[...] = ho
        hn_ref[...] = _rms(ho, gain_ref[...]).astype(BF16)

    row = lambda c: pl.BlockSpec((tm, c), lambda i: (i, 0))
    return _pcall(
        body, comm=comm, name=name, grid=(s // tm,),
        in_specs=[row(f), _whole(wd.shape), row(d), _whole((1, d))],
        out_specs=[row(d), row(d)],
        out_shape=[jax.ShapeDtypeStruct((s, d), F32), jax.ShapeDtypeStruct((s, d), BF16)],
        compiler_params=_params("arbitrary"),
    )(a, wd, h, gain)


def _odd_fwd(xn, h, win, sgu, ws, bfull, wout, gffn, tm, comm=None):
    s, d = h.shape
    e = win.shape[1]
    cw = e // 2
    heads = ws.shape[0]

    def body(xn_ref, h_ref, win_ref, sgu_ref, ws_ref, b_ref, wout_ref, gffn_ref,
             pre_ref, gate_ref, mixo_ref, ho_ref, hn_ref, gbuf):
        pre = _nn(xn_ref[...], win_ref[...])
        pre_ref[...] = pre.astype(BF16)
        p = _gelu(pre)
        u, v = p[:, :cw], p[:, cw:]
        vn = _rms(v, sgu_ref[...]).astype(BF16)
        for n in range(tm // CHUNK):
            rows = slice(n * CHUNK, (n + 1) * CHUNK)
            for hd in range(heads):
                cols = slice(hd * CHUNK, (hd + 1) * CHUNK)
                gbuf[rows, cols] = _nn(ws_ref[hd], vn[rows, cols]) + b_ref[:, cols]
        gate = gbuf[...]
        gate_ref[...] = gate.astype(BF16)
        mixo = (u * gate).astype(BF16)
        mixo_ref[...] = mixo
        ho = h_ref[...] + _nn(mixo, wout_ref[...])
        ho_ref[...] = ho
        hn_ref[...] = _rms(ho, gffn_ref[...]).astype(BF16)

    row = lambda c: pl.BlockSpec((tm, c), lambda i: (i, 0))
    return _pcall(
        body, comm=comm, name="odd_fwd", grid=(s // tm,),
        in_specs=[row(d), row(d), _whole(win.shape), _whole(sgu.shape), _whole(ws.shape), _whole(bfull.shape),
                  _whole(wout.shape), _whole((1, d))],
        out_specs=[row(e), row(cw), row(cw), row(d), row(d)],
        out_shape=[jax.ShapeDtypeStruct((s, e), BF16), jax.ShapeDtypeStruct((s, cw), BF16), jax.ShapeDtypeStruct((s, cw), BF16),
                   jax.ShapeDtypeStruct((s, d), F32), jax.ShapeDtypeStruct((s, d), BF16)],
        scratch_shapes=[pltpu.VMEM((tm, cw), F32)],
        compiler_params=_params("arbitrary"),
    )(xn, h, win, sgu, ws, bfull, wout, gffn)


def _loss_bwd(h, gain, target, tm):
    s, d = h.shape

    def body(h_ref, gain_ref, t_ref, dh_ref, dhb_ref, dgain_ref, loss_ref):
        i = pl.program_id(0)
        hv = h_ref[...]
        gain = gain_ref[...]
        err = _rms(hv, gain) - t_ref[...]
        dy = err * (1.0 / d)
        dx, dgain = _rms_bwd(dy, hv, gain)
        dh_ref[...] = dx
        dhb_ref[...] = dx.astype(BF16)
        _acc(dgain_ref, dgain, i == 0)
        _acc(loss_ref, jnp.sum(err * err, axis=0, keepdims=True) * (0.5 / d), i == 0)

    row = pl.BlockSpec((tm, d), lambda i: (i, 0))
    return _pcall(
        body, name="loss_bwd", grid=(s // tm,),
        in_specs=[row, _whole((1, d)), row],
        out_specs=[row, row, _whole((1, d)), _whole((1, d))],
        out_shape=[jax.ShapeDtypeStruct((s, d), F32), jax.ShapeDtypeStruct((s, d), BF16), jax.ShapeDtypeStruct((1, d), F32),
                   jax.ShapeDtypeStruct((1, d), F32)],
        compiler_params=_params("arbitrary"),
    )(h, gain, target)


def _ffn_bwd1(dhb, g, gc, up, wd, cw, tm, tn, name, comm=None):
    s, d = dhb.shape
    f = g.shape[1]
    ni = s // tm

    def body(dh_ref, g_ref, gc_ref, up_ref, wd_ref, cw_ref, dg_ref, dup_ref, dcw_ref, dcb_ref, ebuf):
        i = pl.program_id(1)

        @pl.when(i == 0)
        def _():
            ebuf[tm:tm + HALO, :] = jnp.zeros((HALO, tn), F32)

        da = _nt(dh_ref[_rows(ni - 1 - i, tm), :], wd_ref[...])
        gcv = gc_ref[...].astype(F32)
        sg = jax.nn.sigmoid(gcv)
        silu = gcv * sg
        dup_ref[...] = (da * silu).astype(BF16)
        dgc = da * up_ref[...].astype(F32) * (sg * (1.0 - silu) + silu)
        ebuf[0:tm, :] = dgc
        ext = ebuf[...]
        s1 = pltpu.roll(ext, tm + HALO - 1, axis=0)[0:tm, :]
        s2 = pltpu.roll(ext, tm + HALO - 2, axis=0)[0:tm, :]
        dg_ref[...] = (cw_ref[2:3, :] * dgc + cw_ref[1:2, :] * s1 + cw_ref[0:1, :] * s2).astype(BF16)
        gv = g_ref[...].astype(F32)
        for k, shifted in enumerate((s2, s1, dgc)):
            _acc(dcw_ref.at[k:k + 1, :], jnp.sum(shifted * gv, axis=0, keepdims=True), i == 0)
        _acc(dcb_ref, jnp.sum(dgc, axis=0, keepdims=True), i == 0)
        ebuf[tm:tm + HALO, :] = ebuf[0:HALO, :]

    tile = pl.BlockSpec((tm, tn), lambda j, i: (ni - 1 - i, j))
    wcol = lambda r: pl.BlockSpec((r, tn), lambda j, i: (0, j))
    out = jax.ShapeDtypeStruct((s, f), BF16)
    return _pcall(
        body, comm=comm, name=name, grid=(f // tn, ni),
        in_specs=[_resident((s, d)), tile, tile, tile,
                  pl.BlockSpec((tn, d), lambda j, i: (j, 0)), wcol(3)],
        out_specs=[tile, tile, wcol(3), wcol(1)],
        out_shape=[out, out, jax.ShapeDtypeStruct((3, f), F32), jax.ShapeDtypeStruct((1, f), F32)],
        scratch_shapes=[pltpu.VMEM((tm + HALO, tn), F32)],
        compiler_params=_params("arbitrary", "arbitrary"),
    )(dhb, g, gc, up, wd, cw)


def _ffn_bwd2(dg, dup, wg, wu, h, gain, dh, tm, name, comm=None):
    s, d = h.shape
    f = dg.shape[1]

    def body(dg_ref, dup_ref, wg_ref, wu_ref, h_ref, gain_ref, dh_ref, dho_ref, dhb_ref, dgain_ref):
        dhn = _nn(dg_ref[...], wg_ref[...]) + _nn(dup_ref[...], wu_ref[...])
        dx, dgain = _rms_bwd(dhn, h_ref[...], gain_ref[...])
        dho = dh_ref[...] + dx
        dho_ref[...] = dho
        dhb_ref[...] = dho.astype(BF16)
        _acc(dgain_ref, dgain, pl.program_id(0) == 0)

    row = lambda c: pl.BlockSpec((tm, c), lambda i: (i, 0))
    return _pcall(
        body, comm=comm, name=name, grid=(s // tm,),
        in_specs=[row(f), row(f), _whole(wg.shape), _whole(wu.shape), row(d), _whole((1, d)), row(d)],
        out_specs=[row(d), row(d), _whole((1, d))],
        out_shape=[jax.ShapeDtypeStruct((s, d), F32), jax.ShapeDtypeStruct((s, d), BF16), jax.ShapeDtypeStruct((1, d), F32)],
        compiler_params=_params("arbitrary"),
    )(dg, dup, wg, wu, h, gain, dh)


def _odd_bwd(dhb, dh, wout, pre, gate, ws, wst, sgu, win, h, gmix, tm, comm=None):
    s, d = h.shape
    e = win.shape[1]
    cw = e // 2
    heads = ws.shape[0]
    ni = s // tm

    def body(dhb_ref, dh_ref, wout_ref, pre_ref, gate_ref, ws_ref, wst_ref, sgu_ref, win_ref, h_ref, gmix_ref,
             dpre_ref, dho_ref, dhob_ref, dgain_ref, dsgu_ref, dws_ref, db_ref, vbuf, gacc):
        i = pl.program_id(0)
        first = i == 0
        dmixo = _nt(dhb_ref[...], wout_ref[...])
        pre = pre_ref[...].astype(F32)
        p = _gelu(pre)
        u, v = p[:, :cw], p[:, cw:]
        sgu = sgu_ref[...]
        rv = lax.rsqrt(jnp.mean(v * v, axis=-1, keepdims=True) + EPS)
        vh = v * rv
        vn = (vh * sgu).astype(BF16)
        du = dmixo * gate_ref[...].astype(F32)
        dgate = dmixo * u
        dgate_b = dgate.astype(BF16)
        gsum = dgate[0:CHUNK, :]
        for n in range(1, tm // CHUNK):
            gsum = gsum + dgate[n * CHUNK:(n + 1) * CHUNK, :]
        _acc(gacc, gsum, first)
        for hd in range(heads):
            cols = slice(hd * CHUNK, (hd + 1) * CHUNK)
            dws = None
            for n in range(tm // CHUNK):
                rows = slice(n * CHUNK, (n + 1) * CHUNK)
                vbuf[rows, cols] = _nn(wst_ref[hd], dgate_b[rows, cols])
                part = _nt(dgate_b[rows, cols], vn[rows, cols])
                dws = part if dws is None else dws + part
            _acc(dws_ref.at[hd], dws, first)
        dvn = vbuf[...]
        _acc(dsgu_ref, jnp.sum(dvn * vh, axis=0, keepdims=True), first)
        dvh = dvn * sgu
        dv = rv * (dvh - vh * jnp.mean(dvh * vh, axis=-1, keepdims=True))
        dpre = (jnp.concatenate([du, dv], axis=1) * _gelu_grad(pre)).astype(BF16)
        dpre_ref[...] = dpre
        dx, dgain = _rms_bwd(_nt(dpre, win_ref[...]), h_ref[...], gmix_ref[...])
        dho = dh_ref[...] + dx
        dho_ref[...] = dho
        dhob_ref[...] = dho.astype(BF16)
        _acc(dgain_ref, dgain, first)

        @pl.when(i == ni - 1)
        def _():
            ones = jnp.ones((8, CHUNK), F32)
            for hd in range(heads):
                tot = lax.dot_general(ones, gacc[:, hd * CHUNK:(hd + 1) * CHUNK], (((1,), (1,)), ((), ())),
                                      preferred_element_type=F32, precision=lax.Precision.HIGHEST)
                db_ref[hd:hd + 1, :] = tot[0:1, :]

    row = lambda c: pl.BlockSpec((tm, c), lambda i: (i, 0))
    return _pcall(
        body, comm=comm, name="odd_bwd", grid=(ni,),
        in_specs=[row(d), row(d), _whole(wout.shape), row(e), row(cw), _whole(ws.shape), _whole(wst.shape), _whole(sgu.shape),
                  _whole(win.shape), row(d), _whole((1, d))],
        out_specs=[row(e), row(d), row(d), _whole((1, d)), _whole((1, cw)), _whole(ws.shape), _whole((heads, CHUNK))],
        out_shape=[jax.ShapeDtypeStruct((s, e), BF16), jax.ShapeDtypeStruct((s, d), F32), jax.ShapeDtypeStruct((s, d), BF16),
                   jax.ShapeDtypeStruct((1, d), F32), jax.ShapeDtypeStruct((1, cw), F32), jax.ShapeDtypeStruct(ws.shape, F32),
                   jax.ShapeDtypeStruct((heads, CHUNK), F32)],
        scratch_shapes=[pltpu.VMEM((tm, cw), F32), pltpu.VMEM((CHUNK, cw), F32)],
        compiler_params=_params("arbitrary"),
    )(dhb, dh, wout, pre, gate, ws, wst, sgu, win, h, gmix)


def _even_bwd(dhb, dh, wout, proj, cq, pooled, conva, wpool, wpoolt, pscale, win, x, gmix, tm, comm=None):
    s, d = x.shape
    e = win.shape[1]
    aw = e // 4
    ni = s // tm

    def body(dhb_ref, dh_ref, wout_ref, proj_ref, cq_ref, pooled_ref, ca_ref, wp_ref, wpt_ref, ps_ref, win_ref, x_ref, gmix_ref,
             dproj_ref, dx_ref, dgain_ref, dca_ref, dwp_ref, dps_ref, cbuf, ebuf):
        i = pl.program_id(0)
        first = i == 0

        @pl.when(first)
        def _():
            cbuf[tm:tm + HALO, :] = jnp.zeros((HALO, aw), F32)
            ebuf[tm:tm + HALO, :] = jnp.zeros((HALO, aw), F32)

        dmix = _nt(dhb_ref[...], wout_ref[...])
        dy_a, dy_b = dmix[:, :aw], dmix[:, aw:]
        proj = proj_ref[...].astype(F32)
        a_b, a_c, a_v = (proj[:, k * aw:(k + 1) * aw] for k in range(3))
        da_b = dy_a * cq_ref[...].astype(F32)
        dcq = dy_a * a_b
        cbuf[0:tm, :] = dcq
        cext = cbuf[...]
        s1 = _shift(cext, 1, False)[0:tm, :]
        s2 = _shift(cext, 2, False)[0:tm, :]
        q = a_c * a_v
        for k, shifted in enumerate((s2, s1, dcq)):
            _acc(dca_ref.at[k:k + 1, :], jnp.sum(shifted * q, axis=0, keepdims=True), first)
        dq = ca_ref[2:3, :] * dcq + ca_ref[1:2, :] * s1 + ca_ref[0:1, :] * s2
        da_c = dq * a_v
        da_v = dq * a_c
        dps, dpool = [], []
        for g, w in enumerate(WINDOWS):
            cols = slice(g * LANE, (g + 1) * LANE)
            pooled = pooled_ref[:, cols]
            mixed = _nn(pooled, wp_ref[g])
            dps.append(jnp.sum(dy_b[:, cols] * mixed, axis=0, keepdims=True))
            dmixed = (dy_b[:, cols] * ps_ref[:, cols]).astype(BF16)
            _acc(dwp_ref.at[g], _tn(pooled, dmixed), first)
            dp = _nn(dmixed, wpt_ref[g])
            dpool.append(dp)
            ebuf[0:tm, cols] = dp / _counts((ni - 1 - i) * tm, tm, w)
        _acc(dps_ref, jnp.concatenate(dps, axis=1), first)
        eext = ebuf[...]
        dzs = []
        for g, w in enumerate(WINDOWS):
            cols = slice(g * LANE, (g + 1) * LANE)
            dzs.append(_window_sum(eext[:, cols], w, False)[0:tm, :] - dpool[g])
        dproj = jnp.concatenate([da_b, da_c, da_v] + dzs, axis=1).astype(BF16)
        dproj_ref[...] = dproj
        dx, dgain = _rms_bwd(_nt(dproj, win_ref[...]), x_ref[...], gmix_ref[...])
        dx_ref[...] = dh_ref[...] + dx
        _acc(dgain_ref, dgain, first)
        cbuf[tm:tm + HALO, :] = cbuf[0:HALO, :]
        ebuf[tm:tm + HALO, :] = ebuf[0:HALO, :]

    row = lambda c: pl.BlockSpec((tm, c), lambda i: (ni - 1 - i, 0))
    return _pcall(
        body, comm=comm, name="even_bwd", grid=(ni,),
        in_specs=[row(d), row(d), _whole(wout.shape), row(e), row(aw), row(aw), _whole(conva.shape), _whole(wpool.shape),
                  _whole(wpoolt.shape), _whole(pscale.shape), _whole(win.shape), row(d), _whole((1, d))],
        out_specs=[row(e), row(d), _whole((1, d)), _whole(conva.shape), _whole(wpool.shape), _whole(pscale.shape)],
        out_shape=[jax.ShapeDtypeStruct((s, e), BF16), jax.ShapeDtypeStruct((s, d), F32), jax.ShapeDtypeStruct((1, d), F32),
                   jax.ShapeDtypeStruct(conva.shape, F32), jax.ShapeDtypeStruct(wpool.shape, F32),
                   jax.ShapeDtypeStruct(pscale.shape, F32)],
        scratch_shapes=[pltpu.VMEM((tm + HALO, aw), F32), pltpu.VMEM((tm + HALO, aw), F32)],
        compiler_params=_params("arbitrary"),
    )(dhb, dh, wout, proj, cq, pooled, conva, wpool, wpoolt, pscale, win, x, gmix)


def _wgrad(a, b, tk, ts, name, comm=None):
    s, ka = a.shape
    nb = b.shape[1]
    nt = s // ts

    def body(a_ref, b_ref, o_ref, acc):
        t = pl.program_id(1)
        _acc(acc, _tn(a_ref[...], b_ref[...]), t == 0)

        @pl.when(t == nt - 1)
        def _():
            o_ref[...] = acc[...].astype(BF16)

    return _pcall(
        body, comm=comm, name=name, grid=(ka // tk, nt),
        in_specs=[pl.BlockSpec((ts, tk), lambda k, t: (t, k)), pl.BlockSpec((ts, nb), lambda k, t: (t, 0))],
        out_specs=pl.BlockSpec((tk, nb), lambda k, t: (k, 0)),
        out_shape=jax.ShapeDtypeStruct((ka, nb), BF16),
        scratch_shapes=[pltpu.VMEM((tk, nb), F32)],
        compiler_params=_params("arbitrary", "arbitrary"),
    )(a, b)


def _adamw(parts, w, m, v, tr, name, layer=None, into=None):
    r, c = w.shape[-2:]
    n, rp, cp = parts.shape
    assert rp >= r and r % tr == 0

    def body(p_ref, w_ref, m_ref, v_ref, *rest):
        g_ref, d_ref, mo_ref, vo_ref = rest[-4:]
        g = p_ref[0, 0:tr, 0:c].astype(F32)
        for j in range(1, n):
            g = g + p_ref[j, 0:tr, 0:c].astype(F32)
        g_ref[...] = g
        mn = ADAM_B1 * m_ref[...] + (1.0 - ADAM_B1) * g
        vn = ADAM_B2 * v_ref[...] + (1.0 - ADAM_B2) * (g * g)
        mo_ref[...] = mn
        vo_ref[...] = vn
        m_hat = mn / (1.0 - ADAM_B1 ** ADAM_STEP)
        v_hat = vn / (1.0 - ADAM_B2 ** ADAM_STEP)
        d_ref[...] = -ADAM_LR * (m_hat / (jnp.sqrt(v_hat) + ADAM_EPS) + ADAM_WD * w_ref[...])

    if layer is None:
        row = pl.BlockSpec((tr, c), lambda i: (i, 0))
    else:
        row = pl.BlockSpec((None, tr, c), lambda i: (layer, i, 0))
    out = jax.ShapeDtypeStruct(w.shape, F32)
    prev = [] if into is None else list(into)
    return _pcall(
        body, name=name, grid=(r // tr,),
        in_specs=[pl.BlockSpec((n, tr, cp), lambda i: (0, i, 0)), row, row, row] + [pl.BlockSpec(memory_space=pl.ANY)] * len(prev),
        out_specs=[row, row, row, row], out_shape=[out, out, out, out],
        input_output_aliases={4 + k: k for k in range(len(prev))},
        compiler_params=_params("arbitrary"),
    )(parts, w, m, v, *prev)


def _pair_add(grad, other, spec, core, name):
    axis, width = spec
    slot = other.shape[1:]

    def body(core_ref, g_ref, o_ref, out_ref):
        out_ref[...] = (g_ref[...].astype(F32) + o_ref[...].astype(F32)).astype(BF16)

    if axis == 0:
        gspec = pl.BlockSpec(slot, lambda q, core_ref: (2 * q + core_ref[0], 0))
    else:
        gspec = pl.BlockSpec(slot, lambda q, core_ref: (0, 2 * q + core_ref[0]))
    per_chip = pl.BlockSpec((None,) + slot, lambda q, core_ref: (q, 0, 0))
    return _pcall(
        body, name=name,
        grid_spec=pltpu.PrefetchScalarGridSpec(num_scalar_prefetch=1, grid=(N_CHIP,), in_specs=[gspec, per_chip], out_specs=per_chip),
        out_shape=jax.ShapeDtypeStruct(other.shape, BF16),
        compiler_params=_params("arbitrary"),
    )(core, grad, other)


def _hbm():
    return pl.BlockSpec(memory_space=pltpu.HBM)


def _window(ref, spec, j):
    axis, width = spec
    start = pl.multiple_of(j * width, width)
    return ref.at[(slice(None),) * axis + (pl.ds(start, width),)]


def _here():
    return lax.axis_index("x"), lax.axis_index("y"), lax.axis_index("c")


class _Gather:
    def __init__(self, shards, specs, fulls):
        n = len(shards)
        self.ins, self.specs, self.out_shape = list(shards), list(specs), list(fulls)
        self.sems = [pltpu.SemaphoreType.DMA((7 * n,)), pltpu.SemaphoreType.DMA((7 * n,)), pltpu.SemaphoreType.DMA((n,))]

    def _plan(self, ins, outs, sems):
        send_sems, recv_sems, local_sems = sems
        x, y, c = _here()
        me, sibling = (x, y, c), (x, y, 1 - c)
        chips = [(1 - x, y), (x, 1 - y), (1 - x, 1 - y)]

        def slot(t, dev):
            return _window(outs[t], self.specs[t], 4 * dev[0] + 2 * dev[1] + dev[2])

        def copy(t, k, block, to, src=None):
            return pltpu.make_async_remote_copy(
                src_ref=slot(t, block) if src is None else src, dst_ref=slot(t, block),
                send_sem=send_sems.at[7 * t + k], recv_sem=recv_sems.at[7 * t + k], device_id=to, device_id_type=MESH)

        plan = []
        for t in range(len(ins)):
            plan.append(dict(
                mine=pltpu.make_async_copy(ins[t], slot(t, me), local_sems.at[t]),
                first=[copy(t, 0, me, sibling, src=ins[t])] + [copy(t, 1 + j, me, (*q, c), src=ins[t]) for j, q in enumerate(chips)],
                over_ici=[copy(t, 1 + j, (*q, c), me) for j, q in enumerate(chips)],
                passed=[copy(t, 4 + j, (*q, c), sibling) for j, q in enumerate(chips)],
                from_sibling=[copy(t, 0, sibling, me)] + [copy(t, 4 + j, (*q, 1 - c), me) for j, q in enumerate(chips)]))
        return plan

    def start(self, ins, outs, sems):
        for p in self._plan(ins, outs, sems):
            p["mine"].start()
            for cp in p["first"]:
                cp.start()

    def middle(self, ins, outs, sems):
        for p in self._plan(ins, outs, sems):
            for arrived, onward in zip(p["over_ici"], p["passed"]):
                arrived.wait_recv()
                onward.start()

    def finish(self, ins, outs, sems):
        plan = self._plan(ins, outs, sems)
        for p in plan:
            for cp in p["from_sibling"]:
                cp.wait_recv()
        for p in plan:
            for cp in p["first"] + p["passed"]:
                cp.wait_send()
            p["mine"].wait()


class _PairExchange:
    def __init__(self, grads, specs):
        n = len(grads)
        self.ins, self.specs = list(grads), list(specs)
        self.out_shape = [jax.ShapeDtypeStruct((N_CHIP,) + a.shape[:sp[0]] + (sp[1],) + a.shape[sp[0] + 1:], a.dtype)
                          for a, sp in zip(grads, specs)]
        self.sems = [pltpu.SemaphoreType.DMA((n,)), pltpu.SemaphoreType.DMA((n,))]

    def start(self, ins, outs, sems):
        send_sems, recv_sems = sems
        x, y, c = _here()
        for t in range(len(ins)):
            for q in range(N_CHIP):
                pltpu.make_async_remote_copy(
                    src_ref=_window(ins[t], self.specs[t], 2 * q + (1 - c)), dst_ref=outs[t].at[q],
                    send_sem=send_sems.at[t], recv_sem=recv_sems.at[t], device_id=(x, y, 1 - c), device_id_type=MESH).start()

    def middle(self, ins, outs, sems):
        pass

    def finish(self, ins, outs, sems):
        send_sems, recv_sems = sems
        x, y, c = _here()
        for t in range(len(ins)):
            every = pltpu.make_async_remote_copy(src_ref=outs[t], dst_ref=outs[t], send_sem=send_sems.at[t],
                                                 recv_sem=recv_sems.at[t], device_id=(x, y, 1 - c), device_id_type=MESH)
            every.wait_send()
            every.wait_recv()


class _ChipExchange:
    def __init__(self, pairs, slotted=(), whole=()):
        self.ins = list(pairs) + list(slotted) + list(whole)
        self.npair, self.nslot = len(pairs), len(pairs) + len(slotted)
        n = len(self.ins)
        self.out_shape = ([jax.ShapeDtypeStruct(a.shape, a.dtype) for a in list(pairs) + list(slotted)]
                          + [jax.ShapeDtypeStruct((N_DEV,) + a.shape, a.dtype) for a in whole])
        self.sems = [pltpu.SemaphoreType.DMA((7 * n,)), pltpu.SemaphoreType.DMA((7 * n,)), pltpu.SemaphoreType.DMA((n,))]

    def _plan(self, ins, outs, sems):
        send_sems, recv_sems, local_sems = sems
        npair, nslot = self.npair, self.nslot
        x, y, c = _here()
        me, chip = 4 * x + 2 * y + c, 2 * x + y
        chips = [(1 - x, y), (x, 1 - y), (1 - x, 1 - y)]
        peers = [(x, y, 1 - c)] + [(*q, c) for q in chips] + [(*q, 1 - c) for q in chips]

        def index(dev):
            return 4 * dev[0] + 2 * dev[1] + dev[2]

        def copy(t, k, arriving):
            peer = peers[k]
            if t < npair:
                src, mine, theirs = ins[t].at[2 * peer[0] + peer[1]], chip, 2 * peer[0] + peer[1]
            else:
                src, mine, theirs = (ins[t].at[index(peer)] if t < nslot else ins[t]), me, index(peer)
            return pltpu.make_async_remote_copy(
                src_ref=src, dst_ref=outs[t].at[theirs if arriving else mine],
                send_sem=send_sems.at[7 * t + k], recv_sem=recv_sems.at[7 * t + k], device_id=peer, device_id_type=MESH)

        own, sent, arriving = [], [], []
        for t in range(len(ins)):
            fan = range(1, 4) if t < npair else range(7)
            if t < npair:
                own.append(pltpu.make_async_copy(ins[t].at[chip], outs[t].at[chip], local_sems.at[t]))
            else:
                own.append(pltpu.make_async_copy(ins[t].at[me] if t < nslot else ins[t], outs[t].at[me], local_sems.at[t]))
            sent += [copy(t, k, False) for k in fan]
            arriving += [copy(t, k, True) for k in fan]
        return own, sent, arriving

    def start(self, ins, outs, sems):
        own, sent, _ = self._plan(ins, outs, sems)
        for cp in own + sent:
            cp.start()

    def middle(self, ins, outs, sems):
        pass

    def finish(self, ins, outs, sems):
        own, sent, arriving = self._plan(ins, outs, sems)
        for cp in arriving:
            cp.wait_recv()
        for cp in sent:
            cp.wait_send()
        for cp in own:
            cp.wait()


class _Jobs:
    def __init__(self, *jobs):
        self.jobs = jobs
        self.ins = [a for j in jobs for a in j.ins]
        self.out_shape = [a for j in jobs for a in j.out_shape]
        self.sems = [a for j in jobs for a in j.sems]

    def _split(self, ins, outs, sems):
        i = o = s = 0
        for j in self.jobs:
            yield j, ins[i:i + len(j.ins)], outs[o:o + len(j.out_shape)], sems[s:s + len(j.sems)]
            i, o, s = i + len(j.ins), o + len(j.out_shape), s + len(j.sems)

    def start(self, ins, outs, sems):
        for j, a, b, c in self._split(ins, outs, sems):
            j.start(a, b, c)

    def middle(self, ins, outs, sems):
        for j, a, b, c in self._split(ins, outs, sems):
            j.middle(a, b, c)

    def finish(self, ins, outs, sems):
        for j, a, b, c in self._split(ins, outs, sems):
            j.finish(a, b, c)

    def results(self, outs):
        return [b for _, _, b, _ in self._split((), outs, ())]


def _alone(job, name):
    return _pcall(lambda: None, comm=job, name=name, in_specs=[], out_specs=[], out_shape=[])()[1]


SMALL_ROWS = 24
REP_COLS = 1024


def _pad_to(a, rows, cols):
    return jnp.pad(a, ((0, rows - a.shape[0]), (0, cols - a.shape[1])))


def _pack_small(conv_a, sgu_norm, conv_ffn, cols):
    return jnp.concatenate([_pad_to(conv_a, 8, cols), _pad_to(sgu_norm, 8, cols),
                            _pad_to(conv_ffn.reshape(-1, conv_ffn.shape[-1]), 8, cols)], axis=0)


def _unpack_small(p, ca_w, sg_w, cf_w):
    return p[0:3, 0:ca_w], p[8:9, 0:sg_w], p[16:22, 0:cf_w].reshape(2, 3, cf_w)


def _tile_rows(rows):
    return -(-rows // 8) * 8


def _pack_rows(parts):
    return jnp.concatenate([_pad_to(a, _tile_rows(a.shape[0]), REP_COLS) for a in parts], axis=0)


def _unpack_rows(p, shapes):
    out, r0 = [], 0
    for r, c in shapes:
        out.append(p[r0:r0 + r, 0:c])
        r0 += _tile_rows(r)
    return out


def _rep_late(norm_mix0, norm_ffn0, pool_scale, b_conv0, w_pool):
    return _pack_rows([norm_mix0, norm_ffn0, pool_scale.reshape(1, -1), _pad_to(b_conv0, 1, 3 * REP_COLS).reshape(3, REP_COLS),
                       w_pool.reshape(-1, REP_COLS)])


def _rep_early(norm_mix1, norm_ffn1, final_norm, b_spatial, b_conv1, w_spatial, loss):
    return _pack_rows([norm_mix1, norm_ffn1, final_norm.reshape(1, -1), b_spatial.reshape(1, -1),
                       _pad_to(b_conv1, 1, 3 * REP_COLS).reshape(3, REP_COLS), w_spatial.reshape(-1, REP_COLS), loss])


def _unpack_rep(late, early, like):
    f = like["b_conv_ffn"].shape[1]
    nm0, nf0, ps, bc0, wp = _unpack_rows(late, [(1, REP_COLS), (1, REP_COLS), (1, like["pool_scale"].shape[1]), (3, REP_COLS),
                                                (like["w_pool"].size // REP_COLS, REP_COLS)])
    nm1, nf1, fin, bs, bc1, wsp, _ = _unpack_rows(early, [(1, REP_COLS)] * 4 + [(3, REP_COLS), (like["w_spatial"].size // REP_COLS, REP_COLS),
                                                          (1, REP_COLS)])
    return {
        "norm_mix": jnp.concatenate([nm0, nm1]), "norm_ffn": jnp.concatenate([nf0, nf1]), "final_norm": fin[0], "pool_scale": ps,
        "b_spatial": bs.reshape(like["b_spatial"].shape),
        "b_conv_ffn": jnp.concatenate([bc0.reshape(1, -1), bc1.reshape(1, -1)])[:, 0:f],
        "w_pool": wp.reshape(like["w_pool"].shape), "w_spatial": wsp.reshape(like["w_spatial"].shape),
    }


def _pad_slots(a, width, padded):
    a = a.reshape(*a.shape[:-1], N_DEV, width)
    a = jnp.pad(a, ((0, 0),) * (a.ndim - 1) + ((0, padded - width),))
    return a.reshape(*a.shape[:-2], N_DEV * padded)


def _unpad_slots(a, width, padded):
    a = a.reshape(*a.shape[:-1], N_DEV, padded)[..., 0:width]
    return a.reshape(*a.shape[:-2], N_DEV * width)


def kernel(x, norm_mix, norm_ffn, final_norm, w_in_even, conv_a, w_pool, pool_scale, w_out_even, w_in_odd, sgu_norm, w_spatial, b_spatial, w_out_odd, w_ffn_gate, w_ffn_up, conv_ffn, b_conv_ffn, w_ffn_down, loss_target, m_norm_mix, m_norm_ffn, m_final_norm, m_w_in_even, m_conv_a, m_w_pool, m_pool_scale, m_w_out_even, m_w_in_odd, m_sgu_norm, m_w_spatial, m_b_spatial, m_w_out_odd, m_w_ffn_gate, m_w_ffn_up, m_conv_ffn, m_b_conv_ffn, m_w_ffn_down, v_norm_mix, v_norm_ffn, v_final_norm, v_w_in_even, v_conv_a, v_w_pool, v_pool_scale, v_w_out_even, v_w_in_odd, v_sgu_norm, v_w_spatial, v_b_spatial, v_w_out_odd, v_w_ffn_gate, v_w_ffn_up, v_conv_ffn, v_b_conv_ffn, v_w_ffn_down):
    s, d = x.shape[1], x.shape[2]
    x2, target = x[0], loss_target[0]
    tm = min(512, s)
    tm_wide = min(2048, s)
    tn = 256
    row = lambda a: a.reshape(1, -1)
    ein, ro = w_in_even.shape[2], w_out_even.shape[1]
    fs = w_ffn_gate.shape[2]
    fsp = -(-fs // LANE) * LANE
    fp = N_DEV * fsp
    full = lambda shape, dtype=BF16: jax.ShapeDtypeStruct(shape, dtype)

    wg_s = [jnp.pad(w_ffn_gate[l].T, ((0, fsp - fs), (0, 0))).astype(BF16) for l in range(2)]
    wu_s = [jnp.pad(w_ffn_up[l].T, ((0, fsp - fs), (0, 0))).astype(BF16) for l in range(2)]
    wd_s = [jnp.pad(w_ffn_down[l], ((0, fsp - fs), (0, 0))).astype(BF16) for l in range(2)]
    small_s = _pack_small(conv_a[0], sgu_norm, conv_ffn, fsp)[None]
    in_spec, out_spec, gu_spec, down_spec = (1, ein), (0, ro), (0, fsp), (0, fsp)
    full_in, full_out, full_gu, full_down = full((d, N_DEV * ein)), full((N_DEV * ro, d)), full((fp, d)), full((fp, d))
    win_e, wout_e, gsmall = _alone(_Gather([w_in_even[0].astype(BF16), w_out_even[0].astype(BF16), small_s], [in_spec, out_spec, (0, 1)],
                                           [full_in, full_out, full((N_DEV, SMALL_ROWS, fsp), F32)]), "gather_mix0")
    ca_full = jnp.moveaxis(gsmall[:, 0:3, 0:conv_a.shape[2]], 0, 1).reshape(3, -1)
    sgu_full = gsmall[:, 8, 0:sgu_norm.shape[1]].reshape(1, -1)
    cf_full = jnp.moveaxis(gsmall[:, 16:22, :].reshape(N_DEV, 2, 3, fsp), 0, 2).reshape(2, 3, fp)
    cb_full = _pad_slots(b_conv_ffn, fs, fsp)

    tril = jnp.tril(jnp.ones((CHUNK, CHUNK), F32))
    ws_m = w_spatial[0] * tril
    ws_b = ws_m.astype(BF16)
    wst_b = jnp.swapaxes(ws_m, 1, 2).astype(BF16)
    bfull = jnp.repeat(b_spatial[0].T, CHUNK, axis=1)
    wpool_b = w_pool[0].astype(BF16)
    wpoolt_b = jnp.swapaxes(w_pool[0], 1, 2).astype(BF16)

    (xn0, proj0, cq0, pooled0, mix0, h1, hn0), (wg0,) = _even_fwd(
        x2, norm_mix[0:1], norm_ffn[0:1], win_e, ca_full, wpool_b, pool_scale, wout_e, tm,
        comm=_Gather([wg_s[0]], [gu_spec], [full_gu]))
    (g0, gc0), (wu0,) = _ffn_gate(hn0, wg0, cf_full[0], cb_full[0:1], tm_wide, tn, "ffn_gate_l0", comm=_Gather([wu_s[0]], [gu_spec], [full_gu]))
    (up0, a0), (wd0,) = _ffn_up(hn0, wu0, gc0, tm_wide, tn, "ffn_up_l0", comm=_Gather([wd_s[0]], [down_spec], [full_down]))
    (h2, xn1), (win_o, wout_o) = _ffn_fwd2(a0, wd0, h1, norm_mix[1:2], tm, "ffn_fwd2_l0", comm=_Gather(
        [w_in_odd[0].astype(BF16), w_out_odd[0].astype(BF16)], [in_spec, out_spec], [full_in, full_out]))
    (pre1, gate1, mixo1, h3, hn1), (wg1,) = _odd_fwd(xn1, h2, win_o, sgu_full, ws_b, bfull, wout_o, norm_ffn[1:2], tm,
                                                    comm=_Gather([wg_s[1]], [gu_spec], [full_gu]))
    (g1, gc1), (wu1,) = _ffn_gate(hn1, wg1, cf_full[1], cb_full[1:2], tm_wide, tn, "ffn_gate_l1", comm=_Gather([wu_s[1]], [gu_spec], [full_gu]))
    (up1, a1), (wd1,) = _ffn_up(hn1, wu1, gc1, tm_wide, tn, "ffn_up_l1", comm=_Gather([wd_s[1]], [down_spec], [full_down]))
    h4, _ = _ffn_fwd2(a1, wd1, h3, row(final_norm), tm, "ffn_fwd2_l1")

    core = lax.axis_index("c").astype(jnp.int32).reshape(1)
    ts = min(1024, s)
    dh4, dh4b, d_final, lossvec = _loss_bwd(h4, row(final_norm), target, tm)
    gd1 = _wgrad(a1, dh4b, tn, s, "wgrad_down_l1")
    (dg1, dup1, dcw1, dcb1), (o_d1,) = _ffn_bwd1(dh4b, g1, gc1, up1, wd1, cf_full[1], tm_wide, tn, "ffn_bwd1_l1",
                                                 comm=_PairExchange([gd1], [down_spec]))
    p_d1 = _pair_add(gd1, o_d1, down_spec, core, "pair_add_down_l1")
    gg1 = _wgrad(dg1, hn1, tn, s, "wgrad_gate_l1")
    gu1, (o_g1,) = _wgrad(dup1, hn1, tn, s, "wgrad_up_l1", comm=_PairExchange([gg1], [gu_spec]))
    p_g1 = _pair_add(gg1, o_g1, gu_spec, core, "pair_add_gate_l1")
    jobs = _Jobs(_ChipExchange([p_d1]), _PairExchange([gu1], [gu_spec]))
    (dh3, dh3b, d_nffn1), res = _ffn_bwd2(dg1, dup1, wg1, wu1, h3, norm_ffn[1:2], dh4, tm // 2, "ffn_bwd2_l1", comm=jobs)
    (s_d1,), (o_u1,) = jobs.results(res)
    p_u1 = _pair_add(gu1, o_u1, gu_spec, core, "pair_add_up_l1")
    (dpre1, dh2, dh2b, d_nmix1, d_sgu, d_ws, d_b), (s_g1, s_u1) = _odd_bwd(
        dh3b, dh3, wout_o, pre1, gate1, ws_b, wst_b, sgu_full, win_o, h2, norm_mix[1:2], tm, comm=_ChipExchange([p_g1, p_u1]))
    gi1 = _wgrad(xn1, dpre1, 512, ts, "wgrad_in_odd")
    go1, (o_i1,) = _wgrad(mixo1, dh3b, 512, ts, "wgrad_out_odd", comm=_PairExchange([gi1], [in_spec]))
    p_i1 = _pair_add(gi1, o_i1, in_spec, core, "pair_add_in_odd")
    gd0, (o_o1,) = _wgrad(a0, dh2b, tn, s, "wgrad_down_l0", comm=_PairExchange([go1], [out_spec]))
    p_o1 = _pair_add(go1, o_o1, out_spec, core, "pair_add_out_odd")
    d_early = _rep_early(d_nmix1, d_nffn1, d_final, d_b, _unpad_slots(dcb1, fs, fsp), d_ws * tril, lossvec)
    jobs = _Jobs(_ChipExchange([p_i1, p_o1], [], [d_early]), _PairExchange([gd0], [down_spec]))
    (dg0, dup0, dcw0, dcb0), res = _ffn_bwd1(dh2b, g0, gc0, up0, wd0, cf_full[0], tm_wide, tn, "ffn_bwd1_l0", comm=jobs)
    (s_i1, s_o1, r_early), (o_d0,) = jobs.results(res)
    p_d0 = _pair_add(gd0, o_d0, down_spec, core, "pair_add_down_l0")
    gg0, (s_d0,) = _wgrad(dg0, hn0, tn, s, "wgrad_gate_l0", comm=_ChipExchange([p_d0]))
    gu0, (o_g0,) = _wgrad(dup0, hn0, tn, s, "wgrad_up_l0", comm=_PairExchange([gg0], [gu_spec]))
    p_g0 = _pair_add(gg0, o_g0, gu_spec, core, "pair_add_gate_l0")
    jobs = _Jobs(_ChipExchange([p_g0]), _PairExchange([gu0], [gu_spec]))
    (dh1, dh1b, d_nffn0), res = _ffn_bwd2(dg0, dup0, wg0, wu0, h1, norm_ffn[0:1], dh2, tm // 2, "ffn_bwd2_l0", comm=jobs)
    (s_g0,), (o_u0,) = jobs.results(res)
    p_u0 = _pair_add(gu0, o_u0, gu_spec, core, "pair_add_up_l0")
    go0 = _wgrad(mix0, dh1b, 512, ts, "wgrad_out_even")
    jobs = _Jobs(_ChipExchange([p_u0]), _PairExchange([go0], [out_spec]))
    (dproj0, grad_x, d_nmix0, d_ca, d_wp, d_ps), res = _even_bwd(
        dh1b, dh1, wout_e, proj0, cq0, pooled0, ca_full, wpool_b, wpoolt_b, pool_scale, win_e, x2, norm_mix[0:1], tm, comm=jobs)
    (s_u0,), (o_o0,) = jobs.results(res)
    p_o0 = _pair_add(go0, o_o0, out_spec, core, "pair_add_out_even")
    d_small = jnp.stack([_pack_small(a, b, c, fsp) for a, b, c in zip(
        jnp.moveaxis(d_ca.reshape(3, N_DEV, -1), 1, 0), jnp.moveaxis(d_sgu.reshape(1, N_DEV, -1), 1, 0),
        jnp.moveaxis(jnp.stack([dcw0, dcw1]).reshape(2, 3, N_DEV, fsp), 2, 0))])
    d_late = _rep_late(d_nmix0, d_nffn0, d_ps, _unpad_slots(dcb0, fs, fsp), d_wp)
    gi0, (s_o0, r_small, r_late) = _wgrad(xn0, dproj0, 512, ts, "wgrad_in_even", comm=_ChipExchange([p_o0], [d_small], [d_late]))
    (o_i0,) = _alone(_PairExchange([gi0], [in_spec]), "pair_exchange_in_even")
    p_i0 = _pair_add(gi0, o_i0, in_spec, core, "pair_add_in_even")
    (s_i0,) = _alone(_ChipExchange([p_i0]), "chip_exchange_last")
    loss = jnp.sum(r_early[:, r_early.shape[1] - 8, :])

    out = {}
    out["w_in_even"] = _adamw(s_i0, w_in_even[0], m_w_in_even[0], v_w_in_even[0], 256, "adamw_in_even")
    out["w_out_even"] = _adamw(s_o0, w_out_even[0], m_w_out_even[0], v_w_out_even[0], ro // 2, "adamw_out_even")
    out["w_in_odd"] = _adamw(s_i1, w_in_odd[0], m_w_in_odd[0], v_w_in_odd[0], 256, "adamw_in_odd")
    out["w_out_odd"] = _adamw(s_o1, w_out_odd[0], m_w_out_odd[0], v_w_out_odd[0], ro // 2, "adamw_out_odd")
    tp = lambda a: jnp.swapaxes(a, 1, 2)
    for nm, s1, s0, w, m, v, back in (("w_ffn_gate", s_g1, s_g0, tp(w_ffn_gate), tp(m_w_ffn_gate), tp(v_w_ffn_gate), tp),
                                      ("w_ffn_up", s_u1, s_u0, tp(w_ffn_up), tp(m_w_ffn_up), tp(v_w_ffn_up), tp),
                                      ("w_ffn_down", s_d1, s_d0, w_ffn_down, m_w_ffn_down, v_w_ffn_down, lambda a: a)):
        l1 = _adamw(s1, w, m, v, fs // 2, "adamw_%s_l1" % nm, layer=1)
        out[nm] = [back(a) for a in _adamw(s0, w, m, v, fs // 2, "adamw_%s_l0" % nm, layer=0, into=l1)]
    small = _adamw(r_small, _pack_small(conv_a[0], sgu_norm, conv_ffn, fsp), _pack_small(m_conv_a[0], m_sgu_norm, m_conv_ffn, fsp),
                   _pack_small(v_conv_a[0], v_sgu_norm, v_conv_ffn, fsp), SMALL_ROWS, "adamw_small")
    no_loss = jnp.zeros((1, REP_COLS), F32)
    early = _adamw(r_early, *[_rep_early(nm[1:2], nf[1:2], fn, bs, bc[1:2], wsp, no_loss) for nm, nf, fn, bs, bc, wsp in (
        (norm_mix, norm_ffn, final_norm, b_spatial, b_conv_ffn, w_spatial), (m_norm_mix, m_norm_ffn, m_final_norm, m_b_spatial, m_b_conv_ffn, m_w_spatial),
        (v_norm_mix, v_norm_ffn, v_final_norm, v_b_spatial, v_b_conv_ffn, v_w_spatial))], r_early.shape[1], "adamw_replicated_early")
    late = _adamw(r_late, *[_rep_late(nm[0:1], nf[0:1], ps, bc[0:1], wp) for nm, nf, ps, bc, wp in (
        (norm_mix, norm_ffn, pool_scale, b_conv_ffn, w_pool), (m_norm_mix, m_norm_ffn, m_pool_scale, m_b_conv_ffn, m_w_pool),
        (v_norm_mix, v_norm_ffn, v_pool_scale, v_b_conv_ffn, v_w_pool))], r_late.shape[1], "adamw_replicated_late")

    names = ["norm_mix", "norm_ffn", "final_norm", "w_in_even", "conv_a", "w_pool", "pool_scale", "w_out_even", "w_in_odd", "sgu_norm",
             "w_spatial", "b_spatial", "w_out_odd", "w_ffn_gate", "w_ffn_up", "conv_ffn", "b_conv_ffn", "w_ffn_down"]
    like = {"norm_mix": norm_mix, "norm_ffn": norm_ffn, "final_norm": final_norm, "w_in_even": w_in_even, "conv_a": conv_a,
            "w_pool": w_pool, "pool_scale": pool_scale, "w_out_even": w_out_even, "w_in_odd": w_in_odd, "sgu_norm": sgu_norm,
            "w_spatial": w_spatial, "b_spatial": b_spatial, "w_out_odd": w_out_odd, "w_ffn_gate": w_ffn_gate, "w_ffn_up": w_ffn_up,
            "conv_ffn": conv_ffn, "b_conv_ffn": b_conv_ffn, "w_ffn_down": w_ffn_down}
    groups = []
    for k in range(4):
        ca_k, sg_k, cf_k = _unpack_small(small[k], conv_a.shape[2], sgu_norm.shape[1], conv_ffn.shape[2])
        vals = dict(_unpack_rep(late[k], early[k], like))
        vals.update(conv_a=ca_k, sgu_norm=sg_k, conv_ffn=cf_k)
        for nm in ("w_in_even", "w_in_odd", "w_out_even", "w_out_odd", "w_ffn_gate", "w_ffn_up", "w_ffn_down"):
            vals[nm] = out[nm][k]
        groups.append([vals[nm].reshape(like[nm].shape) for nm in names])
    return (loss, grad_x[None], *groups[0], *groups[1], *groups[2], *groups[3])
```

```python
import functools

import jax
import jax.numpy as jnp
from jax import lax
from jax.experimental import pallas as pl
from jax.experimental.pallas import tpu as pltpu

F32, BF16 = jnp.float32, jnp.bfloat16
EPS = 1e-6
WINDOWS = (2, 4, 8, 16)
HALO = 16
CHUNK = 128
N_DEV = 8
N_CHIP = 4
MESH = pl.DeviceIdType.MESH
VMEM_LIMIT = 56 * 2**20
LANE = 128
ADAM_LR, ADAM_B1, ADAM_B2, ADAM_EPS, ADAM_WD, ADAM_STEP = 0.001, 0.9, 0.999, 1e-08, 0.01, 10
LATE_NUM, LATE_DEN = 7, 8
INV_SQRT2 = 0.7071067811865476
INV_SQRT2PI = 0.3989422804014327


def _pcall(body, comm=None, **kw):
    if comm is None:
        return pl.pallas_call(body, **kw)
    in_specs, out_specs, out_shape = list(kw.pop("in_specs")), kw.pop("out_specs"), kw.pop("out_shape")
    single = not isinstance(out_shape, (list, tuple))
    out_specs, out_shape = ([out_specs], [out_shape]) if single else (list(out_specs), list(out_shape))
    scratch = list(kw.pop("scratch_shapes", []))
    grid = kw.get("grid", ())
    n_in, n_out, n_scr, c_in, c_out = len(in_specs), len(out_specs), len(scratch), len(comm.ins), len(comm.out_shape)

    def hosted(*refs):
        cuts = [0, n_in, n_in + c_in, n_in + c_in + n_out, n_in + c_in + n_out + c_out, n_in + c_in + n_out + c_out + n_scr, len(refs)]
        ins, cins, outs, couts, scr, sems = (refs[a:b] for a, b in zip(cuts[:-1], cuts[1:]))
        if grid:
            step, steps = 0, 1
            for axis, size in enumerate(grid):
                step, steps = step * size + pl.program_id(axis), steps * size
            pl.when(step == 0)(lambda: comm.start(cins, couts, sems))
            pl.when(step == (steps * LATE_NUM) // LATE_DEN)(lambda: comm.middle(cins, couts, sems))
            body(*ins, *outs, *scr)
            pl.when(step == steps - 1)(lambda: comm.finish(cins, couts, sems))
        else:
            comm.start(cins, couts, sems)
            comm.middle(cins, couts, sems)
            body(*ins, *outs, *scr)
            comm.finish(cins, couts, sems)

    call = pl.pallas_call(hosted, in_specs=in_specs + [_hbm()] * c_in, out_specs=out_specs + [_hbm()] * c_out,
                          out_shape=out_shape + list(comm.out_shape), scratch_shapes=scratch + list(comm.sems), **kw)

    def run(*args):
        res = call(*args, *comm.ins)
        own = res[0] if single else res[:n_out]
        return own, res[n_out:]

    return run


def _params(*sem):
    return pltpu.CompilerParams(dimension_semantics=sem, vmem_limit_bytes=VMEM_LIMIT)


def _whole(shape):
    return pl.BlockSpec(shape, lambda *_: (0,) * len(shape))


def _resident(shape):
    return pl.BlockSpec(shape, lambda *_: (0,) * len(shape), pipeline_mode=pl.Buffered(1))


def _rows(i, tm):
    return pl.ds(pl.multiple_of(i * tm, tm), tm)


def _nn(a, b):
    return jnp.dot(a, b, preferred_element_type=F32)


def _nt(a, b):
    return lax.dot_general(a, b, (((1,), (1,)), ((), ())), preferred_element_type=F32)


def _tn(a, b):
    return lax.dot_general(a, b, (((0,), (0,)), ((), ())), preferred_element_type=F32)


def _rms(x, gain):
    r = lax.rsqrt(jnp.mean(x * x, axis=-1, keepdims=True) + EPS)
    return x * r * gain


def _rms_bwd(dy, x, gain):
    r = lax.rsqrt(jnp.mean(x * x, axis=-1, keepdims=True) + EPS)
    xh = x * r
    dgain = jnp.sum(dy * xh, axis=0, keepdims=True)
    dxh = dy * gain
    dx = r * (dxh - xh * jnp.mean(dxh * xh, axis=-1, keepdims=True))
    return dx, dgain


def _gelu(x):
    return 0.5 * x * (1.0 + lax.erf(x * INV_SQRT2))


def _gelu_grad(x):
    return 0.5 * (1.0 + lax.erf(x * INV_SQRT2)) + x * jnp.exp(-0.5 * x * x) * INV_SQRT2PI


def _acc(ref, val, first):
    @pl.when(first)
    def _():
        ref[...] = val

    @pl.when(jnp.logical_not(first))
    def _():
        ref[...] += val


def _shift(ext, k, back):
    n = ext.shape[0]
    return pltpu.roll(ext, k if back else n - k, axis=0)


def _window_sum(ext, w, back):
    total, step = ext, 1
    while step < w:
        total = total + _shift(total, step, back)
        step *= 2
    return total


def _counts(row0, tm, w):
    pos1 = (row0 + lax.broadcasted_iota(jnp.int32, (tm, 1), 0) + 1).astype(F32)
    return jnp.minimum(pos1, float(w))


def _even_fwd(x, gmix, gffn, win, conva, wpool, pscale, wout, tm, comm=None):
    s, d = x.shape
    e = win.shape[1]
    aw = e // 4

    def body(x_ref, gmix_ref, gffn_ref, win_ref, ca_ref, wp_ref, ps_ref, wout_ref,
             xn_ref, proj_ref, cq_ref, pooled_ref, mix_ref, h_ref, hn_ref, qbuf, zbuf):
        i = pl.program_id(0)

        @pl.when(i == 0)
        def _():
            qbuf[0:HALO, :] = jnp.zeros((HALO, aw), F32)
            zbuf[0:HALO, :] = jnp.zeros((HALO, aw), F32)

        xv = x_ref[...]
        xn = _rms(xv, gmix_ref[...]).astype(BF16)
        xn_ref[...] = xn
        proj = _nn(xn, win_ref[...])
        proj_ref[...] = proj.astype(BF16)
        a_b, a_c, a_v, z = (proj[:, k * aw:(k + 1) * aw] for k in range(4))
        q = a_c * a_v
        qbuf[HALO:HALO + tm, :] = q
        qext = qbuf[...]
        cur = slice(HALO, HALO + tm)
        cq = ca_ref[2:3, :] * q + ca_ref[1:2, :] * _shift(qext, 1, True)[cur, :] + ca_ref[0:1, :] * _shift(qext, 2, True)[cur, :]
        cq_ref[...] = cq.astype(BF16)
        y_a = a_b * cq
        zbuf[HALO:HALO + tm, :] = z
        zext = zbuf[...]
        ys = []
        for g, w in enumerate(WINDOWS):
            cols = slice(g * LANE, (g + 1) * LANE)
            acc = _window_sum(zext[:, cols], w, True)[cur, :]
            pooled = (acc / _counts(i * tm, tm, w) - z[:, cols]).astype(BF16)
            pooled_ref[:, cols] = pooled
            ys.append(_nn(pooled, wp_ref[g]))
        y_b = jnp.concatenate(ys, axis=1) * ps_ref[...]
        mix = jnp.concatenate([y_a, y_b], axis=1).astype(BF16)
        mix_ref[...] = mix
        h = xv + _nn(mix, wout_ref[...])
        h_ref[...] = h
        hn_ref[...] = _rms(h, gffn_ref[...]).astype(BF16)
        qbuf[0:HALO, :] = qbuf[tm:tm + HALO, :]
        zbuf[0:HALO, :] = zbuf[tm:tm + HALO, :]

    row = lambda c: pl.BlockSpec((tm, c), lambda i: (i, 0))
    return _pcall(
        body, comm=comm, name="even_fwd", grid=(s // tm,),
        in_specs=[row(d), _whole((1, d)), _whole((1, d)), _whole(win.shape), _whole(conva.shape), _whole(wpool.shape),
                  _whole(pscale.shape), _whole(wout.shape)],
        out_specs=[row(d), row(e), row(aw), row(aw), row(d), row(d), row(d)],
        out_shape=[jax.ShapeDtypeStruct((s, d), BF16), jax.ShapeDtypeStruct((s, e), BF16), jax.ShapeDtypeStruct((s, aw), BF16),
                   jax.ShapeDtypeStruct((s, aw), BF16), jax.ShapeDtypeStruct((s, d), BF16), jax.ShapeDtypeStruct((s, d), F32),
                   jax.ShapeDtypeStruct((s, d), BF16)],
        scratch_shapes=[pltpu.VMEM((tm + HALO, aw), F32), pltpu.VMEM((tm + HALO, aw), F32)],
        compiler_params=_params("arbitrary"),
    )(x, gmix, gffn, win, conva, wpool, pscale, wout)


def _ffn_gate(hn, wgt, cw, cb, tm, tn, name, comm=None):
    s, d = hn.shape
    f = wgt.shape[0]

    def body(hn_ref, wg_ref, cw_ref, cb_ref, g_ref, gc_ref, gbuf):
        i = pl.program_id(1)

        @pl.when(i == 0)
        def _():
            gbuf[0:HALO, :] = jnp.zeros((HALO, tn), F32)

        g = _nt(hn_ref[_rows(i, tm), :], wg_ref[...])
        g_ref[...] = g.astype(BF16)
        gbuf[HALO:HALO + tm, :] = g
        ext = gbuf[...]
        gc = (cw_ref[2:3, :] * g + cw_ref[1:2, :] * pltpu.roll(ext, 1, axis=0)[HALO:HALO + tm, :]
              + cw_ref[0:1, :] * pltpu.roll(ext, 2, axis=0)[HALO:HALO + tm, :] + cb_ref[...])
        gc_ref[...] = gc.astype(BF16)
        gbuf[0:HALO, :] = gbuf[tm:tm + HALO, :]

    tile = pl.BlockSpec((tm, tn), lambda j, i: (i, j))
    wcol = lambda r: pl.BlockSpec((r, tn), lambda j, i: (0, j))
    out = jax.ShapeDtypeStruct((s, f), BF16)
    return _pcall(
        body, comm=comm, name=name, grid=(f // tn, s // tm),
        in_specs=[_resident((s, d)), pl.BlockSpec((tn, d), lambda j, i: (j, 0)), wcol(3), wcol(1)],
        out_specs=[tile, tile], out_shape=[out, out],
        scratch_shapes=[pltpu.VMEM((tm + HALO, tn), F32)],
        compiler_params=_params("arbitrary", "arbitrary"),
    )(hn, wgt, cw, cb)


def _ffn_up(hn, wut, gc, tm, tn, name, comm=None):
    s, d = hn.shape
    f = wut.shape[0]

    def body(hn_ref, wu_ref, gc_ref, up_ref, a_ref):
        up = _nt(hn_ref[_rows(pl.program_id(1), tm), :], wu_ref[...])
        up_ref[...] = up.astype(BF16)
        gc = gc_ref[...].astype(F32)
        a_ref[...] = (gc * jax.nn.sigmoid(gc) * up).astype(BF16)

    tile = pl.BlockSpec((tm, tn), lambda j, i: (i, j))
    out = jax.ShapeDtypeStruct((s, f), BF16)
    return _pcall(
        body, comm=comm, name=name, grid=(f // tn, s // tm),
        in_specs=[_resident((s, d)), pl.BlockSpec((tn, d), lambda j, i: (j, 0)), tile],
        out_specs=[tile, tile], out_shape=[out, out],
        compiler_params=_params("arbitrary", "arbitrary"),
    )(hn, wut, gc)


def _ffn_fwd2(a, wd, h, gain, tm, name, comm=None):
    s, d = h.shape
    f = a.shape[1]

    def body(a_ref, wd_ref, h_ref, gain_ref, ho_ref, hn_ref):
        ho = h_ref[...] + _nn(a_ref[...], wd_ref[...])
        ho_ref[...] = ho
        hn_ref[...] = _rms(ho, gain_ref[...]).astype(BF16)

    row = lambda c: pl.BlockSpec((tm, c), lambda i: (i, 0))
    return _pcall(
        body, comm=comm, name=name, grid=(s // tm,),
        in_specs=[row(f), _whole(wd.shape), row(d), _whole((1, d))],
        out_specs=[row(d), row(d)],
        out_shape=[jax.ShapeDtypeStruct((s, d), F32), jax.ShapeDtypeStruct((s, d), BF16)],
        compiler_params=_params("arbitrary"),
    )(a, wd, h, gain)


def _odd_fwd(xn, h, win, sgu, ws, bfull, wout, gffn, tm, comm=None):
    s, d = h.shape
    e = win.shape[1]
    cw = e // 2
    heads = ws.shape[0]

    def body(xn_ref, h_ref, win_ref, sgu_ref, ws_ref, b_ref, wout_ref, gffn_ref,
             pre_ref, gate_ref, mixo_ref, ho_ref, hn_ref, gbuf):
        pre = _nn(xn_ref[...], win_ref[...])
        pre_ref[...] = pre.astype(BF16)
        p = _gelu(pre)
        u, v = p[:, :cw], p[:, cw:]
        vn = _rms(v, sgu_ref[...]).astype(BF16)
        for n in range(tm // CHUNK):
            rows = slice(n * CHUNK, (n + 1) * CHUNK)
            for hd in range(heads):
                cols = slice(hd * CHUNK, (hd + 1) * CHUNK)
                gbuf[rows, cols] = _nn(ws_ref[hd], vn[rows, cols]) + b_ref[:, cols]
        gate = gbuf[...]
        gate_ref[...] = gate.astype(BF16)
        mixo = (u * gate).astype(BF16)
        mixo_ref[...] = mixo
        ho = h_ref[...] + _nn(mixo, wout_ref[...])
        ho_ref[...] = ho
        hn_ref[...] = _rms(ho, gffn_ref[...]).astype(BF16)

    row = lambda c: pl.BlockSpec((tm, c), lambda i: (i, 0))
    return _pcall(
        body, comm=comm, name="odd_fwd", grid=(s // tm,),
        in_specs=[row(d), row(d), _whole(win.shape), _whole(sgu.shape), _whole(ws.shape), _whole(bfull.shape),
                  _whole(wout.shape), _whole((1, d))],
        out_specs=[row(e), row(cw), row(cw), row(d), row(d)],
        out_shape=[jax.ShapeDtypeStruct((s, e), BF16), jax.ShapeDtypeStruct((s, cw), BF16), jax.ShapeDtypeStruct((s, cw), BF16),
                   jax.ShapeDtypeStruct((s, d), F32), jax.ShapeDtypeStruct((s, d), BF16)],
        scratch_shapes=[pltpu.VMEM((tm, cw), F32)],
        compiler_params=_params("arbitrary"),
    )(xn, h, win, sgu, ws, bfull, wout, gffn)


def _loss_bwd(h, gain, target, tm):
    s, d = h.shape

    def body(h_ref, gain_ref, t_ref, dh_ref, dhb_ref, dgain_ref, loss_ref):
        i = pl.program_id(0)
        hv = h_ref[...]
        gain = gain_ref[...]
        err = _rms(hv, gain) - t_ref[...]
        dy = err * (1.0 / d)
        dx, dgain = _rms_bwd(dy, hv, gain)
        dh_ref[...] = dx
        dhb_ref[...] = dx.astype(BF16)
        _acc(dgain_ref, dgain, i == 0)
        _acc(loss_ref, jnp.sum(err * err, axis=0, keepdims=True) * (0.5 / d), i == 0)

    row = pl.BlockSpec((tm, d), lambda i: (i, 0))
    return _pcall(
        body, name="loss_bwd", grid=(s // tm,),
        in_specs=[row, _whole((1, d)), row],
        out_specs=[row, row, _whole((1, d)), _whole((1, d))],
        out_shape=[jax.ShapeDtypeStruct((s, d), F32), jax.ShapeDtypeStruct((s, d), BF16), jax.ShapeDtypeStruct((1, d), F32),
                   jax.ShapeDtypeStruct((1, d), F32)],
        compiler_params=_params("arbitrary"),
    )(h, gain, target)


def _ffn_bwd1(dhb, g, gc, up, wd, cw, tm, tn, name, comm=None):
    s, d = dhb.shape
    f = g.shape[1]
    ni = s // tm

    def body(dh_ref, g_ref, gc_ref, up_ref, wd_ref, cw_ref, dg_ref, dup_ref, dcw_ref, dcb_ref, ebuf):
        i = pl.program_id(1)

        @pl.when(i == 0)
        def _():
            ebuf[tm:tm + HALO, :] = jnp.zeros((HALO, tn), F32)

        da = _nt(dh_ref[_rows(ni - 1 - i, tm), :], wd_ref[...])
        gcv = gc_ref[...].astype(F32)
        sg = jax.nn.sigmoid(gcv)
        silu = gcv * sg
        dup_ref[...] = (da * silu).astype(BF16)
        dgc = da * up_ref[...].astype(F32) * (sg * (1.0 - silu) + silu)
        ebuf[0:tm, :] = dgc
        ext = ebuf[...]
        s1 = pltpu.roll(ext, tm + HALO - 1, axis=0)[0:tm, :]
        s2 = pltpu.roll(ext, tm + HALO - 2, axis=0)[0:tm, :]
        dg_ref[...] = (cw_ref[2:3, :] * dgc + cw_ref[1:2, :] * s1 + cw_ref[0:1, :] * s2).astype(BF16)
        gv = g_ref[...].astype(F32)
        for k, shifted in enumerate((s2, s1, dgc)):
            _acc(dcw_ref.at[k:k + 1, :], jnp.sum(shifted * gv, axis=0, keepdims=True), i == 0)
        _acc(dcb_ref, jnp.sum(dgc, axis=0, keepdims=True), i == 0)
        ebuf[tm:tm + HALO, :] = ebuf[0:HALO, :]

    tile = pl.BlockSpec((tm, tn), lambda j, i: (ni - 1 - i, j))
    wcol = lambda r: pl.BlockSpec((r, tn), lambda j, i: (0, j))
    out = jax.ShapeDtypeStruct((s, f), BF16)
    return _pcall(
        body, comm=comm, name=name, grid=(f // tn, ni),
        in_specs=[_resident((s, d)), tile, tile, tile,
                  pl.BlockSpec((tn, d), lambda j, i: (j, 0)), wcol(3)],
        out_specs=[tile, tile, wcol(3), wcol(1)],
        out_shape=[out, out, jax.ShapeDtypeStruct((3, f), F32), jax.ShapeDtypeStruct((1, f), F32)],
        scratch_shapes=[pltpu.VMEM((tm + HALO, tn), F32)],
        compiler_params=_params("arbitrary", "arbitrary"),
    )(dhb, g, gc, up, wd, cw)


def _ffn_bwd2(dg, dup, wg, wu, h, gain, dh, tm, name, comm=None):
    s, d = h.shape
    f = dg.shape[1]

    def body(dg_ref, dup_ref, wg_ref, wu_ref, h_ref, gain_ref, dh_ref, dho_ref, dhb_ref, dgain_ref):
        dhn = _nn(dg_ref[...], wg_ref[...]) + _nn(dup_ref[...], wu_ref[...])
        dx, dgain = _rms_bwd(dhn, h_ref[...], gain_ref[...])
        dho = dh_ref[...] + dx
        dho_ref[...] = dho
        dhb_ref[...] = dho.astype(BF16)
        _acc(dgain_ref, dgain, pl.program_id(0) == 0)

    row = lambda c: pl.BlockSpec((tm, c), lambda i: (i, 0))
    return _pcall(
        body, comm=comm, name=name, grid=(s // tm,),
        in_specs=[row(f), row(f), _whole(wg.shape), _whole(wu.shape), row(d), _whole((1, d)), row(d)],
        out_specs=[row(d), row(d), _whole((1, d))],
        out_shape=[jax.ShapeDtypeStruct((s, d), F32), jax.ShapeDtypeStruct((s, d), BF16), jax.ShapeDtypeStruct((1, d), F32)],
        compiler_params=_params("arbitrary"),
    )(dg, dup, wg, wu, h, gain, dh)


def _odd_bwd(dhb, dh, wout, pre, gate, ws, wst, sgu, win, h, gmix, tm, comm=None):
    s, d = h.shape
    e = win.shape[1]
    cw = e // 2
    heads = ws.shape[0]
    ni = s // tm

    def body(dhb_ref, dh_ref, wout_ref, pre_ref, gate_ref, ws_ref, wst_ref, sgu_ref, win_ref, h_ref, gmix_ref,
             dpre_ref, dho_ref, dhob_ref, dgain_ref, dsgu_ref, dws_ref, db_ref, vbuf, gacc):
        i = pl.program_id(0)
        first = i == 0
        dmixo = _nt(dhb_ref[...], wout_ref[...])
        pre = pre_ref[...].astype(F32)
        p = _gelu(pre)
        u, v = p[:, :cw], p[:, cw:]
        sgu = sgu_ref[...]
        rv = lax.rsqrt(jnp.mean(v * v, axis=-1, keepdims=True) + EPS)
        vh = v * rv
        vn = (vh * sgu).astype(BF16)
        du = dmixo * gate_ref[...].astype(F32)
        dgate = dmixo * u
        dgate_b = dgate.astype(BF16)
        gsum = dgate[0:CHUNK, :]
        for n in range(1, tm // CHUNK):
            gsum = gsum + dgate[n * CHUNK:(n + 1) * CHUNK, :]
        _acc(gacc, gsum, first)
        for hd in range(heads):
            cols = slice(hd * CHUNK, (hd + 1) * CHUNK)
            dws = None
            for n in range(tm // CHUNK):
                rows = slice(n * CHUNK, (n + 1) * CHUNK)
                vbuf[rows, cols] = _nn(wst_ref[hd], dgate_b[rows, cols])
                part = _nt(dgate_b[rows, cols], vn[rows, cols])
                dws = part if dws is None else dws + part
            _acc(dws_ref.at[hd], dws, first)
        dvn = vbuf[...]
        _acc(dsgu_ref, jnp.sum(dvn * vh, axis=0, keepdims=True), first)
        dvh = dvn * sgu
        dv = rv * (dvh - vh * jnp.mean(dvh * vh, axis=-1, keepdims=True))
        dpre = (jnp.concatenate([du, dv], axis=1) * _gelu_grad(pre)).astype(BF16)
        dpre_ref[...] = dpre
        dx, dgain = _rms_bwd(_nt(dpre, win_ref[...]), h_ref[...], gmix_ref[...])
        dho = dh_ref[...] + dx
        dho_ref[...] = dho
        dhob_ref[...] = dho.astype(BF16)
        _acc(dgain_ref, dgain, first)

        @pl.when(i == ni - 1)
        def _():
            ones = jnp.ones((8, CHUNK), F32)
            for hd in range(heads):
                tot = lax.dot_general(ones, gacc[:, hd * CHUNK:(hd + 1) * CHUNK], (((1,), (1,)), ((), ())),
                                      preferred_element_type=F32, precision=lax.Precision.HIGHEST)
                db_ref[hd:hd + 1, :] = tot[0:1, :]

    row = lambda c: pl.BlockSpec((tm, c), lambda i: (i, 0))
    return _pcall(
        body, comm=comm, name="odd_bwd", grid=(ni,),
        in_specs=[row(d), row(d), _whole(wout.shape), row(e), row(cw), _whole(ws.shape), _whole(wst.shape), _whole(sgu.shape),
                  _whole(win.shape), row(d), _whole((1, d))],
        out_specs=[row(e), row(d), row(d), _whole((1, d)), _whole((1, cw)), _whole(ws.shape), _whole((heads, CHUNK))],
        out_shape=[jax.ShapeDtypeStruct((s, e), BF16), jax.ShapeDtypeStruct((s, d), F32), jax.ShapeDtypeStruct((s, d), BF16),
                   jax.ShapeDtypeStruct((1, d), F32), jax.ShapeDtypeStruct((1, cw), F32), jax.ShapeDtypeStruct(ws.shape, F32),
                   jax.ShapeDtypeStruct((heads, CHUNK), F32)],
        scratch_shapes=[pltpu.VMEM((tm, cw), F32), pltpu.VMEM((CHUNK, cw), F32)],
        compiler_params=_params("arbitrary"),
    )(dhb, dh, wout, pre, gate, ws, wst, sgu, win, h, gmix)


def _even_bwd(dhb, dh, wout, proj, cq, pooled, conva, wpool, wpoolt, pscale, win, x, gmix, tm, comm=None):
    s, d = x.shape
    e = win.shape[1]
    aw = e // 4
    ni = s // tm

    def body(dhb_ref, dh_ref, wout_ref, proj_ref, cq_ref, pooled_ref, ca_ref, wp_ref, wpt_ref, ps_ref, win_ref, x_ref, gmix_ref,
             dproj_ref, dx_ref, dgain_ref, dca_ref, dwp_ref, dps_ref, cbuf, ebuf):
        i = pl.program_id(0)
        first = i == 0

        @pl.when(first)
        def _():
            cbuf[tm:tm + HALO, :] = jnp.zeros((HALO, aw), F32)
            ebuf[tm:tm + HALO, :] = jnp.zeros((HALO, aw), F32)

        dmix = _nt(dhb_ref[...], wout_ref[...])
        dy_a, dy_b = dmix[:, :aw], dmix[:, aw:]
        proj = proj_ref[...].astype(F32)
        a_b, a_c, a_v = (proj[:, k * aw:(k + 1) * aw] for k in range(3))
        da_b = dy_a * cq_ref[...].astype(F32)
        dcq = dy_a * a_b
        cbuf[0:tm, :] = dcq
        cext = cbuf[...]
        s1 = _shift(cext, 1, False)[0:tm, :]
        s2 = _shift(cext, 2, False)[0:tm, :]
        q = a_c * a_v
        for k, shifted in enumerate((s2, s1, dcq)):
            _acc(dca_ref.at[k:k + 1, :], jnp.sum(shifted * q, axis=0, keepdims=True), first)
        dq = ca_ref[2:3, :] * dcq + ca_ref[1:2, :] * s1 + ca_ref[0:1, :] * s2
        da_c = dq * a_v
        da_v = dq * a_c
        dps, dpool = [], []
        for g, w in enumerate(WINDOWS):
            cols = slice(g * LANE, (g + 1) * LANE)
            pooled = pooled_ref[:, cols]
            mixed = _nn(pooled, wp_ref[g])
            dps.append(jnp.sum(dy_b[:, cols] * mixed, axis=0, keepdims=True))
            dmixed = (dy_b[:, cols] * ps_ref[:, cols]).astype(BF16)
            _acc(dwp_ref.at[g], _tn(pooled, dmixed), first)
            dp = _nn(dmixed, wpt_ref[g])
            dpool.append(dp)
            ebuf[0:tm, cols] = dp / _counts((ni - 1 - i) * tm, tm, w)
        _acc(dps_ref, jnp.concatenate(dps, axis=1), first)
        eext = ebuf[...]
        dzs = []
        for g, w in enumerate(WINDOWS):
            cols = slice(g * LANE, (g + 1) * LANE)
            dzs.append(_window_sum(eext[:, cols], w, False)[0:tm, :] - dpool[g])
        dproj = jnp.concatenate([da_b, da_c, da_v] + dzs, axis=1).astype(BF16)
        dproj_ref[...] = dproj
        dx, dgain = _rms_bwd(_nt(dproj, win_ref[...]), x_ref[...], gmix_ref[...])
        dx_ref[...] = dh_ref[...] + dx
        _acc(dgain_ref, dgain, first)
        cbuf[tm:tm + HALO, :] = cbuf[0:HALO, :]
        ebuf[tm:tm + HALO, :] = ebuf[0:HALO, :]

    row = lambda c: pl.BlockSpec((tm, c), lambda i: (ni - 1 - i, 0))
    return _pcall(
        body, comm=comm, name="even_bwd", grid=(ni,),
        in_specs=[row(d), row(d), _whole(wout.shape), row(e), row(aw), row(aw), _whole(conva.shape), _whole(wpool.shape),
                  _whole(wpoolt.shape), _whole(pscale.shape), _whole(win.shape), row(d), _whole((1, d))],
        out_specs=[row(e), row(d), _whole((1, d)), _whole(conva.shape), _whole(wpool.shape), _whole(pscale.shape)],
        out_shape=[jax.ShapeDtypeStruct((s, e), BF16), jax.ShapeDtypeStruct((s, d), F32), jax.ShapeDtypeStruct((1, d), F32),
                   jax.ShapeDtypeStruct(conva.shape, F32), jax.ShapeDtypeStruct(wpool.shape, F32),
                   jax.ShapeDtypeStruct(pscale.shape, F32)],
        scratch_shapes=[pltpu.VMEM((tm + HALO, aw), F32), pltpu.VMEM((tm + HALO, aw), F32)],
        compiler_params=_params("arbitrary"),
    )(dhb, dh, wout, proj, cq, pooled, conva, wpool, wpoolt, pscale, win, x, gmix)


def _wgrad(a, b, tk, ts, name, comm=None):
    s, ka = a.shape
    nb = b.shape[1]
    nt = s // ts

    def body(a_ref, b_ref, o_ref, acc):
        t = pl.program_id(1)
        _acc(acc, _tn(a_ref[...], b_ref[...]), t == 0)

        @pl.when(t == nt - 1)
        def _():
            o_ref[...] = acc[...].astype(BF16)

    return _pcall(
        body, comm=comm, name=name, grid=(ka // tk, nt),
        in_specs=[pl.BlockSpec((ts, tk), lambda k, t: (t, k)), pl.BlockSpec((ts, nb), lambda k, t: (t, 0))],
        out_specs=pl.BlockSpec((tk, nb), lambda k, t: (k, 0)),
        out_shape=jax.ShapeDtypeStruct((ka, nb), BF16),
        scratch_shapes=[pltpu.VMEM((tk, nb), F32)],
        compiler_params=_params("arbitrary", "arbitrary"),
    )(a, b)


def _adamw(parts, w, m, v, tr, name, layer=None, into=None):
    r, c = w.shape[-2:]
    n, rp, cp = parts.shape
    assert rp >= r and r % tr == 0

    def body(p_ref, w_ref, m_ref, v_ref, *rest):
        g_ref, d_ref, mo_ref, vo_ref = rest[-4:]
        g = p_ref[0, 0:tr, 0:c].astype(F32)
        for j in range(1, n):
            g = g + p_ref[j, 0:tr, 0:c].astype(F32)
        g_ref[...] = g
        mn = ADAM_B1 * m_ref[...] + (1.0 - ADAM_B1) * g
        vn = ADAM_B2 * v_ref[...] + (1.0 - ADAM_B2) * (g * g)
        mo_ref[...] = mn
        vo_ref[...] = vn
        m_hat = mn / (1.0 - ADAM_B1 ** ADAM_STEP)
        v_hat = vn / (1.0 - ADAM_B2 ** ADAM_STEP)
        d_ref[...] = -ADAM_LR * (m_hat / (jnp.sqrt(v_hat) + ADAM_EPS) + ADAM_WD * w_ref[...])

    if layer is None:
        row = pl.BlockSpec((tr, c), lambda i: (i, 0))
    else:
        row = pl.BlockSpec((None, tr, c), lambda i: (layer, i, 0))
    out = jax.ShapeDtypeStruct(w.shape, F32)
    prev = [] if into is None else list(into)
    return _pcall(
        body, name=name, grid=(r // tr,),
        in_specs=[pl.BlockSpec((n, tr, cp), lambda i: (0, i, 0)), row, row, row] + [pl.BlockSpec(memory_space=pl.ANY)] * len(prev),
        out_specs=[row, row, row, row], out_shape=[out, out, out, out],
        input_output_aliases={4 + k: k for k in range(len(prev))},
        compiler_params=_params("arbitrary"),
    )(parts, w, m, v, *prev)


def _pair_add(grad, other, spec, core, name):
    axis, width = spec
    slot = other.shape[1:]

    def body(core_ref, g_ref, o_ref, out_ref):
        out_ref[...] = (g_ref[...].astype(F32) + o_ref[...].astype(F32)).astype(BF16)

    if axis == 0:
        gspec = pl.BlockSpec(slot, lambda q, core_ref: (2 * q + core_ref[0], 0))
    else:
        gspec = pl.BlockSpec(slot, lambda q, core_ref: (0, 2 * q + core_ref[0]))
    per_chip = pl.BlockSpec((None,) + slot, lambda q, core_ref: (q, 0, 0))
    return _pcall(
        body, name=name,
        grid_spec=pltpu.PrefetchScalarGridSpec(num_scalar_prefetch=1, grid=(N_CHIP,), in_specs=[gspec, per_chip], out_specs=per_chip),
        out_shape=jax.ShapeDtypeStruct(other.shape, BF16),
        compiler_params=_params("arbitrary"),
    )(core, grad, other)


def _hbm():
    return pl.BlockSpec(memory_space=pltpu.HBM)


def _window(ref, spec, j):
    axis, width = spec
    start = pl.multiple_of(j * width, width)
    return ref.at[(slice(None),) * axis + (pl.ds(start, width),)]


def _here():
    return lax.axis_index("x"), lax.axis_index("y"), lax.axis_index("c")


class _Gather:
    def __init__(self, shards, specs, fulls):
        n = len(shards)
        self.ins, self.specs, self.out_shape = list(shards), list(specs), list(fulls)
        self.sems = [pltpu.SemaphoreType.DMA((7 * n,)), pltpu.SemaphoreType.DMA((7 * n,)), pltpu.SemaphoreType.DMA((n,))]

    def _plan(self, ins, outs, sems):
        send_sems, recv_sems, local_sems = sems
        x, y, c = _here()
        me, sibling = (x, y, c), (x, y, 1 - c)
        chips = [(1 - x, y), (x, 1 - y), (1 - x, 1 - y)]

        def slot(t, dev):
            return _window(outs[t], self.specs[t], 4 * dev[0] + 2 * dev[1] + dev[2])

        def copy(t, k, block, to, src=None):
            return pltpu.make_async_remote_copy(
                src_ref=slot(t, block) if src is None else src, dst_ref=slot(t, block),
                send_sem=send_sems.at[7 * t + k], recv_sem=recv_sems.at[7 * t + k], device_id=to, device_id_type=MESH)

        plan = []
        for t in range(len(ins)):
            plan.append(dict(
                mine=pltpu.make_async_copy(ins[t], slot(t, me), local_sems.at[t]),
                first=[copy(t, 0, me, sibling, src=ins[t])] + [copy(t, 1 + j, me, (*q, c), src=ins[t]) for j, q in enumerate(chips)],
                over_ici=[copy(t, 1 + j, (*q, c), me) for j, q in enumerate(chips)],
                passed=[copy(t, 4 + j, (*q, c), sibling) for j, q in enumerate(chips)],
                from_sibling=[copy(t, 0, sibling, me)] + [copy(t, 4 + j, (*q, 1 - c), me) for j, q in enumerate(chips)]))
        return plan

    def start(self, ins, outs, sems):
        for p in self._plan(ins, outs, sems):
            p["mine"].start()
            for cp in p["first"]:
                cp.start()

    def middle(self, ins, outs, sems):
        for p in self._plan(ins, outs, sems):
            for arrived, onward in zip(p["over_ici"], p["passed"]):
                arrived.wait_recv()
                onward.start()

    def finish(self, ins, outs, sems):
        plan = self._plan(ins, outs, sems)
        for p in plan:
            for cp in p["from_sibling"]:
                cp.wait_recv()
        for p in plan:
            for cp in p["first"] + p["passed"]:
                cp.wait_send()
            p["mine"].wait()


class _PairExchange:
    def __init__(self, grads, specs):
        n = len(grads)
        self.ins, self.specs = list(grads), list(specs)
        self.out_shape = [jax.ShapeDtypeStruct((N_CHIP,) + a.shape[:sp[0]] + (sp[1],) + a.shape[sp[0] + 1:], a.dtype)
                          for a, sp in zip(grads, specs)]
        self.sems = [pltpu.SemaphoreType.DMA((n,)), pltpu.SemaphoreType.DMA((n,))]

    def start(self, ins, outs, sems):
        send_sems, recv_sems = sems
        x, y, c = _here()
        for t in range(len(ins)):
            for q in range(N_CHIP):
                pltpu.make_async_remote_copy(
                    src_ref=_window(ins[t], self.specs[t], 2 * q + (1 - c)), dst_ref=outs[t].at[q],
                    send_sem=send_sems.at[t], recv_sem=recv_sems.at[t], device_id=(x, y, 1 - c), device_id_type=MESH).start()

    def middle(self, ins, outs, sems):
        pass

    def finish(self, ins, outs, sems):
        send_sems, recv_sems = sems
        x, y, c = _here()
        for t in range(len(ins)):
            every = pltpu.make_async_remote_copy(src_ref=outs[t], dst_ref=outs[t], send_sem=send_sems.at[t],
                                                 recv_sem=recv_sems.at[t], device_id=(x, y, 1 - c), device_id_type=MESH)
            every.wait_send()
            every.wait_recv()


class _ChipExchange:
    def __init__(self, pairs, slotted=(), whole=()):
        self.ins = list(pairs) + list(slotted) + list(whole)
        self.npair, self.nslot = len(pairs), len(pairs) + len(slotted)
        n = len(self.ins)
        self.out_shape = ([jax.ShapeDtypeStruct(a.shape, a.dtype) for a in list(pairs) + list(slotted)]
                          + [jax.ShapeDtypeStruct((N_DEV,) + a.shape, a.dtype) for a in whole])
        self.sems = [pltpu.SemaphoreType.DMA((7 * n,)), pltpu.SemaphoreType.DMA((7 * n,)), pltpu.SemaphoreType.DMA((n,))]

    def _plan(self, ins, outs, sems):
        send_sems, recv_sems, local_sems = sems
        npair, nslot = self.npair, self.nslot
        x, y, c = _here()
        me, chip = 4 * x + 2 * y + c, 2 * x + y
        chips = [(1 - x, y), (x, 1 - y), (1 - x, 1 - y)]
        peers = [(x, y, 1 - c)] + [(*q, c) for q in chips] + [(*q, 1 - c) for q in chips]

        def index(dev):
            return 4 * dev[0] + 2 * dev[1] + dev[2]

        def copy(t, k, arriving):
            peer = peers[k]
            if t < npair:
                src, mine, theirs = ins[t].at[2 * peer[0] + peer[1]], chip, 2 * peer[0] + peer[1]
            else:
                src, mine, theirs = (ins[t].at[index(peer)] if t < nslot else ins[t]), me, index(peer)
            return pltpu.make_async_remote_copy(
                src_ref=src, dst_ref=outs[t].at[theirs if arriving else mine],
                send_sem=send_sems.at[7 * t + k], recv_sem=recv_sems.at[7 * t + k], device_id=peer, device_id_type=MESH)

        own, sent, arriving = [], [], []
        for t in range(len(ins)):
            fan = range(1, 4) if t < npair else range(7)
            if t < npair:
                own.append(pltpu.make_async_copy(ins[t].at[chip], outs[t].at[chip], local_sems.at[t]))
            else:
                own.append(pltpu.make_async_copy(ins[t].at[me] if t < nslot else ins[t], outs[t].at[me], local_sems.at[t]))
            sent += [copy(t, k, False) for k in fan]
            arriving += [copy(t, k, True) for k in fan]
        return own, sent, arriving

    def start(self, ins, outs, sems):
        own, sent, _ = self._plan(ins, outs, sems)
        for cp in own + sent:
            cp.start()

    def middle(self, ins, outs, sems):
        pass

    def finish(self, ins, outs, sems):
        own, sent, arriving = self._plan(ins, outs, sems)
        for cp in arriving:
            cp.wait_recv()
        for cp in sent:
            cp.wait_send()
        for cp in own:
            cp.wait()


class _Jobs:
    def __init__(self, *jobs):
        self.jobs = jobs
        self.ins = [a for j in jobs for a in j.ins]
        self.out_shape = [a for j in jobs for a in j.out_shape]
        self.sems = [a for j in jobs for a in j.sems]

    def _split(self, ins, outs, sems):
        i = o = s = 0
        for j in self.jobs:
            yield j, ins[i:i + len(j.ins)], outs[o:o + len(j.out_shape)], sems[s:s + len(j.sems)]
            i, o, s = i + len(j.ins), o + len(j.out_shape), s + len(j.sems)

    def start(self, ins, outs, sems):
        for j, a, b, c in self._split(ins, outs, sems):
            j.start(a, b, c)

    def middle(self, ins, outs, sems):
        for j, a, b, c in self._split(ins, outs, sems):
            j.middle(a, b, c)

    def finish(self, ins, outs, sems):
        for j, a, b, c in self._split(ins, outs, sems):
            j.finish(a, b, c)

    def results(self, outs):
        return [b for _, _, b, _ in self._split((), outs, ())]


def _alone(job, name):
    return _pcall(lambda: None, comm=job, name=name, in_specs=[], out_specs=[], out_shape=[])()[1]


SMALL_ROWS = 24
REP_COLS = 1024


def _pad_to(a, rows, cols):
    return jnp.pad(a, ((0, rows - a.shape[0]), (0, cols - a.shape[1])))


def _pack_small(conv_a, sgu_norm, conv_ffn, cols):
    return jnp.concatenate([_pad_to(conv_a, 8, cols), _pad_to(sgu_norm, 8, cols),
                            _pad_to(conv_ffn.reshape(-1, conv_ffn.shape[-1]), 8, cols)], axis=0)


def _unpack_small(p, ca_w, sg_w, cf_w):
    return p[0:3, 0:ca_w], p[8:9, 0:sg_w], p[16:22, 0:cf_w].reshape(2, 3, cf_w)


def _tile_rows(rows):
    return -(-rows // 8) * 8


def _pack_rows(parts):
    return jnp.concatenate([_pad_to(a, _tile_rows(a.shape[0]), REP_COLS) for a in parts], axis=0)


def _unpack_rows(p, shapes):
    out, r0 = [], 0
    for r, c in shapes:
        out.append(p[r0:r0 + r, 0:c])
        r0 += _tile_rows(r)
    return out


def _rep_late(norm_mix0, norm_ffn0, pool_scale, b_conv0, w_pool):
    return _pack_rows([norm_mix0, norm_ffn0, pool_scale.reshape(1, -1), _pad_to(b_conv0, 1, 3 * REP_COLS).reshape(3, REP_COLS),
                       w_pool.reshape(-1, REP_COLS)])


def _rep_early(norm_mix1, norm_ffn1, final_norm, b_spatial, b_conv1, w_spatial, loss):
    return _pack_rows([norm_mix1, norm_ffn1, final_norm.reshape(1, -1), b_spatial.reshape(1, -1),
                       _pad_to(b_conv1, 1, 3 * REP_COLS).reshape(3, REP_COLS), w_spatial.reshape(-1, REP_COLS), loss])


def _unpack_rep(late, early, like):
    f = like["b_conv_ffn"].shape[1]
    nm0, nf0, ps, bc0, wp = _unpack_rows(late, [(1, REP_COLS), (1, REP_COLS), (1, like["pool_scale"].shape[1]), (3, REP_COLS),
                                                (like["w_pool"].size // REP_COLS, REP_COLS)])
    nm1, nf1, fin, bs, bc1, wsp, _ = _unpack_rows(early, [(1, REP_COLS)] * 4 + [(3, REP_COLS), (like["w_spatial"].size // REP_COLS, REP_COLS),
                                                          (1, REP_COLS)])
    return {
        "norm_mix": jnp.concatenate([nm0, nm1]), "norm_ffn": jnp.concatenate([nf0, nf1]), "final_norm": fin[0], "pool_scale": ps,
        "b_spatial": bs.reshape(like["b_spatial"].shape),
        "b_conv_ffn": jnp.concatenate([bc0.reshape(1, -1), bc1.reshape(1, -1)])[:, 0:f],
        "w_pool": wp.reshape(like["w_pool"].shape), "w_spatial": wsp.reshape(like["w_spatial"].shape),
    }


def _pad_slots(a, width, padded):
    a = a.reshape(*a.shape[:-1], N_DEV, width)
    a = jnp.pad(a, ((0, 0),) * (a.ndim - 1) + ((0, padded - width),))
    return a.reshape(*a.shape[:-2], N_DEV * padded)


def _unpad_slots(a, width, padded):
    a = a.reshape(*a.shape[:-1], N_DEV, padded)[..., 0:width]
    return a.reshape(*a.shape[:-2], N_DEV * width)


def kernel(x, norm_mix, norm_ffn, final_norm, w_in_even, conv_a, w_pool, pool_scale, w_out_even, w_in_odd, sgu_norm, w_spatial, b_spatial, w_out_odd, w_ffn_gate, w_ffn_up, conv_ffn, b_conv_ffn, w_ffn_down, loss_target, m_norm_mix, m_norm_ffn, m_final_norm, m_w_in_even, m_conv_a, m_w_pool, m_pool_scale, m_w_out_even, m_w_in_odd, m_sgu_norm, m_w_spatial, m_b_spatial, m_w_out_odd, m_w_ffn_gate, m_w_ffn_up, m_conv_ffn, m_b_conv_ffn, m_w_ffn_down, v_norm_mix, v_norm_ffn, v_final_norm, v_w_in_even, v_conv_a, v_w_pool, v_pool_scale, v_w_out_even, v_w_in_odd, v_sgu_norm, v_w_spatial, v_b_spatial, v_w_out_odd, v_w_ffn_gate, v_w_ffn_up, v_conv_ffn, v_b_conv_ffn, v_w_ffn_down):
    s, d = x.shape[1], x.shape[2]
    x2, target = x[0], loss_target[0]
    tm = min(512, s)
    tm_wide = min(2048, s)
    tn = 256
    tn_fwd = 512
    row = lambda a: a.reshape(1, -1)
    ein, ro = w_in_even.shape[2], w_out_even.shape[1]
    fs = w_ffn_gate.shape[2]
    fsp = -(-fs // LANE) * LANE
    fp = N_DEV * fsp
    full = lambda shape, dtype=BF16: jax.ShapeDtypeStruct(shape, dtype)

    wg_s = [jnp.pad(w_ffn_gate[l].T, ((0, fsp - fs), (0, 0))).astype(BF16) for l in range(2)]
    wu_s = [jnp.pad(w_ffn_up[l].T, ((0, fsp - fs), (0, 0))).astype(BF16) for l in range(2)]
    wd_s = [jnp.pad(w_ffn_down[l], ((0, fsp - fs), (0, 0))).astype(BF16) for l in range(2)]
    small_s = _pack_small(conv_a[0], sgu_norm, conv_ffn, fsp)[None]
    in_spec, out_spec, gu_spec, down_spec = (1, ein), (0, ro), (0, fsp), (0, fsp)
    full_in, full_out, full_gu, full_down = full((d, N_DEV * ein)), full((N_DEV * ro, d)), full((fp, d)), full((fp, d))
    win_e, wout_e, gsmall = _alone(_Gather([w_in_even[0].astype(BF16), w_out_even[0].astype(BF16), small_s], [in_spec, out_spec, (0, 1)],
                                           [full_in, full_out, full((N_DEV, SMALL_ROWS, fsp), F32)]), "gather_mix0")
    ca_full = jnp.moveaxis(gsmall[:, 0:3, 0:conv_a.shape[2]], 0, 1).reshape(3, -1)
    sgu_full = gsmall[:, 8, 0:sgu_norm.shape[1]].reshape(1, -1)
    cf_full = jnp.moveaxis(gsmall[:, 16:22, :].reshape(N_DEV, 2, 3, fsp), 0, 2).reshape(2, 3, fp)
    cb_full = _pad_slots(b_conv_ffn, fs, fsp)

    tril = jnp.tril(jnp.ones((CHUNK, CHUNK), F32))
    ws_m = w_spatial[0] * tril
    ws_b = ws_m.astype(BF16)
    wst_b = jnp.swapaxes(ws_m, 1, 2).astype(BF16)
    bfull = jnp.repeat(b_spatial[0].T, CHUNK, axis=1)
    wpool_b = w_pool[0].astype(BF16)
    wpoolt_b = jnp.swapaxes(w_pool[0], 1, 2).astype(BF16)

    (xn0, proj0, cq0, pooled0, mix0, h1, hn0), (wg0,) = _even_fwd(
        x2, norm_mix[0:1], norm_ffn[0:1], win_e, ca_full, wpool_b, pool_scale, wout_e, tm,
        comm=_Gather([wg_s[0]], [gu_spec], [full_gu]))
    (g0, gc0), (wu0,) = _ffn_gate(hn0, wg0, cf_full[0], cb_full[0:1], tm_wide, tn_fwd, "ffn_gate_l0", comm=_Gather([wu_s[0]], [gu_spec], [full_gu]))
    (up0, a0), (wd0,) = _ffn_up(hn0, wu0, gc0, tm_wide, tn_fwd, "ffn_up_l0", comm=_Gather([wd_s[0]], [down_spec], [full_down]))
    (h2, xn1), (win_o, wout_o) = _ffn_fwd2(a0, wd0, h1, norm_mix[1:2], tm, "ffn_fwd2_l0", comm=_Gather(
        [w_in_odd[0].astype(BF16), w_out_odd[0].astype(BF16)], [in_spec, out_spec], [full_in, full_out]))
    (pre1, gate1, mixo1, h3, hn1), (wg1,) = _odd_fwd(xn1, h2, win_o, sgu_full, ws_b, bfull, wout_o, norm_ffn[1:2], tm,
                                                    comm=_Gather([wg_s[1]], [gu_spec], [full_gu]))
    (g1, gc1), (wu1,) = _ffn_gate(hn1, wg1, cf_full[1], cb_full[1:2], tm_wide, tn_fwd, "ffn_gate_l1", comm=_Gather([wu_s[1]], [gu_spec], [full_gu]))
    (up1, a1), (wd1,) = _ffn_up(hn1, wu1, gc1, tm_wide, tn_fwd, "ffn_up_l1", comm=_Gather([wd_s[1]], [down_spec], [full_down]))
    h4, _ = _ffn_fwd2(a1, wd1, h3, row(final_norm), tm, "ffn_fwd2_l1")

    core = lax.axis_index("c").astype(jnp.int32).reshape(1)
    ts = min(1024, s)
    dh4, dh4b, d_final, lossvec = _loss_bwd(h4, row(final_norm), target, tm)
    gd1 = _wgrad(a1, dh4b, tn, s, "wgrad_down_l1")
    (dg1, dup1, dcw1, dcb1), (o_d1,) = _ffn_bwd1(dh4b, g1, gc1, up1, wd1, cf_full[1], tm_wide, tn, "ffn_bwd1_l1",
                                                 comm=_PairExchange([gd1], [down_spec]))
    p_d1 = _pair_add(gd1, o_d1, down_spec, core, "pair_add_down_l1")
    gg1 = _wgrad(dg1, hn1, tn, s, "wgrad_gate_l1")
    gu1, (o_g1,) = _wgrad(dup1, hn1, tn, s, "wgrad_up_l1", comm=_PairExchange([gg1], [gu_spec]))
    p_g1 = _pair_add(gg1, o_g1, gu_spec, core, "pair_add_gate_l1")
    jobs = _Jobs(_ChipExchange([p_d1]), _PairExchange([gu1], [gu_spec]))
    (dh3, dh3b, d_nffn1), res = _ffn_bwd2(dg1, dup1, wg1, wu1, h3, norm_ffn[1:2], dh4, tm // 2, "ffn_bwd2_l1", comm=jobs)
    (s_d1,), (o_u1,) = jobs.results(res)
    p_u1 = _pair_add(gu1, o_u1, gu_spec, core, "pair_add_up_l1")
    (dpre1, dh2, dh2b, d_nmix1, d_sgu, d_ws, d_b), (s_g1, s_u1) = _odd_bwd(
        dh3b, dh3, wout_o, pre1, gate1, ws_b, wst_b, sgu_full, win_o, h2, norm_mix[1:2], tm, comm=_ChipExchange([p_g1, p_u1]))
    gi1 = _wgrad(xn1, dpre1, 512, ts, "wgrad_in_odd")
    go1, (o_i1,) = _wgrad(mixo1, dh3b, 512, ts, "wgrad_out_odd", comm=_PairExchange([gi1], [in_spec]))
    p_i1 = _pair_add(gi1, o_i1, in_spec, core, "pair_add_in_odd")
    gd0, (o_o1,) = _wgrad(a0, dh2b, tn, s, "wgrad_down_l0", comm=_PairExchange([go1], [out_spec]))
    p_o1 = _pair_add(go1, o_o1, out_spec, core, "pair_add_out_odd")
    d_early = _rep_early(d_nmix1, d_nffn1, d_final, d_b, _unpad_slots(dcb1, fs, fsp), d_ws * tril, lossvec)
    jobs = _Jobs(_ChipExchange([p_i1, p_o1], [], [d_early]), _PairExchange([gd0], [down_spec]))
    (dg0, dup0, dcw0, dcb0), res = _ffn_bwd1(dh2b, g0, gc0, up0, wd0, cf_full[0], tm_wide, tn, "ffn_bwd1_l0", comm=jobs)
    (s_i1, s_o1, r_early), (o_d0,) = jobs.results(res)
    p_d0 = _pair_add(gd0, o_d0, down_spec, core, "pair_add_down_l0")
    gg0, (s_d0,) = _wgrad(dg0, hn0, tn, s, "wgrad_gate_l0", comm=_ChipExchange([p_d0]))
    gu0, (o_g0,) = _wgrad(dup0, hn0, tn, s, "wgrad_up_l0", comm=_PairExchange([gg0], [gu_spec]))
    p_g0 = _pair_add(gg0, o_g0, gu_spec, core, "pair_add_gate_l0")
    jobs = _Jobs(_ChipExchange([p_g0]), _PairExchange([gu0], [gu_spec]))
    (dh1, dh1b, d_nffn0), res = _ffn_bwd2(dg0, dup0, wg0, wu0, h1, norm_ffn[0:1], dh2, tm // 2, "ffn_bwd2_l0", comm=jobs)
    (s_g0,), (o_u0,) = jobs.results(res)
    p_u0 = _pair_add(gu0, o_u0, gu_spec, core, "pair_add_up_l0")
    go0 = _wgrad(mix0, dh1b, 512, ts, "wgrad_out_even")
    jobs = _Jobs(_ChipExchange([p_u0]), _PairExchange([go0], [out_spec]))
    (dproj0, grad_x, d_nmix0, d_ca, d_wp, d_ps), res = _even_bwd(
        dh1b, dh1, wout_e, proj0, cq0, pooled0, ca_full, wpool_b, wpoolt_b, pool_scale, win_e, x2, norm_mix[0:1], tm, comm=jobs)
    (s_u0,), (o_o0,) = jobs.results(res)
    p_o0 = _pair_add(go0, o_o0, out_spec, core, "pair_add_out_even")
    d_small = jnp.stack([_pack_small(a, b, c, fsp) for a, b, c in zip(
        jnp.moveaxis(d_ca.reshape(3, N_DEV, -1), 1, 0), jnp.moveaxis(d_sgu.reshape(1, N_DEV, -1), 1, 0),
        jnp.moveaxis(jnp.stack([dcw0, dcw1]).reshape(2, 3, N_DEV, fsp), 2, 0))])
    d_late = _rep_late(d_nmix0, d_nffn0, d_ps, _unpad_slots(dcb0, fs, fsp), d_wp)
    gi0, (r_small, r_late) = _wgrad(xn0, dproj0, 512, ts, "wgrad_in_even", comm=_ChipExchange([], [d_small], [d_late]))
    jobs = _Jobs(_PairExchange([gi0], [in_spec]), _ChipExchange([p_o0]))
    (o_i0,), (s_o0,) = jobs.results(_alone(jobs, "pair_exchange_in_even"))
    p_i0 = _pair_add(gi0, o_i0, in_spec, core, "pair_add_in_even")
    (s_i0,) = _alone(_ChipExchange([p_i0]), "chip_exchange_last")
    loss = jnp.sum(r_early[:, r_early.shape[1] - 8, :])

    out = {}
    out["w_in_even"] = _adamw(s_i0, w_in_even[0], m_w_in_even[0], v_w_in_even[0], 256, "adamw_in_even")
    out["w_out_even"] = _adamw(s_o0, w_out_even[0], m_w_out_even[0], v_w_out_even[0], ro // 2, "adamw_out_even")
    out["w_in_odd"] = _adamw(s_i1, w_in_odd[0], m_w_in_odd[0], v_w_in_odd[0], 256, "adamw_in_odd")
    out["w_out_odd"] = _adamw(s_o1, w_out_odd[0], m_w_out_odd[0], v_w_out_odd[0], ro // 2, "adamw_out_odd")
    tp = lambda a: jnp.swapaxes(a, 1, 2)
    for nm, s1, s0, w, m, v, back in (("w_ffn_gate", s_g1, s_g0, tp(w_ffn_gate), tp(m_w_ffn_gate), tp(v_w_ffn_gate), tp),
                                      ("w_ffn_up", s_u1, s_u0, tp(w_ffn_up), tp(m_w_ffn_up), tp(v_w_ffn_up), tp),
                                      ("w_ffn_down", s_d1, s_d0, w_ffn_down, m_w_ffn_down, v_w_ffn_down, lambda a: a)):
        l1 = _adamw(s1, w, m, v, fs // 2, "adamw_%s_l1" % nm, layer=1)
        out[nm] = [back(a) for a in _adamw(s0, w, m, v, fs // 2, "adamw_%s_l0" % nm, layer=0, into=l1)]
    small = _adamw(r_small, _pack_small(conv_a[0], sgu_norm, conv_ffn, fsp), _pack_small(m_conv_a[0], m_sgu_norm, m_conv_ffn, fsp),
                   _pack_small(v_conv_a[0], v_sgu_norm, v_conv_ffn, fsp), SMALL_ROWS, "adamw_small")
    no_loss = jnp.zeros((1, REP_COLS), F32)
    early = _adamw(r_early, *[_rep_early(nm[1:2], nf[1:2], fn, bs, bc[1:2], wsp, no_loss) for nm, nf, fn, bs, bc, wsp in (
        (norm_mix, norm_ffn, final_norm, b_spatial, b_conv_ffn, w_spatial), (m_norm_mix, m_norm_ffn, m_final_norm, m_b_spatial, m_b_conv_ffn, m_w_spatial),
        (v_norm_mix, v_norm_ffn, v_final_norm, v_b_spatial, v_b_conv_ffn, v_w_spatial))], r_early.shape[1], "adamw_replicated_early")
    late = _adamw(r_late, *[_rep_late(nm[0:1], nf[0:1], ps, bc[0:1], wp) for nm, nf, ps, bc, wp in (
        (norm_mix, norm_ffn, pool_scale, b_conv_ffn, w_pool), (m_norm_mix, m_norm_ffn, m_pool_scale, m_b_conv_ffn, m_w_pool),
        (v_norm_mix, v_norm_ffn, v_pool_scale, v_b_conv_ffn, v_w_pool))], r_late.shape[1], "adamw_replicated_late")

    names = ["norm_mix", "norm_ffn", "final_norm", "w_in_even", "conv_a", "w_pool", "pool_scale", "w_out_even", "w_in_odd", "sgu_norm",
             "w_spatial", "b_spatial", "w_out_odd", "w_ffn_gate", "w_ffn_up", "conv_ffn", "b_conv_ffn", "w_ffn_down"]
    like = {"norm_mix": norm_mix, "norm_ffn": norm_ffn, "final_norm": final_norm, "w_in_even": w_in_even, "conv_a": conv_a,
            "w_pool": w_pool, "pool_scale": pool_scale, "w_out_even": w_out_even, "w_in_odd": w_in_odd, "sgu_norm": sgu_norm,
            "w_spatial": w_spatial, "b_spatial": b_spatial, "w_out_odd": w_out_odd, "w_ffn_gate": w_ffn_gate, "w_ffn_up": w_ffn_up,
            "conv_ffn": conv_ffn, "b_conv_ffn": b_conv_ffn, "w_ffn_down": w_ffn_down}
    groups = []
    for k in range(4):
        ca_k, sg_k, cf_k = _unpack_small(small[k], conv_a.shape[2], sgu_norm.shape[1], conv_ffn.shape[2])
        vals = dict(_unpack_rep(late[k], early[k], like))
        vals.update(conv_a=ca_k, sgu_norm=sg_k, conv_ffn=cf_k)
        for nm in ("w_in_even", "w_in_odd", "w_out_even", "w_out_odd", "w_ffn_gate", "w_ffn_up", "w_ffn_down"):
            vals[nm] = out[nm][k]
        groups.append([vals[nm].reshape(like[nm].shape) for nm in names])
    return (loss, grad_x[None], *groups[0], *groups[1], *groups[2], *groups[3])
```

```python
import functools

import jax
import jax.numpy as jnp
from jax import lax
from jax.experimental import pallas as pl
from jax.experimental.pallas import tpu as pltpu

F32, BF16 = jnp.float32, jnp.bfloat16
EPS = 1e-6
WINDOWS = (2, 4, 8, 16)
HALO = 16
CHUNK = 128
N_DEV = 8
N_CHIP = 4
MESH = pl.DeviceIdType.MESH
VMEM_LIMIT = 56 * 2**20
LANE = 128
ADAM_LR, ADAM_B1, ADAM_B2, ADAM_EPS, ADAM_WD, ADAM_STEP = 0.001, 0.9, 0.999, 1e-08, 0.01, 10
LATE_NUM, LATE_DEN = 7, 8
INV_SQRT2 = 0.7071067811865476
INV_SQRT2PI = 0.3989422804014327


def _pcall(body, comm=None, **kw):
    if comm is None:
        return pl.pallas_call(body, **kw)
    in_specs, out_specs, out_shape = list(kw.pop("in_specs")), kw.pop("out_specs"), kw.pop("out_shape")
    single = not isinstance(out_shape, (list, tuple))
    out_specs, out_shape = ([out_specs], [out_shape]) if single else (list(out_specs), list(out_shape))
    scratch = list(kw.pop("scratch_shapes", []))
    grid = kw.get("grid", ())
    n_in, n_out, n_scr, c_in, c_out = len(in_specs), len(out_specs), len(scratch), len(comm.ins), len(comm.out_shape)

    def hosted(*refs):
        cuts = [0, n_in, n_in + c_in, n_in + c_in + n_out, n_in + c_in + n_out + c_out, n_in + c_in + n_out + c_out + n_scr, len(refs)]
        ins, cins, outs, couts, scr, sems = (refs[a:b] for a, b in zip(cuts[:-1], cuts[1:]))
        if grid:
            step, steps = 0, 1
            for axis, size in enumerate(grid):
                step, steps = step * size + pl.program_id(axis), steps * size
            pl.when(step == 0)(lambda: comm.start(cins, couts, sems))
            pl.when(step == (steps * LATE_NUM) // LATE_DEN)(lambda: comm.middle(cins, couts, sems))
            body(*ins, *outs, *scr)
            pl.when(step == steps - 1)(lambda: comm.finish(cins, couts, sems))
        else:
            comm.start(cins, couts, sems)
            comm.middle(cins, couts, sems)
            body(*ins, *outs, *scr)
            comm.finish(cins, couts, sems)

    call = pl.pallas_call(hosted, in_specs=in_specs + [_hbm()] * c_in, out_specs=out_specs + [_hbm()] * c_out,
                          out_shape=out_shape + list(comm.out_shape), scratch_shapes=scratch + list(comm.sems), **kw)

    def run(*args):
        res = call(*args, *comm.ins)
        own = res[0] if single else res[:n_out]
        return own, res[n_out:]

    return run


def _params(*sem):
    return pltpu.CompilerParams(dimension_semantics=sem, vmem_limit_bytes=VMEM_LIMIT)


def _whole(shape):
    return pl.BlockSpec(shape, lambda *_: (0,) * len(shape))


def _resident(shape):
    return pl.BlockSpec(shape, lambda *_: (0,) * len(shape), pipeline_mode=pl.Buffered(1))


def _rows(i, tm):
    return pl.ds(pl.multiple_of(i * tm, tm), tm)


def _nn(a, b):
    return jnp.dot(a, b, preferred_element_type=F32)


def _nt(a, b):
    return lax.dot_general(a, b, (((1,), (1,)), ((), ())), preferred_element_type=F32)


def _tn(a, b):
    return lax.dot_general(a, b, (((0,), (0,)), ((), ())), preferred_element_type=F32)


def _rms(x, gain):
    r = lax.rsqrt(jnp.mean(x * x, axis=-1, keepdims=True) + EPS)
    return x * r * gain


def _rms_bwd(dy, x, gain):
    r = lax.rsqrt(jnp.mean(x * x, axis=-1, keepdims=True) + EPS)
    xh = x * r
    dgain = jnp.sum(dy * xh, axis=0, keepdims=True)
    dxh = dy * gain
    dx = r * (dxh - xh * jnp.mean(dxh * xh, axis=-1, keepdims=True))
    return dx, dgain


def _gelu(x):
    return 0.5 * x * (1.0 + lax.erf(x * INV_SQRT2))


def _gelu_grad(x):
    return 0.5 * (1.0 + lax.erf(x * INV_SQRT2)) + x * jnp.exp(-0.5 * x * x) * INV_SQRT2PI


def _acc(ref, val, first):
    @pl.when(first)
    def _():
        ref[...] = val

    @pl.when(jnp.logical_not(first))
    def _():
        ref[...] += val


def _shift(ext, k, back):
    n = ext.shape[0]
    return pltpu.roll(ext, k if back else n - k, axis=0)


def _window_sum(ext, w, back):
    total, step = ext, 1
    while step < w:
        total = total + _shift(total, step, back)
        step *= 2
    return total


def _counts(row0, tm, w):
    pos1 = (row0 + lax.broadcasted_iota(jnp.int32, (tm, 1), 0) + 1).astype(F32)
    return jnp.minimum(pos1, float(w))


def _even_fwd(x, gmix, gffn, win, conva, wpool, pscale, wout, tm, comm=None):
    s, d = x.shape
    e = win.shape[1]
    aw = e // 4

    def body(x_ref, gmix_ref, gffn_ref, win_ref, ca_ref, wp_ref, ps_ref, wout_ref,
             xn_ref, proj_ref, cq_ref, pooled_ref, mix_ref, h_ref, hn_ref, qbuf, zbuf):
        i = pl.program_id(0)

        @pl.when(i == 0)
        def _():
            qbuf[0:HALO, :] = jnp.zeros((HALO, aw), F32)
            zbuf[0:HALO, :] = jnp.zeros((HALO, aw), F32)

        xv = x_ref[...]
        xn = _rms(xv, gmix_ref[...]).astype(BF16)
        xn_ref[...] = xn
        proj = _nn(xn, win_ref[...])
        proj_ref[...] = proj.astype(BF16)
        a_b, a_c, a_v, z = (proj[:, k * aw:(k + 1) * aw] for k in range(4))
        q = a_c * a_v
        qbuf[HALO:HALO + tm, :] = q
        qext = qbuf[...]
        cur = slice(HALO, HALO + tm)
        cq = ca_ref[2:3, :] * q + ca_ref[1:2, :] * _shift(qext, 1, True)[cur, :] + ca_ref[0:1, :] * _shift(qext, 2, True)[cur, :]
        cq_ref[...] = cq.astype(BF16)
        y_a = a_b * cq
        zbuf[HALO:HALO + tm, :] = z
        zext = zbuf[...]
        ys = []
        for g, w in enumerate(WINDOWS):
            cols = slice(g * LANE, (g + 1) * LANE)
            acc = _window_sum(zext[:, cols], w, True)[cur, :]
            pooled = (acc / _counts(i * tm, tm, w) - z[:, cols]).astype(BF16)
            pooled_ref[:, cols] = pooled
            ys.append(_nn(pooled, wp_ref[g]))
        y_b = jnp.concatenate(ys, axis=1) * ps_ref[...]
        mix = jnp.concatenate([y_a, y_b], axis=1).astype(BF16)
        mix_ref[...] = mix
        h = xv + _nn(mix, wout_ref[...])
        h_ref[...] = h
        hn_ref[...] = _rms(h, gffn_ref[...]).astype(BF16)
        qbuf[0:HALO, :] = qbuf[tm:tm + HALO, :]
        zbuf[0:HALO, :] = zbuf[tm:tm + HALO, :]

    row = lambda c: pl.BlockSpec((tm, c), lambda i: (i, 0))
    return _pcall(
        body, comm=comm, name="even_fwd", grid=(s // tm,),
        in_specs=[row(d), _whole((1, d)), _whole((1, d)), _whole(win.shape), _whole(conva.shape), _whole(wpool.shape),
                  _whole(pscale.shape), _whole(wout.shape)],
        out_specs=[row(d), row(e), row(aw), row(aw), row(d), row(d), row(d)],
        out_shape=[jax.ShapeDtypeStruct((s, d), BF16), jax.ShapeDtypeStruct((s, e), BF16), jax.ShapeDtypeStruct((s, aw), BF16),
                   jax.ShapeDtypeStruct((s, aw), BF16), jax.ShapeDtypeStruct((s, d), BF16), jax.ShapeDtypeStruct((s, d), F32),
                   jax.ShapeDtypeStruct((s, d), BF16)],
        scratch_shapes=[pltpu.VMEM((tm + HALO, aw), F32), pltpu.VMEM((tm + HALO, aw), F32)],
        compiler_params=_params("arbitrary"),
    )(x, gmix, gffn, win, conva, wpool, pscale, wout)


def _ffn_gate(hn, wgt, cw, cb, tm, tn, name, comm=None):
    s, d = hn.shape
    f = wgt.shape[0]

    def body(hn_ref, wg_ref, cw_ref, cb_ref, g_ref, gc_ref, gbuf):
        i = pl.program_id(1)

        @pl.when(i == 0)
        def _():
            gbuf[0:HALO, :] = jnp.zeros((HALO, tn), F32)

        g = _nt(hn_ref[_rows(i, tm), :], wg_ref[...])
        g_ref[...] = g.astype(BF16)
        gbuf[HALO:HALO + tm, :] = g
        ext = gbuf[...]
        gc = (cw_ref[2:3, :] * g + cw_ref[1:2, :] * pltpu.roll(ext, 1, axis=0)[HALO:HALO + tm, :]
              + cw_ref[0:1, :] * pltpu.roll(ext, 2, axis=0)[HALO:HALO + tm, :] + cb_ref[...])
        gc_ref[...] = gc.astype(BF16)
        gbuf[0:HALO, :] = gbuf[tm:tm + HALO, :]

    tile = pl.BlockSpec((tm, tn), lambda j, i: (i, j))
    wcol = lambda r: pl.BlockSpec((r, tn), lambda j, i: (0, j))
    out = jax.ShapeDtypeStruct((s, f), BF16)
    return _pcall(
        body, comm=comm, name=name, grid=(f // tn, s // tm),
        in_specs=[_resident((s, d)), pl.BlockSpec((tn, d), lambda j, i: (j, 0)), wcol(3), wcol(1)],
        out_specs=[tile, tile], out_shape=[out, out],
        scratch_shapes=[pltpu.VMEM((tm + HALO, tn), F32)],
        compiler_params=_params("arbitrary", "arbitrary"),
    )(hn, wgt, cw, cb)


def _ffn_up(hn, wut, gc, tm, tn, name, comm=None):
    s, d = hn.shape
    f = wut.shape[0]

    def body(hn_ref, wu_ref, gc_ref, up_ref, a_ref):
        up = _nt(hn_ref[_rows(pl.program_id(1), tm), :], wu_ref[...])
        up_ref[...] = up.astype(BF16)
        gc = gc_ref[...].astype(F32)
        a_ref[...] = (gc * jax.nn.sigmoid(gc) * up).astype(BF16)

    tile = pl.BlockSpec((tm, tn), lambda j, i: (i, j))
    out = jax.ShapeDtypeStruct((s, f), BF16)
    return _pcall(
        body, comm=comm, name=name, grid=(f // tn, s // tm),
        in_specs=[_resident((s, d)), pl.BlockSpec((tn, d), lambda j, i: (j, 0)), tile],
        out_specs=[tile, tile], out_shape=[out, out],
        compiler_params=_params("arbitrary", "arbitrary"),
    )(hn, wut, gc)


def _ffn_fwd2(a, wd, h, gain, tm, name, comm=None):
    s, d = h.shape
    f = a.shape[1]

    def body(a_ref, wd_ref, h_ref, gain_ref, ho_ref, hn_ref):
        ho = h_ref[...] + _nn(a_ref[...], wd_ref[...])
        ho_ref[...] = ho
        hn_ref[...] = _rms(ho, gain_ref[...]).astype(BF16)

    row = lambda c: pl.BlockSpec((tm, c), lambda i: (i, 0))
    return _pcall(
        body, comm=comm, name=name, grid=(s // tm,),
        in_specs=[row(f), _resident(wd.shape), row(d), _whole((1, d))],
        out_specs=[row(d), row(d)],
        out_shape=[jax.ShapeDtypeStruct((s, d), F32), jax.ShapeDtypeStruct((s, d), BF16)],
        compiler_params=_params("arbitrary"),
    )(a, wd, h, gain)


def _odd_fwd(xn, h, win, sgu, ws, bfull, wout, gffn, tm, comm=None):
    s, d = h.shape
    e = win.shape[1]
    cw = e // 2
    heads = ws.shape[0]

    def body(xn_ref, h_ref, win_ref, sgu_ref, ws_ref, b_ref, wout_ref, gffn_ref,
             pre_ref, gate_ref, mixo_ref, ho_ref, hn_ref, gbuf):
        pre = _nn(xn_ref[...], win_ref[...])
        pre_ref[...] = pre.astype(BF16)
        p = _gelu(pre)
        u, v = p[:, :cw], p[:, cw:]
        vn = _rms(v, sgu_ref[...]).astype(BF16)
        for n in range(tm // CHUNK):
            rows = slice(n * CHUNK, (n + 1) * CHUNK)
            for hd in range(heads):
                cols = slice(hd * CHUNK, (hd + 1) * CHUNK)
                gbuf[rows, cols] = _nn(ws_ref[hd], vn[rows, cols]) + b_ref[:, cols]
        gate = gbuf[...]
        gate_ref[...] = gate.astype(BF16)
        mixo = (u * gate).astype(BF16)
        mixo_ref[...] = mixo
        ho = h_ref[...] + _nn(mixo, wout_ref[...])
        ho_ref[...] = ho
        hn_ref[...] = _rms(ho, gffn_ref[...]).astype(BF16)

    row = lambda c: pl.BlockSpec((tm, c), lambda i: (i, 0))
    return _pcall(
        body, comm=comm, name="odd_fwd", grid=(s // tm,),
        in_specs=[row(d), row(d), _whole(win.shape), _whole(sgu.shape), _whole(ws.shape), _whole(bfull.shape),
                  _whole(wout.shape), _whole((1, d))],
        out_specs=[row(e), row(cw), row(cw), row(d), row(d)],
        out_shape=[jax.ShapeDtypeStruct((s, e), BF16), jax.ShapeDtypeStruct((s, cw), BF16), jax.ShapeDtypeStruct((s, cw), BF16),
                   jax.ShapeDtypeStruct((s, d), F32), jax.ShapeDtypeStruct((s, d), BF16)],
        scratch_shapes=[pltpu.VMEM((tm, cw), F32)],
        compiler_params=_params("arbitrary"),
    )(xn, h, win, sgu, ws, bfull, wout, gffn)


def _loss_bwd(h, gain, target, tm):
    s, d = h.shape

    def body(h_ref, gain_ref, t_ref, dh_ref, dhb_ref, dgain_ref, loss_ref):
        i = pl.program_id(0)
        hv = h_ref[...]
        gain = gain_ref[...]
        err = _rms(hv, gain) - t_ref[...]
        dy = err * (1.0 / d)
        dx, dgain = _rms_bwd(dy, hv, gain)
        dh_ref[...] = dx
        dhb_ref[...] = dx.astype(BF16)
        _acc(dgain_ref, dgain, i == 0)
        _acc(loss_ref, jnp.sum(err * err, axis=0, keepdims=True) * (0.5 / d), i == 0)

    row = pl.BlockSpec((tm, d), lambda i: (i, 0))
    return _pcall(
        body, name="loss_bwd", grid=(s // tm,),
        in_specs=[row, _whole((1, d)), row],
        out_specs=[row, row, _whole((1, d)), _whole((1, d))],
        out_shape=[jax.ShapeDtypeStruct((s, d), F32), jax.ShapeDtypeStruct((s, d), BF16), jax.ShapeDtypeStruct((1, d), F32),
                   jax.ShapeDtypeStruct((1, d), F32)],
        compiler_params=_params("arbitrary"),
    )(h, gain, target)


def _ffn_bwd1(dhb, g, gc, up, wd, cw, tm, tn, name, comm=None):
    s, d = dhb.shape
    f = g.shape[1]
    ni = s // tm

    def body(dh_ref, g_ref, gc_ref, up_ref, wd_ref, cw_ref, dg_ref, dup_ref, dcw_ref, dcb_ref, ebuf):
        i = pl.program_id(1)

        @pl.when(i == 0)
        def _():
            ebuf[tm:tm + HALO, :] = jnp.zeros((HALO, tn), F32)

        da = _nt(dh_ref[_rows(ni - 1 - i, tm), :], wd_ref[...])
        gcv = gc_ref[...].astype(F32)
        sg = jax.nn.sigmoid(gcv)
        silu = gcv * sg
        dup_ref[...] = (da * silu).astype(BF16)
        dgc = da * up_ref[...].astype(F32) * (sg * (1.0 - silu) + silu)
        ebuf[0:tm, :] = dgc
        ext = ebuf[...]
        s1 = pltpu.roll(ext, tm + HALO - 1, axis=0)[0:tm, :]
        s2 = pltpu.roll(ext, tm + HALO - 2, axis=0)[0:tm, :]
        dg_ref[...] = (cw_ref[2:3, :] * dgc + cw_ref[1:2, :] * s1 + cw_ref[0:1, :] * s2).astype(BF16)
        gv = g_ref[...].astype(F32)
        for k, shifted in enumerate((s2, s1, dgc)):
            _acc(dcw_ref.at[k:k + 1, :], jnp.sum(shifted * gv, axis=0, keepdims=True), i == 0)
        _acc(dcb_ref, jnp.sum(dgc, axis=0, keepdims=True), i == 0)
        ebuf[tm:tm + HALO, :] = ebuf[0:HALO, :]

    tile = pl.BlockSpec((tm, tn), lambda j, i: (ni - 1 - i, j))
    wcol = lambda r: pl.BlockSpec((r, tn), lambda j, i: (0, j))
    out = jax.ShapeDtypeStruct((s, f), BF16)
    return _pcall(
        body, comm=comm, name=name, grid=(f // tn, ni),
        in_specs=[_resident((s, d)), tile, tile, tile,
                  pl.BlockSpec((tn, d), lambda j, i: (j, 0)), wcol(3)],
        out_specs=[tile, tile, wcol(3), wcol(1)],
        out_shape=[out, out, jax.ShapeDtypeStruct((3, f), F32), jax.ShapeDtypeStruct((1, f), F32)],
        scratch_shapes=[pltpu.VMEM((tm + HALO, tn), F32)],
        compiler_params=_params("arbitrary", "arbitrary"),
    )(dhb, g, gc, up, wd, cw)


def _ffn_bwd2(dg, dup, wg, wu, h, gain, dh, tm, name, comm=None):
    s, d = h.shape
    f = dg.shape[1]

    def body(dg_ref, dup_ref, wg_ref, wu_ref, h_ref, gain_ref, dh_ref, dho_ref, dhb_ref, dgain_ref):
        dhn = _nn(dg_ref[...], wg_ref[...]) + _nn(dup_ref[...], wu_ref[...])
        dx, dgain = _rms_bwd(dhn, h_ref[...], gain_ref[...])
        dho = dh_ref[...] + dx
        dho_ref[...] = dho
        dhb_ref[...] = dho.astype(BF16)
        _acc(dgain_ref, dgain, pl.program_id(0) == 0)

    row = lambda c: pl.BlockSpec((tm, c), lambda i: (i, 0))
    return _pcall(
        body, comm=comm, name=name, grid=(s // tm,),
        in_specs=[row(f), row(f), _resident(wg.shape), _resident(wu.shape), row(d), _whole((1, d)), row(d)],
        out_specs=[row(d), row(d), _whole((1, d))],
        out_shape=[jax.ShapeDtypeStruct((s, d), F32), jax.ShapeDtypeStruct((s, d), BF16), jax.ShapeDtypeStruct((1, d), F32)],
        compiler_params=_params("arbitrary"),
    )(dg, dup, wg, wu, h, gain, dh)


def _odd_bwd(dhb, dh, wout, pre, gate, ws, wst, sgu, win, h, gmix, tm, comm=None):
    s, d = h.shape
    e = win.shape[1]
    cw = e // 2
    heads = ws.shape[0]
    ni = s // tm

    def body(dhb_ref, dh_ref, wout_ref, pre_ref, gate_ref, ws_ref, wst_ref, sgu_ref, win_ref, h_ref, gmix_ref,
             dpre_ref, dho_ref, dhob_ref, dgain_ref, dsgu_ref, dws_ref, db_ref, vbuf, gacc):
        i = pl.program_id(0)
        first = i == 0
        dmixo = _nt(dhb_ref[...], wout_ref[...])
        pre = pre_ref[...].astype(F32)
        p = _gelu(pre)
        u, v = p[:, :cw], p[:, cw:]
        sgu = sgu_ref[...]
        rv = lax.rsqrt(jnp.mean(v * v, axis=-1, keepdims=True) + EPS)
        vh = v * rv
        vn = (vh * sgu).astype(BF16)
        du = dmixo * gate_ref[...].astype(F32)
        dgate = dmixo * u
        dgate_b = dgate.astype(BF16)
        gsum = dgate[0:CHUNK, :]
        for n in range(1, tm // CHUNK):
            gsum = gsum + dgate[n * CHUNK:(n + 1) * CHUNK, :]
        _acc(gacc, gsum, first)
        for hd in range(heads):
            cols = slice(hd * CHUNK, (hd + 1) * CHUNK)
            dws = None
            for n in range(tm // CHUNK):
                rows = slice(n * CHUNK, (n + 1) * CHUNK)
                vbuf[rows, cols] = _nn(wst_ref[hd], dgate_b[rows, cols])
                part = _nt(dgate_b[rows, cols], vn[rows, cols])
                dws = part if dws is None else dws + part
            _acc(dws_ref.at[hd], dws, first)
        dvn = vbuf[...]
        _acc(dsgu_ref, jnp.sum(dvn * vh, axis=0, keepdims=True), first)
        dvh = dvn * sgu
        dv = rv * (dvh - vh * jnp.mean(dvh * vh, axis=-1, keepdims=True))
        dpre = (jnp.concatenate([du, dv], axis=1) * _gelu_grad(pre)).astype(BF16)
        dpre_ref[...] = dpre
        dx, dgain = _rms_bwd(_nt(dpre, win_ref[...]), h_ref[...], gmix_ref[...])
        dho = dh_ref[...] + dx
        dho_ref[...] = dho
        dhob_ref[...] = dho.astype(BF16)
        _acc(dgain_ref, dgain, first)

        @pl.when(i == ni - 1)
        def _():
            ones = jnp.ones((8, CHUNK), F32)
            for hd in range(heads):
                tot = lax.dot_general(ones, gacc[:, hd * CHUNK:(hd + 1) * CHUNK], (((1,), (1,)), ((), ())),
                                      preferred_element_type=F32, precision=lax.Precision.HIGHEST)
                db_ref[hd:hd + 1, :] = tot[0:1, :]

    row = lambda c: pl.BlockSpec((tm, c), lambda i: (i, 0))
    return _pcall(
        body, comm=comm, name="odd_bwd", grid=(ni,),
        in_specs=[row(d), row(d), _whole(wout.shape), row(e), row(cw), _whole(ws.shape), _whole(wst.shape), _whole(sgu.shape),
                  _whole(win.shape), row(d), _whole((1, d))],
        out_specs=[row(e), row(d), row(d), _whole((1, d)), _whole((1, cw)), _whole(ws.shape), _whole((heads, CHUNK))],
        out_shape=[jax.ShapeDtypeStruct((s, e), BF16), jax.ShapeDtypeStruct((s, d), F32), jax.ShapeDtypeStruct((s, d), BF16),
                   jax.ShapeDtypeStruct((1, d), F32), jax.ShapeDtypeStruct((1, cw), F32), jax.ShapeDtypeStruct(ws.shape, F32),
                   jax.ShapeDtypeStruct((heads, CHUNK), F32)],
        scratch_shapes=[pltpu.VMEM((tm, cw), F32), pltpu.VMEM((CHUNK, cw), F32)],
        compiler_params=_params("arbitrary"),
    )(dhb, dh, wout, pre, gate, ws, wst, sgu, win, h, gmix)


def _even_bwd(dhb, dh, wout, proj, cq, pooled, conva, wpool, wpoolt, pscale, win, x, gmix, tm, comm=None):
    s, d = x.shape
    e = win.shape[1]
    aw = e // 4
    ni = s // tm

    def body(dhb_ref, dh_ref, wout_ref, proj_ref, cq_ref, pooled_ref, ca_ref, wp_ref, wpt_ref, ps_ref, win_ref, x_ref, gmix_ref,
             dproj_ref, dx_ref, dgain_ref, dca_ref, dwp_ref, dps_ref, cbuf, ebuf):
        i = pl.program_id(0)
        first = i == 0

        @pl.when(first)
        def _():
            cbuf[tm:tm + HALO, :] = jnp.zeros((HALO, aw), F32)
            ebuf[tm:tm + HALO, :] = jnp.zeros((HALO, aw), F32)

        dmix = _nt(dhb_ref[...], wout_ref[...])
        dy_a, dy_b = dmix[:, :aw], dmix[:, aw:]
        proj = proj_ref[...].astype(F32)
        a_b, a_c, a_v = (proj[:, k * aw:(k + 1) * aw] for k in range(3))
        da_b = dy_a * cq_ref[...].astype(F32)
        dcq = dy_a * a_b
        cbuf[0:tm, :] = dcq
        cext = cbuf[...]
        s1 = _shift(cext, 1, False)[0:tm, :]
        s2 = _shift(cext, 2, False)[0:tm, :]
        q = a_c * a_v
        for k, shifted in enumerate((s2, s1, dcq)):
            _acc(dca_ref.at[k:k + 1, :], jnp.sum(shifted * q, axis=0, keepdims=True), first)
        dq = ca_ref[2:3, :] * dcq + ca_ref[1:2, :] * s1 + ca_ref[0:1, :] * s2
        da_c = dq * a_v
        da_v = dq * a_c
        dps, dpool = [], []
        for g, w in enumerate(WINDOWS):
            cols = slice(g * LANE, (g + 1) * LANE)
            pooled = pooled_ref[:, cols]
            mixed = _nn(pooled, wp_ref[g])
            dps.append(jnp.sum(dy_b[:, cols] * mixed, axis=0, keepdims=True))
            dmixed = (dy_b[:, cols] * ps_ref[:, cols]).astype(BF16)
            _acc(dwp_ref.at[g], _tn(pooled, dmixed), first)
            dp = _nn(dmixed, wpt_ref[g])
            dpool.append(dp)
            ebuf[0:tm, cols] = dp / _counts((ni - 1 - i) * tm, tm, w)
        _acc(dps_ref, jnp.concatenate(dps, axis=1), first)
        eext = ebuf[...]
        dzs = []
        for g, w in enumerate(WINDOWS):
            cols = slice(g * LANE, (g + 1) * LANE)
            dzs.append(_window_sum(eext[:, cols], w, False)[0:tm, :] - dpool[g])
        dproj = jnp.concatenate([da_b, da_c, da_v] + dzs, axis=1).astype(BF16)
        dproj_ref[...] = dproj
        dx, dgain = _rms_bwd(_nt(dproj, win_ref[...]), x_ref[...], gmix_ref[...])
        dx_ref[...] = dh_ref[...] + dx
        _acc(dgain_ref, dgain, first)
        cbuf[tm:tm + HALO, :] = cbuf[0:HALO, :]
        ebuf[tm:tm + HALO, :] = ebuf[0:HALO, :]

    row = lambda c: pl.BlockSpec((tm, c), lambda i: (ni - 1 - i, 0))
    return _pcall(
        body, comm=comm, name="even_bwd", grid=(ni,),
        in_specs=[row(d), row(d), _whole(wout.shape), row(e), row(aw), row(aw), _whole(conva.shape), _whole(wpool.shape),
                  _whole(wpoolt.shape), _whole(pscale.shape), _whole(win.shape), row(d), _whole((1, d))],
        out_specs=[row(e), row(d), _whole((1, d)), _whole(conva.shape), _whole(wpool.shape), _whole(pscale.shape)],
        out_shape=[jax.ShapeDtypeStruct((s, e), BF16), jax.ShapeDtypeStruct((s, d), F32), jax.ShapeDtypeStruct((1, d), F32),
                   jax.ShapeDtypeStruct(conva.shape, F32), jax.ShapeDtypeStruct(wpool.shape, F32),
                   jax.ShapeDtypeStruct(pscale.shape, F32)],
        scratch_shapes=[pltpu.VMEM((tm + HALO, aw), F32), pltpu.VMEM((tm + HALO, aw), F32)],
        compiler_params=_params("arbitrary"),
    )(dhb, dh, wout, proj, cq, pooled, conva, wpool, wpoolt, pscale, win, x, gmix)


def _wgrad(a, b, tk, ts, name, comm=None):
    s, ka = a.shape
    nb = b.shape[1]
    nt = s // ts

    def body(a_ref, b_ref, o_ref, acc):
        t = pl.program_id(1)
        _acc(acc, _tn(a_ref[...], b_ref[...]), t == 0)

        @pl.when(t == nt - 1)
        def _():
            o_ref[...] = acc[...].astype(BF16)

    return _pcall(
        body, comm=comm, name=name, grid=(ka // tk, nt),
        in_specs=[pl.BlockSpec((ts, tk), lambda k, t: (t, k)),
                  _resident((s, nb)) if nt == 1 else pl.BlockSpec((ts, nb), lambda k, t: (t, 0))],
        out_specs=pl.BlockSpec((tk, nb), lambda k, t: (k, 0)),
        out_shape=jax.ShapeDtypeStruct((ka, nb), BF16),
        scratch_shapes=[pltpu.VMEM((tk, nb), F32)],
        compiler_params=_params("arbitrary", "arbitrary"),
    )(a, b)


def _adamw(parts, w, m, v, tr, name, layer=None, into=None):
    r, c = w.shape[-2:]
    n, rp, cp = parts.shape
    assert rp >= r and r % tr == 0

    def body(p_ref, w_ref, m_ref, v_ref, *rest):
        g_ref, d_ref, mo_ref, vo_ref = rest[-4:]
        g = p_ref[0, 0:tr, 0:c].astype(F32)
        for j in range(1, n):
            g = g + p_ref[j, 0:tr, 0:c].astype(F32)
        g_ref[...] = g
        mn = ADAM_B1 * m_ref[...] + (1.0 - ADAM_B1) * g
        vn = ADAM_B2 * v_ref[...] + (1.0 - ADAM_B2) * (g * g)
        mo_ref[...] = mn
        vo_ref[...] = vn
        m_hat = mn / (1.0 - ADAM_B1 ** ADAM_STEP)
        v_hat = vn / (1.0 - ADAM_B2 ** ADAM_STEP)
        d_ref[...] = -ADAM_LR * (m_hat / (jnp.sqrt(v_hat) + ADAM_EPS) + ADAM_WD * w_ref[...])

    if layer is None:
        row = pl.BlockSpec((tr, c), lambda i: (i, 0))
    else:
        row = pl.BlockSpec((None, tr, c), lambda i: (layer, i, 0))
    out = jax.ShapeDtypeStruct(w.shape, F32)
    prev = [] if into is None else list(into)
    return _pcall(
        body, name=name, grid=(r // tr,),
        in_specs=[pl.BlockSpec((n, tr, cp), lambda i: (0, i, 0)), row, row, row] + [pl.BlockSpec(memory_space=pl.ANY)] * len(prev),
        out_specs=[row, row, row, row], out_shape=[out, out, out, out],
        input_output_aliases={4 + k: k for k in range(len(prev))},
        compiler_params=_params("arbitrary"),
    )(parts, w, m, v, *prev)


def _pair_add(grad, other, spec, core, name):
    axis, width = spec
    slot = other.shape[1:]

    def body(core_ref, g_ref, o_ref, out_ref):
        out_ref[...] = (g_ref[...].astype(F32) + o_ref[...].astype(F32)).astype(BF16)

    if axis == 0:
        gspec = pl.BlockSpec(slot, lambda q, core_ref: (2 * q + core_ref[0], 0))
    else:
        gspec = pl.BlockSpec(slot, lambda q, core_ref: (0, 2 * q + core_ref[0]))
    per_chip = pl.BlockSpec((None,) + slot, lambda q, core_ref: (q, 0, 0))
    return _pcall(
        body, name=name,
        grid_spec=pltpu.PrefetchScalarGridSpec(num_scalar_prefetch=1, grid=(N_CHIP,), in_specs=[gspec, per_chip], out_specs=per_chip),
        out_shape=jax.ShapeDtypeStruct(other.shape, BF16),
        compiler_params=_params("arbitrary"),
    )(core, grad, other)


def _hbm():
    return pl.BlockSpec(memory_space=pltpu.HBM)


def _window(ref, spec, j):
    axis, width = spec
    start = pl.multiple_of(j * width, width)
    return ref.at[(slice(None),) * axis + (pl.ds(start, width),)]


def _here():
    return lax.axis_index("x"), lax.axis_index("y"), lax.axis_index("c")


class _Gather:
    def __init__(self, shards, specs, fulls):
        n = len(shards)
        self.ins, self.specs, self.out_shape = list(shards), list(specs), list(fulls)
        self.sems = [pltpu.SemaphoreType.DMA((7 * n,)), pltpu.SemaphoreType.DMA((7 * n,)), pltpu.SemaphoreType.DMA((n,))]

    def _plan(self, ins, outs, sems):
        send_sems, recv_sems, local_sems = sems
        x, y, c = _here()
        me, sibling = (x, y, c), (x, y, 1 - c)
        chips = [(1 - x, y), (x, 1 - y), (1 - x, 1 - y)]

        def slot(t, dev):
            return _window(outs[t], self.specs[t], 4 * dev[0] + 2 * dev[1] + dev[2])

        def copy(t, k, block, to, src=None):
            return pltpu.make_async_remote_copy(
                src_ref=slot(t, block) if src is None else src, dst_ref=slot(t, block),
                send_sem=send_sems.at[7 * t + k], recv_sem=recv_sems.at[7 * t + k], device_id=to, device_id_type=MESH)

        plan = []
        for t in range(len(ins)):
            plan.append(dict(
                mine=pltpu.make_async_copy(ins[t], slot(t, me), local_sems.at[t]),
                first=[copy(t, 0, me, sibling, src=ins[t])] + [copy(t, 1 + j, me, (*q, c), src=ins[t]) for j, q in enumerate(chips)],
                over_ici=[copy(t, 1 + j, (*q, c), me) for j, q in enumerate(chips)],
                passed=[copy(t, 4 + j, (*q, c), sibling) for j, q in enumerate(chips)],
                from_sibling=[copy(t, 0, sibling, me)] + [copy(t, 4 + j, (*q, 1 - c), me) for j, q in enumerate(chips)]))
        return plan

    def start(self, ins, outs, sems):
        for p in self._plan(ins, outs, sems):
            p["mine"].start()
            for cp in p["first"]:
                cp.start()

    def middle(self, ins, outs, sems):
        for p in self._plan(ins, outs, sems):
            for arrived, onward in zip(p["over_ici"], p["passed"]):
                arrived.wait_recv()
                onward.start()

    def finish(self, ins, outs, sems):
        plan = self._plan(ins, outs, sems)
        for p in plan:
            for cp in p["from_sibling"]:
                cp.wait_recv()
        for p in plan:
            for cp in p["first"] + p["passed"]:
                cp.wait_send()
            p["mine"].wait()


class _PairExchange:
    def __init__(self, grads, specs):
        n = len(grads)
        self.ins, self.specs = list(grads), list(specs)
        self.out_shape = [jax.ShapeDtypeStruct((N_CHIP,) + a.shape[:sp[0]] + (sp[1],) + a.shape[sp[0] + 1:], a.dtype)
                          for a, sp in zip(grads, specs)]
        self.sems = [pltpu.SemaphoreType.DMA((n,)), pltpu.SemaphoreType.DMA((n,))]

    def start(self, ins, outs, sems):
        send_sems, recv_sems = sems
        x, y, c = _here()
        for t in range(len(ins)):
            for q in range(N_CHIP):
                pltpu.make_async_remote_copy(
                    src_ref=_window(ins[t], self.specs[t], 2 * q + (1 - c)), dst_ref=outs[t].at[q],
                    send_sem=send_sems.at[t], recv_sem=recv_sems.at[t], device_id=(x, y, 1 - c), device_id_type=MESH).start()

    def middle(self, ins, outs, sems):
        pass

    def finish(self, ins, outs, sems):
        send_sems, recv_sems = sems
        x, y, c = _here()
        for t in range(len(ins)):
            every = pltpu.make_async_remote_copy(src_ref=outs[t], dst_ref=outs[t], send_sem=send_sems.at[t],
                                                 recv_sem=recv_sems.at[t], device_id=(x, y, 1 - c), device_id_type=MESH)
            every.wait_send()
            every.wait_recv()


class _ChipExchange:
    def __init__(self, pairs, slotted=(), whole=()):
        self.ins = list(pairs) + list(slotted) + list(whole)
        self.npair, self.nslot = len(pairs), len(pairs) + len(slotted)
        n = len(self.ins)
        self.out_shape = ([jax.ShapeDtypeStruct(a.shape, a.dtype) for a in list(pairs) + list(slotted)]
                          + [jax.ShapeDtypeStruct((N_DEV,) + a.shape, a.dtype) for a in whole])
        self.sems = [pltpu.SemaphoreType.DMA((7 * n,)), pltpu.SemaphoreType.DMA((7 * n,)), pltpu.SemaphoreType.DMA((n,))]

    def _plan(self, ins, outs, sems):
        send_sems, recv_sems, local_sems = sems
        npair, nslot = self.npair, self.nslot
        x, y, c = _here()
        me, chip = 4 * x + 2 * y + c, 2 * x + y
        chips = [(1 - x, y), (x, 1 - y), (1 - x, 1 - y)]
        peers = [(x, y, 1 - c)] + [(*q, c) for q in chips] + [(*q, 1 - c) for q in chips]

        def index(dev):
            return 4 * dev[0] + 2 * dev[1] + dev[2]

        def copy(t, k, arriving):
            peer = peers[k]
            if t < npair:
                src, mine, theirs = ins[t].at[2 * peer[0] + peer[1]], chip, 2 * peer[0] + peer[1]
            else:
                src, mine, theirs = (ins[t].at[index(peer)] if t < nslot else ins[t]), me, index(peer)
            return pltpu.make_async_remote_copy(
                src_ref=src, dst_ref=outs[t].at[theirs if arriving else mine],
                send_sem=send_sems.at[7 * t + k], recv_sem=recv_sems.at[7 * t + k], device_id=peer, device_id_type=MESH)

        own, sent, arriving = [], [], []
        for t in range(len(ins)):
            fan = range(1, 4) if t < npair else range(7)
            if t < npair:
                own.append(pltpu.make_async_copy(ins[t].at[chip], outs[t].at[chip], local_sems.at[t]))
            else:
                own.append(pltpu.make_async_copy(ins[t].at[me] if t < nslot else ins[t], outs[t].at[me], local_sems.at[t]))
            sent += [copy(t, k, False) for k in fan]
            arriving += [copy(t, k, True) for k in fan]
        return own, sent, arriving

    def start(self, ins, outs, sems):
        own, sent, _ = self._plan(ins, outs, sems)
        for cp in own + sent:
            cp.start()

    def middle(self, ins, outs, sems):
        pass

    def finish(self, ins, outs, sems):
        own, sent, arriving = self._plan(ins, outs, sems)
        for cp in arriving:
            cp.wait_recv()
        for cp in sent:
            cp.wait_send()
        for cp in own:
            cp.wait()


class _Jobs:
    def __init__(self, *jobs):
        self.jobs = jobs
        self.ins = [a for j in jobs for a in j.ins]
        self.out_shape = [a for j in jobs for a in j.out_shape]
        self.sems = [a for j in jobs for a in j.sems]

    def _split(self, ins, outs, sems):
        i = o = s = 0
        for j in self.jobs:
            yield j, ins[i:i + len(j.ins)], outs[o:o + len(j.out_shape)], sems[s:s + len(j.sems)]
            i, o, s = i + len(j.ins), o + len(j.out_shape), s + len(j.sems)

    def start(self, ins, outs, sems):
        for j, a, b, c in self._split(ins, outs, sems):
            j.start(a, b, c)

    def middle(self, ins, outs, sems):
        for j, a, b, c in self._split(ins, outs, sems):
            j.middle(a, b, c)

    def finish(self, ins, outs, sems):
        for j, a, b, c in self._split(ins, outs, sems):
            j.finish(a, b, c)

    def results(self, outs):
        return [b for _, _, b, _ in self._split((), outs, ())]


def _alone(job, name):
    return _pcall(lambda: None, comm=job, name=name, in_specs=[], out_specs=[], out_shape=[])()[1]


SMALL_ROWS = 24
REP_COLS = 1024


def _pad_to(a, rows, cols):
    return jnp.pad(a, ((0, rows - a.shape[0]), (0, cols - a.shape[1])))


def _pack_small(conv_a, sgu_norm, conv_ffn, cols):
    return jnp.concatenate([_pad_to(conv_a, 8, cols), _pad_to(sgu_norm, 8, cols),
                            _pad_to(conv_ffn.reshape(-1, conv_ffn.shape[-1]), 8, cols)], axis=0)


def _unpack_small(p, ca_w, sg_w, cf_w):
    return p[0:3, 0:ca_w], p[8:9, 0:sg_w], p[16:22, 0:cf_w].reshape(2, 3, cf_w)


def _tile_rows(rows):
    return -(-rows // 8) * 8


def _pack_rows(parts):
    return jnp.concatenate([_pad_to(a, _tile_rows(a.shape[0]), REP_COLS) for a in parts], axis=0)


def _unpack_rows(p, shapes):
    out, r0 = [], 0
    for r, c in shapes:
        out.append(p[r0:r0 + r, 0:c])
        r0 += _tile_rows(r)
    return out


def _rep_late(norm_mix0, norm_ffn0, pool_scale, b_conv0, w_pool):
    return _pack_rows([norm_mix0, norm_ffn0, pool_scale.reshape(1, -1), _pad_to(b_conv0, 1, 3 * REP_COLS).reshape(3, REP_COLS),
                       w_pool.reshape(-1, REP_COLS)])


def _rep_early(norm_mix1, norm_ffn1, final_norm, b_spatial, b_conv1, w_spatial, loss):
    return _pack_rows([norm_mix1, norm_ffn1, final_norm.reshape(1, -1), b_spatial.reshape(1, -1),
                       _pad_to(b_conv1, 1, 3 * REP_COLS).reshape(3, REP_COLS), w_spatial.reshape(-1, REP_COLS), loss])


def _unpack_rep(late, early, like):
    f = like["b_conv_ffn"].shape[1]
    nm0, nf0, ps, bc0, wp = _unpack_rows(late, [(1, REP_COLS), (1, REP_COLS), (1, like["pool_scale"].shape[1]), (3, REP_COLS),
                                                (like["w_pool"].size // REP_COLS, REP_COLS)])
    nm1, nf1, fin, bs, bc1, wsp, _ = _unpack_rows(early, [(1, REP_COLS)] * 4 + [(3, REP_COLS), (like["w_spatial"].size // REP_COLS, REP_COLS),
                                                          (1, REP_COLS)])
    return {
        "norm_mix": jnp.concatenate([nm0, nm1]), "norm_ffn": jnp.concatenate([nf0, nf1]), "final_norm": fin[0], "pool_scale": ps,
        "b_spatial": bs.reshape(like["b_spatial"].shape),
        "b_conv_ffn": jnp.concatenate([bc0.reshape(1, -1), bc1.reshape(1, -1)])[:, 0:f],
        "w_pool": wp.reshape(like["w_pool"].shape), "w_spatial": wsp.reshape(like["w_spatial"].shape),
    }


def _pad_slots(a, width, padded):
    a = a.reshape(*a.shape[:-1], N_DEV, width)
    a = jnp.pad(a, ((0, 0),) * (a.ndim - 1) + ((0, padded - width),))
    return a.reshape(*a.shape[:-2], N_DEV * padded)


def _unpad_slots(a, width, padded):
    a = a.reshape(*a.shape[:-1], N_DEV, padded)[..., 0:width]
    return a.reshape(*a.shape[:-2], N_DEV * width)


def kernel(x, norm_mix, norm_ffn, final_norm, w_in_even, conv_a, w_pool, pool_scale, w_out_even, w_in_odd, sgu_norm, w_spatial, b_spatial, w_out_odd, w_ffn_gate, w_ffn_up, conv_ffn, b_conv_ffn, w_ffn_down, loss_target, m_norm_mix, m_norm_ffn, m_final_norm, m_w_in_even, m_conv_a, m_w_pool, m_pool_scale, m_w_out_even, m_w_in_odd, m_sgu_norm, m_w_spatial, m_b_spatial, m_w_out_odd, m_w_ffn_gate, m_w_ffn_up, m_conv_ffn, m_b_conv_ffn, m_w_ffn_down, v_norm_mix, v_norm_ffn, v_final_norm, v_w_in_even, v_conv_a, v_w_pool, v_pool_scale, v_w_out_even, v_w_in_odd, v_sgu_norm, v_w_spatial, v_b_spatial, v_w_out_odd, v_w_ffn_gate, v_w_ffn_up, v_conv_ffn, v_b_conv_ffn, v_w_ffn_down):
    s, d = x.shape[1], x.shape[2]
    x2, target = x[0], loss_target[0]
    tm = min(512, s)
    tm_wide = min(2048, s)
    tn = 256
    tk_wide = 1024
    tn_fwd = 512
    row = lambda a: a.reshape(1, -1)
    ein, ro = w_in_even.shape[2], w_out_even.shape[1]
    fs = w_ffn_gate.shape[2]
    fsp = -(-fs // LANE) * LANE
    fp = N_DEV * fsp
    full = lambda shape, dtype=BF16: jax.ShapeDtypeStruct(shape, dtype)

    wg_s = [jnp.pad(w_ffn_gate[l].T, ((0, fsp - fs), (0, 0))).astype(BF16) for l in range(2)]
    wu_s = [jnp.pad(w_ffn_up[l].T, ((0, fsp - fs), (0, 0))).astype(BF16) for l in range(2)]
    wd_s = [jnp.pad(w_ffn_down[l], ((0, fsp - fs), (0, 0))).astype(BF16) for l in range(2)]
    small_s = _pack_small(conv_a[0], sgu_norm, conv_ffn, fsp)[None]
    in_spec, out_spec, gu_spec, down_spec = (1, ein), (0, ro), (0, fsp), (0, fsp)
    full_in, full_out, full_gu, full_down = full((d, N_DEV * ein)), full((N_DEV * ro, d)), full((fp, d)), full((fp, d))
    win_e, wout_e, gsmall = _alone(_Gather([w_in_even[0].astype(BF16), w_out_even[0].astype(BF16), small_s], [in_spec, out_spec, (0, 1)],
                                           [full_in, full_out, full((N_DEV, SMALL_ROWS, fsp), F32)]), "gather_mix0")
    ca_full = jnp.moveaxis(gsmall[:, 0:3, 0:conv_a.shape[2]], 0, 1).reshape(3, -1)
    sgu_full = gsmall[:, 8, 0:sgu_norm.shape[1]].reshape(1, -1)
    cf_full = jnp.moveaxis(gsmall[:, 16:22, :].reshape(N_DEV, 2, 3, fsp), 0, 2).reshape(2, 3, fp)
    cb_full = _pad_slots(b_conv_ffn, fs, fsp)

    tril = jnp.tril(jnp.ones((CHUNK, CHUNK), F32))
    ws_m = w_spatial[0] * tril
    ws_b = ws_m.astype(BF16)
    wst_b = jnp.swapaxes(ws_m, 1, 2).astype(BF16)
    bfull = jnp.repeat(b_spatial[0].T, CHUNK, axis=1)
    wpool_b = w_pool[0].astype(BF16)
    wpoolt_b = jnp.swapaxes(w_pool[0], 1, 2).astype(BF16)

    (xn0, proj0, cq0, pooled0, mix0, h1, hn0), (wg0,) = _even_fwd(
        x2, norm_mix[0:1], norm_ffn[0:1], win_e, ca_full, wpool_b, pool_scale, wout_e, tm,
        comm=_Gather([wg_s[0]], [gu_spec], [full_gu]))
    (g0, gc0), (wu0,) = _ffn_gate(hn0, wg0, cf_full[0], cb_full[0:1], tm_wide, tn_fwd, "ffn_gate_l0", comm=_Gather([wu_s[0]], [gu_spec], [full_gu]))
    (up0, a0), (wd0,) = _ffn_up(hn0, wu0, gc0, tm_wide, tn_fwd, "ffn_up_l0", comm=_Gather([wd_s[0]], [down_spec], [full_down]))
    (h2, xn1), (win_o, wout_o) = _ffn_fwd2(a0, wd0, h1, norm_mix[1:2], min(2 * tm, s), "ffn_fwd2_l0", comm=_Gather(
        [w_in_odd[0].astype(BF16), w_out_odd[0].astype(BF16)], [in_spec, out_spec], [full_in, full_out]))
    (pre1, gate1, mixo1, h3, hn1), (wg1,) = _odd_fwd(xn1, h2, win_o, sgu_full, ws_b, bfull, wout_o, norm_ffn[1:2], tm,
                                                    comm=_Gather([wg_s[1]], [gu_spec], [full_gu]))
    (g1, gc1), (wu1,) = _ffn_gate(hn1, wg1, cf_full[1], cb_full[1:2], tm_wide, tn_fwd, "ffn_gate_l1", comm=_Gather([wu_s[1]], [gu_spec], [full_gu]))
    (up1, a1), (wd1,) = _ffn_up(hn1, wu1, gc1, tm_wide, tn_fwd, "ffn_up_l1", comm=_Gather([wd_s[1]], [down_spec], [full_down]))
    h4, _ = _ffn_fwd2(a1, wd1, h3, row(final_norm), min(2 * tm, s), "ffn_fwd2_l1")

    core = lax.axis_index("c").astype(jnp.int32).reshape(1)
    ts = min(1024, s)
    dh4, dh4b, d_final, lossvec = _loss_bwd(h4, row(final_norm), target, tm)
    gd1 = _wgrad(a1, dh4b, tk_wide, s, "wgrad_down_l1")
    (dg1, dup1, dcw1, dcb1), (o_d1,) = _ffn_bwd1(dh4b, g1, gc1, up1, wd1, cf_full[1], tm_wide, tn, "ffn_bwd1_l1",
                                                 comm=_PairExchange([gd1], [down_spec]))
    p_d1 = _pair_add(gd1, o_d1, down_spec, core, "pair_add_down_l1")
    gg1 = _wgrad(dg1, hn1, tk_wide, s, "wgrad_gate_l1")
    gu1, (o_g1,) = _wgrad(dup1, hn1, tk_wide, s, "wgrad_up_l1", comm=_PairExchange([gg1], [gu_spec]))
    p_g1 = _pair_add(gg1, o_g1, gu_spec, core, "pair_add_gate_l1")
    jobs = _Jobs(_ChipExchange([p_d1]), _PairExchange([gu1], [gu_spec]))
    (dh3, dh3b, d_nffn1), res = _ffn_bwd2(dg1, dup1, wg1, wu1, h3, norm_ffn[1:2], dh4, tm, "ffn_bwd2_l1", comm=jobs)
    (s_d1,), (o_u1,) = jobs.results(res)
    p_u1 = _pair_add(gu1, o_u1, gu_spec, core, "pair_add_up_l1")
    (dpre1, dh2, dh2b, d_nmix1, d_sgu, d_ws, d_b), (s_g1, s_u1) = _odd_bwd(
        dh3b, dh3, wout_o, pre1, gate1, ws_b, wst_b, sgu_full, win_o, h2, norm_mix[1:2], tm, comm=_ChipExchange([p_g1, p_u1]))
    gi1 = _wgrad(xn1, dpre1, tk_wide, ts, "wgrad_in_odd")
    go1, (o_i1,) = _wgrad(mixo1, dh3b, tk_wide, ts, "wgrad_out_odd", comm=_PairExchange([gi1], [in_spec]))
    p_i1 = _pair_add(gi1, o_i1, in_spec, core, "pair_add_in_odd")
    gd0, (o_o1,) = _wgrad(a0, dh2b, tk_wide, s, "wgrad_down_l0", comm=_PairExchange([go1], [out_spec]))
    p_o1 = _pair_add(go1, o_o1, out_spec, core, "pair_add_out_odd")
    d_early = _rep_early(d_nmix1, d_nffn1, d_final, d_b, _unpad_slots(dcb1, fs, fsp), d_ws * tril, lossvec)
    jobs = _Jobs(_ChipExchange([p_i1, p_o1], [], [d_early]), _PairExchange([gd0], [down_spec]))
    (dg0, dup0, dcw0, dcb0), res = _ffn_bwd1(dh2b, g0, gc0, up0, wd0, cf_full[0], tm_wide, tn, "ffn_bwd1_l0", comm=jobs)
    (s_i1, s_o1, r_early), (o_d0,) = jobs.results(res)
    p_d0 = _pair_add(gd0, o_d0, down_spec, core, "pair_add_down_l0")
    gg0, (s_d0,) = _wgrad(dg0, hn0, tk_wide, s, "wgrad_gate_l0", comm=_ChipExchange([p_d0]))
    gu0, (o_g0,) = _wgrad(dup0, hn0, tk_wide, s, "wgrad_up_l0", comm=_PairExchange([gg0], [gu_spec]))
    p_g0 = _pair_add(gg0, o_g0, gu_spec, core, "pair_add_gate_l0")
    jobs = _Jobs(_ChipExchange([p_g0]), _PairExchange([gu0], [gu_spec]))
    (dh1, dh1b, d_nffn0), res = _ffn_bwd2(dg0, dup0, wg0, wu0, h1, norm_ffn[0:1], dh2, tm, "ffn_bwd2_l0", comm=jobs)
    (s_g0,), (o_u0,) = jobs.results(res)
    p_u0 = _pair_add(gu0, o_u0, gu_spec, core, "pair_add_up_l0")
    go0 = _wgrad(mix0, dh1b, tk_wide, ts, "wgrad_out_even")
    jobs = _Jobs(_ChipExchange([p_u0]), _PairExchange([go0], [out_spec]))
    (dproj0, grad_x, d_nmix0, d_ca, d_wp, d_ps), res = _even_bwd(
        dh1b, dh1, wout_e, proj0, cq0, pooled0, ca_full, wpool_b, wpoolt_b, pool_scale, win_e, x2, norm_mix[0:1], tm, comm=jobs)
    (s_u0,), (o_o0,) = jobs.results(res)
    p_o0 = _pair_add(go0, o_o0, out_spec, core, "pair_add_out_even")
    d_small = jnp.stack([_pack_small(a, b, c, fsp) for a, b, c in zip(
        jnp.moveaxis(d_ca.reshape(3, N_DEV, -1), 1, 0), jnp.moveaxis(d_sgu.reshape(1, N_DEV, -1), 1, 0),
        jnp.moveaxis(jnp.stack([dcw0, dcw1]).reshape(2, 3, N_DEV, fsp), 2, 0))])
    d_late = _rep_late(d_nmix0, d_nffn0, d_ps, _unpad_slots(dcb0, fs, fsp), d_wp)
    gi0, (r_small, r_late) = _wgrad(xn0, dproj0, tk_wide, ts, "wgrad_in_even", comm=_ChipExchange([], [d_small], [d_late]))
    jobs = _Jobs(_PairExchange([gi0], [in_spec]), _ChipExchange([p_o0]))
    (o_i0,), (s_o0,) = jobs.results(_alone(jobs, "pair_exchange_in_even"))
    p_i0 = _pair_add(gi0, o_i0, in_spec, core, "pair_add_in_even")
    (s_i0,) = _alone(_ChipExchange([p_i0]), "chip_exchange_last")
    loss = jnp.sum(r_early[:, r_early.shape[1] - 8, :])

    out = {}
    out["w_in_even"] = _adamw(s_i0, w_in_even[0], m_w_in_even[0], v_w_in_even[0], 256, "adamw_in_even")
    out["w_out_even"] = _adamw(s_o0, w_out_even[0], m_w_out_even[0], v_w_out_even[0], ro // 2, "adamw_out_even")
    out["w_in_odd"] = _adamw(s_i1, w_in_odd[0], m_w_in_odd[0], v_w_in_odd[0], 256, "adamw_in_odd")
    out["w_out_odd"] = _adamw(s_o1, w_out_odd[0], m_w_out_odd[0], v_w_out_odd[0], ro // 2, "adamw_out_odd")
    tp = lambda a: jnp.swapaxes(a, 1, 2)
    for nm, s1, s0, w, m, v, back in (("w_ffn_gate", s_g1, s_g0, tp(w_ffn_gate), tp(m_w_ffn_gate), tp(v_w_ffn_gate), tp),
                                      ("w_ffn_up", s_u1, s_u0, tp(w_ffn_up), tp(m_w_ffn_up), tp(v_w_ffn_up), tp),
                                      ("w_ffn_down", s_d1, s_d0, w_ffn_down, m_w_ffn_down, v_w_ffn_down, lambda a: a)):
        l1 = _adamw(s1, w, m, v, fs // 2, "adamw_%s_l1" % nm, layer=1)
        out[nm] = [back(a) for a in _adamw(s0, w, m, v, fs // 2, "adamw_%s_l0" % nm, layer=0, into=l1)]
    small = _adamw(r_small, _pack_small(conv_a[0], sgu_norm, conv_ffn, fsp), _pack_small(m_conv_a[0], m_sgu_norm, m_conv_ffn, fsp),
                   _pack_small(v_conv_a[0], v_sgu_norm, v_conv_ffn, fsp), SMALL_ROWS, "adamw_small")
    no_loss = jnp.zeros((1, REP_COLS), F32)
    early = _adamw(r_early, *[_rep_early(nm[1:2], nf[1:2], fn, bs, bc[1:2], wsp, no_loss) for nm, nf, fn, bs, bc, wsp in (
        (norm_mix, norm_ffn, final_norm, b_spatial, b_conv_ffn, w_spatial), (m_norm_mix, m_norm_ffn, m_final_norm, m_b_spatial, m_b_conv_ffn, m_w_spatial),
        (v_norm_mix, v_norm_ffn, v_final_norm, v_b_spatial, v_b_conv_ffn, v_w_spatial))], r_early.shape[1], "adamw_replicated_early")
    late = _adamw(r_late, *[_rep_late(nm[0:1], nf[0:1], ps, bc[0:1], wp) for nm, nf, ps, bc, wp in (
        (norm_mix, norm_ffn, pool_scale, b_conv_ffn, w_pool), (m_norm_mix, m_norm_ffn, m_pool_scale, m_b_conv_ffn, m_w_pool),
        (v_norm_mix, v_norm_ffn, v_pool_scale, v_b_conv_ffn, v_w_pool))], r_late.shape[1], "adamw_replicated_late")

    names = ["norm_mix", "norm_ffn", "final_norm", "w_in_even", "conv_a", "w_pool", "pool_scale", "w_out_even", "w_in_odd", "sgu_norm",
             "w_spatial", "b_spatial", "w_out_odd", "w_ffn_gate", "w_ffn_up", "conv_ffn", "b_conv_ffn", "w_ffn_down"]
    like = {"norm_mix": norm_mix, "norm_ffn": norm_ffn, "final_norm": final_norm, "w_in_even": w_in_even, "conv_a": conv_a,
            "w_pool": w_pool, "pool_scale": pool_scale, "w_out_even": w_out_even, "w_in_odd": w_in_odd, "sgu_norm": sgu_norm,
            "w_spatial": w_spatial, "b_spatial": b_spatial, "w_out_odd": w_out_odd, "w_ffn_gate": w_ffn_gate, "w_ffn_up": w_ffn_up,
            "conv_ffn": conv_ffn, "b_conv_ffn": b_conv_ffn, "w_ffn_down": w_ffn_down}
    groups = []
    for k in range(4):
        ca_k, sg_k, cf_k = _unpack_small(small[k], conv_a.shape[2], sgu_norm.shape[1], conv_ffn.shape[2])
        vals = dict(_unpack_rep(late[k], early[k], like))
        vals.update(conv_a=ca_k, sgu_norm=sg_k, conv_ffn=cf_k)
        for nm in ("w_in_even", "w_in_odd", "w_out_even", "w_out_odd", "w_ffn_gate", "w_ffn_up", "w_ffn_down"):
            vals[nm] = out[nm][k]
        groups.append([vals[nm].reshape(like[nm].shape) for nm in names])
    return (loss, grad_x[None], *groups[0], *groups[1], *groups[2], *groups[3])
```

```python
import functools

import jax
import jax.numpy as jnp
from jax import lax
from jax.experimental import pallas as pl
from jax.experimental.pallas import tpu as pltpu

F32, BF16 = jnp.float32, jnp.bfloat16
EPS = 1e-6
WINDOWS = (2, 4, 8, 16)
HALO = 16
CHUNK = 128
N_DEV = 8
N_CHIP = 4
MESH = pl.DeviceIdType.MESH
VMEM_LIMIT = 56 * 2**20
LANE = 128
ADAM_LR, ADAM_B1, ADAM_B2, ADAM_EPS, ADAM_WD, ADAM_STEP = 0.001, 0.9, 0.999, 1e-08, 0.01, 10
LATE_NUM, LATE_DEN = 7, 8
INV_SQRT2 = 0.7071067811865476
INV_SQRT2PI = 0.3989422804014327


def _pcall(body, comm=None, **kw):
    if comm is None:
        return pl.pallas_call(body, **kw)
    in_specs, out_specs, out_shape = list(kw.pop("in_specs")), kw.pop("out_specs"), kw.pop("out_shape")
    single = not isinstance(out_shape, (list, tuple))
    out_specs, out_shape = ([out_specs], [out_shape]) if single else (list(out_specs), list(out_shape))
    scratch = list(kw.pop("scratch_shapes", []))
    grid = kw.get("grid", ())
    n_in, n_out, n_scr, c_in, c_out = len(in_specs), len(out_specs), len(scratch), len(comm.ins), len(comm.out_shape)

    def hosted(*refs):
        cuts = [0, n_in, n_in + c_in, n_in + c_in + n_out, n_in + c_in + n_out + c_out, n_in + c_in + n_out + c_out + n_scr, len(refs)]
        ins, cins, outs, couts, scr, sems = (refs[a:b] for a, b in zip(cuts[:-1], cuts[1:]))
        if grid:
            step, steps = 0, 1
            for axis, size in enumerate(grid):
                step, steps = step * size + pl.program_id(axis), steps * size
            pl.when(step == 0)(lambda: comm.start(cins, couts, sems))
            pl.when(step == (steps * LATE_NUM) // LATE_DEN)(lambda: comm.middle(cins, couts, sems))
            body(*ins, *outs, *scr)
            pl.when(step == steps - 1)(lambda: comm.finish(cins, couts, sems))
        else:
            comm.start(cins, couts, sems)
            comm.middle(cins, couts, sems)
            body(*ins, *outs, *scr)
            comm.finish(cins, couts, sems)

    call = pl.pallas_call(hosted, in_specs=in_specs + [_hbm()] * c_in, out_specs=out_specs + [_hbm()] * c_out,
                          out_shape=out_shape + list(comm.out_shape), scratch_shapes=scratch + list(comm.sems), **kw)

    def run(*args):
        res = call(*args, *comm.ins)
        own = res[0] if single else res[:n_out]
        return own, res[n_out:]

    return run


def _params(*sem):
    return pltpu.CompilerParams(dimension_semantics=sem, vmem_limit_bytes=VMEM_LIMIT)


def _whole(shape):
    return pl.BlockSpec(shape, lambda *_: (0,) * len(shape))


def _resident(shape):
    return pl.BlockSpec(shape, lambda *_: (0,) * len(shape), pipeline_mode=pl.Buffered(1))


def _rows(i, tm):
    return pl.ds(pl.multiple_of(i * tm, tm), tm)


def _nn(a, b):
    return jnp.dot(a, b, preferred_element_type=F32)


def _nt(a, b):
    return lax.dot_general(a, b, (((1,), (1,)), ((), ())), preferred_element_type=F32)


def _tn(a, b):
    return lax.dot_general(a, b, (((0,), (0,)), ((), ())), preferred_element_type=F32)


def _rms(x, gain):
    r = lax.rsqrt(jnp.mean(x * x, axis=-1, keepdims=True) + EPS)
    return x * r * gain


def _rms_bwd(dy, x, gain):
    r = lax.rsqrt(jnp.mean(x * x, axis=-1, keepdims=True) + EPS)
    xh = x * r
    dgain = jnp.sum(dy * xh, axis=0, keepdims=True)
    dxh = dy * gain
    dx = r * (dxh - xh * jnp.mean(dxh * xh, axis=-1, keepdims=True))
    return dx, dgain


def _gelu(x):
    return 0.5 * x * (1.0 + lax.erf(x * INV_SQRT2))


def _gelu_grad(x):
    return 0.5 * (1.0 + lax.erf(x * INV_SQRT2)) + x * jnp.exp(-0.5 * x * x) * INV_SQRT2PI


def _acc(ref, val, first):
    @pl.when(first)
    def _():
        ref[...] = val

    @pl.when(jnp.logical_not(first))
    def _():
        ref[...] += val


def _shift(ext, k, back):
    n = ext.shape[0]
    return pltpu.roll(ext, k if back else n - k, axis=0)


def _window_sum(ext, w, back):
    total, step = ext, 1
    while step < w:
        total = total + _shift(total, step, back)
        step *= 2
    return total


def _counts(row0, tm, w):
    pos1 = (row0 + lax.broadcasted_iota(jnp.int32, (tm, 1), 0) + 1).astype(F32)
    return jnp.minimum(pos1, float(w))


def _even_fwd(x, gmix, gffn, win, conva, wpool, pscale, wout, tm, comm=None):
    s, d = x.shape
    e = win.shape[1]
    aw = e // 4

    def body(x_ref, gmix_ref, gffn_ref, win_ref, ca_ref, wp_ref, ps_ref, wout_ref,
             xn_ref, proj_ref, cq_ref, pooled_ref, mix_ref, h_ref, hn_ref, qbuf, zbuf):
        i = pl.program_id(0)

        @pl.when(i == 0)
        def _():
            qbuf[0:HALO, :] = jnp.zeros((HALO, aw), F32)
            zbuf[0:HALO, :] = jnp.zeros((HALO, aw), F32)

        xv = x_ref[...]
        xn = _rms(xv, gmix_ref[...]).astype(BF16)
        xn_ref[...] = xn
        proj = _nn(xn, win_ref[...])
        proj_ref[...] = proj.astype(BF16)
        a_b, a_c, a_v, z = (proj[:, k * aw:(k + 1) * aw] for k in range(4))
        q = a_c * a_v
        qbuf[HALO:HALO + tm, :] = q
        qext = qbuf[...]
        cur = slice(HALO, HALO + tm)
        cq = ca_ref[2:3, :] * q + ca_ref[1:2, :] * _shift(qext, 1, True)[cur, :] + ca_ref[0:1, :] * _shift(qext, 2, True)[cur, :]
        cq_ref[...] = cq.astype(BF16)
        y_a = a_b * cq
        zbuf[HALO:HALO + tm, :] = z
        zext = zbuf[...]
        ys = []
        for g, w in enumerate(WINDOWS):
            cols = slice(g * LANE, (g + 1) * LANE)
            acc = _window_sum(zext[:, cols], w, True)[cur, :]
            pooled = (acc / _counts(i * tm, tm, w) - z[:, cols]).astype(BF16)
            pooled_ref[:, cols] = pooled
            ys.append(_nn(pooled, wp_ref[g]))
        y_b = jnp.concatenate(ys, axis=1) * ps_ref[...]
        mix = jnp.concatenate([y_a, y_b], axis=1).astype(BF16)
        mix_ref[...] = mix
        h = xv + _nn(mix, wout_ref[...])
        h_ref[...] = h
        hn_ref[...] = _rms(h, gffn_ref[...]).astype(BF16)
        qbuf[0:HALO, :] = qbuf[tm:tm + HALO, :]
        zbuf[0:HALO, :] = zbuf[tm:tm + HALO, :]

    row = lambda c: pl.BlockSpec((tm, c), lambda i: (i, 0))
    return _pcall(
        body, comm=comm, name="even_fwd", grid=(s // tm,),
        in_specs=[row(d), _whole((1, d)), _whole((1, d)), _whole(win.shape), _whole(conva.shape), _whole(wpool.shape),
                  _whole(pscale.shape), _whole(wout.shape)],
        out_specs=[row(d), row(e), row(aw), row(aw), row(d), row(d), row(d)],
        out_shape=[jax.ShapeDtypeStruct((s, d), BF16), jax.ShapeDtypeStruct((s, e), BF16), jax.ShapeDtypeStruct((s, aw), BF16),
                   jax.ShapeDtypeStruct((s, aw), BF16), jax.ShapeDtypeStruct((s, d), BF16), jax.ShapeDtypeStruct((s, d), F32),
                   jax.ShapeDtypeStruct((s, d), BF16)],
        scratch_shapes=[pltpu.VMEM((tm + HALO, aw), F32), pltpu.VMEM((tm + HALO, aw), F32)],
        compiler_params=_params("arbitrary"),
    )(x, gmix, gffn, win, conva, wpool, pscale, wout)


def _ffn_gate(hn, wgt, cw, cb, tm, tn, name, comm=None):
    s, d = hn.shape
    f = wgt.shape[0]

    def body(hn_ref, wg_ref, cw_ref, cb_ref, g_ref, gc_ref, gbuf):
        i = pl.program_id(1)

        @pl.when(i == 0)
        def _():
            gbuf[0:HALO, :] = jnp.zeros((HALO, tn), F32)

        g = _nt(hn_ref[_rows(i, tm), :], wg_ref[...])
        g_ref[...] = g.astype(BF16)
        gbuf[HALO:HALO + tm, :] = g
        ext = gbuf[...]
        gc = (cw_ref[2:3, :] * g + cw_ref[1:2, :] * pltpu.roll(ext, 1, axis=0)[HALO:HALO + tm, :]
              + cw_ref[0:1, :] * pltpu.roll(ext, 2, axis=0)[HALO:HALO + tm, :] + cb_ref[...])
        gc_ref[...] = gc.astype(BF16)
        gbuf[0:HALO, :] = gbuf[tm:tm + HALO, :]

    tile = pl.BlockSpec((tm, tn), lambda j, i: (i, j))
    wcol = lambda r: pl.BlockSpec((r, tn), lambda j, i: (0, j))
    out = jax.ShapeDtypeStruct((s, f), BF16)
    return _pcall(
        body, comm=comm, name=name, grid=(f // tn, s // tm),
        in_specs=[_resident((s, d)), pl.BlockSpec((tn, d), lambda j, i: (j, 0)), wcol(3), wcol(1)],
        out_specs=[tile, tile], out_shape=[out, out],
        scratch_shapes=[pltpu.VMEM((tm + HALO, tn), F32)],
        compiler_params=_params("arbitrary", "arbitrary"),
    )(hn, wgt, cw, cb)


def _ffn_up(hn, wut, gc, tm, tn, name, comm=None):
    s, d = hn.shape
    f = wut.shape[0]

    def body(hn_ref, wu_ref, gc_ref, up_ref, a_ref, silu_ref, dsilu_ref):
        up = _nt(hn_ref[_rows(pl.program_id(1), tm), :], wu_ref[...])
        up_ref[...] = up.astype(BF16)
        gc = gc_ref[...].astype(F32)
        sg = jax.nn.sigmoid(gc)
        silu = gc * sg
        silu_ref[...] = silu.astype(BF16)
        dsilu_ref[...] = (sg * (1.0 - silu) + silu).astype(BF16)
        a_ref[...] = (silu * up).astype(BF16)

    tile = pl.BlockSpec((tm, tn), lambda j, i: (i, j))
    out = jax.ShapeDtypeStruct((s, f), BF16)
    return _pcall(
        body, comm=comm, name=name, grid=(f // tn, s // tm),
        in_specs=[_resident((s, d)), pl.BlockSpec((tn, d), lambda j, i: (j, 0)), tile],
        out_specs=[tile, tile, tile, tile], out_shape=[out, out, out, out],
        compiler_params=_params("arbitrary", "arbitrary"),
    )(hn, wut, gc)


def _ffn_fwd2(a, wd, h, gain, tm, name, comm=None):
    s, d = h.shape
    f = a.shape[1]

    def body(a_ref, wd_ref, h_ref, gain_ref, ho_ref, hn_ref):
        ho = h_ref[...] + _nn(a_ref[...], wd_ref[...])
        ho_ref[...] = ho
        hn_ref[...] = _rms(ho, gain_ref[...]).astype(BF16)

    row = lambda c: pl.BlockSpec((tm, c), lambda i: (i, 0))
    return _pcall(
        body, comm=comm, name=name, grid=(s // tm,),
        in_specs=[row(f), _resident(wd.shape), row(d), _whole((1, d))],
        out_specs=[row(d), row(d)],
        out_shape=[jax.ShapeDtypeStruct((s, d), F32), jax.ShapeDtypeStruct((s, d), BF16)],
        compiler_params=_params("arbitrary"),
    )(a, wd, h, gain)


def _odd_fwd(xn, h, win, sgu, ws, bfull, wout, gffn, tm, comm=None):
    s, d = h.shape
    e = win.shape[1]
    cw = e // 2
    heads = ws.shape[0]

    def body(xn_ref, h_ref, win_ref, sgu_ref, ws_ref, b_ref, wout_ref, gffn_ref,
             pre_ref, gate_ref, mixo_ref, ho_ref, hn_ref, gbuf):
        pre = _nn(xn_ref[...], win_ref[...])
        pre_ref[...] = pre.astype(BF16)
        p = _gelu(pre)
        u, v = p[:, :cw], p[:, cw:]
        vn = _rms(v, sgu_ref[...]).astype(BF16)
        for n in range(tm // CHUNK):
            rows = slice(n * CHUNK, (n + 1) * CHUNK)
            for hd in range(heads):
                cols = slice(hd * CHUNK, (hd + 1) * CHUNK)
                gbuf[rows, cols] = _nn(ws_ref[hd], vn[rows, cols]) + b_ref[:, cols]
        gate = gbuf[...]
        gate_ref[...] = gate.astype(BF16)
        mixo = (u * gate).astype(BF16)
        mixo_ref[...] = mixo
        ho = h_ref[...] + _nn(mixo, wout_ref[...])
        ho_ref[...] = ho
        hn_ref[...] = _rms(ho, gffn_ref[...]).astype(BF16)

    row = lambda c: pl.BlockSpec((tm, c), lambda i: (i, 0))
    return _pcall(
        body, comm=comm, name="odd_fwd", grid=(s // tm,),
        in_specs=[row(d), row(d), _whole(win.shape), _whole(sgu.shape), _whole(ws.shape), _whole(bfull.shape),
                  _whole(wout.shape), _whole((1, d))],
        out_specs=[row(e), row(cw), row(cw), row(d), row(d)],
        out_shape=[jax.ShapeDtypeStruct((s, e), BF16), jax.ShapeDtypeStruct((s, cw), BF16), jax.ShapeDtypeStruct((s, cw), BF16),
                   jax.ShapeDtypeStruct((s, d), F32), jax.ShapeDtypeStruct((s, d), BF16)],
        scratch_shapes=[pltpu.VMEM((tm, cw), F32)],
        compiler_params=_params("arbitrary"),
    )(xn, h, win, sgu, ws, bfull, wout, gffn)


def _loss_bwd(h, gain, target, tm):
    s, d = h.shape

    def body(h_ref, gain_ref, t_ref, dh_ref, dhb_ref, dgain_ref, loss_ref):
        i = pl.program_id(0)
        hv = h_ref[...]
        gain = gain_ref[...]
        err = _rms(hv, gain) - t_ref[...]
        dy = err * (1.0 / d)
        dx, dgain = _rms_bwd(dy, hv, gain)
        dh_ref[...] = dx
        dhb_ref[...] = dx.astype(BF16)
        _acc(dgain_ref, dgain, i == 0)
        _acc(loss_ref, jnp.sum(err * err, axis=0, keepdims=True) * (0.5 / d), i == 0)

    row = pl.BlockSpec((tm, d), lambda i: (i, 0))
    return _pcall(
        body, name="loss_bwd", grid=(s // tm,),
        in_specs=[row, _whole((1, d)), row],
        out_specs=[row, row, _whole((1, d)), _whole((1, d))],
        out_shape=[jax.ShapeDtypeStruct((s, d), F32), jax.ShapeDtypeStruct((s, d), BF16), jax.ShapeDtypeStruct((1, d), F32),
                   jax.ShapeDtypeStruct((1, d), F32)],
        compiler_params=_params("arbitrary"),
    )(h, gain, target)


def _ffn_bwd1(dhb, g, silu, dsilu, up, wd, cw, tm, tn, name, comm=None):
    s, d = dhb.shape
    f = g.shape[1]
    ni = s // tm

    def body(dh_ref, g_ref, silu_ref, dsilu_ref, up_ref, wd_ref, cw_ref, dg_ref, dup_ref, dcw_ref, dcb_ref, ebuf):
        i = pl.program_id(1)

        @pl.when(i == 0)
        def _():
            ebuf[tm:tm + HALO, :] = jnp.zeros((HALO, tn), F32)

        da = _nt(dh_ref[_rows(ni - 1 - i, tm), :], wd_ref[...])
        dup_ref[...] = (da * silu_ref[...].astype(F32)).astype(BF16)
        dgc = da * up_ref[...].astype(F32) * dsilu_ref[...].astype(F32)
        ebuf[0:tm, :] = dgc
        ext = ebuf[...]
        s1 = pltpu.roll(ext, tm + HALO - 1, axis=0)[0:tm, :]
        s2 = pltpu.roll(ext, tm + HALO - 2, axis=0)[0:tm, :]
        dg_ref[...] = (cw_ref[2:3, :] * dgc + cw_ref[1:2, :] * s1 + cw_ref[0:1, :] * s2).astype(BF16)
        gv = g_ref[...].astype(F32)
        for k, shifted in enumerate((s2, s1, dgc)):
            _acc(dcw_ref.at[k:k + 1, :], jnp.sum(shifted * gv, axis=0, keepdims=True), i == 0)
        _acc(dcb_ref, jnp.sum(dgc, axis=0, keepdims=True), i == 0)
        ebuf[tm:tm + HALO, :] = ebuf[0:HALO, :]

    tile = pl.BlockSpec((tm, tn), lambda j, i: (ni - 1 - i, j))
    wcol = lambda r: pl.BlockSpec((r, tn), lambda j, i: (0, j))
    out = jax.ShapeDtypeStruct((s, f), BF16)
    return _pcall(
        body, comm=comm, name=name, grid=(f // tn, ni),
        in_specs=[_resident((s, d)), tile, tile, tile, tile,
                  pl.BlockSpec((tn, d), lambda j, i: (j, 0)), wcol(3)],
        out_specs=[tile, tile, wcol(3), wcol(1)],
        out_shape=[out, out, jax.ShapeDtypeStruct((3, f), F32), jax.ShapeDtypeStruct((1, f), F32)],
        scratch_shapes=[pltpu.VMEM((tm + HALO, tn), F32)],
        compiler_params=_params("arbitrary", "arbitrary"),
    )(dhb, g, silu, dsilu, up, wd, cw)


def _ffn_bwd2(dg, dup, wg, wu, h, gain, dh, tm, name, comm=None):
    s, d = h.shape
    f = dg.shape[1]

    def body(dg_ref, dup_ref, wg_ref, wu_ref, h_ref, gain_ref, dh_ref, dho_ref, dhb_ref, dgain_ref):
        dhn = _nn(dg_ref[...], wg_ref[...]) + _nn(dup_ref[...], wu_ref[...])
        dx, dgain = _rms_bwd(dhn, h_ref[...], gain_ref[...])
        dho = dh_ref[...] + dx
        dho_ref[...] = dho
        dhb_ref[...] = dho.astype(BF16)
        _acc(dgain_ref, dgain, pl.program_id(0) == 0)

    row = lambda c: pl.BlockSpec((tm, c), lambda i: (i, 0))
    return _pcall(
        body, comm=comm, name=name, grid=(s // tm,),
        in_specs=[row(f), row(f), _resident(wg.shape), _resident(wu.shape), row(d), _whole((1, d)), row(d)],
        out_specs=[row(d), row(d), _whole((1, d))],
        out_shape=[jax.ShapeDtypeStruct((s, d), F32), jax.ShapeDtypeStruct((s, d), BF16), jax.ShapeDtypeStruct((1, d), F32)],
        compiler_params=_params("arbitrary"),
    )(dg, dup, wg, wu, h, gain, dh)


def _odd_bwd(dhb, dh, wout, pre, gate, ws, wst, sgu, win, h, gmix, tm, comm=None):
    s, d = h.shape
    e = win.shape[1]
    cw = e // 2
    heads = ws.shape[0]
    ni = s // tm

    def body(dhb_ref, dh_ref, wout_ref, pre_ref, gate_ref, ws_ref, wst_ref, sgu_ref, win_ref, h_ref, gmix_ref,
             dpre_ref, dho_ref, dhob_ref, dgain_ref, dsgu_ref, dws_ref, db_ref, vbuf, gacc):
        i = pl.program_id(0)
        first = i == 0
        dmixo = _nt(dhb_ref[...], wout_ref[...])
        pre = pre_ref[...].astype(F32)
        p = _gelu(pre)
        u, v = p[:, :cw], p[:, cw:]
        sgu = sgu_ref[...]
        rv = lax.rsqrt(jnp.mean(v * v, axis=-1, keepdims=True) + EPS)
        vh = v * rv
        vn = (vh * sgu).astype(BF16)
        du = dmixo * gate_ref[...].astype(F32)
        dgate = dmixo * u
        dgate_b = dgate.astype(BF16)
        gsum = dgate[0:CHUNK, :]
        for n in range(1, tm // CHUNK):
            gsum = gsum + dgate[n * CHUNK:(n + 1) * CHUNK, :]
        _acc(gacc, gsum, first)
        for hd in range(heads):
            cols = slice(hd * CHUNK, (hd + 1) * CHUNK)
            dws = None
            for n in range(tm // CHUNK):
                rows = slice(n * CHUNK, (n + 1) * CHUNK)
                vbuf[rows, cols] = _nn(wst_ref[hd], dgate_b[rows, cols])
                part = _nt(dgate_b[rows, cols], vn[rows, cols])
                dws = part if dws is None else dws + part
            _acc(dws_ref.at[hd], dws, first)
        dvn = vbuf[...]
        _acc(dsgu_ref, jnp.sum(dvn * vh, axis=0, keepdims=True), first)
        dvh = dvn * sgu
        dv = rv * (dvh - vh * jnp.mean(dvh * vh, axis=-1, keepdims=True))
        dpre = (jnp.concatenate([du, dv], axis=1) * _gelu_grad(pre)).astype(BF16)
        dpre_ref[...] = dpre
        dx, dgain = _rms_bwd(_nt(dpre, win_ref[...]), h_ref[...], gmix_ref[...])
        dho = dh_ref[...] + dx
        dho_ref[...] = dho
        dhob_ref[...] = dho.astype(BF16)
        _acc(dgain_ref, dgain, first)

        @pl.when(i == ni - 1)
        def _():
            ones = jnp.ones((8, CHUNK), F32)
            for hd in range(heads):
                tot = lax.dot_general(ones, gacc[:, hd * CHUNK:(hd + 1) * CHUNK], (((1,), (1,)), ((), ())),
                                      preferred_element_type=F32, precision=lax.Precision.HIGHEST)
                db_ref[hd:hd + 1, :] = tot[0:1, :]

    row = lambda c: pl.BlockSpec((tm, c), lambda i: (i, 0))
    return _pcall(
        body, comm=comm, name="odd_bwd", grid=(ni,),
        in_specs=[row(d), row(d), _whole(wout.shape), row(e), row(cw), _whole(ws.shape), _whole(wst.shape), _whole(sgu.shape),
                  _whole(win.shape), row(d), _whole((1, d))],
        out_specs=[row(e), row(d), row(d), _whole((1, d)), _whole((1, cw)), _whole(ws.shape), _whole((heads, CHUNK))],
        out_shape=[jax.ShapeDtypeStruct((s, e), BF16), jax.ShapeDtypeStruct((s, d), F32), jax.ShapeDtypeStruct((s, d), BF16),
                   jax.ShapeDtypeStruct((1, d), F32), jax.ShapeDtypeStruct((1, cw), F32), jax.ShapeDtypeStruct(ws.shape, F32),
                   jax.ShapeDtypeStruct((heads, CHUNK), F32)],
        scratch_shapes=[pltpu.VMEM((tm, cw), F32), pltpu.VMEM((CHUNK, cw), F32)],
        compiler_params=_params("arbitrary"),
    )(dhb, dh, wout, pre, gate, ws, wst, sgu, win, h, gmix)


def _even_bwd(dhb, dh, wout, proj, cq, pooled, conva, wpool, wpoolt, pscale, win, x, gmix, tm, comm=None):
    s, d = x.shape
    e = win.shape[1]
    aw = e // 4
    ni = s // tm

    def body(dhb_ref, dh_ref, wout_ref, proj_ref, cq_ref, pooled_ref, ca_ref, wp_ref, wpt_ref, ps_ref, win_ref, x_ref, gmix_ref,
             dproj_ref, dx_ref, dgain_ref, dca_ref, dwp_ref, dps_ref, cbuf, ebuf):
        i = pl.program_id(0)
        first = i == 0

        @pl.when(first)
        def _():
            cbuf[tm:tm + HALO, :] = jnp.zeros((HALO, aw), F32)
            ebuf[tm:tm + HALO, :] = jnp.zeros((HALO, aw), F32)

        dmix = _nt(dhb_ref[...], wout_ref[...])
        dy_a, dy_b = dmix[:, :aw], dmix[:, aw:]
        proj = proj_ref[...].astype(F32)
        a_b, a_c, a_v = (proj[:, k * aw:(k + 1) * aw] for k in range(3))
        da_b = dy_a * cq_ref[...].astype(F32)
        dcq = dy_a * a_b
        cbuf[0:tm, :] = dcq
        cext = cbuf[...]
        s1 = _shift(cext, 1, False)[0:tm, :]
        s2 = _shift(cext, 2, False)[0:tm, :]
        q = a_c * a_v
        for k, shifted in enumerate((s2, s1, dcq)):
            _acc(dca_ref.at[k:k + 1, :], jnp.sum(shifted * q, axis=0, keepdims=True), first)
        dq = ca_ref[2:3, :] * dcq + ca_ref[1:2, :] * s1 + ca_ref[0:1, :] * s2
        da_c = dq * a_v
        da_v = dq * a_c
        dps, dpool = [], []
        for g, w in enumerate(WINDOWS):
            cols = slice(g * LANE, (g + 1) * LANE)
            pooled = pooled_ref[:, cols]
            mixed = _nn(pooled, wp_ref[g])
            dps.append(jnp.sum(dy_b[:, cols] * mixed, axis=0, keepdims=True))
            dmixed = (dy_b[:, cols] * ps_ref[:, cols]).astype(BF16)
            _acc(dwp_ref.at[g], _tn(pooled, dmixed), first)
            dp = _nn(dmixed, wpt_ref[g])
            dpool.append(dp)
            ebuf[0:tm, cols] = dp / _counts((ni - 1 - i) * tm, tm, w)
        _acc(dps_ref, jnp.concatenate(dps, axis=1), first)
        eext = ebuf[...]
        dzs = []
        for g, w in enumerate(WINDOWS):
            cols = slice(g * LANE, (g + 1) * LANE)
            dzs.append(_window_sum(eext[:, cols], w, False)[0:tm, :] - dpool[g])
        dproj = jnp.concatenate([da_b, da_c, da_v] + dzs, axis=1).astype(BF16)
        dproj_ref[...] = dproj
        dx, dgain = _rms_bwd(_nt(dproj, win_ref[...]), x_ref[...], gmix_ref[...])
        dx_ref[...] = dh_ref[...] + dx
        _acc(dgain_ref, dgain, first)
        cbuf[tm:tm + HALO, :] = cbuf[0:HALO, :]
        ebuf[tm:tm + HALO, :] = ebuf[0:HALO, :]

    row = lambda c: pl.BlockSpec((tm, c), lambda i: (ni - 1 - i, 0))
    return _pcall(
        body, comm=comm, name="even_bwd", grid=(ni,),
        in_specs=[row(d), row(d), _whole(wout.shape), row(e), row(aw), row(aw), _whole(conva.shape), _whole(wpool.shape),
                  _whole(wpoolt.shape), _whole(pscale.shape), _whole(win.shape), row(d), _whole((1, d))],
        out_specs=[row(e), row(d), _whole((1, d)), _whole(conva.shape), _whole(wpool.shape), _whole(pscale.shape)],
        out_shape=[jax.ShapeDtypeStruct((s, e), BF16), jax.ShapeDtypeStruct((s, d), F32), jax.ShapeDtypeStruct((1, d), F32),
                   jax.ShapeDtypeStruct(conva.shape, F32), jax.ShapeDtypeStruct(wpool.shape, F32),
                   jax.ShapeDtypeStruct(pscale.shape, F32)],
        scratch_shapes=[pltpu.VMEM((tm + HALO, aw), F32), pltpu.VMEM((tm + HALO, aw), F32)],
        compiler_params=_params("arbitrary"),
    )(dhb, dh, wout, proj, cq, pooled, conva, wpool, wpoolt, pscale, win, x, gmix)


def _wgrad(a, b, tk, ts, name, comm=None):
    s, ka = a.shape
    nb = b.shape[1]
    nt = s // ts

    def body(a_ref, b_ref, o_ref, acc):
        t = pl.program_id(1)
        _acc(acc, _tn(a_ref[...], b_ref[...]), t == 0)

        @pl.when(t == nt - 1)
        def _():
            o_ref[...] = acc[...].astype(BF16)

    return _pcall(
        body, comm=comm, name=name, grid=(ka // tk, nt),
        in_specs=[pl.BlockSpec((ts, tk), lambda k, t: (t, k)),
                  _resident((s, nb)) if nt == 1 else pl.BlockSpec((ts, nb), lambda k, t: (t, 0))],
        out_specs=pl.BlockSpec((tk, nb), lambda k, t: (k, 0)),
        out_shape=jax.ShapeDtypeStruct((ka, nb), BF16),
        scratch_shapes=[pltpu.VMEM((tk, nb), F32)],
        compiler_params=_params("arbitrary", "arbitrary"),
    )(a, b)


def _adamw(parts, w, m, v, tr, name, layer=None, into=None):
    r, c = w.shape[-2:]
    n, rp, cp = parts.shape
    assert rp >= r and r % tr == 0

    def body(p_ref, w_ref, m_ref, v_ref, *rest):
        g_ref, d_ref, mo_ref, vo_ref = rest[-4:]
        g = p_ref[0, 0:tr, 0:c].astype(F32)
        for j in range(1, n):
            g = g + p_ref[j, 0:tr, 0:c].astype(F32)
        g_ref[...] = g
        mn = ADAM_B1 * m_ref[...] + (1.0 - ADAM_B1) * g
        vn = ADAM_B2 * v_ref[...] + (1.0 - ADAM_B2) * (g * g)
        mo_ref[...] = mn
        vo_ref[...] = vn
        m_hat = mn / (1.0 - ADAM_B1 ** ADAM_STEP)
        v_hat = vn / (1.0 - ADAM_B2 ** ADAM_STEP)
        d_ref[...] = -ADAM_LR * (m_hat / (jnp.sqrt(v_hat) + ADAM_EPS) + ADAM_WD * w_ref[...])

    if layer is None:
        row = pl.BlockSpec((tr, c), lambda i: (i, 0))
    else:
        row = pl.BlockSpec((None, tr, c), lambda i: (layer, i, 0))
    out = jax.ShapeDtypeStruct(w.shape, F32)
    prev = [] if into is None else list(into)
    return _pcall(
        body, name=name, grid=(r // tr,),
        in_specs=[pl.BlockSpec((n, tr, cp), lambda i: (0, i, 0)), row, row, row] + [pl.BlockSpec(memory_space=pl.ANY)] * len(prev),
        out_specs=[row, row, row, row], out_shape=[out, out, out, out],
        input_output_aliases={4 + k: k for k in range(len(prev))},
        compiler_params=_params("arbitrary"),
    )(parts, w, m, v, *prev)


def _pair_add(grad, other, spec, core, name):
    axis, width = spec
    slot = other.shape[1:]

    def body(core_ref, g_ref, o_ref, out_ref):
        out_ref[...] = (g_ref[...].astype(F32) + o_ref[...].astype(F32)).astype(BF16)

    if axis == 0:
        gspec = pl.BlockSpec(slot, lambda q, core_ref: (2 * q + core_ref[0], 0))
    else:
        gspec = pl.BlockSpec(slot, lambda q, core_ref: (0, 2 * q + core_ref[0]))
    per_chip = pl.BlockSpec((None,) + slot, lambda q, core_ref: (q, 0, 0))
    return _pcall(
        body, name=name,
        grid_spec=pltpu.PrefetchScalarGridSpec(num_scalar_prefetch=1, grid=(N_CHIP,), in_specs=[gspec, per_chip], out_specs=per_chip),
        out_shape=jax.ShapeDtypeStruct(other.shape, BF16),
        compiler_params=_params("arbitrary"),
    )(core, grad, other)


def _hbm():
    return pl.BlockSpec(memory_space=pltpu.HBM)


def _window(ref, spec, j):
    axis, width = spec
    start = pl.multiple_of(j * width, width)
    return ref.at[(slice(None),) * axis + (pl.ds(start, width),)]


def _here():
    return lax.axis_index("x"), lax.axis_index("y"), lax.axis_index("c")


class _Gather:
    def __init__(self, shards, specs, fulls):
        n = len(shards)
        self.ins, self.specs, self.out_shape = list(shards), list(specs), list(fulls)
        self.sems = [pltpu.SemaphoreType.DMA((7 * n,)), pltpu.SemaphoreType.DMA((7 * n,)), pltpu.SemaphoreType.DMA((n,))]

    def _plan(self, ins, outs, sems):
        send_sems, recv_sems, local_sems = sems
        x, y, c = _here()
        me, sibling = (x, y, c), (x, y, 1 - c)
        chips = [(1 - x, y), (x, 1 - y), (1 - x, 1 - y)]

        def slot(t, dev):
            return _window(outs[t], self.specs[t], 4 * dev[0] + 2 * dev[1] + dev[2])

        def copy(t, k, block, to, src=None):
            return pltpu.make_async_remote_copy(
                src_ref=slot(t, block) if src is None else src, dst_ref=slot(t, block),
                send_sem=send_sems.at[7 * t + k], recv_sem=recv_sems.at[7 * t + k], device_id=to, device_id_type=MESH)

        plan = []
        for t in range(len(ins)):
            plan.append(dict(
                mine=pltpu.make_async_copy(ins[t], slot(t, me), local_sems.at[t]),
                first=[copy(t, 0, me, sibling, src=ins[t])] + [copy(t, 1 + j, me, (*q, c), src=ins[t]) for j, q in enumerate(chips)],
                over_ici=[copy(t, 1 + j, (*q, c), me) for j, q in enumerate(chips)],
                passed=[copy(t, 4 + j, (*q, c), sibling) for j, q in enumerate(chips)],
                from_sibling=[copy(t, 0, sibling, me)] + [copy(t, 4 + j, (*q, 1 - c), me) for j, q in enumerate(chips)]))
        return plan

    def start(self, ins, outs, sems):
        for p in self._plan(ins, outs, sems):
            p["mine"].start()
            for cp in p["first"]:
                cp.start()

    def middle(self, ins, outs, sems):
        for p in self._plan(ins, outs, sems):
            for arrived, onward in zip(p["over_ici"], p["passed"]):
                arrived.wait_recv()
                onward.start()

    def finish(self, ins, outs, sems):
        plan = self._plan(ins, outs, sems)
        for p in plan:
            for cp in p["from_sibling"]:
                cp.wait_recv()
        for p in plan:
            for cp in p["first"] + p["passed"]:
                cp.wait_send()
            p["mine"].wait()


class _PairExchange:
    def __init__(self, grads, specs):
        n = len(grads)
        self.ins, self.specs = list(grads), list(specs)
        self.out_shape = [jax.ShapeDtypeStruct((N_CHIP,) + a.shape[:sp[0]] + (sp[1],) + a.shape[sp[0] + 1:], a.dtype)
                          for a, sp in zip(grads, specs)]
        self.sems = [pltpu.SemaphoreType.DMA((n,)), pltpu.SemaphoreType.DMA((n,))]

    def start(self, ins, outs, sems):
        send_sems, recv_sems = sems
        x, y, c = _here()
        for t in range(len(ins)):
            for q in range(N_CHIP):
                pltpu.make_async_remote_copy(
                    src_ref=_window(ins[t], self.specs[t], 2 * q + (1 - c)), dst_ref=outs[t].at[q],
                    send_sem=send_sems.at[t], recv_sem=recv_sems.at[t], device_id=(x, y, 1 - c), device_id_type=MESH).start()

    def middle(self, ins, outs, sems):
        pass

    def finish(self, ins, outs, sems):
        send_sems, recv_sems = sems
        x, y, c = _here()
        for t in range(len(ins)):
            every = pltpu.make_async_remote_copy(src_ref=outs[t], dst_ref=outs[t], send_sem=send_sems.at[t],
                                                 recv_sem=recv_sems.at[t], device_id=(x, y, 1 - c), device_id_type=MESH)
            every.wait_send()
            every.wait_recv()


class _ChipExchange:
    def __init__(self, pairs, slotted=(), whole=()):
        self.ins = list(pairs) + list(slotted) + list(whole)
        self.npair, self.nslot = len(pairs), len(pairs) + len(slotted)
        n = len(self.ins)
        self.out_shape = ([jax.ShapeDtypeStruct(a.shape, a.dtype) for a in list(pairs) + list(slotted)]
                          + [jax.ShapeDtypeStruct((N_DEV,) + a.shape, a.dtype) for a in whole])
        self.sems = [pltpu.SemaphoreType.DMA((7 * n,)), pltpu.SemaphoreType.DMA((7 * n,)), pltpu.SemaphoreType.DMA((n,))]

    def _plan(self, ins, outs, sems):
        send_sems, recv_sems, local_sems = sems
        npair, nslot = self.npair, self.nslot
        x, y, c = _here()
        me, chip = 4 * x + 2 * y + c, 2 * x + y
        chips = [(1 - x, y), (x, 1 - y), (1 - x, 1 - y)]
        peers = [(x, y, 1 - c)] + [(*q, c) for q in chips] + [(*q, 1 - c) for q in chips]

        def index(dev):
            return 4 * dev[0] + 2 * dev[1] + dev[2]

        def copy(t, k, arriving):
            peer = peers[k]
            if t < npair:
                src, mine, theirs = ins[t].at[2 * peer[0] + peer[1]], chip, 2 * peer[0] + peer[1]
            else:
                src, mine, theirs = (ins[t].at[index(peer)] if t < nslot else ins[t]), me, index(peer)
            return pltpu.make_async_remote_copy(
                src_ref=src, dst_ref=outs[t].at[theirs if arriving else mine],
                send_sem=send_sems.at[7 * t + k], recv_sem=recv_sems.at[7 * t + k], device_id=peer, device_id_type=MESH)

        own, sent, arriving = [], [], []
        for t in range(len(ins)):
            fan = range(1, 4) if t < npair else range(7)
            if t < npair:
                own.append(pltpu.make_async_copy(ins[t].at[chip], outs[t].at[chip], local_sems.at[t]))
            else:
                own.append(pltpu.make_async_copy(ins[t].at[me] if t < nslot else ins[t], outs[t].at[me], local_sems.at[t]))
            sent += [copy(t, k, False) for k in fan]
            arriving += [copy(t, k, True) for k in fan]
        return own, sent, arriving

    def start(self, ins, outs, sems):
        own, sent, _ = self._plan(ins, outs, sems)
        for cp in own + sent:
            cp.start()

    def middle(self, ins, outs, sems):
        pass

    def finish(self, ins, outs, sems):
        own, sent, arriving = self._plan(ins, outs, sems)
        for cp in arriving:
            cp.wait_recv()
        for cp in sent:
            cp.wait_send()
        for cp in own:
            cp.wait()


class _Jobs:
    def __init__(self, *jobs):
        self.jobs = jobs
        self.ins = [a for j in jobs for a in j.ins]
        self.out_shape = [a for j in jobs for a in j.out_shape]
        self.sems = [a for j in jobs for a in j.sems]

    def _split(self, ins, outs, sems):
        i = o = s = 0
        for j in self.jobs:
            yield j, ins[i:i + len(j.ins)], outs[o:o + len(j.out_shape)], sems[s:s + len(j.sems)]
            i, o, s = i + len(j.ins), o + len(j.out_shape), s + len(j.sems)

    def start(self, ins, outs, sems):
        for j, a, b, c in self._split(ins, outs, sems):
            j.start(a, b, c)

    def middle(self, ins, outs, sems):
        for j, a, b, c in self._split(ins, outs, sems):
            j.middle(a, b, c)

    def finish(self, ins, outs, sems):
        for j, a, b, c in self._split(ins, outs, sems):
            j.finish(a, b, c)

    def results(self, outs):
        return [b for _, _, b, _ in self._split((), outs, ())]


def _alone(job, name):
    return _pcall(lambda: None, comm=job, name=name, in_specs=[], out_specs=[], out_shape=[])()[1]


SMALL_ROWS = 24
REP_COLS = 1024


def _pad_to(a, rows, cols):
    return jnp.pad(a, ((0, rows - a.shape[0]), (0, cols - a.shape[1])))


def _pack_small(conv_a, sgu_norm, conv_ffn, cols):
    return jnp.concatenate([_pad_to(conv_a, 8, cols), _pad_to(sgu_norm, 8, cols),
                            _pad_to(conv_ffn.reshape(-1, conv_ffn.shape[-1]), 8, cols)], axis=0)


def _unpack_small(p, ca_w, sg_w, cf_w):
    return p[0:3, 0:ca_w], p[8:9, 0:sg_w], p[16:22, 0:cf_w].reshape(2, 3, cf_w)


def _tile_rows(rows):
    return -(-rows // 8) * 8


def _pack_rows(parts):
    return jnp.concatenate([_pad_to(a, _tile_rows(a.shape[0]), REP_COLS) for a in parts], axis=0)


def _unpack_rows(p, shapes):
    out, r0 = [], 0
    for r, c in shapes:
        out.append(p[r0:r0 + r, 0:c])
        r0 += _tile_rows(r)
    return out


def _rep_late(norm_mix0, norm_ffn0, pool_scale, b_conv0, w_pool):
    return _pack_rows([norm_mix0, norm_ffn0, pool_scale.reshape(1, -1), _pad_to(b_conv0, 1, 3 * REP_COLS).reshape(3, REP_COLS),
                       w_pool.reshape(-1, REP_COLS)])


def _rep_early(norm_mix1, norm_ffn1, final_norm, b_spatial, b_conv1, w_spatial, loss):
    return _pack_rows([norm_mix1, norm_ffn1, final_norm.reshape(1, -1), b_spatial.reshape(1, -1),
                       _pad_to(b_conv1, 1, 3 * REP_COLS).reshape(3, REP_COLS), w_spatial.reshape(-1, REP_COLS), loss])


def _unpack_rep(late, early, like):
    f = like["b_conv_ffn"].shape[1]
    nm0, nf0, ps, bc0, wp = _unpack_rows(late, [(1, REP_COLS), (1, REP_COLS), (1, like["pool_scale"].shape[1]), (3, REP_COLS),
                                                (like["w_pool"].size // REP_COLS, REP_COLS)])
    nm1, nf1, fin, bs, bc1, wsp, _ = _unpack_rows(early, [(1, REP_COLS)] * 4 + [(3, REP_COLS), (like["w_spatial"].size // REP_COLS, REP_COLS),
                                                          (1, REP_COLS)])
    return {
        "norm_mix": jnp.concatenate([nm0, nm1]), "norm_ffn": jnp.concatenate([nf0, nf1]), "final_norm": fin[0], "pool_scale": ps,
        "b_spatial": bs.reshape(like["b_spatial"].shape),
        "b_conv_ffn": jnp.concatenate([bc0.reshape(1, -1), bc1.reshape(1, -1)])[:, 0:f],
        "w_pool": wp.reshape(like["w_pool"].shape), "w_spatial": wsp.reshape(like["w_spatial"].shape),
    }


def _pad_slots(a, width, padded):
    a = a.reshape(*a.shape[:-1], N_DEV, width)
    a = jnp.pad(a, ((0, 0),) * (a.ndim - 1) + ((0, padded - width),))
    return a.reshape(*a.shape[:-2], N_DEV * padded)


def _unpad_slots(a, width, padded):
    a = a.reshape(*a.shape[:-1], N_DEV, padded)[..., 0:width]
    return a.reshape(*a.shape[:-2], N_DEV * width)


def kernel(x, norm_mix, norm_ffn, final_norm, w_in_even, conv_a, w_pool, pool_scale, w_out_even, w_in_odd, sgu_norm, w_spatial, b_spatial, w_out_odd, w_ffn_gate, w_ffn_up, conv_ffn, b_conv_ffn, w_ffn_down, loss_target, m_norm_mix, m_norm_ffn, m_final_norm, m_w_in_even, m_conv_a, m_w_pool, m_pool_scale, m_w_out_even, m_w_in_odd, m_sgu_norm, m_w_spatial, m_b_spatial, m_w_out_odd, m_w_ffn_gate, m_w_ffn_up, m_conv_ffn, m_b_conv_ffn, m_w_ffn_down, v_norm_mix, v_norm_ffn, v_final_norm, v_w_in_even, v_conv_a, v_w_pool, v_pool_scale, v_w_out_even, v_w_in_odd, v_sgu_norm, v_w_spatial, v_b_spatial, v_w_out_odd, v_w_ffn_gate, v_w_ffn_up, v_conv_ffn, v_b_conv_ffn, v_w_ffn_down):
    s, d = x.shape[1], x.shape[2]
    x2, target = x[0], loss_target[0]
    tm = min(512, s)
    tm_wide = min(2048, s)
    tn = 256
    tk_wide = 1024
    tn_fwd = 512
    row = lambda a: a.reshape(1, -1)
    ein, ro = w_in_even.shape[2], w_out_even.shape[1]
    fs = w_ffn_gate.shape[2]
    fsp = -(-fs // LANE) * LANE
    fp = N_DEV * fsp
    full = lambda shape, dtype=BF16: jax.ShapeDtypeStruct(shape, dtype)

    wg_s = [jnp.pad(w_ffn_gate[l].T, ((0, fsp - fs), (0, 0))).astype(BF16) for l in range(2)]
    wu_s = [jnp.pad(w_ffn_up[l].T, ((0, fsp - fs), (0, 0))).astype(BF16) for l in range(2)]
    wd_s = [jnp.pad(w_ffn_down[l], ((0, fsp - fs), (0, 0))).astype(BF16) for l in range(2)]
    small_s = _pack_small(conv_a[0], sgu_norm, conv_ffn, fsp)[None]
    in_spec, out_spec, gu_spec, down_spec = (1, ein), (0, ro), (0, fsp), (0, fsp)
    full_in, full_out, full_gu, full_down = full((d, N_DEV * ein)), full((N_DEV * ro, d)), full((fp, d)), full((fp, d))
    win_e, wout_e, gsmall = _alone(_Gather([w_in_even[0].astype(BF16), w_out_even[0].astype(BF16), small_s], [in_spec, out_spec, (0, 1)],
                                           [full_in, full_out, full((N_DEV, SMALL_ROWS, fsp), F32)]), "gather_mix0")
    ca_full = jnp.moveaxis(gsmall[:, 0:3, 0:conv_a.shape[2]], 0, 1).reshape(3, -1)
    sgu_full = gsmall[:, 8, 0:sgu_norm.shape[1]].reshape(1, -1)
    cf_full = jnp.moveaxis(gsmall[:, 16:22, :].reshape(N_DEV, 2, 3, fsp), 0, 2).reshape(2, 3, fp)
    cb_full = _pad_slots(b_conv_ffn, fs, fsp)

    tril = jnp.tril(jnp.ones((CHUNK, CHUNK), F32))
    ws_m = w_spatial[0] * tril
    ws_b = ws_m.astype(BF16)
    wst_b = jnp.swapaxes(ws_m, 1, 2).astype(BF16)
    bfull = jnp.repeat(b_spatial[0].T, CHUNK, axis=1)
    wpool_b = w_pool[0].astype(BF16)
    wpoolt_b = jnp.swapaxes(w_pool[0], 1, 2).astype(BF16)

    (xn0, proj0, cq0, pooled0, mix0, h1, hn0), (wg0,) = _even_fwd(
        x2, norm_mix[0:1], norm_ffn[0:1], win_e, ca_full, wpool_b, pool_scale, wout_e, tm,
        comm=_Gather([wg_s[0]], [gu_spec], [full_gu]))
    (g0, gc0), (wu0,) = _ffn_gate(hn0, wg0, cf_full[0], cb_full[0:1], tm_wide, tn_fwd, "ffn_gate_l0", comm=_Gather([wu_s[0]], [gu_spec], [full_gu]))
    (up0, a0, sl0, ds0), (wd0,) = _ffn_up(hn0, wu0, gc0, tm_wide, tn_fwd, "ffn_up_l0", comm=_Gather([wd_s[0]], [down_spec], [full_down]))
    (h2, xn1), (win_o, wout_o) = _ffn_fwd2(a0, wd0, h1, norm_mix[1:2], tm, "ffn_fwd2_l0", comm=_Gather(
        [w_in_odd[0].astype(BF16), w_out_odd[0].astype(BF16)], [in_spec, out_spec], [full_in, full_out]))
    (pre1, gate1, mixo1, h3, hn1), (wg1,) = _odd_fwd(xn1, h2, win_o, sgu_full, ws_b, bfull, wout_o, norm_ffn[1:2], tm,
                                                    comm=_Gather([wg_s[1]], [gu_spec], [full_gu]))
    (g1, gc1), (wu1,) = _ffn_gate(hn1, wg1, cf_full[1], cb_full[1:2], tm_wide, tn_fwd, "ffn_gate_l1", comm=_Gather([wu_s[1]], [gu_spec], [full_gu]))
    (up1, a1, sl1, ds1), (wd1,) = _ffn_up(hn1, wu1, gc1, tm_wide, tn_fwd, "ffn_up_l1", comm=_Gather([wd_s[1]], [down_spec], [full_down]))
    h4, _ = _ffn_fwd2(a1, wd1, h3, row(final_norm), tm, "ffn_fwd2_l1")

    core = lax.axis_index("c").astype(jnp.int32).reshape(1)
    ts = min(1024, s)
    dh4, dh4b, d_final, lossvec = _loss_bwd(h4, row(final_norm), target, tm)
    gd1 = _wgrad(a1, dh4b, tk_wide, s, "wgrad_down_l1")
    (dg1, dup1, dcw1, dcb1), (o_d1,) = _ffn_bwd1(dh4b, g1, sl1, ds1, up1, wd1, cf_full[1], tm_wide, tn, "ffn_bwd1_l1",
                                                 comm=_PairExchange([gd1], [down_spec]))
    p_d1 = _pair_add(gd1, o_d1, down_spec, core, "pair_add_down_l1")
    gg1 = _wgrad(dg1, hn1, tk_wide, s, "wgrad_gate_l1")
    gu1, (o_g1,) = _wgrad(dup1, hn1, tk_wide, s, "wgrad_up_l1", comm=_PairExchange([gg1], [gu_spec]))
    p_g1 = _pair_add(gg1, o_g1, gu_spec, core, "pair_add_gate_l1")
    jobs = _Jobs(_ChipExchange([p_d1]), _PairExchange([gu1], [gu_spec]))
    (dh3, dh3b, d_nffn1), res = _ffn_bwd2(dg1, dup1, wg1, wu1, h3, norm_ffn[1:2], dh4, tm, "ffn_bwd2_l1", comm=jobs)
    (s_d1,), (o_u1,) = jobs.results(res)
    p_u1 = _pair_add(gu1, o_u1, gu_spec, core, "pair_add_up_l1")
    (dpre1, dh2, dh2b, d_nmix1, d_sgu, d_ws, d_b), (s_g1, s_u1) = _odd_bwd(
        dh3b, dh3, wout_o, pre1, gate1, ws_b, wst_b, sgu_full, win_o, h2, norm_mix[1:2], tm, comm=_ChipExchange([p_g1, p_u1]))
    gi1 = _wgrad(xn1, dpre1, tk_wide, ts, "wgrad_in_odd")
    go1, (o_i1,) = _wgrad(mixo1, dh3b, tk_wide, ts, "wgrad_out_odd", comm=_PairExchange([gi1], [in_spec]))
    p_i1 = _pair_add(gi1, o_i1, in_spec, core, "pair_add_in_odd")
    gd0, (o_o1,) = _wgrad(a0, dh2b, tk_wide, s, "wgrad_down_l0", comm=_PairExchange([go1], [out_spec]))
    p_o1 = _pair_add(go1, o_o1, out_spec, core, "pair_add_out_odd")
    d_early = _rep_early(d_nmix1, d_nffn1, d_final, d_b, _unpad_slots(dcb1, fs, fsp), d_ws * tril, lossvec)
    jobs = _Jobs(_ChipExchange([p_i1, p_o1], [], [d_early]), _PairExchange([gd0], [down_spec]))
    (dg0, dup0, dcw0, dcb0), res = _ffn_bwd1(dh2b, g0, sl0, ds0, up0, wd0, cf_full[0], tm_wide, tn, "ffn_bwd1_l0", comm=jobs)
    (s_i1, s_o1, r_early), (o_d0,) = jobs.results(res)
    p_d0 = _pair_add(gd0, o_d0, down_spec, core, "pair_add_down_l0")
    gg0, (s_d0,) = _wgrad(dg0, hn0, tk_wide, s, "wgrad_gate_l0", comm=_ChipExchange([p_d0]))
    gu0, (o_g0,) = _wgrad(dup0, hn0, tk_wide, s, "wgrad_up_l0", comm=_PairExchange([gg0], [gu_spec]))
    p_g0 = _pair_add(gg0, o_g0, gu_spec, core, "pair_add_gate_l0")
    jobs = _Jobs(_ChipExchange([p_g0]), _PairExchange([gu0], [gu_spec]))
    (dh1, dh1b, d_nffn0), res = _ffn_bwd2(dg0, dup0, wg0, wu0, h1, norm_ffn[0:1], dh2, tm, "ffn_bwd2_l0", comm=jobs)
    (s_g0,), (o_u0,) = jobs.results(res)
    p_u0 = _pair_add(gu0, o_u0, gu_spec, core, "pair_add_up_l0")
    go0 = _wgrad(mix0, dh1b, tk_wide, ts, "wgrad_out_even")
    jobs = _Jobs(_ChipExchange([p_u0]), _PairExchange([go0], [out_spec]))
    (dproj0, grad_x, d_nmix0, d_ca, d_wp, d_ps), res = _even_bwd(
        dh1b, dh1, wout_e, proj0, cq0, pooled0, ca_full, wpool_b, wpoolt_b, pool_scale, win_e, x2, norm_mix[0:1], tm, comm=jobs)
    (s_u0,), (o_o0,) = jobs.results(res)
    p_o0 = _pair_add(go0, o_o0, out_spec, core, "pair_add_out_even")
    d_small = jnp.stack([_pack_small(a, b, c, fsp) for a, b, c in zip(
        jnp.moveaxis(d_ca.reshape(3, N_DEV, -1), 1, 0), jnp.moveaxis(d_sgu.reshape(1, N_DEV, -1), 1, 0),
        jnp.moveaxis(jnp.stack([dcw0, dcw1]).reshape(2, 3, N_DEV, fsp), 2, 0))])
    d_late = _rep_late(d_nmix0, d_nffn0, d_ps, _unpad_slots(dcb0, fs, fsp), d_wp)
    gi0, (r_small, r_late) = _wgrad(xn0, dproj0, tk_wide, ts, "wgrad_in_even", comm=_ChipExchange([], [d_small], [d_late]))
    jobs = _Jobs(_PairExchange([gi0], [in_spec]), _ChipExchange([p_o0]))
    (o_i0,), (s_o0,) = jobs.results(_alone(jobs, "pair_exchange_in_even"))
    p_i0 = _pair_add(gi0, o_i0, in_spec, core, "pair_add_in_even")
    (s_i0,) = _alone(_ChipExchange([p_i0]), "chip_exchange_last")
    loss = jnp.sum(r_early[:, r_early.shape[1] - 8, :])

    out = {}
    out["w_in_even"] = _adamw(s_i0, w_in_even[0], m_w_in_even[0], v_w_in_even[0], 256, "adamw_in_even")
    out["w_out_even"] = _adamw(s_o0, w_out_even[0], m_w_out_even[0], v_w_out_even[0], ro // 2, "adamw_out_even")
    out["w_in_odd"] = _adamw(s_i1, w_in_odd[0], m_w_in_odd[0], v_w_in_odd[0], 256, "adamw_in_odd")
    out["w_out_odd"] = _adamw(s_o1, w_out_odd[0], m_w_out_odd[0], v_w_out_odd[0], ro // 2, "adamw_out_odd")
    tp = lambda a: jnp.swapaxes(a, 1, 2)
    for nm, s1, s0, w, m, v, back in (("w_ffn_gate", s_g1, s_g0, tp(w_ffn_gate), tp(m_w_ffn_gate), tp(v_w_ffn_gate), tp),
                                      ("w_ffn_up", s_u1, s_u0, tp(w_ffn_up), tp(m_w_ffn_up), tp(v_w_ffn_up), tp),
                                      ("w_ffn_down", s_d1, s_d0, w_ffn_down, m_w_ffn_down, v_w_ffn_down, lambda a: a)):
        l1 = _adamw(s1, w, m, v, fs // 2, "adamw_%s_l1" % nm, layer=1)
        out[nm] = [back(a) for a in _adamw(s0, w, m, v, fs // 2, "adamw_%s_l0" % nm, layer=0, into=l1)]
    small = _adamw(r_small, _pack_small(conv_a[0], sgu_norm, conv_ffn, fsp), _pack_small(m_conv_a[0], m_sgu_norm, m_conv_ffn, fsp),
                   _pack_small(v_conv_a[0], v_sgu_norm, v_conv_ffn, fsp), SMALL_ROWS, "adamw_small")
    no_loss = jnp.zeros((1, REP_COLS), F32)
    early = _adamw(r_early, *[_rep_early(nm[1:2], nf[1:2], fn, bs, bc[1:2], wsp, no_loss) for nm, nf, fn, bs, bc, wsp in (
        (norm_mix, norm_ffn, final_norm, b_spatial, b_conv_ffn, w_spatial), (m_norm_mix, m_norm_ffn, m_final_norm, m_b_spatial, m_b_conv_ffn, m_w_spatial),
        (v_norm_mix, v_norm_ffn, v_final_norm, v_b_spatial, v_b_conv_ffn, v_w_spatial))], r_early.shape[1], "adamw_replicated_early")
    late = _adamw(r_late, *[_rep_late(nm[0:1], nf[0:1], ps, bc[0:1], wp) for nm, nf, ps, bc, wp in (
        (norm_mix, norm_ffn, pool_scale, b_conv_ffn, w_pool), (m_norm_mix, m_norm_ffn, m_pool_scale, m_b_conv_ffn, m_w_pool),
        (v_norm_mix, v_norm_ffn, v_pool_scale, v_b_conv_ffn, v_w_pool))], r_late.shape[1], "adamw_replicated_late")

    names = ["norm_mix", "norm_ffn", "final_norm", "w_in_even", "conv_a", "w_pool", "pool_scale", "w_out_even", "w_in_odd", "sgu_norm",
             "w_spatial", "b_spatial", "w_out_odd", "w_ffn_gate", "w_ffn_up", "conv_ffn", "b_conv_ffn", "w_ffn_down"]
    like = {"norm_mix": norm_mix, "norm_ffn": norm_ffn, "final_norm": final_norm, "w_in_even": w_in_even, "conv_a": conv_a,
            "w_pool": w_pool, "pool_scale": pool_scale, "w_out_even": w_out_even, "w_in_odd": w_in_odd, "sgu_norm": sgu_norm,
            "w_spatial": w_spatial, "b_spatial": b_spatial, "w_out_odd": w_out_odd, "w_ffn_gate": w_ffn_gate, "w_ffn_up": w_ffn_up,
            "conv_ffn": conv_ffn, "b_conv_ffn": b_conv_ffn, "w_ffn_down": w_ffn_down}
    groups = []
    for k in range(4):
        ca_k, sg_k, cf_k = _unpack_small(small[k], conv_a.shape[2], sgu_norm.shape[1], conv_ffn.shape[2])
        vals = dict(_unpack_rep(late[k], early[k], like))
        vals.update(conv_a=ca_k, sgu_norm=sg_k, conv_ffn=cf_k)
        for nm in ("w_in_even", "w_in_odd", "w_out_even", "w_out_odd", "w_ffn_gate", "w_ffn_up", "w_ffn_down"):
            vals[nm] = out[nm][k]
        groups.append([vals[nm].reshape(like[nm].shape) for nm in names])
    return (loss, grad_x[None], *groups[0], *groups[1], *groups[2], *groups[3])
```

```python
import functools

import jax
import jax.numpy as jnp
from jax import lax
from jax.experimental import pallas as pl
from jax.experimental.pallas import tpu as pltpu

F32, BF16 = jnp.float32, jnp.bfloat16
EPS = 1e-6
WINDOWS = (2, 4, 8, 16)
HALO = 16
CHUNK = 128
N_DEV = 8
N_CHIP = 4
MESH = pl.DeviceIdType.MESH
VMEM_LIMIT = 56 * 2**20
LANE = 128
ADAM_LR, ADAM_B1, ADAM_B2, ADAM_EPS, ADAM_WD, ADAM_STEP = 0.001, 0.9, 0.999, 1e-08, 0.01, 10
SUB_ROWS = 256
LATE_NUM, LATE_DEN = 7, 8
INV_SQRT2 = 0.7071067811865476
INV_SQRT2PI = 0.3989422804014327


def _pcall(body, comm=None, **kw):
    if comm is None:
        return pl.pallas_call(body, **kw)
    in_specs, out_specs, out_shape = list(kw.pop("in_specs")), kw.pop("out_specs"), kw.pop("out_shape")
    single = not isinstance(out_shape, (list, tuple))
    out_specs, out_shape = ([out_specs], [out_shape]) if single else (list(out_specs), list(out_shape))
    scratch = list(kw.pop("scratch_shapes", []))
    grid = kw.get("grid", ())
    n_in, n_out, n_scr, c_in, c_out = len(in_specs), len(out_specs), len(scratch), len(comm.ins), len(comm.out_shape)

    def hosted(*refs):
        cuts = [0, n_in, n_in + c_in, n_in + c_in + n_out, n_in + c_in + n_out + c_out, n_in + c_in + n_out + c_out + n_scr, len(refs)]
        ins, cins, outs, couts, scr, sems = (refs[a:b] for a, b in zip(cuts[:-1], cuts[1:]))
        if grid:
            step, steps = 0, 1
            for axis, size in enumerate(grid):
                step, steps = step * size + pl.program_id(axis), steps * size
            pl.when(step == 0)(lambda: comm.start(cins, couts, sems))
            pl.when(step == (steps * LATE_NUM) // LATE_DEN)(lambda: comm.middle(cins, couts, sems))
            body(*ins, *outs, *scr)
            pl.when(step == steps - 1)(lambda: comm.finish(cins, couts, sems))
        else:
            comm.start(cins, couts, sems)
            comm.middle(cins, couts, sems)
            body(*ins, *outs, *scr)
            comm.finish(cins, couts, sems)

    call = pl.pallas_call(hosted, in_specs=in_specs + [_hbm()] * c_in, out_specs=out_specs + [_hbm()] * c_out,
                          out_shape=out_shape + list(comm.out_shape), scratch_shapes=scratch + list(comm.sems), **kw)

    def run(*args):
        res = call(*args, *comm.ins)
        own = res[0] if single else res[:n_out]
        return own, res[n_out:]

    return run


def _params(*sem):
    return pltpu.CompilerParams(dimension_semantics=sem, vmem_limit_bytes=VMEM_LIMIT)


def _whole(shape):
    return pl.BlockSpec(shape, lambda *_: (0,) * len(shape))


def _resident(shape):
    return pl.BlockSpec(shape, lambda *_: (0,) * len(shape), pipeline_mode=pl.Buffered(1))


def _rows(i, tm):
    return pl.ds(pl.multiple_of(i * tm, tm), tm)


def _nn(a, b):
    return jnp.dot(a, b, preferred_element_type=F32)


def _nt(a, b):
    return lax.dot_general(a, b, (((1,), (1,)), ((), ())), preferred_element_type=F32)


def _tn(a, b):
    return lax.dot_general(a, b, (((0,), (0,)), ((), ())), preferred_element_type=F32)


def _rms(x, gain):
    r = lax.rsqrt(jnp.mean(x * x, axis=-1, keepdims=True) + EPS)
    return x * r * gain


def _rms_bwd(dy, x, gain):
    r = lax.rsqrt(jnp.mean(x * x, axis=-1, keepdims=True) + EPS)
    xh = x * r
    dgain = jnp.sum(dy * xh, axis=0, keepdims=True)
    dxh = dy * gain
    dx = r * (dxh - xh * jnp.mean(dxh * xh, axis=-1, keepdims=True))
    return dx, dgain


def _gelu(x):
    return 0.5 * x * (1.0 + lax.erf(x * INV_SQRT2))


def _gelu_grad(x):
    return 0.5 * (1.0 + lax.erf(x * INV_SQRT2)) + x * jnp.exp(-0.5 * x * x) * INV_SQRT2PI


def _acc(ref, val, first):
    @pl.when(first)
    def _():
        ref[...] = val

    @pl.when(jnp.logical_not(first))
    def _():
        ref[...] += val


def _shift(ext, k, back):
    n = ext.shape[0]
    return pltpu.roll(ext, k if back else n - k, axis=0)


def _window_sum(ext, w, back):
    total, step = ext, 1
    while step < w:
        total = total + _shift(total, step, back)
        step *= 2
    return total


def _counts(row0, tm, w):
    pos1 = (row0 + lax.broadcasted_iota(jnp.int32, (tm, 1), 0) + 1).astype(F32)
    return jnp.minimum(pos1, float(w))


def _even_fwd(x, gmix, gffn, win, conva, wpool, pscale, wout, tm, comm=None):
    s, d = x.shape
    e = win.shape[1]
    aw = e // 4

    def body(x_ref, gmix_ref, gffn_ref, win_ref, ca_ref, wp_ref, ps_ref, wout_ref,
             xn_ref, proj_ref, cq_ref, pooled_ref, mix_ref, h_ref, hn_ref, qbuf, zbuf):
        i = pl.program_id(0)

        @pl.when(i == 0)
        def _():
            qbuf[0:HALO, :] = jnp.zeros((HALO, aw), F32)
            zbuf[0:HALO, :] = jnp.zeros((HALO, aw), F32)

        xv = x_ref[...]
        xn = _rms(xv, gmix_ref[...]).astype(BF16)
        xn_ref[...] = xn
        proj = _nn(xn, win_ref[...])
        proj_ref[...] = proj.astype(BF16)
        a_b, a_c, a_v, z = (proj[:, k * aw:(k + 1) * aw] for k in range(4))
        q = a_c * a_v
        qbuf[HALO:HALO + tm, :] = q
        qext = qbuf[...]
        cur = slice(HALO, HALO + tm)
        cq = ca_ref[2:3, :] * q + ca_ref[1:2, :] * _shift(qext, 1, True)[cur, :] + ca_ref[0:1, :] * _shift(qext, 2, True)[cur, :]
        cq_ref[...] = cq.astype(BF16)
        y_a = a_b * cq
        zbuf[HALO:HALO + tm, :] = z
        zext = zbuf[...]
        ys = []
        for g, w in enumerate(WINDOWS):
            cols = slice(g * LANE, (g + 1) * LANE)
            acc = _window_sum(zext[:, cols], w, True)[cur, :]
            pooled = (acc / _counts(i * tm, tm, w) - z[:, cols]).astype(BF16)
            pooled_ref[:, cols] = pooled
            ys.append(_nn(pooled, wp_ref[g]))
        y_b = jnp.concatenate(ys, axis=1) * ps_ref[...]
        mix = jnp.concatenate([y_a, y_b], axis=1).astype(BF16)
        mix_ref[...] = mix
        h = xv + _nn(mix, wout_ref[...])
        h_ref[...] = h
        hn_ref[...] = _rms(h, gffn_ref[...]).astype(BF16)
        qbuf[0:HALO, :] = qbuf[tm:tm + HALO, :]
        zbuf[0:HALO, :] = zbuf[tm:tm + HALO, :]

    row = lambda c: pl.BlockSpec((tm, c), lambda i: (i, 0))
    return _pcall(
        body, comm=comm, name="even_fwd", grid=(s // tm,),
        in_specs=[row(d), _whole((1, d)), _whole((1, d)), _whole(win.shape), _whole(conva.shape), _whole(wpool.shape),
                  _whole(pscale.shape), _whole(wout.shape)],
        out_specs=[row(d), row(e), row(aw), row(aw), row(d), row(d), row(d)],
        out_shape=[jax.ShapeDtypeStruct((s, d), BF16), jax.ShapeDtypeStruct((s, e), BF16), jax.ShapeDtypeStruct((s, aw), BF16),
                   jax.ShapeDtypeStruct((s, aw), BF16), jax.ShapeDtypeStruct((s, d), BF16), jax.ShapeDtypeStruct((s, d), F32),
                   jax.ShapeDtypeStruct((s, d), BF16)],
        scratch_shapes=[pltpu.VMEM((tm + HALO, aw), F32), pltpu.VMEM((tm + HALO, aw), F32)],
        compiler_params=_params("arbitrary"),
    )(x, gmix, gffn, win, conva, wpool, pscale, wout)


def _ffn_gate(hn, wgt, cw, cb, tm, tn, name, comm=None):
    s, d = hn.shape
    f = wgt.shape[0]
    sub = min(SUB_ROWS, tm)

    def body(hn_ref, wg_ref, cw_ref, cb_ref, g_ref, gc_ref, gbuf):
        i = pl.program_id(1)

        @pl.when(i == 0)
        def _():
            gbuf[0:HALO, :] = jnp.zeros((HALO, tn), F32)

        wg = wg_ref[...]
        for c in range(tm // sub):
            rows = pl.ds(c * sub, sub)
            g = _nt(hn_ref[pl.ds(pl.multiple_of(i * tm + c * sub, sub), sub), :], wg)
            g_ref[rows, :] = g.astype(BF16)
            gbuf[pl.ds(HALO + c * sub, sub), :] = g
            ext = gbuf[pl.ds(c * sub, sub + HALO), :]
            gc = (cw_ref[2:3, :] * g + cw_ref[1:2, :] * _shift(ext, 1, True)[HALO:, :]
                  + cw_ref[0:1, :] * _shift(ext, 2, True)[HALO:, :] + cb_ref[...])
            gc_ref[rows, :] = gc.astype(BF16)
        gbuf[0:HALO, :] = gbuf[tm:tm + HALO, :]

    tile = pl.BlockSpec((tm, tn), lambda j, i: (i, j))
    wcol = lambda r: pl.BlockSpec((r, tn), lambda j, i: (0, j))
    out = jax.ShapeDtypeStruct((s, f), BF16)
    return _pcall(
        body, comm=comm, name=name, grid=(f // tn, s // tm),
        in_specs=[_resident((s, d)), pl.BlockSpec((tn, d), lambda j, i: (j, 0)), wcol(3), wcol(1)],
        out_specs=[tile, tile], out_shape=[out, out],
        scratch_shapes=[pltpu.VMEM((tm + HALO, tn), F32)],
        compiler_params=_params("arbitrary", "arbitrary"),
    )(hn, wgt, cw, cb)


def _ffn_up(hn, wut, gc, tm, tn, name, comm=None):
    s, d = hn.shape
    f = wut.shape[0]
    sub = min(SUB_ROWS, tm)

    def body(hn_ref, wu_ref, gc_ref, up_ref, a_ref, silu_ref, dsilu_ref):
        wu = wu_ref[...]
        for c in range(tm // sub):
            rows = pl.ds(c * sub, sub)
            up = _nt(hn_ref[pl.ds(pl.multiple_of(pl.program_id(1) * tm + c * sub, sub), sub), :], wu)
            up_ref[rows, :] = up.astype(BF16)
            gc = gc_ref[rows, :].astype(F32)
            sg = jax.nn.sigmoid(gc)
            silu = gc * sg
            silu_ref[rows, :] = silu.astype(BF16)
            dsilu_ref[rows, :] = (sg * (1.0 - silu) + silu).astype(BF16)
            a_ref[rows, :] = (silu * up).astype(BF16)

    tile = pl.BlockSpec((tm, tn), lambda j, i: (i, j))
    out = jax.ShapeDtypeStruct((s, f), BF16)
    return _pcall(
        body, comm=comm, name=name, grid=(f // tn, s // tm),
        in_specs=[_resident((s, d)), pl.BlockSpec((tn, d), lambda j, i: (j, 0)), tile],
        out_specs=[tile, tile, tile, tile], out_shape=[out, out, out, out],
        compiler_params=_params("arbitrary", "arbitrary"),
    )(hn, wut, gc)


def _ffn_fwd2(a, wd, h, gain, tm, name, comm=None):
    s, d = h.shape
    f = a.shape[1]

    def body(a_ref, wd_ref, h_ref, gain_ref, ho_ref, hn_ref):
        wd_v, gain = wd_ref[...], gain_ref[...]
        sub = min(SUB_ROWS, tm)
        for c in range(tm // sub):
            rows = pl.ds(c * sub, sub)
            ho = h_ref[rows, :] + _nn(a_ref[rows, :], wd_v)
            ho_ref[rows, :] = ho
            hn_ref[rows, :] = _rms(ho, gain).astype(BF16)

    row = lambda c: pl.BlockSpec((tm, c), lambda i: (i, 0))
    return _pcall(
        body, comm=comm, name=name, grid=(s // tm,),
        in_specs=[row(f), _resident(wd.shape), row(d), _whole((1, d))],
        out_specs=[row(d), row(d)],
        out_shape=[jax.ShapeDtypeStruct((s, d), F32), jax.ShapeDtypeStruct((s, d), BF16)],
        compiler_params=_params("arbitrary"),
    )(a, wd, h, gain)


def _odd_fwd(xn, h, win, sgu, ws, bfull, wout, gffn, tm, comm=None):
    s, d = h.shape
    e = win.shape[1]
    cw = e // 2
    heads = ws.shape[0]

    def body(xn_ref, h_ref, win_ref, sgu_ref, ws_ref, b_ref, wout_ref, gffn_ref,
             pre_ref, gate_ref, mixo_ref, ho_ref, hn_ref, gbuf):
        pre = _nn(xn_ref[...], win_ref[...])
        pre_ref[...] = pre.astype(BF16)
        p = _gelu(pre)
        u, v = p[:, :cw], p[:, cw:]
        vn = _rms(v, sgu_ref[...]).astype(BF16)
        for n in range(tm // CHUNK):
            rows = slice(n * CHUNK, (n + 1) * CHUNK)
            for hd in range(heads):
                cols = slice(hd * CHUNK, (hd + 1) * CHUNK)
                gbuf[rows, cols] = _nn(ws_ref[hd], vn[rows, cols]) + b_ref[:, cols]
        gate = gbuf[...]
        gate_ref[...] = gate.astype(BF16)
        mixo = (u * gate).astype(BF16)
        mixo_ref[...] = mixo
        ho = h_ref[...] + _nn(mixo, wout_ref[...])
        ho_ref[...] = ho
        hn_ref[...] = _rms(ho, gffn_ref[...]).astype(BF16)

    row = lambda c: pl.BlockSpec((tm, c), lambda i: (i, 0))
    return _pcall(
        body, comm=comm, name="odd_fwd", grid=(s // tm,),
        in_specs=[row(d), row(d), _whole(win.shape), _whole(sgu.shape), _whole(ws.shape), _whole(bfull.shape),
                  _whole(wout.shape), _whole((1, d))],
        out_specs=[row(e), row(cw), row(cw), row(d), row(d)],
        out_shape=[jax.ShapeDtypeStruct((s, e), BF16), jax.ShapeDtypeStruct((s, cw), BF16), jax.ShapeDtypeStruct((s, cw), BF16),
                   jax.ShapeDtypeStruct((s, d), F32), jax.ShapeDtypeStruct((s, d), BF16)],
        scratch_shapes=[pltpu.VMEM((tm, cw), F32)],
        compiler_params=_params("arbitrary"),
    )(xn, h, win, sgu, ws, bfull, wout, gffn)


def _loss_bwd(h, gain, target, tm):
    s, d = h.shape

    def body(h_ref, gain_ref, t_ref, dh_ref, dhb_ref, dgain_ref, loss_ref):
        i = pl.program_id(0)
        hv = h_ref[...]
        gain = gain_ref[...]
        err = _rms(hv, gain) - t_ref[...]
        dy = err * (1.0 / d)
        dx, dgain = _rms_bwd(dy, hv, gain)
        dh_ref[...] = dx
        dhb_ref[...] = dx.astype(BF16)
        _acc(dgain_ref, dgain, i == 0)
        _acc(loss_ref, jnp.sum(err * err, axis=0, keepdims=True) * (0.5 / d), i == 0)

    row = pl.BlockSpec((tm, d), lambda i: (i, 0))
    return _pcall(
        body, name="loss_bwd", grid=(s // tm,),
        in_specs=[row, _whole((1, d)), row],
        out_specs=[row, row, _whole((1, d)), _whole((1, d))],
        out_shape=[jax.ShapeDtypeStruct((s, d), F32), jax.ShapeDtypeStruct((s, d), BF16), jax.ShapeDtypeStruct((1, d), F32),
                   jax.ShapeDtypeStruct((1, d), F32)],
        compiler_params=_params("arbitrary"),
    )(h, gain, target)


def _ffn_bwd1(dhb, g, silu, dsilu, up, wd, cw, tm, tn, name, comm=None):
    s, d = dhb.shape
    f = g.shape[1]
    ni = s // tm

    def body(dh_ref, g_ref, silu_ref, dsilu_ref, up_ref, wd_ref, cw_ref, dg_ref, dup_ref, dcw_ref, dcb_ref, ebuf):
        i = pl.program_id(1)

        @pl.when(i == 0)
        def _():
            ebuf[tm:tm + HALO, :] = jnp.zeros((HALO, tn), F32)

        wd_v = wd_ref[...]
        sub = min(SUB_ROWS, tm)
        sums = [None] * 4
        for c in reversed(range(tm // sub)):
            rows = pl.ds(c * sub, sub)
            da = _nt(dh_ref[pl.ds(pl.multiple_of((ni - 1 - i) * tm + c * sub, sub), sub), :], wd_v)
            dup_ref[rows, :] = (da * silu_ref[rows, :].astype(F32)).astype(BF16)
            dgc = da * up_ref[rows, :].astype(F32) * dsilu_ref[rows, :].astype(F32)
            ebuf[rows, :] = dgc
            ext = ebuf[pl.ds(c * sub, sub + HALO), :]
            s1 = _shift(ext, 1, False)[0:sub, :]
            s2 = _shift(ext, 2, False)[0:sub, :]
            dg_ref[rows, :] = (cw_ref[2:3, :] * dgc + cw_ref[1:2, :] * s1 + cw_ref[0:1, :] * s2).astype(BF16)
            gv = g_ref[rows, :].astype(F32)
            for k, term in enumerate((s2 * gv, s1 * gv, dgc * gv, dgc)):
                part = jnp.sum(term, axis=0, keepdims=True)
                sums[k] = part if sums[k] is None else sums[k] + part
        for k in range(3):
            _acc(dcw_ref.at[k:k + 1, :], sums[k], i == 0)
        _acc(dcb_ref, sums[3], i == 0)
        ebuf[tm:tm + HALO, :] = ebuf[0:HALO, :]

    tile = pl.BlockSpec((tm, tn), lambda j, i: (ni - 1 - i, j))
    wcol = lambda r: pl.BlockSpec((r, tn), lambda j, i: (0, j))
    out = jax.ShapeDtypeStruct((s, f), BF16)
    return _pcall(
        body, comm=comm, name=name, grid=(f // tn, ni),
        in_specs=[_resident((s, d)), tile, tile, tile, tile,
                  pl.BlockSpec((tn, d), lambda j, i: (j, 0)), wcol(3)],
        out_specs=[tile, tile, wcol(3), wcol(1)],
        out_shape=[out, out, jax.ShapeDtypeStruct((3, f), F32), jax.ShapeDtypeStruct((1, f), F32)],
        scratch_shapes=[pltpu.VMEM((tm + HALO, tn), F32)],
        compiler_params=_params("arbitrary", "arbitrary"),
    )(dhb, g, silu, dsilu, up, wd, cw)


def _ffn_bwd2(dg, dup, wg, wu, h, gain, dh, tm, name, comm=None):
    s, d = h.shape
    f = dg.shape[1]

    def body(dg_ref, dup_ref, wg_ref, wu_ref, h_ref, gain_ref, dh_ref, dho_ref, dhb_ref, dgain_ref):
        wg_v, wu_v, gain = wg_ref[...], wu_ref[...], gain_ref[...]
        sub = min(SUB_ROWS, tm)
        dgain = None
        for c in range(tm // sub):
            rows = pl.ds(c * sub, sub)
            dhn = _nn(dg_ref[rows, :], wg_v) + _nn(dup_ref[rows, :], wu_v)
            dx, part = _rms_bwd(dhn, h_ref[rows, :], gain)
            dgain = part if dgain is None else dgain + part
            dho = dh_ref[rows, :] + dx
            dho_ref[rows, :] = dho
            dhb_ref[rows, :] = dho.astype(BF16)
        _acc(dgain_ref, dgain, pl.program_id(0) == 0)

    row = lambda c: pl.BlockSpec((tm, c), lambda i: (i, 0))
    return _pcall(
        body, comm=comm, name=name, grid=(s // tm,),
        in_specs=[row(f), row(f), _resident(wg.shape), _resident(wu.shape), row(d), _whole((1, d)), row(d)],
        out_specs=[row(d), row(d), _whole((1, d))],
        out_shape=[jax.ShapeDtypeStruct((s, d), F32), jax.ShapeDtypeStruct((s, d), BF16), jax.ShapeDtypeStruct((1, d), F32)],
        compiler_params=_params("arbitrary"),
    )(dg, dup, wg, wu, h, gain, dh)


def _odd_bwd(dhb, dh, wout, pre, gate, ws, wst, sgu, win, h, gmix, tm, comm=None):
    s, d = h.shape
    e = win.shape[1]
    cw = e // 2
    heads = ws.shape[0]
    ni = s // tm

    def body(dhb_ref, dh_ref, wout_ref, pre_ref, gate_ref, ws_ref, wst_ref, sgu_ref, win_ref, h_ref, gmix_ref,
             dpre_ref, dho_ref, dhob_ref, dgain_ref, dsgu_ref, dws_ref, db_ref, vbuf, gacc):
        i = pl.program_id(0)
        first = i == 0
        dmixo = _nt(dhb_ref[...], wout_ref[...])
        pre = pre_ref[...].astype(F32)
        p = _gelu(pre)
        u, v = p[:, :cw], p[:, cw:]
        sgu = sgu_ref[...]
        rv = lax.rsqrt(jnp.mean(v * v, axis=-1, keepdims=True) + EPS)
        vh = v * rv
        vn = (vh * sgu).astype(BF16)
        du = dmixo * gate_ref[...].astype(F32)
        dgate = dmixo * u
        dgate_b = dgate.astype(BF16)
        gsum = dgate[0:CHUNK, :]
        for n in range(1, tm // CHUNK):
            gsum = gsum + dgate[n * CHUNK:(n + 1) * CHUNK, :]
        _acc(gacc, gsum, first)
        for hd in range(heads):
            cols = slice(hd * CHUNK, (hd + 1) * CHUNK)
            dws = None
            for n in range(tm // CHUNK):
                rows = slice(n * CHUNK, (n + 1) * CHUNK)
                vbuf[rows, cols] = _nn(wst_ref[hd], dgate_b[rows, cols])
                part = _nt(dgate_b[rows, cols], vn[rows, cols])
                dws = part if dws is None else dws + part
            _acc(dws_ref.at[hd], dws, first)
        dvn = vbuf[...]
        _acc(dsgu_ref, jnp.sum(dvn * vh, axis=0, keepdims=True), first)
        dvh = dvn * sgu
        dv = rv * (dvh - vh * jnp.mean(dvh * vh, axis=-1, keepdims=True))
        dpre = (jnp.concatenate([du, dv], axis=1) * _gelu_grad(pre)).astype(BF16)
        dpre_ref[...] = dpre
        dx, dgain = _rms_bwd(_nt(dpre, win_ref[...]), h_ref[...], gmix_ref[...])
        dho = dh_ref[...] + dx
        dho_ref[...] = dho
        dhob_ref[...] = dho.astype(BF16)
        _acc(dgain_ref, dgain, first)

        @pl.when(i == ni - 1)
        def _():
            ones = jnp.ones((8, CHUNK), F32)
            for hd in range(heads):
                tot = lax.dot_general(ones, gacc[:, hd * CHUNK:(hd + 1) * CHUNK], (((1,), (1,)), ((), ())),
                                      preferred_element_type=F32, precision=lax.Precision.HIGHEST)
                db_ref[hd:hd + 1, :] = tot[0:1, :]

    row = lambda c: pl.BlockSpec((tm, c), lambda i: (i, 0))
    return _pcall(
        body, comm=comm, name="odd_bwd", grid=(ni,),
        in_specs=[row(d), row(d), _whole(wout.shape), row(e), row(cw), _whole(ws.shape), _whole(wst.shape), _whole(sgu.shape),
                  _whole(win.shape), row(d), _whole((1, d))],
        out_specs=[row(e), row(d), row(d), _whole((1, d)), _whole((1, cw)), _whole(ws.shape), _whole((heads, CHUNK))],
        out_shape=[jax.ShapeDtypeStruct((s, e), BF16), jax.ShapeDtypeStruct((s, d), F32), jax.ShapeDtypeStruct((s, d), BF16),
                   jax.ShapeDtypeStruct((1, d), F32), jax.ShapeDtypeStruct((1, cw), F32), jax.ShapeDtypeStruct(ws.shape, F32),
                   jax.ShapeDtypeStruct((heads, CHUNK), F32)],
        scratch_shapes=[pltpu.VMEM((tm, cw), F32), pltpu.VMEM((CHUNK, cw), F32)],
        compiler_params=_params("arbitrary"),
    )(dhb, dh, wout, pre, gate, ws, wst, sgu, win, h, gmix)


def _even_bwd(dhb, dh, wout, proj, cq, pooled, conva, wpool, wpoolt, pscale, win, x, gmix, tm, comm=None):
    s, d = x.shape
    e = win.shape[1]
    aw = e // 4
    ni = s // tm

    def body(dhb_ref, dh_ref, wout_ref, proj_ref, cq_ref, pooled_ref, ca_ref, wp_ref, wpt_ref, ps_ref, win_ref, x_ref, gmix_ref,
             dproj_ref, dx_ref, dgain_ref, dca_ref, dwp_ref, dps_ref, cbuf, ebuf):
        i = pl.program_id(0)
        first = i == 0

        @pl.when(first)
        def _():
            cbuf[tm:tm + HALO, :] = jnp.zeros((HALO, aw), F32)
            ebuf[tm:tm + HALO, :] = jnp.zeros((HALO, aw), F32)

        dmix = _nt(dhb_ref[...], wout_ref[...])
        dy_a, dy_b = dmix[:, :aw], dmix[:, aw:]
        proj = proj_ref[...].astype(F32)
        a_b, a_c, a_v = (proj[:, k * aw:(k + 1) * aw] for k in range(3))
        da_b = dy_a * cq_ref[...].astype(F32)
        dcq = dy_a * a_b
        cbuf[0:tm, :] = dcq
        cext = cbuf[...]
        s1 = _shift(cext, 1, False)[0:tm, :]
        s2 = _shift(cext, 2, False)[0:tm, :]
        q = a_c * a_v
        for k, shifted in enumerate((s2, s1, dcq)):
            _acc(dca_ref.at[k:k + 1, :], jnp.sum(shifted * q, axis=0, keepdims=True), first)
        dq = ca_ref[2:3, :] * dcq + ca_ref[1:2, :] * s1 + ca_ref[0:1, :] * s2
        da_c = dq * a_v
        da_v = dq * a_c
        dps, dpool = [], []
        for g, w in enumerate(WINDOWS):
            cols = slice(g * LANE, (g + 1) * LANE)
            pooled = pooled_ref[:, cols]
            mixed = _nn(pooled, wp_ref[g])
            dps.append(jnp.sum(dy_b[:, cols] * mixed, axis=0, keepdims=True))
            dmixed = (dy_b[:, cols] * ps_ref[:, cols]).astype(BF16)
            _acc(dwp_ref.at[g], _tn(pooled, dmixed), first)
            dp = _nn(dmixed, wpt_ref[g])
            dpool.append(dp)
            ebuf[0:tm, cols] = dp / _counts((ni - 1 - i) * tm, tm, w)
        _acc(dps_ref, jnp.concatenate(dps, axis=1), first)
        eext = ebuf[...]
        dzs = []
        for g, w in enumerate(WINDOWS):
            cols = slice(g * LANE, (g + 1) * LANE)
            dzs.append(_window_sum(eext[:, cols], w, False)[0:tm, :] - dpool[g])
        dproj = jnp.concatenate([da_b, da_c, da_v] + dzs, axis=1).astype(BF16)
        dproj_ref[...] = dproj
        dx, dgain = _rms_bwd(_nt(dproj, win_ref[...]), x_ref[...], gmix_ref[...])
        dx_ref[...] = dh_ref[...] + dx
        _acc(dgain_ref, dgain, first)
        cbuf[tm:tm + HALO, :] = cbuf[0:HALO, :]
        ebuf[tm:tm + HALO, :] = ebuf[0:HALO, :]

    row = lambda c: pl.BlockSpec((tm, c), lambda i: (ni - 1 - i, 0))
    return _pcall(
        body, comm=comm, name="even_bwd", grid=(ni,),
        in_specs=[row(d), row(d), _whole(wout.shape), row(e), row(aw), row(aw), _whole(conva.shape), _whole(wpool.shape),
                  _whole(wpoolt.shape), _whole(pscale.shape), _whole(win.shape), row(d), _whole((1, d))],
        out_specs=[row(e), row(d), _whole((1, d)), _whole(conva.shape), _whole(wpool.shape), _whole(pscale.shape)],
        out_shape=[jax.ShapeDtypeStruct((s, e), BF16), jax.ShapeDtypeStruct((s, d), F32), jax.ShapeDtypeStruct((1, d), F32),
                   jax.ShapeDtypeStruct(conva.shape, F32), jax.ShapeDtypeStruct(wpool.shape, F32),
                   jax.ShapeDtypeStruct(pscale.shape, F32)],
        scratch_shapes=[pltpu.VMEM((tm + HALO, aw), F32), pltpu.VMEM((tm + HALO, aw), F32)],
        compiler_params=_params("arbitrary"),
    )(dhb, dh, wout, proj, cq, pooled, conva, wpool, wpoolt, pscale, win, x, gmix)


def _wgrad(a, b, tk, ts, name, comm=None):
    s, ka = a.shape
    nb = b.shape[1]
    nt = s // ts

    def body(a_ref, b_ref, o_ref, acc):
        t = pl.program_id(1)
        _acc(acc, _tn(a_ref[...], b_ref[...]), t == 0)

        @pl.when(t == nt - 1)
        def _():
            o_ref[...] = acc[...].astype(BF16)

    return _pcall(
        body, comm=comm, name=name, grid=(ka // tk, nt),
        in_specs=[pl.BlockSpec((ts, tk), lambda k, t: (t, k)),
                  _resident((s, nb)) if nt == 1 else pl.BlockSpec((ts, nb), lambda k, t: (t, 0))],
        out_specs=pl.BlockSpec((tk, nb), lambda k, t: (k, 0)),
        out_shape=jax.ShapeDtypeStruct((ka, nb), BF16),
        scratch_shapes=[pltpu.VMEM((tk, nb), F32)],
        compiler_params=_params("arbitrary", "arbitrary"),
    )(a, b)


def _adamw(parts, w, m, v, tr, name, layer=None, into=None):
    r, c = w.shape[-2:]
    n, rp, cp = parts.shape
    assert rp >= r and r % tr == 0

    def body(p_ref, w_ref, m_ref, v_ref, *rest):
        g_ref, d_ref, mo_ref, vo_ref = rest[-4:]
        g = p_ref[0, 0:tr, 0:c].astype(F32)
        for j in range(1, n):
            g = g + p_ref[j, 0:tr, 0:c].astype(F32)
        g_ref[...] = g
        mn = ADAM_B1 * m_ref[...] + (1.0 - ADAM_B1) * g
        vn = ADAM_B2 * v_ref[...] + (1.0 - ADAM_B2) * (g * g)
        mo_ref[...] = mn
        vo_ref[...] = vn
        m_hat = mn / (1.0 - ADAM_B1 ** ADAM_STEP)
        v_hat = vn / (1.0 - ADAM_B2 ** ADAM_STEP)
        d_ref[...] = -ADAM_LR * (m_hat / (jnp.sqrt(v_hat) + ADAM_EPS) + ADAM_WD * w_ref[...])

    if layer is None:
        row = pl.BlockSpec((tr, c), lambda i: (i, 0))
    else:
        row = pl.BlockSpec((None, tr, c), lambda i: (layer, i, 0))
    out = jax.ShapeDtypeStruct(w.shape, F32)
    prev = [] if into is None else list(into)
    return _pcall(
        body, name=name, grid=(r // tr,),
        in_specs=[pl.BlockSpec((n, tr, cp), lambda i: (0, i, 0)), row, row, row] + [pl.BlockSpec(memory_space=pl.ANY)] * len(prev),
        out_specs=[row, row, row, row], out_shape=[out, out, out, out],
        input_output_aliases={4 + k: k for k in range(len(prev))},
        compiler_params=_params("arbitrary"),
    )(parts, w, m, v, *prev)


def _pair_add(grad, other, spec, core, name):
    axis, width = spec
    slot = other.shape[1:]

    def body(core_ref, g_ref, o_ref, out_ref):
        out_ref[...] = (g_ref[...].astype(F32) + o_ref[...].astype(F32)).astype(BF16)

    if axis == 0:
        gspec = pl.BlockSpec(slot, lambda q, core_ref: (2 * q + core_ref[0], 0))
    else:
        gspec = pl.BlockSpec(slot, lambda q, core_ref: (0, 2 * q + core_ref[0]))
    per_chip = pl.BlockSpec((None,) + slot, lambda q, core_ref: (q, 0, 0))
    return _pcall(
        body, name=name,
        grid_spec=pltpu.PrefetchScalarGridSpec(num_scalar_prefetch=1, grid=(N_CHIP,), in_specs=[gspec, per_chip], out_specs=per_chip),
        out_shape=jax.ShapeDtypeStruct(other.shape, BF16),
        compiler_params=_params("arbitrary"),
    )(core, grad, other)


def _hbm():
    return pl.BlockSpec(memory_space=pltpu.HBM)


def _window(ref, spec, j):
    axis, width = spec
    start = pl.multiple_of(j * width, width)
    return ref.at[(slice(None),) * axis + (pl.ds(start, width),)]


def _here():
    return lax.axis_index("x"), lax.axis_index("y"), lax.axis_index("c")


class _Gather:
    def __init__(self, shards, specs, fulls):
        n = len(shards)
        self.ins, self.specs, self.out_shape = list(shards), list(specs), list(fulls)
        self.sems = [pltpu.SemaphoreType.DMA((7 * n,)), pltpu.SemaphoreType.DMA((7 * n,)), pltpu.SemaphoreType.DMA((n,))]

    def _plan(self, ins, outs, sems):
        send_sems, recv_sems, local_sems = sems
        x, y, c = _here()
        me, sibling = (x, y, c), (x, y, 1 - c)
        chips = [(1 - x, y), (x, 1 - y), (1 - x, 1 - y)]

        def slot(t, dev):
            return _window(outs[t], self.specs[t], 4 * dev[0] + 2 * dev[1] + dev[2])

        def copy(t, k, block, to, src=None):
            return pltpu.make_async_remote_copy(
                src_ref=slot(t, block) if src is None else src, dst_ref=slot(t, block),
                send_sem=send_sems.at[7 * t + k], recv_sem=recv_sems.at[7 * t + k], device_id=to, device_id_type=MESH)

        plan = []
        for t in range(len(ins)):
            plan.append(dict(
                mine=pltpu.make_async_copy(ins[t], slot(t, me), local_sems.at[t]),
                first=[copy(t, 0, me, sibling, src=ins[t])] + [copy(t, 1 + j, me, (*q, c), src=ins[t]) for j, q in enumerate(chips)],
                over_ici=[copy(t, 1 + j, (*q, c), me) for j, q in enumerate(chips)],
                passed=[copy(t, 4 + j, (*q, c), sibling) for j, q in enumerate(chips)],
                from_sibling=[copy(t, 0, sibling, me)] + [copy(t, 4 + j, (*q, 1 - c), me) for j, q in enumerate(chips)]))
        return plan

    def start(self, ins, outs, sems):
        for p in self._plan(ins, outs, sems):
            p["mine"].start()
            for cp in p["first"]:
                cp.start()

    def middle(self, ins, outs, sems):
        for p in self._plan(ins, outs, sems):
            for arrived, onward in zip(p["over_ici"], p["passed"]):
                arrived.wait_recv()
                onward.start()

    def finish(self, ins, outs, sems):
        plan = self._plan(ins, outs, sems)
        for p in plan:
            for cp in p["from_sibling"]:
                cp.wait_recv()
        for p in plan:
            for cp in p["first"] + p["passed"]:
                cp.wait_send()
            p["mine"].wait()


class _PairExchange:
    def __init__(self, grads, specs):
        n = len(grads)
        self.ins, self.specs = list(grads), list(specs)
        self.out_shape = [jax.ShapeDtypeStruct((N_CHIP,) + a.shape[:sp[0]] + (sp[1],) + a.shape[sp[0] + 1:], a.dtype)
                          for a, sp in zip(grads, specs)]
        self.sems = [pltpu.SemaphoreType.DMA((n,)), pltpu.SemaphoreType.DMA((n,))]

    def start(self, ins, outs, sems):
        send_sems, recv_sems = sems
        x, y, c = _here()
        for t in range(len(ins)):
            for q in range(N_CHIP):
                pltpu.make_async_remote_copy(
                    src_ref=_window(ins[t], self.specs[t], 2 * q + (1 - c)), dst_ref=outs[t].at[q],
                    send_sem=send_sems.at[t], recv_sem=recv_sems.at[t], device_id=(x, y, 1 - c), device_id_type=MESH).start()

    def middle(self, ins, outs, sems):
        pass

    def finish(self, ins, outs, sems):
        send_sems, recv_sems = sems
        x, y, c = _here()
        for t in range(len(ins)):
            every = pltpu.make_async_remote_copy(src_ref=outs[t], dst_ref=outs[t], send_sem=send_sems.at[t],
                                                 recv_sem=recv_sems.at[t], device_id=(x, y, 1 - c), device_id_type=MESH)
            every.wait_send()
            every.wait_recv()


class _ChipExchange:
    def __init__(self, pairs, slotted=(), whole=()):
        self.ins = list(pairs) + list(slotted) + list(whole)
        self.npair, self.nslot = len(pairs), len(pairs) + len(slotted)
        n = len(self.ins)
        self.out_shape = ([jax.ShapeDtypeStruct(a.shape, a.dtype) for a in list(pairs) + list(slotted)]
                          + [jax.ShapeDtypeStruct((N_DEV,) + a.shape, a.dtype) for a in whole])
        self.sems = [pltpu.SemaphoreType.DMA((7 * n,)), pltpu.SemaphoreType.DMA((7 * n,)), pltpu.SemaphoreType.DMA((n,))]

    def _plan(self, ins, outs, sems):
        send_sems, recv_sems, local_sems = sems
        npair, nslot = self.npair, self.nslot
        x, y, c = _here()
        me, chip = 4 * x + 2 * y + c, 2 * x + y
        chips = [(1 - x, y), (x, 1 - y), (1 - x, 1 - y)]
        peers = [(x, y, 1 - c)] + [(*q, c) for q in chips] + [(*q, 1 - c) for q in chips]

        def index(dev):
            return 4 * dev[0] + 2 * dev[1] + dev[2]

        def copy(t, k, arriving):
            peer = peers[k]
            if t < npair:
                src, mine, theirs = ins[t].at[2 * peer[0] + peer[1]], chip, 2 * peer[0] + peer[1]
            else:
                src, mine, theirs = (ins[t].at[index(peer)] if t < nslot else ins[t]), me, index(peer)
            return pltpu.make_async_remote_copy(
                src_ref=src, dst_ref=outs[t].at[theirs if arriving else mine],
                send_sem=send_sems.at[7 * t + k], recv_sem=recv_sems.at[7 * t + k], device_id=peer, device_id_type=MESH)

        own, sent, arriving = [], [], []
        for t in range(len(ins)):
            fan = range(1, 4) if t < npair else range(7)
            if t < npair:
                own.append(pltpu.make_async_copy(ins[t].at[chip], outs[t].at[chip], local_sems.at[t]))
            else:
                own.append(pltpu.make_async_copy(ins[t].at[me] if t < nslot else ins[t], outs[t].at[me], local_sems.at[t]))
            sent += [copy(t, k, False) for k in fan]
            arriving += [copy(t, k, True) for k in fan]
        return own, sent, arriving

    def start(self, ins, outs, sems):
        own, sent, _ = self._plan(ins, outs, sems)
        for cp in own + sent:
            cp.start()

    def middle(self, ins, outs, sems):
        pass

    def finish(self, ins, outs, sems):
        own, sent, arriving = self._plan(ins, outs, sems)
        for cp in arriving:
            cp.wait_recv()
        for cp in sent:
            cp.wait_send()
        for cp in own:
            cp.wait()


class _Jobs:
    def __init__(self, *jobs):
        self.jobs = jobs
        self.ins = [a for j in jobs for a in j.ins]
        self.out_shape = [a for j in jobs for a in j.out_shape]
        self.sems = [a for j in jobs for a in j.sems]

    def _split(self, ins, outs, sems):
        i = o = s = 0
        for j in self.jobs:
            yield j, ins[i:i + len(j.ins)], outs[o:o + len(j.out_shape)], sems[s:s + len(j.sems)]
            i, o, s = i + len(j.ins), o + len(j.out_shape), s + len(j.sems)

    def start(self, ins, outs, sems):
        for j, a, b, c in self._split(ins, outs, sems):
            j.start(a, b, c)

    def middle(self, ins, outs, sems):
        for j, a, b, c in self._split(ins, outs, sems):
            j.middle(a, b, c)

    def finish(self, ins, outs, sems):
        for j, a, b, c in self._split(ins, outs, sems):
            j.finish(a, b, c)

    def results(self, outs):
        return [b for _, _, b, _ in self._split((), outs, ())]


def _alone(job, name):
    return _pcall(lambda: None, comm=job, name=name, in_specs=[], out_specs=[], out_shape=[])()[1]


SMALL_ROWS = 24
REP_COLS = 1024


def _pad_to(a, rows, cols):
    return jnp.pad(a, ((0, rows - a.shape[0]), (0, cols - a.shape[1])))


def _pack_small(conv_a, sgu_norm, conv_ffn, cols):
    return jnp.concatenate([_pad_to(conv_a, 8, cols), _pad_to(sgu_norm, 8, cols),
                            _pad_to(conv_ffn.reshape(-1, conv_ffn.shape[-1]), 8, cols)], axis=0)


def _unpack_small(p, ca_w, sg_w, cf_w):
    return p[0:3, 0:ca_w], p[8:9, 0:sg_w], p[16:22, 0:cf_w].reshape(2, 3, cf_w)


def _tile_rows(rows):
    return -(-rows // 8) * 8


def _pack_rows(parts):
    return jnp.concatenate([_pad_to(a, _tile_rows(a.shape[0]), REP_COLS) for a in parts], axis=0)


def _unpack_rows(p, shapes):
    out, r0 = [], 0
    for r, c in shapes:
        out.append(p[r0:r0 + r, 0:c])
        r0 += _tile_rows(r)
    return out


def _rep_late(norm_mix0, norm_ffn0, pool_scale, b_conv0, w_pool):
    return _pack_rows([norm_mix0, norm_ffn0, pool_scale.reshape(1, -1), _pad_to(b_conv0, 1, 3 * REP_COLS).reshape(3, REP_COLS),
                       w_pool.reshape(-1, REP_COLS)])


def _rep_early(norm_mix1, norm_ffn1, final_norm, b_spatial, b_conv1, w_spatial, loss):
    return _pack_rows([norm_mix1, norm_ffn1, final_norm.reshape(1, -1), b_spatial.reshape(1, -1),
                       _pad_to(b_conv1, 1, 3 * REP_COLS).reshape(3, REP_COLS), w_spatial.reshape(-1, REP_COLS), loss])


def _unpack_rep(late, early, like):
    f = like["b_conv_ffn"].shape[1]
    nm0, nf0, ps, bc0, wp = _unpack_rows(late, [(1, REP_COLS), (1, REP_COLS), (1, like["pool_scale"].shape[1]), (3, REP_COLS),
                                                (like["w_pool"].size // REP_COLS, REP_COLS)])
    nm1, nf1, fin, bs, bc1, wsp, _ = _unpack_rows(early, [(1, REP_COLS)] * 4 + [(3, REP_COLS), (like["w_spatial"].size // REP_COLS, REP_COLS),
                                                          (1, REP_COLS)])
    return {
        "norm_mix": jnp.concatenate([nm0, nm1]), "norm_ffn": jnp.concatenate([nf0, nf1]), "final_norm": fin[0], "pool_scale": ps,
        "b_spatial": bs.reshape(like["b_spatial"].shape),
        "b_conv_ffn": jnp.concatenate([bc0.reshape(1, -1), bc1.reshape(1, -1)])[:, 0:f],
        "w_pool": wp.reshape(like["w_pool"].shape), "w_spatial": wsp.reshape(like["w_spatial"].shape),
    }


def _pad_slots(a, width, padded):
    a = a.reshape(*a.shape[:-1], N_DEV, width)
    a = jnp.pad(a, ((0, 0),) * (a.ndim - 1) + ((0, padded - width),))
    return a.reshape(*a.shape[:-2], N_DEV * padded)


def _unpad_slots(a, width, padded):
    a = a.reshape(*a.shape[:-1], N_DEV, padded)[..., 0:width]
    return a.reshape(*a.shape[:-2], N_DEV * width)


def kernel(x, norm_mix, norm_ffn, final_norm, w_in_even, conv_a, w_pool, pool_scale, w_out_even, w_in_odd, sgu_norm, w_spatial, b_spatial, w_out_odd, w_ffn_gate, w_ffn_up, conv_ffn, b_conv_ffn, w_ffn_down, loss_target, m_norm_mix, m_norm_ffn, m_final_norm, m_w_in_even, m_conv_a, m_w_pool, m_pool_scale, m_w_out_even, m_w_in_odd, m_sgu_norm, m_w_spatial, m_b_spatial, m_w_out_odd, m_w_ffn_gate, m_w_ffn_up, m_conv_ffn, m_b_conv_ffn, m_w_ffn_down, v_norm_mix, v_norm_ffn, v_final_norm, v_w_in_even, v_conv_a, v_w_pool, v_pool_scale, v_w_out_even, v_w_in_odd, v_sgu_norm, v_w_spatial, v_b_spatial, v_w_out_odd, v_w_ffn_gate, v_w_ffn_up, v_conv_ffn, v_b_conv_ffn, v_w_ffn_down):
    s, d = x.shape[1], x.shape[2]
    x2, target = x[0], loss_target[0]
    tm = min(512, s)
    tm_wide = min(2048, s)
    tn = 256
    tk_wide = 1024
    tn_fwd = 512
    row = lambda a: a.reshape(1, -1)
    ein, ro = w_in_even.shape[2], w_out_even.shape[1]
    fs = w_ffn_gate.shape[2]
    fsp = -(-fs // LANE) * LANE
    fp = N_DEV * fsp
    full = lambda shape, dtype=BF16: jax.ShapeDtypeStruct(shape, dtype)

    wg_s = [jnp.pad(w_ffn_gate[l].T, ((0, fsp - fs), (0, 0))).astype(BF16) for l in range(2)]
    wu_s = [jnp.pad(w_ffn_up[l].T, ((0, fsp - fs), (0, 0))).astype(BF16) for l in range(2)]
    wd_s = [jnp.pad(w_ffn_down[l], ((0, fsp - fs), (0, 0))).astype(BF16) for l in range(2)]
    small_s = _pack_small(conv_a[0], sgu_norm, conv_ffn, fsp)[None]
    in_spec, out_spec, gu_spec, down_spec = (1, ein), (0, ro), (0, fsp), (0, fsp)
    full_in, full_out, full_gu, full_down = full((d, N_DEV * ein)), full((N_DEV * ro, d)), full((fp, d)), full((fp, d))
    win_e, wout_e, gsmall = _alone(_Gather([w_in_even[0].astype(BF16), w_out_even[0].astype(BF16), small_s], [in_spec, out_spec, (0, 1)],
                                           [full_in, full_out, full((N_DEV, SMALL_ROWS, fsp), F32)]), "gather_mix0")
    ca_full = jnp.moveaxis(gsmall[:, 0:3, 0:conv_a.shape[2]], 0, 1).reshape(3, -1)
    sgu_full = gsmall[:, 8, 0:sgu_norm.shape[1]].reshape(1, -1)
    cf_full = jnp.moveaxis(gsmall[:, 16:22, :].reshape(N_DEV, 2, 3, fsp), 0, 2).reshape(2, 3, fp)
    cb_full = _pad_slots(b_conv_ffn, fs, fsp)

    tril = jnp.tril(jnp.ones((CHUNK, CHUNK), F32))
    ws_m = w_spatial[0] * tril
    ws_b = ws_m.astype(BF16)
    wst_b = jnp.swapaxes(ws_m, 1, 2).astype(BF16)
    bfull = jnp.repeat(b_spatial[0].T, CHUNK, axis=1)
    wpool_b = w_pool[0].astype(BF16)
    wpoolt_b = jnp.swapaxes(w_pool[0], 1, 2).astype(BF16)

    (xn0, proj0, cq0, pooled0, mix0, h1, hn0), (wg0,) = _even_fwd(
        x2, norm_mix[0:1], norm_ffn[0:1], win_e, ca_full, wpool_b, pool_scale, wout_e, tm,
        comm=_Gather([wg_s[0]], [gu_spec], [full_gu]))
    (g0, gc0), (wu0,) = _ffn_gate(hn0, wg0, cf_full[0], cb_full[0:1], tm_wide, tn_fwd, "ffn_gate_l0", comm=_Gather([wu_s[0]], [gu_spec], [full_gu]))
    (up0, a0, sl0, ds0), (wd0,) = _ffn_up(hn0, wu0, gc0, tm_wide, tn_fwd, "ffn_up_l0", comm=_Gather([wd_s[0]], [down_spec], [full_down]))
    (h2, xn1), (win_o, wout_o) = _ffn_fwd2(a0, wd0, h1, norm_mix[1:2], tm, "ffn_fwd2_l0", comm=_Gather(
        [w_in_odd[0].astype(BF16), w_out_odd[0].astype(BF16)], [in_spec, out_spec], [full_in, full_out]))
    (pre1, gate1, mixo1, h3, hn1), (wg1,) = _odd_fwd(xn1, h2, win_o, sgu_full, ws_b, bfull, wout_o, norm_ffn[1:2], tm,
                                                    comm=_Gather([wg_s[1]], [gu_spec], [full_gu]))
    (g1, gc1), (wu1,) = _ffn_gate(hn1, wg1, cf_full[1], cb_full[1:2], tm_wide, tn_fwd, "ffn_gate_l1", comm=_Gather([wu_s[1]], [gu_spec], [full_gu]))
    (up1, a1, sl1, ds1), (wd1,) = _ffn_up(hn1, wu1, gc1, tm_wide, tn_fwd, "ffn_up_l1", comm=_Gather([wd_s[1]], [down_spec], [full_down]))
    h4, _ = _ffn_fwd2(a1, wd1, h3, row(final_norm), tm, "ffn_fwd2_l1")

    core = lax.axis_index("c").astype(jnp.int32).reshape(1)
    ts = min(1024, s)
    dh4, dh4b, d_final, lossvec = _loss_bwd(h4, row(final_norm), target, tm)
    gd1 = _wgrad(a1, dh4b, tk_wide, s, "wgrad_down_l1")
    (dg1, dup1, dcw1, dcb1), (o_d1,) = _ffn_bwd1(dh4b, g1, sl1, ds1, up1, wd1, cf_full[1], tm_wide, tn, "ffn_bwd1_l1",
                                                 comm=_PairExchange([gd1], [down_spec]))
    p_d1 = _pair_add(gd1, o_d1, down_spec, core, "pair_add_down_l1")
    gg1 = _wgrad(dg1, hn1, tk_wide, s, "wgrad_gate_l1")
    gu1, (o_g1,) = _wgrad(dup1, hn1, tk_wide, s, "wgrad_up_l1", comm=_PairExchange([gg1], [gu_spec]))
    p_g1 = _pair_add(gg1, o_g1, gu_spec, core, "pair_add_gate_l1")
    jobs = _Jobs(_ChipExchange([p_d1]), _PairExchange([gu1], [gu_spec]))
    (dh3, dh3b, d_nffn1), res = _ffn_bwd2(dg1, dup1, wg1, wu1, h3, norm_ffn[1:2], dh4, tm, "ffn_bwd2_l1", comm=jobs)
    (s_d1,), (o_u1,) = jobs.results(res)
    p_u1 = _pair_add(gu1, o_u1, gu_spec, core, "pair_add_up_l1")
    (dpre1, dh2, dh2b, d_nmix1, d_sgu, d_ws, d_b), (s_g1, s_u1) = _odd_bwd(
        dh3b, dh3, wout_o, pre1, gate1, ws_b, wst_b, sgu_full, win_o, h2, norm_mix[1:2], tm, comm=_ChipExchange([p_g1, p_u1]))
    gi1 = _wgrad(xn1, dpre1, tk_wide, ts, "wgrad_in_odd")
    go1, (o_i1,) = _wgrad(mixo1, dh3b, tk_wide, ts, "wgrad_out_odd", comm=_PairExchange([gi1], [in_spec]))
    p_i1 = _pair_add(gi1, o_i1, in_spec, core, "pair_add_in_odd")
    gd0, (o_o1,) = _wgrad(a0, dh2b, tk_wide, s, "wgrad_down_l0", comm=_PairExchange([go1], [out_spec]))
    p_o1 = _pair_add(go1, o_o1, out_spec, core, "pair_add_out_odd")
    d_early = _rep_early(d_nmix1, d_nffn1, d_final, d_b, _unpad_slots(dcb1, fs, fsp), d_ws * tril, lossvec)
    jobs = _Jobs(_ChipExchange([p_i1, p_o1], [], [d_early]), _PairExchange([gd0], [down_spec]))
    (dg0, dup0, dcw0, dcb0), res = _ffn_bwd1(dh2b, g0, sl0, ds0, up0, wd0, cf_full[0], tm_wide, tn, "ffn_bwd1_l0", comm=jobs)
    (s_i1, s_o1, r_early), (o_d0,) = jobs.results(res)
    p_d0 = _pair_add(gd0, o_d0, down_spec, core, "pair_add_down_l0")
    gg0, (s_d0,) = _wgrad(dg0, hn0, tk_wide, s, "wgrad_gate_l0", comm=_ChipExchange([p_d0]))
    gu0, (o_g0,) = _wgrad(dup0, hn0, tk_wide, s, "wgrad_up_l0", comm=_PairExchange([gg0], [gu_spec]))
    p_g0 = _pair_add(gg0, o_g0, gu_spec, core, "pair_add_gate_l0")
    jobs = _Jobs(_ChipExchange([p_g0]), _PairExchange([gu0], [gu_spec]))
    (dh1, dh1b, d_nffn0), res = _ffn_bwd2(dg0, dup0, wg0, wu0, h1, norm_ffn[0:1], dh2, tm, "ffn_bwd2_l0", comm=jobs)
    (s_g0,), (o_u0,) = jobs.results(res)
    p_u0 = _pair_add(gu0, o_u0, gu_spec, core, "pair_add_up_l0")
    go0 = _wgrad(mix0, dh1b, tk_wide, ts, "wgrad_out_even")
    jobs = _Jobs(_ChipExchange([p_u0]), _PairExchange([go0], [out_spec]))
    (dproj0, grad_x, d_nmix0, d_ca, d_wp, d_ps), res = _even_bwd(
        dh1b, dh1, wout_e, proj0, cq0, pooled0, ca_full, wpool_b, wpoolt_b, pool_scale, win_e, x2, norm_mix[0:1], tm, comm=jobs)
    (s_u0,), (o_o0,) = jobs.results(res)
    p_o0 = _pair_add(go0, o_o0, out_spec, core, "pair_add_out_even")
    d_small = jnp.stack([_pack_small(a, b, c, fsp) for a, b, c in zip(
        jnp.moveaxis(d_ca.reshape(3, N_DEV, -1), 1, 0), jnp.moveaxis(d_sgu.reshape(1, N_DEV, -1), 1, 0),
        jnp.moveaxis(jnp.stack([dcw0, dcw1]).reshape(2, 3, N_DEV, fsp), 2, 0))])
    d_late = _rep_late(d_nmix0, d_nffn0, d_ps, _unpad_slots(dcb0, fs, fsp), d_wp)
    gi0, (r_small, r_late) = _wgrad(xn0, dproj0, tk_wide, ts, "wgrad_in_even", comm=_ChipExchange([], [d_small], [d_late]))
    jobs = _Jobs(_PairExchange([gi0], [in_spec]), _ChipExchange([p_o0]))
    (o_i0,), (s_o0,) = jobs.results(_alone(jobs, "pair_exchange_in_even"))
    p_i0 = _pair_add(gi0, o_i0, in_spec, core, "pair_add_in_even")
    (s_i0,) = _alone(_ChipExchange([p_i0]), "chip_exchange_last")
    loss = jnp.sum(r_early[:, r_early.shape[1] - 8, :])

    out = {}
    out["w_in_even"] = _adamw(s_i0, w_in_even[0], m_w_in_even[0], v_w_in_even[0], 256, "adamw_in_even")
    out["w_out_even"] = _adamw(s_o0, w_out_even[0], m_w_out_even[0], v_w_out_even[0], ro // 2, "adamw_out_even")
    out["w_in_odd"] = _adamw(s_i1, w_in_odd[0], m_w_in_odd[0], v_w_in_odd[0], 256, "adamw_in_odd")
    out["w_out_odd"] = _adamw(s_o1, w_out_odd[0], m_w_out_odd[0], v_w_out_odd[0], ro // 2, "adamw_out_odd")
    tp = lambda a: jnp.swapaxes(a, 1, 2)
    for nm, s1, s0, w, m, v, back in (("w_ffn_gate", s_g1, s_g0, tp(w_ffn_gate), tp(m_w_ffn_gate), tp(v_w_ffn_gate), tp),
                                      ("w_ffn_up", s_u1, s_u0, tp(w_ffn_up), tp(m_w_ffn_up), tp(v_w_ffn_up), tp),
                                      ("w_ffn_down", s_d1, s_d0, w_ffn_down, m_w_ffn_down, v_w_ffn_down, lambda a: a)):
        l1 = _adamw(s1, w, m, v, fs // 2, "adamw_%s_l1" % nm, layer=1)
        out[nm] = [back(a) for a in _adamw(s0, w, m, v, fs // 2, "adamw_%s_l0" % nm, layer=0, into=l1)]
    small = _adamw(r_small, _pack_small(conv_a[0], sgu_norm, conv_ffn, fsp), _pack_small(m_conv_a[0], m_sgu_norm, m_conv_ffn, fsp),
                   _pack_small(v_conv_a[0], v_sgu_norm, v_conv_ffn, fsp), SMALL_ROWS, "adamw_small")
    no_loss = jnp.zeros((1, REP_COLS), F32)
    early = _adamw(r_early, *[_rep_early(nm[1:2], nf[1:2], fn, bs, bc[1:2], wsp, no_loss) for nm, nf, fn, bs, bc, wsp in (
        (norm_mix, norm_ffn, final_norm, b_spatial, b_conv_ffn, w_spatial), (m_norm_mix, m_norm_ffn, m_final_norm, m_b_spatial, m_b_conv_ffn, m_w_spatial),
        (v_norm_mix, v_norm_ffn, v_final_norm, v_b_spatial, v_b_conv_ffn, v_w_spatial))], r_early.shape[1], "adamw_replicated_early")
    late = _adamw(r_late, *[_rep_late(nm[0:1], nf[0:1], ps, bc[0:1], wp) for nm, nf, ps, bc, wp in (
        (norm_mix, norm_ffn, pool_scale, b_conv_ffn, w_pool), (m_norm_mix, m_norm_ffn, m_pool_scale, m_b_conv_ffn, m_w_pool),
        (v_norm_mix, v_norm_ffn, v_pool_scale, v_b_conv_ffn, v_w_pool))], r_late.shape[1], "adamw_replicated_late")

    names = ["norm_mix", "norm_ffn", "final_norm", "w_in_even", "conv_a", "w_pool", "pool_scale", "w_out_even", "w_in_odd", "sgu_norm",
             "w_spatial", "b_spatial", "w_out_odd", "w_ffn_gate", "w_ffn_up", "conv_ffn", "b_conv_ffn", "w_ffn_down"]
    like = {"norm_mix": norm_mix, "norm_ffn": norm_ffn, "final_norm": final_norm, "w_in_even": w_in_even, "conv_a": conv_a,
            "w_pool": w_pool, "pool_scale": pool_scale, "w_out_even": w_out_even, "w_in_odd": w_in_odd, "sgu_norm": sgu_norm,
            "w_spatial": w_spatial, "b_spatial": b_spatial, "w_out_odd": w_out_odd, "w_ffn_gate": w_ffn_gate, "w_ffn_up": w_ffn_up,
            "conv_ffn": conv_ffn, "b_conv_ffn": b_conv_ffn, "w_ffn_down": w_ffn_down}
    groups = []
    for k in range(4):
        ca_k, sg_k, cf_k = _unpack_small(small[k], conv_a.shape[2], sgu_norm.shape[1], conv_ffn.shape[2])
        vals = dict(_unpack_rep(late[k], early[k], like))
        vals.update(conv_a=ca_k, sgu_norm=sg_k, conv_ffn=cf_k)
        for nm in ("w_in_even", "w_in_odd", "w_out_even", "w_out_odd", "w_ffn_gate", "w_ffn_up", "w_ffn_down"):
            vals[nm] = out[nm][k]
        groups.append([vals[nm].reshape(like[nm].shape) for nm in names])
    return (loss, grad_x[None], *groups[0], *groups[1], *groups[2], *groups[3])
```

```python
import functools

import jax
import jax.numpy as jnp
from jax import lax
from jax.experimental import pallas as pl
from jax.experimental.pallas import tpu as pltpu

F32, BF16 = jnp.float32, jnp.bfloat16
EPS = 1e-6
WINDOWS = (2, 4, 8, 16)
HALO = 16
CHUNK = 128
N_DEV = 8
N_CHIP = 4
MESH = pl.DeviceIdType.MESH
VMEM_LIMIT = 56 * 2**20
LANE = 128
ADAM_LR, ADAM_B1, ADAM_B2, ADAM_EPS, ADAM_WD, ADAM_STEP = 0.001, 0.9, 0.999, 1e-08, 0.01, 10
SUB_ROWS = 256
LATE_NUM, LATE_DEN = 7, 8
INV_SQRT2 = 0.7071067811865476
INV_SQRT2PI = 0.3989422804014327


def _pcall(body, comm=None, **kw):
    if comm is None:
        return pl.pallas_call(body, **kw)
    in_specs, out_specs, out_shape = list(kw.pop("in_specs")), kw.pop("out_specs"), kw.pop("out_shape")
    single = not isinstance(out_shape, (list, tuple))
    out_specs, out_shape = ([out_specs], [out_shape]) if single else (list(out_specs), list(out_shape))
    scratch = list(kw.pop("scratch_shapes", []))
    grid = kw.get("grid", ())
    n_in, n_out, n_scr, c_in, c_out = len(in_specs), len(out_specs), len(scratch), len(comm.ins), len(comm.out_shape)

    def hosted(*refs):
        cuts = [0, n_in, n_in + c_in, n_in + c_in + n_out, n_in + c_in + n_out + c_out, n_in + c_in + n_out + c_out + n_scr, len(refs)]
        ins, cins, outs, couts, scr, sems = (refs[a:b] for a, b in zip(cuts[:-1], cuts[1:]))
        if grid:
            step, steps = 0, 1
            for axis, size in enumerate(grid):
                step, steps = step * size + pl.program_id(axis), steps * size
            pl.when(step == 0)(lambda: comm.start(cins, couts, sems))
            pl.when(step == (steps * LATE_NUM) // LATE_DEN)(lambda: comm.middle(cins, couts, sems))
            body(*ins, *outs, *scr)
            pl.when(step == steps - 1)(lambda: comm.finish(cins, couts, sems))
        else:
            comm.start(cins, couts, sems)
            comm.middle(cins, couts, sems)
            body(*ins, *outs, *scr)
            comm.finish(cins, couts, sems)

    call = pl.pallas_call(hosted, in_specs=in_specs + [_hbm()] * c_in, out_specs=out_specs + [_hbm()] * c_out,
                          out_shape=out_shape + list(comm.out_shape), scratch_shapes=scratch + list(comm.sems), **kw)

    def run(*args):
        res = call(*args, *comm.ins)
        own = res[0] if single else res[:n_out]
        return own, res[n_out:]

    return run


def _params(*sem):
    return pltpu.CompilerParams(dimension_semantics=sem, vmem_limit_bytes=VMEM_LIMIT)


def _whole(shape):
    return pl.BlockSpec(shape, lambda *_: (0,) * len(shape))


def _resident(shape):
    return pl.BlockSpec(shape, lambda *_: (0,) * len(shape), pipeline_mode=pl.Buffered(1))


def _rows(i, tm):
    return pl.ds(pl.multiple_of(i * tm, tm), tm)


def _nn(a, b):
    return jnp.dot(a, b, preferred_element_type=F32)


def _nt(a, b):
    return lax.dot_general(a, b, (((1,), (1,)), ((), ())), preferred_element_type=F32)


def _tn(a, b):
    return lax.dot_general(a, b, (((0,), (0,)), ((), ())), preferred_element_type=F32)


def _rms(x, gain):
    r = lax.rsqrt(jnp.mean(x * x, axis=-1, keepdims=True) + EPS)
    return x * r * gain


def _rms_bwd(dy, x, gain):
    r = lax.rsqrt(jnp.mean(x * x, axis=-1, keepdims=True) + EPS)
    xh = x * r
    dgain = jnp.sum(dy * xh, axis=0, keepdims=True)
    dxh = dy * gain
    dx = r * (dxh - xh * jnp.mean(dxh * xh, axis=-1, keepdims=True))
    return dx, dgain


def _gelu(x):
    return 0.5 * x * (1.0 + lax.erf(x * INV_SQRT2))


def _gelu_grad(x):
    return 0.5 * (1.0 + lax.erf(x * INV_SQRT2)) + x * jnp.exp(-0.5 * x * x) * INV_SQRT2PI


def _acc(ref, val, first):
    @pl.when(first)
    def _():
        ref[...] = val

    @pl.when(jnp.logical_not(first))
    def _():
        ref[...] += val


def _shift(ext, k, back):
    n = ext.shape[0]
    return pltpu.roll(ext, k if back else n - k, axis=0)


def _window_sum(ext, w, back):
    total, step = ext, 1
    while step < w:
        total = total + _shift(total, step, back)
        step *= 2
    return total


def _counts(row0, tm, w):
    pos1 = (row0 + lax.broadcasted_iota(jnp.int32, (tm, 1), 0) + 1).astype(F32)
    return jnp.minimum(pos1, float(w))


def _even_fwd(x, gmix, gffn, win, conva, wpool, pscale, wout, tm, comm=None):
    s, d = x.shape
    e = win.shape[1]
    aw = e // 4

    def body(x_ref, gmix_ref, gffn_ref, win_ref, ca_ref, wp_ref, ps_ref, wout_ref,
             xn_ref, proj_ref, cq_ref, pooled_ref, mix_ref, h_ref, hn_ref, qbuf, zbuf):
        i = pl.program_id(0)

        @pl.when(i == 0)
        def _():
            qbuf[0:HALO, :] = jnp.zeros((HALO, aw), F32)
            zbuf[0:HALO, :] = jnp.zeros((HALO, aw), F32)

        xv = x_ref[...]
        xn = _rms(xv, gmix_ref[...]).astype(BF16)
        xn_ref[...] = xn
        proj = _nn(xn, win_ref[...])
        proj_ref[...] = proj.astype(BF16)
        a_b, a_c, a_v, z = (proj[:, k * aw:(k + 1) * aw] for k in range(4))
        q = a_c * a_v
        qbuf[HALO:HALO + tm, :] = q
        qext = qbuf[...]
        cur = slice(HALO, HALO + tm)
        cq = ca_ref[2:3, :] * q + ca_ref[1:2, :] * _shift(qext, 1, True)[cur, :] + ca_ref[0:1, :] * _shift(qext, 2, True)[cur, :]
        cq_ref[...] = cq.astype(BF16)
        y_a = a_b * cq
        zbuf[HALO:HALO + tm, :] = z
        zext = zbuf[...]
        ys = []
        for g, w in enumerate(WINDOWS):
            cols = slice(g * LANE, (g + 1) * LANE)
            acc = _window_sum(zext[:, cols], w, True)[cur, :]
            pooled = (acc / _counts(i * tm, tm, w) - z[:, cols]).astype(BF16)
            pooled_ref[:, cols] = pooled
            ys.append(_nn(pooled, wp_ref[g]))
        y_b = jnp.concatenate(ys, axis=1) * ps_ref[...]
        mix = jnp.concatenate([y_a, y_b], axis=1).astype(BF16)
        mix_ref[...] = mix
        h = xv + _nn(mix, wout_ref[...])
        h_ref[...] = h
        hn_ref[...] = _rms(h, gffn_ref[...]).astype(BF16)
        qbuf[0:HALO, :] = qbuf[tm:tm + HALO, :]
        zbuf[0:HALO, :] = zbuf[tm:tm + HALO, :]

    row = lambda c: pl.BlockSpec((tm, c), lambda i: (i, 0))
    return _pcall(
        body, comm=comm, name="even_fwd", grid=(s // tm,),
        in_specs=[row(d), _whole((1, d)), _whole((1, d)), _whole(win.shape), _whole(conva.shape), _whole(wpool.shape),
                  _whole(pscale.shape), _whole(wout.shape)],
        out_specs=[row(d), row(e), row(aw), row(aw), row(d), row(d), row(d)],
        out_shape=[jax.ShapeDtypeStruct((s, d), BF16), jax.ShapeDtypeStruct((s, e), BF16), jax.ShapeDtypeStruct((s, aw), BF16),
                   jax.ShapeDtypeStruct((s, aw), BF16), jax.ShapeDtypeStruct((s, d), BF16), jax.ShapeDtypeStruct((s, d), F32),
                   jax.ShapeDtypeStruct((s, d), BF16)],
        scratch_shapes=[pltpu.VMEM((tm + HALO, aw), F32), pltpu.VMEM((tm + HALO, aw), F32)],
        compiler_params=_params("arbitrary"),
    )(x, gmix, gffn, win, conva, wpool, pscale, wout)


def _ffn_gate(hn, wgt, cw, cb, tm, tn, name, comm=None):
    s, d = hn.shape
    f = wgt.shape[0]
    sub = tm

    def body(hn_ref, wg_ref, cw_ref, cb_ref, g_ref, gc_ref, gbuf):
        i = pl.program_id(1)

        @pl.when(i == 0)
        def _():
            gbuf[0:HALO, :] = jnp.zeros((HALO, tn), F32)

        wg = wg_ref[...]
        for c in range(tm // sub):
            rows = pl.ds(c * sub, sub)
            g = _nt(hn_ref[pl.ds(pl.multiple_of(i * tm + c * sub, sub), sub), :], wg)
            g_ref[rows, :] = g.astype(BF16)
            gbuf[pl.ds(HALO + c * sub, sub), :] = g
            ext = gbuf[pl.ds(c * sub, sub + HALO), :]
            gc = (cw_ref[2:3, :] * g + cw_ref[1:2, :] * _shift(ext, 1, True)[HALO:, :]
                  + cw_ref[0:1, :] * _shift(ext, 2, True)[HALO:, :] + cb_ref[...])
            gc_ref[rows, :] = gc.astype(BF16)
        gbuf[0:HALO, :] = gbuf[tm:tm + HALO, :]

    tile = pl.BlockSpec((tm, tn), lambda j, i: (i, j))
    wcol = lambda r: pl.BlockSpec((r, tn), lambda j, i: (0, j))
    out = jax.ShapeDtypeStruct((s, f), BF16)
    return _pcall(
        body, comm=comm, name=name, grid=(f // tn, s // tm),
        in_specs=[_resident((s, d)), pl.BlockSpec((tn, d), lambda j, i: (j, 0)), wcol(3), wcol(1)],
        out_specs=[tile, tile], out_shape=[out, out],
        scratch_shapes=[pltpu.VMEM((tm + HALO, tn), F32)],
        compiler_params=_params("arbitrary", "arbitrary"),
    )(hn, wgt, cw, cb)


def _ffn_up(hn, wut, gc, tm, tn, name, comm=None):
    s, d = hn.shape
    f = wut.shape[0]
    sub = min(SUB_ROWS, tm)

    def body(hn_ref, wu_ref, gc_ref, up_ref, a_ref, silu_ref, dsilu_ref):
        wu = wu_ref[...]
        for c in range(tm // sub):
            rows = pl.ds(c * sub, sub)
            up = _nt(hn_ref[pl.ds(pl.multiple_of(pl.program_id(1) * tm + c * sub, sub), sub), :], wu)
            up_ref[rows, :] = up.astype(BF16)
            gc = gc_ref[rows, :].astype(F32)
            sg = jax.nn.sigmoid(gc)
            silu = gc * sg
            silu_ref[rows, :] = silu.astype(BF16)
            dsilu_ref[rows, :] = (sg * (1.0 - silu) + silu).astype(BF16)
            a_ref[rows, :] = (silu * up).astype(BF16)

    tile = pl.BlockSpec((tm, tn), lambda j, i: (i, j))
    out = jax.ShapeDtypeStruct((s, f), BF16)
    return _pcall(
        body, comm=comm, name=name, grid=(f // tn, s // tm),
        in_specs=[_resident((s, d)), pl.BlockSpec((tn, d), lambda j, i: (j, 0)), tile],
        out_specs=[tile, tile, tile, tile], out_shape=[out, out, out, out],
        compiler_params=_params("arbitrary", "arbitrary"),
    )(hn, wut, gc)


def _ffn_fwd2(a, wd, h, gain, tm, name, comm=None):
    s, d = h.shape
    f = a.shape[1]

    def body(a_ref, wd_ref, h_ref, gain_ref, ho_ref, hn_ref):
        wd_v, gain = wd_ref[...], gain_ref[...]
        sub = min(SUB_ROWS, tm)
        for c in range(tm // sub):
            rows = pl.ds(c * sub, sub)
            ho = h_ref[rows, :] + _nn(a_ref[rows, :], wd_v)
            ho_ref[rows, :] = ho
            hn_ref[rows, :] = _rms(ho, gain).astype(BF16)

    row = lambda c: pl.BlockSpec((tm, c), lambda i: (i, 0))
    return _pcall(
        body, comm=comm, name=name, grid=(s // tm,),
        in_specs=[row(f), _resident(wd.shape), row(d), _whole((1, d))],
        out_specs=[row(d), row(d)],
        out_shape=[jax.ShapeDtypeStruct((s, d), F32), jax.ShapeDtypeStruct((s, d), BF16)],
        compiler_params=_params("arbitrary"),
    )(a, wd, h, gain)


def _odd_fwd(xn, h, win, sgu, ws, bfull, wout, gffn, tm, comm=None):
    s, d = h.shape
    e = win.shape[1]
    cw = e // 2
    heads = ws.shape[0]

    def body(xn_ref, h_ref, win_ref, sgu_ref, ws_ref, b_ref, wout_ref, gffn_ref,
             pre_ref, gate_ref, mixo_ref, ho_ref, hn_ref, gbuf):
        pre = _nn(xn_ref[...], win_ref[...])
        pre_ref[...] = pre.astype(BF16)
        p = _gelu(pre)
        u, v = p[:, :cw], p[:, cw:]
        vn = _rms(v, sgu_ref[...]).astype(BF16)
        for n in range(tm // CHUNK):
            rows = slice(n * CHUNK, (n + 1) * CHUNK)
            for hd in range(heads):
                cols = slice(hd * CHUNK, (hd + 1) * CHUNK)
                gbuf[rows, cols] = _nn(ws_ref[hd], vn[rows, cols]) + b_ref[:, cols]
        gate = gbuf[...]
        gate_ref[...] = gate.astype(BF16)
        mixo = (u * gate).astype(BF16)
        mixo_ref[...] = mixo
        ho = h_ref[...] + _nn(mixo, wout_ref[...])
        ho_ref[...] = ho
        hn_ref[...] = _rms(ho, gffn_ref[...]).astype(BF16)

    row = lambda c: pl.BlockSpec((tm, c), lambda i: (i, 0))
    return _pcall(
        body, comm=comm, name="odd_fwd", grid=(s // tm,),
        in_specs=[row(d), row(d), _whole(win.shape), _whole(sgu.shape), _whole(ws.shape), _whole(bfull.shape),
                  _whole(wout.shape), _whole((1, d))],
        out_specs=[row(e), row(cw), row(cw), row(d), row(d)],
        out_shape=[jax.ShapeDtypeStruct((s, e), BF16), jax.ShapeDtypeStruct((s, cw), BF16), jax.ShapeDtypeStruct((s, cw), BF16),
                   jax.ShapeDtypeStruct((s, d), F32), jax.ShapeDtypeStruct((s, d), BF16)],
        scratch_shapes=[pltpu.VMEM((tm, cw), F32)],
        compiler_params=_params("arbitrary"),
    )(xn, h, win, sgu, ws, bfull, wout, gffn)


def _loss_bwd(h, gain, target, tm):
    s, d = h.shape

    def body(h_ref, gain_ref, t_ref, dh_ref, dhb_ref, dgain_ref, loss_ref):
        i = pl.program_id(0)
        hv = h_ref[...]
        gain = gain_ref[...]
        err = _rms(hv, gain) - t_ref[...]
        dy = err * (1.0 / d)
        dx, dgain = _rms_bwd(dy, hv, gain)
        dh_ref[...] = dx
        dhb_ref[...] = dx.astype(BF16)
        _acc(dgain_ref, dgain, i == 0)
        _acc(loss_ref, jnp.sum(err * err, axis=0, keepdims=True) * (0.5 / d), i == 0)

    row = pl.BlockSpec((tm, d), lambda i: (i, 0))
    return _pcall(
        body, name="loss_bwd", grid=(s // tm,),
        in_specs=[row, _whole((1, d)), row],
        out_specs=[row, row, _whole((1, d)), _whole((1, d))],
        out_shape=[jax.ShapeDtypeStruct((s, d), F32), jax.ShapeDtypeStruct((s, d), BF16), jax.ShapeDtypeStruct((1, d), F32),
                   jax.ShapeDtypeStruct((1, d), F32)],
        compiler_params=_params("arbitrary"),
    )(h, gain, target)


def _ffn_bwd1(dhb, g, silu, dsilu, up, wd, cw, tm, tn, name, comm=None):
    s, d = dhb.shape
    f = g.shape[1]
    ni = s // tm

    def body(dh_ref, g_ref, silu_ref, dsilu_ref, up_ref, wd_ref, cw_ref, dg_ref, dup_ref, dcw_ref, dcb_ref, ebuf):
        i = pl.program_id(1)

        @pl.when(i == 0)
        def _():
            ebuf[tm:tm + HALO, :] = jnp.zeros((HALO, tn), F32)

        wd_v = wd_ref[...]
        sub = min(SUB_ROWS, tm)
        sums = [None] * 4
        for c in reversed(range(tm // sub)):
            rows = pl.ds(c * sub, sub)
            da = _nt(dh_ref[pl.ds(pl.multiple_of((ni - 1 - i) * tm + c * sub, sub), sub), :], wd_v)
            dup_ref[rows, :] = (da * silu_ref[rows, :].astype(F32)).astype(BF16)
            dgc = da * up_ref[rows, :].astype(F32) * dsilu_ref[rows, :].astype(F32)
            ebuf[rows, :] = dgc
            ext = ebuf[pl.ds(c * sub, sub + HALO), :]
            s1 = _shift(ext, 1, False)[0:sub, :]
            s2 = _shift(ext, 2, False)[0:sub, :]
            dg_ref[rows, :] = (cw_ref[2:3, :] * dgc + cw_ref[1:2, :] * s1 + cw_ref[0:1, :] * s2).astype(BF16)
            gv = g_ref[rows, :].astype(F32)
            for k, term in enumerate((s2 * gv, s1 * gv, dgc * gv, dgc)):
                part = jnp.sum(term, axis=0, keepdims=True)
                sums[k] = part if sums[k] is None else sums[k] + part
        for k in range(3):
            _acc(dcw_ref.at[k:k + 1, :], sums[k], i == 0)
        _acc(dcb_ref, sums[3], i == 0)
        ebuf[tm:tm + HALO, :] = ebuf[0:HALO, :]

    tile = pl.BlockSpec((tm, tn), lambda j, i: (ni - 1 - i, j))
    wcol = lambda r: pl.BlockSpec((r, tn), lambda j, i: (0, j))
    out = jax.ShapeDtypeStruct((s, f), BF16)
    return _pcall(
        body, comm=comm, name=name, grid=(f // tn, ni),
        in_specs=[_resident((s, d)), tile, tile, tile, tile,
                  pl.BlockSpec((tn, d), lambda j, i: (j, 0)), wcol(3)],
        out_specs=[tile, tile, wcol(3), wcol(1)],
        out_shape=[out, out, jax.ShapeDtypeStruct((3, f), F32), jax.ShapeDtypeStruct((1, f), F32)],
        scratch_shapes=[pltpu.VMEM((tm + HALO, tn), F32)],
        compiler_params=_params("arbitrary", "arbitrary"),
    )(dhb, g, silu, dsilu, up, wd, cw)


def _ffn_bwd2(dg, dup, wg, wu, h, gain, dh, tm, name, comm=None):
    s, d = h.shape
    f = dg.shape[1]

    def body(dg_ref, dup_ref, wg_ref, wu_ref, h_ref, gain_ref, dh_ref, dho_ref, dhb_ref, dgain_ref):
        wg_v, wu_v, gain = wg_ref[...], wu_ref[...], gain_ref[...]
        sub = min(SUB_ROWS, tm)
        dgain = None
        for c in range(tm // sub):
            rows = pl.ds(c * sub, sub)
            dhn = _nn(dg_ref[rows, :], wg_v) + _nn(dup_ref[rows, :], wu_v)
            dx, part = _rms_bwd(dhn, h_ref[rows, :], gain)
            dgain = part if dgain is None else dgain + part
            dho = dh_ref[rows, :] + dx
            dho_ref[rows, :] = dho
            dhb_ref[rows, :] = dho.astype(BF16)
        _acc(dgain_ref, dgain, pl.program_id(0) == 0)

    row = lambda c: pl.BlockSpec((tm, c), lambda i: (i, 0))
    return _pcall(
        body, comm=comm, name=name, grid=(s // tm,),
        in_specs=[row(f), row(f), _resident(wg.shape), _resident(wu.shape), row(d), _whole((1, d)), row(d)],
        out_specs=[row(d), row(d), _whole((1, d))],
        out_shape=[jax.ShapeDtypeStruct((s, d), F32), jax.ShapeDtypeStruct((s, d), BF16), jax.ShapeDtypeStruct((1, d), F32)],
        compiler_params=_params("arbitrary"),
    )(dg, dup, wg, wu, h, gain, dh)


def _odd_bwd(dhb, dh, wout, pre, gate, ws, wst, sgu, win, h, gmix, tm, comm=None):
    s, d = h.shape
    e = win.shape[1]
    cw = e // 2
    heads = ws.shape[0]
    ni = s // tm

    def body(dhb_ref, dh_ref, wout_ref, pre_ref, gate_ref, ws_ref, wst_ref, sgu_ref, win_ref, h_ref, gmix_ref,
             dpre_ref, dho_ref, dhob_ref, dgain_ref, dsgu_ref, dws_ref, db_ref, vbuf, gacc):
        i = pl.program_id(0)
        first = i == 0
        dmixo = _nt(dhb_ref[...], wout_ref[...])
        pre = pre_ref[...].astype(F32)
        p = _gelu(pre)
        u, v = p[:, :cw], p[:, cw:]
        sgu = sgu_ref[...]
        rv = lax.rsqrt(jnp.mean(v * v, axis=-1, keepdims=True) + EPS)
        vh = v * rv
        vn = (vh * sgu).astype(BF16)
        du = dmixo * gate_ref[...].astype(F32)
        dgate = dmixo * u
        dgate_b = dgate.astype(BF16)
        gsum = dgate[0:CHUNK, :]
        for n in range(1, tm // CHUNK):
            gsum = gsum + dgate[n * CHUNK:(n + 1) * CHUNK, :]
        _acc(gacc, gsum, first)
        for hd in range(heads):
            cols = slice(hd * CHUNK, (hd + 1) * CHUNK)
            dws = None
            for n in range(tm // CHUNK):
                rows = slice(n * CHUNK, (n + 1) * CHUNK)
                vbuf[rows, cols] = _nn(wst_ref[hd], dgate_b[rows, cols])
                part = _nt(dgate_b[rows, cols], vn[rows, cols])
                dws = part if dws is None else dws + part
            _acc(dws_ref.at[hd], dws, first)
        dvn = vbuf[...]
        _acc(dsgu_ref, jnp.sum(dvn * vh, axis=0, keepdims=True), first)
        dvh = dvn * sgu
        dv = rv * (dvh - vh * jnp.mean(dvh * vh, axis=-1, keepdims=True))
        dpre = (jnp.concatenate([du, dv], axis=1) * _gelu_grad(pre)).astype(BF16)
        dpre_ref[...] = dpre
        dx, dgain = _rms_bwd(_nt(dpre, win_ref[...]), h_ref[...], gmix_ref[...])
        dho = dh_ref[...] + dx
        dho_ref[...] = dho
        dhob_ref[...] = dho.astype(BF16)
        _acc(dgain_ref, dgain, first)

        @pl.when(i == ni - 1)
        def _():
            ones = jnp.ones((8, CHUNK), F32)
            for hd in range(heads):
                tot = lax.dot_general(ones, gacc[:, hd * CHUNK:(hd + 1) * CHUNK], (((1,), (1,)), ((), ())),
                                      preferred_element_type=F32, precision=lax.Precision.HIGHEST)
                db_ref[hd:hd + 1, :] = tot[0:1, :]

    row = lambda c: pl.BlockSpec((tm, c), lambda i: (i, 0))
    return _pcall(
        body, comm=comm, name="odd_bwd", grid=(ni,),
        in_specs=[row(d), row(d), _whole(wout.shape), row(e), row(cw), _whole(ws.shape), _whole(wst.shape), _whole(sgu.shape),
                  _whole(win.shape), row(d), _whole((1, d))],
        out_specs=[row(e), row(d), row(d), _whole((1, d)), _whole((1, cw)), _whole(ws.shape), _whole((heads, CHUNK))],
        out_shape=[jax.ShapeDtypeStruct((s, e), BF16), jax.ShapeDtypeStruct((s, d), F32), jax.ShapeDtypeStruct((s, d), BF16),
                   jax.ShapeDtypeStruct((1, d), F32), jax.ShapeDtypeStruct((1, cw), F32), jax.ShapeDtypeStruct(ws.shape, F32),
                   jax.ShapeDtypeStruct((heads, CHUNK), F32)],
        scratch_shapes=[pltpu.VMEM((tm, cw), F32), pltpu.VMEM((CHUNK, cw), F32)],
        compiler_params=_params("arbitrary"),
    )(dhb, dh, wout, pre, gate, ws, wst, sgu, win, h, gmix)


def _even_bwd(dhb, dh, wout, proj, cq, pooled, conva, wpool, wpoolt, pscale, win, x, gmix, tm, comm=None):
    s, d = x.shape
    e = win.shape[1]
    aw = e // 4
    ni = s // tm

    def body(dhb_ref, dh_ref, wout_ref, proj_ref, cq_ref, pooled_ref, ca_ref, wp_ref, wpt_ref, ps_ref, win_ref, x_ref, gmix_ref,
             dproj_ref, dx_ref, dgain_ref, dca_ref, dwp_ref, dps_ref, cbuf, ebuf):
        i = pl.program_id(0)
        first = i == 0

        @pl.when(first)
        def _():
            cbuf[tm:tm + HALO, :] = jnp.zeros((HALO, aw), F32)
            ebuf[tm:tm + HALO, :] = jnp.zeros((HALO, aw), F32)

        dmix = _nt(dhb_ref[...], wout_ref[...])
        dy_a, dy_b = dmix[:, :aw], dmix[:, aw:]
        proj = proj_ref[...].astype(F32)
        a_b, a_c, a_v = (proj[:, k * aw:(k + 1) * aw] for k in range(3))
        da_b = dy_a * cq_ref[...].astype(F32)
        dcq = dy_a * a_b
        cbuf[0:tm, :] = dcq
        cext = cbuf[...]
        s1 = _shift(cext, 1, False)[0:tm, :]
        s2 = _shift(cext, 2, False)[0:tm, :]
        q = a_c * a_v
        for k, shifted in enumerate((s2, s1, dcq)):
            _acc(dca_ref.at[k:k + 1, :], jnp.sum(shifted * q, axis=0, keepdims=True), first)
        dq = ca_ref[2:3, :] * dcq + ca_ref[1:2, :] * s1 + ca_ref[0:1, :] * s2
        da_c = dq * a_v
        da_v = dq * a_c
        dps, dpool = [], []
        for g, w in enumerate(WINDOWS):
            cols = slice(g * LANE, (g + 1) * LANE)
            pooled = pooled_ref[:, cols]
            mixed = _nn(pooled, wp_ref[g])
            dps.append(jnp.sum(dy_b[:, cols] * mixed, axis=0, keepdims=True))
            dmixed = (dy_b[:, cols] * ps_ref[:, cols]).astype(BF16)
            _acc(dwp_ref.at[g], _tn(pooled, dmixed), first)
            dp = _nn(dmixed, wpt_ref[g])
            dpool.append(dp)
            ebuf[0:tm, cols] = dp / _counts((ni - 1 - i) * tm, tm, w)
        _acc(dps_ref, jnp.concatenate(dps, axis=1), first)
        eext = ebuf[...]
        dzs = []
        for g, w in enumerate(WINDOWS):
            cols = slice(g * LANE, (g + 1) * LANE)
            dzs.append(_window_sum(eext[:, cols], w, False)[0:tm, :] - dpool[g])
        dproj = jnp.concatenate([da_b, da_c, da_v] + dzs, axis=1).astype(BF16)
        dproj_ref[...] = dproj
        dx, dgain = _rms_bwd(_nt(dproj, win_ref[...]), x_ref[...], gmix_ref[...])
        dx_ref[...] = dh_ref[...] + dx
        _acc(dgain_ref, dgain, first)
        cbuf[tm:tm + HALO, :] = cbuf[0:HALO, :]
        ebuf[tm:tm + HALO, :] = ebuf[0:HALO, :]

    row = lambda c: pl.BlockSpec((tm, c), lambda i: (ni - 1 - i, 0))
    return _pcall(
        body, comm=comm, name="even_bwd", grid=(ni,),
        in_specs=[row(d), row(d), _whole(wout.shape), row(e), row(aw), row(aw), _whole(conva.shape), _whole(wpool.shape),
                  _whole(wpoolt.shape), _whole(pscale.shape), _whole(win.shape), row(d), _whole((1, d))],
        out_specs=[row(e), row(d), _whole((1, d)), _whole(conva.shape), _whole(wpool.shape), _whole(pscale.shape)],
        out_shape=[jax.ShapeDtypeStruct((s, e), BF16), jax.ShapeDtypeStruct((s, d), F32), jax.ShapeDtypeStruct((1, d), F32),
                   jax.ShapeDtypeStruct(conva.shape, F32), jax.ShapeDtypeStruct(wpool.shape, F32),
                   jax.ShapeDtypeStruct(pscale.shape, F32)],
        scratch_shapes=[pltpu.VMEM((tm + HALO, aw), F32), pltpu.VMEM((tm + HALO, aw), F32)],
        compiler_params=_params("arbitrary"),
    )(dhb, dh, wout, proj, cq, pooled, conva, wpool, wpoolt, pscale, win, x, gmix)


def _wgrad(a, b, tk, ts, name, comm=None):
    s, ka = a.shape
    nb = b.shape[1]
    nt = s // ts

    def body(a_ref, b_ref, o_ref, acc):
        t = pl.program_id(1)
        _acc(acc, _tn(a_ref[...], b_ref[...]), t == 0)

        @pl.when(t == nt - 1)
        def _():
            o_ref[...] = acc[...].astype(BF16)

    return _pcall(
        body, comm=comm, name=name, grid=(ka // tk, nt),
        in_specs=[pl.BlockSpec((ts, tk), lambda k, t: (t, k)),
                  _resident((s, nb)) if nt == 1 else pl.BlockSpec((ts, nb), lambda k, t: (t, 0))],
        out_specs=pl.BlockSpec((tk, nb), lambda k, t: (k, 0)),
        out_shape=jax.ShapeDtypeStruct((ka, nb), BF16),
        scratch_shapes=[pltpu.VMEM((tk, nb), F32)],
        compiler_params=_params("arbitrary", "arbitrary"),
    )(a, b)


def _adamw(parts, w, m, v, tr, name, layer=None, into=None):
    r, c = w.shape[-2:]
    n, rp, cp = parts.shape
    assert rp >= r and r % tr == 0

    def body(p_ref, w_ref, m_ref, v_ref, *rest):
        g_ref, d_ref, mo_ref, vo_ref = rest[-4:]
        g = p_ref[0, 0:tr, 0:c].astype(F32)
        for j in range(1, n):
            g = g + p_ref[j, 0:tr, 0:c].astype(F32)
        g_ref[...] = g
        mn = ADAM_B1 * m_ref[...] + (1.0 - ADAM_B1) * g
        vn = ADAM_B2 * v_ref[...] + (1.0 - ADAM_B2) * (g * g)
        mo_ref[...] = mn
        vo_ref[...] = vn
        m_hat = mn / (1.0 - ADAM_B1 ** ADAM_STEP)
        v_hat = vn / (1.0 - ADAM_B2 ** ADAM_STEP)
        d_ref[...] = -ADAM_LR * (m_hat / (jnp.sqrt(v_hat) + ADAM_EPS) + ADAM_WD * w_ref[...])

    if layer is None:
        row = pl.BlockSpec((tr, c), lambda i: (i, 0))
    else:
        row = pl.BlockSpec((None, tr, c), lambda i: (layer, i, 0))
    out = jax.ShapeDtypeStruct(w.shape, F32)
    prev = [] if into is None else list(into)
    return _pcall(
        body, name=name, grid=(r // tr,),
        in_specs=[pl.BlockSpec((n, tr, cp), lambda i: (0, i, 0)), row, row, row] + [pl.BlockSpec(memory_space=pl.ANY)] * len(prev),
        out_specs=[row, row, row, row], out_shape=[out, out, out, out],
        input_output_aliases={4 + k: k for k in range(len(prev))},
        compiler_params=_params("arbitrary"),
    )(parts, w, m, v, *prev)


def _pair_add(grad, other, spec, core, name):
    axis, width = spec
    slot = other.shape[1:]

    def body(core_ref, g_ref, o_ref, out_ref):
        out_ref[...] = (g_ref[...].astype(F32) + o_ref[...].astype(F32)).astype(BF16)

    if axis == 0:
        gspec = pl.BlockSpec(slot, lambda q, core_ref: (2 * q + core_ref[0], 0))
    else:
        gspec = pl.BlockSpec(slot, lambda q, core_ref: (0, 2 * q + core_ref[0]))
    per_chip = pl.BlockSpec((None,) + slot, lambda q, core_ref: (q, 0, 0))
    return _pcall(
        body, name=name,
        grid_spec=pltpu.PrefetchScalarGridSpec(num_scalar_prefetch=1, grid=(N_CHIP,), in_specs=[gspec, per_chip], out_specs=per_chip),
        out_shape=jax.ShapeDtypeStruct(other.shape, BF16),
        compiler_params=_params("arbitrary"),
    )(core, grad, other)


def _hbm():
    return pl.BlockSpec(memory_space=pltpu.HBM)


def _window(ref, spec, j):
    axis, width = spec
    start = pl.multiple_of(j * width, width)
    return ref.at[(slice(None),) * axis + (pl.ds(start, width),)]


def _here():
    return lax.axis_index("x"), lax.axis_index("y"), lax.axis_index("c")


class _Gather:
    def __init__(self, shards, specs, fulls):
        n = len(shards)
        self.ins, self.specs, self.out_shape = list(shards), list(specs), list(fulls)
        self.sems = [pltpu.SemaphoreType.DMA((7 * n,)), pltpu.SemaphoreType.DMA((7 * n,)), pltpu.SemaphoreType.DMA((n,))]

    def _plan(self, ins, outs, sems):
        send_sems, recv_sems, local_sems = sems
        x, y, c = _here()
        me, sibling = (x, y, c), (x, y, 1 - c)
        chips = [(1 - x, y), (x, 1 - y), (1 - x, 1 - y)]

        def slot(t, dev):
            return _window(outs[t], self.specs[t], 4 * dev[0] + 2 * dev[1] + dev[2])

        def copy(t, k, block, to, src=None):
            return pltpu.make_async_remote_copy(
                src_ref=slot(t, block) if src is None else src, dst_ref=slot(t, block),
                send_sem=send_sems.at[7 * t + k], recv_sem=recv_sems.at[7 * t + k], device_id=to, device_id_type=MESH)

        plan = []
        for t in range(len(ins)):
            plan.append(dict(
                mine=pltpu.make_async_copy(ins[t], slot(t, me), local_sems.at[t]),
                first=[copy(t, 0, me, sibling, src=ins[t])] + [copy(t, 1 + j, me, (*q, c), src=ins[t]) for j, q in enumerate(chips)],
                over_ici=[copy(t, 1 + j, (*q, c), me) for j, q in enumerate(chips)],
                passed=[copy(t, 4 + j, (*q, c), sibling) for j, q in enumerate(chips)],
                from_sibling=[copy(t, 0, sibling, me)] + [copy(t, 4 + j, (*q, 1 - c), me) for j, q in enumerate(chips)]))
        return plan

    def start(self, ins, outs, sems):
        for p in self._plan(ins, outs, sems):
            p["mine"].start()
            for cp in p["first"]:
                cp.start()

    def middle(self, ins, outs, sems):
        for p in self._plan(ins, outs, sems):
            for arrived, onward in zip(p["over_ici"], p["passed"]):
                arrived.wait_recv()
                onward.start()

    def finish(self, ins, outs, sems):
        plan = self._plan(ins, outs, sems)
        for p in plan:
            for cp in p["from_sibling"]:
                cp.wait_recv()
        for p in plan:
            for cp in p["first"] + p["passed"]:
                cp.wait_send()
            p["mine"].wait()


class _PairExchange:
    def __init__(self, grads, specs):
        n = len(grads)
        self.ins, self.specs = list(grads), list(specs)
        self.out_shape = [jax.ShapeDtypeStruct((N_CHIP,) + a.shape[:sp[0]] + (sp[1],) + a.shape[sp[0] + 1:], a.dtype)
                          for a, sp in zip(grads, specs)]
        self.sems = [pltpu.SemaphoreType.DMA((n,)), pltpu.SemaphoreType.DMA((n,))]

    def start(self, ins, outs, sems):
        send_sems, recv_sems = sems
        x, y, c = _here()
        for t in range(len(ins)):
            for q in range(N_CHIP):
                pltpu.make_async_remote_copy(
                    src_ref=_window(ins[t], self.specs[t], 2 * q + (1 - c)), dst_ref=outs[t].at[q],
                    send_sem=send_sems.at[t], recv_sem=recv_sems.at[t], device_id=(x, y, 1 - c), device_id_type=MESH).start()

    def middle(self, ins, outs, sems):
        pass

    def finish(self, ins, outs, sems):
        send_sems, recv_sems = sems
        x, y, c = _here()
        for t in range(len(ins)):
            every = pltpu.make_async_remote_copy(src_ref=outs[t], dst_ref=outs[t], send_sem=send_sems.at[t],
                                                 recv_sem=recv_sems.at[t], device_id=(x, y, 1 - c), device_id_type=MESH)
            every.wait_send()
            every.wait_recv()


class _ChipExchange:
    def __init__(self, pairs, slotted=(), whole=()):
        self.ins = list(pairs) + list(slotted) + list(whole)
        self.npair, self.nslot = len(pairs), len(pairs) + len(slotted)
        n = len(self.ins)
        self.out_shape = ([jax.ShapeDtypeStruct(a.shape, a.dtype) for a in list(pairs) + list(slotted)]
                          + [jax.ShapeDtypeStruct((N_DEV,) + a.shape, a.dtype) for a in whole])
        self.sems = [pltpu.SemaphoreType.DMA((7 * n,)), pltpu.SemaphoreType.DMA((7 * n,)), pltpu.SemaphoreType.DMA((n,))]

    def _plan(self, ins, outs, sems):
        send_sems, recv_sems, local_sems = sems
        npair, nslot = self.npair, self.nslot
        x, y, c = _here()
        me, chip = 4 * x + 2 * y + c, 2 * x + y
        chips = [(1 - x, y), (x, 1 - y), (1 - x, 1 - y)]
        peers = [(x, y, 1 - c)] + [(*q, c) for q in chips] + [(*q, 1 - c) for q in chips]

        def index(dev):
            return 4 * dev[0] + 2 * dev[1] + dev[2]

        def copy(t, k, arriving):
            peer = peers[k]
            if t < npair:
                src, mine, theirs = ins[t].at[2 * peer[0] + peer[1]], chip, 2 * peer[0] + peer[1]
            else:
                src, mine, theirs = (ins[t].at[index(peer)] if t < nslot else ins[t]), me, index(peer)
            return pltpu.make_async_remote_copy(
                src_ref=src, dst_ref=outs[t].at[theirs if arriving else mine],
                send_sem=send_sems.at[7 * t + k], recv_sem=recv_sems.at[7 * t + k], device_id=peer, device_id_type=MESH)

        own, sent, arriving = [], [], []
        for t in range(len(ins)):
            fan = range(1, 4) if t < npair else range(7)
            if t < npair:
                own.append(pltpu.make_async_copy(ins[t].at[chip], outs[t].at[chip], local_sems.at[t]))
            else:
                own.append(pltpu.make_async_copy(ins[t].at[me] if t < nslot else ins[t], outs[t].at[me], local_sems.at[t]))
            sent += [copy(t, k, False) for k in fan]
            arriving += [copy(t, k, True) for k in fan]
        return own, sent, arriving

    def start(self, ins, outs, sems):
        own, sent, _ = self._plan(ins, outs, sems)
        for cp in own + sent:
            cp.start()

    def middle(self, ins, outs, sems):
        pass

    def finish(self, ins, outs, sems):
        own, sent, arriving = self._plan(ins, outs, sems)
        for cp in arriving:
            cp.wait_recv()
        for cp in sent:
            cp.wait_send()
        for cp in own:
            cp.wait()


class _Jobs:
    def __init__(self, *jobs):
        self.jobs = jobs
        self.ins = [a for j in jobs for a in j.ins]
        self.out_shape = [a for j in jobs for a in j.out_shape]
        self.sems = [a for j in jobs for a in j.sems]

    def _split(self, ins, outs, sems):
        i = o = s = 0
        for j in self.jobs:
            yield j, ins[i:i + len(j.ins)], outs[o:o + len(j.out_shape)], sems[s:s + len(j.sems)]
            i, o, s = i + len(j.ins), o + len(j.out_shape), s + len(j.sems)

    def start(self, ins, outs, sems):
        for j, a, b, c in self._split(ins, outs, sems):
            j.start(a, b, c)

    def middle(self, ins, outs, sems):
        for j, a, b, c in self._split(ins, outs, sems):
            j.middle(a, b, c)

    def finish(self, ins, outs, sems):
        for j, a, b, c in self._split(ins, outs, sems):
            j.finish(a, b, c)

    def results(self, outs):
        return [b for _, _, b, _ in self._split((), outs, ())]


def _alone(job, name):
    return _pcall(lambda: None, comm=job, name=name, in_specs=[], out_specs=[], out_shape=[])()[1]


SMALL_ROWS = 24
REP_COLS = 1024


def _pad_to(a, rows, cols):
    return jnp.pad(a, ((0, rows - a.shape[0]), (0, cols - a.shape[1])))


def _pack_small(conv_a, sgu_norm, conv_ffn, cols):
    return jnp.concatenate([_pad_to(conv_a, 8, cols), _pad_to(sgu_norm, 8, cols),
                            _pad_to(conv_ffn.reshape(-1, conv_ffn.shape[-1]), 8, cols)], axis=0)


def _unpack_small(p, ca_w, sg_w, cf_w):
    return p[0:3, 0:ca_w], p[8:9, 0:sg_w], p[16:22, 0:cf_w].reshape(2, 3, cf_w)


def _tile_rows(rows):
    return -(-rows // 8) * 8


def _pack_rows(parts):
    return jnp.concatenate([_pad_to(a, _tile_rows(a.shape[0]), REP_COLS) for a in parts], axis=0)


def _unpack_rows(p, shapes):
    out, r0 = [], 0
    for r, c in shapes:
        out.append(p[r0:r0 + r, 0:c])
        r0 += _tile_rows(r)
    return out


def _rep_late(norm_mix0, norm_ffn0, pool_scale, b_conv0, w_pool):
    return _pack_rows([norm_mix0, norm_ffn0, pool_scale.reshape(1, -1), _pad_to(b_conv0, 1, 3 * REP_COLS).reshape(3, REP_COLS),
                       w_pool.reshape(-1, REP_COLS)])


def _rep_early(norm_mix1, norm_ffn1, final_norm, b_spatial, b_conv1, loss):
    return _pack_rows([norm_mix1, norm_ffn1, final_norm.reshape(1, -1), b_spatial.reshape(1, -1),
                       _pad_to(b_conv1, 1, 3 * REP_COLS).reshape(3, REP_COLS), loss])


def _unpack_rep(late, early, wsp, like):
    f = like["b_conv_ffn"].shape[1]
    nm0, nf0, ps, bc0, wp = _unpack_rows(late, [(1, REP_COLS), (1, REP_COLS), (1, like["pool_scale"].shape[1]), (3, REP_COLS),
                                                (like["w_pool"].size // REP_COLS, REP_COLS)])
    nm1, nf1, fin, bs, bc1, _ = _unpack_rows(early, [(1, REP_COLS)] * 4 + [(3, REP_COLS), (1, REP_COLS)])
    return {
        "norm_mix": jnp.concatenate([nm0, nm1]), "norm_ffn": jnp.concatenate([nf0, nf1]), "final_norm": fin[0], "pool_scale": ps,
        "b_spatial": bs.reshape(like["b_spatial"].shape),
        "b_conv_ffn": jnp.concatenate([bc0.reshape(1, -1), bc1.reshape(1, -1)])[:, 0:f],
        "w_pool": wp.reshape(like["w_pool"].shape), "w_spatial": wsp.reshape(like["w_spatial"].shape),
    }


def _pad_slots(a, width, padded):
    a = a.reshape(*a.shape[:-1], N_DEV, width)
    a = jnp.pad(a, ((0, 0),) * (a.ndim - 1) + ((0, padded - width),))
    return a.reshape(*a.shape[:-2], N_DEV * padded)


def _unpad_slots(a, width, padded):
    a = a.reshape(*a.shape[:-1], N_DEV, padded)[..., 0:width]
    return a.reshape(*a.shape[:-2], N_DEV * width)


def kernel(x, norm_mix, norm_ffn, final_norm, w_in_even, conv_a, w_pool, pool_scale, w_out_even, w_in_odd, sgu_norm, w_spatial, b_spatial, w_out_odd, w_ffn_gate, w_ffn_up, conv_ffn, b_conv_ffn, w_ffn_down, loss_target, m_norm_mix, m_norm_ffn, m_final_norm, m_w_in_even, m_conv_a, m_w_pool, m_pool_scale, m_w_out_even, m_w_in_odd, m_sgu_norm, m_w_spatial, m_b_spatial, m_w_out_odd, m_w_ffn_gate, m_w_ffn_up, m_conv_ffn, m_b_conv_ffn, m_w_ffn_down, v_norm_mix, v_norm_ffn, v_final_norm, v_w_in_even, v_conv_a, v_w_pool, v_pool_scale, v_w_out_even, v_w_in_odd, v_sgu_norm, v_w_spatial, v_b_spatial, v_w_out_odd, v_w_ffn_gate, v_w_ffn_up, v_conv_ffn, v_b_conv_ffn, v_w_ffn_down):
    s, d = x.shape[1], x.shape[2]
    x2, target = x[0], loss_target[0]
    tm = min(512, s)
    tm_wide = min(2048, s)
    tn = 256
    tk_wide = 1024
    tn_fwd = 512
    row = lambda a: a.reshape(1, -1)
    ein, ro = w_in_even.shape[2], w_out_even.shape[1]
    fs = w_ffn_gate.shape[2]
    fsp = -(-fs // LANE) * LANE
    fp = N_DEV * fsp
    full = lambda shape, dtype=BF16: jax.ShapeDtypeStruct(shape, dtype)

    wg_s = [jnp.pad(w_ffn_gate[l].T, ((0, fsp - fs), (0, 0))).astype(BF16) for l in range(2)]
    wu_s = [jnp.pad(w_ffn_up[l].T, ((0, fsp - fs), (0, 0))).astype(BF16) for l in range(2)]
    wd_s = [jnp.pad(w_ffn_down[l], ((0, fsp - fs), (0, 0))).astype(BF16) for l in range(2)]
    small_s = _pack_small(conv_a[0], sgu_norm, conv_ffn, fsp)[None]
    in_spec, out_spec, gu_spec, down_spec = (1, ein), (0, ro), (0, fsp), (0, fsp)
    full_in, full_out, full_gu, full_down = full((d, N_DEV * ein)), full((N_DEV * ro, d)), full((fp, d)), full((fp, d))
    win_e, wout_e, gsmall = _alone(_Gather([w_in_even[0].astype(BF16), w_out_even[0].astype(BF16), small_s], [in_spec, out_spec, (0, 1)],
                                           [full_in, full_out, full((N_DEV, SMALL_ROWS, fsp), F32)]), "gather_mix0")
    ca_full = jnp.moveaxis(gsmall[:, 0:3, 0:conv_a.shape[2]], 0, 1).reshape(3, -1)
    sgu_full = gsmall[:, 8, 0:sgu_norm.shape[1]].reshape(1, -1)
    cf_full = jnp.moveaxis(gsmall[:, 16:22, :].reshape(N_DEV, 2, 3, fsp), 0, 2).reshape(2, 3, fp)
    cb_full = _pad_slots(b_conv_ffn, fs, fsp)

    tril = jnp.tril(jnp.ones((CHUNK, CHUNK), F32))
    ws_m = w_spatial[0] * tril
    ws_b = ws_m.astype(BF16)
    wst_b = jnp.swapaxes(ws_m, 1, 2).astype(BF16)
    bfull = jnp.repeat(b_spatial[0].T, CHUNK, axis=1)
    wpool_b = w_pool[0].astype(BF16)
    wpoolt_b = jnp.swapaxes(w_pool[0], 1, 2).astype(BF16)

    (xn0, proj0, cq0, pooled0, mix0, h1, hn0), (wg0,) = _even_fwd(
        x2, norm_mix[0:1], norm_ffn[0:1], win_e, ca_full, wpool_b, pool_scale, wout_e, tm,
        comm=_Gather([wg_s[0]], [gu_spec], [full_gu]))
    (g0, gc0), (wu0,) = _ffn_gate(hn0, wg0, cf_full[0], cb_full[0:1], tm_wide, tn_fwd, "ffn_gate_l0", comm=_Gather([wu_s[0]], [gu_spec], [full_gu]))
    (up0, a0, sl0, ds0), (wd0,) = _ffn_up(hn0, wu0, gc0, tm_wide, tn_fwd, "ffn_up_l0", comm=_Gather([wd_s[0]], [down_spec], [full_down]))
    (h2, xn1), (win_o, wout_o) = _ffn_fwd2(a0, wd0, h1, norm_mix[1:2], tm, "ffn_fwd2_l0", comm=_Gather(
        [w_in_odd[0].astype(BF16), w_out_odd[0].astype(BF16)], [in_spec, out_spec], [full_in, full_out]))
    (pre1, gate1, mixo1, h3, hn1), (wg1,) = _odd_fwd(xn1, h2, win_o, sgu_full, ws_b, bfull, wout_o, norm_ffn[1:2], tm,
                                                    comm=_Gather([wg_s[1]], [gu_spec], [full_gu]))
    (g1, gc1), (wu1,) = _ffn_gate(hn1, wg1, cf_full[1], cb_full[1:2], tm_wide, tn_fwd, "ffn_gate_l1", comm=_Gather([wu_s[1]], [gu_spec], [full_gu]))
    (up1, a1, sl1, ds1), (wd1,) = _ffn_up(hn1, wu1, gc1, tm_wide, tn_fwd, "ffn_up_l1", comm=_Gather([wd_s[1]], [down_spec], [full_down]))
    h4, _ = _ffn_fwd2(a1, wd1, h3, row(final_norm), tm, "ffn_fwd2_l1")

    core = lax.axis_index("c").astype(jnp.int32).reshape(1)
    ts = min(1024, s)
    dh4, dh4b, d_final, lossvec = _loss_bwd(h4, row(final_norm), target, tm)
    gd1 = _wgrad(a1, dh4b, tk_wide, s, "wgrad_down_l1")
    (dg1, dup1, dcw1, dcb1), (o_d1,) = _ffn_bwd1(dh4b, g1, sl1, ds1, up1, wd1, cf_full[1], tm_wide, tn, "ffn_bwd1_l1",
                                                 comm=_PairExchange([gd1], [down_spec]))
    p_d1 = _pair_add(gd1, o_d1, down_spec, core, "pair_add_down_l1")
    gg1 = _wgrad(dg1, hn1, tk_wide, s, "wgrad_gate_l1")
    gu1, (o_g1,) = _wgrad(dup1, hn1, tk_wide, s, "wgrad_up_l1", comm=_PairExchange([gg1], [gu_spec]))
    p_g1 = _pair_add(gg1, o_g1, gu_spec, core, "pair_add_gate_l1")
    jobs = _Jobs(_ChipExchange([p_d1]), _PairExchange([gu1], [gu_spec]))
    (dh3, dh3b, d_nffn1), res = _ffn_bwd2(dg1, dup1, wg1, wu1, h3, norm_ffn[1:2], dh4, tm, "ffn_bwd2_l1", comm=jobs)
    (s_d1,), (o_u1,) = jobs.results(res)
    p_u1 = _pair_add(gu1, o_u1, gu_spec, core, "pair_add_up_l1")
    (dpre1, dh2, dh2b, d_nmix1, d_sgu, d_ws, d_b), (s_g1, s_u1) = _odd_bwd(
        dh3b, dh3, wout_o, pre1, gate1, ws_b, wst_b, sgu_full, win_o, h2, norm_mix[1:2], tm, comm=_ChipExchange([p_g1, p_u1]))
    gi1 = _wgrad(xn1, dpre1, tk_wide, ts, "wgrad_in_odd")
    go1, (o_i1,) = _wgrad(mixo1, dh3b, tk_wide, ts, "wgrad_out_odd", comm=_PairExchange([gi1], [in_spec]))
    p_i1 = _pair_add(gi1, o_i1, in_spec, core, "pair_add_in_odd")
    d_wsp = (d_ws * tril).reshape(-1, REP_COLS).astype(BF16)
    jobs = _Jobs(_PairExchange([go1], [out_spec]), _ChipExchange([], [], [d_wsp]))
    gd0, res = _wgrad(a0, dh2b, tk_wide, s, "wgrad_down_l0", comm=jobs)
    (o_o1,), (r_wsp,) = jobs.results(res)
    p_o1 = _pair_add(go1, o_o1, out_spec, core, "pair_add_out_odd")
    d_early = _rep_early(d_nmix1, d_nffn1, d_final, d_b, _unpad_slots(dcb1, fs, fsp), lossvec)
    jobs = _Jobs(_ChipExchange([p_i1, p_o1], [], [d_early]), _PairExchange([gd0], [down_spec]))
    (dg0, dup0, dcw0, dcb0), res = _ffn_bwd1(dh2b, g0, sl0, ds0, up0, wd0, cf_full[0], tm_wide, tn, "ffn_bwd1_l0", comm=jobs)
    (s_i1, s_o1, r_early), (o_d0,) = jobs.results(res)
    p_d0 = _pair_add(gd0, o_d0, down_spec, core, "pair_add_down_l0")
    gg0, (s_d0,) = _wgrad(dg0, hn0, tk_wide, s, "wgrad_gate_l0", comm=_ChipExchange([p_d0]))
    gu0, (o_g0,) = _wgrad(dup0, hn0, tk_wide, s, "wgrad_up_l0", comm=_PairExchange([gg0], [gu_spec]))
    p_g0 = _pair_add(gg0, o_g0, gu_spec, core, "pair_add_gate_l0")
    jobs = _Jobs(_ChipExchange([p_g0]), _PairExchange([gu0], [gu_spec]))
    (dh1, dh1b, d_nffn0), res = _ffn_bwd2(dg0, dup0, wg0, wu0, h1, norm_ffn[0:1], dh2, tm, "ffn_bwd2_l0", comm=jobs)
    (s_g0,), (o_u0,) = jobs.results(res)
    p_u0 = _pair_add(gu0, o_u0, gu_spec, core, "pair_add_up_l0")
    go0 = _wgrad(mix0, dh1b, tk_wide, ts, "wgrad_out_even")
    jobs = _Jobs(_ChipExchange([p_u0]), _PairExchange([go0], [out_spec]))
    (dproj0, grad_x, d_nmix0, d_ca, d_wp, d_ps), res = _even_bwd(
        dh1b, dh1, wout_e, proj0, cq0, pooled0, ca_full, wpool_b, wpoolt_b, pool_scale, win_e, x2, norm_mix[0:1], tm, comm=jobs)
    (s_u0,), (o_o0,) = jobs.results(res)
    p_o0 = _pair_add(go0, o_o0, out_spec, core, "pair_add_out_even")
    d_small = jnp.stack([_pack_small(a, b, c, fsp) for a, b, c in zip(
        jnp.moveaxis(d_ca.reshape(3, N_DEV, -1), 1, 0), jnp.moveaxis(d_sgu.reshape(1, N_DEV, -1), 1, 0),
        jnp.moveaxis(jnp.stack([dcw0, dcw1]).reshape(2, 3, N_DEV, fsp), 2, 0))])
    d_late = _rep_late(d_nmix0, d_nffn0, d_ps, _unpad_slots(dcb0, fs, fsp), d_wp)
    gi0, (r_small, r_late) = _wgrad(xn0, dproj0, tk_wide, ts, "wgrad_in_even", comm=_ChipExchange([], [d_small], [d_late]))
    jobs = _Jobs(_PairExchange([gi0], [in_spec]), _ChipExchange([p_o0]))
    (o_i0,), (s_o0,) = jobs.results(_alone(jobs, "pair_exchange_in_even"))
    p_i0 = _pair_add(gi0, o_i0, in_spec, core, "pair_add_in_even")
    (s_i0,) = _alone(_ChipExchange([p_i0]), "chip_exchange_last")
    loss = jnp.sum(r_early[:, r_early.shape[1] - 8, :])

    out = {}
    out["w_in_even"] = _adamw(s_i0, w_in_even[0], m_w_in_even[0], v_w_in_even[0], 256, "adamw_in_even")
    out["w_out_even"] = _adamw(s_o0, w_out_even[0], m_w_out_even[0], v_w_out_even[0], ro // 2, "adamw_out_even")
    out["w_in_odd"] = _adamw(s_i1, w_in_odd[0], m_w_in_odd[0], v_w_in_odd[0], 256, "adamw_in_odd")
    out["w_out_odd"] = _adamw(s_o1, w_out_odd[0], m_w_out_odd[0], v_w_out_odd[0], ro // 2, "adamw_out_odd")
    tp = lambda a: jnp.swapaxes(a, 1, 2)
    for nm, s1, s0, w, m, v, back in (("w_ffn_gate", s_g1, s_g0, tp(w_ffn_gate), tp(m_w_ffn_gate), tp(v_w_ffn_gate), tp),
                                      ("w_ffn_up", s_u1, s_u0, tp(w_ffn_up), tp(m_w_ffn_up), tp(v_w_ffn_up), tp),
                                      ("w_ffn_down", s_d1, s_d0, w_ffn_down, m_w_ffn_down, v_w_ffn_down, lambda a: a)):
        l1 = _adamw(s1, w, m, v, fs // 2, "adamw_%s_l1" % nm, layer=1)
        out[nm] = [back(a) for a in _adamw(s0, w, m, v, fs // 2, "adamw_%s_l0" % nm, layer=0, into=l1)]
    small = _adamw(r_small, _pack_small(conv_a[0], sgu_norm, conv_ffn, fsp), _pack_small(m_conv_a[0], m_sgu_norm, m_conv_ffn, fsp),
                   _pack_small(v_conv_a[0], v_sgu_norm, v_conv_ffn, fsp), SMALL_ROWS, "adamw_small")
    no_loss = jnp.zeros((1, REP_COLS), F32)
    early = _adamw(r_early, *[_rep_early(nm[1:2], nf[1:2], fn, bs, bc[1:2], no_loss) for nm, nf, fn, bs, bc in (
        (norm_mix, norm_ffn, final_norm, b_spatial, b_conv_ffn), (m_norm_mix, m_norm_ffn, m_final_norm, m_b_spatial, m_b_conv_ffn),
        (v_norm_mix, v_norm_ffn, v_final_norm, v_b_spatial, v_b_conv_ffn))], r_early.shape[1], "adamw_replicated_early")
    wsp = _adamw(r_wsp, w_spatial.reshape(-1, REP_COLS), m_w_spatial.reshape(-1, REP_COLS), v_w_spatial.reshape(-1, REP_COLS),
                 r_wsp.shape[1], "adamw_w_spatial")
    late = _adamw(r_late, *[_rep_late(nm[0:1], nf[0:1], ps, bc[0:1], wp) for nm, nf, ps, bc, wp in (
        (norm_mix, norm_ffn, pool_scale, b_conv_ffn, w_pool), (m_norm_mix, m_norm_ffn, m_pool_scale, m_b_conv_ffn, m_w_pool),
        (v_norm_mix, v_norm_ffn, v_pool_scale, v_b_conv_ffn, v_w_pool))], r_late.shape[1], "adamw_replicated_late")

    names = ["norm_mix", "norm_ffn", "final_norm", "w_in_even", "conv_a", "w_pool", "pool_scale", "w_out_even", "w_in_odd", "sgu_norm",
             "w_spatial", "b_spatial", "w_out_odd", "w_ffn_gate", "w_ffn_up", "conv_ffn", "b_conv_ffn", "w_ffn_down"]
    like = {"norm_mix": norm_mix, "norm_ffn": norm_ffn, "final_norm": final_norm, "w_in_even": w_in_even, "conv_a": conv_a,
            "w_pool": w_pool, "pool_scale": pool_scale, "w_out_even": w_out_even, "w_in_odd": w_in_odd, "sgu_norm": sgu_norm,
            "w_spatial": w_spatial, "b_spatial": b_spatial, "w_out_odd": w_out_odd, "w_ffn_gate": w_ffn_gate, "w_ffn_up": w_ffn_up,
            "conv_ffn": conv_ffn, "b_conv_ffn": b_conv_ffn, "w_ffn_down": w_ffn_down}
    groups = []
    for k in range(4):
        ca_k, sg_k, cf_k = _unpack_small(small[k], conv_a.shape[2], sgu_norm.shape[1], conv_ffn.shape[2])
        vals = dict(_unpack_rep(late[k], early[k], wsp[k], like))
        vals.update(conv_a=ca_k, sgu_norm=sg_k, conv_ffn=cf_k)
        for nm in ("w_in_even", "w_in_odd", "w_out_even", "w_out_odd", "w_ffn_gate", "w_ffn_up", "w_ffn_down"):
            vals[nm] = out[nm][k]
        groups.append([vals[nm].reshape(like[nm].shape) for nm in names])
    return (loss, grad_x[None], *groups[0], *groups[1], *groups[2], *groups[3])
```

```python
import functools

import jax
import jax.numpy as jnp
from jax import lax
from jax.experimental import pallas as pl
from jax.experimental.pallas import tpu as pltpu

F32, BF16 = jnp.float32, jnp.bfloat16
EPS = 1e-6
WINDOWS = (2, 4, 8, 16)
HALO = 16
CHUNK = 128
N_DEV = 8
N_CHIP = 4
MESH = pl.DeviceIdType.MESH
VMEM_LIMIT = 56 * 2**20
LANE = 128
ADAM_LR, ADAM_B1, ADAM_B2, ADAM_EPS, ADAM_WD, ADAM_STEP = 0.001, 0.9, 0.999, 1e-08, 0.01, 10
SUB_ROWS = 256
LATE_NUM, LATE_DEN = 7, 8
INV_SQRT2 = 0.7071067811865476
INV_SQRT2PI = 0.3989422804014327


def _pcall(body, comm=None, **kw):
    if comm is None:
        return pl.pallas_call(body, **kw)
    in_specs, out_specs, out_shape = list(kw.pop("in_specs")), kw.pop("out_specs"), kw.pop("out_shape")
    single = not isinstance(out_shape, (list, tuple))
    out_specs, out_shape = ([out_specs], [out_shape]) if single else (list(out_specs), list(out_shape))
    scratch = list(kw.pop("scratch_shapes", []))
    grid = kw.get("grid", ())
    n_in, n_out, n_scr, c_in, c_out = len(in_specs), len(out_specs), len(scratch), len(comm.ins), len(comm.out_shape)

    def hosted(*refs):
        cuts = [0, n_in, n_in + c_in, n_in + c_in + n_out, n_in + c_in + n_out + c_out, n_in + c_in + n_out + c_out + n_scr, len(refs)]
        ins, cins, outs, couts, scr, sems = (refs[a:b] for a, b in zip(cuts[:-1], cuts[1:]))
        if grid:
            step, steps = 0, 1
            for axis, size in enumerate(grid):
                step, steps = step * size + pl.program_id(axis), steps * size
            pl.when(step == 0)(lambda: comm.start(cins, couts, sems))
            pl.when(step == (steps * LATE_NUM) // LATE_DEN)(lambda: comm.middle(cins, couts, sems))
            body(*ins, *outs, *scr)
            pl.when(step == steps - 1)(lambda: comm.finish(cins, couts, sems))
        else:
            comm.start(cins, couts, sems)
            comm.middle(cins, couts, sems)
            body(*ins, *outs, *scr)
            comm.finish(cins, couts, sems)

    call = pl.pallas_call(hosted, in_specs=in_specs + [_hbm()] * c_in, out_specs=out_specs + [_hbm()] * c_out,
                          out_shape=out_shape + list(comm.out_shape), scratch_shapes=scratch + list(comm.sems), **kw)

    def run(*args):
        res = call(*args, *comm.ins)
        own = res[0] if single else res[:n_out]
        return own, res[n_out:]

    return run


def _params(*sem):
    return pltpu.CompilerParams(dimension_semantics=sem, vmem_limit_bytes=VMEM_LIMIT)


def _whole(shape):
    return pl.BlockSpec(shape, lambda *_: (0,) * len(shape))


def _resident(shape):
    return pl.BlockSpec(shape, lambda *_: (0,) * len(shape), pipeline_mode=pl.Buffered(1))


def _rows(i, tm):
    return pl.ds(pl.multiple_of(i * tm, tm), tm)


def _nn(a, b):
    return jnp.dot(a, b, preferred_element_type=F32)


def _nt(a, b):
    return lax.dot_general(a, b, (((1,), (1,)), ((), ())), preferred_element_type=F32)


def _tn(a, b):
    return lax.dot_general(a, b, (((0,), (0,)), ((), ())), preferred_element_type=F32)


def _rms(x, gain):
    r = lax.rsqrt(jnp.mean(x * x, axis=-1, keepdims=True) + EPS)
    return x * r * gain


def _rms_bwd(dy, x, gain):
    r = lax.rsqrt(jnp.mean(x * x, axis=-1, keepdims=True) + EPS)
    xh = x * r
    dgain = jnp.sum(dy * xh, axis=0, keepdims=True)
    dxh = dy * gain
    dx = r * (dxh - xh * jnp.mean(dxh * xh, axis=-1, keepdims=True))
    return dx, dgain


def _gelu(x):
    return 0.5 * x * (1.0 + lax.erf(x * INV_SQRT2))


def _gelu_grad(x):
    return 0.5 * (1.0 + lax.erf(x * INV_SQRT2)) + x * jnp.exp(-0.5 * x * x) * INV_SQRT2PI


def _acc(ref, val, first):
    @pl.when(first)
    def _():
        ref[...] = val

    @pl.when(jnp.logical_not(first))
    def _():
        ref[...] += val


def _shift(ext, k, back):
    n = ext.shape[0]
    return pltpu.roll(ext, k if back else n - k, axis=0)


def _window_sum(ext, w, back):
    total, step = ext, 1
    while step < w:
        total = total + _shift(total, step, back)
        step *= 2
    return total


def _counts(row0, tm, w):
    pos1 = (row0 + lax.broadcasted_iota(jnp.int32, (tm, 1), 0) + 1).astype(F32)
    return jnp.minimum(pos1, float(w))


def _even_fwd(x, gmix, gffn, win, conva, wpool, pscale, wout, tm, comm=None):
    s, d = x.shape
    e = win.shape[1]
    aw = e // 4

    def body(x_ref, gmix_ref, gffn_ref, win_ref, ca_ref, wp_ref, ps_ref, wout_ref,
             xn_ref, proj_ref, cq_ref, pooled_ref, mix_ref, h_ref, hn_ref, qbuf, zbuf):
        i = pl.program_id(0)

        @pl.when(i == 0)
        def _():
            qbuf[0:HALO, :] = jnp.zeros((HALO, aw), F32)
            zbuf[0:HALO, :] = jnp.zeros((HALO, aw), F32)

        xv = x_ref[...]
        xn = _rms(xv, gmix_ref[...]).astype(BF16)
        xn_ref[...] = xn
        proj = _nn(xn, win_ref[...])
        proj_ref[...] = proj.astype(BF16)
        a_b, a_c, a_v, z = (proj[:, k * aw:(k + 1) * aw] for k in range(4))
        q = a_c * a_v
        qbuf[HALO:HALO + tm, :] = q
        qext = qbuf[...]
        cur = slice(HALO, HALO + tm)
        cq = ca_ref[2:3, :] * q + ca_ref[1:2, :] * _shift(qext, 1, True)[cur, :] + ca_ref[0:1, :] * _shift(qext, 2, True)[cur, :]
        cq_ref[...] = cq.astype(BF16)
        y_a = a_b * cq
        zbuf[HALO:HALO + tm, :] = z
        zext = zbuf[...]
        ys = []
        for g, w in enumerate(WINDOWS):
            cols = slice(g * LANE, (g + 1) * LANE)
            acc = _window_sum(zext[:, cols], w, True)[cur, :]
            pooled = (acc / _counts(i * tm, tm, w) - z[:, cols]).astype(BF16)
            pooled_ref[:, cols] = pooled
            ys.append(_nn(pooled, wp_ref[g]))
        y_b = jnp.concatenate(ys, axis=1) * ps_ref[...]
        mix = jnp.concatenate([y_a, y_b], axis=1).astype(BF16)
        mix_ref[...] = mix
        h = xv + _nn(mix, wout_ref[...])
        h_ref[...] = h
        hn_ref[...] = _rms(h, gffn_ref[...]).astype(BF16)
        qbuf[0:HALO, :] = qbuf[tm:tm + HALO, :]
        zbuf[0:HALO, :] = zbuf[tm:tm + HALO, :]

    row = lambda c: pl.BlockSpec((tm, c), lambda i: (i, 0))
    return _pcall(
        body, comm=comm, name="even_fwd", grid=(s // tm,),
        in_specs=[row(d), _whole((1, d)), _whole((1, d)), _whole(win.shape), _whole(conva.shape), _whole(wpool.shape),
                  _whole(pscale.shape), _whole(wout.shape)],
        out_specs=[row(d), row(e), row(aw), row(aw), row(d), row(d), row(d)],
        out_shape=[jax.ShapeDtypeStruct((s, d), BF16), jax.ShapeDtypeStruct((s, e), BF16), jax.ShapeDtypeStruct((s, aw), BF16),
                   jax.ShapeDtypeStruct((s, aw), BF16), jax.ShapeDtypeStruct((s, d), BF16), jax.ShapeDtypeStruct((s, d), F32),
                   jax.ShapeDtypeStruct((s, d), BF16)],
        scratch_shapes=[pltpu.VMEM((tm + HALO, aw), F32), pltpu.VMEM((tm + HALO, aw), F32)],
        compiler_params=_params("arbitrary"),
    )(x, gmix, gffn, win, conva, wpool, pscale, wout)


def _ffn_gate(hn, wgt, cw, cb, tm, tn, name, comm=None):
    s, d = hn.shape
    f = wgt.shape[0]
    sub = tm

    def body(hn_ref, wg_ref, cw_ref, cb_ref, g_ref, gc_ref, gbuf):
        i = pl.program_id(1)

        @pl.when(i == 0)
        def _():
            gbuf[0:HALO, :] = jnp.zeros((HALO, tn), F32)

        wg = wg_ref[...]
        for c in range(tm // sub):
            rows = pl.ds(c * sub, sub)
            g = _nt(hn_ref[pl.ds(pl.multiple_of(i * tm + c * sub, sub), sub), :], wg)
            g_ref[rows, :] = g.astype(BF16)
            gbuf[pl.ds(HALO + c * sub, sub), :] = g
            ext = gbuf[pl.ds(c * sub, sub + HALO), :]
            gc = (cw_ref[2:3, :] * g + cw_ref[1:2, :] * _shift(ext, 1, True)[HALO:, :]
                  + cw_ref[0:1, :] * _shift(ext, 2, True)[HALO:, :] + cb_ref[...])
            gc_ref[rows, :] = gc.astype(BF16)
        gbuf[0:HALO, :] = gbuf[tm:tm + HALO, :]

    tile = pl.BlockSpec((tm, tn), lambda j, i: (i, j))
    wcol = lambda r: pl.BlockSpec((r, tn), lambda j, i: (0, j))
    out = jax.ShapeDtypeStruct((s, f), BF16)
    return _pcall(
        body, comm=comm, name=name, grid=(f // tn, s // tm),
        in_specs=[_resident((s, d)), pl.BlockSpec((tn, d), lambda j, i: (j, 0)), wcol(3), wcol(1)],
        out_specs=[tile, tile], out_shape=[out, out],
        scratch_shapes=[pltpu.VMEM((tm + HALO, tn), F32)],
        compiler_params=_params("arbitrary", "arbitrary"),
    )(hn, wgt, cw, cb)


def _ffn_up(hn, wut, gc, tm, tn, name, comm=None):
    s, d = hn.shape
    f = wut.shape[0]
    sub = min(SUB_ROWS, tm)

    def body(hn_ref, wu_ref, gc_ref, up_ref, a_ref, silu_ref, dsilu_ref):
        wu = wu_ref[...]
        for c in range(tm // sub):
            rows = pl.ds(c * sub, sub)
            up = _nt(hn_ref[pl.ds(pl.multiple_of(pl.program_id(1) * tm + c * sub, sub), sub), :], wu)
            up_ref[rows, :] = up.astype(BF16)
            gc = gc_ref[rows, :].astype(F32)
            sg = jax.nn.sigmoid(gc)
            silu = gc * sg
            silu_ref[rows, :] = silu.astype(BF16)
            dsilu_ref[rows, :] = (sg * (1.0 - silu) + silu).astype(BF16)
            a_ref[rows, :] = (silu * up).astype(BF16)

    tile = pl.BlockSpec((tm, tn), lambda j, i: (i, j))
    out = jax.ShapeDtypeStruct((s, f), BF16)
    return _pcall(
        body, comm=comm, name=name, grid=(f // tn, s // tm),
        in_specs=[_resident((s, d)), pl.BlockSpec((tn, d), lambda j, i: (j, 0)), tile],
        out_specs=[tile, tile, tile, tile], out_shape=[out, out, out, out],
        compiler_params=_params("arbitrary", "arbitrary"),
    )(hn, wut, gc)


def _ffn_fwd2(a, wd, h, gain, tm, name, comm=None):
    s, d = h.shape
    f = a.shape[1]

    def body(a_ref, wd_ref, h_ref, gain_ref, ho_ref, hn_ref):
        wd_v, gain = wd_ref[...], gain_ref[...]
        sub = min(SUB_ROWS, tm)
        for c in range(tm // sub):
            rows = pl.ds(c * sub, sub)
            ho = h_ref[rows, :] + _nn(a_ref[rows, :], wd_v)
            ho_ref[rows, :] = ho
            hn_ref[rows, :] = _rms(ho, gain).astype(BF16)

    row = lambda c: pl.BlockSpec((tm, c), lambda i: (i, 0))
    return _pcall(
        body, comm=comm, name=name, grid=(s // tm,),
        in_specs=[row(f), _resident(wd.shape), row(d), _whole((1, d))],
        out_specs=[row(d), row(d)],
        out_shape=[jax.ShapeDtypeStruct((s, d), F32), jax.ShapeDtypeStruct((s, d), BF16)],
        compiler_params=_params("arbitrary"),
    )(a, wd, h, gain)


def _odd_fwd(xn, h, win, sgu, ws, bfull, wout, gffn, tm, comm=None):
    s, d = h.shape
    e = win.shape[1]
    cw = e // 2
    heads = ws.shape[0]

    def body(xn_ref, h_ref, win_ref, sgu_ref, ws_ref, b_ref, wout_ref, gffn_ref,
             pre_ref, gate_ref, mixo_ref, ho_ref, hn_ref, gbuf):
        pre = _nn(xn_ref[...], win_ref[...])
        pre_ref[...] = pre.astype(BF16)
        p = _gelu(pre)
        u, v = p[:, :cw], p[:, cw:]
        vn = _rms(v, sgu_ref[...]).astype(BF16)
        for n in range(tm // CHUNK):
            rows = slice(n * CHUNK, (n + 1) * CHUNK)
            for hd in range(heads):
                cols = slice(hd * CHUNK, (hd + 1) * CHUNK)
                gbuf[rows, cols] = _nn(ws_ref[hd], vn[rows, cols]) + b_ref[:, cols]
        gate = gbuf[...]
        gate_ref[...] = gate.astype(BF16)
        mixo = (u * gate).astype(BF16)
        mixo_ref[...] = mixo
        ho = h_ref[...] + _nn(mixo, wout_ref[...])
        ho_ref[...] = ho
        hn_ref[...] = _rms(ho, gffn_ref[...]).astype(BF16)

    row = lambda c: pl.BlockSpec((tm, c), lambda i: (i, 0))
    return _pcall(
        body, comm=comm, name="odd_fwd", grid=(s // tm,),
        in_specs=[row(d), row(d), _whole(win.shape), _whole(sgu.shape), _whole(ws.shape), _whole(bfull.shape),
                  _whole(wout.shape), _whole((1, d))],
        out_specs=[row(e), row(cw), row(cw), row(d), row(d)],
        out_shape=[jax.ShapeDtypeStruct((s, e), BF16), jax.ShapeDtypeStruct((s, cw), BF16), jax.ShapeDtypeStruct((s, cw), BF16),
                   jax.ShapeDtypeStruct((s, d), F32), jax.ShapeDtypeStruct((s, d), BF16)],
        scratch_shapes=[pltpu.VMEM((tm, cw), F32)],
        compiler_params=_params("arbitrary"),
    )(xn, h, win, sgu, ws, bfull, wout, gffn)


def _loss_bwd(h, gain, target, tm):
    s, d = h.shape

    def body(h_ref, gain_ref, t_ref, dh_ref, dhb_ref, dgain_ref, loss_ref):
        i = pl.program_id(0)
        hv = h_ref[...]
        gain = gain_ref[...]
        err = _rms(hv, gain) - t_ref[...]
        dy = err * (1.0 / d)
        dx, dgain = _rms_bwd(dy, hv, gain)
        dh_ref[...] = dx
        dhb_ref[...] = dx.astype(BF16)
        _acc(dgain_ref, dgain, i == 0)
        _acc(loss_ref, jnp.sum(err * err, axis=0, keepdims=True) * (0.5 / d), i == 0)

    row = pl.BlockSpec((tm, d), lambda i: (i, 0))
    return _pcall(
        body, name="loss_bwd", grid=(s // tm,),
        in_specs=[row, _whole((1, d)), row],
        out_specs=[row, row, _whole((1, d)), _whole((1, d))],
        out_shape=[jax.ShapeDtypeStruct((s, d), F32), jax.ShapeDtypeStruct((s, d), BF16), jax.ShapeDtypeStruct((1, d), F32),
                   jax.ShapeDtypeStruct((1, d), F32)],
        compiler_params=_params("arbitrary"),
    )(h, gain, target)


def _ffn_bwd1(dhb, g, silu, dsilu, up, wd, cw, tm, tn, name, comm=None):
    s, d = dhb.shape
    f = g.shape[1]
    ni = s // tm

    def body(dh_ref, g_ref, silu_ref, dsilu_ref, up_ref, wd_ref, cw_ref, dg_ref, dup_ref, dcw_ref, dcb_ref, ebuf):
        i = pl.program_id(1)

        @pl.when(i == 0)
        def _():
            ebuf[tm:tm + HALO, :] = jnp.zeros((HALO, tn), F32)

        wd_v = wd_ref[...]
        sub = min(SUB_ROWS, tm)
        sums = [None] * 4
        for c in reversed(range(tm // sub)):
            rows = pl.ds(c * sub, sub)
            da = _nt(dh_ref[pl.ds(pl.multiple_of((ni - 1 - i) * tm + c * sub, sub), sub), :], wd_v)
            dup_ref[rows, :] = (da * silu_ref[rows, :].astype(F32)).astype(BF16)
            dgc = da * up_ref[rows, :].astype(F32) * dsilu_ref[rows, :].astype(F32)
            ebuf[rows, :] = dgc
            ext = ebuf[pl.ds(c * sub, sub + HALO), :]
            s1 = _shift(ext, 1, False)[0:sub, :]
            s2 = _shift(ext, 2, False)[0:sub, :]
            dg_ref[rows, :] = (cw_ref[2:3, :] * dgc + cw_ref[1:2, :] * s1 + cw_ref[0:1, :] * s2).astype(BF16)
            gv = g_ref[rows, :].astype(F32)
            for k, term in enumerate((s2 * gv, s1 * gv, dgc * gv, dgc)):
                part = jnp.sum(term, axis=0, keepdims=True)
                sums[k] = part if sums[k] is None else sums[k] + part
        for k in range(3):
            _acc(dcw_ref.at[k:k + 1, :], sums[k], i == 0)
        _acc(dcb_ref, sums[3], i == 0)
        ebuf[tm:tm + HALO, :] = ebuf[0:HALO, :]

    tile = pl.BlockSpec((tm, tn), lambda j, i: (ni - 1 - i, j))
    wcol = lambda r: pl.BlockSpec((r, tn), lambda j, i: (0, j))
    out = jax.ShapeDtypeStruct((s, f), BF16)
    return _pcall(
        body, comm=comm, name=name, grid=(f // tn, ni),
        in_specs=[_resident((s, d)), tile, tile, tile, tile,
                  pl.BlockSpec((tn, d), lambda j, i: (j, 0)), wcol(3)],
        out_specs=[tile, tile, wcol(3), wcol(1)],
        out_shape=[out, out, jax.ShapeDtypeStruct((3, f), F32), jax.ShapeDtypeStruct((1, f), F32)],
        scratch_shapes=[pltpu.VMEM((tm + HALO, tn), F32)],
        compiler_params=_params("arbitrary", "arbitrary"),
    )(dhb, g, silu, dsilu, up, wd, cw)


def _ffn_bwd2(dg, dup, wg, wu, h, gain, dh, tm, name, comm=None):
    s, d = h.shape
    f = dg.shape[1]

    def body(dg_ref, dup_ref, wg_ref, wu_ref, h_ref, gain_ref, dh_ref, dho_ref, dhb_ref, dgain_ref):
        wg_v, wu_v, gain = wg_ref[...], wu_ref[...], gain_ref[...]
        sub = min(SUB_ROWS, tm)
        dgain = None
        for c in range(tm // sub):
            rows = pl.ds(c * sub, sub)
            dhn = _nn(dg_ref[rows, :], wg_v) + _nn(dup_ref[rows, :], wu_v)
            dx, part = _rms_bwd(dhn, h_ref[rows, :], gain)
            dgain = part if dgain is None else dgain + part
            dho = dh_ref[rows, :] + dx
            dho_ref[rows, :] = dho
            dhb_ref[rows, :] = dho.astype(BF16)
        _acc(dgain_ref, dgain, pl.program_id(0) == 0)

    row = lambda c: pl.BlockSpec((tm, c), lambda i: (i, 0))
    return _pcall(
        body, comm=comm, name=name, grid=(s // tm,),
        in_specs=[row(f), row(f), _resident(wg.shape), _resident(wu.shape), row(d), _whole((1, d)), row(d)],
        out_specs=[row(d), row(d), _whole((1, d))],
        out_shape=[jax.ShapeDtypeStruct((s, d), F32), jax.ShapeDtypeStruct((s, d), BF16), jax.ShapeDtypeStruct((1, d), F32)],
        compiler_params=_params("arbitrary"),
    )(dg, dup, wg, wu, h, gain, dh)


def _odd_bwd(dhb, dh, wout, pre, gate, ws, wst, sgu, win, h, gmix, tm, comm=None):
    s, d = h.shape
    e = win.shape[1]
    cw = e // 2
    heads = ws.shape[0]
    ni = s // tm

    def body(dhb_ref, dh_ref, wout_ref, pre_ref, gate_ref, ws_ref, wst_ref, sgu_ref, win_ref, h_ref, gmix_ref,
             dpre_ref, dho_ref, dhob_ref, dgain_ref, dsgu_ref, dws_ref, db_ref, vbuf, gacc):
        i = pl.program_id(0)
        first = i == 0
        dmixo = _nt(dhb_ref[...], wout_ref[...])
        pre = pre_ref[...].astype(F32)
        p = _gelu(pre)
        u, v = p[:, :cw], p[:, cw:]
        sgu = sgu_ref[...]
        rv = lax.rsqrt(jnp.mean(v * v, axis=-1, keepdims=True) + EPS)
        vh = v * rv
        vn = (vh * sgu).astype(BF16)
        du = dmixo * gate_ref[...].astype(F32)
        dgate = dmixo * u
        dgate_b = dgate.astype(BF16)
        gsum = dgate[0:CHUNK, :]
        for n in range(1, tm // CHUNK):
            gsum = gsum + dgate[n * CHUNK:(n + 1) * CHUNK, :]
        _acc(gacc, gsum, first)
        for hd in range(heads):
            cols = slice(hd * CHUNK, (hd + 1) * CHUNK)
            dws = None
            for n in range(tm // CHUNK):
                rows = slice(n * CHUNK, (n + 1) * CHUNK)
                vbuf[rows, cols] = _nn(wst_ref[hd], dgate_b[rows, cols])
                part = _nt(dgate_b[rows, cols], vn[rows, cols])
                dws = part if dws is None else dws + part
            _acc(dws_ref.at[hd], dws, first)
        dvn = vbuf[...]
        _acc(dsgu_ref, jnp.sum(dvn * vh, axis=0, keepdims=True), first)
        dvh = dvn * sgu
        dv = rv * (dvh - vh * jnp.mean(dvh * vh, axis=-1, keepdims=True))
        dpre = (jnp.concatenate([du, dv], axis=1) * _gelu_grad(pre)).astype(BF16)
        dpre_ref[...] = dpre
        dx, dgain = _rms_bwd(_nt(dpre, win_ref[...]), h_ref[...], gmix_ref[...])
        dho = dh_ref[...] + dx
        dho_ref[...] = dho
        dhob_ref[...] = dho.astype(BF16)
        _acc(dgain_ref, dgain, first)

        @pl.when(i == ni - 1)
        def _():
            ones = jnp.ones((8, CHUNK), F32)
            for hd in range(heads):
                tot = lax.dot_general(ones, gacc[:, hd * CHUNK:(hd + 1) * CHUNK], (((1,), (1,)), ((), ())),
                                      preferred_element_type=F32, precision=lax.Precision.HIGHEST)
                db_ref[hd:hd + 1, :] = tot[0:1, :]

    row = lambda c: pl.BlockSpec((tm, c), lambda i: (i, 0))
    return _pcall(
        body, comm=comm, name="odd_bwd", grid=(ni,),
        in_specs=[row(d), row(d), _whole(wout.shape), row(e), row(cw), _whole(ws.shape), _whole(wst.shape), _whole(sgu.shape),
                  _whole(win.shape), row(d), _whole((1, d))],
        out_specs=[row(e), row(d), row(d), _whole((1, d)), _whole((1, cw)), _whole(ws.shape), _whole((heads, CHUNK))],
        out_shape=[jax.ShapeDtypeStruct((s, e), BF16), jax.ShapeDtypeStruct((s, d), F32), jax.ShapeDtypeStruct((s, d), BF16),
                   jax.ShapeDtypeStruct((1, d), F32), jax.ShapeDtypeStruct((1, cw), F32), jax.ShapeDtypeStruct(ws.shape, F32),
                   jax.ShapeDtypeStruct((heads, CHUNK), F32)],
        scratch_shapes=[pltpu.VMEM((tm, cw), F32), pltpu.VMEM((CHUNK, cw), F32)],
        compiler_params=_params("arbitrary"),
    )(dhb, dh, wout, pre, gate, ws, wst, sgu, win, h, gmix)


def _even_bwd(dhb, dh, wout, proj, cq, pooled, conva, wpool, wpoolt, pscale, win, x, gmix, tm, comm=None):
    s, d = x.shape
    e = win.shape[1]
    aw = e // 4
    ni = s // tm

    def body(dhb_ref, dh_ref, wout_ref, proj_ref, cq_ref, pooled_ref, ca_ref, wp_ref, wpt_ref, ps_ref, win_ref, x_ref, gmix_ref,
             dproj_ref, dx_ref, dgain_ref, dca_ref, dwp_ref, dps_ref, cbuf, ebuf):
        i = pl.program_id(0)
        first = i == 0

        @pl.when(first)
        def _():
            cbuf[tm:tm + HALO, :] = jnp.zeros((HALO, aw), F32)
            ebuf[tm:tm + HALO, :] = jnp.zeros((HALO, aw), F32)

        dmix = _nt(dhb_ref[...], wout_ref[...])
        dy_a, dy_b = dmix[:, :aw], dmix[:, aw:]
        proj = proj_ref[...].astype(F32)
        a_b, a_c, a_v = (proj[:, k * aw:(k + 1) * aw] for k in range(3))
        da_b = dy_a * cq_ref[...].astype(F32)
        dcq = dy_a * a_b
        cbuf[0:tm, :] = dcq
        cext = cbuf[...]
        s1 = _shift(cext, 1, False)[0:tm, :]
        s2 = _shift(cext, 2, False)[0:tm, :]
        q = a_c * a_v
        for k, shifted in enumerate((s2, s1, dcq)):
            _acc(dca_ref.at[k:k + 1, :], jnp.sum(shifted * q, axis=0, keepdims=True), first)
        dq = ca_ref[2:3, :] * dcq + ca_ref[1:2, :] * s1 + ca_ref[0:1, :] * s2
        da_c = dq * a_v
        da_v = dq * a_c
        dps, dpool = [], []
        for g, w in enumerate(WINDOWS):
            cols = slice(g * LANE, (g + 1) * LANE)
            pooled = pooled_ref[:, cols]
            mixed = _nn(pooled, wp_ref[g])
            dps.append(jnp.sum(dy_b[:, cols] * mixed, axis=0, keepdims=True))
            dmixed = (dy_b[:, cols] * ps_ref[:, cols]).astype(BF16)
            _acc(dwp_ref.at[g], _tn(pooled, dmixed), first)
            dp = _nn(dmixed, wpt_ref[g])
            dpool.append(dp)
            ebuf[0:tm, cols] = dp / _counts((ni - 1 - i) * tm, tm, w)
        _acc(dps_ref, jnp.concatenate(dps, axis=1), first)
        eext = ebuf[...]
        dzs = []
        for g, w in enumerate(WINDOWS):
            cols = slice(g * LANE, (g + 1) * LANE)
            dzs.append(_window_sum(eext[:, cols], w, False)[0:tm, :] - dpool[g])
        dproj = jnp.concatenate([da_b, da_c, da_v] + dzs, axis=1).astype(BF16)
        dproj_ref[...] = dproj
        dx, dgain = _rms_bwd(_nt(dproj, win_ref[...]), x_ref[...], gmix_ref[...])
        dx_ref[...] = dh_ref[...] + dx
        _acc(dgain_ref, dgain, first)
        cbuf[tm:tm + HALO, :] = cbuf[0:HALO, :]
        ebuf[tm:tm + HALO, :] = ebuf[0:HALO, :]

    row = lambda c: pl.BlockSpec((tm, c), lambda i: (ni - 1 - i, 0))
    return _pcall(
        body, comm=comm, name="even_bwd", grid=(ni,),
        in_specs=[row(d), row(d), _whole(wout.shape), row(e), row(aw), row(aw), _whole(conva.shape), _whole(wpool.shape),
                  _whole(wpoolt.shape), _whole(pscale.shape), _whole(win.shape), row(d), _whole((1, d))],
        out_specs=[row(e), row(d), _whole((1, d)), _whole(conva.shape), _whole(wpool.shape), _whole(pscale.shape)],
        out_shape=[jax.ShapeDtypeStruct((s, e), BF16), jax.ShapeDtypeStruct((s, d), F32), jax.ShapeDtypeStruct((1, d), F32),
                   jax.ShapeDtypeStruct(conva.shape, F32), jax.ShapeDtypeStruct(wpool.shape, F32),
                   jax.ShapeDtypeStruct(pscale.shape, F32)],
        scratch_shapes=[pltpu.VMEM((tm + HALO, aw), F32), pltpu.VMEM((tm + HALO, aw), F32)],
        compiler_params=_params("arbitrary"),
    )(dhb, dh, wout, proj, cq, pooled, conva, wpool, wpoolt, pscale, win, x, gmix)


def _wgrad(a, b, tk, ts, name, comm=None):
    s, ka = a.shape
    nb = b.shape[1]
    nt = s // ts

    def body(a_ref, b_ref, o_ref, acc):
        t = pl.program_id(1)
        _acc(acc, _tn(a_ref[...], b_ref[...]), t == 0)

        @pl.when(t == nt - 1)
        def _():
            o_ref[...] = acc[...].astype(BF16)

    return _pcall(
        body, comm=comm, name=name, grid=(ka // tk, nt),
        in_specs=[pl.BlockSpec((ts, tk), lambda k, t: (t, k)),
                  _resident((s, nb)) if nt == 1 else pl.BlockSpec((ts, nb), lambda k, t: (t, 0))],
        out_specs=pl.BlockSpec((tk, nb), lambda k, t: (k, 0)),
        out_shape=jax.ShapeDtypeStruct((ka, nb), BF16),
        scratch_shapes=[pltpu.VMEM((tk, nb), F32)],
        compiler_params=_params("arbitrary", "arbitrary"),
    )(a, b)


def _adamw(parts, w, m, v, tr, name, layer=None, into=None, carried=()):
    r, c = w.shape[-2:]
    n, rp, cp = parts.shape
    assert rp >= r and r % tr == 0
    prev, carried = ([] if into is None else list(into)), list(carried)

    def body(p_ref, w_ref, m_ref, v_ref, *rest):
        g_ref, d_ref, mo_ref, vo_ref = rest[len(prev) + len(carried):len(prev) + len(carried) + 4]
        g = p_ref[0, 0:tr, 0:c].astype(F32)
        for j in range(1, n):
            g = g + p_ref[j, 0:tr, 0:c].astype(F32)
        g_ref[...] = g
        mn = ADAM_B1 * m_ref[...] + (1.0 - ADAM_B1) * g
        vn = ADAM_B2 * v_ref[...] + (1.0 - ADAM_B2) * (g * g)
        mo_ref[...] = mn
        vo_ref[...] = vn
        m_hat = mn / (1.0 - ADAM_B1 ** ADAM_STEP)
        v_hat = vn / (1.0 - ADAM_B2 ** ADAM_STEP)
        d_ref[...] = -ADAM_LR * (m_hat / (jnp.sqrt(v_hat) + ADAM_EPS) + ADAM_WD * w_ref[...])

    if layer is None:
        row = pl.BlockSpec((tr, c), lambda i: (i, 0))
    else:
        row = pl.BlockSpec((None, tr, c), lambda i: (layer, i, 0))
    out = jax.ShapeDtypeStruct(w.shape, F32)
    untouched = pl.BlockSpec(memory_space=pl.ANY)
    aliases = {4 + k: k for k in range(len(prev))}
    aliases.update({4 + len(prev) + k: 4 + k for k in range(len(carried))})
    res = _pcall(
        body, name=name, grid=(r // tr,),
        in_specs=[pl.BlockSpec((n, tr, cp), lambda i: (0, i, 0)), row, row, row] + [untouched] * (len(prev) + len(carried)),
        out_specs=[row, row, row, row] + [untouched] * len(carried),
        out_shape=[out, out, out, out] + [jax.ShapeDtypeStruct(a.shape, a.dtype) for a in carried],
        input_output_aliases=aliases,
        compiler_params=_params("arbitrary"),
    )(parts, w, m, v, *prev, *carried)
    return (res[:4], res[4:]) if carried else res


def _pair_add(grad, other, spec, core, name):
    axis, width = spec
    slot = other.shape[1:]

    def body(core_ref, g_ref, o_ref, out_ref):
        out_ref[...] = (g_ref[...].astype(F32) + o_ref[...].astype(F32)).astype(BF16)

    if axis == 0:
        gspec = pl.BlockSpec(slot, lambda q, core_ref: (2 * q + core_ref[0], 0))
    else:
        gspec = pl.BlockSpec(slot, lambda q, core_ref: (0, 2 * q + core_ref[0]))
    per_chip = pl.BlockSpec((None,) + slot, lambda q, core_ref: (q, 0, 0))
    return _pcall(
        body, name=name,
        grid_spec=pltpu.PrefetchScalarGridSpec(num_scalar_prefetch=1, grid=(N_CHIP,), in_specs=[gspec, per_chip], out_specs=per_chip),
        out_shape=jax.ShapeDtypeStruct(other.shape, BF16),
        compiler_params=_params("arbitrary"),
    )(core, grad, other)


def _hbm():
    return pl.BlockSpec(memory_space=pltpu.HBM)


def _window(ref, spec, j):
    axis, width = spec
    start = pl.multiple_of(j * width, width)
    return ref.at[(slice(None),) * axis + (pl.ds(start, width),)]


def _here():
    return lax.axis_index("x"), lax.axis_index("y"), lax.axis_index("c")


class _Gather:
    def __init__(self, shards, specs, fulls):
        n = len(shards)
        self.ins, self.specs, self.out_shape = list(shards), list(specs), list(fulls)
        self.sems = [pltpu.SemaphoreType.DMA((7 * n,)), pltpu.SemaphoreType.DMA((7 * n,)), pltpu.SemaphoreType.DMA((n,))]

    def _plan(self, ins, outs, sems):
        send_sems, recv_sems, local_sems = sems
        x, y, c = _here()
        me, sibling = (x, y, c), (x, y, 1 - c)
        chips = [(1 - x, y), (x, 1 - y), (1 - x, 1 - y)]

        def slot(t, dev):
            return _window(outs[t], self.specs[t], 4 * dev[0] + 2 * dev[1] + dev[2])

        def copy(t, k, block, to, src=None):
            return pltpu.make_async_remote_copy(
                src_ref=slot(t, block) if src is None else src, dst_ref=slot(t, block),
                send_sem=send_sems.at[7 * t + k], recv_sem=recv_sems.at[7 * t + k], device_id=to, device_id_type=MESH)

        plan = []
        for t in range(len(ins)):
            plan.append(dict(
                mine=pltpu.make_async_copy(ins[t], slot(t, me), local_sems.at[t]),
                first=[copy(t, 0, me, sibling, src=ins[t])] + [copy(t, 1 + j, me, (*q, c), src=ins[t]) for j, q in enumerate(chips)],
                over_ici=[copy(t, 1 + j, (*q, c), me) for j, q in enumerate(chips)],
                passed=[copy(t, 4 + j, (*q, c), sibling) for j, q in enumerate(chips)],
                from_sibling=[copy(t, 0, sibling, me)] + [copy(t, 4 + j, (*q, 1 - c), me) for j, q in enumerate(chips)]))
        return plan

    def start(self, ins, outs, sems):
        for p in self._plan(ins, outs, sems):
            p["mine"].start()
            for cp in p["first"]:
                cp.start()

    def middle(self, ins, outs, sems):
        for p in self._plan(ins, outs, sems):
            for arrived, onward in zip(p["over_ici"], p["passed"]):
                arrived.wait_recv()
                onward.start()

    def finish(self, ins, outs, sems):
        plan = self._plan(ins, outs, sems)
        for p in plan:
            for cp in p["from_sibling"]:
                cp.wait_recv()
        for p in plan:
            for cp in p["first"] + p["passed"]:
                cp.wait_send()
            p["mine"].wait()


class _PairExchange:
    def __init__(self, grads, specs):
        n = len(grads)
        self.ins, self.specs = list(grads), list(specs)
        self.out_shape = [jax.ShapeDtypeStruct((N_CHIP,) + a.shape[:sp[0]] + (sp[1],) + a.shape[sp[0] + 1:], a.dtype)
                          for a, sp in zip(grads, specs)]
        self.sems = [pltpu.SemaphoreType.DMA((n,)), pltpu.SemaphoreType.DMA((n,))]

    def start(self, ins, outs, sems):
        send_sems, recv_sems = sems
        x, y, c = _here()
        for t in range(len(ins)):
            for q in range(N_CHIP):
                pltpu.make_async_remote_copy(
                    src_ref=_window(ins[t], self.specs[t], 2 * q + (1 - c)), dst_ref=outs[t].at[q],
                    send_sem=send_sems.at[t], recv_sem=recv_sems.at[t], device_id=(x, y, 1 - c), device_id_type=MESH).start()

    def middle(self, ins, outs, sems):
        pass

    def finish(self, ins, outs, sems):
        send_sems, recv_sems = sems
        x, y, c = _here()
        for t in range(len(ins)):
            every = pltpu.make_async_remote_copy(src_ref=outs[t], dst_ref=outs[t], send_sem=send_sems.at[t],
                                                 recv_sem=recv_sems.at[t], device_id=(x, y, 1 - c), device_id_type=MESH)
            every.wait_send()
            every.wait_recv()


class _ChipExchange:
    def __init__(self, pairs, slotted=(), whole=()):
        self.ins = list(pairs) + list(slotted) + list(whole)
        self.npair, self.nslot = len(pairs), len(pairs) + len(slotted)
        n = len(self.ins)
        self.out_shape = ([jax.ShapeDtypeStruct(a.shape, a.dtype) for a in list(pairs) + list(slotted)]
                          + [jax.ShapeDtypeStruct((N_DEV,) + a.shape, a.dtype) for a in whole])
        self.sems = [pltpu.SemaphoreType.DMA((7 * n,)), pltpu.SemaphoreType.DMA((7 * n,)), pltpu.SemaphoreType.DMA((n,))]

    def _plan(self, ins, outs, sems):
        send_sems, recv_sems, local_sems = sems
        npair, nslot = self.npair, self.nslot
        x, y, c = _here()
        me, chip = 4 * x + 2 * y + c, 2 * x + y
        chips = [(1 - x, y), (x, 1 - y), (1 - x, 1 - y)]
        peers = [(x, y, 1 - c)] + [(*q, c) for q in chips] + [(*q, 1 - c) for q in chips]

        def index(dev):
            return 4 * dev[0] + 2 * dev[1] + dev[2]

        def copy(t, k, arriving):
            peer = peers[k]
            if t < npair:
                src, mine, theirs = ins[t].at[2 * peer[0] + peer[1]], chip, 2 * peer[0] + peer[1]
            else:
                src, mine, theirs = (ins[t].at[index(peer)] if t < nslot else ins[t]), me, index(peer)
            return pltpu.make_async_remote_copy(
                src_ref=src, dst_ref=outs[t].at[theirs if arriving else mine],
                send_sem=send_sems.at[7 * t + k], recv_sem=recv_sems.at[7 * t + k], device_id=peer, device_id_type=MESH)

        own, sent, arriving = [], [], []
        for t in range(len(ins)):
            fan = range(1, 4) if t < npair else range(7)
            if t < npair:
                own.append(pltpu.make_async_copy(ins[t].at[chip], outs[t].at[chip], local_sems.at[t]))
            else:
                own.append(pltpu.make_async_copy(ins[t].at[me] if t < nslot else ins[t], outs[t].at[me], local_sems.at[t]))
            sent += [copy(t, k, False) for k in fan]
            arriving += [copy(t, k, True) for k in fan]
        return own, sent, arriving

    def start(self, ins, outs, sems):
        own, sent, _ = self._plan(ins, outs, sems)
        for cp in own + sent:
            cp.start()

    def middle(self, ins, outs, sems):
        pass

    def finish(self, ins, outs, sems):
        own, sent, arriving = self._plan(ins, outs, sems)
        for cp in arriving:
            cp.wait_recv()
        for cp in sent:
            cp.wait_send()
        for cp in own:
            cp.wait()


class _Jobs:
    def __init__(self, *jobs):
        self.jobs = jobs
        self.ins = [a for j in jobs for a in j.ins]
        self.out_shape = [a for j in jobs for a in j.out_shape]
        self.sems = [a for j in jobs for a in j.sems]

    def _split(self, ins, outs, sems):
        i = o = s = 0
        for j in self.jobs:
            yield j, ins[i:i + len(j.ins)], outs[o:o + len(j.out_shape)], sems[s:s + len(j.sems)]
            i, o, s = i + len(j.ins), o + len(j.out_shape), s + len(j.sems)

    def start(self, ins, outs, sems):
        for j, a, b, c in self._split(ins, outs, sems):
            j.start(a, b, c)

    def middle(self, ins, outs, sems):
        for j, a, b, c in self._split(ins, outs, sems):
            j.middle(a, b, c)

    def finish(self, ins, outs, sems):
        for j, a, b, c in self._split(ins, outs, sems):
            j.finish(a, b, c)

    def results(self, outs):
        return [b for _, _, b, _ in self._split((), outs, ())]


def _alone(job, name):
    return _pcall(lambda: None, comm=job, name=name, in_specs=[], out_specs=[], out_shape=[])()[1]


SMALL_ROWS = 24
REP_COLS = 1024


def _pad_to(a, rows, cols):
    return jnp.pad(a, ((0, rows - a.shape[0]), (0, cols - a.shape[1])))


def _pack_small(conv_a, sgu_norm, conv_ffn, cols):
    return jnp.concatenate([_pad_to(conv_a, 8, cols), _pad_to(sgu_norm, 8, cols),
                            _pad_to(conv_ffn.reshape(-1, conv_ffn.shape[-1]), 8, cols)], axis=0)


def _unpack_small(p, ca_w, sg_w, cf_w):
    return p[0:3, 0:ca_w], p[8:9, 0:sg_w], p[16:22, 0:cf_w].reshape(2, 3, cf_w)


def _tile_rows(rows):
    return -(-rows // 8) * 8


def _pack_rows(parts):
    return jnp.concatenate([_pad_to(a, _tile_rows(a.shape[0]), REP_COLS) for a in parts], axis=0)


def _unpack_rows(p, shapes):
    out, r0 = [], 0
    for r, c in shapes:
        out.append(p[r0:r0 + r, 0:c])
        r0 += _tile_rows(r)
    return out


def _rep_late(norm_mix0, norm_ffn0, pool_scale, b_conv0, w_pool):
    return _pack_rows([norm_mix0, norm_ffn0, pool_scale.reshape(1, -1), _pad_to(b_conv0, 1, 3 * REP_COLS).reshape(3, REP_COLS),
                       w_pool.reshape(-1, REP_COLS)])


def _rep_early(norm_mix1, norm_ffn1, final_norm, b_spatial, b_conv1, loss):
    return _pack_rows([norm_mix1, norm_ffn1, final_norm.reshape(1, -1), b_spatial.reshape(1, -1),
                       _pad_to(b_conv1, 1, 3 * REP_COLS).reshape(3, REP_COLS), loss])


def _unpack_rep(late, early, wsp, like):
    f = like["b_conv_ffn"].shape[1]
    nm0, nf0, ps, bc0, wp = _unpack_rows(late, [(1, REP_COLS), (1, REP_COLS), (1, like["pool_scale"].shape[1]), (3, REP_COLS),
                                                (like["w_pool"].size // REP_COLS, REP_COLS)])
    nm1, nf1, fin, bs, bc1, _ = _unpack_rows(early, [(1, REP_COLS)] * 4 + [(3, REP_COLS), (1, REP_COLS)])
    return {
        "norm_mix": jnp.concatenate([nm0, nm1]), "norm_ffn": jnp.concatenate([nf0, nf1]), "final_norm": fin[0], "pool_scale": ps,
        "b_spatial": bs.reshape(like["b_spatial"].shape),
        "b_conv_ffn": jnp.concatenate([bc0.reshape(1, -1), bc1.reshape(1, -1)])[:, 0:f],
        "w_pool": wp.reshape(like["w_pool"].shape), "w_spatial": wsp.reshape(like["w_spatial"].shape),
    }


def _pad_slots(a, width, padded):
    a = a.reshape(*a.shape[:-1], N_DEV, width)
    a = jnp.pad(a, ((0, 0),) * (a.ndim - 1) + ((0, padded - width),))
    return a.reshape(*a.shape[:-2], N_DEV * padded)


def _unpad_slots(a, width, padded):
    a = a.reshape(*a.shape[:-1], N_DEV, padded)[..., 0:width]
    return a.reshape(*a.shape[:-2], N_DEV * width)


def kernel(x, norm_mix, norm_ffn, final_norm, w_in_even, conv_a, w_pool, pool_scale, w_out_even, w_in_odd, sgu_norm, w_spatial, b_spatial, w_out_odd, w_ffn_gate, w_ffn_up, conv_ffn, b_conv_ffn, w_ffn_down, loss_target, m_norm_mix, m_norm_ffn, m_final_norm, m_w_in_even, m_conv_a, m_w_pool, m_pool_scale, m_w_out_even, m_w_in_odd, m_sgu_norm, m_w_spatial, m_b_spatial, m_w_out_odd, m_w_ffn_gate, m_w_ffn_up, m_conv_ffn, m_b_conv_ffn, m_w_ffn_down, v_norm_mix, v_norm_ffn, v_final_norm, v_w_in_even, v_conv_a, v_w_pool, v_pool_scale, v_w_out_even, v_w_in_odd, v_sgu_norm, v_w_spatial, v_b_spatial, v_w_out_odd, v_w_ffn_gate, v_w_ffn_up, v_conv_ffn, v_b_conv_ffn, v_w_ffn_down):
    s, d = x.shape[1], x.shape[2]
    x2, target = x[0], loss_target[0]
    tm = min(512, s)
    tm_wide = min(2048, s)
    tk_wide = 1024
    tn_fwd = 512
    row = lambda a: a.reshape(1, -1)
    ein, ro = w_in_even.shape[2], w_out_even.shape[1]
    fs = w_ffn_gate.shape[2]
    fsp = -(-fs // LANE) * LANE
    fp = N_DEV * fsp
    full = lambda shape, dtype=BF16: jax.ShapeDtypeStruct(shape, dtype)

    wg_s = [jnp.pad(w_ffn_gate[l].T, ((0, fsp - fs), (0, 0))).astype(BF16) for l in range(2)]
    wu_s = [jnp.pad(w_ffn_up[l].T, ((0, fsp - fs), (0, 0))).astype(BF16) for l in range(2)]
    wd_s = [jnp.pad(w_ffn_down[l], ((0, fsp - fs), (0, 0))).astype(BF16) for l in range(2)]
    small_s = _pack_small(conv_a[0], sgu_norm, conv_ffn, fsp)[None]
    in_spec, out_spec, gu_spec, down_spec = (1, ein), (0, ro), (0, fsp), (0, fsp)
    full_in, full_out, full_gu, full_down = full((d, N_DEV * ein)), full((N_DEV * ro, d)), full((fp, d)), full((fp, d))
    win_e, wout_e, gsmall = _alone(_Gather([w_in_even[0].astype(BF16), w_out_even[0].astype(BF16), small_s], [in_spec, out_spec, (0, 1)],
                                           [full_in, full_out, full((N_DEV, SMALL_ROWS, fsp), F32)]), "gather_mix0")
    ca_full = jnp.moveaxis(gsmall[:, 0:3, 0:conv_a.shape[2]], 0, 1).reshape(3, -1)
    sgu_full = gsmall[:, 8, 0:sgu_norm.shape[1]].reshape(1, -1)
    cf_full = jnp.moveaxis(gsmall[:, 16:22, :].reshape(N_DEV, 2, 3, fsp), 0, 2).reshape(2, 3, fp)
    cb_full = _pad_slots(b_conv_ffn, fs, fsp)

    tril = jnp.tril(jnp.ones((CHUNK, CHUNK), F32))
    ws_m = w_spatial[0] * tril
    ws_b = ws_m.astype(BF16)
    wst_b = jnp.swapaxes(ws_m, 1, 2).astype(BF16)
    bfull = jnp.repeat(b_spatial[0].T, CHUNK, axis=1)
    wpool_b = w_pool[0].astype(BF16)
    wpoolt_b = jnp.swapaxes(w_pool[0], 1, 2).astype(BF16)

    (xn0, proj0, cq0, pooled0, mix0, h1, hn0), (wg0,) = _even_fwd(
        x2, norm_mix[0:1], norm_ffn[0:1], win_e, ca_full, wpool_b, pool_scale, wout_e, tm,
        comm=_Gather([wg_s[0]], [gu_spec], [full_gu]))
    (g0, gc0), (wu0,) = _ffn_gate(hn0, wg0, cf_full[0], cb_full[0:1], tm_wide, tn_fwd, "ffn_gate_l0", comm=_Gather([wu_s[0]], [gu_spec], [full_gu]))
    (up0, a0, sl0, ds0), (wd0,) = _ffn_up(hn0, wu0, gc0, tm_wide, tn_fwd, "ffn_up_l0", comm=_Gather([wd_s[0]], [down_spec], [full_down]))
    (h2, xn1), (win_o, wout_o) = _ffn_fwd2(a0, wd0, h1, norm_mix[1:2], tm, "ffn_fwd2_l0", comm=_Gather(
        [w_in_odd[0].astype(BF16), w_out_odd[0].astype(BF16)], [in_spec, out_spec], [full_in, full_out]))
    (pre1, gate1, mixo1, h3, hn1), (wg1,) = _odd_fwd(xn1, h2, win_o, sgu_full, ws_b, bfull, wout_o, norm_ffn[1:2], tm,
                                                    comm=_Gather([wg_s[1]], [gu_spec], [full_gu]))
    (g1, gc1), (wu1,) = _ffn_gate(hn1, wg1, cf_full[1], cb_full[1:2], tm_wide, tn_fwd, "ffn_gate_l1", comm=_Gather([wu_s[1]], [gu_spec], [full_gu]))
    (up1, a1, sl1, ds1), (wd1,) = _ffn_up(hn1, wu1, gc1, tm_wide, tn_fwd, "ffn_up_l1", comm=_Gather([wd_s[1]], [down_spec], [full_down]))
    h4, _ = _ffn_fwd2(a1, wd1, h3, row(final_norm), tm, "ffn_fwd2_l1")

    core = lax.axis_index("c").astype(jnp.int32).reshape(1)
    ts = min(1024, s)
    dh4, dh4b, d_final, lossvec = _loss_bwd(h4, row(final_norm), target, tm)
    gd1 = _wgrad(a1, dh4b, tk_wide, s, "wgrad_down_l1")
    (dg1, dup1, dcw1, dcb1), (o_d1,) = _ffn_bwd1(dh4b, g1, sl1, ds1, up1, wd1, cf_full[1], tm_wide, tn_fwd, "ffn_bwd1_l1",
                                                 comm=_PairExchange([gd1], [down_spec]))
    p_d1 = _pair_add(gd1, o_d1, down_spec, core, "pair_add_down_l1")
    gg1 = _wgrad(dg1, hn1, tk_wide, s, "wgrad_gate_l1")
    gu1, (o_g1,) = _wgrad(dup1, hn1, tk_wide, s, "wgrad_up_l1", comm=_PairExchange([gg1], [gu_spec]))
    p_g1 = _pair_add(gg1, o_g1, gu_spec, core, "pair_add_gate_l1")
    jobs = _Jobs(_ChipExchange([p_d1]), _PairExchange([gu1], [gu_spec]))
    (dh3, dh3b, d_nffn1), res = _ffn_bwd2(dg1, dup1, wg1, wu1, h3, norm_ffn[1:2], dh4, tm, "ffn_bwd2_l1", comm=jobs)
    (s_d1,), (o_u1,) = jobs.results(res)
    p_u1 = _pair_add(gu1, o_u1, gu_spec, core, "pair_add_up_l1")
    (dpre1, dh2, dh2b, d_nmix1, d_sgu, d_ws, d_b), (s_g1, s_u1) = _odd_bwd(
        dh3b, dh3, wout_o, pre1, gate1, ws_b, wst_b, sgu_full, win_o, h2, norm_mix[1:2], tm, comm=_ChipExchange([p_g1, p_u1]))
    gi1 = _wgrad(xn1, dpre1, tk_wide, ts, "wgrad_in_odd")
    go1, (o_i1,) = _wgrad(mixo1, dh3b, tk_wide, ts, "wgrad_out_odd", comm=_PairExchange([gi1], [in_spec]))
    p_i1 = _pair_add(gi1, o_i1, in_spec, core, "pair_add_in_odd")
    d_wsp = (d_ws * tril).reshape(-1, REP_COLS).astype(BF16)
    jobs = _Jobs(_PairExchange([go1], [out_spec]), _ChipExchange([], [], [d_wsp]))
    gd0, res = _wgrad(a0, dh2b, tk_wide, s, "wgrad_down_l0", comm=jobs)
    (o_o1,), (r_wsp,) = jobs.results(res)
    p_o1 = _pair_add(go1, o_o1, out_spec, core, "pair_add_out_odd")
    d_early = _rep_early(d_nmix1, d_nffn1, d_final, d_b, _unpad_slots(dcb1, fs, fsp), lossvec)
    jobs = _Jobs(_ChipExchange([p_i1, p_o1], [], [d_early]), _PairExchange([gd0], [down_spec]))
    (dg0, dup0, dcw0, dcb0), res = _ffn_bwd1(dh2b, g0, sl0, ds0, up0, wd0, cf_full[0], tm_wide, tn_fwd, "ffn_bwd1_l0", comm=jobs)
    (s_i1, s_o1, r_early), (o_d0,) = jobs.results(res)
    p_d0 = _pair_add(gd0, o_d0, down_spec, core, "pair_add_down_l0")
    gg0, (s_d0,) = _wgrad(dg0, hn0, tk_wide, s, "wgrad_gate_l0", comm=_ChipExchange([p_d0]))
    gu0, (o_g0,) = _wgrad(dup0, hn0, tk_wide, s, "wgrad_up_l0", comm=_PairExchange([gg0], [gu_spec]))
    p_g0 = _pair_add(gg0, o_g0, gu_spec, core, "pair_add_gate_l0")
    jobs = _Jobs(_ChipExchange([p_g0]), _PairExchange([gu0], [gu_spec]))
    (dh1, dh1b, d_nffn0), res = _ffn_bwd2(dg0, dup0, wg0, wu0, h1, norm_ffn[0:1], dh2, tm, "ffn_bwd2_l0", comm=jobs)
    (s_g0,), (o_u0,) = jobs.results(res)
    p_u0 = _pair_add(gu0, o_u0, gu_spec, core, "pair_add_up_l0")
    go0 = _wgrad(mix0, dh1b, tk_wide, ts, "wgrad_out_even")
    jobs = _Jobs(_ChipExchange([p_u0]), _PairExchange([go0], [out_spec]))
    (dproj0, grad_x, d_nmix0, d_ca, d_wp, d_ps), res = _even_bwd(
        dh1b, dh1, wout_e, proj0, cq0, pooled0, ca_full, wpool_b, wpoolt_b, pool_scale, win_e, x2, norm_mix[0:1], tm, comm=jobs)
    (s_u0,), (o_o0,) = jobs.results(res)
    p_o0 = _pair_add(go0, o_o0, out_spec, core, "pair_add_out_even")
    d_small = jnp.stack([_pack_small(a, b, c, fsp) for a, b, c in zip(
        jnp.moveaxis(d_ca.reshape(3, N_DEV, -1), 1, 0), jnp.moveaxis(d_sgu.reshape(1, N_DEV, -1), 1, 0),
        jnp.moveaxis(jnp.stack([dcw0, dcw1]).reshape(2, 3, N_DEV, fsp), 2, 0))])
    d_late = _rep_late(d_nmix0, d_nffn0, d_ps, _unpad_slots(dcb0, fs, fsp), d_wp)
    gi0, (r_small, r_late) = _wgrad(xn0, dproj0, tk_wide, ts, "wgrad_in_even", comm=_ChipExchange([], [d_small], [d_late]))
    jobs = _Jobs(_PairExchange([gi0], [in_spec]), _ChipExchange([p_o0]))
    (o_i0,), (s_o0,) = jobs.results(_alone(jobs, "pair_exchange_in_even"))
    p_i0 = _pair_add(gi0, o_i0, in_spec, core, "pair_add_in_even")
    (s_i0,) = _alone(_ChipExchange([p_i0]), "chip_exchange_last")
    loss = jnp.sum(r_early[:, r_early.shape[1] - 8, :])

    out = {}
    out["w_in_even"], (grad_x,) = _adamw(s_i0, w_in_even[0], m_w_in_even[0], v_w_in_even[0], 256, "adamw_in_even", carried=[grad_x])
    out["w_out_even"] = _adamw(s_o0, w_out_even[0], m_w_out_even[0], v_w_out_even[0], ro // 2, "adamw_out_even")
    out["w_in_odd"] = _adamw(s_i1, w_in_odd[0], m_w_in_odd[0], v_w_in_odd[0], 256, "adamw_in_odd")
    out["w_out_odd"] = _adamw(s_o1, w_out_odd[0], m_w_out_odd[0], v_w_out_odd[0], ro // 2, "adamw_out_odd")
    tp = lambda a: jnp.swapaxes(a, 1, 2)
    for nm, s1, s0, w, m, v, back in (("w_ffn_gate", s_g1, s_g0, tp(w_ffn_gate), tp(m_w_ffn_gate), tp(v_w_ffn_gate), tp),
                                      ("w_ffn_up", s_u1, s_u0, tp(w_ffn_up), tp(m_w_ffn_up), tp(v_w_ffn_up), tp),
                                      ("w_ffn_down", s_d1, s_d0, w_ffn_down, m_w_ffn_down, v_w_ffn_down, lambda a: a)):
        l1 = _adamw(s1, w, m, v, fs // 2, "adamw_%s_l1" % nm, layer=1)
        out[nm] = [back(a) for a in _adamw(s0, w, m, v, fs // 2, "adamw_%s_l0" % nm, layer=0, into=l1)]
    small = _adamw(r_small, _pack_small(conv_a[0], sgu_norm, conv_ffn, fsp), _pack_small(m_conv_a[0], m_sgu_norm, m_conv_ffn, fsp),
                   _pack_small(v_conv_a[0], v_sgu_norm, v_conv_ffn, fsp), SMALL_ROWS, "adamw_small")
    no_loss = jnp.zeros((1, REP_COLS), F32)
    early = _adamw(r_early, *[_rep_early(nm[1:2], nf[1:2], fn, bs, bc[1:2], no_loss) for nm, nf, fn, bs, bc in (
        (norm_mix, norm_ffn, final_norm, b_spatial, b_conv_ffn), (m_norm_mix, m_norm_ffn, m_final_norm, m_b_spatial, m_b_conv_ffn),
        (v_norm_mix, v_norm_ffn, v_final_norm, v_b_spatial, v_b_conv_ffn))], r_early.shape[1], "adamw_replicated_early")
    wsp = _adamw(r_wsp, w_spatial.reshape(-1, REP_COLS), m_w_spatial.reshape(-1, REP_COLS), v_w_spatial.reshape(-1, REP_COLS),
                 r_wsp.shape[1], "adamw_w_spatial")
    late = _adamw(r_late, *[_rep_late(nm[0:1], nf[0:1], ps, bc[0:1], wp) for nm, nf, ps, bc, wp in (
        (norm_mix, norm_ffn, pool_scale, b_conv_ffn, w_pool), (m_norm_mix, m_norm_ffn, m_pool_scale, m_b_conv_ffn, m_w_pool),
        (v_norm_mix, v_norm_ffn, v_pool_scale, v_b_conv_ffn, v_w_pool))], r_late.shape[1], "adamw_replicated_late")

    names = ["norm_mix", "norm_ffn", "final_norm", "w_in_even", "conv_a", "w_pool", "pool_scale", "w_out_even", "w_in_odd", "sgu_norm",
             "w_spatial", "b_spatial", "w_out_odd", "w_ffn_gate", "w_ffn_up", "conv_ffn", "b_conv_ffn", "w_ffn_down"]
    like = {"norm_mix": norm_mix, "norm_ffn": norm_ffn, "final_norm": final_norm, "w_in_even": w_in_even, "conv_a": conv_a,
            "w_pool": w_pool, "pool_scale": pool_scale, "w_out_even": w_out_even, "w_in_odd": w_in_odd, "sgu_norm": sgu_norm,
            "w_spatial": w_spatial, "b_spatial": b_spatial, "w_out_odd": w_out_odd, "w_ffn_gate": w_ffn_gate, "w_ffn_up": w_ffn_up,
            "conv_ffn": conv_ffn, "b_conv_ffn": b_conv_ffn, "w_ffn_down": w_ffn_down}
    groups = []
    for k in range(4):
        ca_k, sg_k, cf_k = _unpack_small(small[k], conv_a.shape[2], sgu_norm.shape[1], conv_ffn.shape[2])
        vals = dict(_unpack_rep(late[k], early[k], wsp[k], like))
        vals.update(conv_a=ca_k, sgu_norm=sg_k, conv_ffn=cf_k)
        for nm in ("w_in_even", "w_in_odd", "w_out_even", "w_out_odd", "w_ffn_gate", "w_ffn_up", "w_ffn_down"):
            vals[nm] = out[nm][k]
        groups.append([vals[nm].reshape(like[nm].shape) for nm in names])
    return (loss, grad_x[None], *groups[0], *groups[1], *groups[2], *groups[3])
```

```python
import functools

import jax
import jax.numpy as jnp
from jax import lax
from jax.experimental import pallas as pl
from jax.experimental.pallas import tpu as pltpu

F32, BF16 = jnp.float32, jnp.bfloat16
EPS = 1e-6
WINDOWS = (2, 4, 8, 16)
HALO = 16
CHUNK = 128
N_DEV = 8
N_CHIP = 4
MESH = pl.DeviceIdType.MESH
VMEM_LIMIT = 56 * 2**20
LANE = 128
ADAM_LR, ADAM_B1, ADAM_B2, ADAM_EPS, ADAM_WD, ADAM_STEP = 0.001, 0.9, 0.999, 1e-08, 0.01, 10
SUB_ROWS = 256
LATE_NUM, LATE_DEN = 7, 8
INV_SQRT2 = 0.7071067811865476
INV_SQRT2PI = 0.3989422804014327


def _pcall(body, comm=None, **kw):
    if comm is None:
        return pl.pallas_call(body, **kw)
    in_specs, out_specs, out_shape = list(kw.pop("in_specs")), kw.pop("out_specs"), kw.pop("out_shape")
    single = not isinstance(out_shape, (list, tuple))
    out_specs, out_shape = ([out_specs], [out_shape]) if single else (list(out_specs), list(out_shape))
    scratch = list(kw.pop("scratch_shapes", []))
    grid = kw.get("grid", ())
    n_in, n_out, n_scr, c_in, c_out = len(in_specs), len(out_specs), len(scratch), len(comm.ins), len(comm.out_shape)

    def hosted(*refs):
        cuts = [0, n_in, n_in + c_in, n_in + c_in + n_out, n_in + c_in + n_out + c_out, n_in + c_in + n_out + c_out + n_scr, len(refs)]
        ins, cins, outs, couts, scr, sems = (refs[a:b] for a, b in zip(cuts[:-1], cuts[1:]))
        if grid:
            step, steps = 0, 1
            for axis, size in enumerate(grid):
                step, steps = step * size + pl.program_id(axis), steps * size
            pl.when(step == 0)(lambda: comm.start(cins, couts, sems))
            pl.when(step == (steps * LATE_NUM) // LATE_DEN)(lambda: comm.middle(cins, couts, sems))
            body(*ins, *outs, *scr)
            pl.when(step == steps - 1)(lambda: comm.finish(cins, couts, sems))
        else:
            comm.start(cins, couts, sems)
            comm.middle(cins, couts, sems)
            body(*ins, *outs, *scr)
            comm.finish(cins, couts, sems)

    call = pl.pallas_call(hosted, in_specs=in_specs + [_hbm()] * c_in, out_specs=out_specs + [_hbm()] * c_out,
                          out_shape=out_shape + list(comm.out_shape), scratch_shapes=scratch + list(comm.sems), **kw)

    def run(*args):
        res = call(*args, *comm.ins)
        own = res[0] if single else res[:n_out]
        return own, res[n_out:]

    return run


def _params(*sem):
    return pltpu.CompilerParams(dimension_semantics=sem, vmem_limit_bytes=VMEM_LIMIT)


def _whole(shape):
    return pl.BlockSpec(shape, lambda *_: (0,) * len(shape))


def _resident(shape):
    return pl.BlockSpec(shape, lambda *_: (0,) * len(shape), pipeline_mode=pl.Buffered(1))


def _rows(i, tm):
    return pl.ds(pl.multiple_of(i * tm, tm), tm)


def _nn(a, b):
    return jnp.dot(a, b, preferred_element_type=F32)


def _nt(a, b):
    return lax.dot_general(a, b, (((1,), (1,)), ((), ())), preferred_element_type=F32)


def _tn(a, b):
    return lax.dot_general(a, b, (((0,), (0,)), ((), ())), preferred_element_type=F32)


def _rms(x, gain):
    r = lax.rsqrt(jnp.mean(x * x, axis=-1, keepdims=True) + EPS)
    return x * r * gain


def _rms_bwd(dy, x, gain):
    r = lax.rsqrt(jnp.mean(x * x, axis=-1, keepdims=True) + EPS)
    xh = x * r
    dgain = jnp.sum(dy * xh, axis=0, keepdims=True)
    dxh = dy * gain
    dx = r * (dxh - xh * jnp.mean(dxh * xh, axis=-1, keepdims=True))
    return dx, dgain


def _gelu(x):
    return 0.5 * x * (1.0 + lax.erf(x * INV_SQRT2))


def _gelu_grad(x):
    return 0.5 * (1.0 + lax.erf(x * INV_SQRT2)) + x * jnp.exp(-0.5 * x * x) * INV_SQRT2PI


def _acc(ref, val, first):
    @pl.when(first)
    def _():
        ref[...] = val

    @pl.when(jnp.logical_not(first))
    def _():
        ref[...] += val


def _shift(ext, k, back):
    n = ext.shape[0]
    return pltpu.roll(ext, k if back else n - k, axis=0)


def _window_sum(ext, w, back):
    total, step = ext, 1
    while step < w:
        total = total + _shift(total, step, back)
        step *= 2
    return total


def _counts(row0, tm, w):
    pos1 = (row0 + lax.broadcasted_iota(jnp.int32, (tm, 1), 0) + 1).astype(F32)
    return jnp.minimum(pos1, float(w))


def _even_fwd(x, gmix, gffn, win, conva, wpool, pscale, wout, tm, comm=None):
    s, d = x.shape
    e = win.shape[1]
    aw = e // 4

    def body(x_ref, gmix_ref, gffn_ref, win_ref, ca_ref, wp_ref, ps_ref, wout_ref,
             xn_ref, proj_ref, cq_ref, pooled_ref, mix_ref, h_ref, hn_ref, qbuf, zbuf):
        i = pl.program_id(0)

        @pl.when(i == 0)
        def _():
            qbuf[0:HALO, :] = jnp.zeros((HALO, aw), F32)
            zbuf[0:HALO, :] = jnp.zeros((HALO, aw), F32)

        xv = x_ref[...]
        xn = _rms(xv, gmix_ref[...]).astype(BF16)
        xn_ref[...] = xn
        proj = _nn(xn, win_ref[...])
        proj_ref[...] = proj.astype(BF16)
        a_b, a_c, a_v, z = (proj[:, k * aw:(k + 1) * aw] for k in range(4))
        q = a_c * a_v
        qbuf[HALO:HALO + tm, :] = q
        qext = qbuf[...]
        cur = slice(HALO, HALO + tm)
        cq = ca_ref[2:3, :] * q + ca_ref[1:2, :] * _shift(qext, 1, True)[cur, :] + ca_ref[0:1, :] * _shift(qext, 2, True)[cur, :]
        cq_ref[...] = cq.astype(BF16)
        y_a = a_b * cq
        zbuf[HALO:HALO + tm, :] = z
        zext = zbuf[...]
        ys = []
        for g, w in enumerate(WINDOWS):
            cols = slice(g * LANE, (g + 1) * LANE)
            acc = _window_sum(zext[:, cols], w, True)[cur, :]
            pooled = (acc / _counts(i * tm, tm, w) - z[:, cols]).astype(BF16)
            pooled_ref[:, cols] = pooled
            ys.append(_nn(pooled, wp_ref[g]))
        y_b = jnp.concatenate(ys, axis=1) * ps_ref[...]
        mix = jnp.concatenate([y_a, y_b], axis=1).astype(BF16)
        mix_ref[...] = mix
        h = xv + _nn(mix, wout_ref[...])
        h_ref[...] = h
        hn_ref[...] = _rms(h, gffn_ref[...]).astype(BF16)
        qbuf[0:HALO, :] = qbuf[tm:tm + HALO, :]
        zbuf[0:HALO, :] = zbuf[tm:tm + HALO, :]

    row = lambda c: pl.BlockSpec((tm, c), lambda i: (i, 0))
    return _pcall(
        body, comm=comm, name="even_fwd", grid=(s // tm,),
        in_specs=[row(d), _whole((1, d)), _whole((1, d)), _whole(win.shape), _whole(conva.shape), _whole(wpool.shape),
                  _whole(pscale.shape), _whole(wout.shape)],
        out_specs=[row(d), row(e), row(aw), row(aw), row(d), row(d), row(d)],
        out_shape=[jax.ShapeDtypeStruct((s, d), BF16), jax.ShapeDtypeStruct((s, e), BF16), jax.ShapeDtypeStruct((s, aw), BF16),
                   jax.ShapeDtypeStruct((s, aw), BF16), jax.ShapeDtypeStruct((s, d), BF16), jax.ShapeDtypeStruct((s, d), F32),
                   jax.ShapeDtypeStruct((s, d), BF16)],
        scratch_shapes=[pltpu.VMEM((tm + HALO, aw), F32), pltpu.VMEM((tm + HALO, aw), F32)],
        compiler_params=_params("arbitrary"),
    )(x, gmix, gffn, win, conva, wpool, pscale, wout)


def _ffn_gate(hn, wgt, cw, cb, tm, tn, name, comm=None):
    s, d = hn.shape
    f = wgt.shape[0]
    sub = tm

    def body(hn_ref, wg_ref, cw_ref, cb_ref, g_ref, gc_ref, gbuf):
        i = pl.program_id(1)

        @pl.when(i == 0)
        def _():
            gbuf[0:HALO, :] = jnp.zeros((HALO, tn), F32)

        wg = wg_ref[...]
        for c in range(tm // sub):
            rows = pl.ds(c * sub, sub)
            g = _nt(hn_ref[pl.ds(pl.multiple_of(i * tm + c * sub, sub), sub), :], wg)
            g_ref[rows, :] = g.astype(BF16)
            gbuf[pl.ds(HALO + c * sub, sub), :] = g
            ext = gbuf[pl.ds(c * sub, sub + HALO), :]
            gc = (cw_ref[2:3, :] * g + cw_ref[1:2, :] * _shift(ext, 1, True)[HALO:, :]
                  + cw_ref[0:1, :] * _shift(ext, 2, True)[HALO:, :] + cb_ref[...])
            gc_ref[rows, :] = gc.astype(BF16)
        gbuf[0:HALO, :] = gbuf[tm:tm + HALO, :]

    tile = pl.BlockSpec((tm, tn), lambda j, i: (i, j))
    wcol = lambda r: pl.BlockSpec((r, tn), lambda j, i: (0, j))
    out = jax.ShapeDtypeStruct((s, f), BF16)
    return _pcall(
        body, comm=comm, name=name, grid=(f // tn, s // tm),
        in_specs=[_resident((s, d)), pl.BlockSpec((tn, d), lambda j, i: (j, 0)), wcol(3), wcol(1)],
        out_specs=[tile, tile], out_shape=[out, out],
        scratch_shapes=[pltpu.VMEM((tm + HALO, tn), F32)],
        compiler_params=_params("arbitrary", "arbitrary"),
    )(hn, wgt, cw, cb)


def _ffn_up(hn, wut, gc, tm, tn, name, comm=None):
    s, d = hn.shape
    f = wut.shape[0]
    sub = min(SUB_ROWS, tm)

    def body(hn_ref, wu_ref, gc_ref, a_ref, silu_ref, upds_ref):
        wu = wu_ref[...]
        for c in range(tm // sub):
            rows = pl.ds(c * sub, sub)
            up = _nt(hn_ref[pl.ds(pl.multiple_of(pl.program_id(1) * tm + c * sub, sub), sub), :], wu)
            gc = gc_ref[rows, :].astype(F32)
            sg = jax.nn.sigmoid(gc)
            silu = gc * sg
            silu_ref[rows, :] = silu.astype(BF16)
            upds_ref[rows, :] = (up * (sg * (1.0 - silu) + silu)).astype(BF16)
            a_ref[rows, :] = (silu * up).astype(BF16)

    tile = pl.BlockSpec((tm, tn), lambda j, i: (i, j))
    out = jax.ShapeDtypeStruct((s, f), BF16)
    return _pcall(
        body, comm=comm, name=name, grid=(f // tn, s // tm),
        in_specs=[_resident((s, d)), pl.BlockSpec((tn, d), lambda j, i: (j, 0)), tile],
        out_specs=[tile, tile, tile], out_shape=[out, out, out],
        compiler_params=_params("arbitrary", "arbitrary"),
    )(hn, wut, gc)


def _ffn_fwd2(a, wd, h, gain, tm, name, comm=None):
    s, d = h.shape
    f = a.shape[1]

    def body(a_ref, wd_ref, h_ref, gain_ref, ho_ref, hn_ref):
        wd_v, gain = wd_ref[...], gain_ref[...]
        sub = min(SUB_ROWS, tm)
        for c in range(tm // sub):
            rows = pl.ds(c * sub, sub)
            ho = h_ref[rows, :] + _nn(a_ref[rows, :], wd_v)
            ho_ref[rows, :] = ho
            hn_ref[rows, :] = _rms(ho, gain).astype(BF16)

    row = lambda c: pl.BlockSpec((tm, c), lambda i: (i, 0))
    return _pcall(
        body, comm=comm, name=name, grid=(s // tm,),
        in_specs=[row(f), _resident(wd.shape), row(d), _whole((1, d))],
        out_specs=[row(d), row(d)],
        out_shape=[jax.ShapeDtypeStruct((s, d), F32), jax.ShapeDtypeStruct((s, d), BF16)],
        compiler_params=_params("arbitrary"),
    )(a, wd, h, gain)


def _odd_fwd(xn, h, win, sgu, ws, bfull, wout, gffn, tm, comm=None):
    s, d = h.shape
    e = win.shape[1]
    cw = e // 2
    heads = ws.shape[0]

    def body(xn_ref, h_ref, win_ref, sgu_ref, ws_ref, b_ref, wout_ref, gffn_ref,
             pre_ref, gate_ref, mixo_ref, ho_ref, hn_ref, gbuf):
        pre = _nn(xn_ref[...], win_ref[...])
        pre_ref[...] = pre.astype(BF16)
        p = _gelu(pre)
        u, v = p[:, :cw], p[:, cw:]
        vn = _rms(v, sgu_ref[...]).astype(BF16)
        for n in range(tm // CHUNK):
            rows = slice(n * CHUNK, (n + 1) * CHUNK)
            for hd in range(heads):
                cols = slice(hd * CHUNK, (hd + 1) * CHUNK)
                gbuf[rows, cols] = _nn(ws_ref[hd], vn[rows, cols]) + b_ref[:, cols]
        gate = gbuf[...]
        gate_ref[...] = gate.astype(BF16)
        mixo = (u * gate).astype(BF16)
        mixo_ref[...] = mixo
        ho = h_ref[...] + _nn(mixo, wout_ref[...])
        ho_ref[...] = ho
        hn_ref[...] = _rms(ho, gffn_ref[...]).astype(BF16)

    row = lambda c: pl.BlockSpec((tm, c), lambda i: (i, 0))
    return _pcall(
        body, comm=comm, name="odd_fwd", grid=(s // tm,),
        in_specs=[row(d), row(d), _whole(win.shape), _whole(sgu.shape), _whole(ws.shape), _whole(bfull.shape),
                  _whole(wout.shape), _whole((1, d))],
        out_specs=[row(e), row(cw), row(cw), row(d), row(d)],
        out_shape=[jax.ShapeDtypeStruct((s, e), BF16), jax.ShapeDtypeStruct((s, cw), BF16), jax.ShapeDtypeStruct((s, cw), BF16),
                   jax.ShapeDtypeStruct((s, d), F32), jax.ShapeDtypeStruct((s, d), BF16)],
        scratch_shapes=[pltpu.VMEM((tm, cw), F32)],
        compiler_params=_params("arbitrary"),
    )(xn, h, win, sgu, ws, bfull, wout, gffn)


def _loss_bwd(h, gain, target, tm):
    s, d = h.shape

    def body(h_ref, gain_ref, t_ref, dh_ref, dhb_ref, dgain_ref, loss_ref):
        i = pl.program_id(0)
        hv = h_ref[...]
        gain = gain_ref[...]
        err = _rms(hv, gain) - t_ref[...]
        dy = err * (1.0 / d)
        dx, dgain = _rms_bwd(dy, hv, gain)
        dh_ref[...] = dx
        dhb_ref[...] = dx.astype(BF16)
        _acc(dgain_ref, dgain, i == 0)
        _acc(loss_ref, jnp.sum(err * err, axis=0, keepdims=True) * (0.5 / d), i == 0)

    row = pl.BlockSpec((tm, d), lambda i: (i, 0))
    return _pcall(
        body, name="loss_bwd", grid=(s // tm,),
        in_specs=[row, _whole((1, d)), row],
        out_specs=[row, row, _whole((1, d)), _whole((1, d))],
        out_shape=[jax.ShapeDtypeStruct((s, d), F32), jax.ShapeDtypeStruct((s, d), BF16), jax.ShapeDtypeStruct((1, d), F32),
                   jax.ShapeDtypeStruct((1, d), F32)],
        compiler_params=_params("arbitrary"),
    )(h, gain, target)


def _ffn_bwd1(dhb, g, silu, upds, wd, cw, tm, tn, name, comm=None):
    s, d = dhb.shape
    f = g.shape[1]
    ni = s // tm

    def body(dh_ref, g_ref, silu_ref, upds_ref, wd_ref, cw_ref, dg_ref, dup_ref, dcw_ref, dcb_ref, ebuf):
        i = pl.program_id(1)

        @pl.when(i == 0)
        def _():
            ebuf[tm:tm + HALO, :] = jnp.zeros((HALO, tn), F32)

        wd_v = wd_ref[...]
        sub = min(SUB_ROWS, tm)
        sums = [None] * 4
        for c in reversed(range(tm // sub)):
            rows = pl.ds(c * sub, sub)
            da = _nt(dh_ref[pl.ds(pl.multiple_of((ni - 1 - i) * tm + c * sub, sub), sub), :], wd_v)
            dup_ref[rows, :] = (da * silu_ref[rows, :].astype(F32)).astype(BF16)
            dgc = da * upds_ref[rows, :].astype(F32)
            ebuf[rows, :] = dgc
            ext = ebuf[pl.ds(c * sub, sub + HALO), :]
            s1 = _shift(ext, 1, False)[0:sub, :]
            s2 = _shift(ext, 2, False)[0:sub, :]
            dg_ref[rows, :] = (cw_ref[2:3, :] * dgc + cw_ref[1:2, :] * s1 + cw_ref[0:1, :] * s2).astype(BF16)
            gv = g_ref[rows, :].astype(F32)
            for k, term in enumerate((s2 * gv, s1 * gv, dgc * gv, dgc)):
                part = jnp.sum(term, axis=0, keepdims=True)
                sums[k] = part if sums[k] is None else sums[k] + part
        for k in range(3):
            _acc(dcw_ref.at[k:k + 1, :], sums[k], i == 0)
        _acc(dcb_ref, sums[3], i == 0)
        ebuf[tm:tm + HALO, :] = ebuf[0:HALO, :]

    tile = pl.BlockSpec((tm, tn), lambda j, i: (ni - 1 - i, j))
    wcol = lambda r: pl.BlockSpec((r, tn), lambda j, i: (0, j))
    out = jax.ShapeDtypeStruct((s, f), BF16)
    return _pcall(
        body, comm=comm, name=name, grid=(f // tn, ni),
        in_specs=[_resident((s, d)), tile, tile, tile,
                  pl.BlockSpec((tn, d), lambda j, i: (j, 0)), wcol(3)],
        out_specs=[tile, tile, wcol(3), wcol(1)],
        out_shape=[out, out, jax.ShapeDtypeStruct((3, f), F32), jax.ShapeDtypeStruct((1, f), F32)],
        scratch_shapes=[pltpu.VMEM((tm + HALO, tn), F32)],
        compiler_params=_params("arbitrary", "arbitrary"),
    )(dhb, g, silu, upds, wd, cw)


def _ffn_bwd2(dg, dup, wg, wu, h, gain, dh, tm, name, comm=None):
    s, d = h.shape
    f = dg.shape[1]

    def body(dg_ref, dup_ref, wg_ref, wu_ref, h_ref, gain_ref, dh_ref, dho_ref, dhb_ref, dgain_ref):
        wg_v, wu_v, gain = wg_ref[...], wu_ref[...], gain_ref[...]
        sub = min(SUB_ROWS, tm)
        dgain = None
        for c in range(tm // sub):
            rows = pl.ds(c * sub, sub)
            dhn = _nn(dg_ref[rows, :], wg_v) + _nn(dup_ref[rows, :], wu_v)
            dx, part = _rms_bwd(dhn, h_ref[rows, :], gain)
            dgain = part if dgain is None else dgain + part
            dho = dh_ref[rows, :] + dx
            dho_ref[rows, :] = dho
            dhb_ref[rows, :] = dho.astype(BF16)
        _acc(dgain_ref, dgain, pl.program_id(0) == 0)

    row = lambda c: pl.BlockSpec((tm, c), lambda i: (i, 0))
    return _pcall(
        body, comm=comm, name=name, grid=(s // tm,),
        in_specs=[row(f), row(f), _resident(wg.shape), _resident(wu.shape), row(d), _whole((1, d)), row(d)],
        out_specs=[row(d), row(d), _whole((1, d))],
        out_shape=[jax.ShapeDtypeStruct((s, d), F32), jax.ShapeDtypeStruct((s, d), BF16), jax.ShapeDtypeStruct((1, d), F32)],
        compiler_params=_params("arbitrary"),
    )(dg, dup, wg, wu, h, gain, dh)


def _odd_bwd(dhb, dh, wout, pre, gate, ws, wst, sgu, win, h, gmix, tm, comm=None):
    s, d = h.shape
    e = win.shape[1]
    cw = e // 2
    heads = ws.shape[0]
    ni = s // tm

    def body(dhb_ref, dh_ref, wout_ref, pre_ref, gate_ref, ws_ref, wst_ref, sgu_ref, win_ref, h_ref, gmix_ref,
             dpre_ref, dho_ref, dhob_ref, dgain_ref, dsgu_ref, dws_ref, db_ref, vbuf, gacc):
        i = pl.program_id(0)
        first = i == 0
        dmixo = _nt(dhb_ref[...], wout_ref[...])
        pre = pre_ref[...].astype(F32)
        p = _gelu(pre)
        u, v = p[:, :cw], p[:, cw:]
        sgu = sgu_ref[...]
        rv = lax.rsqrt(jnp.mean(v * v, axis=-1, keepdims=True) + EPS)
        vh = v * rv
        vn = (vh * sgu).astype(BF16)
        du = dmixo * gate_ref[...].astype(F32)
        dgate = dmixo * u
        dgate_b = dgate.astype(BF16)
        gsum = dgate[0:CHUNK, :]
        for n in range(1, tm // CHUNK):
            gsum = gsum + dgate[n * CHUNK:(n + 1) * CHUNK, :]
        _acc(gacc, gsum, first)
        for hd in range(heads):
            cols = slice(hd * CHUNK, (hd + 1) * CHUNK)
            dws = None
            for n in range(tm // CHUNK):
                rows = slice(n * CHUNK, (n + 1) * CHUNK)
                vbuf[rows, cols] = _nn(wst_ref[hd], dgate_b[rows, cols])
                part = _nt(dgate_b[rows, cols], vn[rows, cols])
                dws = part if dws is None else dws + part
            _acc(dws_ref.at[hd], dws, first)
        dvn = vbuf[...]
        _acc(dsgu_ref, jnp.sum(dvn * vh, axis=0, keepdims=True), first)
        dvh = dvn * sgu
        dv = rv * (dvh - vh * jnp.mean(dvh * vh, axis=-1, keepdims=True))
        dpre = (jnp.concatenate([du, dv], axis=1) * _gelu_grad(pre)).astype(BF16)
        dpre_ref[...] = dpre
        dx, dgain = _rms_bwd(_nt(dpre, win_ref[...]), h_ref[...], gmix_ref[...])
        dho = dh_ref[...] + dx
        dho_ref[...] = dho
        dhob_ref[...] = dho.astype(BF16)
        _acc(dgain_ref, dgain, first)

        @pl.when(i == ni - 1)
        def _():
            ones = jnp.ones((8, CHUNK), F32)
            for hd in range(heads):
                tot = lax.dot_general(ones, gacc[:, hd * CHUNK:(hd + 1) * CHUNK], (((1,), (1,)), ((), ())),
                                      preferred_element_type=F32, precision=lax.Precision.HIGHEST)
                db_ref[hd:hd + 1, :] = tot[0:1, :]

    row = lambda c: pl.BlockSpec((tm, c), lambda i: (i, 0))
    return _pcall(
        body, comm=comm, name="odd_bwd", grid=(ni,),
        in_specs=[row(d), row(d), _whole(wout.shape), row(e), row(cw), _whole(ws.shape), _whole(wst.shape), _whole(sgu.shape),
                  _whole(win.shape), row(d), _whole((1, d))],
        out_specs=[row(e), row(d), row(d), _whole((1, d)), _whole((1, cw)), _whole(ws.shape), _whole((heads, CHUNK))],
        out_shape=[jax.ShapeDtypeStruct((s, e), BF16), jax.ShapeDtypeStruct((s, d), F32), jax.ShapeDtypeStruct((s, d), BF16),
                   jax.ShapeDtypeStruct((1, d), F32), jax.ShapeDtypeStruct((1, cw), F32), jax.ShapeDtypeStruct(ws.shape, F32),
                   jax.ShapeDtypeStruct((heads, CHUNK), F32)],
        scratch_shapes=[pltpu.VMEM((tm, cw), F32), pltpu.VMEM((CHUNK, cw), F32)],
        compiler_params=_params("arbitrary"),
    )(dhb, dh, wout, pre, gate, ws, wst, sgu, win, h, gmix)


def _even_bwd(dhb, dh, wout, proj, cq, pooled, conva, wpool, wpoolt, pscale, win, x, gmix, tm, comm=None):
    s, d = x.shape
    e = win.shape[1]
    aw = e // 4
    ni = s // tm

    def body(dhb_ref, dh_ref, wout_ref, proj_ref, cq_ref, pooled_ref, ca_ref, wp_ref, wpt_ref, ps_ref, win_ref, x_ref, gmix_ref,
             dproj_ref, dx_ref, dgain_ref, dca_ref, dwp_ref, dps_ref, cbuf, ebuf):
        i = pl.program_id(0)
        first = i == 0

        @pl.when(first)
        def _():
            cbuf[tm:tm + HALO, :] = jnp.zeros((HALO, aw), F32)
            ebuf[tm:tm + HALO, :] = jnp.zeros((HALO, aw), F32)

        dmix = _nt(dhb_ref[...], wout_ref[...])
        dy_a, dy_b = dmix[:, :aw], dmix[:, aw:]
        proj = proj_ref[...].astype(F32)
        a_b, a_c, a_v = (proj[:, k * aw:(k + 1) * aw] for k in range(3))
        da_b = dy_a * cq_ref[...].astype(F32)
        dcq = dy_a * a_b
        cbuf[0:tm, :] = dcq
        cext = cbuf[...]
        s1 = _shift(cext, 1, False)[0:tm, :]
        s2 = _shift(cext, 2, False)[0:tm, :]
        q = a_c * a_v
        for k, shifted in enumerate((s2, s1, dcq)):
            _acc(dca_ref.at[k:k + 1, :], jnp.sum(shifted * q, axis=0, keepdims=True), first)
        dq = ca_ref[2:3, :] * dcq + ca_ref[1:2, :] * s1 + ca_ref[0:1, :] * s2
        da_c = dq * a_v
        da_v = dq * a_c
        dps, dpool = [], []
        for g, w in enumerate(WINDOWS):
            cols = slice(g * LANE, (g + 1) * LANE)
            pooled = pooled_ref[:, cols]
            mixed = _nn(pooled, wp_ref[g])
            dps.append(jnp.sum(dy_b[:, cols] * mixed, axis=0, keepdims=True))
            dmixed = (dy_b[:, cols] * ps_ref[:, cols]).astype(BF16)
            _acc(dwp_ref.at[g], _tn(pooled, dmixed), first)
            dp = _nn(dmixed, wpt_ref[g])
            dpool.append(dp)
            ebuf[0:tm, cols] = dp / _counts((ni - 1 - i) * tm, tm, w)
        _acc(dps_ref, jnp.concatenate(dps, axis=1), first)
        eext = ebuf[...]
        dzs = []
        for g, w in enumerate(WINDOWS):
            cols = slice(g * LANE, (g + 1) * LANE)
            dzs.append(_window_sum(eext[:, cols], w, False)[0:tm, :] - dpool[g])
        dproj = jnp.concatenate([da_b, da_c, da_v] + dzs, axis=1).astype(BF16)
        dproj_ref[...] = dproj
        dx, dgain = _rms_bwd(_nt(dproj, win_ref[...]), x_ref[...], gmix_ref[...])
        dx_ref[...] = dh_ref[...] + dx
        _acc(dgain_ref, dgain, first)
        cbuf[tm:tm + HALO, :] = cbuf[0:HALO, :]
        ebuf[tm:tm + HALO, :] = ebuf[0:HALO, :]

    row = lambda c: pl.BlockSpec((tm, c), lambda i: (ni - 1 - i, 0))
    return _pcall(
        body, comm=comm, name="even_bwd", grid=(ni,),
        in_specs=[row(d), row(d), _whole(wout.shape), row(e), row(aw), row(aw), _whole(conva.shape), _whole(wpool.shape),
                  _whole(wpoolt.shape), _whole(pscale.shape), _whole(win.shape), row(d), _whole((1, d))],
        out_specs=[row(e), row(d), _whole((1, d)), _whole(conva.shape), _whole(wpool.shape), _whole(pscale.shape)],
        out_shape=[jax.ShapeDtypeStruct((s, e), BF16), jax.ShapeDtypeStruct((s, d), F32), jax.ShapeDtypeStruct((1, d), F32),
                   jax.ShapeDtypeStruct(conva.shape, F32), jax.ShapeDtypeStruct(wpool.shape, F32),
                   jax.ShapeDtypeStruct(pscale.shape, F32)],
        scratch_shapes=[pltpu.VMEM((tm + HALO, aw), F32), pltpu.VMEM((tm + HALO, aw), F32)],
        compiler_params=_params("arbitrary"),
    )(dhb, dh, wout, proj, cq, pooled, conva, wpool, wpoolt, pscale, win, x, gmix)


def _wgrad(a, b, tk, ts, name, comm=None):
    s, ka = a.shape
    nb = b.shape[1]
    nt = s // ts

    def body(a_ref, b_ref, o_ref, acc):
        t = pl.program_id(1)
        _acc(acc, _tn(a_ref[...], b_ref[...]), t == 0)

        @pl.when(t == nt - 1)
        def _():
            o_ref[...] = acc[...].astype(BF16)

    return _pcall(
        body, comm=comm, name=name, grid=(ka // tk, nt),
        in_specs=[pl.BlockSpec((ts, tk), lambda k, t: (t, k)),
                  _resident((s, nb)) if nt == 1 else pl.BlockSpec((ts, nb), lambda k, t: (t, 0))],
        out_specs=pl.BlockSpec((tk, nb), lambda k, t: (k, 0)),
        out_shape=jax.ShapeDtypeStruct((ka, nb), BF16),
        scratch_shapes=[pltpu.VMEM((tk, nb), F32)],
        compiler_params=_params("arbitrary", "arbitrary"),
    )(a, b)


def _adamw(parts, w, m, v, tr, name, layer=None, into=None, carried=()):
    r, c = w.shape[-2:]
    n, rp, cp = parts.shape
    assert rp >= r and r % tr == 0
    prev, carried = ([] if into is None else list(into)), list(carried)

    def body(p_ref, w_ref, m_ref, v_ref, *rest):
        g_ref, d_ref, mo_ref, vo_ref = rest[len(prev) + len(carried):len(prev) + len(carried) + 4]
        g = p_ref[0, 0:tr, 0:c].astype(F32)
        for j in range(1, n):
            g = g + p_ref[j, 0:tr, 0:c].astype(F32)
        g_ref[...] = g
        mn = ADAM_B1 * m_ref[...] + (1.0 - ADAM_B1) * g
        vn = ADAM_B2 * v_ref[...] + (1.0 - ADAM_B2) * (g * g)
        mo_ref[...] = mn
        vo_ref[...] = vn
        m_hat = mn / (1.0 - ADAM_B1 ** ADAM_STEP)
        v_hat = vn / (1.0 - ADAM_B2 ** ADAM_STEP)
        d_ref[...] = -ADAM_LR * (m_hat / (jnp.sqrt(v_hat) + ADAM_EPS) + ADAM_WD * w_ref[...])

    if layer is None:
        row = pl.BlockSpec((tr, c), lambda i: (i, 0))
    else:
        row = pl.BlockSpec((None, tr, c), lambda i: (layer, i, 0))
    out = jax.ShapeDtypeStruct(w.shape, F32)
    untouched = pl.BlockSpec(memory_space=pl.ANY)
    aliases = {4 + k: k for k in range(len(prev))}
    aliases.update({4 + len(prev) + k: 4 + k for k in range(len(carried))})
    res = _pcall(
        body, name=name, grid=(r // tr,),
        in_specs=[pl.BlockSpec((n, tr, cp), lambda i: (0, i, 0)), row, row, row] + [untouched] * (len(prev) + len(carried)),
        out_specs=[row, row, row, row] + [untouched] * len(carried),
        out_shape=[out, out, out, out] + [jax.ShapeDtypeStruct(a.shape, a.dtype) for a in carried],
        input_output_aliases=aliases,
        compiler_params=_params("arbitrary"),
    )(parts, w, m, v, *prev, *carried)
    return (res[:4], res[4:]) if carried else res


def _pair_add(grad, other, spec, core, name):
    axis, width = spec
    slot = other.shape[1:]

    def body(core_ref, g_ref, o_ref, out_ref):
        out_ref[...] = (g_ref[...].astype(F32) + o_ref[...].astype(F32)).astype(BF16)

    if axis == 0:
        gspec = pl.BlockSpec(slot, lambda q, core_ref: (2 * q + core_ref[0], 0))
    else:
        gspec = pl.BlockSpec(slot, lambda q, core_ref: (0, 2 * q + core_ref[0]))
    per_chip = pl.BlockSpec((None,) + slot, lambda q, core_ref: (q, 0, 0))
    return _pcall(
        body, name=name,
        grid_spec=pltpu.PrefetchScalarGridSpec(num_scalar_prefetch=1, grid=(N_CHIP,), in_specs=[gspec, per_chip], out_specs=per_chip),
        out_shape=jax.ShapeDtypeStruct(other.shape, BF16),
        compiler_params=_params("arbitrary"),
    )(core, grad, other)


def _hbm():
    return pl.BlockSpec(memory_space=pltpu.HBM)


def _window(ref, spec, j):
    axis, width = spec
    start = pl.multiple_of(j * width, width)
    return ref.at[(slice(None),) * axis + (pl.ds(start, width),)]


def _here():
    return lax.axis_index("x"), lax.axis_index("y"), lax.axis_index("c")


class _Gather:
    def __init__(self, shards, specs, fulls):
        n = len(shards)
        self.ins, self.specs, self.out_shape = list(shards), list(specs), list(fulls)
        self.sems = [pltpu.SemaphoreType.DMA((7 * n,)), pltpu.SemaphoreType.DMA((7 * n,)), pltpu.SemaphoreType.DMA((n,))]

    def _plan(self, ins, outs, sems):
        send_sems, recv_sems, local_sems = sems
        x, y, c = _here()
        me, sibling = (x, y, c), (x, y, 1 - c)
        chips = [(1 - x, y), (x, 1 - y), (1 - x, 1 - y)]

        def slot(t, dev):
            return _window(outs[t], self.specs[t], 4 * dev[0] + 2 * dev[1] + dev[2])

        def copy(t, k, block, to, src=None):
            return pltpu.make_async_remote_copy(
                src_ref=slot(t, block) if src is None else src, dst_ref=slot(t, block),
                send_sem=send_sems.at[7 * t + k], recv_sem=recv_sems.at[7 * t + k], device_id=to, device_id_type=MESH)

        plan = []
        for t in range(len(ins)):
            plan.append(dict(
                mine=pltpu.make_async_copy(ins[t], slot(t, me), local_sems.at[t]),
                first=[copy(t, 0, me, sibling, src=ins[t])] + [copy(t, 1 + j, me, (*q, c), src=ins[t]) for j, q in enumerate(chips)],
                over_ici=[copy(t, 1 + j, (*q, c), me) for j, q in enumerate(chips)],
                passed=[copy(t, 4 + j, (*q, c), sibling) for j, q in enumerate(chips)],
                from_sibling=[copy(t, 0, sibling, me)] + [copy(t, 4 + j, (*q, 1 - c), me) for j, q in enumerate(chips)]))
        return plan

    def start(self, ins, outs, sems):
        for p in self._plan(ins, outs, sems):
            p["mine"].start()
            for cp in p["first"]:
                cp.start()

    def middle(self, ins, outs, sems):
        for p in self._plan(ins, outs, sems):
            for arrived, onward in zip(p["over_ici"], p["passed"]):
                arrived.wait_recv()
                onward.start()

    def finish(self, ins, outs, sems):
        plan = self._plan(ins, outs, sems)
        for p in plan:
            for cp in p["from_sibling"]:
                cp.wait_recv()
        for p in plan:
            for cp in p["first"] + p["passed"]:
                cp.wait_send()
            p["mine"].wait()


class _PairExchange:
    def __init__(self, grads, specs):
        n = len(grads)
        self.ins, self.specs = list(grads), list(specs)
        self.out_shape = [jax.ShapeDtypeStruct((N_CHIP,) + a.shape[:sp[0]] + (sp[1],) + a.shape[sp[0] + 1:], a.dtype)
                          for a, sp in zip(grads, specs)]
        self.sems = [pltpu.SemaphoreType.DMA((n,)), pltpu.SemaphoreType.DMA((n,))]

    def start(self, ins, outs, sems):
        send_sems, recv_sems = sems
        x, y, c = _here()
        for t in range(len(ins)):
            for q in range(N_CHIP):
                pltpu.make_async_remote_copy(
                    src_ref=_window(ins[t], self.specs[t], 2 * q + (1 - c)), dst_ref=outs[t].at[q],
                    send_sem=send_sems.at[t], recv_sem=recv_sems.at[t], device_id=(x, y, 1 - c), device_id_type=MESH).start()

    def middle(self, ins, outs, sems):
        pass

    def finish(self, ins, outs, sems):
        send_sems, recv_sems = sems
        x, y, c = _here()
        for t in range(len(ins)):
            every = pltpu.make_async_remote_copy(src_ref=outs[t], dst_ref=outs[t], send_sem=send_sems.at[t],
                                                 recv_sem=recv_sems.at[t], device_id=(x, y, 1 - c), device_id_type=MESH)
            every.wait_send()
            every.wait_recv()


class _ChipExchange:
    def __init__(self, pairs, slotted=(), whole=()):
        self.ins = list(pairs) + list(slotted) + list(whole)
        self.npair, self.nslot = len(pairs), len(pairs) + len(slotted)
        n = len(self.ins)
        self.out_shape = ([jax.ShapeDtypeStruct(a.shape, a.dtype) for a in list(pairs) + list(slotted)]
                          + [jax.ShapeDtypeStruct((N_DEV,) + a.shape, a.dtype) for a in whole])
        self.sems = [pltpu.SemaphoreType.DMA((7 * n,)), pltpu.SemaphoreType.DMA((7 * n,)), pltpu.SemaphoreType.DMA((n,))]

    def _plan(self, ins, outs, sems):
        send_sems, recv_sems, local_sems = sems
        npair, nslot = self.npair, self.nslot
        x, y, c = _here()
        me, chip = 4 * x + 2 * y + c, 2 * x + y
        chips = [(1 - x, y), (x, 1 - y), (1 - x, 1 - y)]
        peers = [(x, y, 1 - c)] + [(*q, c) for q in chips] + [(*q, 1 - c) for q in chips]

        def index(dev):
            return 4 * dev[0] + 2 * dev[1] + dev[2]

        def copy(t, k, arriving):
            peer = peers[k]
            if t < npair:
                src, mine, theirs = ins[t].at[2 * peer[0] + peer[1]], chip, 2 * peer[0] + peer[1]
            else:
                src, mine, theirs = (ins[t].at[index(peer)] if t < nslot else ins[t]), me, index(peer)
            return pltpu.make_async_remote_copy(
                src_ref=src, dst_ref=outs[t].at[theirs if arriving else mine],
                send_sem=send_sems.at[7 * t + k], recv_sem=recv_sems.at[7 * t + k], device_id=peer, device_id_type=MESH)

        own, sent, arriving = [], [], []
        for t in range(len(ins)):
            fan = range(1, 4) if t < npair else range(7)
            if t < npair:
                own.append(pltpu.make_async_copy(ins[t].at[chip], outs[t].at[chip], local_sems.at[t]))
            else:
                own.append(pltpu.make_async_copy(ins[t].at[me] if t < nslot else ins[t], outs[t].at[me], local_sems.at[t]))
            sent += [copy(t, k, False) for k in fan]
            arriving += [copy(t, k, True) for k in fan]
        return own, sent, arriving

    def start(self, ins, outs, sems):
        own, sent, _ = self._plan(ins, outs, sems)
        for cp in own + sent:
            cp.start()

    def middle(self, ins, outs, sems):
        pass

    def finish(self, ins, outs, sems):
        own, sent, arriving = self._plan(ins, outs, sems)
        for cp in arriving:
            cp.wait_recv()
        for cp in sent:
            cp.wait_send()
        for cp in own:
            cp.wait()


class _Jobs:
    def __init__(self, *jobs):
        self.jobs = jobs
        self.ins = [a for j in jobs for a in j.ins]
        self.out_shape = [a for j in jobs for a in j.out_shape]
        self.sems = [a for j in jobs for a in j.sems]

    def _split(self, ins, outs, sems):
        i = o = s = 0
        for j in self.jobs:
            yield j, ins[i:i + len(j.ins)], outs[o:o + len(j.out_shape)], sems[s:s + len(j.sems)]
            i, o, s = i + len(j.ins), o + len(j.out_shape), s + len(j.sems)

    def start(self, ins, outs, sems):
        for j, a, b, c in self._split(ins, outs, sems):
            j.start(a, b, c)

    def middle(self, ins, outs, sems):
        for j, a, b, c in self._split(ins, outs, sems):
            j.middle(a, b, c)

    def finish(self, ins, outs, sems):
        for j, a, b, c in self._split(ins, outs, sems):
            j.finish(a, b, c)

    def results(self, outs):
        return [b for _, _, b, _ in self._split((), outs, ())]


def _alone(job, name):
    return _pcall(lambda: None, comm=job, name=name, in_specs=[], out_specs=[], out_shape=[])()[1]


SMALL_ROWS = 24
REP_COLS = 1024


def _pad_to(a, rows, cols):
    return jnp.pad(a, ((0, rows - a.shape[0]), (0, cols - a.shape[1])))


def _pack_small(conv_a, sgu_norm, conv_ffn, cols):
    return jnp.concatenate([_pad_to(conv_a, 8, cols), _pad_to(sgu_norm, 8, cols),
                            _pad_to(conv_ffn.reshape(-1, conv_ffn.shape[-1]), 8, cols)], axis=0)


def _unpack_small(p, ca_w, sg_w, cf_w):
    return p[0:3, 0:ca_w], p[8:9, 0:sg_w], p[16:22, 0:cf_w].reshape(2, 3, cf_w)


def _tile_rows(rows):
    return -(-rows // 8) * 8


def _pack_rows(parts):
    return jnp.concatenate([_pad_to(a, _tile_rows(a.shape[0]), REP_COLS) for a in parts], axis=0)


def _unpack_rows(p, shapes):
    out, r0 = [], 0
    for r, c in shapes:
        out.append(p[r0:r0 + r, 0:c])
        r0 += _tile_rows(r)
    return out


def _rep_late(norm_mix0, norm_ffn0, pool_scale, b_conv0, w_pool):
    return _pack_rows([norm_mix0, norm_ffn0, pool_scale.reshape(1, -1), _pad_to(b_conv0, 1, 3 * REP_COLS).reshape(3, REP_COLS),
                       w_pool.reshape(-1, REP_COLS)])


def _rep_early(norm_mix1, norm_ffn1, final_norm, b_spatial, b_conv1, loss):
    return _pack_rows([norm_mix1, norm_ffn1, final_norm.reshape(1, -1), b_spatial.reshape(1, -1),
                       _pad_to(b_conv1, 1, 3 * REP_COLS).reshape(3, REP_COLS), loss])


def _unpack_rep(late, early, wsp, like):
    f = like["b_conv_ffn"].shape[1]
    nm0, nf0, ps, bc0, wp = _unpack_rows(late, [(1, REP_COLS), (1, REP_COLS), (1, like["pool_scale"].shape[1]), (3, REP_COLS),
                                                (like["w_pool"].size // REP_COLS, REP_COLS)])
    nm1, nf1, fin, bs, bc1, _ = _unpack_rows(early, [(1, REP_COLS)] * 4 + [(3, REP_COLS), (1, REP_COLS)])
    return {
        "norm_mix": jnp.concatenate([nm0, nm1]), "norm_ffn": jnp.concatenate([nf0, nf1]), "final_norm": fin[0], "pool_scale": ps,
        "b_spatial": bs.reshape(like["b_spatial"].shape),
        "b_conv_ffn": jnp.concatenate([bc0.reshape(1, -1), bc1.reshape(1, -1)])[:, 0:f],
        "w_pool": wp.reshape(like["w_pool"].shape), "w_spatial": wsp.reshape(like["w_spatial"].shape),
    }


def _pad_slots(a, width, padded):
    a = a.reshape(*a.shape[:-1], N_DEV, width)
    a = jnp.pad(a, ((0, 0),) * (a.ndim - 1) + ((0, padded - width),))
    return a.reshape(*a.shape[:-2], N_DEV * padded)


def _unpad_slots(a, width, padded):
    a = a.reshape(*a.shape[:-1], N_DEV, padded)[..., 0:width]
    return a.reshape(*a.shape[:-2], N_DEV * width)


def kernel(x, norm_mix, norm_ffn, final_norm, w_in_even, conv_a, w_pool, pool_scale, w_out_even, w_in_odd, sgu_norm, w_spatial, b_spatial, w_out_odd, w_ffn_gate, w_ffn_up, conv_ffn, b_conv_ffn, w_ffn_down, loss_target, m_norm_mix, m_norm_ffn, m_final_norm, m_w_in_even, m_conv_a, m_w_pool, m_pool_scale, m_w_out_even, m_w_in_odd, m_sgu_norm, m_w_spatial, m_b_spatial, m_w_out_odd, m_w_ffn_gate, m_w_ffn_up, m_conv_ffn, m_b_conv_ffn, m_w_ffn_down, v_norm_mix, v_norm_ffn, v_final_norm, v_w_in_even, v_conv_a, v_w_pool, v_pool_scale, v_w_out_even, v_w_in_odd, v_sgu_norm, v_w_spatial, v_b_spatial, v_w_out_odd, v_w_ffn_gate, v_w_ffn_up, v_conv_ffn, v_b_conv_ffn, v_w_ffn_down):
    s, d = x.shape[1], x.shape[2]
    x2, target = x[0], loss_target[0]
    tm = min(512, s)
    tm_wide = min(2048, s)
    tk_wide = 1024
    tn_fwd = 512
    row = lambda a: a.reshape(1, -1)
    ein, ro = w_in_even.shape[2], w_out_even.shape[1]
    fs = w_ffn_gate.shape[2]
    fsp = -(-fs // LANE) * LANE
    fp = N_DEV * fsp
    full = lambda shape, dtype=BF16: jax.ShapeDtypeStruct(shape, dtype)

    wg_s = [jnp.pad(w_ffn_gate[l].T, ((0, fsp - fs), (0, 0))).astype(BF16) for l in range(2)]
    wu_s = [jnp.pad(w_ffn_up[l].T, ((0, fsp - fs), (0, 0))).astype(BF16) for l in range(2)]
    wd_s = [jnp.pad(w_ffn_down[l], ((0, fsp - fs), (0, 0))).astype(BF16) for l in range(2)]
    small_s = _pack_small(conv_a[0], sgu_norm, conv_ffn, fsp)[None]
    in_spec, out_spec, gu_spec, down_spec = (1, ein), (0, ro), (0, fsp), (0, fsp)
    full_in, full_out, full_gu, full_down = full((d, N_DEV * ein)), full((N_DEV * ro, d)), full((fp, d)), full((fp, d))
    win_e, wout_e, gsmall = _alone(_Gather([w_in_even[0].astype(BF16), w_out_even[0].astype(BF16), small_s], [in_spec, out_spec, (0, 1)],
                                           [full_in, full_out, full((N_DEV, SMALL_ROWS, fsp), F32)]), "gather_mix0")
    ca_full = jnp.moveaxis(gsmall[:, 0:3, 0:conv_a.shape[2]], 0, 1).reshape(3, -1)
    sgu_full = gsmall[:, 8, 0:sgu_norm.shape[1]].reshape(1, -1)
    cf_full = jnp.moveaxis(gsmall[:, 16:22, :].reshape(N_DEV, 2, 3, fsp), 0, 2).reshape(2, 3, fp)
    cb_full = _pad_slots(b_conv_ffn, fs, fsp)

    tril = jnp.tril(jnp.ones((CHUNK, CHUNK), F32))
    ws_m = w_spatial[0] * tril
    ws_b = ws_m.astype(BF16)
    wst_b = jnp.swapaxes(ws_m, 1, 2).astype(BF16)
    bfull = jnp.repeat(b_spatial[0].T, CHUNK, axis=1)
    wpool_b = w_pool[0].astype(BF16)
    wpoolt_b = jnp.swapaxes(w_pool[0], 1, 2).astype(BF16)

    (xn0, proj0, cq0, pooled0, mix0, h1, hn0), (wg0,) = _even_fwd(
        x2, norm_mix[0:1], norm_ffn[0:1], win_e, ca_full, wpool_b, pool_scale, wout_e, tm,
        comm=_Gather([wg_s[0]], [gu_spec], [full_gu]))
    (g0, gc0), (wu0,) = _ffn_gate(hn0, wg0, cf_full[0], cb_full[0:1], tm_wide, tn_fwd, "ffn_gate_l0", comm=_Gather([wu_s[0]], [gu_spec], [full_gu]))
    (a0, sl0, ud0), (wd0,) = _ffn_up(hn0, wu0, gc0, tm_wide, tn_fwd, "ffn_up_l0", comm=_Gather([wd_s[0]], [down_spec], [full_down]))
    (h2, xn1), (win_o, wout_o) = _ffn_fwd2(a0, wd0, h1, norm_mix[1:2], tm, "ffn_fwd2_l0", comm=_Gather(
        [w_in_odd[0].astype(BF16), w_out_odd[0].astype(BF16)], [in_spec, out_spec], [full_in, full_out]))
    (pre1, gate1, mixo1, h3, hn1), (wg1,) = _odd_fwd(xn1, h2, win_o, sgu_full, ws_b, bfull, wout_o, norm_ffn[1:2], tm,
                                                    comm=_Gather([wg_s[1]], [gu_spec], [full_gu]))
    (g1, gc1), (wu1,) = _ffn_gate(hn1, wg1, cf_full[1], cb_full[1:2], tm_wide, tn_fwd, "ffn_gate_l1", comm=_Gather([wu_s[1]], [gu_spec], [full_gu]))
    (a1, sl1, ud1), (wd1,) = _ffn_up(hn1, wu1, gc1, tm_wide, tn_fwd, "ffn_up_l1", comm=_Gather([wd_s[1]], [down_spec], [full_down]))
    h4, _ = _ffn_fwd2(a1, wd1, h3, row(final_norm), tm, "ffn_fwd2_l1")

    core = lax.axis_index("c").astype(jnp.int32).reshape(1)
    ts = min(1024, s)
    dh4, dh4b, d_final, lossvec = _loss_bwd(h4, row(final_norm), target, tm)
    gd1 = _wgrad(a1, dh4b, tk_wide, s, "wgrad_down_l1")
    (dg1, dup1, dcw1, dcb1), (o_d1,) = _ffn_bwd1(dh4b, g1, sl1, ud1, wd1, cf_full[1], tm_wide, tn_fwd, "ffn_bwd1_l1",
                                                 comm=_PairExchange([gd1], [down_spec]))
    p_d1 = _pair_add(gd1, o_d1, down_spec, core, "pair_add_down_l1")
    gg1 = _wgrad(dg1, hn1, tk_wide, s, "wgrad_gate_l1")
    gu1, (o_g1,) = _wgrad(dup1, hn1, tk_wide, s, "wgrad_up_l1", comm=_PairExchange([gg1], [gu_spec]))
    p_g1 = _pair_add(gg1, o_g1, gu_spec, core, "pair_add_gate_l1")
    jobs = _Jobs(_ChipExchange([p_d1]), _PairExchange([gu1], [gu_spec]))
    (dh3, dh3b, d_nffn1), res = _ffn_bwd2(dg1, dup1, wg1, wu1, h3, norm_ffn[1:2], dh4, tm, "ffn_bwd2_l1", comm=jobs)
    (s_d1,), (o_u1,) = jobs.results(res)
    p_u1 = _pair_add(gu1, o_u1, gu_spec, core, "pair_add_up_l1")
    (dpre1, dh2, dh2b, d_nmix1, d_sgu, d_ws, d_b), (s_g1, s_u1) = _odd_bwd(
        dh3b, dh3, wout_o, pre1, gate1, ws_b, wst_b, sgu_full, win_o, h2, norm_mix[1:2], tm, comm=_ChipExchange([p_g1, p_u1]))
    gi1 = _wgrad(xn1, dpre1, tk_wide, ts, "wgrad_in_odd")
    go1, (o_i1,) = _wgrad(mixo1, dh3b, tk_wide, ts, "wgrad_out_odd", comm=_PairExchange([gi1], [in_spec]))
    p_i1 = _pair_add(gi1, o_i1, in_spec, core, "pair_add_in_odd")
    d_wsp = (d_ws * tril).reshape(-1, REP_COLS).astype(BF16)
    jobs = _Jobs(_PairExchange([go1], [out_spec]), _ChipExchange([], [], [d_wsp]))
    gd0, res = _wgrad(a0, dh2b, tk_wide, s, "wgrad_down_l0", comm=jobs)
    (o_o1,), (r_wsp,) = jobs.results(res)
    p_o1 = _pair_add(go1, o_o1, out_spec, core, "pair_add_out_odd")
    d_early = _rep_early(d_nmix1, d_nffn1, d_final, d_b, _unpad_slots(dcb1, fs, fsp), lossvec)
    jobs = _Jobs(_ChipExchange([p_i1, p_o1], [], [d_early]), _PairExchange([gd0], [down_spec]))
    (dg0, dup0, dcw0, dcb0), res = _ffn_bwd1(dh2b, g0, sl0, ud0, wd0, cf_full[0], tm_wide, tn_fwd, "ffn_bwd1_l0", comm=jobs)
    (s_i1, s_o1, r_early), (o_d0,) = jobs.results(res)
    p_d0 = _pair_add(gd0, o_d0, down_spec, core, "pair_add_down_l0")
    gg0, (s_d0,) = _wgrad(dg0, hn0, tk_wide, s, "wgrad_gate_l0", comm=_ChipExchange([p_d0]))
    gu0, (o_g0,) = _wgrad(dup0, hn0, tk_wide, s, "wgrad_up_l0", comm=_PairExchange([gg0], [gu_spec]))
    p_g0 = _pair_add(gg0, o_g0, gu_spec, core, "pair_add_gate_l0")
    jobs = _Jobs(_ChipExchange([p_g0]), _PairExchange([gu0], [gu_spec]))
    (dh1, dh1b, d_nffn0), res = _ffn_bwd2(dg0, dup0, wg0, wu0, h1, norm_ffn[0:1], dh2, tm, "ffn_bwd2_l0", comm=jobs)
    (s_g0,), (o_u0,) = jobs.results(res)
    p_u0 = _pair_add(gu0, o_u0, gu_spec, core, "pair_add_up_l0")
    go0 = _wgrad(mix0, dh1b, tk_wide, ts, "wgrad_out_even")
    jobs = _Jobs(_ChipExchange([p_u0]), _PairExchange([go0], [out_spec]))
    (dproj0, grad_x, d_nmix0, d_ca, d_wp, d_ps), res = _even_bwd(
        dh1b, dh1, wout_e, proj0, cq0, pooled0, ca_full, wpool_b, wpoolt_b, pool_scale, win_e, x2, norm_mix[0:1], tm, comm=jobs)
    (s_u0,), (o_o0,) = jobs.results(res)
    p_o0 = _pair_add(go0, o_o0, out_spec, core, "pair_add_out_even")
    d_small = jnp.stack([_pack_small(a, b, c, fsp) for a, b, c in zip(
        jnp.moveaxis(d_ca.reshape(3, N_DEV, -1), 1, 0), jnp.moveaxis(d_sgu.reshape(1, N_DEV, -1), 1, 0),
        jnp.moveaxis(jnp.stack([dcw0, dcw1]).reshape(2, 3, N_DEV, fsp), 2, 0))])
    d_late = _rep_late(d_nmix0, d_nffn0, d_ps, _unpad_slots(dcb0, fs, fsp), d_wp)
    gi0, (r_small, r_late) = _wgrad(xn0, dproj0, tk_wide, ts, "wgrad_in_even", comm=_ChipExchange([], [d_small], [d_late]))
    jobs = _Jobs(_PairExchange([gi0], [in_spec]), _ChipExchange([p_o0]))
    (o_i0,), (s_o0,) = jobs.results(_alone(jobs, "pair_exchange_in_even"))
    p_i0 = _pair_add(gi0, o_i0, in_spec, core, "pair_add_in_even")
    (s_i0,) = _alone(_ChipExchange([p_i0]), "chip_exchange_last")
    loss = jnp.sum(r_early[:, r_early.shape[1] - 8, :])

    out = {}
    out["w_in_even"], (grad_x,) = _adamw(s_i0, w_in_even[0], m_w_in_even[0], v_w_in_even[0], 256, "adamw_in_even", carried=[grad_x])
    out["w_out_even"] = _adamw(s_o0, w_out_even[0], m_w_out_even[0], v_w_out_even[0], ro // 2, "adamw_out_even")
    out["w_in_odd"] = _adamw(s_i1, w_in_odd[0], m_w_in_odd[0], v_w_in_odd[0], 256, "adamw_in_odd")
    out["w_out_odd"] = _adamw(s_o1, w_out_odd[0], m_w_out_odd[0], v_w_out_odd[0], ro // 2, "adamw_out_odd")
    tp = lambda a: jnp.swapaxes(a, 1, 2)
    for nm, s1, s0, w, m, v, back in (("w_ffn_gate", s_g1, s_g0, tp(w_ffn_gate), tp(m_w_ffn_gate), tp(v_w_ffn_gate), tp),
                                      ("w_ffn_up", s_u1, s_u0, tp(w_ffn_up), tp(m_w_ffn_up), tp(v_w_ffn_up), tp),
                                      ("w_ffn_down", s_d1, s_d0, w_ffn_down, m_w_ffn_down, v_w_ffn_down, lambda a: a)):
        l1 = _adamw(s1, w, m, v, fs // 2, "adamw_%s_l1" % nm, layer=1)
        out[nm] = [back(a) for a in _adamw(s0, w, m, v, fs // 2, "adamw_%s_l0" % nm, layer=0, into=l1)]
    small = _adamw(r_small, _pack_small(conv_a[0], sgu_norm, conv_ffn, fsp), _pack_small(m_conv_a[0], m_sgu_norm, m_conv_ffn, fsp),
                   _pack_small(v_conv_a[0], v_sgu_norm, v_conv_ffn, fsp), SMALL_ROWS, "adamw_small")
    no_loss = jnp.zeros((1, REP_COLS), F32)
    early = _adamw(r_early, *[_rep_early(nm[1:2], nf[1:2], fn, bs, bc[1:2], no_loss) for nm, nf, fn, bs, bc in (
        (norm_mix, norm_ffn, final_norm, b_spatial, b_conv_ffn), (m_norm_mix, m_norm_ffn, m_final_norm, m_b_spatial, m_b_conv_ffn),
        (v_norm_mix, v_norm_ffn, v_final_norm, v_b_spatial, v_b_conv_ffn))], r_early.shape[1], "adamw_replicated_early")
    wsp = _adamw(r_wsp, w_spatial.reshape(-1, REP_COLS), m_w_spatial.reshape(-1, REP_COLS), v_w_spatial.reshape(-1, REP_COLS),
                 r_wsp.shape[1], "adamw_w_spatial")
    late = _adamw(r_late, *[_rep_late(nm[0:1], nf[0:1], ps, bc[0:1], wp) for nm, nf, ps, bc, wp in (
        (norm_mix, norm_ffn, pool_scale, b_conv_ffn, w_pool), (m_norm_mix, m_norm_ffn, m_pool_scale, m_b_conv_ffn, m_w_pool),
        (v_norm_mix, v_norm_ffn, v_pool_scale, v_b_conv_ffn, v_w_pool))], r_late.shape[1], "adamw_replicated_late")

    names = ["norm_mix", "norm_ffn", "final_norm", "w_in_even", "conv_a", "w_pool", "pool_scale", "w_out_even", "w_in_odd", "sgu_norm",
             "w_spatial", "b_spatial", "w_out_odd", "w_ffn_gate", "w_ffn_up", "conv_ffn", "b_conv_ffn", "w_ffn_down"]
    like = {"norm_mix": norm_mix, "norm_ffn": norm_ffn, "final_norm": final_norm, "w_in_even": w_in_even, "conv_a": conv_a,
            "w_pool": w_pool, "pool_scale": pool_scale, "w_out_even": w_out_even, "w_in_odd": w_in_odd, "sgu_norm": sgu_norm,
            "w_spatial": w_spatial, "b_spatial": b_spatial, "w_out_odd": w_out_odd, "w_ffn_gate": w_ffn_gate, "w_ffn_up": w_ffn_up,
            "conv_ffn": conv_ffn, "b_conv_ffn": b_conv_ffn, "w_ffn_down": w_ffn_down}
    groups = []
    for k in range(4):
        ca_k, sg_k, cf_k = _unpack_small(small[k], conv_a.shape[2], sgu_norm.shape[1], conv_ffn.shape[2])
        vals = dict(_unpack_rep(late[k], early[k], wsp[k], like))
        vals.update(conv_a=ca_k, sgu_norm=sg_k, conv_ffn=cf_k)
        for nm in ("w_in_even", "w_in_odd", "w_out_even", "w_out_odd", "w_ffn_gate", "w_ffn_up", "w_ffn_down"):
            vals[nm] = out[nm][k]
        groups.append([vals[nm].reshape(like[nm].shape) for nm in names])
    return (loss, grad_x[None], *groups[0], *groups[1], *groups[2], *groups[3])
```

```python
import functools

import jax
import jax.numpy as jnp
from jax import lax
from jax.experimental import pallas as pl
from jax.experimental.pallas import tpu as pltpu

F32, BF16 = jnp.float32, jnp.bfloat16
EPS = 1e-6
WINDOWS = (2, 4, 8, 16)
HALO = 16
CHUNK = 128
N_DEV = 8
N_CHIP = 4
MESH = pl.DeviceIdType.MESH
VMEM_LIMIT = 56 * 2**20
LANE = 128
ADAM_LR, ADAM_B1, ADAM_B2, ADAM_EPS, ADAM_WD, ADAM_STEP = 0.001, 0.9, 0.999, 1e-08, 0.01, 10
SUB_ROWS = 256
LATE_NUM, LATE_DEN = 7, 8
INV_SQRT2 = 0.7071067811865476
INV_SQRT2PI = 0.3989422804014327


def _pcall(body, comm=None, **kw):
    if comm is None:
        return pl.pallas_call(body, **kw)
    in_specs, out_specs, out_shape = list(kw.pop("in_specs")), kw.pop("out_specs"), kw.pop("out_shape")
    single = not isinstance(out_shape, (list, tuple))
    out_specs, out_shape = ([out_specs], [out_shape]) if single else (list(out_specs), list(out_shape))
    scratch = list(kw.pop("scratch_shapes", []))
    grid = kw.get("grid", ())
    n_in, n_out, n_scr, c_in, c_out = len(in_specs), len(out_specs), len(scratch), len(comm.ins), len(comm.out_shape)

    def hosted(*refs):
        cuts = [0, n_in, n_in + c_in, n_in + c_in + n_out, n_in + c_in + n_out + c_out, n_in + c_in + n_out + c_out + n_scr, len(refs)]
        ins, cins, outs, couts, scr, sems = (refs[a:b] for a, b in zip(cuts[:-1], cuts[1:]))
        if grid:
            step, steps = 0, 1
            for axis, size in enumerate(grid):
                step, steps = step * size + pl.program_id(axis), steps * size
            pl.when(step == 0)(lambda: comm.start(cins, couts, sems))
            pl.when(step == (steps * LATE_NUM) // LATE_DEN)(lambda: comm.middle(cins, couts, sems))
            body(*ins, *outs, *scr)
            pl.when(step == steps - 1)(lambda: comm.finish(cins, couts, sems))
        else:
            comm.start(cins, couts, sems)
            comm.middle(cins, couts, sems)
            body(*ins, *outs, *scr)
            comm.finish(cins, couts, sems)

    call = pl.pallas_call(hosted, in_specs=in_specs + [_hbm()] * c_in, out_specs=out_specs + [_hbm()] * c_out,
                          out_shape=out_shape + list(comm.out_shape), scratch_shapes=scratch + list(comm.sems), **kw)

    def run(*args):
        res = call(*args, *comm.ins)
        own = res[0] if single else res[:n_out]
        return own, res[n_out:]

    return run


def _params(*sem):
    return pltpu.CompilerParams(dimension_semantics=sem, vmem_limit_bytes=VMEM_LIMIT)


def _whole(shape):
    return pl.BlockSpec(shape, lambda *_: (0,) * len(shape))


def _resident(shape):
    return pl.BlockSpec(shape, lambda *_: (0,) * len(shape), pipeline_mode=pl.Buffered(1))


def _rows(i, tm):
    return pl.ds(pl.multiple_of(i * tm, tm), tm)


def _nn(a, b):
    return jnp.dot(a, b, preferred_element_type=F32)


def _nt(a, b):
    return lax.dot_general(a, b, (((1,), (1,)), ((), ())), preferred_element_type=F32)


def _tn(a, b):
    return lax.dot_general(a, b, (((0,), (0,)), ((), ())), preferred_element_type=F32)


def _rms(x, gain):
    r = lax.rsqrt(jnp.mean(x * x, axis=-1, keepdims=True) + EPS)
    return x * r * gain


def _rms_bwd(dy, x, gain):
    r = lax.rsqrt(jnp.mean(x * x, axis=-1, keepdims=True) + EPS)
    xh = x * r
    dgain = jnp.sum(dy * xh, axis=0, keepdims=True)
    dxh = dy * gain
    dx = r * (dxh - xh * jnp.mean(dxh * xh, axis=-1, keepdims=True))
    return dx, dgain


def _gelu(x):
    return 0.5 * x * (1.0 + lax.erf(x * INV_SQRT2))


def _gelu_grad(x):
    return 0.5 * (1.0 + lax.erf(x * INV_SQRT2)) + x * jnp.exp(-0.5 * x * x) * INV_SQRT2PI


def _acc(ref, val, first):
    @pl.when(first)
    def _():
        ref[...] = val

    @pl.when(jnp.logical_not(first))
    def _():
        ref[...] += val


def _shift(ext, k, back):
    n = ext.shape[0]
    return pltpu.roll(ext, k if back else n - k, axis=0)


def _window_sum(ext, w, back):
    total, step = ext, 1
    while step < w:
        total = total + _shift(total, step, back)
        step *= 2
    return total


def _counts(row0, tm, w):
    pos1 = (row0 + lax.broadcasted_iota(jnp.int32, (tm, 1), 0) + 1).astype(F32)
    return jnp.minimum(pos1, float(w))


def _even_fwd(x, gmix, gffn, win, conva, wpool, pscale, wout, tm, comm=None):
    s, d = x.shape
    e = win.shape[1]
    aw = e // 4

    def body(x_ref, gmix_ref, gffn_ref, win_ref, ca_ref, wp_ref, ps_ref, wout_ref,
             xn_ref, proj_ref, cq_ref, pooled_ref, mix_ref, h_ref, hn_ref, qbuf, zbuf):
        i = pl.program_id(0)

        @pl.when(i == 0)
        def _():
            qbuf[0:HALO, :] = jnp.zeros((HALO, aw), F32)
            zbuf[0:HALO, :] = jnp.zeros((HALO, aw), F32)

        xv = x_ref[...]
        xn = _rms(xv, gmix_ref[...]).astype(BF16)
        xn_ref[...] = xn
        proj = _nn(xn, win_ref[...])
        proj_ref[...] = proj.astype(BF16)
        a_b, a_c, a_v, z = (proj[:, k * aw:(k + 1) * aw] for k in range(4))
        q = a_c * a_v
        qbuf[HALO:HALO + tm, :] = q
        qext = qbuf[...]
        cur = slice(HALO, HALO + tm)
        cq = ca_ref[2:3, :] * q + ca_ref[1:2, :] * _shift(qext, 1, True)[cur, :] + ca_ref[0:1, :] * _shift(qext, 2, True)[cur, :]
        cq_ref[...] = cq.astype(BF16)
        y_a = a_b * cq
        zbuf[HALO:HALO + tm, :] = z
        zext = zbuf[...]
        ys = []
        for g, w in enumerate(WINDOWS):
            cols = slice(g * LANE, (g + 1) * LANE)
            acc = _window_sum(zext[:, cols], w, True)[cur, :]
            pooled = (acc / _counts(i * tm, tm, w) - z[:, cols]).astype(BF16)
            pooled_ref[:, cols] = pooled
            ys.append(_nn(pooled, wp_ref[g]))
        y_b = jnp.concatenate(ys, axis=1) * ps_ref[...]
        mix = jnp.concatenate([y_a, y_b], axis=1).astype(BF16)
        mix_ref[...] = mix
        h = xv + _nn(mix, wout_ref[...])
        h_ref[...] = h
        hn_ref[...] = _rms(h, gffn_ref[...]).astype(BF16)
        qbuf[0:HALO, :] = qbuf[tm:tm + HALO, :]
        zbuf[0:HALO, :] = zbuf[tm:tm + HALO, :]

    row = lambda c: pl.BlockSpec((tm, c), lambda i: (i, 0))
    return _pcall(
        body, comm=comm, name="even_fwd", grid=(s // tm,),
        in_specs=[row(d), _whole((1, d)), _whole((1, d)), _whole(win.shape), _whole(conva.shape), _whole(wpool.shape),
                  _whole(pscale.shape), _whole(wout.shape)],
        out_specs=[row(d), row(e), row(aw), row(aw), row(d), row(d), row(d)],
        out_shape=[jax.ShapeDtypeStruct((s, d), BF16), jax.ShapeDtypeStruct((s, e), BF16), jax.ShapeDtypeStruct((s, aw), BF16),
                   jax.ShapeDtypeStruct((s, aw), BF16), jax.ShapeDtypeStruct((s, d), BF16), jax.ShapeDtypeStruct((s, d), F32),
                   jax.ShapeDtypeStruct((s, d), BF16)],
        scratch_shapes=[pltpu.VMEM((tm + HALO, aw), F32), pltpu.VMEM((tm + HALO, aw), F32)],
        compiler_params=_params("arbitrary"),
    )(x, gmix, gffn, win, conva, wpool, pscale, wout)


def _ffn_gate(hn, wgt, cw, cb, tm, tn, name, comm=None):
    s, d = hn.shape
    f = wgt.shape[0]
    sub = tm

    def body(hn_ref, wg_ref, cw_ref, cb_ref, g_ref, gc_ref, gbuf):
        i = pl.program_id(1)

        @pl.when(i == 0)
        def _():
            gbuf[0:HALO, :] = jnp.zeros((HALO, tn), F32)

        wg = wg_ref[...]
        for c in range(tm // sub):
            rows = pl.ds(c * sub, sub)
            g = _nt(hn_ref[pl.ds(pl.multiple_of(i * tm + c * sub, sub), sub), :], wg)
            g_ref[rows, :] = g.astype(BF16)
            gbuf[pl.ds(HALO + c * sub, sub), :] = g
            ext = gbuf[pl.ds(c * sub, sub + HALO), :]
            gc = (cw_ref[2:3, :] * g + cw_ref[1:2, :] * _shift(ext, 1, True)[HALO:, :]
                  + cw_ref[0:1, :] * _shift(ext, 2, True)[HALO:, :] + cb_ref[...])
            gc_ref[rows, :] = gc.astype(BF16)
        gbuf[0:HALO, :] = gbuf[tm:tm + HALO, :]

    tile = pl.BlockSpec((tm, tn), lambda j, i: (i, j))
    wcol = lambda r: pl.BlockSpec((r, tn), lambda j, i: (0, j))
    out = jax.ShapeDtypeStruct((s, f), BF16)
    return _pcall(
        body, comm=comm, name=name, grid=(f // tn, s // tm),
        in_specs=[_resident((s, d)), pl.BlockSpec((tn, d), lambda j, i: (j, 0)), wcol(3), wcol(1)],
        out_specs=[tile, tile], out_shape=[out, out],
        scratch_shapes=[pltpu.VMEM((tm + HALO, tn), F32)],
        compiler_params=_params("arbitrary", "arbitrary"),
    )(hn, wgt, cw, cb)


def _ffn_up(hn, wut, gc, tm, tn, name, comm=None):
    s, d = hn.shape
    f = wut.shape[0]
    sub = min(SUB_ROWS, tm)

    def body(hn_ref, wu_ref, gc_ref, a_ref, silu_ref, upds_ref):
        wu = wu_ref[...]
        for c in range(tm // sub):
            rows = pl.ds(c * sub, sub)
            up = _nt(hn_ref[pl.ds(pl.multiple_of(pl.program_id(1) * tm + c * sub, sub), sub), :], wu)
            gc = gc_ref[rows, :].astype(F32)
            sg = jax.nn.sigmoid(gc)
            silu = gc * sg
            silu_ref[rows, :] = silu.astype(BF16)
            upds_ref[rows, :] = (up * (sg * (1.0 - silu) + silu)).astype(BF16)
            a_ref[rows, :] = (silu * up).astype(BF16)

    tile = pl.BlockSpec((tm, tn), lambda j, i: (i, j))
    out = jax.ShapeDtypeStruct((s, f), BF16)
    return _pcall(
        body, comm=comm, name=name, grid=(f // tn, s // tm),
        in_specs=[_resident((s, d)), pl.BlockSpec((tn, d), lambda j, i: (j, 0)), tile],
        out_specs=[tile, tile, tile], out_shape=[out, out, out],
        compiler_params=_params("arbitrary", "arbitrary"),
    )(hn, wut, gc)


def _ffn_fwd2(a, wd, h, gain, tm, name, comm=None):
    s, d = h.shape
    f = a.shape[1]

    def body(a_ref, wd_ref, h_ref, gain_ref, ho_ref, hn_ref):
        wd_v, gain = wd_ref[...], gain_ref[...]
        sub = min(SUB_ROWS, tm)
        for c in range(tm // sub):
            rows = pl.ds(c * sub, sub)
            ho = h_ref[rows, :] + _nn(a_ref[rows, :], wd_v)
            ho_ref[rows, :] = ho
            hn_ref[rows, :] = _rms(ho, gain).astype(BF16)

    row = lambda c: pl.BlockSpec((tm, c), lambda i: (i, 0))
    return _pcall(
        body, comm=comm, name=name, grid=(s // tm,),
        in_specs=[row(f), _resident(wd.shape), row(d), _whole((1, d))],
        out_specs=[row(d), row(d)],
        out_shape=[jax.ShapeDtypeStruct((s, d), F32), jax.ShapeDtypeStruct((s, d), BF16)],
        compiler_params=_params("arbitrary"),
    )(a, wd, h, gain)


def _ffn_fwd2_loss(a, wd, h, gain, target, tm):
    s, d = h.shape
    f = a.shape[1]

    def body(a_ref, wd_ref, h_ref, gain_ref, t_ref, dh_ref, dhb_ref, dgain_ref, loss_ref):
        wd_v, gain = wd_ref[...], gain_ref[...]
        sub = min(SUB_ROWS, tm)
        dgain = loss = None
        for c in range(tm // sub):
            rows = pl.ds(c * sub, sub)
            ho = h_ref[rows, :] + _nn(a_ref[rows, :], wd_v)
            err = _rms(ho, gain) - t_ref[rows, :]
            dx, part = _rms_bwd(err * (1.0 / d), ho, gain)
            dh_ref[rows, :] = dx
            dhb_ref[rows, :] = dx.astype(BF16)
            sq = jnp.sum(err * err, axis=0, keepdims=True) * (0.5 / d)
            dgain, loss = (part, sq) if dgain is None else (dgain + part, loss + sq)
        _acc(dgain_ref, dgain, pl.program_id(0) == 0)
        _acc(loss_ref, loss, pl.program_id(0) == 0)

    row = lambda c: pl.BlockSpec((tm, c), lambda i: (i, 0))
    return _pcall(
        body, name="ffn_fwd2_loss", grid=(s // tm,),
        in_specs=[row(f), _resident(wd.shape), row(d), _whole((1, d)), row(d)],
        out_specs=[row(d), row(d), _whole((1, d)), _whole((1, d))],
        out_shape=[jax.ShapeDtypeStruct((s, d), F32), jax.ShapeDtypeStruct((s, d), BF16), jax.ShapeDtypeStruct((1, d), F32),
                   jax.ShapeDtypeStruct((1, d), F32)],
        compiler_params=_params("arbitrary"),
    )(a, wd, h, gain, target)


def _odd_fwd(xn, h, win, sgu, ws, bfull, wout, gffn, tm, comm=None):
    s, d = h.shape
    e = win.shape[1]
    cw = e // 2
    heads = ws.shape[0]

    def body(xn_ref, h_ref, win_ref, sgu_ref, ws_ref, b_ref, wout_ref, gffn_ref,
             pre_ref, gate_ref, mixo_ref, ho_ref, hn_ref, gbuf):
        pre = _nn(xn_ref[...], win_ref[...])
        pre_ref[...] = pre.astype(BF16)
        p = _gelu(pre)
        u, v = p[:, :cw], p[:, cw:]
        vn = _rms(v, sgu_ref[...]).astype(BF16)
        for n in range(tm // CHUNK):
            rows = slice(n * CHUNK, (n + 1) * CHUNK)
            for hd in range(heads):
                cols = slice(hd * CHUNK, (hd + 1) * CHUNK)
                gbuf[rows, cols] = _nn(ws_ref[hd], vn[rows, cols]) + b_ref[:, cols]
        gate = gbuf[...]
        gate_ref[...] = gate.astype(BF16)
        mixo = (u * gate).astype(BF16)
        mixo_ref[...] = mixo
        ho = h_ref[...] + _nn(mixo, wout_ref[...])
        ho_ref[...] = ho
        hn_ref[...] = _rms(ho, gffn_ref[...]).astype(BF16)

    row = lambda c: pl.BlockSpec((tm, c), lambda i: (i, 0))
    return _pcall(
        body, comm=comm, name="odd_fwd", grid=(s // tm,),
        in_specs=[row(d), row(d), _whole(win.shape), _whole(sgu.shape), _whole(ws.shape), _whole(bfull.shape),
                  _whole(wout.shape), _whole((1, d))],
        out_specs=[row(e), row(cw), row(cw), row(d), row(d)],
        out_shape=[jax.ShapeDtypeStruct((s, e), BF16), jax.ShapeDtypeStruct((s, cw), BF16), jax.ShapeDtypeStruct((s, cw), BF16),
                   jax.ShapeDtypeStruct((s, d), F32), jax.ShapeDtypeStruct((s, d), BF16)],
        scratch_shapes=[pltpu.VMEM((tm, cw), F32)],
        compiler_params=_params("arbitrary"),
    )(xn, h, win, sgu, ws, bfull, wout, gffn)


def _ffn_bwd1(dhb, g, silu, upds, wd, cw, tm, tn, name, comm=None):
    s, d = dhb.shape
    f = g.shape[1]
    ni = s // tm

    def body(dh_ref, g_ref, silu_ref, upds_ref, wd_ref, cw_ref, dg_ref, dup_ref, dcw_ref, dcb_ref, ebuf):
        i = pl.program_id(1)

        @pl.when(i == 0)
        def _():
            ebuf[tm:tm + HALO, :] = jnp.zeros((HALO, tn), F32)

        wd_v = wd_ref[...]
        sub = min(SUB_ROWS, tm)
        sums = [None] * 4
        for c in reversed(range(tm // sub)):
            rows = pl.ds(c * sub, sub)
            da = _nt(dh_ref[pl.ds(pl.multiple_of((ni - 1 - i) * tm + c * sub, sub), sub), :], wd_v)
            dup_ref[rows, :] = (da * silu_ref[rows, :].astype(F32)).astype(BF16)
            dgc = da * upds_ref[rows, :].astype(F32)
            ebuf[rows, :] = dgc
            ext = ebuf[pl.ds(c * sub, sub + HALO), :]
            s1 = _shift(ext, 1, False)[0:sub, :]
            s2 = _shift(ext, 2, False)[0:sub, :]
            dg_ref[rows, :] = (cw_ref[2:3, :] * dgc + cw_ref[1:2, :] * s1 + cw_ref[0:1, :] * s2).astype(BF16)
            gv = g_ref[rows, :].astype(F32)
            for k, term in enumerate((s2 * gv, s1 * gv, dgc * gv, dgc)):
                part = jnp.sum(term, axis=0, keepdims=True)
                sums[k] = part if sums[k] is None else sums[k] + part
        for k in range(3):
            _acc(dcw_ref.at[k:k + 1, :], sums[k], i == 0)
        _acc(dcb_ref, sums[3], i == 0)
        ebuf[tm:tm + HALO, :] = ebuf[0:HALO, :]

    tile = pl.BlockSpec((tm, tn), lambda j, i: (ni - 1 - i, j))
    wcol = lambda r: pl.BlockSpec((r, tn), lambda j, i: (0, j))
    out = jax.ShapeDtypeStruct((s, f), BF16)
    return _pcall(
        body, comm=comm, name=name, grid=(f // tn, ni),
        in_specs=[_resident((s, d)), tile, tile, tile,
                  pl.BlockSpec((tn, d), lambda j, i: (j, 0)), wcol(3)],
        out_specs=[tile, tile, wcol(3), wcol(1)],
        out_shape=[out, out, jax.ShapeDtypeStruct((3, f), F32), jax.ShapeDtypeStruct((1, f), F32)],
        scratch_shapes=[pltpu.VMEM((tm + HALO, tn), F32)],
        compiler_params=_params("arbitrary", "arbitrary"),
    )(dhb, g, silu, upds, wd, cw)


def _ffn_bwd2(dg, dup, wg, wu, h, gain, dh, tm, name, comm=None):
    s, d = h.shape
    f = dg.shape[1]

    def body(dg_ref, dup_ref, wg_ref, wu_ref, h_ref, gain_ref, dh_ref, dho_ref, dhb_ref, dgain_ref):
        wg_v, wu_v, gain = wg_ref[...], wu_ref[...], gain_ref[...]
        sub = min(SUB_ROWS, tm)
        dgain = None
        for c in range(tm // sub):
            rows = pl.ds(c * sub, sub)
            dhn = _nn(dg_ref[rows, :], wg_v) + _nn(dup_ref[rows, :], wu_v)
            dx, part = _rms_bwd(dhn, h_ref[rows, :], gain)
            dgain = part if dgain is None else dgain + part
            dho = dh_ref[rows, :] + dx
            dho_ref[rows, :] = dho
            dhb_ref[rows, :] = dho.astype(BF16)
        _acc(dgain_ref, dgain, pl.program_id(0) == 0)

    row = lambda c: pl.BlockSpec((tm, c), lambda i: (i, 0))
    return _pcall(
        body, comm=comm, name=name, grid=(s // tm,),
        in_specs=[row(f), row(f), _resident(wg.shape), _resident(wu.shape), row(d), _whole((1, d)), row(d)],
        out_specs=[row(d), row(d), _whole((1, d))],
        out_shape=[jax.ShapeDtypeStruct((s, d), F32), jax.ShapeDtypeStruct((s, d), BF16), jax.ShapeDtypeStruct((1, d), F32)],
        compiler_params=_params("arbitrary"),
    )(dg, dup, wg, wu, h, gain, dh)


def _odd_bwd(dhb, dh, wout, pre, gate, ws, wst, sgu, win, h, gmix, tm, comm=None):
    s, d = h.shape
    e = win.shape[1]
    cw = e // 2
    heads = ws.shape[0]
    ni = s // tm

    def body(dhb_ref, dh_ref, wout_ref, pre_ref, gate_ref, ws_ref, wst_ref, sgu_ref, win_ref, h_ref, gmix_ref,
             dpre_ref, dho_ref, dhob_ref, dgain_ref, dsgu_ref, dws_ref, db_ref, vbuf, gacc):
        i = pl.program_id(0)
        first = i == 0
        dmixo = _nt(dhb_ref[...], wout_ref[...])
        pre = pre_ref[...].astype(F32)
        p = _gelu(pre)
        u, v = p[:, :cw], p[:, cw:]
        sgu = sgu_ref[...]
        rv = lax.rsqrt(jnp.mean(v * v, axis=-1, keepdims=True) + EPS)
        vh = v * rv
        vn = (vh * sgu).astype(BF16)
        du = dmixo * gate_ref[...].astype(F32)
        dgate = dmixo * u
        dgate_b = dgate.astype(BF16)
        gsum = dgate[0:CHUNK, :]
        for n in range(1, tm // CHUNK):
            gsum = gsum + dgate[n * CHUNK:(n + 1) * CHUNK, :]
        _acc(gacc, gsum, first)
        for hd in range(heads):
            cols = slice(hd * CHUNK, (hd + 1) * CHUNK)
            dws = None
            for n in range(tm // CHUNK):
                rows = slice(n * CHUNK, (n + 1) * CHUNK)
                vbuf[rows, cols] = _nn(wst_ref[hd], dgate_b[rows, cols])
                part = _nt(dgate_b[rows, cols], vn[rows, cols])
                dws = part if dws is None else dws + part
            _acc(dws_ref.at[hd], dws, first)
        dvn = vbuf[...]
        _acc(dsgu_ref, jnp.sum(dvn * vh, axis=0, keepdims=True), first)
        dvh = dvn * sgu
        dv = rv * (dvh - vh * jnp.mean(dvh * vh, axis=-1, keepdims=True))
        dpre = (jnp.concatenate([du, dv], axis=1) * _gelu_grad(pre)).astype(BF16)
        dpre_ref[...] = dpre
        dx, dgain = _rms_bwd(_nt(dpre, win_ref[...]), h_ref[...], gmix_ref[...])
        dho = dh_ref[...] + dx
        dho_ref[...] = dho
        dhob_ref[...] = dho.astype(BF16)
        _acc(dgain_ref, dgain, first)

        @pl.when(i == ni - 1)
        def _():
            ones = jnp.ones((8, CHUNK), F32)
            for hd in range(heads):
                tot = lax.dot_general(ones, gacc[:, hd * CHUNK:(hd + 1) * CHUNK], (((1,), (1,)), ((), ())),
                                      preferred_element_type=F32, precision=lax.Precision.HIGHEST)
                db_ref[hd:hd + 1, :] = tot[0:1, :]

    row = lambda c: pl.BlockSpec((tm, c), lambda i: (i, 0))
    return _pcall(
        body, comm=comm, name="odd_bwd", grid=(ni,),
        in_specs=[row(d), row(d), _whole(wout.shape), row(e), row(cw), _whole(ws.shape), _whole(wst.shape), _whole(sgu.shape),
                  _whole(win.shape), row(d), _whole((1, d))],
        out_specs=[row(e), row(d), row(d), _whole((1, d)), _whole((1, cw)), _whole(ws.shape), _whole((heads, CHUNK))],
        out_shape=[jax.ShapeDtypeStruct((s, e), BF16), jax.ShapeDtypeStruct((s, d), F32), jax.ShapeDtypeStruct((s, d), BF16),
                   jax.ShapeDtypeStruct((1, d), F32), jax.ShapeDtypeStruct((1, cw), F32), jax.ShapeDtypeStruct(ws.shape, F32),
                   jax.ShapeDtypeStruct((heads, CHUNK), F32)],
        scratch_shapes=[pltpu.VMEM((tm, cw), F32), pltpu.VMEM((CHUNK, cw), F32)],
        compiler_params=_params("arbitrary"),
    )(dhb, dh, wout, pre, gate, ws, wst, sgu, win, h, gmix)


def _even_bwd(dhb, dh, wout, proj, cq, pooled, conva, wpool, wpoolt, pscale, win, x, gmix, tm, comm=None):
    s, d = x.shape
    e = win.shape[1]
    aw = e // 4
    ni = s // tm

    def body(dhb_ref, dh_ref, wout_ref, proj_ref, cq_ref, pooled_ref, ca_ref, wp_ref, wpt_ref, ps_ref, win_ref, x_ref, gmix_ref,
             dproj_ref, dx_ref, dgain_ref, dca_ref, dwp_ref, dps_ref, cbuf, ebuf):
        i = pl.program_id(0)
        first = i == 0

        @pl.when(first)
        def _():
            cbuf[tm:tm + HALO, :] = jnp.zeros((HALO, aw), F32)
            ebuf[tm:tm + HALO, :] = jnp.zeros((HALO, aw), F32)

        dmix = _nt(dhb_ref[...], wout_ref[...])
        dy_a, dy_b = dmix[:, :aw], dmix[:, aw:]
        proj = proj_ref[...].astype(F32)
        a_b, a_c, a_v = (proj[:, k * aw:(k + 1) * aw] for k in range(3))
        da_b = dy_a * cq_ref[...].astype(F32)
        dcq = dy_a * a_b
        cbuf[0:tm, :] = dcq
        cext = cbuf[...]
        s1 = _shift(cext, 1, False)[0:tm, :]
        s2 = _shift(cext, 2, False)[0:tm, :]
        q = a_c * a_v
        for k, shifted in enumerate((s2, s1, dcq)):
            _acc(dca_ref.at[k:k + 1, :], jnp.sum(shifted * q, axis=0, keepdims=True), first)
        dq = ca_ref[2:3, :] * dcq + ca_ref[1:2, :] * s1 + ca_ref[0:1, :] * s2
        da_c = dq * a_v
        da_v = dq * a_c
        dps, dpool = [], []
        for g, w in enumerate(WINDOWS):
            cols = slice(g * LANE, (g + 1) * LANE)
            pooled = pooled_ref[:, cols]
            mixed = _nn(pooled, wp_ref[g])
            dps.append(jnp.sum(dy_b[:, cols] * mixed, axis=0, keepdims=True))
            dmixed = (dy_b[:, cols] * ps_ref[:, cols]).astype(BF16)
            _acc(dwp_ref.at[g], _tn(pooled, dmixed), first)
            dp = _nn(dmixed, wpt_ref[g])
            dpool.append(dp)
            ebuf[0:tm, cols] = dp / _counts((ni - 1 - i) * tm, tm, w)
        _acc(dps_ref, jnp.concatenate(dps, axis=1), first)
        eext = ebuf[...]
        dzs = []
        for g, w in enumerate(WINDOWS):
            cols = slice(g * LANE, (g + 1) * LANE)
            dzs.append(_window_sum(eext[:, cols], w, False)[0:tm, :] - dpool[g])
        dproj = jnp.concatenate([da_b, da_c, da_v] + dzs, axis=1).astype(BF16)
        dproj_ref[...] = dproj
        dx, dgain = _rms_bwd(_nt(dproj, win_ref[...]), x_ref[...], gmix_ref[...])
        dx_ref[...] = dh_ref[...] + dx
        _acc(dgain_ref, dgain, first)
        cbuf[tm:tm + HALO, :] = cbuf[0:HALO, :]
        ebuf[tm:tm + HALO, :] = ebuf[0:HALO, :]

    row = lambda c: pl.BlockSpec((tm, c), lambda i: (ni - 1 - i, 0))
    return _pcall(
        body, comm=comm, name="even_bwd", grid=(ni,),
        in_specs=[row(d), row(d), _whole(wout.shape), row(e), row(aw), row(aw), _whole(conva.shape), _whole(wpool.shape),
                  _whole(wpoolt.shape), _whole(pscale.shape), _whole(win.shape), row(d), _whole((1, d))],
        out_specs=[row(e), row(d), _whole((1, d)), _whole(conva.shape), _whole(wpool.shape), _whole(pscale.shape)],
        out_shape=[jax.ShapeDtypeStruct((s, e), BF16), jax.ShapeDtypeStruct((s, d), F32), jax.ShapeDtypeStruct((1, d), F32),
                   jax.ShapeDtypeStruct(conva.shape, F32), jax.ShapeDtypeStruct(wpool.shape, F32),
                   jax.ShapeDtypeStruct(pscale.shape, F32)],
        scratch_shapes=[pltpu.VMEM((tm + HALO, aw), F32), pltpu.VMEM((tm + HALO, aw), F32)],
        compiler_params=_params("arbitrary"),
    )(dhb, dh, wout, proj, cq, pooled, conva, wpool, wpoolt, pscale, win, x, gmix)


def _wgrad(a, b, tk, ts, name, comm=None):
    s, ka = a.shape
    nb = b.shape[1]
    nt = s // ts

    def body(a_ref, b_ref, o_ref, acc):
        t = pl.program_id(1)
        _acc(acc, _tn(a_ref[...], b_ref[...]), t == 0)

        @pl.when(t == nt - 1)
        def _():
            o_ref[...] = acc[...].astype(BF16)

    return _pcall(
        body, comm=comm, name=name, grid=(ka // tk, nt),
        in_specs=[pl.BlockSpec((ts, tk), lambda k, t: (t, k)),
                  _resident((s, nb)) if nt == 1 else pl.BlockSpec((ts, nb), lambda k, t: (t, 0))],
        out_specs=pl.BlockSpec((tk, nb), lambda k, t: (k, 0)),
        out_shape=jax.ShapeDtypeStruct((ka, nb), BF16),
        scratch_shapes=[pltpu.VMEM((tk, nb), F32)],
        compiler_params=_params("arbitrary", "arbitrary"),
    )(a, b)


def _adamw(parts, w, m, v, tr, name, layer=None, into=None, carried=()):
    r, c = w.shape[-2:]
    n, rp, cp = parts.shape
    assert rp >= r and r % tr == 0
    prev, carried = ([] if into is None else list(into)), list(carried)

    def body(p_ref, w_ref, m_ref, v_ref, *rest):
        g_ref, d_ref, mo_ref, vo_ref = rest[len(prev) + len(carried):len(prev) + len(carried) + 4]
        g = p_ref[0, 0:tr, 0:c].astype(F32)
        for j in range(1, n):
            g = g + p_ref[j, 0:tr, 0:c].astype(F32)
        g_ref[...] = g
        mn = ADAM_B1 * m_ref[...] + (1.0 - ADAM_B1) * g
        vn = ADAM_B2 * v_ref[...] + (1.0 - ADAM_B2) * (g * g)
        mo_ref[...] = mn
        vo_ref[...] = vn
        m_hat = mn / (1.0 - ADAM_B1 ** ADAM_STEP)
        v_hat = vn / (1.0 - ADAM_B2 ** ADAM_STEP)
        d_ref[...] = -ADAM_LR * (m_hat / (jnp.sqrt(v_hat) + ADAM_EPS) + ADAM_WD * w_ref[...])

    if layer is None:
        row = pl.BlockSpec((tr, c), lambda i: (i, 0))
    else:
        row = pl.BlockSpec((None, tr, c), lambda i: (layer, i, 0))
    out = jax.ShapeDtypeStruct(w.shape, F32)
    untouched = pl.BlockSpec(memory_space=pl.ANY)
    aliases = {4 + k: k for k in range(len(prev))}
    aliases.update({4 + len(prev) + k: 4 + k for k in range(len(carried))})
    res = _pcall(
        body, name=name, grid=(r // tr,),
        in_specs=[pl.BlockSpec((n, tr, cp), lambda i: (0, i, 0)), row, row, row] + [untouched] * (len(prev) + len(carried)),
        out_specs=[row, row, row, row] + [untouched] * len(carried),
        out_shape=[out, out, out, out] + [jax.ShapeDtypeStruct(a.shape, a.dtype) for a in carried],
        input_output_aliases=aliases,
        compiler_params=_params("arbitrary"),
    )(parts, w, m, v, *prev, *carried)
    return (res[:4], res[4:]) if carried else res


def _pair_add(grad, other, spec, core, name):
    axis, width = spec
    slot = other.shape[1:]

    def body(core_ref, g_ref, o_ref, out_ref):
        out_ref[...] = (g_ref[...].astype(F32) + o_ref[...].astype(F32)).astype(BF16)

    if axis == 0:
        gspec = pl.BlockSpec(slot, lambda q, core_ref: (2 * q + core_ref[0], 0))
    else:
        gspec = pl.BlockSpec(slot, lambda q, core_ref: (0, 2 * q + core_ref[0]))
    per_chip = pl.BlockSpec((None,) + slot, lambda q, core_ref: (q, 0, 0))
    return _pcall(
        body, name=name,
        grid_spec=pltpu.PrefetchScalarGridSpec(num_scalar_prefetch=1, grid=(N_CHIP,), in_specs=[gspec, per_chip], out_specs=per_chip),
        out_shape=jax.ShapeDtypeStruct(other.shape, BF16),
        compiler_params=_params("arbitrary"),
    )(core, grad, other)


def _hbm():
    return pl.BlockSpec(memory_space=pltpu.HBM)


def _window(ref, spec, j):
    axis, width = spec
    start = pl.multiple_of(j * width, width)
    return ref.at[(slice(None),) * axis + (pl.ds(start, width),)]


def _here():
    return lax.axis_index("x"), lax.axis_index("y"), lax.axis_index("c")


class _Gather:
    def __init__(self, shards, specs, fulls):
        n = len(shards)
        self.ins, self.specs, self.out_shape = list(shards), list(specs), list(fulls)
        self.sems = [pltpu.SemaphoreType.DMA((7 * n,)), pltpu.SemaphoreType.DMA((7 * n,)), pltpu.SemaphoreType.DMA((n,))]

    def _plan(self, ins, outs, sems):
        send_sems, recv_sems, local_sems = sems
        x, y, c = _here()
        me, sibling = (x, y, c), (x, y, 1 - c)
        chips = [(1 - x, y), (x, 1 - y), (1 - x, 1 - y)]

        def slot(t, dev):
            return _window(outs[t], self.specs[t], 4 * dev[0] + 2 * dev[1] + dev[2])

        def copy(t, k, block, to, src=None):
            return pltpu.make_async_remote_copy(
                src_ref=slot(t, block) if src is None else src, dst_ref=slot(t, block),
                send_sem=send_sems.at[7 * t + k], recv_sem=recv_sems.at[7 * t + k], device_id=to, device_id_type=MESH)

        plan = []
        for t in range(len(ins)):
            plan.append(dict(
                mine=pltpu.make_async_copy(ins[t], slot(t, me), local_sems.at[t]),
                first=[copy(t, 0, me, sibling, src=ins[t])] + [copy(t, 1 + j, me, (*q, c), src=ins[t]) for j, q in enumerate(chips)],
                over_ici=[copy(t, 1 + j, (*q, c), me) for j, q in enumerate(chips)],
                passed=[copy(t, 4 + j, (*q, c), sibling) for j, q in enumerate(chips)],
                from_sibling=[copy(t, 0, sibling, me)] + [copy(t, 4 + j, (*q, 1 - c), me) for j, q in enumerate(chips)]))
        return plan

    def start(self, ins, outs, sems):
        for p in self._plan(ins, outs, sems):
            p["mine"].start()
            for cp in p["first"]:
                cp.start()

    def middle(self, ins, outs, sems):
        for p in self._plan(ins, outs, sems):
            for arrived, onward in zip(p["over_ici"], p["passed"]):
                arrived.wait_recv()
                onward.start()

    def finish(self, ins, outs, sems):
        plan = self._plan(ins, outs, sems)
        for p in plan:
            for cp in p["from_sibling"]:
                cp.wait_recv()
        for p in plan:
            for cp in p["first"] + p["passed"]:
                cp.wait_send()
            p["mine"].wait()


class _PairExchange:
    def __init__(self, grads, specs):
        n = len(grads)
        self.ins, self.specs = list(grads), list(specs)
        self.out_shape = [jax.ShapeDtypeStruct((N_CHIP,) + a.shape[:sp[0]] + (sp[1],) + a.shape[sp[0] + 1:], a.dtype)
                          for a, sp in zip(grads, specs)]
        self.sems = [pltpu.SemaphoreType.DMA((n,)), pltpu.SemaphoreType.DMA((n,))]

    def start(self, ins, outs, sems):
        send_sems, recv_sems = sems
        x, y, c = _here()
        for t in range(len(ins)):
            for q in range(N_CHIP):
                pltpu.make_async_remote_copy(
                    src_ref=_window(ins[t], self.specs[t], 2 * q + (1 - c)), dst_ref=outs[t].at[q],
                    send_sem=send_sems.at[t], recv_sem=recv_sems.at[t], device_id=(x, y, 1 - c), device_id_type=MESH).start()

    def middle(self, ins, outs, sems):
        pass

    def finish(self, ins, outs, sems):
        send_sems, recv_sems = sems
        x, y, c = _here()
        for t in range(len(ins)):
            every = pltpu.make_async_remote_copy(src_ref=outs[t], dst_ref=outs[t], send_sem=send_sems.at[t],
                                                 recv_sem=recv_sems.at[t], device_id=(x, y, 1 - c), device_id_type=MESH)
            every.wait_send()
            every.wait_recv()


class _ChipExchange:
    def __init__(self, pairs, slotted=(), whole=()):
        self.ins = list(pairs) + list(slotted) + list(whole)
        self.npair, self.nslot = len(pairs), len(pairs) + len(slotted)
        n = len(self.ins)
        self.out_shape = ([jax.ShapeDtypeStruct(a.shape, a.dtype) for a in list(pairs) + list(slotted)]
                          + [jax.ShapeDtypeStruct((N_DEV,) + a.shape, a.dtype) for a in whole])
        self.sems = [pltpu.SemaphoreType.DMA((7 * n,)), pltpu.SemaphoreType.DMA((7 * n,)), pltpu.SemaphoreType.DMA((n,))]

    def _plan(self, ins, outs, sems):
        send_sems, recv_sems, local_sems = sems
        npair, nslot = self.npair, self.nslot
        x, y, c = _here()
        me, chip = 4 * x + 2 * y + c, 2 * x + y
        chips = [(1 - x, y), (x, 1 - y), (1 - x, 1 - y)]
        peers = [(x, y, 1 - c)] + [(*q, c) for q in chips] + [(*q, 1 - c) for q in chips]

        def index(dev):
            return 4 * dev[0] + 2 * dev[1] + dev[2]

        def copy(t, k, arriving):
            peer = peers[k]
            if t < npair:
                src, mine, theirs = ins[t].at[2 * peer[0] + peer[1]], chip, 2 * peer[0] + peer[1]
            else:
                src, mine, theirs = (ins[t].at[index(peer)] if t < nslot else ins[t]), me, index(peer)
            return pltpu.make_async_remote_copy(
                src_ref=src, dst_ref=outs[t].at[theirs if arriving else mine],
                send_sem=send_sems.at[7 * t + k], recv_sem=recv_sems.at[7 * t + k], device_id=peer, device_id_type=MESH)

        own, sent, arriving = [], [], []
        for t in range(len(ins)):
            fan = range(1, 4) if t < npair else range(7)
            if t < npair:
                own.append(pltpu.make_async_copy(ins[t].at[chip], outs[t].at[chip], local_sems.at[t]))
            else:
                own.append(pltpu.make_async_copy(ins[t].at[me] if t < nslot else ins[t], outs[t].at[me], local_sems.at[t]))
            sent += [copy(t, k, False) for k in fan]
            arriving += [copy(t, k, True) for k in fan]
        return own, sent, arriving

    def start(self, ins, outs, sems):
        own, sent, _ = self._plan(ins, outs, sems)
        for cp in own + sent:
            cp.start()

    def middle(self, ins, outs, sems):
        pass

    def finish(self, ins, outs, sems):
        own, sent, arriving = self._plan(ins, outs, sems)
        for cp in arriving:
            cp.wait_recv()
        for cp in sent:
            cp.wait_send()
        for cp in own:
            cp.wait()


class _Jobs:
    def __init__(self, *jobs):
        self.jobs = jobs
        self.ins = [a for j in jobs for a in j.ins]
        self.out_shape = [a for j in jobs for a in j.out_shape]
        self.sems = [a for j in jobs for a in j.sems]

    def _split(self, ins, outs, sems):
        i = o = s = 0
        for j in self.jobs:
            yield j, ins[i:i + len(j.ins)], outs[o:o + len(j.out_shape)], sems[s:s + len(j.sems)]
            i, o, s = i + len(j.ins), o + len(j.out_shape), s + len(j.sems)

    def start(self, ins, outs, sems):
        for j, a, b, c in self._split(ins, outs, sems):
            j.start(a, b, c)

    def middle(self, ins, outs, sems):
        for j, a, b, c in self._split(ins, outs, sems):
            j.middle(a, b, c)

    def finish(self, ins, outs, sems):
        for j, a, b, c in self._split(ins, outs, sems):
            j.finish(a, b, c)

    def results(self, outs):
        return [b for _, _, b, _ in self._split((), outs, ())]


def _alone(job, name):
    return _pcall(lambda: None, comm=job, name=name, in_specs=[], out_specs=[], out_shape=[])()[1]


SMALL_ROWS = 24
REP_COLS = 1024


def _pad_to(a, rows, cols):
    return jnp.pad(a, ((0, rows - a.shape[0]), (0, cols - a.shape[1])))


def _pack_small(conv_a, sgu_norm, conv_ffn, cols):
    return jnp.concatenate([_pad_to(conv_a, 8, cols), _pad_to(sgu_norm, 8, cols),
                            _pad_to(conv_ffn.reshape(-1, conv_ffn.shape[-1]), 8, cols)], axis=0)


def _unpack_small(p, ca_w, sg_w, cf_w):
    return p[0:3, 0:ca_w], p[8:9, 0:sg_w], p[16:22, 0:cf_w].reshape(2, 3, cf_w)


def _tile_rows(rows):
    return -(-rows // 8) * 8


def _pack_rows(parts):
    return jnp.concatenate([_pad_to(a, _tile_rows(a.shape[0]), REP_COLS) for a in parts], axis=0)


def _unpack_rows(p, shapes):
    out, r0 = [], 0
    for r, c in shapes:
        out.append(p[r0:r0 + r, 0:c])
        r0 += _tile_rows(r)
    return out


def _rep_late(norm_mix0, norm_ffn0, pool_scale, b_conv0, w_pool):
    return _pack_rows([norm_mix0, norm_ffn0, pool_scale.reshape(1, -1), _pad_to(b_conv0, 1, 3 * REP_COLS).reshape(3, REP_COLS),
                       w_pool.reshape(-1, REP_COLS)])


def _rep_early(norm_mix1, norm_ffn1, final_norm, b_spatial, b_conv1, loss):
    return _pack_rows([norm_mix1, norm_ffn1, final_norm.reshape(1, -1), b_spatial.reshape(1, -1),
                       _pad_to(b_conv1, 1, 3 * REP_COLS).reshape(3, REP_COLS), loss])


def _unpack_rep(late, early, wsp, like):
    f = like["b_conv_ffn"].shape[1]
    nm0, nf0, ps, bc0, wp = _unpack_rows(late, [(1, REP_COLS), (1, REP_COLS), (1, like["pool_scale"].shape[1]), (3, REP_COLS),
                                                (like["w_pool"].size // REP_COLS, REP_COLS)])
    nm1, nf1, fin, bs, bc1, _ = _unpack_rows(early, [(1, REP_COLS)] * 4 + [(3, REP_COLS), (1, REP_COLS)])
    return {
        "norm_mix": jnp.concatenate([nm0, nm1]), "norm_ffn": jnp.concatenate([nf0, nf1]), "final_norm": fin[0], "pool_scale": ps,
        "b_spatial": bs.reshape(like["b_spatial"].shape),
        "b_conv_ffn": jnp.concatenate([bc0.reshape(1, -1), bc1.reshape(1, -1)])[:, 0:f],
        "w_pool": wp.reshape(like["w_pool"].shape), "w_spatial": wsp.reshape(like["w_spatial"].shape),
    }


def _pad_slots(a, width, padded):
    a = a.reshape(*a.shape[:-1], N_DEV, width)
    a = jnp.pad(a, ((0, 0),) * (a.ndim - 1) + ((0, padded - width),))
    return a.reshape(*a.shape[:-2], N_DEV * padded)


def _unpad_slots(a, width, padded):
    a = a.reshape(*a.shape[:-1], N_DEV, padded)[..., 0:width]
    return a.reshape(*a.shape[:-2], N_DEV * width)


def kernel(x, norm_mix, norm_ffn, final_norm, w_in_even, conv_a, w_pool, pool_scale, w_out_even, w_in_odd, sgu_norm, w_spatial, b_spatial, w_out_odd, w_ffn_gate, w_ffn_up, conv_ffn, b_conv_ffn, w_ffn_down, loss_target, m_norm_mix, m_norm_ffn, m_final_norm, m_w_in_even, m_conv_a, m_w_pool, m_pool_scale, m_w_out_even, m_w_in_odd, m_sgu_norm, m_w_spatial, m_b_spatial, m_w_out_odd, m_w_ffn_gate, m_w_ffn_up, m_conv_ffn, m_b_conv_ffn, m_w_ffn_down, v_norm_mix, v_norm_ffn, v_final_norm, v_w_in_even, v_conv_a, v_w_pool, v_pool_scale, v_w_out_even, v_w_in_odd, v_sgu_norm, v_w_spatial, v_b_spatial, v_w_out_odd, v_w_ffn_gate, v_w_ffn_up, v_conv_ffn, v_b_conv_ffn, v_w_ffn_down):
    s, d = x.shape[1], x.shape[2]
    x2, target = x[0], loss_target[0]
    tm = min(512, s)
    tm_wide = min(2048, s)
    tk_wide = 1024
    tn_fwd = 512
    row = lambda a: a.reshape(1, -1)
    ein, ro = w_in_even.shape[2], w_out_even.shape[1]
    fs = w_ffn_gate.shape[2]
    fsp = -(-fs // LANE) * LANE
    fp = N_DEV * fsp
    full = lambda shape, dtype=BF16: jax.ShapeDtypeStruct(shape, dtype)

    wg_s = [jnp.pad(w_ffn_gate[l].T, ((0, fsp - fs), (0, 0))).astype(BF16) for l in range(2)]
    wu_s = [jnp.pad(w_ffn_up[l].T, ((0, fsp - fs), (0, 0))).astype(BF16) for l in range(2)]
    wd_s = [jnp.pad(w_ffn_down[l], ((0, fsp - fs), (0, 0))).astype(BF16) for l in range(2)]
    small_s = _pack_small(conv_a[0], sgu_norm, conv_ffn, fsp)[None]
    in_spec, out_spec, gu_spec, down_spec = (1, ein), (0, ro), (0, fsp), (0, fsp)
    full_in, full_out, full_gu, full_down = full((d, N_DEV * ein)), full((N_DEV * ro, d)), full((fp, d)), full((fp, d))
    win_e, wout_e, gsmall = _alone(_Gather([w_in_even[0].astype(BF16), w_out_even[0].astype(BF16), small_s], [in_spec, out_spec, (0, 1)],
                                           [full_in, full_out, full((N_DEV, SMALL_ROWS, fsp), F32)]), "gather_mix0")
    ca_full = jnp.moveaxis(gsmall[:, 0:3, 0:conv_a.shape[2]], 0, 1).reshape(3, -1)
    sgu_full = gsmall[:, 8, 0:sgu_norm.shape[1]].reshape(1, -1)
    cf_full = jnp.moveaxis(gsmall[:, 16:22, :].reshape(N_DEV, 2, 3, fsp), 0, 2).reshape(2, 3, fp)
    cb_full = _pad_slots(b_conv_ffn, fs, fsp)

    tril = jnp.tril(jnp.ones((CHUNK, CHUNK), F32))
    ws_m = w_spatial[0] * tril
    ws_b = ws_m.astype(BF16)
    wst_b = jnp.swapaxes(ws_m, 1, 2).astype(BF16)
    bfull = jnp.repeat(b_spatial[0].T, CHUNK, axis=1)
    wpool_b = w_pool[0].astype(BF16)
    wpoolt_b = jnp.swapaxes(w_pool[0], 1, 2).astype(BF16)

    (xn0, proj0, cq0, pooled0, mix0, h1, hn0), (wg0,) = _even_fwd(
        x2, norm_mix[0:1], norm_ffn[0:1], win_e, ca_full, wpool_b, pool_scale, wout_e, tm,
        comm=_Gather([wg_s[0]], [gu_spec], [full_gu]))
    (g0, gc0), (wu0,) = _ffn_gate(hn0, wg0, cf_full[0], cb_full[0:1], tm_wide, tn_fwd, "ffn_gate_l0", comm=_Gather([wu_s[0]], [gu_spec], [full_gu]))
    (a0, sl0, ud0), (wd0,) = _ffn_up(hn0, wu0, gc0, tm_wide, tn_fwd, "ffn_up_l0", comm=_Gather([wd_s[0]], [down_spec], [full_down]))
    (h2, xn1), (win_o, wout_o) = _ffn_fwd2(a0, wd0, h1, norm_mix[1:2], tm, "ffn_fwd2_l0", comm=_Gather(
        [w_in_odd[0].astype(BF16), w_out_odd[0].astype(BF16)], [in_spec, out_spec], [full_in, full_out]))
    (pre1, gate1, mixo1, h3, hn1), (wg1,) = _odd_fwd(xn1, h2, win_o, sgu_full, ws_b, bfull, wout_o, norm_ffn[1:2], tm,
                                                    comm=_Gather([wg_s[1]], [gu_spec], [full_gu]))
    (g1, gc1), (wu1,) = _ffn_gate(hn1, wg1, cf_full[1], cb_full[1:2], tm_wide, tn_fwd, "ffn_gate_l1", comm=_Gather([wu_s[1]], [gu_spec], [full_gu]))
    (a1, sl1, ud1), (wd1,) = _ffn_up(hn1, wu1, gc1, tm_wide, tn_fwd, "ffn_up_l1", comm=_Gather([wd_s[1]], [down_spec], [full_down]))
    dh4, dh4b, d_final, lossvec = _ffn_fwd2_loss(a1, wd1, h3, row(final_norm), target, tm)

    core = lax.axis_index("c").astype(jnp.int32).reshape(1)
    ts = min(1024, s)
    gd1 = _wgrad(a1, dh4b, tk_wide, s, "wgrad_down_l1")
    (dg1, dup1, dcw1, dcb1), (o_d1,) = _ffn_bwd1(dh4b, g1, sl1, ud1, wd1, cf_full[1], tm_wide, tn_fwd, "ffn_bwd1_l1",
                                                 comm=_PairExchange([gd1], [down_spec]))
    p_d1 = _pair_add(gd1, o_d1, down_spec, core, "pair_add_down_l1")
    gg1 = _wgrad(dg1, hn1, tk_wide, s, "wgrad_gate_l1")
    gu1, (o_g1,) = _wgrad(dup1, hn1, tk_wide, s, "wgrad_up_l1", comm=_PairExchange([gg1], [gu_spec]))
    p_g1 = _pair_add(gg1, o_g1, gu_spec, core, "pair_add_gate_l1")
    jobs = _Jobs(_ChipExchange([p_d1]), _PairExchange([gu1], [gu_spec]))
    (dh3, dh3b, d_nffn1), res = _ffn_bwd2(dg1, dup1, wg1, wu1, h3, norm_ffn[1:2], dh4, tm, "ffn_bwd2_l1", comm=jobs)
    (s_d1,), (o_u1,) = jobs.results(res)
    p_u1 = _pair_add(gu1, o_u1, gu_spec, core, "pair_add_up_l1")
    (dpre1, dh2, dh2b, d_nmix1, d_sgu, d_ws, d_b), (s_g1, s_u1) = _odd_bwd(
        dh3b, dh3, wout_o, pre1, gate1, ws_b, wst_b, sgu_full, win_o, h2, norm_mix[1:2], tm, comm=_ChipExchange([p_g1, p_u1]))
    gi1 = _wgrad(xn1, dpre1, tk_wide, ts, "wgrad_in_odd")
    go1, (o_i1,) = _wgrad(mixo1, dh3b, tk_wide, ts, "wgrad_out_odd", comm=_PairExchange([gi1], [in_spec]))
    p_i1 = _pair_add(gi1, o_i1, in_spec, core, "pair_add_in_odd")
    d_wsp = (d_ws * tril).reshape(-1, REP_COLS).astype(BF16)
    jobs = _Jobs(_PairExchange([go1], [out_spec]), _ChipExchange([], [], [d_wsp]))
    gd0, res = _wgrad(a0, dh2b, tk_wide, s, "wgrad_down_l0", comm=jobs)
    (o_o1,), (r_wsp,) = jobs.results(res)
    p_o1 = _pair_add(go1, o_o1, out_spec, core, "pair_add_out_odd")
    d_early = _rep_early(d_nmix1, d_nffn1, d_final, d_b, _unpad_slots(dcb1, fs, fsp), lossvec)
    jobs = _Jobs(_ChipExchange([p_i1, p_o1], [], [d_early]), _PairExchange([gd0], [down_spec]))
    (dg0, dup0, dcw0, dcb0), res = _ffn_bwd1(dh2b, g0, sl0, ud0, wd0, cf_full[0], tm_wide, tn_fwd, "ffn_bwd1_l0", comm=jobs)
    (s_i1, s_o1, r_early), (o_d0,) = jobs.results(res)
    p_d0 = _pair_add(gd0, o_d0, down_spec, core, "pair_add_down_l0")
    gg0, (s_d0,) = _wgrad(dg0, hn0, tk_wide, s, "wgrad_gate_l0", comm=_ChipExchange([p_d0]))
    gu0, (o_g0,) = _wgrad(dup0, hn0, tk_wide, s, "wgrad_up_l0", comm=_PairExchange([gg0], [gu_spec]))
    p_g0 = _pair_add(gg0, o_g0, gu_spec, core, "pair_add_gate_l0")
    jobs = _Jobs(_ChipExchange([p_g0]), _PairExchange([gu0], [gu_spec]))
    (dh1, dh1b, d_nffn0), res = _ffn_bwd2(dg0, dup0, wg0, wu0, h1, norm_ffn[0:1], dh2, tm, "ffn_bwd2_l0", comm=jobs)
    (s_g0,), (o_u0,) = jobs.results(res)
    p_u0 = _pair_add(gu0, o_u0, gu_spec, core, "pair_add_up_l0")
    go0 = _wgrad(mix0, dh1b, tk_wide, ts, "wgrad_out_even")
    jobs = _Jobs(_ChipExchange([p_u0]), _PairExchange([go0], [out_spec]))
    (dproj0, grad_x, d_nmix0, d_ca, d_wp, d_ps), res = _even_bwd(
        dh1b, dh1, wout_e, proj0, cq0, pooled0, ca_full, wpool_b, wpoolt_b, pool_scale, win_e, x2, norm_mix[0:1], tm, comm=jobs)
    (s_u0,), (o_o0,) = jobs.results(res)
    p_o0 = _pair_add(go0, o_o0, out_spec, core, "pair_add_out_even")
    d_small = jnp.stack([_pack_small(a, b, c, fsp) for a, b, c in zip(
        jnp.moveaxis(d_ca.reshape(3, N_DEV, -1), 1, 0), jnp.moveaxis(d_sgu.reshape(1, N_DEV, -1), 1, 0),
        jnp.moveaxis(jnp.stack([dcw0, dcw1]).reshape(2, 3, N_DEV, fsp), 2, 0))])
    d_late = _rep_late(d_nmix0, d_nffn0, d_ps, _unpad_slots(dcb0, fs, fsp), d_wp)
    gi0, (r_small, r_late) = _wgrad(xn0, dproj0, tk_wide, ts, "wgrad_in_even", comm=_ChipExchange([], [d_small], [d_late]))
    jobs = _Jobs(_PairExchange([gi0], [in_spec]), _ChipExchange([p_o0]))
    (o_i0,), (s_o0,) = jobs.results(_alone(jobs, "pair_exchange_in_even"))
    p_i0 = _pair_add(gi0, o_i0, in_spec, core, "pair_add_in_even")
    (s_i0,) = _alone(_ChipExchange([p_i0]), "chip_exchange_last")
    loss = jnp.sum(r_early[:, r_early.shape[1] - 8, :])

    out = {}
    out["w_in_even"], (grad_x,) = _adamw(s_i0, w_in_even[0], m_w_in_even[0], v_w_in_even[0], 256, "adamw_in_even", carried=[grad_x])
    out["w_out_even"] = _adamw(s_o0, w_out_even[0], m_w_out_even[0], v_w_out_even[0], ro // 2, "adamw_out_even")
    out["w_in_odd"] = _adamw(s_i1, w_in_odd[0], m_w_in_odd[0], v_w_in_odd[0], 256, "adamw_in_odd")
    out["w_out_odd"] = _adamw(s_o1, w_out_odd[0], m_w_out_odd[0], v_w_out_odd[0], ro // 2, "adamw_out_odd")
    tp = lambda a: jnp.swapaxes(a, 1, 2)
    for nm, s1, s0, w, m, v, back in (("w_ffn_gate", s_g1, s_g0, tp(w_ffn_gate), tp(m_w_ffn_gate), tp(v_w_ffn_gate), tp),
                                      ("w_ffn_up", s_u1, s_u0, tp(w_ffn_up), tp(m_w_ffn_up), tp(v_w_ffn_up), tp),
                                      ("w_ffn_down", s_d1, s_d0, w_ffn_down, m_w_ffn_down, v_w_ffn_down, lambda a: a)):
        l1 = _adamw(s1, w, m, v, fs // 2, "adamw_%s_l1" % nm, layer=1)
        out[nm] = [back(a) for a in _adamw(s0, w, m, v, fs // 2, "adamw_%s_l0" % nm, layer=0, into=l1)]
    small = _adamw(r_small, _pack_small(conv_a[0], sgu_norm, conv_ffn, fsp), _pack_small(m_conv_a[0], m_sgu_norm, m_conv_ffn, fsp),
                   _pack_small(v_conv_a[0], v_sgu_norm, v_conv_ffn, fsp), SMALL_ROWS, "adamw_small")
    no_loss = jnp.zeros((1, REP_COLS), F32)
    early = _adamw(r_early, *[_rep_early(nm[1:2], nf[1:2], fn, bs, bc[1:2], no_loss) for nm, nf, fn, bs, bc in (
        (norm_mix, norm_ffn, final_norm, b_spatial, b_conv_ffn), (m_norm_mix, m_norm_ffn, m_final_norm, m_b_spatial, m_b_conv_ffn),
        (v_norm_mix, v_norm_ffn, v_final_norm, v_b_spatial, v_b_conv_ffn))], r_early.shape[1], "adamw_replicated_early")
    wsp = _adamw(r_wsp, w_spatial.reshape(-1, REP_COLS), m_w_spatial.reshape(-1, REP_COLS), v_w_spatial.reshape(-1, REP_COLS),
                 r_wsp.shape[1], "adamw_w_spatial")
    late = _adamw(r_late, *[_rep_late(nm[0:1], nf[0:1], ps, bc[0:1], wp) for nm, nf, ps, bc, wp in (
        (norm_mix, norm_ffn, pool_scale, b_conv_ffn, w_pool), (m_norm_mix, m_norm_ffn, m_pool_scale, m_b_conv_ffn, m_w_pool),
        (v_norm_mix, v_norm_ffn, v_pool_scale, v_b_conv_ffn, v_w_pool))], r_late.shape[1], "adamw_replicated_late")

    names = ["norm_mix", "norm_ffn", "final_norm", "w_in_even", "conv_a", "w_pool", "pool_scale", "w_out_even", "w_in_odd", "sgu_norm",
             "w_spatial", "b_spatial", "w_out_odd", "w_ffn_gate", "w_ffn_up", "conv_ffn", "b_conv_ffn", "w_ffn_down"]
    like = {"norm_mix": norm_mix, "norm_ffn": norm_ffn, "final_norm": final_norm, "w_in_even": w_in_even, "conv_a": conv_a,
            "w_pool": w_pool, "pool_scale": pool_scale, "w_out_even": w_out_even, "w_in_odd": w_in_odd, "sgu_norm": sgu_norm,
            "w_spatial": w_spatial, "b_spatial": b_spatial, "w_out_odd": w_out_odd, "w_ffn_gate": w_ffn_gate, "w_ffn_up": w_ffn_up,
            "conv_ffn": conv_ffn, "b_conv_ffn": b_conv_ffn, "w_ffn_down": w_ffn_down}
    groups = []
    for k in range(4):
        ca_k, sg_k, cf_k = _unpack_small(small[k], conv_a.shape[2], sgu_norm.shape[1], conv_ffn.shape[2])
        vals = dict(_unpack_rep(late[k], early[k], wsp[k], like))
        vals.update(conv_a=ca_k, sgu_norm=sg_k, conv_ffn=cf_k)
        for nm in ("w_in_even", "w_in_odd", "w_out_even", "w_out_odd", "w_ffn_gate", "w_ffn_up", "w_ffn_down"):
            vals[nm] = out[nm][k]
        groups.append([vals[nm].reshape(like[nm].shape) for nm in names])
    return (loss, grad_x[None], *groups[0], *groups[1], *groups[2], *groups[3])
```

```python
import functools

import jax
import jax.numpy as jnp
from jax import lax
from jax.experimental import pallas as pl
from jax.experimental.pallas import tpu as pltpu

F32, BF16 = jnp.float32, jnp.bfloat16
EPS = 1e-6
WINDOWS = (2, 4, 8, 16)
HALO = 16
CHUNK = 128
N_DEV = 8
N_CHIP = 4
MESH = pl.DeviceIdType.MESH
VMEM_LIMIT = 56 * 2**20
LANE = 128
ADAM_LR, ADAM_B1, ADAM_B2, ADAM_EPS, ADAM_WD, ADAM_STEP = 0.001, 0.9, 0.999, 1e-08, 0.01, 10
SUB_ROWS = 256
LATE_NUM, LATE_DEN = 7, 8
INV_SQRT2 = 0.7071067811865476
INV_SQRT2PI = 0.3989422804014327


def _pcall(body, comm=None, **kw):
    if comm is None:
        return pl.pallas_call(body, **kw)
    in_specs, out_specs, out_shape = list(kw.pop("in_specs")), kw.pop("out_specs"), kw.pop("out_shape")
    single = not isinstance(out_shape, (list, tuple))
    out_specs, out_shape = ([out_specs], [out_shape]) if single else (list(out_specs), list(out_shape))
    scratch = list(kw.pop("scratch_shapes", []))
    grid = kw.get("grid", ())
    n_in, n_out, n_scr, c_in, c_out = len(in_specs), len(out_specs), len(scratch), len(comm.ins), len(comm.out_shape)

    def hosted(*refs):
        cuts = [0, n_in, n_in + c_in, n_in + c_in + n_out, n_in + c_in + n_out + c_out, n_in + c_in + n_out + c_out + n_scr, len(refs)]
        ins, cins, outs, couts, scr, sems = (refs[a:b] for a, b in zip(cuts[:-1], cuts[1:]))
        if grid:
            step, steps = 0, 1
            for axis, size in enumerate(grid):
                step, steps = step * size + pl.program_id(axis), steps * size
            pl.when(step == 0)(lambda: comm.start(cins, couts, sems))
            pl.when(step == (steps * LATE_NUM) // LATE_DEN)(lambda: comm.middle(cins, couts, sems))
            body(*ins, *outs, *scr)
            pl.when(step == steps - 1)(lambda: comm.finish(cins, couts, sems))
        else:
            comm.start(cins, couts, sems)
            comm.middle(cins, couts, sems)
            body(*ins, *outs, *scr)
            comm.finish(cins, couts, sems)

    call = pl.pallas_call(hosted, in_specs=in_specs + [_hbm()] * c_in, out_specs=out_specs + [_hbm()] * c_out,
                          out_shape=out_shape + list(comm.out_shape), scratch_shapes=scratch + list(comm.sems), **kw)

    def run(*args):
        res = call(*args, *comm.ins)
        own = res[0] if single else res[:n_out]
        return own, res[n_out:]

    return run


def _params(*sem):
    return pltpu.CompilerParams(dimension_semantics=sem, vmem_limit_bytes=VMEM_LIMIT)


def _whole(shape):
    return pl.BlockSpec(shape, lambda *_: (0,) * len(shape))


def _resident(shape):
    return pl.BlockSpec(shape, lambda *_: (0,) * len(shape), pipeline_mode=pl.Buffered(1))


def _rows(i, tm):
    return pl.ds(pl.multiple_of(i * tm, tm), tm)


def _nn(a, b):
    return jnp.dot(a, b, preferred_element_type=F32)


def _nt(a, b):
    return lax.dot_general(a, b, (((1,), (1,)), ((), ())), preferred_element_type=F32)


def _tn(a, b):
    return lax.dot_general(a, b, (((0,), (0,)), ((), ())), preferred_element_type=F32)


def _rms(x, gain):
    r = lax.rsqrt(jnp.mean(x * x, axis=-1, keepdims=True) + EPS)
    return x * r * gain


def _rms_bwd(dy, x, gain):
    r = lax.rsqrt(jnp.mean(x * x, axis=-1, keepdims=True) + EPS)
    xh = x * r
    dgain = jnp.sum(dy * xh, axis=0, keepdims=True)
    dxh = dy * gain
    dx = r * (dxh - xh * jnp.mean(dxh * xh, axis=-1, keepdims=True))
    return dx, dgain


def _gelu(x):
    return 0.5 * x * (1.0 + lax.erf(x * INV_SQRT2))


def _gelu_grad(x):
    return 0.5 * (1.0 + lax.erf(x * INV_SQRT2)) + x * jnp.exp(-0.5 * x * x) * INV_SQRT2PI


def _acc(ref, val, first):
    @pl.when(first)
    def _():
        ref[...] = val

    @pl.when(jnp.logical_not(first))
    def _():
        ref[...] += val


def _shift(ext, k, back):
    n = ext.shape[0]
    return pltpu.roll(ext, k if back else n - k, axis=0)


def _window_sum(ext, w, back):
    total, step = ext, 1
    while step < w:
        total = total + _shift(total, step, back)
        step *= 2
    return total


def _counts(row0, tm, w):
    pos1 = (row0 + lax.broadcasted_iota(jnp.int32, (tm, 1), 0) + 1).astype(F32)
    return jnp.minimum(pos1, float(w))


def _even_fwd(x, gmix, gffn, win, conva, wpool, pscale, wout, tm, comm=None):
    s, d = x.shape
    e = win.shape[1]
    aw = e // 4

    def body(x_ref, gmix_ref, gffn_ref, win_ref, ca_ref, wp_ref, ps_ref, wout_ref,
             xn_ref, proj_ref, cq_ref, pooled_ref, mix_ref, h_ref, hn_ref, qbuf, zbuf):
        i = pl.program_id(0)

        @pl.when(i == 0)
        def _():
            qbuf[0:HALO, :] = jnp.zeros((HALO, aw), F32)
            zbuf[0:HALO, :] = jnp.zeros((HALO, aw), F32)

        xv = x_ref[...]
        xn = _rms(xv, gmix_ref[...]).astype(BF16)
        xn_ref[...] = xn
        proj = _nn(xn, win_ref[...])
        proj_ref[...] = proj.astype(BF16)
        a_b, a_c, a_v, z = (proj[:, k * aw:(k + 1) * aw] for k in range(4))
        q = a_c * a_v
        qbuf[HALO:HALO + tm, :] = q
        qext = qbuf[...]
        cur = slice(HALO, HALO + tm)
        cq = ca_ref[2:3, :] * q + ca_ref[1:2, :] * _shift(qext, 1, True)[cur, :] + ca_ref[0:1, :] * _shift(qext, 2, True)[cur, :]
        cq_ref[...] = cq.astype(BF16)
        y_a = a_b * cq
        zbuf[HALO:HALO + tm, :] = z
        zext = zbuf[...]
        ys = []
        for g, w in enumerate(WINDOWS):
            cols = slice(g * LANE, (g + 1) * LANE)
            acc = _window_sum(zext[:, cols], w, True)[cur, :]
            pooled = (acc / _counts(i * tm, tm, w) - z[:, cols]).astype(BF16)
            pooled_ref[:, cols] = pooled
            ys.append(_nn(pooled, wp_ref[g]))
        y_b = jnp.concatenate(ys, axis=1) * ps_ref[...]
        mix = jnp.concatenate([y_a, y_b], axis=1).astype(BF16)
        mix_ref[...] = mix
        h = xv + _nn(mix, wout_ref[...])
        h_ref[...] = h
        hn_ref[...] = _rms(h, gffn_ref[...]).astype(BF16)
        qbuf[0:HALO, :] = qbuf[tm:tm + HALO, :]
        zbuf[0:HALO, :] = zbuf[tm:tm + HALO, :]

    row = lambda c: pl.BlockSpec((tm, c), lambda i: (i, 0))
    return _pcall(
        body, comm=comm, name="even_fwd", grid=(s // tm,),
        in_specs=[row(d), _whole((1, d)), _whole((1, d)), _whole(win.shape), _whole(conva.shape), _whole(wpool.shape),
                  _whole(pscale.shape), _whole(wout.shape)],
        out_specs=[row(d), row(e), row(aw), row(aw), row(d), row(d), row(d)],
        out_shape=[jax.ShapeDtypeStruct((s, d), BF16), jax.ShapeDtypeStruct((s, e), BF16), jax.ShapeDtypeStruct((s, aw), BF16),
                   jax.ShapeDtypeStruct((s, aw), BF16), jax.ShapeDtypeStruct((s, d), BF16), jax.ShapeDtypeStruct((s, d), F32),
                   jax.ShapeDtypeStruct((s, d), BF16)],
        scratch_shapes=[pltpu.VMEM((tm + HALO, aw), F32), pltpu.VMEM((tm + HALO, aw), F32)],
        compiler_params=_params("arbitrary"),
    )(x, gmix, gffn, win, conva, wpool, pscale, wout)


def _ffn_gate(hn, wgt, cw, cb, tm, tn, name, comm=None):
    s, d = hn.shape
    f = wgt.shape[0]
    sub = tm

    def body(hn_ref, wg_ref, cw_ref, cb_ref, g_ref, gc_ref, gbuf):
        i = pl.program_id(1)

        @pl.when(i == 0)
        def _():
            gbuf[0:HALO, :] = jnp.zeros((HALO, tn), F32)

        wg = wg_ref[...]
        for c in range(tm // sub):
            rows = pl.ds(c * sub, sub)
            g = _nt(hn_ref[pl.ds(pl.multiple_of(i * tm + c * sub, sub), sub), :], wg)
            g_ref[rows, :] = g.astype(BF16)
            gbuf[pl.ds(HALO + c * sub, sub), :] = g
            ext = gbuf[pl.ds(c * sub, sub + HALO), :]
            gc = (cw_ref[2:3, :] * g + cw_ref[1:2, :] * _shift(ext, 1, True)[HALO:, :]
                  + cw_ref[0:1, :] * _shift(ext, 2, True)[HALO:, :] + cb_ref[...])
            gc_ref[rows, :] = gc.astype(BF16)
        gbuf[0:HALO, :] = gbuf[tm:tm + HALO, :]

    tile = pl.BlockSpec((tm, tn), lambda j, i: (i, j))
    wcol = lambda r: pl.BlockSpec((r, tn), lambda j, i: (0, j))
    out = jax.ShapeDtypeStruct((s, f), BF16)
    return _pcall(
        body, comm=comm, name=name, grid=(f // tn, s // tm),
        in_specs=[_resident((s, d)), pl.BlockSpec((tn, d), lambda j, i: (j, 0)), wcol(3), wcol(1)],
        out_specs=[tile, tile], out_shape=[out, out],
        scratch_shapes=[pltpu.VMEM((tm + HALO, tn), F32)],
        compiler_params=_params("arbitrary", "arbitrary"),
    )(hn, wgt, cw, cb)


def _ffn_up(hn, wut, gc, tm, tn, name, comm=None):
    s, d = hn.shape
    f = wut.shape[0]
    sub = min(SUB_ROWS, tm)

    def body(hn_ref, wu_ref, gc_ref, a_ref, silu_ref, upds_ref):
        wu = wu_ref[...]
        for c in range(tm // sub):
            rows = pl.ds(c * sub, sub)
            up = _nt(hn_ref[pl.ds(pl.multiple_of(pl.program_id(1) * tm + c * sub, sub), sub), :], wu)
            gc = gc_ref[rows, :].astype(F32)
            sg = jax.nn.sigmoid(gc)
            silu = gc * sg
            silu_ref[rows, :] = silu.astype(BF16)
            upds_ref[rows, :] = (up * (sg * (1.0 - silu) + silu)).astype(BF16)
            a_ref[rows, :] = (silu * up).astype(BF16)

    tile = pl.BlockSpec((tm, tn), lambda j, i: (i, j))
    out = jax.ShapeDtypeStruct((s, f), BF16)
    return _pcall(
        body, comm=comm, name=name, grid=(f // tn, s // tm),
        in_specs=[_resident((s, d)), pl.BlockSpec((tn, d), lambda j, i: (j, 0)), tile],
        out_specs=[tile, tile, tile], out_shape=[out, out, out],
        compiler_params=_params("arbitrary", "arbitrary"),
    )(hn, wut, gc)


def _ffn_fwd2(a, wd, h, gain, tm, name, comm=None):
    s, d = h.shape
    f = a.shape[1]

    def body(a_ref, wd_ref, h_ref, gain_ref, ho_ref, hn_ref):
        wd_v, gain = wd_ref[...], gain_ref[...]
        sub = min(SUB_ROWS, tm)
        for c in range(tm // sub):
            rows = pl.ds(c * sub, sub)
            ho = h_ref[rows, :] + _nn(a_ref[rows, :], wd_v)
            ho_ref[rows, :] = ho
            hn_ref[rows, :] = _rms(ho, gain).astype(BF16)

    row = lambda c: pl.BlockSpec((tm, c), lambda i: (i, 0))
    return _pcall(
        body, comm=comm, name=name, grid=(s // tm,),
        in_specs=[row(f), _resident(wd.shape), row(d), _whole((1, d))],
        out_specs=[row(d), row(d)],
        out_shape=[jax.ShapeDtypeStruct((s, d), F32), jax.ShapeDtypeStruct((s, d), BF16)],
        compiler_params=_params("arbitrary"),
    )(a, wd, h, gain)


def _ffn_fwd2_loss(a, wd, h, gain, target, tm):
    s, d = h.shape
    f = a.shape[1]

    def body(a_ref, wd_ref, h_ref, gain_ref, t_ref, dh_ref, dhb_ref, dgain_ref, loss_ref):
        wd_v, gain = wd_ref[...], gain_ref[...]
        sub = min(SUB_ROWS, tm)
        dgain = loss = None
        for c in range(tm // sub):
            rows = pl.ds(c * sub, sub)
            ho = h_ref[rows, :] + _nn(a_ref[rows, :], wd_v)
            err = _rms(ho, gain) - t_ref[rows, :]
            dx, part = _rms_bwd(err * (1.0 / d), ho, gain)
            dh_ref[rows, :] = dx
            dhb_ref[rows, :] = dx.astype(BF16)
            sq = jnp.sum(err * err, axis=0, keepdims=True) * (0.5 / d)
            dgain, loss = (part, sq) if dgain is None else (dgain + part, loss + sq)
        _acc(dgain_ref, dgain, pl.program_id(0) == 0)
        _acc(loss_ref, loss, pl.program_id(0) == 0)

    row = lambda c: pl.BlockSpec((tm, c), lambda i: (i, 0))
    return _pcall(
        body, name="ffn_fwd2_loss", grid=(s // tm,),
        in_specs=[row(f), _resident(wd.shape), row(d), _whole((1, d)), row(d)],
        out_specs=[row(d), row(d), _whole((1, d)), _whole((1, d))],
        out_shape=[jax.ShapeDtypeStruct((s, d), F32), jax.ShapeDtypeStruct((s, d), BF16), jax.ShapeDtypeStruct((1, d), F32),
                   jax.ShapeDtypeStruct((1, d), F32)],
        compiler_params=_params("arbitrary"),
    )(a, wd, h, gain, target)


def _odd_fwd(xn, h, win, sgu, ws, bfull, wout, gffn, tm, comm=None):
    s, d = h.shape
    e = win.shape[1]
    cw = e // 2
    heads = ws.shape[0]

    def body(xn_ref, h_ref, win_ref, sgu_ref, ws_ref, b_ref, wout_ref, gffn_ref,
             pre_ref, gate_ref, mixo_ref, ho_ref, hn_ref, gbuf):
        pre = _nn(xn_ref[...], win_ref[...])
        pre_ref[...] = pre.astype(BF16)
        p = _gelu(pre)
        u, v = p[:, :cw], p[:, cw:]
        vn = _rms(v, sgu_ref[...]).astype(BF16)
        for n in range(tm // CHUNK):
            rows = slice(n * CHUNK, (n + 1) * CHUNK)
            for hd in range(heads):
                cols = slice(hd * CHUNK, (hd + 1) * CHUNK)
                gbuf[rows, cols] = _nn(ws_ref[hd], vn[rows, cols]) + b_ref[:, cols]
        gate = gbuf[...]
        gate_ref[...] = gate.astype(BF16)
        mixo = (u * gate).astype(BF16)
        mixo_ref[...] = mixo
        ho = h_ref[...] + _nn(mixo, wout_ref[...])
        ho_ref[...] = ho
        hn_ref[...] = _rms(ho, gffn_ref[...]).astype(BF16)

    row = lambda c: pl.BlockSpec((tm, c), lambda i: (i, 0))
    return _pcall(
        body, comm=comm, name="odd_fwd", grid=(s // tm,),
        in_specs=[row(d), row(d), _whole(win.shape), _whole(sgu.shape), _whole(ws.shape), _whole(bfull.shape),
                  _whole(wout.shape), _whole((1, d))],
        out_specs=[row(e), row(cw), row(cw), row(d), row(d)],
        out_shape=[jax.ShapeDtypeStruct((s, e), BF16), jax.ShapeDtypeStruct((s, cw), BF16), jax.ShapeDtypeStruct((s, cw), BF16),
                   jax.ShapeDtypeStruct((s, d), F32), jax.ShapeDtypeStruct((s, d), BF16)],
        scratch_shapes=[pltpu.VMEM((tm, cw), F32)],
        compiler_params=_params("arbitrary"),
    )(xn, h, win, sgu, ws, bfull, wout, gffn)


def _ffn_bwd1(dhb, g, silu, upds, wd, cw, tm, tn, name, comm=None):
    s, d = dhb.shape
    f = g.shape[1]
    ni = s // tm

    def body(dh_ref, g_ref, silu_ref, upds_ref, wd_ref, cw_ref, dg_ref, dup_ref, dcw_ref, dcb_ref, ebuf):
        i = pl.program_id(1)

        @pl.when(i == 0)
        def _():
            ebuf[tm:tm + HALO, :] = jnp.zeros((HALO, tn), F32)

        wd_v = wd_ref[...]
        sub = min(SUB_ROWS, tm)
        sums = [None] * 4
        for c in reversed(range(tm // sub)):
            rows = pl.ds(c * sub, sub)
            da = _nt(dh_ref[pl.ds(pl.multiple_of((ni - 1 - i) * tm + c * sub, sub), sub), :], wd_v)
            dup_ref[rows, :] = (da * silu_ref[rows, :].astype(F32)).astype(BF16)
            dgc = da * upds_ref[rows, :].astype(F32)
            ebuf[rows, :] = dgc
            ext = ebuf[pl.ds(c * sub, sub + HALO), :]
            s1 = _shift(ext, 1, False)[0:sub, :]
            s2 = _shift(ext, 2, False)[0:sub, :]
            dg_ref[rows, :] = (cw_ref[2:3, :] * dgc + cw_ref[1:2, :] * s1 + cw_ref[0:1, :] * s2).astype(BF16)
            gv = g_ref[rows, :].astype(F32)
            for k, term in enumerate((s2 * gv, s1 * gv, dgc * gv, dgc)):
                part = jnp.sum(term, axis=0, keepdims=True)
                sums[k] = part if sums[k] is None else sums[k] + part
        for k in range(3):
            _acc(dcw_ref.at[k:k + 1, :], sums[k], i == 0)
        _acc(dcb_ref, sums[3], i == 0)
        ebuf[tm:tm + HALO, :] = ebuf[0:HALO, :]

    tile = pl.BlockSpec((tm, tn), lambda j, i: (ni - 1 - i, j))
    wcol = lambda r: pl.BlockSpec((r, tn), lambda j, i: (0, j))
    out = jax.ShapeDtypeStruct((s, f), BF16)
    return _pcall(
        body, comm=comm, name=name, grid=(f // tn, ni),
        in_specs=[_resident((s, d)), tile, tile, tile,
                  pl.BlockSpec((tn, d), lambda j, i: (j, 0)), wcol(3)],
        out_specs=[tile, tile, wcol(3), wcol(1)],
        out_shape=[out, out, jax.ShapeDtypeStruct((3, f), F32), jax.ShapeDtypeStruct((1, f), F32)],
        scratch_shapes=[pltpu.VMEM((tm + HALO, tn), F32)],
        compiler_params=_params("arbitrary", "arbitrary"),
    )(dhb, g, silu, upds, wd, cw)


def _ffn_bwd2(dg, dup, wg, wu, h, gain, dh, tm, name, comm=None):
    s, d = h.shape
    f = dg.shape[1]

    def body(dg_ref, dup_ref, wg_ref, wu_ref, h_ref, gain_ref, dh_ref, dho_ref, dhb_ref, dgain_ref):
        wg_v, wu_v, gain = wg_ref[...], wu_ref[...], gain_ref[...]
        sub = min(SUB_ROWS, tm)
        dgain = None
        for c in range(tm // sub):
            rows = pl.ds(c * sub, sub)
            dhn = _nn(dg_ref[rows, :], wg_v) + _nn(dup_ref[rows, :], wu_v)
            dx, part = _rms_bwd(dhn, h_ref[rows, :], gain)
            dgain = part if dgain is None else dgain + part
            dho = dh_ref[rows, :] + dx
            dho_ref[rows, :] = dho
            dhb_ref[rows, :] = dho.astype(BF16)
        _acc(dgain_ref, dgain, pl.program_id(0) == 0)

    row = lambda c: pl.BlockSpec((tm, c), lambda i: (i, 0))
    return _pcall(
        body, comm=comm, name=name, grid=(s // tm,),
        in_specs=[row(f), row(f), _resident(wg.shape), _resident(wu.shape), row(d), _whole((1, d)), row(d)],
        out_specs=[row(d), row(d), _whole((1, d))],
        out_shape=[jax.ShapeDtypeStruct((s, d), F32), jax.ShapeDtypeStruct((s, d), BF16), jax.ShapeDtypeStruct((1, d), F32)],
        compiler_params=_params("arbitrary"),
    )(dg, dup, wg, wu, h, gain, dh)


def _odd_bwd(dhb, dh, wout, pre, gate, ws, wst, sgu, win, h, gmix, tm, comm=None):
    s, d = h.shape
    e = win.shape[1]
    cw = e // 2
    heads = ws.shape[0]
    ni = s // tm

    def body(dhb_ref, dh_ref, wout_ref, pre_ref, gate_ref, ws_ref, wst_ref, sgu_ref, win_ref, h_ref, gmix_ref,
             dpre_ref, dho_ref, dhob_ref, dgain_ref, dsgu_ref, dws_ref, db_ref, vbuf, gacc):
        i = pl.program_id(0)
        first = i == 0
        dmixo = _nt(dhb_ref[...], wout_ref[...])
        pre = pre_ref[...].astype(F32)
        p = _gelu(pre)
        u, v = p[:, :cw], p[:, cw:]
        sgu = sgu_ref[...]
        rv = lax.rsqrt(jnp.mean(v * v, axis=-1, keepdims=True) + EPS)
        vh = v * rv
        vn = (vh * sgu).astype(BF16)
        du = dmixo * gate_ref[...].astype(F32)
        dgate = dmixo * u
        dgate_b = dgate.astype(BF16)
        gsum = dgate[0:CHUNK, :]
        for n in range(1, tm // CHUNK):
            gsum = gsum + dgate[n * CHUNK:(n + 1) * CHUNK, :]
        _acc(gacc, gsum, first)
        for hd in range(heads):
            cols = slice(hd * CHUNK, (hd + 1) * CHUNK)
            dws = None
            for n in range(tm // CHUNK):
                rows = slice(n * CHUNK, (n + 1) * CHUNK)
                vbuf[rows, cols] = _nn(wst_ref[hd], dgate_b[rows, cols])
                part = _nt(dgate_b[rows, cols], vn[rows, cols])
                dws = part if dws is None else dws + part
            _acc(dws_ref.at[hd], dws, first)
        dvn = vbuf[...]
        _acc(dsgu_ref, jnp.sum(dvn * vh, axis=0, keepdims=True), first)
        dvh = dvn * sgu
        dv = rv * (dvh - vh * jnp.mean(dvh * vh, axis=-1, keepdims=True))
        dpre = (jnp.concatenate([du, dv], axis=1) * _gelu_grad(pre)).astype(BF16)
        dpre_ref[...] = dpre
        dx, dgain = _rms_bwd(_nt(dpre, win_ref[...]), h_ref[...], gmix_ref[...])
        dho = dh_ref[...] + dx
        dho_ref[...] = dho
        dhob_ref[...] = dho.astype(BF16)
        _acc(dgain_ref, dgain, first)

        @pl.when(i == ni - 1)
        def _():
            ones = jnp.ones((8, CHUNK), F32)
            for hd in range(heads):
                tot = lax.dot_general(ones, gacc[:, hd * CHUNK:(hd + 1) * CHUNK], (((1,), (1,)), ((), ())),
                                      preferred_element_type=F32, precision=lax.Precision.HIGHEST)
                db_ref[hd:hd + 1, :] = tot[0:1, :]

    row = lambda c: pl.BlockSpec((tm, c), lambda i: (i, 0))
    return _pcall(
        body, comm=comm, name="odd_bwd", grid=(ni,),
        in_specs=[row(d), row(d), _whole(wout.shape), row(e), row(cw), _whole(ws.shape), _whole(wst.shape), _whole(sgu.shape),
                  _whole(win.shape), row(d), _whole((1, d))],
        out_specs=[row(e), row(d), row(d), _whole((1, d)), _whole((1, cw)), _whole(ws.shape), _whole((heads, CHUNK))],
        out_shape=[jax.ShapeDtypeStruct((s, e), BF16), jax.ShapeDtypeStruct((s, d), F32), jax.ShapeDtypeStruct((s, d), BF16),
                   jax.ShapeDtypeStruct((1, d), F32), jax.ShapeDtypeStruct((1, cw), F32), jax.ShapeDtypeStruct(ws.shape, F32),
                   jax.ShapeDtypeStruct((heads, CHUNK), F32)],
        scratch_shapes=[pltpu.VMEM((tm, cw), F32), pltpu.VMEM((CHUNK, cw), F32)],
        compiler_params=_params("arbitrary"),
    )(dhb, dh, wout, pre, gate, ws, wst, sgu, win, h, gmix)


def _even_bwd(dhb, dh, wout, proj, cq, pooled, conva, wpool, wpoolt, pscale, win, x, gmix, tm, comm=None):
    s, d = x.shape
    e = win.shape[1]
    aw = e // 4
    ni = s // tm

    def body(dhb_ref, dh_ref, wout_ref, proj_ref, cq_ref, pooled_ref, ca_ref, wp_ref, wpt_ref, ps_ref, win_ref, x_ref, gmix_ref,
             dproj_ref, dx_ref, dgain_ref, dca_ref, dwp_ref, dps_ref, cbuf, ebuf):
        i = pl.program_id(0)
        first = i == 0

        @pl.when(first)
        def _():
            cbuf[tm:tm + HALO, :] = jnp.zeros((HALO, aw), F32)
            ebuf[tm:tm + HALO, :] = jnp.zeros((HALO, aw), F32)

        dmix = _nt(dhb_ref[...], wout_ref[...])
        dy_a, dy_b = dmix[:, :aw], dmix[:, aw:]
        proj = proj_ref[...].astype(F32)
        a_b, a_c, a_v = (proj[:, k * aw:(k + 1) * aw] for k in range(3))
        da_b = dy_a * cq_ref[...].astype(F32)
        dcq = dy_a * a_b
        cbuf[0:tm, :] = dcq
        cext = cbuf[...]
        s1 = _shift(cext, 1, False)[0:tm, :]
        s2 = _shift(cext, 2, False)[0:tm, :]
        q = a_c * a_v
        for k, shifted in enumerate((s2, s1, dcq)):
            _acc(dca_ref.at[k:k + 1, :], jnp.sum(shifted * q, axis=0, keepdims=True), first)
        dq = ca_ref[2:3, :] * dcq + ca_ref[1:2, :] * s1 + ca_ref[0:1, :] * s2
        da_c = dq * a_v
        da_v = dq * a_c
        dps, dpool = [], []
        for g, w in enumerate(WINDOWS):
            cols = slice(g * LANE, (g + 1) * LANE)
            pooled = pooled_ref[:, cols]
            mixed = _nn(pooled, wp_ref[g])
            dps.append(jnp.sum(dy_b[:, cols] * mixed, axis=0, keepdims=True))
            dmixed = (dy_b[:, cols] * ps_ref[:, cols]).astype(BF16)
            _acc(dwp_ref.at[g], _tn(pooled, dmixed), first)
            dp = _nn(dmixed, wpt_ref[g])
            dpool.append(dp)
            ebuf[0:tm, cols] = dp / _counts((ni - 1 - i) * tm, tm, w)
        _acc(dps_ref, jnp.concatenate(dps, axis=1), first)
        eext = ebuf[...]
        dzs = []
        for g, w in enumerate(WINDOWS):
            cols = slice(g * LANE, (g + 1) * LANE)
            dzs.append(_window_sum(eext[:, cols], w, False)[0:tm, :] - dpool[g])
        dproj = jnp.concatenate([da_b, da_c, da_v] + dzs, axis=1).astype(BF16)
        dproj_ref[...] = dproj
        dx, dgain = _rms_bwd(_nt(dproj, win_ref[...]), x_ref[...], gmix_ref[...])
        dx_ref[...] = dh_ref[...] + dx
        _acc(dgain_ref, dgain, first)
        cbuf[tm:tm + HALO, :] = cbuf[0:HALO, :]
        ebuf[tm:tm + HALO, :] = ebuf[0:HALO, :]

    row = lambda c: pl.BlockSpec((tm, c), lambda i: (ni - 1 - i, 0))
    return _pcall(
        body, comm=comm, name="even_bwd", grid=(ni,),
        in_specs=[row(d), row(d), _whole(wout.shape), row(e), row(aw), row(aw), _whole(conva.shape), _whole(wpool.shape),
                  _whole(wpoolt.shape), _whole(pscale.shape), _whole(win.shape), row(d), _whole((1, d))],
        out_specs=[row(e), row(d), _whole((1, d)), _whole(conva.shape), _whole(wpool.shape), _whole(pscale.shape)],
        out_shape=[jax.ShapeDtypeStruct((s, e), BF16), jax.ShapeDtypeStruct((s, d), F32), jax.ShapeDtypeStruct((1, d), F32),
                   jax.ShapeDtypeStruct(conva.shape, F32), jax.ShapeDtypeStruct(wpool.shape, F32),
                   jax.ShapeDtypeStruct(pscale.shape, F32)],
        scratch_shapes=[pltpu.VMEM((tm + HALO, aw), F32), pltpu.VMEM((tm + HALO, aw), F32)],
        compiler_params=_params("arbitrary"),
    )(dhb, dh, wout, proj, cq, pooled, conva, wpool, wpoolt, pscale, win, x, gmix)


def _wgrad(a, b, tk, ts, name, comm=None):
    s, ka = a.shape
    nb = b.shape[1]
    nt = s // ts

    def body(a_ref, b_ref, o_ref, acc):
        t = pl.program_id(1)
        _acc(acc, _tn(a_ref[...], b_ref[...]), t == 0)

        @pl.when(t == nt - 1)
        def _():
            o_ref[...] = acc[...].astype(BF16)

    return _pcall(
        body, comm=comm, name=name, grid=(ka // tk, nt),
        in_specs=[pl.BlockSpec((ts, tk), lambda k, t: (t, k)),
                  _resident((s, nb)) if nt == 1 else pl.BlockSpec((ts, nb), lambda k, t: (t, 0))],
        out_specs=pl.BlockSpec((tk, nb), lambda k, t: (k, 0)),
        out_shape=jax.ShapeDtypeStruct((ka, nb), BF16),
        scratch_shapes=[pltpu.VMEM((tk, nb), F32)],
        compiler_params=_params("arbitrary", "arbitrary"),
    )(a, b)


def _adamw(parts, w, m, v, tr, name, layer=None, into=None, carried=()):
    r, c = w.shape[-2:]
    n, rp, cp = parts.shape
    assert rp >= r and r % tr == 0
    prev, carried = ([] if into is None else list(into)), list(carried)

    def body(p_ref, w_ref, m_ref, v_ref, *rest):
        g_ref, d_ref, mo_ref, vo_ref = rest[len(prev) + len(carried):len(prev) + len(carried) + 4]
        g = p_ref[0, 0:tr, 0:c].astype(F32)
        for j in range(1, n):
            g = g + p_ref[j, 0:tr, 0:c].astype(F32)
        g_ref[...] = g
        mn = ADAM_B1 * m_ref[...] + (1.0 - ADAM_B1) * g
        vn = ADAM_B2 * v_ref[...] + (1.0 - ADAM_B2) * (g * g)
        mo_ref[...] = mn
        vo_ref[...] = vn
        m_hat = mn / (1.0 - ADAM_B1 ** ADAM_STEP)
        v_hat = vn / (1.0 - ADAM_B2 ** ADAM_STEP)
        d_ref[...] = -ADAM_LR * (m_hat / (jnp.sqrt(v_hat) + ADAM_EPS) + ADAM_WD * w_ref[...])

    if layer is None:
        row = pl.BlockSpec((tr, c), lambda i: (i, 0))
    else:
        row = pl.BlockSpec((None, tr, c), lambda i: (layer, i, 0))
    out = jax.ShapeDtypeStruct(w.shape, F32)
    untouched = pl.BlockSpec(memory_space=pl.ANY)
    aliases = {4 + k: k for k in range(len(prev))}
    aliases.update({4 + len(prev) + k: 4 + k for k in range(len(carried))})
    res = _pcall(
        body, name=name, grid=(r // tr,),
        in_specs=[pl.BlockSpec((n, tr, cp), lambda i: (0, i, 0)), row, row, row] + [untouched] * (len(prev) + len(carried)),
        out_specs=[row, row, row, row] + [untouched] * len(carried),
        out_shape=[out, out, out, out] + [jax.ShapeDtypeStruct(a.shape, a.dtype) for a in carried],
        input_output_aliases=aliases,
        compiler_params=_params("arbitrary"),
    )(parts, w, m, v, *prev, *carried)
    return (res[:4], res[4:]) if carried else res


def _pair_add(grad, other, spec, core, name):
    axis, width = spec
    slot = other.shape[1:]

    def body(core_ref, g_ref, o_ref, out_ref):
        out_ref[...] = (g_ref[...].astype(F32) + o_ref[...].astype(F32)).astype(BF16)

    if axis == 0:
        gspec = pl.BlockSpec(slot, lambda q, core_ref: (2 * q + core_ref[0], 0))
    else:
        gspec = pl.BlockSpec(slot, lambda q, core_ref: (0, 2 * q + core_ref[0]))
    per_chip = pl.BlockSpec((None,) + slot, lambda q, core_ref: (q, 0, 0))
    return _pcall(
        body, name=name,
        grid_spec=pltpu.PrefetchScalarGridSpec(num_scalar_prefetch=1, grid=(N_CHIP,), in_specs=[gspec, per_chip], out_specs=per_chip),
        out_shape=jax.ShapeDtypeStruct(other.shape, BF16),
        compiler_params=_params("arbitrary"),
    )(core, grad, other)


def _hbm():
    return pl.BlockSpec(memory_space=pltpu.HBM)


def _window(ref, spec, j):
    axis, width = spec
    start = pl.multiple_of(j * width, width)
    return ref.at[(slice(None),) * axis + (pl.ds(start, width),)]


def _here():
    return lax.axis_index("x"), lax.axis_index("y"), lax.axis_index("c")


class _Gather:
    def __init__(self, shards, specs, fulls):
        n = len(shards)
        self.ins, self.specs, self.out_shape = list(shards), list(specs), list(fulls)
        self.sems = [pltpu.SemaphoreType.DMA((7 * n,)), pltpu.SemaphoreType.DMA((7 * n,)), pltpu.SemaphoreType.DMA((n,))]

    def _plan(self, ins, outs, sems):
        send_sems, recv_sems, local_sems = sems
        x, y, c = _here()
        me, sibling = (x, y, c), (x, y, 1 - c)
        chips = [(1 - x, y), (x, 1 - y), (1 - x, 1 - y)]

        def slot(t, dev):
            return _window(outs[t], self.specs[t], 4 * dev[0] + 2 * dev[1] + dev[2])

        def copy(t, k, block, to, src=None):
            return pltpu.make_async_remote_copy(
                src_ref=slot(t, block) if src is None else src, dst_ref=slot(t, block),
                send_sem=send_sems.at[7 * t + k], recv_sem=recv_sems.at[7 * t + k], device_id=to, device_id_type=MESH)

        plan = []
        for t in range(len(ins)):
            plan.append(dict(
                mine=pltpu.make_async_copy(ins[t], slot(t, me), local_sems.at[t]),
                first=[copy(t, 0, me, sibling, src=ins[t])] + [copy(t, 1 + j, me, (*q, c), src=ins[t]) for j, q in enumerate(chips)],
                over_ici=[copy(t, 1 + j, (*q, c), me) for j, q in enumerate(chips)],
                passed=[copy(t, 4 + j, (*q, c), sibling) for j, q in enumerate(chips)],
                from_sibling=[copy(t, 0, sibling, me)] + [copy(t, 4 + j, (*q, 1 - c), me) for j, q in enumerate(chips)]))
        return plan

    def start(self, ins, outs, sems):
        for p in self._plan(ins, outs, sems):
            p["mine"].start()
            for cp in p["first"]:
                cp.start()

    def middle(self, ins, outs, sems):
        for p in self._plan(ins, outs, sems):
            for arrived, onward in zip(p["over_ici"], p["passed"]):
                arrived.wait_recv()
                onward.start()

    def finish(self, ins, outs, sems):
        plan = self._plan(ins, outs, sems)
        for p in plan:
            for cp in p["from_sibling"]:
                cp.wait_recv()
        for p in plan:
            for cp in p["first"] + p["passed"]:
                cp.wait_send()
            p["mine"].wait()


class _PairExchange:
    def __init__(self, grads, specs):
        n = len(grads)
        self.ins, self.specs = list(grads), list(specs)
        self.out_shape = [jax.ShapeDtypeStruct((N_CHIP,) + a.shape[:sp[0]] + (sp[1],) + a.shape[sp[0] + 1:], a.dtype)
                          for a, sp in zip(grads, specs)]
        self.sems = [pltpu.SemaphoreType.DMA((n,)), pltpu.SemaphoreType.DMA((n,))]

    def start(self, ins, outs, sems):
        send_sems, recv_sems = sems
        x, y, c = _here()
        for t in range(len(ins)):
            for q in range(N_CHIP):
                pltpu.make_async_remote_copy(
                    src_ref=_window(ins[t], self.specs[t], 2 * q + (1 - c)), dst_ref=outs[t].at[q],
                    send_sem=send_sems.at[t], recv_sem=recv_sems.at[t], device_id=(x, y, 1 - c), device_id_type=MESH).start()

    def middle(self, ins, outs, sems):
        pass

    def finish(self, ins, outs, sems):
        send_sems, recv_sems = sems
        x, y, c = _here()
        for t in range(len(ins)):
            every = pltpu.make_async_remote_copy(src_ref=outs[t], dst_ref=outs[t], send_sem=send_sems.at[t],
                                                 recv_sem=recv_sems.at[t], device_id=(x, y, 1 - c), device_id_type=MESH)
            every.wait_send()
            every.wait_recv()


class _ChipExchange:
    def __init__(self, pairs, slotted=(), whole=()):
        self.ins = list(pairs) + list(slotted) + list(whole)
        self.npair, self.nslot = len(pairs), len(pairs) + len(slotted)
        n = len(self.ins)
        self.out_shape = ([jax.ShapeDtypeStruct(a.shape, a.dtype) for a in list(pairs) + list(slotted)]
                          + [jax.ShapeDtypeStruct((N_DEV,) + a.shape, a.dtype) for a in whole])
        self.sems = [pltpu.SemaphoreType.DMA((7 * n,)), pltpu.SemaphoreType.DMA((7 * n,)), pltpu.SemaphoreType.DMA((n,))]

    def _plan(self, ins, outs, sems):
        send_sems, recv_sems, local_sems = sems
        npair, nslot = self.npair, self.nslot
        x, y, c = _here()
        me, chip = 4 * x + 2 * y + c, 2 * x + y
        chips = [(1 - x, y), (x, 1 - y), (1 - x, 1 - y)]
        peers = [(x, y, 1 - c)] + [(*q, c) for q in chips] + [(*q, 1 - c) for q in chips]

        def index(dev):
            return 4 * dev[0] + 2 * dev[1] + dev[2]

        def copy(t, k, arriving):
            peer = peers[k]
            if t < npair:
                src, mine, theirs = ins[t].at[2 * peer[0] + peer[1]], chip, 2 * peer[0] + peer[1]
            else:
                src, mine, theirs = (ins[t].at[index(peer)] if t < nslot else ins[t]), me, index(peer)
            return pltpu.make_async_remote_copy(
                src_ref=src, dst_ref=outs[t].at[theirs if arriving else mine],
                send_sem=send_sems.at[7 * t + k], recv_sem=recv_sems.at[7 * t + k], device_id=peer, device_id_type=MESH)

        own, sent, arriving = [], [], []
        for t in range(len(ins)):
            fan = range(1, 4) if t < npair else range(7)
            if t < npair:
                own.append(pltpu.make_async_copy(ins[t].at[chip], outs[t].at[chip], local_sems.at[t]))
            else:
                own.append(pltpu.make_async_copy(ins[t].at[me] if t < nslot else ins[t], outs[t].at[me], local_sems.at[t]))
            sent += [copy(t, k, False) for k in fan]
            arriving += [copy(t, k, True) for k in fan]
        return own, sent, arriving

    def start(self, ins, outs, sems):
        own, sent, _ = self._plan(ins, outs, sems)
        for cp in own + sent:
            cp.start()

    def middle(self, ins, outs, sems):
        pass

    def finish(self, ins, outs, sems):
        own, sent, arriving = self._plan(ins, outs, sems)
        for cp in arriving:
            cp.wait_recv()
        for cp in sent:
            cp.wait_send()
        for cp in own:
            cp.wait()


class _Jobs:
    def __init__(self, *jobs):
        self.jobs = jobs
        self.ins = [a for j in jobs for a in j.ins]
        self.out_shape = [a for j in jobs for a in j.out_shape]
        self.sems = [a for j in jobs for a in j.sems]

    def _split(self, ins, outs, sems):
        i = o = s = 0
        for j in self.jobs:
            yield j, ins[i:i + len(j.ins)], outs[o:o + len(j.out_shape)], sems[s:s + len(j.sems)]
            i, o, s = i + len(j.ins), o + len(j.out_shape), s + len(j.sems)

    def start(self, ins, outs, sems):
        for j, a, b, c in self._split(ins, outs, sems):
            j.start(a, b, c)

    def middle(self, ins, outs, sems):
        for j, a, b, c in self._split(ins, outs, sems):
            j.middle(a, b, c)

    def finish(self, ins, outs, sems):
        for j, a, b, c in self._split(ins, outs, sems):
            j.finish(a, b, c)

    def results(self, outs):
        return [b for _, _, b, _ in self._split((), outs, ())]


def _alone(job, name):
    return _pcall(lambda: None, comm=job, name=name, in_specs=[], out_specs=[], out_shape=[])()[1]


SMALL_ROWS = 24
REP_COLS = 1024


def _pad_to(a, rows, cols):
    return jnp.pad(a, ((0, rows - a.shape[0]), (0, cols - a.shape[1])))


def _pack_small(conv_a, sgu_norm, conv_ffn, cols):
    return jnp.concatenate([_pad_to(conv_a, 8, cols), _pad_to(sgu_norm, 8, cols),
                            _pad_to(conv_ffn.reshape(-1, conv_ffn.shape[-1]), 8, cols)], axis=0)


def _unpack_small(p, ca_w, sg_w, cf_w):
    return p[0:3, 0:ca_w], p[8:9, 0:sg_w], p[16:22, 0:cf_w].reshape(2, 3, cf_w)


def _tile_rows(rows):
    return -(-rows // 8) * 8


def _pack_rows(parts):
    return jnp.concatenate([_pad_to(a, _tile_rows(a.shape[0]), REP_COLS) for a in parts], axis=0)


def _unpack_rows(p, shapes):
    out, r0 = [], 0
    for r, c in shapes:
        out.append(p[r0:r0 + r, 0:c])
        r0 += _tile_rows(r)
    return out


def _rep_late(norm_mix0, norm_ffn0, pool_scale, b_conv0):
    return _pack_rows([norm_mix0, norm_ffn0, pool_scale.reshape(1, -1), _pad_to(b_conv0, 1, 3 * REP_COLS).reshape(3, REP_COLS)])


def _rep_early(norm_mix1, norm_ffn1, final_norm, b_spatial, b_conv1, loss):
    return _pack_rows([norm_mix1, norm_ffn1, final_norm.reshape(1, -1), b_spatial.reshape(1, -1),
                       _pad_to(b_conv1, 1, 3 * REP_COLS).reshape(3, REP_COLS), loss])


def _unpack_rep(late, early, wsp, wp, like):
    f = like["b_conv_ffn"].shape[1]
    nm0, nf0, ps, bc0 = _unpack_rows(late, [(1, REP_COLS), (1, REP_COLS), (1, like["pool_scale"].shape[1]), (3, REP_COLS)])
    nm1, nf1, fin, bs, bc1, _ = _unpack_rows(early, [(1, REP_COLS)] * 4 + [(3, REP_COLS), (1, REP_COLS)])
    return {
        "norm_mix": jnp.concatenate([nm0, nm1]), "norm_ffn": jnp.concatenate([nf0, nf1]), "final_norm": fin[0], "pool_scale": ps,
        "b_spatial": bs.reshape(like["b_spatial"].shape),
        "b_conv_ffn": jnp.concatenate([bc0.reshape(1, -1), bc1.reshape(1, -1)])[:, 0:f],
        "w_pool": wp.reshape(like["w_pool"].shape), "w_spatial": wsp.reshape(like["w_spatial"].shape),
    }


def _pad_slots(a, width, padded):
    a = a.reshape(*a.shape[:-1], N_DEV, width)
    a = jnp.pad(a, ((0, 0),) * (a.ndim - 1) + ((0, padded - width),))
    return a.reshape(*a.shape[:-2], N_DEV * padded)


def _unpad_slots(a, width, padded):
    a = a.reshape(*a.shape[:-1], N_DEV, padded)[..., 0:width]
    return a.reshape(*a.shape[:-2], N_DEV * width)


def kernel(x, norm_mix, norm_ffn, final_norm, w_in_even, conv_a, w_pool, pool_scale, w_out_even, w_in_odd, sgu_norm, w_spatial, b_spatial, w_out_odd, w_ffn_gate, w_ffn_up, conv_ffn, b_conv_ffn, w_ffn_down, loss_target, m_norm_mix, m_norm_ffn, m_final_norm, m_w_in_even, m_conv_a, m_w_pool, m_pool_scale, m_w_out_even, m_w_in_odd, m_sgu_norm, m_w_spatial, m_b_spatial, m_w_out_odd, m_w_ffn_gate, m_w_ffn_up, m_conv_ffn, m_b_conv_ffn, m_w_ffn_down, v_norm_mix, v_norm_ffn, v_final_norm, v_w_in_even, v_conv_a, v_w_pool, v_pool_scale, v_w_out_even, v_w_in_odd, v_sgu_norm, v_w_spatial, v_b_spatial, v_w_out_odd, v_w_ffn_gate, v_w_ffn_up, v_conv_ffn, v_b_conv_ffn, v_w_ffn_down):
    s, d = x.shape[1], x.shape[2]
    x2, target = x[0], loss_target[0]
    tm = min(512, s)
    tm_wide = min(2048, s)
    tk_wide = 1024
    tn_fwd = 512
    row = lambda a: a.reshape(1, -1)
    ein, ro = w_in_even.shape[2], w_out_even.shape[1]
    fs = w_ffn_gate.shape[2]
    fsp = -(-fs // LANE) * LANE
    fp = N_DEV * fsp
    full = lambda shape, dtype=BF16: jax.ShapeDtypeStruct(shape, dtype)

    wg_s = [jnp.pad(w_ffn_gate[l].T, ((0, fsp - fs), (0, 0))).astype(BF16) for l in range(2)]
    wu_s = [jnp.pad(w_ffn_up[l].T, ((0, fsp - fs), (0, 0))).astype(BF16) for l in range(2)]
    wd_s = [jnp.pad(w_ffn_down[l], ((0, fsp - fs), (0, 0))).astype(BF16) for l in range(2)]
    small_s = _pack_small(conv_a[0], sgu_norm, conv_ffn, fsp)[None]
    in_spec, out_spec, gu_spec, down_spec = (1, ein), (0, ro), (0, fsp), (0, fsp)
    full_in, full_out, full_gu, full_down = full((d, N_DEV * ein)), full((N_DEV * ro, d)), full((fp, d)), full((fp, d))
    win_e, wout_e, gsmall = _alone(_Gather([w_in_even[0].astype(BF16), w_out_even[0].astype(BF16), small_s], [in_spec, out_spec, (0, 1)],
                                           [full_in, full_out, full((N_DEV, SMALL_ROWS, fsp), F32)]), "gather_mix0")
    ca_full = jnp.moveaxis(gsmall[:, 0:3, 0:conv_a.shape[2]], 0, 1).reshape(3, -1)
    sgu_full = gsmall[:, 8, 0:sgu_norm.shape[1]].reshape(1, -1)
    cf_full = jnp.moveaxis(gsmall[:, 16:22, :].reshape(N_DEV, 2, 3, fsp), 0, 2).reshape(2, 3, fp)
    cb_full = _pad_slots(b_conv_ffn, fs, fsp)

    tril = jnp.tril(jnp.ones((CHUNK, CHUNK), F32))
    ws_m = w_spatial[0] * tril
    ws_b = ws_m.astype(BF16)
    wst_b = jnp.swapaxes(ws_m, 1, 2).astype(BF16)
    bfull = jnp.repeat(b_spatial[0].T, CHUNK, axis=1)
    wpool_b = w_pool[0].astype(BF16)
    wpoolt_b = jnp.swapaxes(w_pool[0], 1, 2).astype(BF16)

    (xn0, proj0, cq0, pooled0, mix0, h1, hn0), (wg0,) = _even_fwd(
        x2, norm_mix[0:1], norm_ffn[0:1], win_e, ca_full, wpool_b, pool_scale, wout_e, tm,
        comm=_Gather([wg_s[0]], [gu_spec], [full_gu]))
    (g0, gc0), (wu0,) = _ffn_gate(hn0, wg0, cf_full[0], cb_full[0:1], tm_wide, tn_fwd, "ffn_gate_l0", comm=_Gather([wu_s[0]], [gu_spec], [full_gu]))
    (a0, sl0, ud0), (wd0,) = _ffn_up(hn0, wu0, gc0, tm_wide, tn_fwd, "ffn_up_l0", comm=_Gather([wd_s[0]], [down_spec], [full_down]))
    (h2, xn1), (win_o, wout_o) = _ffn_fwd2(a0, wd0, h1, norm_mix[1:2], tm, "ffn_fwd2_l0", comm=_Gather(
        [w_in_odd[0].astype(BF16), w_out_odd[0].astype(BF16)], [in_spec, out_spec], [full_in, full_out]))
    (pre1, gate1, mixo1, h3, hn1), (wg1,) = _odd_fwd(xn1, h2, win_o, sgu_full, ws_b, bfull, wout_o, norm_ffn[1:2], tm,
                                                    comm=_Gather([wg_s[1]], [gu_spec], [full_gu]))
    (g1, gc1), (wu1,) = _ffn_gate(hn1, wg1, cf_full[1], cb_full[1:2], tm_wide, tn_fwd, "ffn_gate_l1", comm=_Gather([wu_s[1]], [gu_spec], [full_gu]))
    (a1, sl1, ud1), (wd1,) = _ffn_up(hn1, wu1, gc1, tm_wide, tn_fwd, "ffn_up_l1", comm=_Gather([wd_s[1]], [down_spec], [full_down]))
    dh4, dh4b, d_final, lossvec = _ffn_fwd2_loss(a1, wd1, h3, row(final_norm), target, tm)

    core = lax.axis_index("c").astype(jnp.int32).reshape(1)
    ts = min(1024, s)
    gd1 = _wgrad(a1, dh4b, tk_wide, s, "wgrad_down_l1")
    (dg1, dup1, dcw1, dcb1), (o_d1,) = _ffn_bwd1(dh4b, g1, sl1, ud1, wd1, cf_full[1], tm_wide, tn_fwd, "ffn_bwd1_l1",
                                                 comm=_PairExchange([gd1], [down_spec]))
    p_d1 = _pair_add(gd1, o_d1, down_spec, core, "pair_add_down_l1")
    gg1 = _wgrad(dg1, hn1, tk_wide, s, "wgrad_gate_l1")
    gu1, (o_g1,) = _wgrad(dup1, hn1, tk_wide, s, "wgrad_up_l1", comm=_PairExchange([gg1], [gu_spec]))
    p_g1 = _pair_add(gg1, o_g1, gu_spec, core, "pair_add_gate_l1")
    jobs = _Jobs(_ChipExchange([p_d1]), _PairExchange([gu1], [gu_spec]))
    (dh3, dh3b, d_nffn1), res = _ffn_bwd2(dg1, dup1, wg1, wu1, h3, norm_ffn[1:2], dh4, tm, "ffn_bwd2_l1", comm=jobs)
    (s_d1,), (o_u1,) = jobs.results(res)
    p_u1 = _pair_add(gu1, o_u1, gu_spec, core, "pair_add_up_l1")
    (dpre1, dh2, dh2b, d_nmix1, d_sgu, d_ws, d_b), (s_g1, s_u1) = _odd_bwd(
        dh3b, dh3, wout_o, pre1, gate1, ws_b, wst_b, sgu_full, win_o, h2, norm_mix[1:2], tm, comm=_ChipExchange([p_g1, p_u1]))
    gi1 = _wgrad(xn1, dpre1, tk_wide, ts, "wgrad_in_odd")
    go1, (o_i1,) = _wgrad(mixo1, dh3b, tk_wide, ts, "wgrad_out_odd", comm=_PairExchange([gi1], [in_spec]))
    p_i1 = _pair_add(gi1, o_i1, in_spec, core, "pair_add_in_odd")
    d_wsp = (d_ws * tril).reshape(-1, REP_COLS).astype(BF16)
    jobs = _Jobs(_PairExchange([go1], [out_spec]), _ChipExchange([], [], [d_wsp]))
    gd0, res = _wgrad(a0, dh2b, tk_wide, s, "wgrad_down_l0", comm=jobs)
    (o_o1,), (r_wsp,) = jobs.results(res)
    p_o1 = _pair_add(go1, o_o1, out_spec, core, "pair_add_out_odd")
    d_early = _rep_early(d_nmix1, d_nffn1, d_final, d_b, _unpad_slots(dcb1, fs, fsp), lossvec)
    jobs = _Jobs(_ChipExchange([p_i1, p_o1], [], [d_early]), _PairExchange([gd0], [down_spec]))
    (dg0, dup0, dcw0, dcb0), res = _ffn_bwd1(dh2b, g0, sl0, ud0, wd0, cf_full[0], tm_wide, tn_fwd, "ffn_bwd1_l0", comm=jobs)
    (s_i1, s_o1, r_early), (o_d0,) = jobs.results(res)
    p_d0 = _pair_add(gd0, o_d0, down_spec, core, "pair_add_down_l0")
    gg0, (s_d0,) = _wgrad(dg0, hn0, tk_wide, s, "wgrad_gate_l0", comm=_ChipExchange([p_d0]))
    gu0, (o_g0,) = _wgrad(dup0, hn0, tk_wide, s, "wgrad_up_l0", comm=_PairExchange([gg0], [gu_spec]))
    p_g0 = _pair_add(gg0, o_g0, gu_spec, core, "pair_add_gate_l0")
    jobs = _Jobs(_ChipExchange([p_g0]), _PairExchange([gu0], [gu_spec]))
    (dh1, dh1b, d_nffn0), res = _ffn_bwd2(dg0, dup0, wg0, wu0, h1, norm_ffn[0:1], dh2, tm, "ffn_bwd2_l0", comm=jobs)
    (s_g0,), (o_u0,) = jobs.results(res)
    p_u0 = _pair_add(gu0, o_u0, gu_spec, core, "pair_add_up_l0")
    go0 = _wgrad(mix0, dh1b, tk_wide, ts, "wgrad_out_even")
    jobs = _Jobs(_ChipExchange([p_u0]), _PairExchange([go0], [out_spec]))
    (dproj0, grad_x, d_nmix0, d_ca, d_wp, d_ps), res = _even_bwd(
        dh1b, dh1, wout_e, proj0, cq0, pooled0, ca_full, wpool_b, wpoolt_b, pool_scale, win_e, x2, norm_mix[0:1], tm, comm=jobs)
    (s_u0,), (o_o0,) = jobs.results(res)
    p_o0 = _pair_add(go0, o_o0, out_spec, core, "pair_add_out_even")
    d_small = jnp.stack([_pack_small(a, b, c, fsp) for a, b, c in zip(
        jnp.moveaxis(d_ca.reshape(3, N_DEV, -1), 1, 0), jnp.moveaxis(d_sgu.reshape(1, N_DEV, -1), 1, 0),
        jnp.moveaxis(jnp.stack([dcw0, dcw1]).reshape(2, 3, N_DEV, fsp), 2, 0))])
    d_late = _rep_late(d_nmix0, d_nffn0, d_ps, _unpad_slots(dcb0, fs, fsp))
    d_wpl = d_wp.reshape(-1, REP_COLS).astype(BF16)
    gi0, (r_small, r_late, r_wpl) = _wgrad(xn0, dproj0, tk_wide, ts, "wgrad_in_even", comm=_ChipExchange([], [d_small], [d_late, d_wpl]))
    jobs = _Jobs(_PairExchange([gi0], [in_spec]), _ChipExchange([p_o0]))
    (o_i0,), (s_o0,) = jobs.results(_alone(jobs, "pair_exchange_in_even"))
    p_i0 = _pair_add(gi0, o_i0, in_spec, core, "pair_add_in_even")
    (s_i0,) = _alone(_ChipExchange([p_i0]), "chip_exchange_last")
    loss = jnp.sum(r_early[:, r_early.shape[1] - 8, :])

    out = {}
    out["w_in_even"], (grad_x,) = _adamw(s_i0, w_in_even[0], m_w_in_even[0], v_w_in_even[0], 256, "adamw_in_even", carried=[grad_x])
    out["w_out_even"] = _adamw(s_o0, w_out_even[0], m_w_out_even[0], v_w_out_even[0], ro // 2, "adamw_out_even")
    out["w_in_odd"] = _adamw(s_i1, w_in_odd[0], m_w_in_odd[0], v_w_in_odd[0], 256, "adamw_in_odd")
    out["w_out_odd"] = _adamw(s_o1, w_out_odd[0], m_w_out_odd[0], v_w_out_odd[0], ro // 2, "adamw_out_odd")
    tp = lambda a: jnp.swapaxes(a, 1, 2)
    for nm, s1, s0, w, m, v, back in (("w_ffn_gate", s_g1, s_g0, tp(w_ffn_gate), tp(m_w_ffn_gate), tp(v_w_ffn_gate), tp),
                                      ("w_ffn_up", s_u1, s_u0, tp(w_ffn_up), tp(m_w_ffn_up), tp(v_w_ffn_up), tp),
                                      ("w_ffn_down", s_d1, s_d0, w_ffn_down, m_w_ffn_down, v_w_ffn_down, lambda a: a)):
        l1 = _adamw(s1, w, m, v, fs // 2, "adamw_%s_l1" % nm, layer=1)
        out[nm] = [back(a) for a in _adamw(s0, w, m, v, fs // 2, "adamw_%s_l0" % nm, layer=0, into=l1)]
    small = _adamw(r_small, _pack_small(conv_a[0], sgu_norm, conv_ffn, fsp), _pack_small(m_conv_a[0], m_sgu_norm, m_conv_ffn, fsp),
                   _pack_small(v_conv_a[0], v_sgu_norm, v_conv_ffn, fsp), SMALL_ROWS, "adamw_small")
    no_loss = jnp.zeros((1, REP_COLS), F32)
    early = _adamw(r_early, *[_rep_early(nm[1:2], nf[1:2], fn, bs, bc[1:2], no_loss) for nm, nf, fn, bs, bc in (
        (norm_mix, norm_ffn, final_norm, b_spatial, b_conv_ffn), (m_norm_mix, m_norm_ffn, m_final_norm, m_b_spatial, m_b_conv_ffn),
        (v_norm_mix, v_norm_ffn, v_final_norm, v_b_spatial, v_b_conv_ffn))], r_early.shape[1], "adamw_replicated_early")
    wsp = _adamw(r_wsp, w_spatial.reshape(-1, REP_COLS), m_w_spatial.reshape(-1, REP_COLS), v_w_spatial.reshape(-1, REP_COLS),
                 r_wsp.shape[1], "adamw_w_spatial")
    late = _adamw(r_late, *[_rep_late(nm[0:1], nf[0:1], ps, bc[0:1]) for nm, nf, ps, bc in (
        (norm_mix, norm_ffn, pool_scale, b_conv_ffn), (m_norm_mix, m_norm_ffn, m_pool_scale, m_b_conv_ffn),
        (v_norm_mix, v_norm_ffn, v_pool_scale, v_b_conv_ffn))], r_late.shape[1], "adamw_replicated_late")
    wpl = _adamw(r_wpl, w_pool.reshape(-1, REP_COLS), m_w_pool.reshape(-1, REP_COLS), v_w_pool.reshape(-1, REP_COLS),
                 r_wpl.shape[1], "adamw_w_pool")

    names = ["norm_mix", "norm_ffn", "final_norm", "w_in_even", "conv_a", "w_pool", "pool_scale", "w_out_even", "w_in_odd", "sgu_norm",
             "w_spatial", "b_spatial", "w_out_odd", "w_ffn_gate", "w_ffn_up", "conv_ffn", "b_conv_ffn", "w_ffn_down"]
    like = {"norm_mix": norm_mix, "norm_ffn": norm_ffn, "final_norm": final_norm, "w_in_even": w_in_even, "conv_a": conv_a,
            "w_pool": w_pool, "pool_scale": pool_scale, "w_out_even": w_out_even, "w_in_odd": w_in_odd, "sgu_norm": sgu_norm,
            "w_spatial": w_spatial, "b_spatial": b_spatial, "w_out_odd": w_out_odd, "w_ffn_gate": w_ffn_gate, "w_ffn_up": w_ffn_up,
            "conv_ffn": conv_ffn, "b_conv_ffn": b_conv_ffn, "w_ffn_down": w_ffn_down}
    groups = []
    for k in range(4):
        ca_k, sg_k, cf_k = _unpack_small(small[k], conv_a.shape[2], sgu_norm.shape[1], conv_ffn.shape[2])
        vals = dict(_unpack_rep(late[k], early[k], wsp[k], wpl[k], like))
        vals.update(conv_a=ca_k, sgu_norm=sg_k, conv_ffn=cf_k)
        for nm in ("w_in_even", "w_in_odd", "w_out_even", "w_out_odd", "w_ffn_gate", "w_ffn_up", "w_ffn_down"):
            vals[nm] = out[nm][k]
        groups.append([vals[nm].reshape(like[nm].shape) for nm in names])
    return (loss, grad_x[None], *groups[0], *groups[1], *groups[2], *groups[3])
```

```python
import functools

import jax
import jax.numpy as jnp
from jax import lax
from jax.experimental import pallas as pl
from jax.experimental.pallas import tpu as pltpu

F32, BF16 = jnp.float32, jnp.bfloat16
EPS = 1e-6
WINDOWS = (2, 4, 8, 16)
HALO = 16
CHUNK = 128
N_DEV = 8
N_CHIP = 4
MESH = pl.DeviceIdType.MESH
VMEM_LIMIT = 56 * 2**20
LANE = 128
ADAM_LR, ADAM_B1, ADAM_B2, ADAM_EPS, ADAM_WD, ADAM_STEP = 0.001, 0.9, 0.999, 1e-08, 0.01, 10
SUB_ROWS = 256
LATE_NUM, LATE_DEN = 7, 8
INV_SQRT2 = 0.7071067811865476
INV_SQRT2PI = 0.3989422804014327


def _pcall(body, comm=None, **kw):
    if comm is None:
        return pl.pallas_call(body, **kw)
    in_specs, out_specs, out_shape = list(kw.pop("in_specs")), kw.pop("out_specs"), kw.pop("out_shape")
    single = not isinstance(out_shape, (list, tuple))
    out_specs, out_shape = ([out_specs], [out_shape]) if single else (list(out_specs), list(out_shape))
    scratch = list(kw.pop("scratch_shapes", []))
    grid = kw.get("grid", ())
    n_in, n_out, n_scr, c_in, c_out = len(in_specs), len(out_specs), len(scratch), len(comm.ins), len(comm.out_shape)

    def hosted(*refs):
        cuts = [0, n_in, n_in + c_in, n_in + c_in + n_out, n_in + c_in + n_out + c_out, n_in + c_in + n_out + c_out + n_scr, len(refs)]
        ins, cins, outs, couts, scr, sems = (refs[a:b] for a, b in zip(cuts[:-1], cuts[1:]))
        if grid:
            step, steps = 0, 1
            for axis, size in enumerate(grid):
                step, steps = step * size + pl.program_id(axis), steps * size
            pl.when(step == 0)(lambda: comm.start(cins, couts, sems))
            pl.when(step == (steps * LATE_NUM) // LATE_DEN)(lambda: comm.middle(cins, couts, sems))
            body(*ins, *outs, *scr)
            pl.when(step == steps - 1)(lambda: comm.finish(cins, couts, sems))
        else:
            comm.start(cins, couts, sems)
            comm.middle(cins, couts, sems)
            body(*ins, *outs, *scr)
            comm.finish(cins, couts, sems)

    call = pl.pallas_call(hosted, in_specs=in_specs + [_hbm()] * c_in, out_specs=out_specs + [_hbm()] * c_out,
                          out_shape=out_shape + list(comm.out_shape), scratch_shapes=scratch + list(comm.sems), **kw)

    def run(*args):
        res = call(*args, *comm.ins)
        own = res[0] if single else res[:n_out]
        return own, res[n_out:]

    return run


def _params(*sem):
    return pltpu.CompilerParams(dimension_semantics=sem, vmem_limit_bytes=VMEM_LIMIT)


def _whole(shape):
    return pl.BlockSpec(shape, lambda *_: (0,) * len(shape))


def _resident(shape):
    return pl.BlockSpec(shape, lambda *_: (0,) * len(shape), pipeline_mode=pl.Buffered(1))


def _rows(i, tm):
    return pl.ds(pl.multiple_of(i * tm, tm), tm)


def _nn(a, b):
    return jnp.dot(a, b, preferred_element_type=F32)


def _nt(a, b):
    return lax.dot_general(a, b, (((1,), (1,)), ((), ())), preferred_element_type=F32)


def _tn(a, b):
    return lax.dot_general(a, b, (((0,), (0,)), ((), ())), preferred_element_type=F32)


def _rms(x, gain):
    r = lax.rsqrt(jnp.mean(x * x, axis=-1, keepdims=True) + EPS)
    return x * r * gain


def _rms_bwd(dy, x, gain):
    r = lax.rsqrt(jnp.mean(x * x, axis=-1, keepdims=True) + EPS)
    xh = x * r
    dgain = jnp.sum(dy * xh, axis=0, keepdims=True)
    dxh = dy * gain
    dx = r * (dxh - xh * jnp.mean(dxh * xh, axis=-1, keepdims=True))
    return dx, dgain


def _gelu(x):
    return 0.5 * x * (1.0 + lax.erf(x * INV_SQRT2))


def _gelu_grad(x):
    return 0.5 * (1.0 + lax.erf(x * INV_SQRT2)) + x * jnp.exp(-0.5 * x * x) * INV_SQRT2PI


def _acc(ref, val, first):
    @pl.when(first)
    def _():
        ref[...] = val

    @pl.when(jnp.logical_not(first))
    def _():
        ref[...] += val


def _shift(ext, k, back):
    n = ext.shape[0]
    return pltpu.roll(ext, k if back else n - k, axis=0)


def _window_sum(ext, w, back):
    total, step = ext, 1
    while step < w:
        total = total + _shift(total, step, back)
        step *= 2
    return total


def _counts(row0, tm, w):
    pos1 = (row0 + lax.broadcasted_iota(jnp.int32, (tm, 1), 0) + 1).astype(F32)
    return jnp.minimum(pos1, float(w))


def _even_fwd(x, gmix, gffn, win, conva, wpool, pscale, wout, tm, comm=None):
    s, d = x.shape
    e = win.shape[1]
    aw = e // 4

    def body(x_ref, gmix_ref, gffn_ref, win_ref, ca_ref, wp_ref, ps_ref, wout_ref,
             xn_ref, proj_ref, cq_ref, pooled_ref, mix_ref, h_ref, hn_ref, qbuf, zbuf):
        i = pl.program_id(0)

        @pl.when(i == 0)
        def _():
            qbuf[0:HALO, :] = jnp.zeros((HALO, aw), F32)
            zbuf[0:HALO, :] = jnp.zeros((HALO, aw), F32)

        xv = x_ref[...]
        xn = _rms(xv, gmix_ref[...]).astype(BF16)
        xn_ref[...] = xn
        proj = _nn(xn, win_ref[...])
        proj_ref[...] = proj.astype(BF16)
        a_b, a_c, a_v, z = (proj[:, k * aw:(k + 1) * aw] for k in range(4))
        q = a_c * a_v
        qbuf[HALO:HALO + tm, :] = q
        qext = qbuf[...]
        cur = slice(HALO, HALO + tm)
        cq = ca_ref[2:3, :] * q + ca_ref[1:2, :] * _shift(qext, 1, True)[cur, :] + ca_ref[0:1, :] * _shift(qext, 2, True)[cur, :]
        cq_ref[...] = cq.astype(BF16)
        y_a = a_b * cq
        zbuf[HALO:HALO + tm, :] = z
        zext = zbuf[...]
        ys = []
        for g, w in enumerate(WINDOWS):
            cols = slice(g * LANE, (g + 1) * LANE)
            acc = _window_sum(zext[:, cols], w, True)[cur, :]
            pooled = (acc / _counts(i * tm, tm, w) - z[:, cols]).astype(BF16)
            pooled_ref[:, cols] = pooled
            ys.append(_nn(pooled, wp_ref[g]))
        y_b = jnp.concatenate(ys, axis=1) * ps_ref[...]
        mix = jnp.concatenate([y_a, y_b], axis=1).astype(BF16)
        mix_ref[...] = mix
        h = xv + _nn(mix, wout_ref[...])
        h_ref[...] = h
        hn_ref[...] = _rms(h, gffn_ref[...]).astype(BF16)
        qbuf[0:HALO, :] = qbuf[tm:tm + HALO, :]
        zbuf[0:HALO, :] = zbuf[tm:tm + HALO, :]

    row = lambda c: pl.BlockSpec((tm, c), lambda i: (i, 0))
    return _pcall(
        body, comm=comm, name="even_fwd", grid=(s // tm,),
        in_specs=[row(d), _whole((1, d)), _whole((1, d)), _whole(win.shape), _whole(conva.shape), _whole(wpool.shape),
                  _whole(pscale.shape), _whole(wout.shape)],
        out_specs=[row(d), row(e), row(aw), row(aw), row(d), row(d), row(d)],
        out_shape=[jax.ShapeDtypeStruct((s, d), BF16), jax.ShapeDtypeStruct((s, e), BF16), jax.ShapeDtypeStruct((s, aw), BF16),
                   jax.ShapeDtypeStruct((s, aw), BF16), jax.ShapeDtypeStruct((s, d), BF16), jax.ShapeDtypeStruct((s, d), F32),
                   jax.ShapeDtypeStruct((s, d), BF16)],
        scratch_shapes=[pltpu.VMEM((tm + HALO, aw), F32), pltpu.VMEM((tm + HALO, aw), F32)],
        compiler_params=_params("arbitrary"),
    )(x, gmix, gffn, win, conva, wpool, pscale, wout)


def _ffn_gate(hn, wgt, cw, cb, tm, tn, name, comm=None):
    s, d = hn.shape
    f = wgt.shape[0]
    sub = tm

    def body(hn_ref, wg_ref, cw_ref, cb_ref, g_ref, gc_ref, gbuf):
        i = pl.program_id(1)

        @pl.when(i == 0)
        def _():
            gbuf[0:HALO, :] = jnp.zeros((HALO, tn), F32)

        wg = wg_ref[...]
        for c in range(tm // sub):
            rows = pl.ds(c * sub, sub)
            g = _nt(hn_ref[pl.ds(pl.multiple_of(i * tm + c * sub, sub), sub), :], wg)
            g_ref[rows, :] = g.astype(BF16)
            gbuf[pl.ds(HALO + c * sub, sub), :] = g
            ext = gbuf[pl.ds(c * sub, sub + HALO), :]
            gc = (cw_ref[2:3, :] * g + cw_ref[1:2, :] * _shift(ext, 1, True)[HALO:, :]
                  + cw_ref[0:1, :] * _shift(ext, 2, True)[HALO:, :] + cb_ref[...])
            gc_ref[rows, :] = gc.astype(BF16)
        gbuf[0:HALO, :] = gbuf[tm:tm + HALO, :]

    tile = pl.BlockSpec((tm, tn), lambda j, i: (i, j))
    wcol = lambda r: pl.BlockSpec((r, tn), lambda j, i: (0, j))
    out = jax.ShapeDtypeStruct((s, f), BF16)
    return _pcall(
        body, comm=comm, name=name, grid=(f // tn, s // tm),
        in_specs=[_resident((s, d)), pl.BlockSpec((tn, d), lambda j, i: (j, 0)), wcol(3), wcol(1)],
        out_specs=[tile, tile], out_shape=[out, out],
        scratch_shapes=[pltpu.VMEM((tm + HALO, tn), F32)],
        compiler_params=_params("arbitrary", "arbitrary"),
    )(hn, wgt, cw, cb)


def _ffn_up(hn, wut, gc, tm, tn, name, comm=None):
    s, d = hn.shape
    f = wut.shape[0]
    sub = min(SUB_ROWS, tm)

    def body(hn_ref, wu_ref, gc_ref, a_ref, silu_ref, upds_ref):
        wu = wu_ref[...]
        for c in range(tm // sub):
            rows = pl.ds(c * sub, sub)
            up = _nt(hn_ref[pl.ds(pl.multiple_of(pl.program_id(1) * tm + c * sub, sub), sub), :], wu)
            gc = gc_ref[rows, :].astype(F32)
            sg = jax.nn.sigmoid(gc)
            silu = gc * sg
            silu_ref[rows, :] = silu.astype(BF16)
            upds_ref[rows, :] = (up * (sg * (1.0 - silu) + silu)).astype(BF16)
            a_ref[rows, :] = (silu * up).astype(BF16)

    tile = pl.BlockSpec((tm, tn), lambda j, i: (i, j))
    out = jax.ShapeDtypeStruct((s, f), BF16)
    return _pcall(
        body, comm=comm, name=name, grid=(f // tn, s // tm),
        in_specs=[_resident((s, d)), pl.BlockSpec((tn, d), lambda j, i: (j, 0)), tile],
        out_specs=[tile, tile, tile], out_shape=[out, out, out],
        compiler_params=_params("arbitrary", "arbitrary"),
    )(hn, wut, gc)


def _ffn_fwd2(a, wd, h, gain, tm, name, comm=None):
    s, d = h.shape
    f = a.shape[1]

    def body(a_ref, wd_ref, h_ref, gain_ref, ho_ref, hn_ref):
        wd_v, gain = wd_ref[...], gain_ref[...]
        sub = min(SUB_ROWS, tm)
        for c in range(tm // sub):
            rows = pl.ds(c * sub, sub)
            ho = h_ref[rows, :] + _nn(a_ref[rows, :], wd_v)
            ho_ref[rows, :] = ho
            hn_ref[rows, :] = _rms(ho, gain).astype(BF16)

    row = lambda c: pl.BlockSpec((tm, c), lambda i: (i, 0))
    return _pcall(
        body, comm=comm, name=name, grid=(s // tm,),
        in_specs=[row(f), _resident(wd.shape), row(d), _whole((1, d))],
        out_specs=[row(d), row(d)],
        out_shape=[jax.ShapeDtypeStruct((s, d), F32), jax.ShapeDtypeStruct((s, d), BF16)],
        compiler_params=_params("arbitrary"),
    )(a, wd, h, gain)


def _ffn_fwd2_loss(a, wd, h, gain, target, tm):
    s, d = h.shape
    f = a.shape[1]

    def body(a_ref, wd_ref, h_ref, gain_ref, t_ref, dh_ref, dhb_ref, dgain_ref, loss_ref):
        wd_v, gain = wd_ref[...], gain_ref[...]
        sub = min(SUB_ROWS, tm)
        dgain = loss = None
        for c in range(tm // sub):
            rows = pl.ds(c * sub, sub)
            ho = h_ref[rows, :] + _nn(a_ref[rows, :], wd_v)
            err = _rms(ho, gain) - t_ref[rows, :]
            dx, part = _rms_bwd(err * (1.0 / d), ho, gain)
            dh_ref[rows, :] = dx
            dhb_ref[rows, :] = dx.astype(BF16)
            sq = jnp.sum(err * err, axis=0, keepdims=True) * (0.5 / d)
            dgain, loss = (part, sq) if dgain is None else (dgain + part, loss + sq)
        _acc(dgain_ref, dgain, pl.program_id(0) == 0)
        _acc(loss_ref, loss, pl.program_id(0) == 0)

    row = lambda c: pl.BlockSpec((tm, c), lambda i: (i, 0))
    return _pcall(
        body, name="ffn_fwd2_loss", grid=(s // tm,),
        in_specs=[row(f), _resident(wd.shape), row(d), _whole((1, d)), row(d)],
        out_specs=[row(d), row(d), _whole((1, d)), _whole((1, d))],
        out_shape=[jax.ShapeDtypeStruct((s, d), F32), jax.ShapeDtypeStruct((s, d), BF16), jax.ShapeDtypeStruct((1, d), F32),
                   jax.ShapeDtypeStruct((1, d), F32)],
        compiler_params=_params("arbitrary"),
    )(a, wd, h, gain, target)


def _odd_fwd(xn, h, win, sgu, ws, bfull, wout, gffn, tm, comm=None):
    s, d = h.shape
    e = win.shape[1]
    cw = e // 2
    heads = ws.shape[0]

    def body(xn_ref, h_ref, win_ref, sgu_ref, ws_ref, b_ref, wout_ref, gffn_ref,
             pre_ref, gate_ref, mixo_ref, ho_ref, hn_ref, gbuf):
        pre = _nn(xn_ref[...], win_ref[...])
        pre_ref[...] = pre.astype(BF16)
        p = _gelu(pre)
        u, v = p[:, :cw], p[:, cw:]
        vn = _rms(v, sgu_ref[...]).astype(BF16)
        for n in range(tm // CHUNK):
            rows = slice(n * CHUNK, (n + 1) * CHUNK)
            for hd in range(heads):
                cols = slice(hd * CHUNK, (hd + 1) * CHUNK)
                gbuf[rows, cols] = _nn(ws_ref[hd], vn[rows, cols]) + b_ref[:, cols]
        gate = gbuf[...]
        gate_ref[...] = gate.astype(BF16)
        mixo = (u * gate).astype(BF16)
        mixo_ref[...] = mixo
        ho = h_ref[...] + _nn(mixo, wout_ref[...])
        ho_ref[...] = ho
        hn_ref[...] = _rms(ho, gffn_ref[...]).astype(BF16)

    row = lambda c: pl.BlockSpec((tm, c), lambda i: (i, 0))
    return _pcall(
        body, comm=comm, name="odd_fwd", grid=(s // tm,),
        in_specs=[row(d), row(d), _whole(win.shape), _whole(sgu.shape), _whole(ws.shape), _whole(bfull.shape),
                  _whole(wout.shape), _whole((1, d))],
        out_specs=[row(e), row(cw), row(cw), row(d), row(d)],
        out_shape=[jax.ShapeDtypeStruct((s, e), BF16), jax.ShapeDtypeStruct((s, cw), BF16), jax.ShapeDtypeStruct((s, cw), BF16),
                   jax.ShapeDtypeStruct((s, d), F32), jax.ShapeDtypeStruct((s, d), BF16)],
        scratch_shapes=[pltpu.VMEM((tm, cw), F32)],
        compiler_params=_params("arbitrary"),
    )(xn, h, win, sgu, ws, bfull, wout, gffn)


def _ffn_bwd1(dhb, g, silu, upds, wd, cw, tm, tn, name, comm=None):
    s, d = dhb.shape
    f = g.shape[1]
    ni = s // tm

    def body(dh_ref, g_ref, silu_ref, upds_ref, wd_ref, cw_ref, dg_ref, dup_ref, dcw_ref, dcb_ref, ebuf):
        i = pl.program_id(1)

        @pl.when(i == 0)
        def _():
            ebuf[tm:tm + HALO, :] = jnp.zeros((HALO, tn), F32)

        wd_v = wd_ref[...]
        sub = min(SUB_ROWS, tm)
        sums = [None] * 4
        for c in reversed(range(tm // sub)):
            rows = pl.ds(c * sub, sub)
            da = _nt(dh_ref[pl.ds(pl.multiple_of((ni - 1 - i) * tm + c * sub, sub), sub), :], wd_v)
            dup_ref[rows, :] = (da * silu_ref[rows, :].astype(F32)).astype(BF16)
            dgc = da * upds_ref[rows, :].astype(F32)
            ebuf[rows, :] = dgc
            ext = ebuf[pl.ds(c * sub, sub + HALO), :]
            s1 = _shift(ext, 1, False)[0:sub, :]
            s2 = _shift(ext, 2, False)[0:sub, :]
            dg_ref[rows, :] = (cw_ref[2:3, :] * dgc + cw_ref[1:2, :] * s1 + cw_ref[0:1, :] * s2).astype(BF16)
            gv = g_ref[rows, :].astype(F32)
            for k, term in enumerate((s2 * gv, s1 * gv, dgc * gv, dgc)):
                part = jnp.sum(term, axis=0, keepdims=True)
                sums[k] = part if sums[k] is None else sums[k] + part
        for k in range(3):
            _acc(dcw_ref.at[k:k + 1, :], sums[k], i == 0)
        _acc(dcb_ref, sums[3], i == 0)
        ebuf[tm:tm + HALO, :] = ebuf[0:HALO, :]

    tile = pl.BlockSpec((tm, tn), lambda j, i: (ni - 1 - i, j))
    wcol = lambda r: pl.BlockSpec((r, tn), lambda j, i: (0, j))
    out = jax.ShapeDtypeStruct((s, f), BF16)
    return _pcall(
        body, comm=comm, name=name, grid=(f // tn, ni),
        in_specs=[_resident((s, d)), tile, tile, tile,
                  pl.BlockSpec((tn, d), lambda j, i: (j, 0)), wcol(3)],
        out_specs=[tile, tile, wcol(3), wcol(1)],
        out_shape=[out, out, jax.ShapeDtypeStruct((3, f), F32), jax.ShapeDtypeStruct((1, f), F32)],
        scratch_shapes=[pltpu.VMEM((tm + HALO, tn), F32)],
        compiler_params=_params("arbitrary", "arbitrary"),
    )(dhb, g, silu, upds, wd, cw)


def _ffn_bwd2(dg, dup, wg, wu, h, gain, dh, tm, name, comm=None):
    s, d = h.shape
    f = dg.shape[1]

    def body(dg_ref, dup_ref, wg_ref, wu_ref, h_ref, gain_ref, dh_ref, dho_ref, dhb_ref, dgain_ref):
        wg_v, wu_v, gain = wg_ref[...], wu_ref[...], gain_ref[...]
        sub = min(SUB_ROWS, tm)
        dgain = None
        for c in range(tm // sub):
            rows = pl.ds(c * sub, sub)
            dhn = _nn(dg_ref[rows, :], wg_v) + _nn(dup_ref[rows, :], wu_v)
            dx, part = _rms_bwd(dhn, h_ref[rows, :], gain)
            dgain = part if dgain is None else dgain + part
            dho = dh_ref[rows, :] + dx
            dho_ref[rows, :] = dho
            dhb_ref[rows, :] = dho.astype(BF16)
        _acc(dgain_ref, dgain, pl.program_id(0) == 0)

    row = lambda c: pl.BlockSpec((tm, c), lambda i: (i, 0))
    return _pcall(
        body, comm=comm, name=name, grid=(s // tm,),
        in_specs=[row(f), row(f), _resident(wg.shape), _resident(wu.shape), row(d), _whole((1, d)), row(d)],
        out_specs=[row(d), row(d), _whole((1, d))],
        out_shape=[jax.ShapeDtypeStruct((s, d), F32), jax.ShapeDtypeStruct((s, d), BF16), jax.ShapeDtypeStruct((1, d), F32)],
        compiler_params=_params("arbitrary"),
    )(dg, dup, wg, wu, h, gain, dh)


def _odd_bwd(dhb, dh, wout, pre, gate, ws, wst, sgu, win, h, gmix, tm, comm=None):
    s, d = h.shape
    e = win.shape[1]
    cw = e // 2
    heads = ws.shape[0]
    ni = s // tm

    def body(dhb_ref, dh_ref, wout_ref, pre_ref, gate_ref, ws_ref, wst_ref, sgu_ref, win_ref, h_ref, gmix_ref,
             dpre_ref, dho_ref, dhob_ref, dgain_ref, dsgu_ref, dws_ref, db_ref, vbuf, gacc):
        i = pl.program_id(0)
        first = i == 0
        dmixo = _nt(dhb_ref[...], wout_ref[...])
        pre = pre_ref[...].astype(F32)
        p = _gelu(pre)
        u, v = p[:, :cw], p[:, cw:]
        sgu = sgu_ref[...]
        rv = lax.rsqrt(jnp.mean(v * v, axis=-1, keepdims=True) + EPS)
        vh = v * rv
        vn = (vh * sgu).astype(BF16)
        du = dmixo * gate_ref[...].astype(F32)
        dgate = dmixo * u
        dgate_b = dgate.astype(BF16)
        gsum = dgate[0:CHUNK, :]
        for n in range(1, tm // CHUNK):
            gsum = gsum + dgate[n * CHUNK:(n + 1) * CHUNK, :]
        _acc(gacc, gsum, first)
        for hd in range(heads):
            cols = slice(hd * CHUNK, (hd + 1) * CHUNK)
            dws = None
            for n in range(tm // CHUNK):
                rows = slice(n * CHUNK, (n + 1) * CHUNK)
                vbuf[rows, cols] = _nn(wst_ref[hd], dgate_b[rows, cols])
                part = _nt(dgate_b[rows, cols], vn[rows, cols])
                dws = part if dws is None else dws + part
            _acc(dws_ref.at[hd], dws, first)
        dvn = vbuf[...]
        _acc(dsgu_ref, jnp.sum(dvn * vh, axis=0, keepdims=True), first)
        dvh = dvn * sgu
        dv = rv * (dvh - vh * jnp.mean(dvh * vh, axis=-1, keepdims=True))
        dpre = (jnp.concatenate([du, dv], axis=1) * _gelu_grad(pre)).astype(BF16)
        dpre_ref[...] = dpre
        dx, dgain = _rms_bwd(_nt(dpre, win_ref[...]), h_ref[...], gmix_ref[...])
        dho = dh_ref[...] + dx
        dho_ref[...] = dho
        dhob_ref[...] = dho.astype(BF16)
        _acc(dgain_ref, dgain, first)

        @pl.when(i == ni - 1)
        def _():
            ones = jnp.ones((8, CHUNK), F32)
            for hd in range(heads):
                tot = lax.dot_general(ones, gacc[:, hd * CHUNK:(hd + 1) * CHUNK], (((1,), (1,)), ((), ())),
                                      preferred_element_type=F32, precision=lax.Precision.HIGHEST)
                db_ref[hd:hd + 1, :] = tot[0:1, :]

    row = lambda c: pl.BlockSpec((tm, c), lambda i: (i, 0))
    return _pcall(
        body, comm=comm, name="odd_bwd", grid=(ni,),
        in_specs=[row(d), row(d), _whole(wout.shape), row(e), row(cw), _whole(ws.shape), _whole(wst.shape), _whole(sgu.shape),
                  _whole(win.shape), row(d), _whole((1, d))],
        out_specs=[row(e), row(d), row(d), _whole((1, d)), _whole((1, cw)), _whole(ws.shape), _whole((heads, CHUNK))],
        out_shape=[jax.ShapeDtypeStruct((s, e), BF16), jax.ShapeDtypeStruct((s, d), F32), jax.ShapeDtypeStruct((s, d), BF16),
                   jax.ShapeDtypeStruct((1, d), F32), jax.ShapeDtypeStruct((1, cw), F32), jax.ShapeDtypeStruct(ws.shape, F32),
                   jax.ShapeDtypeStruct((heads, CHUNK), F32)],
        scratch_shapes=[pltpu.VMEM((tm, cw), F32), pltpu.VMEM((CHUNK, cw), F32)],
        compiler_params=_params("arbitrary"),
    )(dhb, dh, wout, pre, gate, ws, wst, sgu, win, h, gmix)


def _even_bwd(dhb, dh, wout, proj, cq, pooled, conva, wpool, wpoolt, pscale, win, x, gmix, tm, comm=None):
    s, d = x.shape
    e = win.shape[1]
    aw = e // 4
    ni = s // tm

    def body(dhb_ref, dh_ref, wout_ref, proj_ref, cq_ref, pooled_ref, ca_ref, wp_ref, wpt_ref, ps_ref, win_ref, x_ref, gmix_ref,
             dproj_ref, dx_ref, dgain_ref, dca_ref, dwp_ref, dps_ref, cbuf, ebuf):
        i = pl.program_id(0)
        first = i == 0

        @pl.when(first)
        def _():
            cbuf[tm:tm + HALO, :] = jnp.zeros((HALO, aw), F32)
            ebuf[tm:tm + HALO, :] = jnp.zeros((HALO, aw), F32)

        dmix = _nt(dhb_ref[...], wout_ref[...])
        dy_a, dy_b = dmix[:, :aw], dmix[:, aw:]
        proj = proj_ref[...].astype(F32)
        a_b, a_c, a_v = (proj[:, k * aw:(k + 1) * aw] for k in range(3))
        da_b = dy_a * cq_ref[...].astype(F32)
        dcq = dy_a * a_b
        cbuf[0:tm, :] = dcq
        cext = cbuf[...]
        s1 = _shift(cext, 1, False)[0:tm, :]
        s2 = _shift(cext, 2, False)[0:tm, :]
        q = a_c * a_v
        for k, shifted in enumerate((s2, s1, dcq)):
            _acc(dca_ref.at[k:k + 1, :], jnp.sum(shifted * q, axis=0, keepdims=True), first)
        dq = ca_ref[2:3, :] * dcq + ca_ref[1:2, :] * s1 + ca_ref[0:1, :] * s2
        da_c = dq * a_v
        da_v = dq * a_c
        dps, dpool = [], []
        for g, w in enumerate(WINDOWS):
            cols = slice(g * LANE, (g + 1) * LANE)
            pooled = pooled_ref[:, cols]
            mixed = _nn(pooled, wp_ref[g])
            dps.append(jnp.sum(dy_b[:, cols] * mixed, axis=0, keepdims=True))
            dmixed = (dy_b[:, cols] * ps_ref[:, cols]).astype(BF16)
            _acc(dwp_ref.at[g], _tn(pooled, dmixed), first)
            dp = _nn(dmixed, wpt_ref[g])
            dpool.append(dp)
            ebuf[0:tm, cols] = dp / _counts((ni - 1 - i) * tm, tm, w)
        _acc(dps_ref, jnp.concatenate(dps, axis=1), first)
        eext = ebuf[...]
        dzs = []
        for g, w in enumerate(WINDOWS):
            cols = slice(g * LANE, (g + 1) * LANE)
            dzs.append(_window_sum(eext[:, cols], w, False)[0:tm, :] - dpool[g])
        dproj = jnp.concatenate([da_b, da_c, da_v] + dzs, axis=1).astype(BF16)
        dproj_ref[...] = dproj
        dx, dgain = _rms_bwd(_nt(dproj, win_ref[...]), x_ref[...], gmix_ref[...])
        dx_ref[...] = dh_ref[...] + dx
        _acc(dgain_ref, dgain, first)
        cbuf[tm:tm + HALO, :] = cbuf[0:HALO, :]
        ebuf[tm:tm + HALO, :] = ebuf[0:HALO, :]

    row = lambda c: pl.BlockSpec((tm, c), lambda i: (ni - 1 - i, 0))
    return _pcall(
        body, comm=comm, name="even_bwd", grid=(ni,),
        in_specs=[row(d), row(d), _whole(wout.shape), row(e), row(aw), row(aw), _whole(conva.shape), _whole(wpool.shape),
                  _whole(wpoolt.shape), _whole(pscale.shape), _whole(win.shape), row(d), _whole((1, d))],
        out_specs=[row(e), row(d), _whole((1, d)), _whole(conva.shape), _whole(wpool.shape), _whole(pscale.shape)],
        out_shape=[jax.ShapeDtypeStruct((s, e), BF16), jax.ShapeDtypeStruct((s, d), F32), jax.ShapeDtypeStruct((1, d), F32),
                   jax.ShapeDtypeStruct(conva.shape, F32), jax.ShapeDtypeStruct(wpool.shape, F32),
                   jax.ShapeDtypeStruct(pscale.shape, F32)],
        scratch_shapes=[pltpu.VMEM((tm + HALO, aw), F32), pltpu.VMEM((tm + HALO, aw), F32)],
        compiler_params=_params("arbitrary"),
    )(dhb, dh, wout, proj, cq, pooled, conva, wpool, wpoolt, pscale, win, x, gmix)


def _wgrad(a, b, tk, ts, name, comm=None):
    s, ka = a.shape
    nb = b.shape[1]
    nt = s // ts

    def body(a_ref, b_ref, o_ref, acc):
        t = pl.program_id(1)
        _acc(acc, _tn(a_ref[...], b_ref[...]), t == 0)

        @pl.when(t == nt - 1)
        def _():
            o_ref[...] = acc[...].astype(BF16)

    return _pcall(
        body, comm=comm, name=name, grid=(ka // tk, nt),
        in_specs=[pl.BlockSpec((ts, tk), lambda k, t: (t, k)),
                  _resident((s, nb)) if nt == 1 else pl.BlockSpec((ts, nb), lambda k, t: (t, 0))],
        out_specs=pl.BlockSpec((tk, nb), lambda k, t: (k, 0)),
        out_shape=jax.ShapeDtypeStruct((ka, nb), BF16),
        scratch_shapes=[pltpu.VMEM((tk, nb), F32)],
        compiler_params=_params("arbitrary", "arbitrary"),
    )(a, b)


def _adamw(parts, w, m, v, tr, name, layer=None, into=None, carried=()):
    r, c = w.shape[-2:]
    n, rp, cp = parts.shape
    assert rp >= r and r % tr == 0
    prev, carried = ([] if into is None else list(into)), list(carried)

    def body(p_ref, w_ref, m_ref, v_ref, *rest):
        g_ref, d_ref, mo_ref, vo_ref = rest[len(prev) + len(carried):len(prev) + len(carried) + 4]
        g = p_ref[0, 0:tr, 0:c].astype(F32)
        for j in range(1, n):
            g = g + p_ref[j, 0:tr, 0:c].astype(F32)
        g_ref[...] = g
        mn = ADAM_B1 * m_ref[...] + (1.0 - ADAM_B1) * g
        vn = ADAM_B2 * v_ref[...] + (1.0 - ADAM_B2) * (g * g)
        mo_ref[...] = mn
        vo_ref[...] = vn
        m_hat = mn / (1.0 - ADAM_B1 ** ADAM_STEP)
        v_hat = vn / (1.0 - ADAM_B2 ** ADAM_STEP)
        d_ref[...] = -ADAM_LR * (m_hat / (jnp.sqrt(v_hat) + ADAM_EPS) + ADAM_WD * w_ref[...])

    if layer is None:
        row = pl.BlockSpec((tr, c), lambda i: (i, 0))
    else:
        row = pl.BlockSpec((None, tr, c), lambda i: (layer, i, 0))
    out = jax.ShapeDtypeStruct(w.shape, F32)
    untouched = pl.BlockSpec(memory_space=pl.ANY)
    aliases = {4 + k: k for k in range(len(prev))}
    aliases.update({4 + len(prev) + k: 4 + k for k in range(len(carried))})
    res = _pcall(
        body, name=name, grid=(r // tr,),
        in_specs=[pl.BlockSpec((n, tr, cp), lambda i: (0, i, 0)), row, row, row] + [untouched] * (len(prev) + len(carried)),
        out_specs=[row, row, row, row] + [untouched] * len(carried),
        out_shape=[out, out, out, out] + [jax.ShapeDtypeStruct(a.shape, a.dtype) for a in carried],
        input_output_aliases=aliases,
        compiler_params=_params("arbitrary"),
    )(parts, w, m, v, *prev, *carried)
    return (res[:4], res[4:]) if carried else res


def _pair_add(grad, other, spec, core, name):
    axis, width = spec
    slot = other.shape[1:]

    def body(core_ref, g_ref, o_ref, out_ref):
        out_ref[...] = (g_ref[...].astype(F32) + o_ref[...].astype(F32)).astype(BF16)

    if axis == 0:
        gspec = pl.BlockSpec(slot, lambda q, core_ref: (2 * q + core_ref[0], 0))
    else:
        gspec = pl.BlockSpec(slot, lambda q, core_ref: (0, 2 * q + core_ref[0]))
    per_chip = pl.BlockSpec((None,) + slot, lambda q, core_ref: (q, 0, 0))
    return _pcall(
        body, name=name,
        grid_spec=pltpu.PrefetchScalarGridSpec(num_scalar_prefetch=1, grid=(N_CHIP,), in_specs=[gspec, per_chip], out_specs=per_chip),
        out_shape=jax.ShapeDtypeStruct(other.shape, BF16),
        compiler_params=_params("arbitrary"),
    )(core, grad, other)


def _hbm():
    return pl.BlockSpec(memory_space=pltpu.HBM)


def _window(ref, spec, j):
    axis, width = spec
    start = pl.multiple_of(j * width, width)
    return ref.at[(slice(None),) * axis + (pl.ds(start, width),)]


def _here():
    return lax.axis_index("x"), lax.axis_index("y"), lax.axis_index("c")


class _Gather:
    def __init__(self, shards, specs, fulls):
        n = len(shards)
        self.ins, self.specs, self.out_shape = list(shards), list(specs), list(fulls)
        self.sems = [pltpu.SemaphoreType.DMA((7 * n,)), pltpu.SemaphoreType.DMA((7 * n,)), pltpu.SemaphoreType.DMA((n,))]

    def _plan(self, ins, outs, sems):
        send_sems, recv_sems, local_sems = sems
        x, y, c = _here()
        me, sibling = (x, y, c), (x, y, 1 - c)
        chips = [(1 - x, y), (x, 1 - y), (1 - x, 1 - y)]

        def slot(t, dev):
            return _window(outs[t], self.specs[t], 4 * dev[0] + 2 * dev[1] + dev[2])

        def copy(t, k, block, to, src=None):
            return pltpu.make_async_remote_copy(
                src_ref=slot(t, block) if src is None else src, dst_ref=slot(t, block),
                send_sem=send_sems.at[7 * t + k], recv_sem=recv_sems.at[7 * t + k], device_id=to, device_id_type=MESH)

        plan = []
        for t in range(len(ins)):
            plan.append(dict(
                mine=pltpu.make_async_copy(ins[t], slot(t, me), local_sems.at[t]),
                first=[copy(t, 0, me, sibling, src=ins[t])] + [copy(t, 1 + j, me, (*q, c), src=ins[t]) for j, q in enumerate(chips)],
                over_ici=[copy(t, 1 + j, (*q, c), me) for j, q in enumerate(chips)],
                passed=[copy(t, 4 + j, (*q, c), sibling) for j, q in enumerate(chips)],
                from_sibling=[copy(t, 0, sibling, me)] + [copy(t, 4 + j, (*q, 1 - c), me) for j, q in enumerate(chips)]))
        return plan

    def start(self, ins, outs, sems):
        for p in self._plan(ins, outs, sems):
            p["mine"].start()
            for cp in p["first"]:
                cp.start()

    def middle(self, ins, outs, sems):
        for p in self._plan(ins, outs, sems):
            for arrived, onward in zip(p["over_ici"], p["passed"]):
                arrived.wait_recv()
                onward.start()

    def finish(self, ins, outs, sems):
        plan = self._plan(ins, outs, sems)
        for p in plan:
            for cp in p["from_sibling"]:
                cp.wait_recv()
        for p in plan:
            for cp in p["first"] + p["passed"]:
                cp.wait_send()
            p["mine"].wait()


class _PairExchange:
    def __init__(self, grads, specs):
        n = len(grads)
        self.ins, self.specs = list(grads), list(specs)
        self.out_shape = [jax.ShapeDtypeStruct((N_CHIP,) + a.shape[:sp[0]] + (sp[1],) + a.shape[sp[0] + 1:], a.dtype)
                          for a, sp in zip(grads, specs)]
        self.sems = [pltpu.SemaphoreType.DMA((n,)), pltpu.SemaphoreType.DMA((n,))]

    def start(self, ins, outs, sems):
        send_sems, recv_sems = sems
        x, y, c = _here()
        for t in range(len(ins)):
            for q in range(N_CHIP):
                pltpu.make_async_remote_copy(
                    src_ref=_window(ins[t], self.specs[t], 2 * q + (1 - c)), dst_ref=outs[t].at[q],
                    send_sem=send_sems.at[t], recv_sem=recv_sems.at[t], device_id=(x, y, 1 - c), device_id_type=MESH).start()

    def middle(self, ins, outs, sems):
        pass

    def finish(self, ins, outs, sems):
        send_sems, recv_sems = sems
        x, y, c = _here()
        for t in range(len(ins)):
            every = pltpu.make_async_remote_copy(src_ref=outs[t], dst_ref=outs[t], send_sem=send_sems.at[t],
                                                 recv_sem=recv_sems.at[t], device_id=(x, y, 1 - c), device_id_type=MESH)
            every.wait_send()
            every.wait_recv()


class _ChipExchange:
    def __init__(self, pairs, slotted=(), whole=()):
        self.ins = list(pairs) + list(slotted) + list(whole)
        self.npair, self.nslot = len(pairs), len(pairs) + len(slotted)
        n = len(self.ins)
        self.out_shape = ([jax.ShapeDtypeStruct(a.shape, a.dtype) for a in list(pairs) + list(slotted)]
                          + [jax.ShapeDtypeStruct((N_DEV,) + a.shape, a.dtype) for a in whole])
        self.sems = [pltpu.SemaphoreType.DMA((7 * n,)), pltpu.SemaphoreType.DMA((7 * n,)), pltpu.SemaphoreType.DMA((n,))]

    def _plan(self, ins, outs, sems):
        send_sems, recv_sems, local_sems = sems
        npair, nslot = self.npair, self.nslot
        x, y, c = _here()
        me, chip = 4 * x + 2 * y + c, 2 * x + y
        chips = [(1 - x, y), (x, 1 - y), (1 - x, 1 - y)]
        peers = [(x, y, 1 - c)] + [(*q, c) for q in chips] + [(*q, 1 - c) for q in chips]

        def index(dev):
            return 4 * dev[0] + 2 * dev[1] + dev[2]

        def copy(t, k, arriving):
            peer = peers[k]
            if t < npair:
                src, mine, theirs = ins[t].at[2 * peer[0] + peer[1]], chip, 2 * peer[0] + peer[1]
            else:
                src, mine, theirs = (ins[t].at[index(peer)] if t < nslot else ins[t]), me, index(peer)
            return pltpu.make_async_remote_copy(
                src_ref=src, dst_ref=outs[t].at[theirs if arriving else mine],
                send_sem=send_sems.at[7 * t + k], recv_sem=recv_sems.at[7 * t + k], device_id=peer, device_id_type=MESH)

        own, sent, arriving = [], [], []
        for t in range(len(ins)):
            fan = range(1, 4) if t < npair else range(7)
            if t < npair:
                own.append(pltpu.make_async_copy(ins[t].at[chip], outs[t].at[chip], local_sems.at[t]))
            else:
                own.append(pltpu.make_async_copy(ins[t].at[me] if t < nslot else ins[t], outs[t].at[me], local_sems.at[t]))
            sent += [copy(t, k, False) for k in fan]
            arriving += [copy(t, k, True) for k in fan]
        return own, sent, arriving

    def start(self, ins, outs, sems):
        own, sent, _ = self._plan(ins, outs, sems)
        for cp in own + sent:
            cp.start()

    def middle(self, ins, outs, sems):
        pass

    def finish(self, ins, outs, sems):
        own, sent, arriving = self._plan(ins, outs, sems)
        for cp in arriving:
            cp.wait_recv()
        for cp in sent:
            cp.wait_send()
        for cp in own:
            cp.wait()


class _Jobs:
    def __init__(self, *jobs):
        self.jobs = jobs
        self.ins = [a for j in jobs for a in j.ins]
        self.out_shape = [a for j in jobs for a in j.out_shape]
        self.sems = [a for j in jobs for a in j.sems]

    def _split(self, ins, outs, sems):
        i = o = s = 0
        for j in self.jobs:
            yield j, ins[i:i + len(j.ins)], outs[o:o + len(j.out_shape)], sems[s:s + len(j.sems)]
            i, o, s = i + len(j.ins), o + len(j.out_shape), s + len(j.sems)

    def start(self, ins, outs, sems):
        for j, a, b, c in self._split(ins, outs, sems):
            j.start(a, b, c)

    def middle(self, ins, outs, sems):
        for j, a, b, c in self._split(ins, outs, sems):
            j.middle(a, b, c)

    def finish(self, ins, outs, sems):
        for j, a, b, c in self._split(ins, outs, sems):
            j.finish(a, b, c)

    def results(self, outs):
        return [b for _, _, b, _ in self._split((), outs, ())]


def _alone(job, name):
    return _pcall(lambda: None, comm=job, name=name, in_specs=[], out_specs=[], out_shape=[])()[1]


SMALL_ROWS = 24
REP_COLS = 1024


def _pad_to(a, rows, cols):
    return jnp.pad(a, ((0, rows - a.shape[0]), (0, cols - a.shape[1])))


def _pack_small(conv_a, sgu_norm, conv_ffn, cols):
    return jnp.concatenate([_pad_to(conv_a, 8, cols), _pad_to(sgu_norm, 8, cols),
                            _pad_to(conv_ffn.reshape(-1, conv_ffn.shape[-1]), 8, cols)], axis=0)


def _unpack_small(p, ca_w, sg_w, cf_w):
    return p[0:3, 0:ca_w], p[8:9, 0:sg_w], p[16:22, 0:cf_w].reshape(2, 3, cf_w)


def _tile_rows(rows):
    return -(-rows // 8) * 8


def _pack_rows(parts):
    return jnp.concatenate([_pad_to(a, _tile_rows(a.shape[0]), REP_COLS) for a in parts], axis=0)


def _unpack_rows(p, shapes):
    out, r0 = [], 0
    for r, c in shapes:
        out.append(p[r0:r0 + r, 0:c])
        r0 += _tile_rows(r)
    return out


def _rep_late(norm_mix0, norm_ffn0, pool_scale, b_conv0):
    return _pack_rows([norm_mix0, norm_ffn0, pool_scale.reshape(1, -1), _pad_to(b_conv0, 1, 3 * REP_COLS).reshape(3, REP_COLS)])


def _rep_early(norm_mix1, norm_ffn1, final_norm, b_spatial, b_conv1, loss):
    return _pack_rows([norm_mix1, norm_ffn1, final_norm.reshape(1, -1), b_spatial.reshape(1, -1),
                       _pad_to(b_conv1, 1, 3 * REP_COLS).reshape(3, REP_COLS), loss])


def _unpack_rep(late, early, wsp, wp, like):
    f = like["b_conv_ffn"].shape[1]
    nm0, nf0, ps, bc0 = _unpack_rows(late, [(1, REP_COLS), (1, REP_COLS), (1, like["pool_scale"].shape[1]), (3, REP_COLS)])
    nm1, nf1, fin, bs, bc1, _ = _unpack_rows(early, [(1, REP_COLS)] * 4 + [(3, REP_COLS), (1, REP_COLS)])
    return {
        "norm_mix": jnp.concatenate([nm0, nm1]), "norm_ffn": jnp.concatenate([nf0, nf1]), "final_norm": fin[0], "pool_scale": ps,
        "b_spatial": bs.reshape(like["b_spatial"].shape),
        "b_conv_ffn": jnp.concatenate([bc0.reshape(1, -1), bc1.reshape(1, -1)])[:, 0:f],
        "w_pool": wp.reshape(like["w_pool"].shape), "w_spatial": wsp.reshape(like["w_spatial"].shape),
    }


def _pad_slots(a, width, padded):
    a = a.reshape(*a.shape[:-1], N_DEV, width)
    a = jnp.pad(a, ((0, 0),) * (a.ndim - 1) + ((0, padded - width),))
    return a.reshape(*a.shape[:-2], N_DEV * padded)


def _unpad_slots(a, width, padded):
    a = a.reshape(*a.shape[:-1], N_DEV, padded)[..., 0:width]
    return a.reshape(*a.shape[:-2], N_DEV * width)


def kernel(x, norm_mix, norm_ffn, final_norm, w_in_even, conv_a, w_pool, pool_scale, w_out_even, w_in_odd, sgu_norm, w_spatial, b_spatial, w_out_odd, w_ffn_gate, w_ffn_up, conv_ffn, b_conv_ffn, w_ffn_down, loss_target, m_norm_mix, m_norm_ffn, m_final_norm, m_w_in_even, m_conv_a, m_w_pool, m_pool_scale, m_w_out_even, m_w_in_odd, m_sgu_norm, m_w_spatial, m_b_spatial, m_w_out_odd, m_w_ffn_gate, m_w_ffn_up, m_conv_ffn, m_b_conv_ffn, m_w_ffn_down, v_norm_mix, v_norm_ffn, v_final_norm, v_w_in_even, v_conv_a, v_w_pool, v_pool_scale, v_w_out_even, v_w_in_odd, v_sgu_norm, v_w_spatial, v_b_spatial, v_w_out_odd, v_w_ffn_gate, v_w_ffn_up, v_conv_ffn, v_b_conv_ffn, v_w_ffn_down):
    s, d = x.shape[1], x.shape[2]
    x2, target = x[0], loss_target[0]
    tm = min(512, s)
    tm_wide = min(2048, s)
    tk_wide = 1024
    tn_fwd = 512
    row = lambda a: a.reshape(1, -1)
    ein, ro = w_in_even.shape[2], w_out_even.shape[1]
    fs = w_ffn_gate.shape[2]
    fsp = -(-fs // LANE) * LANE
    fp = N_DEV * fsp
    full = lambda shape, dtype=BF16: jax.ShapeDtypeStruct(shape, dtype)

    wg_s = [jnp.pad(w_ffn_gate[l].T, ((0, fsp - fs), (0, 0))).astype(BF16) for l in range(2)]
    wu_s = [jnp.pad(w_ffn_up[l].T, ((0, fsp - fs), (0, 0))).astype(BF16) for l in range(2)]
    wd_s = [jnp.pad(w_ffn_down[l], ((0, fsp - fs), (0, 0))).astype(BF16) for l in range(2)]
    small_s = _pack_small(conv_a[0], sgu_norm, conv_ffn, fsp)[None]
    in_spec, out_spec, gu_spec, down_spec = (1, ein), (0, ro), (0, fsp), (0, fsp)
    full_in, full_out, full_gu, full_down = full((d, N_DEV * ein)), full((N_DEV * ro, d)), full((fp, d)), full((fp, d))
    win_e, wout_e, gsmall = _alone(_Gather([w_in_even[0].astype(BF16), w_out_even[0].astype(BF16), small_s], [in_spec, out_spec, (0, 1)],
                                           [full_in, full_out, full((N_DEV, SMALL_ROWS, fsp), F32)]), "gather_mix0")
    ca_full = jnp.moveaxis(gsmall[:, 0:3, 0:conv_a.shape[2]], 0, 1).reshape(3, -1)
    sgu_full = gsmall[:, 8, 0:sgu_norm.shape[1]].reshape(1, -1)
    cf_full = jnp.moveaxis(gsmall[:, 16:22, :].reshape(N_DEV, 2, 3, fsp), 0, 2).reshape(2, 3, fp)
    cb_full = _pad_slots(b_conv_ffn, fs, fsp)

    tril = jnp.tril(jnp.ones((CHUNK, CHUNK), F32))
    ws_m = w_spatial[0] * tril
    ws_b = ws_m.astype(BF16)
    wst_b = jnp.swapaxes(ws_m, 1, 2).astype(BF16)
    bfull = jnp.repeat(b_spatial[0].T, CHUNK, axis=1)
    wpool_b = w_pool[0].astype(BF16)
    wpoolt_b = jnp.swapaxes(w_pool[0], 1, 2).astype(BF16)

    (xn0, proj0, cq0, pooled0, mix0, h1, hn0), (wg0,) = _even_fwd(
        x2, norm_mix[0:1], norm_ffn[0:1], win_e, ca_full, wpool_b, pool_scale, wout_e, tm,
        comm=_Gather([wg_s[0]], [gu_spec], [full_gu]))
    (g0, gc0), (wu0,) = _ffn_gate(hn0, wg0, cf_full[0], cb_full[0:1], tm_wide, tn_fwd, "ffn_gate_l0", comm=_Gather([wu_s[0]], [gu_spec], [full_gu]))
    (a0, sl0, ud0), (wd0,) = _ffn_up(hn0, wu0, gc0, tm_wide, tn_fwd, "ffn_up_l0", comm=_Gather([wd_s[0]], [down_spec], [full_down]))
    (h2, xn1), (win_o, wout_o) = _ffn_fwd2(a0, wd0, h1, norm_mix[1:2], tm, "ffn_fwd2_l0", comm=_Gather(
        [w_in_odd[0].astype(BF16), w_out_odd[0].astype(BF16)], [in_spec, out_spec], [full_in, full_out]))
    (pre1, gate1, mixo1, h3, hn1), (wg1,) = _odd_fwd(xn1, h2, win_o, sgu_full, ws_b, bfull, wout_o, norm_ffn[1:2], tm,
                                                    comm=_Gather([wg_s[1]], [gu_spec], [full_gu]))
    (g1, gc1), (wu1,) = _ffn_gate(hn1, wg1, cf_full[1], cb_full[1:2], tm_wide, tn_fwd, "ffn_gate_l1", comm=_Gather([wu_s[1]], [gu_spec], [full_gu]))
    (a1, sl1, ud1), (wd1,) = _ffn_up(hn1, wu1, gc1, tm_wide, tn_fwd, "ffn_up_l1", comm=_Gather([wd_s[1]], [down_spec], [full_down]))
    dh4, dh4b, d_final, lossvec = _ffn_fwd2_loss(a1, wd1, h3, row(final_norm), target, tm)

    core = lax.axis_index("c").astype(jnp.int32).reshape(1)
    ts = min(1024, s)
    gd1 = _wgrad(a1, dh4b, tk_wide, s, "wgrad_down_l1")
    (dg1, dup1, dcw1, dcb1), (o_d1,) = _ffn_bwd1(dh4b, g1, sl1, ud1, wd1, cf_full[1], tm_wide, tn_fwd, "ffn_bwd1_l1",
                                                 comm=_PairExchange([gd1], [down_spec]))
    p_d1 = _pair_add(gd1, o_d1, down_spec, core, "pair_add_down_l1")
    gg1 = _wgrad(dg1, hn1, tk_wide, s, "wgrad_gate_l1")
    gu1, (o_g1,) = _wgrad(dup1, hn1, tk_wide, s, "wgrad_up_l1", comm=_PairExchange([gg1], [gu_spec]))
    p_g1 = _pair_add(gg1, o_g1, gu_spec, core, "pair_add_gate_l1")
    jobs = _Jobs(_ChipExchange([p_d1]), _PairExchange([gu1], [gu_spec]))
    (dh3, dh3b, d_nffn1), res = _ffn_bwd2(dg1, dup1, wg1, wu1, h3, norm_ffn[1:2], dh4, tm, "ffn_bwd2_l1", comm=jobs)
    (s_d1,), (o_u1,) = jobs.results(res)
    p_u1 = _pair_add(gu1, o_u1, gu_spec, core, "pair_add_up_l1")
    (dpre1, dh2, dh2b, d_nmix1, d_sgu, d_ws, d_b), (s_g1, s_u1) = _odd_bwd(
        dh3b, dh3, wout_o, pre1, gate1, ws_b, wst_b, sgu_full, win_o, h2, norm_mix[1:2], tm, comm=_ChipExchange([p_g1, p_u1]))
    gi1 = _wgrad(xn1, dpre1, tk_wide, ts, "wgrad_in_odd")
    go1, (o_i1,) = _wgrad(mixo1, dh3b, tk_wide, ts, "wgrad_out_odd", comm=_PairExchange([gi1], [in_spec]))
    p_i1 = _pair_add(gi1, o_i1, in_spec, core, "pair_add_in_odd")
    d_wsp = (d_ws * tril).reshape(-1, CHUNK).astype(BF16)
    jobs = _Jobs(_PairExchange([go1], [out_spec]), _ChipExchange([], [], [d_wsp]))
    gd0, res = _wgrad(a0, dh2b, tk_wide, s, "wgrad_down_l0", comm=jobs)
    (o_o1,), (r_wsp,) = jobs.results(res)
    p_o1 = _pair_add(go1, o_o1, out_spec, core, "pair_add_out_odd")
    d_early = _rep_early(d_nmix1, d_nffn1, d_final, d_b, _unpad_slots(dcb1, fs, fsp), lossvec)
    jobs = _Jobs(_ChipExchange([p_i1, p_o1], [], [d_early]), _PairExchange([gd0], [down_spec]))
    (dg0, dup0, dcw0, dcb0), res = _ffn_bwd1(dh2b, g0, sl0, ud0, wd0, cf_full[0], tm_wide, tn_fwd, "ffn_bwd1_l0", comm=jobs)
    (s_i1, s_o1, r_early), (o_d0,) = jobs.results(res)
    p_d0 = _pair_add(gd0, o_d0, down_spec, core, "pair_add_down_l0")
    gg0, (s_d0,) = _wgrad(dg0, hn0, tk_wide, s, "wgrad_gate_l0", comm=_ChipExchange([p_d0]))
    gu0, (o_g0,) = _wgrad(dup0, hn0, tk_wide, s, "wgrad_up_l0", comm=_PairExchange([gg0], [gu_spec]))
    p_g0 = _pair_add(gg0, o_g0, gu_spec, core, "pair_add_gate_l0")
    jobs = _Jobs(_ChipExchange([p_g0]), _PairExchange([gu0], [gu_spec]))
    (dh1, dh1b, d_nffn0), res = _ffn_bwd2(dg0, dup0, wg0, wu0, h1, norm_ffn[0:1], dh2, tm, "ffn_bwd2_l0", comm=jobs)
    (s_g0,), (o_u0,) = jobs.results(res)
    p_u0 = _pair_add(gu0, o_u0, gu_spec, core, "pair_add_up_l0")
    go0 = _wgrad(mix0, dh1b, tk_wide, ts, "wgrad_out_even")
    jobs = _Jobs(_ChipExchange([p_u0]), _PairExchange([go0], [out_spec]))
    (dproj0, grad_x, d_nmix0, d_ca, d_wp, d_ps), res = _even_bwd(
        dh1b, dh1, wout_e, proj0, cq0, pooled0, ca_full, wpool_b, wpoolt_b, pool_scale, win_e, x2, norm_mix[0:1], tm, comm=jobs)
    (s_u0,), (o_o0,) = jobs.results(res)
    p_o0 = _pair_add(go0, o_o0, out_spec, core, "pair_add_out_even")
    d_small = jnp.stack([_pack_small(a, b, c, fsp) for a, b, c in zip(
        jnp.moveaxis(d_ca.reshape(3, N_DEV, -1), 1, 0), jnp.moveaxis(d_sgu.reshape(1, N_DEV, -1), 1, 0),
        jnp.moveaxis(jnp.stack([dcw0, dcw1]).reshape(2, 3, N_DEV, fsp), 2, 0))])
    d_late = _rep_late(d_nmix0, d_nffn0, d_ps, _unpad_slots(dcb0, fs, fsp))
    d_wpl = d_wp.reshape(-1, CHUNK).astype(BF16)
    gi0, (r_small, r_late, r_wpl) = _wgrad(xn0, dproj0, tk_wide, ts, "wgrad_in_even", comm=_ChipExchange([], [d_small], [d_late, d_wpl]))
    jobs = _Jobs(_PairExchange([gi0], [in_spec]), _ChipExchange([p_o0]))
    (o_i0,), (s_o0,) = jobs.results(_alone(jobs, "pair_exchange_in_even"))
    p_i0 = _pair_add(gi0, o_i0, in_spec, core, "pair_add_in_even")
    (s_i0,) = _alone(_ChipExchange([p_i0]), "chip_exchange_last")
    loss = jnp.sum(r_early[:, r_early.shape[1] - 8, :])

    out = {}
    out["w_in_even"], (grad_x,) = _adamw(s_i0, w_in_even[0], m_w_in_even[0], v_w_in_even[0], 256, "adamw_in_even", carried=[grad_x])
    out["w_out_even"] = _adamw(s_o0, w_out_even[0], m_w_out_even[0], v_w_out_even[0], ro // 2, "adamw_out_even")
    out["w_in_odd"] = _adamw(s_i1, w_in_odd[0], m_w_in_odd[0], v_w_in_odd[0], 256, "adamw_in_odd")
    out["w_out_odd"] = _adamw(s_o1, w_out_odd[0], m_w_out_odd[0], v_w_out_odd[0], ro // 2, "adamw_out_odd")
    tp = lambda a: jnp.swapaxes(a, 1, 2)
    for nm, s1, s0, w, m, v, back in (("w_ffn_gate", s_g1, s_g0, tp(w_ffn_gate), tp(m_w_ffn_gate), tp(v_w_ffn_gate), tp),
                                      ("w_ffn_up", s_u1, s_u0, tp(w_ffn_up), tp(m_w_ffn_up), tp(v_w_ffn_up), tp),
                                      ("w_ffn_down", s_d1, s_d0, w_ffn_down, m_w_ffn_down, v_w_ffn_down, lambda a: a)):
        l1 = _adamw(s1, w, m, v, fs // 2, "adamw_%s_l1" % nm, layer=1)
        out[nm] = [back(a) for a in _adamw(s0, w, m, v, fs // 2, "adamw_%s_l0" % nm, layer=0, into=l1)]
    small = _adamw(r_small, _pack_small(conv_a[0], sgu_norm, conv_ffn, fsp), _pack_small(m_conv_a[0], m_sgu_norm, m_conv_ffn, fsp),
                   _pack_small(v_conv_a[0], v_sgu_norm, v_conv_ffn, fsp), SMALL_ROWS, "adamw_small")
    no_loss = jnp.zeros((1, REP_COLS), F32)
    early = _adamw(r_early, *[_rep_early(nm[1:2], nf[1:2], fn, bs, bc[1:2], no_loss) for nm, nf, fn, bs, bc in (
        (norm_mix, norm_ffn, final_norm, b_spatial, b_conv_ffn), (m_norm_mix, m_norm_ffn, m_final_norm, m_b_spatial, m_b_conv_ffn),
        (v_norm_mix, v_norm_ffn, v_final_norm, v_b_spatial, v_b_conv_ffn))], r_early.shape[1], "adamw_replicated_early")
    wsp = _adamw(r_wsp, w_spatial.reshape(-1, CHUNK), m_w_spatial.reshape(-1, CHUNK), v_w_spatial.reshape(-1, CHUNK),
                 r_wsp.shape[1], "adamw_w_spatial")
    late = _adamw(r_late, *[_rep_late(nm[0:1], nf[0:1], ps, bc[0:1]) for nm, nf, ps, bc in (
        (norm_mix, norm_ffn, pool_scale, b_conv_ffn), (m_norm_mix, m_norm_ffn, m_pool_scale, m_b_conv_ffn),
        (v_norm_mix, v_norm_ffn, v_pool_scale, v_b_conv_ffn))], r_late.shape[1], "adamw_replicated_late")
    wpl = _adamw(r_wpl, w_pool.reshape(-1, CHUNK), m_w_pool.reshape(-1, CHUNK), v_w_pool.reshape(-1, CHUNK),
                 r_wpl.shape[1], "adamw_w_pool")

    names = ["norm_mix", "norm_ffn", "final_norm", "w_in_even", "conv_a", "w_pool", "pool_scale", "w_out_even", "w_in_odd", "sgu_norm",
             "w_spatial", "b_spatial", "w_out_odd", "w_ffn_gate", "w_ffn_up", "conv_ffn", "b_conv_ffn", "w_ffn_down"]
    like = {"norm_mix": norm_mix, "norm_ffn": norm_ffn, "final_norm": final_norm, "w_in_even": w_in_even, "conv_a": conv_a,
            "w_pool": w_pool, "pool_scale": pool_scale, "w_out_even": w_out_even, "w_in_odd": w_in_odd, "sgu_norm": sgu_norm,
            "w_spatial": w_spatial, "b_spatial": b_spatial, "w_out_odd": w_out_odd, "w_ffn_gate": w_ffn_gate, "w_ffn_up": w_ffn_up,
            "conv_ffn": conv_ffn, "b_conv_ffn": b_conv_ffn, "w_ffn_down": w_ffn_down}
    groups = []
    for k in range(4):
        ca_k, sg_k, cf_k = _unpack_small(small[k], conv_a.shape[2], sgu_norm.shape[1], conv_ffn.shape[2])
        vals = dict(_unpack_rep(late[k], early[k], wsp[k], wpl[k], like))
        vals.update(conv_a=ca_k, sgu_norm=sg_k, conv_ffn=cf_k)
        for nm in ("w_in_even", "w_in_odd", "w_out_even", "w_out_odd", "w_ffn_gate", "w_ffn_up", "w_ffn_down"):
            vals[nm] = out[nm][k]
        groups.append([vals[nm].reshape(like[nm].shape) for nm in names])
    return (loss, grad_x[None], *groups[0], *groups[1], *groups[2], *groups[3])
```

```python
import functools

import jax
import jax.numpy as jnp
from jax import lax
from jax.experimental import pallas as pl
from jax.experimental.pallas import tpu as pltpu

F32, BF16 = jnp.float32, jnp.bfloat16
EPS = 1e-6
WINDOWS = (2, 4, 8, 16)
HALO = 16
CHUNK = 128
N_DEV = 8
N_CHIP = 4
MESH = pl.DeviceIdType.MESH
VMEM_LIMIT = 56 * 2**20
LANE = 128
ADAM_LR, ADAM_B1, ADAM_B2, ADAM_EPS, ADAM_WD, ADAM_STEP = 0.001, 0.9, 0.999, 1e-08, 0.01, 10
SUB_ROWS = 256
LATE_NUM, LATE_DEN = 7, 8
INV_SQRT2 = 0.7071067811865476
INV_SQRT2PI = 0.3989422804014327


def _pcall(body, comm=None, **kw):
    if comm is None:
        return pl.pallas_call(body, **kw)
    in_specs, out_specs, out_shape = list(kw.pop("in_specs")), kw.pop("out_specs"), kw.pop("out_shape")
    single = not isinstance(out_shape, (list, tuple))
    out_specs, out_shape = ([out_specs], [out_shape]) if single else (list(out_specs), list(out_shape))
    scratch = list(kw.pop("scratch_shapes", []))
    grid = kw.get("grid", ())
    n_in, n_out, n_scr, c_in, c_out = len(in_specs), len(out_specs), len(scratch), len(comm.ins), len(comm.out_shape)

    def hosted(*refs):
        cuts = [0, n_in, n_in + c_in, n_in + c_in + n_out, n_in + c_in + n_out + c_out, n_in + c_in + n_out + c_out + n_scr, len(refs)]
        ins, cins, outs, couts, scr, sems = (refs[a:b] for a, b in zip(cuts[:-1], cuts[1:]))
        if grid:
            step, steps = 0, 1
            for axis, size in enumerate(grid):
                step, steps = step * size + pl.program_id(axis), steps * size
            pl.when(step == 0)(lambda: comm.start(cins, couts, sems))
            pl.when(step == (steps * LATE_NUM) // LATE_DEN)(lambda: comm.middle(cins, couts, sems))
            body(*ins, *outs, *scr)
            pl.when(step == steps - 1)(lambda: comm.finish(cins, couts, sems))
        else:
            comm.start(cins, couts, sems)
            comm.middle(cins, couts, sems)
            body(*ins, *outs, *scr)
            comm.finish(cins, couts, sems)

    call = pl.pallas_call(hosted, in_specs=in_specs + [_hbm()] * c_in, out_specs=out_specs + [_hbm()] * c_out,
                          out_shape=out_shape + list(comm.out_shape), scratch_shapes=scratch + list(comm.sems), **kw)

    def run(*args):
        res = call(*args, *comm.ins)
        own = res[0] if single else res[:n_out]
        return own, res[n_out:]

    return run


def _params(*sem):
    return pltpu.CompilerParams(dimension_semantics=sem, vmem_limit_bytes=VMEM_LIMIT)


def _whole(shape):
    return pl.BlockSpec(shape, lambda *_: (0,) * len(shape))


def _resident(shape):
    return pl.BlockSpec(shape, lambda *_: (0,) * len(shape), pipeline_mode=pl.Buffered(1))


def _rows(i, tm):
    return pl.ds(pl.multiple_of(i * tm, tm), tm)


def _nn(a, b):
    return jnp.dot(a, b, preferred_element_type=F32)


def _nt(a, b):
    return lax.dot_general(a, b, (((1,), (1,)), ((), ())), preferred_element_type=F32)


def _tn(a, b):
    return lax.dot_general(a, b, (((0,), (0,)), ((), ())), preferred_element_type=F32)


def _rms(x, gain):
    r = lax.rsqrt(jnp.mean(x * x, axis=-1, keepdims=True) + EPS)
    return x * r * gain


def _rms_bwd(dy, x, gain):
    r = lax.rsqrt(jnp.mean(x * x, axis=-1, keepdims=True) + EPS)
    xh = x * r
    dgain = jnp.sum(dy * xh, axis=0, keepdims=True)
    dxh = dy * gain
    dx = r * (dxh - xh * jnp.mean(dxh * xh, axis=-1, keepdims=True))
    return dx, dgain


def _gelu(x):
    return 0.5 * x * (1.0 + lax.erf(x * INV_SQRT2))


def _gelu_grad(x):
    return 0.5 * (1.0 + lax.erf(x * INV_SQRT2)) + x * jnp.exp(-0.5 * x * x) * INV_SQRT2PI


def _acc(ref, val, first):
    @pl.when(first)
    def _():
        ref[...] = val

    @pl.when(jnp.logical_not(first))
    def _():
        ref[...] += val


def _shift(ext, k, back):
    n = ext.shape[0]
    return pltpu.roll(ext, k if back else n - k, axis=0)


def _window_sum(ext, w, back):
    total, step = ext, 1
    while step < w:
        total = total + _shift(total, step, back)
        step *= 2
    return total


def _counts(row0, tm, w):
    pos1 = (row0 + lax.broadcasted_iota(jnp.int32, (tm, 1), 0) + 1).astype(F32)
    return jnp.minimum(pos1, float(w))


def _even_fwd(x, gmix, gffn, win, conva, wpool, pscale, wout, tm, comm=None):
    s, d = x.shape
    e = win.shape[1]
    aw = e // 4

    def body(x_ref, gmix_ref, gffn_ref, win_ref, ca_ref, wp_ref, ps_ref, wout_ref,
             xn_ref, proj_ref, cq_ref, pooled_ref, mix_ref, h_ref, hn_ref, qbuf, zbuf):
        i = pl.program_id(0)

        @pl.when(i == 0)
        def _():
            qbuf[0:HALO, :] = jnp.zeros((HALO, aw), F32)
            zbuf[0:HALO, :] = jnp.zeros((HALO, aw), F32)

        xv = x_ref[...]
        xn = _rms(xv, gmix_ref[...]).astype(BF16)
        xn_ref[...] = xn
        proj = _nn(xn, win_ref[...])
        proj_ref[...] = proj.astype(BF16)
        a_b, a_c, a_v, z = (proj[:, k * aw:(k + 1) * aw] for k in range(4))
        q = a_c * a_v
        qbuf[HALO:HALO + tm, :] = q
        qext = qbuf[...]
        cur = slice(HALO, HALO + tm)
        cq = ca_ref[2:3, :] * q + ca_ref[1:2, :] * _shift(qext, 1, True)[cur, :] + ca_ref[0:1, :] * _shift(qext, 2, True)[cur, :]
        cq_ref[...] = cq.astype(BF16)
        y_a = a_b * cq
        zbuf[HALO:HALO + tm, :] = z
        zext = zbuf[...]
        ys = []
        for g, w in enumerate(WINDOWS):
            cols = slice(g * LANE, (g + 1) * LANE)
            acc = _window_sum(zext[:, cols], w, True)[cur, :]
            pooled = (acc / _counts(i * tm, tm, w) - z[:, cols]).astype(BF16)
            pooled_ref[:, cols] = pooled
            ys.append(_nn(pooled, wp_ref[g]))
        y_b = jnp.concatenate(ys, axis=1) * ps_ref[...]
        mix = jnp.concatenate([y_a, y_b], axis=1).astype(BF16)
        mix_ref[...] = mix
        h = xv + _nn(mix, wout_ref[...])
        h_ref[...] = h
        hn_ref[...] = _rms(h, gffn_ref[...]).astype(BF16)
        qbuf[0:HALO, :] = qbuf[tm:tm + HALO, :]
        zbuf[0:HALO, :] = zbuf[tm:tm + HALO, :]

    row = lambda c: pl.BlockSpec((tm, c), lambda i: (i, 0))
    return _pcall(
        body, comm=comm, name="even_fwd", grid=(s // tm,),
        in_specs=[row(d), _whole((1, d)), _whole((1, d)), _whole(win.shape), _whole(conva.shape), _whole(wpool.shape),
                  _whole(pscale.shape), _whole(wout.shape)],
        out_specs=[row(d), row(e), row(aw), row(aw), row(d), row(d), row(d)],
        out_shape=[jax.ShapeDtypeStruct((s, d), BF16), jax.ShapeDtypeStruct((s, e), BF16), jax.ShapeDtypeStruct((s, aw), BF16),
                   jax.ShapeDtypeStruct((s, aw), BF16), jax.ShapeDtypeStruct((s, d), BF16), jax.ShapeDtypeStruct((s, d), F32),
                   jax.ShapeDtypeStruct((s, d), BF16)],
        scratch_shapes=[pltpu.VMEM((tm + HALO, aw), F32), pltpu.VMEM((tm + HALO, aw), F32)],
        compiler_params=_params("arbitrary"),
    )(x, gmix, gffn, win, conva, wpool, pscale, wout)


def _ffn_gate(hn, wgt, cw, cb, tm, tn, name, comm=None):
    s, d = hn.shape
    f = wgt.shape[0]
    sub = tm

    def body(hn_ref, wg_ref, cw_ref, cb_ref, g_ref, gc_ref, gbuf):
        i = pl.program_id(1)

        @pl.when(i == 0)
        def _():
            gbuf[0:HALO, :] = jnp.zeros((HALO, tn), F32)

        wg = wg_ref[...]
        for c in range(tm // sub):
            rows = pl.ds(c * sub, sub)
            g = _nt(hn_ref[pl.ds(pl.multiple_of(i * tm + c * sub, sub), sub), :], wg)
            g_ref[rows, :] = g.astype(BF16)
            gbuf[pl.ds(HALO + c * sub, sub), :] = g
            ext = gbuf[pl.ds(c * sub, sub + HALO), :]
            gc = (cw_ref[2:3, :] * g + cw_ref[1:2, :] * _shift(ext, 1, True)[HALO:, :]
                  + cw_ref[0:1, :] * _shift(ext, 2, True)[HALO:, :] + cb_ref[...])
            gc_ref[rows, :] = gc.astype(BF16)
        gbuf[0:HALO, :] = gbuf[tm:tm + HALO, :]

    tile = pl.BlockSpec((tm, tn), lambda j, i: (i, j))
    wcol = lambda r: pl.BlockSpec((r, tn), lambda j, i: (0, j))
    out = jax.ShapeDtypeStruct((s, f), BF16)
    return _pcall(
        body, comm=comm, name=name, grid=(f // tn, s // tm),
        in_specs=[_resident((s, d)), pl.BlockSpec((tn, d), lambda j, i: (j, 0)), wcol(3), wcol(1)],
        out_specs=[tile, tile], out_shape=[out, out],
        scratch_shapes=[pltpu.VMEM((tm + HALO, tn), F32)],
        compiler_params=_params("arbitrary", "arbitrary"),
    )(hn, wgt, cw, cb)


def _ffn_up(hn, wut, gc, tm, tn, name, comm=None):
    s, d = hn.shape
    f = wut.shape[0]
    sub = min(SUB_ROWS, tm)

    def body(hn_ref, wu_ref, gc_ref, a_ref, silu_ref, upds_ref):
        wu = wu_ref[...]
        for c in range(tm // sub):
            rows = pl.ds(c * sub, sub)
            up = _nt(hn_ref[pl.ds(pl.multiple_of(pl.program_id(1) * tm + c * sub, sub), sub), :], wu)
            gc = gc_ref[rows, :].astype(F32)
            sg = jax.nn.sigmoid(gc)
            silu = gc * sg
            silu_ref[rows, :] = silu.astype(BF16)
            upds_ref[rows, :] = (up * (sg * (1.0 - silu) + silu)).astype(BF16)
            a_ref[rows, :] = (silu * up).astype(BF16)

    tile = pl.BlockSpec((tm, tn), lambda j, i: (i, j))
    out = jax.ShapeDtypeStruct((s, f), BF16)
    return _pcall(
        body, comm=comm, name=name, grid=(f // tn, s // tm),
        in_specs=[_resident((s, d)), pl.BlockSpec((tn, d), lambda j, i: (j, 0)), tile],
        out_specs=[tile, tile, tile], out_shape=[out, out, out],
        compiler_params=_params("arbitrary", "arbitrary"),
    )(hn, wut, gc)


def _ffn_fwd2(a, wd, h, gain, tm, name, comm=None):
    s, d = h.shape
    f = a.shape[1]

    def body(a_ref, wd_ref, h_ref, gain_ref, ho_ref, hn_ref):
        wd_v, gain = wd_ref[...], gain_ref[...]
        sub = min(SUB_ROWS, tm)
        for c in range(tm // sub):
            rows = pl.ds(c * sub, sub)
            ho = h_ref[rows, :] + _nn(a_ref[rows, :], wd_v)
            ho_ref[rows, :] = ho
            hn_ref[rows, :] = _rms(ho, gain).astype(BF16)

    row = lambda c: pl.BlockSpec((tm, c), lambda i: (i, 0))
    return _pcall(
        body, comm=comm, name=name, grid=(s // tm,),
        in_specs=[row(f), _resident(wd.shape), row(d), _whole((1, d))],
        out_specs=[row(d), row(d)],
        out_shape=[jax.ShapeDtypeStruct((s, d), F32), jax.ShapeDtypeStruct((s, d), BF16)],
        compiler_params=_params("arbitrary"),
    )(a, wd, h, gain)


def _ffn_fwd2_loss(a, wd, h, gain, target, tm):
    s, d = h.shape
    f = a.shape[1]

    def body(a_ref, wd_ref, h_ref, gain_ref, t_ref, dh_ref, dhb_ref, dgain_ref, loss_ref):
        wd_v, gain = wd_ref[...], gain_ref[...]
        sub = min(SUB_ROWS, tm)
        dgain = loss = None
        for c in range(tm // sub):
            rows = pl.ds(c * sub, sub)
            ho = h_ref[rows, :] + _nn(a_ref[rows, :], wd_v)
            err = _rms(ho, gain) - t_ref[rows, :]
            dx, part = _rms_bwd(err * (1.0 / d), ho, gain)
            dh_ref[rows, :] = dx
            dhb_ref[rows, :] = dx.astype(BF16)
            sq = jnp.sum(err * err, axis=0, keepdims=True) * (0.5 / d)
            dgain, loss = (part, sq) if dgain is None else (dgain + part, loss + sq)
        _acc(dgain_ref, dgain, pl.program_id(0) == 0)
        _acc(loss_ref, loss, pl.program_id(0) == 0)

    row = lambda c: pl.BlockSpec((tm, c), lambda i: (i, 0))
    return _pcall(
        body, name="ffn_fwd2_loss", grid=(s // tm,),
        in_specs=[row(f), _resident(wd.shape), row(d), _whole((1, d)), row(d)],
        out_specs=[row(d), row(d), _whole((1, d)), _whole((1, d))],
        out_shape=[jax.ShapeDtypeStruct((s, d), F32), jax.ShapeDtypeStruct((s, d), BF16), jax.ShapeDtypeStruct((1, d), F32),
                   jax.ShapeDtypeStruct((1, d), F32)],
        compiler_params=_params("arbitrary"),
    )(a, wd, h, gain, target)


def _odd_fwd(xn, h, win, sgu, ws, bfull, wout, gffn, tm, comm=None):
    s, d = h.shape
    e = win.shape[1]
    cw = e // 2
    heads = ws.shape[0]

    def body(xn_ref, h_ref, win_ref, sgu_ref, ws_ref, b_ref, wout_ref, gffn_ref,
             pre_ref, gate_ref, mixo_ref, ho_ref, hn_ref, gbuf):
        pre = _nn(xn_ref[...], win_ref[...])
        pre_ref[...] = pre.astype(BF16)
        p = _gelu(pre)
        u, v = p[:, :cw], p[:, cw:]
        vn = _rms(v, sgu_ref[...]).astype(BF16)
        for n in range(tm // CHUNK):
            rows = slice(n * CHUNK, (n + 1) * CHUNK)
            for hd in range(heads):
                cols = slice(hd * CHUNK, (hd + 1) * CHUNK)
                gbuf[rows, cols] = _nn(ws_ref[hd], vn[rows, cols]) + b_ref[:, cols]
        gate = gbuf[...]
        gate_ref[...] = gate.astype(BF16)
        mixo = (u * gate).astype(BF16)
        mixo_ref[...] = mixo
        ho = h_ref[...] + _nn(mixo, wout_ref[...])
        ho_ref[...] = ho
        hn_ref[...] = _rms(ho, gffn_ref[...]).astype(BF16)

    row = lambda c: pl.BlockSpec((tm, c), lambda i: (i, 0))
    return _pcall(
        body, comm=comm, name="odd_fwd", grid=(s // tm,),
        in_specs=[row(d), row(d), _whole(win.shape), _whole(sgu.shape), _whole(ws.shape), _whole(bfull.shape),
                  _whole(wout.shape), _whole((1, d))],
        out_specs=[row(e), row(cw), row(cw), row(d), row(d)],
        out_shape=[jax.ShapeDtypeStruct((s, e), BF16), jax.ShapeDtypeStruct((s, cw), BF16), jax.ShapeDtypeStruct((s, cw), BF16),
                   jax.ShapeDtypeStruct((s, d), F32), jax.ShapeDtypeStruct((s, d), BF16)],
        scratch_shapes=[pltpu.VMEM((tm, cw), F32)],
        compiler_params=_params("arbitrary"),
    )(xn, h, win, sgu, ws, bfull, wout, gffn)


def _ffn_bwd1(dhb, g, silu, upds, wd, cw, tm, tn, name, comm=None):
    s, d = dhb.shape
    f = g.shape[1]
    ni = s // tm

    def body(dh_ref, g_ref, silu_ref, upds_ref, wd_ref, cw_ref, dg_ref, dup_ref, dcw_ref, dcb_ref, ebuf):
        i = pl.program_id(1)

        @pl.when(i == 0)
        def _():
            ebuf[tm:tm + HALO, :] = jnp.zeros((HALO, tn), F32)

        wd_v = wd_ref[...]
        sub = min(SUB_ROWS, tm)
        sums = [None] * 4
        for c in reversed(range(tm // sub)):
            rows = pl.ds(c * sub, sub)
            da = _nt(dh_ref[pl.ds(pl.multiple_of((ni - 1 - i) * tm + c * sub, sub), sub), :], wd_v)
            dup_ref[rows, :] = (da * silu_ref[rows, :].astype(F32)).astype(BF16)
            dgc = da * upds_ref[rows, :].astype(F32)
            ebuf[rows, :] = dgc
            ext = ebuf[pl.ds(c * sub, sub + HALO), :]
            s1 = _shift(ext, 1, False)[0:sub, :]
            s2 = _shift(ext, 2, False)[0:sub, :]
            dg_ref[rows, :] = (cw_ref[2:3, :] * dgc + cw_ref[1:2, :] * s1 + cw_ref[0:1, :] * s2).astype(BF16)
            gv = g_ref[rows, :].astype(F32)
            for k, term in enumerate((s2 * gv, s1 * gv, dgc * gv, dgc)):
                part = jnp.sum(term, axis=0, keepdims=True)
                sums[k] = part if sums[k] is None else sums[k] + part
        for k in range(3):
            _acc(dcw_ref.at[k:k + 1, :], sums[k], i == 0)
        _acc(dcb_ref, sums[3], i == 0)
        ebuf[tm:tm + HALO, :] = ebuf[0:HALO, :]

    tile = pl.BlockSpec((tm, tn), lambda j, i: (ni - 1 - i, j))
    wcol = lambda r: pl.BlockSpec((r, tn), lambda j, i: (0, j))
    out = jax.ShapeDtypeStruct((s, f), BF16)
    return _pcall(
        body, comm=comm, name=name, grid=(f // tn, ni),
        in_specs=[_resident((s, d)), tile, tile, tile,
                  pl.BlockSpec((tn, d), lambda j, i: (j, 0)), wcol(3)],
        out_specs=[tile, tile, wcol(3), wcol(1)],
        out_shape=[out, out, jax.ShapeDtypeStruct((3, f), F32), jax.ShapeDtypeStruct((1, f), F32)],
        scratch_shapes=[pltpu.VMEM((tm + HALO, tn), F32)],
        compiler_params=_params("arbitrary", "arbitrary"),
    )(dhb, g, silu, upds, wd, cw)


def _ffn_bwd2(dg, dup, wg, wu, h, gain, dh, tm, name, comm=None):
    s, d = h.shape
    f = dg.shape[1]

    def body(dg_ref, dup_ref, wg_ref, wu_ref, h_ref, gain_ref, dh_ref, dho_ref, dhb_ref, dgain_ref):
        wg_v, wu_v, gain = wg_ref[...], wu_ref[...], gain_ref[...]
        sub = min(SUB_ROWS, tm)
        dgain = None
        for c in range(tm // sub):
            rows = pl.ds(c * sub, sub)
            dhn = _nn(dg_ref[rows, :], wg_v) + _nn(dup_ref[rows, :], wu_v)
            dx, part = _rms_bwd(dhn, h_ref[rows, :], gain)
            dgain = part if dgain is None else dgain + part
            dho = dh_ref[rows, :] + dx
            dho_ref[rows, :] = dho
            dhb_ref[rows, :] = dho.astype(BF16)
        _acc(dgain_ref, dgain, pl.program_id(0) == 0)

    row = lambda c: pl.BlockSpec((tm, c), lambda i: (i, 0))
    return _pcall(
        body, comm=comm, name=name, grid=(s // tm,),
        in_specs=[row(f), row(f), _resident(wg.shape), _resident(wu.shape), row(d), _whole((1, d)), row(d)],
        out_specs=[row(d), row(d), _whole((1, d))],
        out_shape=[jax.ShapeDtypeStruct((s, d), F32), jax.ShapeDtypeStruct((s, d), BF16), jax.ShapeDtypeStruct((1, d), F32)],
        compiler_params=_params("arbitrary"),
    )(dg, dup, wg, wu, h, gain, dh)


def _odd_bwd(dhb, dh, wout, pre, gate, ws, wst, sgu, win, h, gmix, tm, comm=None):
    s, d = h.shape
    e = win.shape[1]
    cw = e // 2
    heads = ws.shape[0]
    ni = s // tm

    def body(dhb_ref, dh_ref, wout_ref, pre_ref, gate_ref, ws_ref, wst_ref, sgu_ref, win_ref, h_ref, gmix_ref,
             dpre_ref, dho_ref, dhob_ref, dgain_ref, dsgu_ref, dws_ref, db_ref, vbuf, gacc):
        i = pl.program_id(0)
        first = i == 0
        dmixo = _nt(dhb_ref[...], wout_ref[...])
        pre = pre_ref[...].astype(F32)
        p = _gelu(pre)
        u, v = p[:, :cw], p[:, cw:]
        sgu = sgu_ref[...]
        rv = lax.rsqrt(jnp.mean(v * v, axis=-1, keepdims=True) + EPS)
        vh = v * rv
        vn = (vh * sgu).astype(BF16)
        du = dmixo * gate_ref[...].astype(F32)
        dgate = dmixo * u
        dgate_b = dgate.astype(BF16)
        gsum = dgate[0:CHUNK, :]
        for n in range(1, tm // CHUNK):
            gsum = gsum + dgate[n * CHUNK:(n + 1) * CHUNK, :]
        _acc(gacc, gsum, first)
        for hd in range(heads):
            cols = slice(hd * CHUNK, (hd + 1) * CHUNK)
            dws = None
            for n in range(tm // CHUNK):
                rows = slice(n * CHUNK, (n + 1) * CHUNK)
                vbuf[rows, cols] = _nn(wst_ref[hd], dgate_b[rows, cols])
                part = _nt(dgate_b[rows, cols], vn[rows, cols])
                dws = part if dws is None else dws + part
            _acc(dws_ref.at[hd], dws, first)
        dvn = vbuf[...]
        _acc(dsgu_ref, jnp.sum(dvn * vh, axis=0, keepdims=True), first)
        dvh = dvn * sgu
        dv = rv * (dvh - vh * jnp.mean(dvh * vh, axis=-1, keepdims=True))
        dpre = (jnp.concatenate([du, dv], axis=1) * _gelu_grad(pre)).astype(BF16)
        dpre_ref[...] = dpre
        dx, dgain = _rms_bwd(_nt(dpre, win_ref[...]), h_ref[...], gmix_ref[...])
        dho = dh_ref[...] + dx
        dho_ref[...] = dho
        dhob_ref[...] = dho.astype(BF16)
        _acc(dgain_ref, dgain, first)

        @pl.when(i == ni - 1)
        def _():
            ones = jnp.ones((8, CHUNK), F32)
            for hd in range(heads):
                tot = lax.dot_general(ones, gacc[:, hd * CHUNK:(hd + 1) * CHUNK], (((1,), (1,)), ((), ())),
                                      preferred_element_type=F32, precision=lax.Precision.HIGHEST)
                db_ref[hd:hd + 1, :] = tot[0:1, :]

    row = lambda c: pl.BlockSpec((tm, c), lambda i: (i, 0))
    return _pcall(
        body, comm=comm, name="odd_bwd", grid=(ni,),
        in_specs=[row(d), row(d), _whole(wout.shape), row(e), row(cw), _whole(ws.shape), _whole(wst.shape), _whole(sgu.shape),
                  _whole(win.shape), row(d), _whole((1, d))],
        out_specs=[row(e), row(d), row(d), _whole((1, d)), _whole((1, cw)), _whole(ws.shape), _whole((heads, CHUNK))],
        out_shape=[jax.ShapeDtypeStruct((s, e), BF16), jax.ShapeDtypeStruct((s, d), F32), jax.ShapeDtypeStruct((s, d), BF16),
                   jax.ShapeDtypeStruct((1, d), F32), jax.ShapeDtypeStruct((1, cw), F32), jax.ShapeDtypeStruct(ws.shape, F32),
                   jax.ShapeDtypeStruct((heads, CHUNK), F32)],
        scratch_shapes=[pltpu.VMEM((tm, cw), F32), pltpu.VMEM((CHUNK, cw), F32)],
        compiler_params=_params("arbitrary"),
    )(dhb, dh, wout, pre, gate, ws, wst, sgu, win, h, gmix)


def _even_bwd(dhb, dh, wout, proj, cq, pooled, conva, wpool, wpoolt, pscale, win, x, gmix, tm, comm=None):
    s, d = x.shape
    e = win.shape[1]
    aw = e // 4
    ni = s // tm

    def body(dhb_ref, dh_ref, wout_ref, proj_ref, cq_ref, pooled_ref, ca_ref, wp_ref, wpt_ref, ps_ref, win_ref, x_ref, gmix_ref,
             dproj_ref, dx_ref, dgain_ref, dca_ref, dwp_ref, dps_ref, cbuf, ebuf):
        i = pl.program_id(0)
        first = i == 0

        @pl.when(first)
        def _():
            cbuf[tm:tm + HALO, :] = jnp.zeros((HALO, aw), F32)
            ebuf[tm:tm + HALO, :] = jnp.zeros((HALO, aw), F32)

        dmix = _nt(dhb_ref[...], wout_ref[...])
        dy_a, dy_b = dmix[:, :aw], dmix[:, aw:]
        proj = proj_ref[...].astype(F32)
        a_b, a_c, a_v = (proj[:, k * aw:(k + 1) * aw] for k in range(3))
        da_b = dy_a * cq_ref[...].astype(F32)
        dcq = dy_a * a_b
        cbuf[0:tm, :] = dcq
        cext = cbuf[...]
        s1 = _shift(cext, 1, False)[0:tm, :]
        s2 = _shift(cext, 2, False)[0:tm, :]
        q = a_c * a_v
        for k, shifted in enumerate((s2, s1, dcq)):
            _acc(dca_ref.at[k:k + 1, :], jnp.sum(shifted * q, axis=0, keepdims=True), first)
        dq = ca_ref[2:3, :] * dcq + ca_ref[1:2, :] * s1 + ca_ref[0:1, :] * s2
        da_c = dq * a_v
        da_v = dq * a_c
        dps, dpool = [], []
        for g, w in enumerate(WINDOWS):
            cols = slice(g * LANE, (g + 1) * LANE)
            pooled = pooled_ref[:, cols]
            mixed = _nn(pooled, wp_ref[g])
            dps.append(jnp.sum(dy_b[:, cols] * mixed, axis=0, keepdims=True))
            dmixed = (dy_b[:, cols] * ps_ref[:, cols]).astype(BF16)
            _acc(dwp_ref.at[g], _tn(pooled, dmixed), first)
            dp = _nn(dmixed, wpt_ref[g])
            dpool.append(dp)
            ebuf[0:tm, cols] = dp / _counts((ni - 1 - i) * tm, tm, w)
        _acc(dps_ref, jnp.concatenate(dps, axis=1), first)
        eext = ebuf[...]
        dzs = []
        for g, w in enumerate(WINDOWS):
            cols = slice(g * LANE, (g + 1) * LANE)
            dzs.append(_window_sum(eext[:, cols], w, False)[0:tm, :] - dpool[g])
        dproj = jnp.concatenate([da_b, da_c, da_v] + dzs, axis=1).astype(BF16)
        dproj_ref[...] = dproj
        dx, dgain = _rms_bwd(_nt(dproj, win_ref[...]), x_ref[...], gmix_ref[...])
        dx_ref[...] = dh_ref[...] + dx
        _acc(dgain_ref, dgain, first)
        cbuf[tm:tm + HALO, :] = cbuf[0:HALO, :]
        ebuf[tm:tm + HALO, :] = ebuf[0:HALO, :]

    row = lambda c: pl.BlockSpec((tm, c), lambda i: (ni - 1 - i, 0))
    return _pcall(
        body, comm=comm, name="even_bwd", grid=(ni,),
        in_specs=[row(d), row(d), _whole(wout.shape), row(e), row(aw), row(aw), _whole(conva.shape), _whole(wpool.shape),
                  _whole(wpoolt.shape), _whole(pscale.shape), _whole(win.shape), row(d), _whole((1, d))],
        out_specs=[row(e), row(d), _whole((1, d)), _whole(conva.shape), _whole(wpool.shape), _whole(pscale.shape)],
        out_shape=[jax.ShapeDtypeStruct((s, e), BF16), jax.ShapeDtypeStruct((s, d), F32), jax.ShapeDtypeStruct((1, d), F32),
                   jax.ShapeDtypeStruct(conva.shape, F32), jax.ShapeDtypeStruct(wpool.shape, F32),
                   jax.ShapeDtypeStruct(pscale.shape, F32)],
        scratch_shapes=[pltpu.VMEM((tm + HALO, aw), F32), pltpu.VMEM((tm + HALO, aw), F32)],
        compiler_params=_params("arbitrary"),
    )(dhb, dh, wout, proj, cq, pooled, conva, wpool, wpoolt, pscale, win, x, gmix)


def _wgrad(a, b, tk, ts, name, comm=None):
    s, ka = a.shape
    nb = b.shape[1]
    nt = s // ts

    def body(a_ref, b_ref, o_ref, acc):
        t = pl.program_id(1)
        _acc(acc, _tn(a_ref[...], b_ref[...]), t == 0)

        @pl.when(t == nt - 1)
        def _():
            o_ref[...] = acc[...].astype(BF16)

    return _pcall(
        body, comm=comm, name=name, grid=(ka // tk, nt),
        in_specs=[pl.BlockSpec((ts, tk), lambda k, t: (t, k)),
                  _resident((s, nb)) if nt == 1 else pl.BlockSpec((ts, nb), lambda k, t: (t, 0))],
        out_specs=pl.BlockSpec((tk, nb), lambda k, t: (k, 0)),
        out_shape=jax.ShapeDtypeStruct((ka, nb), BF16),
        scratch_shapes=[pltpu.VMEM((tk, nb), F32)],
        compiler_params=_params("arbitrary", "arbitrary"),
    )(a, b)


def _adamw(parts, w, m, v, tr, name, layer=None, into=None, carried=()):
    r, c = w.shape[-2:]
    n, rp, cp = parts.shape
    assert rp >= r and r % tr == 0
    prev, carried = ([] if into is None else list(into)), list(carried)

    def body(p_ref, w_ref, m_ref, v_ref, *rest):
        g_ref, d_ref, mo_ref, vo_ref = rest[len(prev) + len(carried):len(prev) + len(carried) + 4]
        g = p_ref[0, 0:tr, 0:c].astype(F32)
        for j in range(1, n):
            g = g + p_ref[j, 0:tr, 0:c].astype(F32)
        g_ref[...] = g
        mn = ADAM_B1 * m_ref[...] + (1.0 - ADAM_B1) * g
        vn = ADAM_B2 * v_ref[...] + (1.0 - ADAM_B2) * (g * g)
        mo_ref[...] = mn
        vo_ref[...] = vn
        m_hat = mn / (1.0 - ADAM_B1 ** ADAM_STEP)
        v_hat = vn / (1.0 - ADAM_B2 ** ADAM_STEP)
        d_ref[...] = -ADAM_LR * (m_hat / (jnp.sqrt(v_hat) + ADAM_EPS) + ADAM_WD * w_ref[...])

    if layer is None:
        row = pl.BlockSpec((tr, c), lambda i: (i, 0))
    else:
        row = pl.BlockSpec((None, tr, c), lambda i: (layer, i, 0))
    out = jax.ShapeDtypeStruct(w.shape, F32)
    untouched = pl.BlockSpec(memory_space=pl.ANY)
    aliases = {4 + k: k for k in range(len(prev))}
    aliases.update({4 + len(prev) + k: 4 + k for k in range(len(carried))})
    res = _pcall(
        body, name=name, grid=(r // tr,),
        in_specs=[pl.BlockSpec((n, tr, cp), lambda i: (0, i, 0)), row, row, row] + [untouched] * (len(prev) + len(carried)),
        out_specs=[row, row, row, row] + [untouched] * len(carried),
        out_shape=[out, out, out, out] + [jax.ShapeDtypeStruct(a.shape, a.dtype) for a in carried],
        input_output_aliases=aliases,
        compiler_params=_params("arbitrary"),
    )(parts, w, m, v, *prev, *carried)
    return (res[:4], res[4:]) if carried else res


def _pair_add(grad, other, spec, core, name):
    axis, width = spec
    slot = other.shape[1:]

    def body(core_ref, g_ref, o_ref, out_ref):
        out_ref[...] = (g_ref[...].astype(F32) + o_ref[...].astype(F32)).astype(BF16)

    if axis == 0:
        gspec = pl.BlockSpec(slot, lambda q, core_ref: (2 * q + core_ref[0], 0))
    else:
        gspec = pl.BlockSpec(slot, lambda q, core_ref: (0, 2 * q + core_ref[0]))
    per_chip = pl.BlockSpec((None,) + slot, lambda q, core_ref: (q, 0, 0))
    return _pcall(
        body, name=name,
        grid_spec=pltpu.PrefetchScalarGridSpec(num_scalar_prefetch=1, grid=(N_CHIP,), in_specs=[gspec, per_chip], out_specs=per_chip),
        out_shape=jax.ShapeDtypeStruct(other.shape, BF16),
        compiler_params=_params("arbitrary"),
    )(core, grad, other)


def _hbm():
    return pl.BlockSpec(memory_space=pltpu.HBM)


def _window(ref, spec, j):
    axis, width = spec
    start = pl.multiple_of(j * width, width)
    return ref.at[(slice(None),) * axis + (pl.ds(start, width),)]


def _here():
    return lax.axis_index("x"), lax.axis_index("y"), lax.axis_index("c")


class _Gather:
    def __init__(self, shards, specs, fulls):
        n = len(shards)
        self.ins, self.specs, self.out_shape = list(shards), list(specs), list(fulls)
        self.sems = [pltpu.SemaphoreType.DMA((7 * n,)), pltpu.SemaphoreType.DMA((7 * n,)), pltpu.SemaphoreType.DMA((n,))]

    def _plan(self, ins, outs, sems):
        send_sems, recv_sems, local_sems = sems
        x, y, c = _here()
        me, sibling = (x, y, c), (x, y, 1 - c)
        chips = [(1 - x, y), (x, 1 - y), (1 - x, 1 - y)]

        def slot(t, dev):
            return _window(outs[t], self.specs[t], 4 * dev[0] + 2 * dev[1] + dev[2])

        def copy(t, k, block, to, src=None):
            return pltpu.make_async_remote_copy(
                src_ref=slot(t, block) if src is None else src, dst_ref=slot(t, block),
                send_sem=send_sems.at[7 * t + k], recv_sem=recv_sems.at[7 * t + k], device_id=to, device_id_type=MESH)

        plan = []
        for t in range(len(ins)):
            plan.append(dict(
                mine=pltpu.make_async_copy(ins[t], slot(t, me), local_sems.at[t]),
                first=[copy(t, 0, me, sibling, src=ins[t])] + [copy(t, 1 + j, me, (*q, c), src=ins[t]) for j, q in enumerate(chips)],
                over_ici=[copy(t, 1 + j, (*q, c), me) for j, q in enumerate(chips)],
                passed=[copy(t, 4 + j, (*q, c), sibling) for j, q in enumerate(chips)],
                from_sibling=[copy(t, 0, sibling, me)] + [copy(t, 4 + j, (*q, 1 - c), me) for j, q in enumerate(chips)]))
        return plan

    def start(self, ins, outs, sems):
        for p in self._plan(ins, outs, sems):
            p["mine"].start()
            for cp in p["first"]:
                cp.start()

    def middle(self, ins, outs, sems):
        for p in self._plan(ins, outs, sems):
            for arrived, onward in zip(p["over_ici"], p["passed"]):
                arrived.wait_recv()
                onward.start()

    def finish(self, ins, outs, sems):
        plan = self._plan(ins, outs, sems)
        for p in plan:
            for cp in p["from_sibling"]:
                cp.wait_recv()
        for p in plan:
            for cp in p["first"] + p["passed"]:
                cp.wait_send()
            p["mine"].wait()


class _PairExchange:
    def __init__(self, grads, specs):
        n = len(grads)
        self.ins, self.specs = list(grads), list(specs)
        self.out_shape = [jax.ShapeDtypeStruct((N_CHIP,) + a.shape[:sp[0]] + (sp[1],) + a.shape[sp[0] + 1:], a.dtype)
                          for a, sp in zip(grads, specs)]
        self.sems = [pltpu.SemaphoreType.DMA((n,)), pltpu.SemaphoreType.DMA((n,))]

    def start(self, ins, outs, sems):
        send_sems, recv_sems = sems
        x, y, c = _here()
        for t in range(len(ins)):
            for q in range(N_CHIP):
                pltpu.make_async_remote_copy(
                    src_ref=_window(ins[t], self.specs[t], 2 * q + (1 - c)), dst_ref=outs[t].at[q],
                    send_sem=send_sems.at[t], recv_sem=recv_sems.at[t], device_id=(x, y, 1 - c), device_id_type=MESH).start()

    def middle(self, ins, outs, sems):
        pass

    def finish(self, ins, outs, sems):
        send_sems, recv_sems = sems
        x, y, c = _here()
        for t in range(len(ins)):
            every = pltpu.make_async_remote_copy(src_ref=outs[t], dst_ref=outs[t], send_sem=send_sems.at[t],
                                                 recv_sem=recv_sems.at[t], device_id=(x, y, 1 - c), device_id_type=MESH)
            every.wait_send()
            every.wait_recv()


class _ChipExchange:
    def __init__(self, pairs, slotted=(), whole=()):
        self.ins = list(pairs) + list(slotted) + list(whole)
        self.npair, self.nslot = len(pairs), len(pairs) + len(slotted)
        n = len(self.ins)
        self.out_shape = ([jax.ShapeDtypeStruct(a.shape, a.dtype) for a in list(pairs) + list(slotted)]
                          + [jax.ShapeDtypeStruct((N_DEV,) + a.shape, a.dtype) for a in whole])
        self.sems = [pltpu.SemaphoreType.DMA((7 * n,)), pltpu.SemaphoreType.DMA((7 * n,)), pltpu.SemaphoreType.DMA((n,))]

    def _plan(self, ins, outs, sems):
        send_sems, recv_sems, local_sems = sems
        npair, nslot = self.npair, self.nslot
        x, y, c = _here()
        me, chip = 4 * x + 2 * y + c, 2 * x + y
        chips = [(1 - x, y), (x, 1 - y), (1 - x, 1 - y)]
        peers = [(x, y, 1 - c)] + [(*q, c) for q in chips] + [(*q, 1 - c) for q in chips]

        def index(dev):
            return 4 * dev[0] + 2 * dev[1] + dev[2]

        def copy(t, k, arriving):
            peer = peers[k]
            if t < npair:
                src, mine, theirs = ins[t].at[2 * peer[0] + peer[1]], chip, 2 * peer[0] + peer[1]
            else:
                src, mine, theirs = (ins[t].at[index(peer)] if t < nslot else ins[t]), me, index(peer)
            return pltpu.make_async_remote_copy(
                src_ref=src, dst_ref=outs[t].at[theirs if arriving else mine],
                send_sem=send_sems.at[7 * t + k], recv_sem=recv_sems.at[7 * t + k], device_id=peer, device_id_type=MESH)

        own, sent, arriving = [], [], []
        for t in range(len(ins)):
            fan = range(1, 4) if t < npair else range(7)
            if t < npair:
                own.append(pltpu.make_async_copy(ins[t].at[chip], outs[t].at[chip], local_sems.at[t]))
            else:
                own.append(pltpu.make_async_copy(ins[t].at[me] if t < nslot else ins[t], outs[t].at[me], local_sems.at[t]))
            sent += [copy(t, k, False) for k in fan]
            arriving += [copy(t, k, True) for k in fan]
        return own, sent, arriving

    def start(self, ins, outs, sems):
        own, sent, _ = self._plan(ins, outs, sems)
        for cp in own + sent:
            cp.start()

    def middle(self, ins, outs, sems):
        pass

    def finish(self, ins, outs, sems):
        own, sent, arriving = self._plan(ins, outs, sems)
        for cp in arriving:
            cp.wait_recv()
        for cp in sent:
            cp.wait_send()
        for cp in own:
            cp.wait()


class _Jobs:
    def __init__(self, *jobs):
        self.jobs = jobs
        self.ins = [a for j in jobs for a in j.ins]
        self.out_shape = [a for j in jobs for a in j.out_shape]
        self.sems = [a for j in jobs for a in j.sems]

    def _split(self, ins, outs, sems):
        i = o = s = 0
        for j in self.jobs:
            yield j, ins[i:i + len(j.ins)], outs[o:o + len(j.out_shape)], sems[s:s + len(j.sems)]
            i, o, s = i + len(j.ins), o + len(j.out_shape), s + len(j.sems)

    def start(self, ins, outs, sems):
        for j, a, b, c in self._split(ins, outs, sems):
            j.start(a, b, c)

    def middle(self, ins, outs, sems):
        for j, a, b, c in self._split(ins, outs, sems):
            j.middle(a, b, c)

    def finish(self, ins, outs, sems):
        for j, a, b, c in self._split(ins, outs, sems):
            j.finish(a, b, c)

    def results(self, outs):
        return [b for _, _, b, _ in self._split((), outs, ())]


def _alone(job, name):
    return _pcall(lambda: None, comm=job, name=name, in_specs=[], out_specs=[], out_shape=[])()[1]


SMALL_ROWS = 24
REP_COLS = 1024


def _pad_to(a, rows, cols):
    return jnp.pad(a, ((0, rows - a.shape[0]), (0, cols - a.shape[1])))


def _pack_small(conv_a, sgu_norm, conv_ffn, cols):
    return jnp.concatenate([_pad_to(conv_a, 8, cols), _pad_to(sgu_norm, 8, cols),
                            _pad_to(conv_ffn.reshape(-1, conv_ffn.shape[-1]), 8, cols)], axis=0)


def _unpack_small(p, ca_w, sg_w, cf_w):
    return p[0:3, 0:ca_w], p[8:9, 0:sg_w], p[16:22, 0:cf_w].reshape(2, 3, cf_w)


def _tile_rows(rows):
    return -(-rows // 8) * 8


def _pack_rows(parts):
    return jnp.concatenate([_pad_to(a, _tile_rows(a.shape[0]), REP_COLS) for a in parts], axis=0)


def _unpack_rows(p, shapes):
    out, r0 = [], 0
    for r, c in shapes:
        out.append(p[r0:r0 + r, 0:c])
        r0 += _tile_rows(r)
    return out


def _rep_late(norm_mix0, norm_ffn0, pool_scale, b_conv0):
    return _pack_rows([norm_mix0, norm_ffn0, pool_scale.reshape(1, -1), _pad_to(b_conv0, 1, 3 * REP_COLS).reshape(3, REP_COLS)])


def _rep_early(norm_mix1, norm_ffn1, final_norm, b_spatial, b_conv1, loss):
    return _pack_rows([norm_mix1, norm_ffn1, final_norm.reshape(1, -1), b_spatial.reshape(1, -1),
                       _pad_to(b_conv1, 1, 3 * REP_COLS).reshape(3, REP_COLS), loss])


def _unpack_rep(late, early, wsp, wp, like):
    f = like["b_conv_ffn"].shape[1]
    nm0, nf0, ps, bc0 = _unpack_rows(late, [(1, REP_COLS), (1, REP_COLS), (1, like["pool_scale"].shape[1]), (3, REP_COLS)])
    nm1, nf1, fin, bs, bc1, _ = _unpack_rows(early, [(1, REP_COLS)] * 4 + [(3, REP_COLS), (1, REP_COLS)])
    return {
        "norm_mix": jnp.concatenate([nm0, nm1]), "norm_ffn": jnp.concatenate([nf0, nf1]), "final_norm": fin[0], "pool_scale": ps,
        "b_spatial": bs.reshape(like["b_spatial"].shape),
        "b_conv_ffn": jnp.concatenate([bc0.reshape(1, -1), bc1.reshape(1, -1)])[:, 0:f],
        "w_pool": wp.reshape(like["w_pool"].shape), "w_spatial": wsp.reshape(like["w_spatial"].shape),
    }


def _pad_slots(a, width, padded):
    a = a.reshape(*a.shape[:-1], N_DEV, width)
    a = jnp.pad(a, ((0, 0),) * (a.ndim - 1) + ((0, padded - width),))
    return a.reshape(*a.shape[:-2], N_DEV * padded)


def _unpad_slots(a, width, padded):
    a = a.reshape(*a.shape[:-1], N_DEV, padded)[..., 0:width]
    return a.reshape(*a.shape[:-2], N_DEV * width)


def kernel(x, norm_mix, norm_ffn, final_norm, w_in_even, conv_a, w_pool, pool_scale, w_out_even, w_in_odd, sgu_norm, w_spatial, b_spatial, w_out_odd, w_ffn_gate, w_ffn_up, conv_ffn, b_conv_ffn, w_ffn_down, loss_target, m_norm_mix, m_norm_ffn, m_final_norm, m_w_in_even, m_conv_a, m_w_pool, m_pool_scale, m_w_out_even, m_w_in_odd, m_sgu_norm, m_w_spatial, m_b_spatial, m_w_out_odd, m_w_ffn_gate, m_w_ffn_up, m_conv_ffn, m_b_conv_ffn, m_w_ffn_down, v_norm_mix, v_norm_ffn, v_final_norm, v_w_in_even, v_conv_a, v_w_pool, v_pool_scale, v_w_out_even, v_w_in_odd, v_sgu_norm, v_w_spatial, v_b_spatial, v_w_out_odd, v_w_ffn_gate, v_w_ffn_up, v_conv_ffn, v_b_conv_ffn, v_w_ffn_down):
    s, d = x.shape[1], x.shape[2]
    x2, target = x[0], loss_target[0]
    tm = min(512, s)
    tm_wide = min(2048, s)
    tk_wide = 1024
    tn_fwd = 512
    row = lambda a: a.reshape(1, -1)
    ein, ro = w_in_even.shape[2], w_out_even.shape[1]
    fs = w_ffn_gate.shape[2]
    fsp = -(-fs // LANE) * LANE
    fp = N_DEV * fsp
    full = lambda shape, dtype=BF16: jax.ShapeDtypeStruct(shape, dtype)

    wg_s = [jnp.pad(w_ffn_gate[l].T, ((0, fsp - fs), (0, 0))).astype(BF16) for l in range(2)]
    wu_s = [jnp.pad(w_ffn_up[l].T, ((0, fsp - fs), (0, 0))).astype(BF16) for l in range(2)]
    wd_s = [jnp.pad(w_ffn_down[l], ((0, fsp - fs), (0, 0))).astype(BF16) for l in range(2)]
    small_s = _pack_small(conv_a[0], sgu_norm, conv_ffn, fsp)[None]
    in_spec, out_spec, gu_spec, down_spec = (1, ein), (0, ro), (0, fsp), (0, fsp)
    full_in, full_out, full_gu, full_down = full((d, N_DEV * ein)), full((N_DEV * ro, d)), full((fp, d)), full((fp, d))
    win_e, wout_e, gsmall = _alone(_Gather([w_in_even[0].astype(BF16), w_out_even[0].astype(BF16), small_s], [in_spec, out_spec, (0, 1)],
                                           [full_in, full_out, full((N_DEV, SMALL_ROWS, fsp), F32)]), "gather_mix0")
    ca_full = jnp.moveaxis(gsmall[:, 0:3, 0:conv_a.shape[2]], 0, 1).reshape(3, -1)
    sgu_full = gsmall[:, 8, 0:sgu_norm.shape[1]].reshape(1, -1)
    cf_full = jnp.moveaxis(gsmall[:, 16:22, :].reshape(N_DEV, 2, 3, fsp), 0, 2).reshape(2, 3, fp)
    cb_full = _pad_slots(b_conv_ffn, fs, fsp)

    tril = jnp.tril(jnp.ones((CHUNK, CHUNK), F32))
    ws_m = w_spatial[0] * tril
    ws_b = ws_m.astype(BF16)
    wst_b = jnp.swapaxes(ws_m, 1, 2).astype(BF16)
    bfull = jnp.repeat(b_spatial[0].T, CHUNK, axis=1)
    wpool_b = w_pool[0].astype(BF16)
    wpoolt_b = jnp.swapaxes(w_pool[0], 1, 2).astype(BF16)

    (xn0, proj0, cq0, pooled0, mix0, h1, hn0), (wg0,) = _even_fwd(
        x2, norm_mix[0:1], norm_ffn[0:1], win_e, ca_full, wpool_b, pool_scale, wout_e, tm,
        comm=_Gather([wg_s[0]], [gu_spec], [full_gu]))
    (g0, gc0), (wu0,) = _ffn_gate(hn0, wg0, cf_full[0], cb_full[0:1], tm_wide, tn_fwd, "ffn_gate_l0", comm=_Gather([wu_s[0]], [gu_spec], [full_gu]))
    (a0, sl0, ud0), (wd0,) = _ffn_up(hn0, wu0, gc0, tm_wide, tn_fwd, "ffn_up_l0", comm=_Gather([wd_s[0]], [down_spec], [full_down]))
    (h2, xn1), (win_o, wout_o) = _ffn_fwd2(a0, wd0, h1, norm_mix[1:2], tm, "ffn_fwd2_l0", comm=_Gather(
        [w_in_odd[0].astype(BF16), w_out_odd[0].astype(BF16)], [in_spec, out_spec], [full_in, full_out]))
    (pre1, gate1, mixo1, h3, hn1), (wg1,) = _odd_fwd(xn1, h2, win_o, sgu_full, ws_b, bfull, wout_o, norm_ffn[1:2], tm,
                                                    comm=_Gather([wg_s[1]], [gu_spec], [full_gu]))
    (g1, gc1), (wu1,) = _ffn_gate(hn1, wg1, cf_full[1], cb_full[1:2], tm_wide, tn_fwd, "ffn_gate_l1", comm=_Gather([wu_s[1]], [gu_spec], [full_gu]))
    (a1, sl1, ud1), (wd1,) = _ffn_up(hn1, wu1, gc1, tm_wide, tn_fwd, "ffn_up_l1", comm=_Gather([wd_s[1]], [down_spec], [full_down]))
    dh4, dh4b, d_final, lossvec = _ffn_fwd2_loss(a1, wd1, h3, row(final_norm), target, tm)

    core = lax.axis_index("c").astype(jnp.int32).reshape(1)
    ts = min(1024, s)
    gd1 = _wgrad(a1, dh4b, tk_wide, s, "wgrad_down_l1")
    (dg1, dup1, dcw1, dcb1), (o_d1,) = _ffn_bwd1(dh4b, g1, sl1, ud1, wd1, cf_full[1], tm_wide, tn_fwd, "ffn_bwd1_l1",
                                                 comm=_PairExchange([gd1], [down_spec]))
    p_d1 = _pair_add(gd1, o_d1, down_spec, core, "pair_add_down_l1")
    gg1 = _wgrad(dg1, hn1, tk_wide, s, "wgrad_gate_l1")
    gu1, (o_g1,) = _wgrad(dup1, hn1, tk_wide, s, "wgrad_up_l1", comm=_PairExchange([gg1], [gu_spec]))
    p_g1 = _pair_add(gg1, o_g1, gu_spec, core, "pair_add_gate_l1")
    jobs = _Jobs(_ChipExchange([p_d1]), _PairExchange([gu1], [gu_spec]))
    (dh3, dh3b, d_nffn1), res = _ffn_bwd2(dg1, dup1, wg1, wu1, h3, norm_ffn[1:2], dh4, tm, "ffn_bwd2_l1", comm=jobs)
    (s_d1,), (o_u1,) = jobs.results(res)
    p_u1 = _pair_add(gu1, o_u1, gu_spec, core, "pair_add_up_l1")
    (dpre1, dh2, dh2b, d_nmix1, d_sgu, d_ws, d_b), (s_g1, s_u1) = _odd_bwd(
        dh3b, dh3, wout_o, pre1, gate1, ws_b, wst_b, sgu_full, win_o, h2, norm_mix[1:2], tm, comm=_ChipExchange([p_g1, p_u1]))
    gi1 = _wgrad(xn1, dpre1, tk_wide, ts, "wgrad_in_odd")
    go1, (o_i1,) = _wgrad(mixo1, dh3b, tk_wide, ts, "wgrad_out_odd", comm=_PairExchange([gi1], [in_spec]))
    p_i1 = _pair_add(gi1, o_i1, in_spec, core, "pair_add_in_odd")
    d_wsp = (d_ws * tril).reshape(-1, CHUNK).astype(BF16)
    jobs = _Jobs(_PairExchange([go1], [out_spec]), _ChipExchange([], [], [d_wsp]))
    gd0, res = _wgrad(a0, dh2b, tk_wide, s, "wgrad_down_l0", comm=jobs)
    (o_o1,), (r_wsp,) = jobs.results(res)
    p_o1 = _pair_add(go1, o_o1, out_spec, core, "pair_add_out_odd")
    d_early = _rep_early(d_nmix1, d_nffn1, d_final, d_b, _unpad_slots(dcb1, fs, fsp), lossvec)
    jobs = _Jobs(_ChipExchange([p_i1, p_o1], [], [d_early]), _PairExchange([gd0], [down_spec]))
    (dg0, dup0, dcw0, dcb0), res = _ffn_bwd1(dh2b, g0, sl0, ud0, wd0, cf_full[0], tm_wide, tn_fwd, "ffn_bwd1_l0", comm=jobs)
    (s_i1, s_o1, r_early), (o_d0,) = jobs.results(res)
    p_d0 = _pair_add(gd0, o_d0, down_spec, core, "pair_add_down_l0")
    gg0, (s_d0,) = _wgrad(dg0, hn0, tk_wide, s, "wgrad_gate_l0", comm=_ChipExchange([p_d0]))
    gu0, (o_g0,) = _wgrad(dup0, hn0, tk_wide, s, "wgrad_up_l0", comm=_PairExchange([gg0], [gu_spec]))
    p_g0 = _pair_add(gg0, o_g0, gu_spec, core, "pair_add_gate_l0")
    jobs = _Jobs(_ChipExchange([p_g0]), _PairExchange([gu0], [gu_spec]))
    (dh1, dh1b, d_nffn0), res = _ffn_bwd2(dg0, dup0, wg0, wu0, h1, norm_ffn[0:1], dh2, tm, "ffn_bwd2_l0", comm=jobs)
    (s_g0,), (o_u0,) = jobs.results(res)
    p_u0 = _pair_add(gu0, o_u0, gu_spec, core, "pair_add_up_l0")
    go0 = _wgrad(mix0, dh1b, tk_wide, ts, "wgrad_out_even")
    jobs = _Jobs(_ChipExchange([p_u0]), _PairExchange([go0], [out_spec]))
    (dproj0, grad_x, d_nmix0, d_ca, d_wp, d_ps), res = _even_bwd(
        dh1b, dh1, wout_e, proj0, cq0, pooled0, ca_full, wpool_b, wpoolt_b, pool_scale, win_e, x2, norm_mix[0:1], tm, comm=jobs)
    (s_u0,), (o_o0,) = jobs.results(res)
    p_o0 = _pair_add(go0, o_o0, out_spec, core, "pair_add_out_even")
    d_small = jnp.stack([_pack_small(a, b, c, fsp) for a, b, c in zip(
        jnp.moveaxis(d_ca.reshape(3, N_DEV, -1), 1, 0), jnp.moveaxis(d_sgu.reshape(1, N_DEV, -1), 1, 0),
        jnp.moveaxis(jnp.stack([dcw0, dcw1]).reshape(2, 3, N_DEV, fsp), 2, 0))])
    d_late = _rep_late(d_nmix0, d_nffn0, d_ps, _unpad_slots(dcb0, fs, fsp))
    d_wpl = d_wp.reshape(-1, CHUNK).astype(BF16)
    gi0, (s_o0, r_small, r_late, r_wpl) = _wgrad(xn0, dproj0, tk_wide, ts, "wgrad_in_even",
                                                 comm=_ChipExchange([p_o0], [d_small], [d_late, d_wpl]))
    (o_i0,) = _alone(_PairExchange([gi0], [in_spec]), "pair_exchange_in_even")
    p_i0 = _pair_add(gi0, o_i0, in_spec, core, "pair_add_in_even")
    (s_i0,) = _alone(_ChipExchange([p_i0]), "chip_exchange_last")
    loss = jnp.sum(r_early[:, r_early.shape[1] - 8, :])

    out = {}
    out["w_in_even"], (grad_x,) = _adamw(s_i0, w_in_even[0], m_w_in_even[0], v_w_in_even[0], 256, "adamw_in_even", carried=[grad_x])
    out["w_out_even"] = _adamw(s_o0, w_out_even[0], m_w_out_even[0], v_w_out_even[0], ro // 2, "adamw_out_even")
    out["w_in_odd"] = _adamw(s_i1, w_in_odd[0], m_w_in_odd[0], v_w_in_odd[0], 256, "adamw_in_odd")
    out["w_out_odd"] = _adamw(s_o1, w_out_odd[0], m_w_out_odd[0], v_w_out_odd[0], ro // 2, "adamw_out_odd")
    tp = lambda a: jnp.swapaxes(a, 1, 2)
    for nm, s1, s0, w, m, v, back in (("w_ffn_gate", s_g1, s_g0, tp(w_ffn_gate), tp(m_w_ffn_gate), tp(v_w_ffn_gate), tp),
                                      ("w_ffn_up", s_u1, s_u0, tp(w_ffn_up), tp(m_w_ffn_up), tp(v_w_ffn_up), tp),
                                      ("w_ffn_down", s_d1, s_d0, w_ffn_down, m_w_ffn_down, v_w_ffn_down, lambda a: a)):
        l1 = _adamw(s1, w, m, v, fs // 2, "adamw_%s_l1" % nm, layer=1)
        out[nm] = [back(a) for a in _adamw(s0, w, m, v, fs // 2, "adamw_%s_l0" % nm, layer=0, into=l1)]
    small = _adamw(r_small, _pack_small(conv_a[0], sgu_norm, conv_ffn, fsp), _pack_small(m_conv_a[0], m_sgu_norm, m_conv_ffn, fsp),
                   _pack_small(v_conv_a[0], v_sgu_norm, v_conv_ffn, fsp), SMALL_ROWS, "adamw_small")
    no_loss = jnp.zeros((1, REP_COLS), F32)
    early = _adamw(r_early, *[_rep_early(nm[1:2], nf[1:2], fn, bs, bc[1:2], no_loss) for nm, nf, fn, bs, bc in (
        (norm_mix, norm_ffn, final_norm, b_spatial, b_conv_ffn), (m_norm_mix, m_norm_ffn, m_final_norm, m_b_spatial, m_b_conv_ffn),
        (v_norm_mix, v_norm_ffn, v_final_norm, v_b_spatial, v_b_conv_ffn))], r_early.shape[1], "adamw_replicated_early")
    wsp = _adamw(r_wsp, w_spatial.reshape(-1, CHUNK), m_w_spatial.reshape(-1, CHUNK), v_w_spatial.reshape(-1, CHUNK),
                 r_wsp.shape[1], "adamw_w_spatial")
    late = _adamw(r_late, *[_rep_late(nm[0:1], nf[0:1], ps, bc[0:1]) for nm, nf, ps, bc in (
        (norm_mix, norm_ffn, pool_scale, b_conv_ffn), (m_norm_mix, m_norm_ffn, m_pool_scale, m_b_conv_ffn),
        (v_norm_mix, v_norm_ffn, v_pool_scale, v_b_conv_ffn))], r_late.shape[1], "adamw_replicated_late")
    wpl = _adamw(r_wpl, w_pool.reshape(-1, CHUNK), m_w_pool.reshape(-1, CHUNK), v_w_pool.reshape(-1, CHUNK),
                 r_wpl.shape[1], "adamw_w_pool")

    names = ["norm_mix", "norm_ffn", "final_norm", "w_in_even", "conv_a", "w_pool", "pool_scale", "w_out_even", "w_in_odd", "sgu_norm",
             "w_spatial", "b_spatial", "w_out_odd", "w_ffn_gate", "w_ffn_up", "conv_ffn", "b_conv_ffn", "w_ffn_down"]
    like = {"norm_mix": norm_mix, "norm_ffn": norm_ffn, "final_norm": final_norm, "w_in_even": w_in_even, "conv_a": conv_a,
            "w_pool": w_pool, "pool_scale": pool_scale, "w_out_even": w_out_even, "w_in_odd": w_in_odd, "sgu_norm": sgu_norm,
            "w_spatial": w_spatial, "b_spatial": b_spatial, "w_out_odd": w_out_odd, "w_ffn_gate": w_ffn_gate, "w_ffn_up": w_ffn_up,
            "conv_ffn": conv_ffn, "b_conv_ffn": b_conv_ffn, "w_ffn_down": w_ffn_down}
    groups = []
    for k in range(4):
        ca_k, sg_k, cf_k = _unpack_small(small[k], conv_a.shape[2], sgu_norm.shape[1], conv_ffn.shape[2])
        vals = dict(_unpack_rep(late[k], early[k], wsp[k], wpl[k], like))
        vals.update(conv_a=ca_k, sgu_norm=sg_k, conv_ffn=cf_k)
        for nm in ("w_in_even", "w_in_odd", "w_out_even", "w_out_odd", "w_ffn_gate", "w_ffn_up", "w_ffn_down"):
            vals[nm] = out[nm][k]
        groups.append([vals[nm].reshape(like[nm].shape) for nm in names])
    return (loss, grad_x[None], *groups[0], *groups[1], *groups[2], *groups[3])
```

```python
import functools

import jax
import jax.numpy as jnp
from jax import lax
from jax.experimental import pallas as pl
from jax.experimental.pallas import tpu as pltpu

F32, BF16 = jnp.float32, jnp.bfloat16
EPS = 1e-6
WINDOWS = (2, 4, 8, 16)
HALO = 16
CHUNK = 128
N_DEV = 8
N_CHIP = 4
MESH = pl.DeviceIdType.MESH
VMEM_LIMIT = 56 * 2**20
LANE = 128
ADAM_LR, ADAM_B1, ADAM_B2, ADAM_EPS, ADAM_WD, ADAM_STEP = 0.001, 0.9, 0.999, 1e-08, 0.01, 10
SUB_ROWS = 256
LATE_NUM, LATE_DEN = 7, 8
INV_SQRT2 = 0.7071067811865476
INV_SQRT2PI = 0.3989422804014327


def _pcall(body, comm=None, **kw):
    if comm is None:
        return pl.pallas_call(body, **kw)
    in_specs, out_specs, out_shape = list(kw.pop("in_specs")), kw.pop("out_specs"), kw.pop("out_shape")
    single = not isinstance(out_shape, (list, tuple))
    out_specs, out_shape = ([out_specs], [out_shape]) if single else (list(out_specs), list(out_shape))
    scratch = list(kw.pop("scratch_shapes", []))
    grid = kw.get("grid", ())
    n_in, n_out, n_scr, c_in, c_out = len(in_specs), len(out_specs), len(scratch), len(comm.ins), len(comm.out_shape)

    def hosted(*refs):
        cuts = [0, n_in, n_in + c_in, n_in + c_in + n_out, n_in + c_in + n_out + c_out, n_in + c_in + n_out + c_out + n_scr, len(refs)]
        ins, cins, outs, couts, scr, sems = (refs[a:b] for a, b in zip(cuts[:-1], cuts[1:]))
        if grid:
            step, steps = 0, 1
            for axis, size in enumerate(grid):
                step, steps = step * size + pl.program_id(axis), steps * size
            pl.when(step == 0)(lambda: comm.start(cins, couts, sems))
            pl.when(step == (steps * LATE_NUM) // LATE_DEN)(lambda: comm.middle(cins, couts, sems))
            body(*ins, *outs, *scr)
            pl.when(step == steps - 1)(lambda: comm.finish(cins, couts, sems))
        else:
            comm.start(cins, couts, sems)
            comm.middle(cins, couts, sems)
            body(*ins, *outs, *scr)
            comm.finish(cins, couts, sems)

    call = pl.pallas_call(hosted, in_specs=in_specs + [_hbm()] * c_in, out_specs=out_specs + [_hbm()] * c_out,
                          out_shape=out_shape + list(comm.out_shape), scratch_shapes=scratch + list(comm.sems), **kw)

    def run(*args):
        res = call(*args, *comm.ins)
        own = res[0] if single else res[:n_out]
        return own, res[n_out:]

    return run


def _params(*sem):
    return pltpu.CompilerParams(dimension_semantics=sem, vmem_limit_bytes=VMEM_LIMIT)


def _whole(shape):
    return pl.BlockSpec(shape, lambda *_: (0,) * len(shape))


def _resident(shape):
    return pl.BlockSpec(shape, lambda *_: (0,) * len(shape), pipeline_mode=pl.Buffered(1))


def _rows(i, tm):
    return pl.ds(pl.multiple_of(i * tm, tm), tm)


def _nn(a, b):
    return jnp.dot(a, b, preferred_element_type=F32)


def _nt(a, b):
    return lax.dot_general(a, b, (((1,), (1,)), ((), ())), preferred_element_type=F32)


def _tn(a, b):
    return lax.dot_general(a, b, (((0,), (0,)), ((), ())), preferred_element_type=F32)


def _rms(x, gain):
    r = lax.rsqrt(jnp.mean(x * x, axis=-1, keepdims=True) + EPS)
    return x * r * gain


def _rms_bwd(dy, x, gain):
    r = lax.rsqrt(jnp.mean(x * x, axis=-1, keepdims=True) + EPS)
    xh = x * r
    dgain = jnp.sum(dy * xh, axis=0, keepdims=True)
    dxh = dy * gain
    dx = r * (dxh - xh * jnp.mean(dxh * xh, axis=-1, keepdims=True))
    return dx, dgain


def _gelu(x):
    return 0.5 * x * (1.0 + lax.erf(x * INV_SQRT2))


def _gelu_grad(x):
    return 0.5 * (1.0 + lax.erf(x * INV_SQRT2)) + x * jnp.exp(-0.5 * x * x) * INV_SQRT2PI


def _acc(ref, val, first):
    @pl.when(first)
    def _():
        ref[...] = val

    @pl.when(jnp.logical_not(first))
    def _():
        ref[...] += val


def _shift(ext, k, back):
    n = ext.shape[0]
    return pltpu.roll(ext, k if back else n - k, axis=0)


def _window_sum(ext, w, back):
    total, step = ext, 1
    while step < w:
        total = total + _shift(total, step, back)
        step *= 2
    return total


def _counts(row0, tm, w):
    pos1 = (row0 + lax.broadcasted_iota(jnp.int32, (tm, 1), 0) + 1).astype(F32)
    return jnp.minimum(pos1, float(w))


def _even_fwd(x, gmix, gffn, win, conva, wpool, pscale, wout, tm, comm=None):
    s, d = x.shape
    e = win.shape[1]
    aw = e // 4

    def body(x_ref, gmix_ref, gffn_ref, win_ref, ca_ref, wp_ref, ps_ref, wout_ref,
             xn_ref, proj_ref, cq_ref, pooled_ref, mix_ref, h_ref, hn_ref, qbuf, zbuf):
        i = pl.program_id(0)

        @pl.when(i == 0)
        def _():
            qbuf[0:HALO, :] = jnp.zeros((HALO, aw), F32)
            zbuf[0:HALO, :] = jnp.zeros((HALO, aw), F32)

        xv = x_ref[...]
        xn = _rms(xv, gmix_ref[...]).astype(BF16)
        xn_ref[...] = xn
        proj = _nn(xn, win_ref[...])
        proj_ref[...] = proj.astype(BF16)
        a_b, a_c, a_v, z = (proj[:, k * aw:(k + 1) * aw] for k in range(4))
        q = a_c * a_v
        qbuf[HALO:HALO + tm, :] = q
        qext = qbuf[...]
        cur = slice(HALO, HALO + tm)
        cq = ca_ref[2:3, :] * q + ca_ref[1:2, :] * _shift(qext, 1, True)[cur, :] + ca_ref[0:1, :] * _shift(qext, 2, True)[cur, :]
        cq_ref[...] = cq.astype(BF16)
        y_a = a_b * cq
        zbuf[HALO:HALO + tm, :] = z
        zext = zbuf[...]
        ys = []
        for g, w in enumerate(WINDOWS):
            cols = slice(g * LANE, (g + 1) * LANE)
            acc = _window_sum(zext[:, cols], w, True)[cur, :]
            pooled = (acc / _counts(i * tm, tm, w) - z[:, cols]).astype(BF16)
            pooled_ref[:, cols] = pooled
            ys.append(_nn(pooled, wp_ref[g]))
        y_b = jnp.concatenate(ys, axis=1) * ps_ref[...]
        mix = jnp.concatenate([y_a, y_b], axis=1).astype(BF16)
        mix_ref[...] = mix
        h = xv + _nn(mix, wout_ref[...])
        h_ref[...] = h
        hn_ref[...] = _rms(h, gffn_ref[...]).astype(BF16)
        qbuf[0:HALO, :] = qbuf[tm:tm + HALO, :]
        zbuf[0:HALO, :] = zbuf[tm:tm + HALO, :]

    row = lambda c: pl.BlockSpec((tm, c), lambda i: (i, 0))
    return _pcall(
        body, comm=comm, name="even_fwd", grid=(s // tm,),
        in_specs=[row(d), _whole((1, d)), _whole((1, d)), _whole(win.shape), _whole(conva.shape), _whole(wpool.shape),
                  _whole(pscale.shape), _whole(wout.shape)],
        out_specs=[row(d), row(e), row(aw), row(aw), row(d), row(d), row(d)],
        out_shape=[jax.ShapeDtypeStruct((s, d), BF16), jax.ShapeDtypeStruct((s, e), BF16), jax.ShapeDtypeStruct((s, aw), BF16),
                   jax.ShapeDtypeStruct((s, aw), BF16), jax.ShapeDtypeStruct((s, d), BF16), jax.ShapeDtypeStruct((s, d), F32),
                   jax.ShapeDtypeStruct((s, d), BF16)],
        scratch_shapes=[pltpu.VMEM((tm + HALO, aw), F32), pltpu.VMEM((tm + HALO, aw), F32)],
        compiler_params=_params("arbitrary"),
    )(x, gmix, gffn, win, conva, wpool, pscale, wout)


def _ffn_gate(hn, wgt, cw, cb, tm, tn, name, comm=None):
    s, d = hn.shape
    f = wgt.shape[0]
    sub = tm

    def body(hn_ref, wg_ref, cw_ref, cb_ref, g_ref, gc_ref, gbuf):
        i = pl.program_id(1)

        @pl.when(i == 0)
        def _():
            gbuf[0:HALO, :] = jnp.zeros((HALO, tn), F32)

        wg = wg_ref[...]
        for c in range(tm // sub):
            rows = pl.ds(c * sub, sub)
            g = _nt(hn_ref[pl.ds(pl.multiple_of(i * tm + c * sub, sub), sub), :], wg)
            g_ref[rows, :] = g.astype(BF16)
            gbuf[pl.ds(HALO + c * sub, sub), :] = g
            ext = gbuf[pl.ds(c * sub, sub + HALO), :]
            gc = (cw_ref[2:3, :] * g + cw_ref[1:2, :] * _shift(ext, 1, True)[HALO:, :]
                  + cw_ref[0:1, :] * _shift(ext, 2, True)[HALO:, :] + cb_ref[...])
            gc_ref[rows, :] = gc.astype(BF16)
        gbuf[0:HALO, :] = gbuf[tm:tm + HALO, :]

    tile = pl.BlockSpec((tm, tn), lambda j, i: (i, j))
    wcol = lambda r: pl.BlockSpec((r, tn), lambda j, i: (0, j))
    out = jax.ShapeDtypeStruct((s, f), BF16)
    return _pcall(
        body, comm=comm, name=name, grid=(f // tn, s // tm),
        in_specs=[_resident((s, d)), pl.BlockSpec((tn, d), lambda j, i: (j, 0)), wcol(3), wcol(1)],
        out_specs=[tile, tile], out_shape=[out, out],
        scratch_shapes=[pltpu.VMEM((tm + HALO, tn), F32)],
        compiler_params=_params("arbitrary", "arbitrary"),
    )(hn, wgt, cw, cb)


def _ffn_up(hn, wut, gc, tm, tn, name, comm=None):
    s, d = hn.shape
    f = wut.shape[0]
    sub = min(SUB_ROWS, tm)

    def body(hn_ref, wu_ref, gc_ref, a_ref, silu_ref, upds_ref):
        wu = wu_ref[...]
        for c in range(tm // sub):
            rows = pl.ds(c * sub, sub)
            up = _nt(hn_ref[pl.ds(pl.multiple_of(pl.program_id(1) * tm + c * sub, sub), sub), :], wu)
            gc = gc_ref[rows, :].astype(F32)
            sg = jax.nn.sigmoid(gc)
            silu = gc * sg
            silu_ref[rows, :] = silu.astype(BF16)
            upds_ref[rows, :] = (up * (sg * (1.0 - silu) + silu)).astype(BF16)
            a_ref[rows, :] = (silu * up).astype(BF16)

    tile = pl.BlockSpec((tm, tn), lambda j, i: (i, j))
    out = jax.ShapeDtypeStruct((s, f), BF16)
    return _pcall(
        body, comm=comm, name=name, grid=(f // tn, s // tm),
        in_specs=[_resident((s, d)), pl.BlockSpec((tn, d), lambda j, i: (j, 0)), tile],
        out_specs=[tile, tile, tile], out_shape=[out, out, out],
        compiler_params=_params("arbitrary", "arbitrary"),
    )(hn, wut, gc)


def _ffn_gate_up(hn, wgt, wut, cw, cb, tm, tn, name, comm=None):
    s, d = hn.shape
    f = wgt.shape[0]
    sub = min(SUB_ROWS, tm)

    def body(hn_ref, wg_ref, wu_ref, cw_ref, cb_ref, g_ref, a_ref, silu_ref, upds_ref, gbuf):
        i = pl.program_id(1)

        @pl.when(i == 0)
        def _():
            gbuf[0:HALO, :] = jnp.zeros((HALO, tn), F32)

        g = _nt(hn_ref[_rows(i, tm), :], wg_ref[...])
        g_ref[...] = g.astype(BF16)
        gbuf[HALO:HALO + tm, :] = g
        ext = gbuf[...]
        gc = (cw_ref[2:3, :] * g + cw_ref[1:2, :] * _shift(ext, 1, True)[HALO:, :]
              + cw_ref[0:1, :] * _shift(ext, 2, True)[HALO:, :] + cb_ref[...])
        gbuf[0:HALO, :] = gbuf[tm:tm + HALO, :]
        wu = wu_ref[...]
        for c in range(tm // sub):
            rows = pl.ds(c * sub, sub)
            up = _nt(hn_ref[pl.ds(pl.multiple_of(i * tm + c * sub, sub), sub), :], wu)
            gcc = gc[c * sub:(c + 1) * sub, :]
            sg = jax.nn.sigmoid(gcc)
            silu = gcc * sg
            silu_ref[rows, :] = silu.astype(BF16)
            upds_ref[rows, :] = (up * (sg * (1.0 - silu) + silu)).astype(BF16)
            a_ref[rows, :] = (silu * up).astype(BF16)

    tile = pl.BlockSpec((tm, tn), lambda j, i: (i, j))
    wcol = lambda r: pl.BlockSpec((r, tn), lambda j, i: (0, j))
    wrow = pl.BlockSpec((tn, d), lambda j, i: (j, 0))
    out = jax.ShapeDtypeStruct((s, f), BF16)
    return _pcall(
        body, comm=comm, name=name, grid=(f // tn, s // tm),
        in_specs=[_resident((s, d)), wrow, wrow, wcol(3), wcol(1)],
        out_specs=[tile, tile, tile, tile], out_shape=[out, out, out, out],
        scratch_shapes=[pltpu.VMEM((tm + HALO, tn), F32)],
        compiler_params=_params("arbitrary", "arbitrary"),
    )(hn, wgt, wut, cw, cb)


def _ffn_fwd2(a, wd, h, gain, tm, name, comm=None):
    s, d = h.shape
    f = a.shape[1]

    def body(a_ref, wd_ref, h_ref, gain_ref, ho_ref, hn_ref):
        wd_v, gain = wd_ref[...], gain_ref[...]
        sub = min(SUB_ROWS, tm)
        for c in range(tm // sub):
            rows = pl.ds(c * sub, sub)
            ho = h_ref[rows, :] + _nn(a_ref[rows, :], wd_v)
            ho_ref[rows, :] = ho
            hn_ref[rows, :] = _rms(ho, gain).astype(BF16)

    row = lambda c: pl.BlockSpec((tm, c), lambda i: (i, 0))
    return _pcall(
        body, comm=comm, name=name, grid=(s // tm,),
        in_specs=[row(f), _resident(wd.shape), row(d), _whole((1, d))],
        out_specs=[row(d), row(d)],
        out_shape=[jax.ShapeDtypeStruct((s, d), F32), jax.ShapeDtypeStruct((s, d), BF16)],
        compiler_params=_params("arbitrary"),
    )(a, wd, h, gain)


def _ffn_fwd2_loss(a, wd, h, gain, target, tm):
    s, d = h.shape
    f = a.shape[1]

    def body(a_ref, wd_ref, h_ref, gain_ref, t_ref, dh_ref, dhb_ref, dgain_ref, loss_ref):
        wd_v, gain = wd_ref[...], gain_ref[...]
        sub = min(SUB_ROWS, tm)
        dgain = loss = None
        for c in range(tm // sub):
            rows = pl.ds(c * sub, sub)
            ho = h_ref[rows, :] + _nn(a_ref[rows, :], wd_v)
            err = _rms(ho, gain) - t_ref[rows, :]
            dx, part = _rms_bwd(err * (1.0 / d), ho, gain)
            dh_ref[rows, :] = dx
            dhb_ref[rows, :] = dx.astype(BF16)
            sq = jnp.sum(err * err, axis=0, keepdims=True) * (0.5 / d)
            dgain, loss = (part, sq) if dgain is None else (dgain + part, loss + sq)
        _acc(dgain_ref, dgain, pl.program_id(0) == 0)
        _acc(loss_ref, loss, pl.program_id(0) == 0)

    row = lambda c: pl.BlockSpec((tm, c), lambda i: (i, 0))
    return _pcall(
        body, name="ffn_fwd2_loss", grid=(s // tm,),
        in_specs=[row(f), _resident(wd.shape), row(d), _whole((1, d)), row(d)],
        out_specs=[row(d), row(d), _whole((1, d)), _whole((1, d))],
        out_shape=[jax.ShapeDtypeStruct((s, d), F32), jax.ShapeDtypeStruct((s, d), BF16), jax.ShapeDtypeStruct((1, d), F32),
                   jax.ShapeDtypeStruct((1, d), F32)],
        compiler_params=_params("arbitrary"),
    )(a, wd, h, gain, target)


def _odd_fwd(xn, h, win, sgu, ws, bfull, wout, gffn, tm, comm=None):
    s, d = h.shape
    e = win.shape[1]
    cw = e // 2
    heads = ws.shape[0]

    def body(xn_ref, h_ref, win_ref, sgu_ref, ws_ref, b_ref, wout_ref, gffn_ref,
             pre_ref, gate_ref, mixo_ref, ho_ref, hn_ref, gbuf):
        pre = _nn(xn_ref[...], win_ref[...])
        pre_ref[...] = pre.astype(BF16)
        p = _gelu(pre)
        u, v = p[:, :cw], p[:, cw:]
        vn = _rms(v, sgu_ref[...]).astype(BF16)
        for n in range(tm // CHUNK):
            rows = slice(n * CHUNK, (n + 1) * CHUNK)
            for hd in range(heads):
                cols = slice(hd * CHUNK, (hd + 1) * CHUNK)
                gbuf[rows, cols] = _nn(ws_ref[hd], vn[rows, cols]) + b_ref[:, cols]
        gate = gbuf[...]
        gate_ref[...] = gate.astype(BF16)
        mixo = (u * gate).astype(BF16)
        mixo_ref[...] = mixo
        ho = h_ref[...] + _nn(mixo, wout_ref[...])
        ho_ref[...] = ho
        hn_ref[...] = _rms(ho, gffn_ref[...]).astype(BF16)

    row = lambda c: pl.BlockSpec((tm, c), lambda i: (i, 0))
    return _pcall(
        body, comm=comm, name="odd_fwd", grid=(s // tm,),
        in_specs=[row(d), row(d), _whole(win.shape), _whole(sgu.shape), _whole(ws.shape), _whole(bfull.shape),
                  _whole(wout.shape), _whole((1, d))],
        out_specs=[row(e), row(cw), row(cw), row(d), row(d)],
        out_shape=[jax.ShapeDtypeStruct((s, e), BF16), jax.ShapeDtypeStruct((s, cw), BF16), jax.ShapeDtypeStruct((s, cw), BF16),
                   jax.ShapeDtypeStruct((s, d), F32), jax.ShapeDtypeStruct((s, d), BF16)],
        scratch_shapes=[pltpu.VMEM((tm, cw), F32)],
        compiler_params=_params("arbitrary"),
    )(xn, h, win, sgu, ws, bfull, wout, gffn)


def _ffn_bwd1(dhb, g, silu, upds, wd, cw, tm, tn, name, comm=None):
    s, d = dhb.shape
    f = g.shape[1]
    ni = s // tm

    def body(dh_ref, g_ref, silu_ref, upds_ref, wd_ref, cw_ref, dg_ref, dup_ref, dcw_ref, dcb_ref, ebuf):
        i = pl.program_id(1)

        @pl.when(i == 0)
        def _():
            ebuf[tm:tm + HALO, :] = jnp.zeros((HALO, tn), F32)

        wd_v = wd_ref[...]
        sub = min(SUB_ROWS, tm)
        sums = [None] * 4
        for c in reversed(range(tm // sub)):
            rows = pl.ds(c * sub, sub)
            da = _nt(dh_ref[pl.ds(pl.multiple_of((ni - 1 - i) * tm + c * sub, sub), sub), :], wd_v)
            dup_ref[rows, :] = (da * silu_ref[rows, :].astype(F32)).astype(BF16)
            dgc = da * upds_ref[rows, :].astype(F32)
            ebuf[rows, :] = dgc
            ext = ebuf[pl.ds(c * sub, sub + HALO), :]
            s1 = _shift(ext, 1, False)[0:sub, :]
            s2 = _shift(ext, 2, False)[0:sub, :]
            dg_ref[rows, :] = (cw_ref[2:3, :] * dgc + cw_ref[1:2, :] * s1 + cw_ref[0:1, :] * s2).astype(BF16)
            gv = g_ref[rows, :].astype(F32)
            for k, term in enumerate((s2 * gv, s1 * gv, dgc * gv, dgc)):
                part = jnp.sum(term, axis=0, keepdims=True)
                sums[k] = part if sums[k] is None else sums[k] + part
        for k in range(3):
            _acc(dcw_ref.at[k:k + 1, :], sums[k], i == 0)
        _acc(dcb_ref, sums[3], i == 0)
        ebuf[tm:tm + HALO, :] = ebuf[0:HALO, :]

    tile = pl.BlockSpec((tm, tn), lambda j, i: (ni - 1 - i, j))
    wcol = lambda r: pl.BlockSpec((r, tn), lambda j, i: (0, j))
    out = jax.ShapeDtypeStruct((s, f), BF16)
    return _pcall(
        body, comm=comm, name=name, grid=(f // tn, ni),
        in_specs=[_resident((s, d)), tile, tile, tile,
                  pl.BlockSpec((tn, d), lambda j, i: (j, 0)), wcol(3)],
        out_specs=[tile, tile, wcol(3), wcol(1)],
        out_shape=[out, out, jax.ShapeDtypeStruct((3, f), F32), jax.ShapeDtypeStruct((1, f), F32)],
        scratch_shapes=[pltpu.VMEM((tm + HALO, tn), F32)],
        compiler_params=_params("arbitrary", "arbitrary"),
    )(dhb, g, silu, upds, wd, cw)


def _ffn_bwd2(dg, dup, wg, wu, h, gain, dh, tm, name, comm=None):
    s, d = h.shape
    f = dg.shape[1]

    def body(dg_ref, dup_ref, wg_ref, wu_ref, h_ref, gain_ref, dh_ref, dho_ref, dhb_ref, dgain_ref):
        wg_v, wu_v, gain = wg_ref[...], wu_ref[...], gain_ref[...]
        sub = min(SUB_ROWS, tm)
        dgain = None
        for c in range(tm // sub):
            rows = pl.ds(c * sub, sub)
            dhn = _nn(dg_ref[rows, :], wg_v) + _nn(dup_ref[rows, :], wu_v)
            dx, part = _rms_bwd(dhn, h_ref[rows, :], gain)
            dgain = part if dgain is None else dgain + part
            dho = dh_ref[rows, :] + dx
            dho_ref[rows, :] = dho
            dhb_ref[rows, :] = dho.astype(BF16)
        _acc(dgain_ref, dgain, pl.program_id(0) == 0)

    row = lambda c: pl.BlockSpec((tm, c), lambda i: (i, 0))
    return _pcall(
        body, comm=comm, name=name, grid=(s // tm,),
        in_specs=[row(f), row(f), _resident(wg.shape), _resident(wu.shape), row(d), _whole((1, d)), row(d)],
        out_specs=[row(d), row(d), _whole((1, d))],
        out_shape=[jax.ShapeDtypeStruct((s, d), F32), jax.ShapeDtypeStruct((s, d), BF16), jax.ShapeDtypeStruct((1, d), F32)],
        compiler_params=_params("arbitrary"),
    )(dg, dup, wg, wu, h, gain, dh)


def _odd_bwd(dhb, dh, wout, pre, gate, ws, wst, sgu, win, h, gmix, tm, comm=None):
    s, d = h.shape
    e = win.shape[1]
    cw = e // 2
    heads = ws.shape[0]
    ni = s // tm

    def body(dhb_ref, dh_ref, wout_ref, pre_ref, gate_ref, ws_ref, wst_ref, sgu_ref, win_ref, h_ref, gmix_ref,
             dpre_ref, dho_ref, dhob_ref, dgain_ref, dsgu_ref, dws_ref, db_ref, vbuf, gacc):
        i = pl.program_id(0)
        first = i == 0
        dmixo = _nt(dhb_ref[...], wout_ref[...])
        pre = pre_ref[...].astype(F32)
        p = _gelu(pre)
        u, v = p[:, :cw], p[:, cw:]
        sgu = sgu_ref[...]
        rv = lax.rsqrt(jnp.mean(v * v, axis=-1, keepdims=True) + EPS)
        vh = v * rv
        vn = (vh * sgu).astype(BF16)
        du = dmixo * gate_ref[...].astype(F32)
        dgate = dmixo * u
        dgate_b = dgate.astype(BF16)
        gsum = dgate[0:CHUNK, :]
        for n in range(1, tm // CHUNK):
            gsum = gsum + dgate[n * CHUNK:(n + 1) * CHUNK, :]
        _acc(gacc, gsum, first)
        for hd in range(heads):
            cols = slice(hd * CHUNK, (hd + 1) * CHUNK)
            dws = None
            for n in range(tm // CHUNK):
                rows = slice(n * CHUNK, (n + 1) * CHUNK)
                vbuf[rows, cols] = _nn(wst_ref[hd], dgate_b[rows, cols])
                part = _nt(dgate_b[rows, cols], vn[rows, cols])
                dws = part if dws is None else dws + part
            _acc(dws_ref.at[hd], dws, first)
        dvn = vbuf[...]
        _acc(dsgu_ref, jnp.sum(dvn * vh, axis=0, keepdims=True), first)
        dvh = dvn * sgu
        dv = rv * (dvh - vh * jnp.mean(dvh * vh, axis=-1, keepdims=True))
        dpre = (jnp.concatenate([du, dv], axis=1) * _gelu_grad(pre)).astype(BF16)
        dpre_ref[...] = dpre
        dx, dgain = _rms_bwd(_nt(dpre, win_ref[...]), h_ref[...], gmix_ref[...])
        dho = dh_ref[...] + dx
        dho_ref[...] = dho
        dhob_ref[...] = dho.astype(BF16)
        _acc(dgain_ref, dgain, first)

        @pl.when(i == ni - 1)
        def _():
            ones = jnp.ones((8, CHUNK), F32)
            for hd in range(heads):
                tot = lax.dot_general(ones, gacc[:, hd * CHUNK:(hd + 1) * CHUNK], (((1,), (1,)), ((), ())),
                                      preferred_element_type=F32, precision=lax.Precision.HIGHEST)
                db_ref[hd:hd + 1, :] = tot[0:1, :]

    row = lambda c: pl.BlockSpec((tm, c), lambda i: (i, 0))
    return _pcall(
        body, comm=comm, name="odd_bwd", grid=(ni,),
        in_specs=[row(d), row(d), _whole(wout.shape), row(e), row(cw), _whole(ws.shape), _whole(wst.shape), _whole(sgu.shape),
                  _whole(win.shape), row(d), _whole((1, d))],
        out_specs=[row(e), row(d), row(d), _whole((1, d)), _whole((1, cw)), _whole(ws.shape), _whole((heads, CHUNK))],
        out_shape=[jax.ShapeDtypeStruct((s, e), BF16), jax.ShapeDtypeStruct((s, d), F32), jax.ShapeDtypeStruct((s, d), BF16),
                   jax.ShapeDtypeStruct((1, d), F32), jax.ShapeDtypeStruct((1, cw), F32), jax.ShapeDtypeStruct(ws.shape, F32),
                   jax.ShapeDtypeStruct((heads, CHUNK), F32)],
        scratch_shapes=[pltpu.VMEM((tm, cw), F32), pltpu.VMEM((CHUNK, cw), F32)],
        compiler_params=_params("arbitrary"),
    )(dhb, dh, wout, pre, gate, ws, wst, sgu, win, h, gmix)


def _even_bwd(dhb, dh, wout, proj, cq, pooled, conva, wpool, wpoolt, pscale, win, x, gmix, tm, comm=None):
    s, d = x.shape
    e = win.shape[1]
    aw = e // 4
    ni = s // tm

    def body(dhb_ref, dh_ref, wout_ref, proj_ref, cq_ref, pooled_ref, ca_ref, wp_ref, wpt_ref, ps_ref, win_ref, x_ref, gmix_ref,
             dproj_ref, dx_ref, dgain_ref, dca_ref, dwp_ref, dps_ref, cbuf, ebuf):
        i = pl.program_id(0)
        first = i == 0

        @pl.when(first)
        def _():
            cbuf[tm:tm + HALO, :] = jnp.zeros((HALO, aw), F32)
            ebuf[tm:tm + HALO, :] = jnp.zeros((HALO, aw), F32)

        dmix = _nt(dhb_ref[...], wout_ref[...])
        dy_a, dy_b = dmix[:, :aw], dmix[:, aw:]
        proj = proj_ref[...].astype(F32)
        a_b, a_c, a_v = (proj[:, k * aw:(k + 1) * aw] for k in range(3))
        da_b = dy_a * cq_ref[...].astype(F32)
        dcq = dy_a * a_b
        cbuf[0:tm, :] = dcq
        cext = cbuf[...]
        s1 = _shift(cext, 1, False)[0:tm, :]
        s2 = _shift(cext, 2, False)[0:tm, :]
        q = a_c * a_v
        for k, shifted in enumerate((s2, s1, dcq)):
            _acc(dca_ref.at[k:k + 1, :], jnp.sum(shifted * q, axis=0, keepdims=True), first)
        dq = ca_ref[2:3, :] * dcq + ca_ref[1:2, :] * s1 + ca_ref[0:1, :] * s2
        da_c = dq * a_v
        da_v = dq * a_c
        dps, dpool = [], []
        for g, w in enumerate(WINDOWS):
            cols = slice(g * LANE, (g + 1) * LANE)
            pooled = pooled_ref[:, cols]
            mixed = _nn(pooled, wp_ref[g])
            dps.append(jnp.sum(dy_b[:, cols] * mixed, axis=0, keepdims=True))
            dmixed = (dy_b[:, cols] * ps_ref[:, cols]).astype(BF16)
            _acc(dwp_ref.at[g], _tn(pooled, dmixed), first)
            dp = _nn(dmixed, wpt_ref[g])
            dpool.append(dp)
            ebuf[0:tm, cols] = dp / _counts((ni - 1 - i) * tm, tm, w)
        _acc(dps_ref, jnp.concatenate(dps, axis=1), first)
        eext = ebuf[...]
        dzs = []
        for g, w in enumerate(WINDOWS):
            cols = slice(g * LANE, (g + 1) * LANE)
            dzs.append(_window_sum(eext[:, cols], w, False)[0:tm, :] - dpool[g])
        dproj = jnp.concatenate([da_b, da_c, da_v] + dzs, axis=1).astype(BF16)
        dproj_ref[...] = dproj
        dx, dgain = _rms_bwd(_nt(dproj, win_ref[...]), x_ref[...], gmix_ref[...])
        dx_ref[...] = dh_ref[...] + dx
        _acc(dgain_ref, dgain, first)
        cbuf[tm:tm + HALO, :] = cbuf[0:HALO, :]
        ebuf[tm:tm + HALO, :] = ebuf[0:HALO, :]

    row = lambda c: pl.BlockSpec((tm, c), lambda i: (ni - 1 - i, 0))
    return _pcall(
        body, comm=comm, name="even_bwd", grid=(ni,),
        in_specs=[row(d), row(d), _whole(wout.shape), row(e), row(aw), row(aw), _whole(conva.shape), _whole(wpool.shape),
                  _whole(wpoolt.shape), _whole(pscale.shape), _whole(win.shape), row(d), _whole((1, d))],
        out_specs=[row(e), row(d), _whole((1, d)), _whole(conva.shape), _whole(wpool.shape), _whole(pscale.shape)],
        out_shape=[jax.ShapeDtypeStruct((s, e), BF16), jax.ShapeDtypeStruct((s, d), F32), jax.ShapeDtypeStruct((1, d), F32),
                   jax.ShapeDtypeStruct(conva.shape, F32), jax.ShapeDtypeStruct(wpool.shape, F32),
                   jax.ShapeDtypeStruct(pscale.shape, F32)],
        scratch_shapes=[pltpu.VMEM((tm + HALO, aw), F32), pltpu.VMEM((tm + HALO, aw), F32)],
        compiler_params=_params("arbitrary"),
    )(dhb, dh, wout, proj, cq, pooled, conva, wpool, wpoolt, pscale, win, x, gmix)


def _wgrad(a, b, tk, ts, name, comm=None):
    s, ka = a.shape
    nb = b.shape[1]
    nt = s // ts

    def body(a_ref, b_ref, o_ref, acc):
        t = pl.program_id(1)
        _acc(acc, _tn(a_ref[...], b_ref[...]), t == 0)

        @pl.when(t == nt - 1)
        def _():
            o_ref[...] = acc[...].astype(BF16)

    return _pcall(
        body, comm=comm, name=name, grid=(ka // tk, nt),
        in_specs=[pl.BlockSpec((ts, tk), lambda k, t: (t, k)),
                  _resident((s, nb)) if nt == 1 else pl.BlockSpec((ts, nb), lambda k, t: (t, 0))],
        out_specs=pl.BlockSpec((tk, nb), lambda k, t: (k, 0)),
        out_shape=jax.ShapeDtypeStruct((ka, nb), BF16),
        scratch_shapes=[pltpu.VMEM((tk, nb), F32)],
        compiler_params=_params("arbitrary", "arbitrary"),
    )(a, b)


def _adamw(parts, w, m, v, tr, name, layer=None, into=None, carried=()):
    r, c = w.shape[-2:]
    n, rp, cp = parts.shape
    assert rp >= r and r % tr == 0
    prev, carried = ([] if into is None else list(into)), list(carried)

    def body(p_ref, w_ref, m_ref, v_ref, *rest):
        g_ref, d_ref, mo_ref, vo_ref = rest[len(prev) + len(carried):len(prev) + len(carried) + 4]
        g = p_ref[0, 0:tr, 0:c].astype(F32)
        for j in range(1, n):
            g = g + p_ref[j, 0:tr, 0:c].astype(F32)
        g_ref[...] = g
        mn = ADAM_B1 * m_ref[...] + (1.0 - ADAM_B1) * g
        vn = ADAM_B2 * v_ref[...] + (1.0 - ADAM_B2) * (g * g)
        mo_ref[...] = mn
        vo_ref[...] = vn
        m_hat = mn / (1.0 - ADAM_B1 ** ADAM_STEP)
        v_hat = vn / (1.0 - ADAM_B2 ** ADAM_STEP)
        d_ref[...] = -ADAM_LR * (m_hat / (jnp.sqrt(v_hat) + ADAM_EPS) + ADAM_WD * w_ref[...])

    if layer is None:
        row = pl.BlockSpec((tr, c), lambda i: (i, 0))
    else:
        row = pl.BlockSpec((None, tr, c), lambda i: (layer, i, 0))
    out = jax.ShapeDtypeStruct(w.shape, F32)
    untouched = pl.BlockSpec(memory_space=pl.ANY)
    aliases = {4 + k: k for k in range(len(prev))}
    aliases.update({4 + len(prev) + k: 4 + k for k in range(len(carried))})
    res = _pcall(
        body, name=name, grid=(r // tr,),
        in_specs=[pl.BlockSpec((n, tr, cp), lambda i: (0, i, 0)), row, row, row] + [untouched] * (len(prev) + len(carried)),
        out_specs=[row, row, row, row] + [untouched] * len(carried),
        out_shape=[out, out, out, out] + [jax.ShapeDtypeStruct(a.shape, a.dtype) for a in carried],
        input_output_aliases=aliases,
        compiler_params=_params("arbitrary"),
    )(parts, w, m, v, *prev, *carried)
    return (res[:4], res[4:]) if carried else res


def _pair_add(grad, other, spec, core, name):
    axis, width = spec
    slot = other.shape[1:]

    def body(core_ref, g_ref, o_ref, out_ref):
        out_ref[...] = (g_ref[...].astype(F32) + o_ref[...].astype(F32)).astype(BF16)

    if axis == 0:
        gspec = pl.BlockSpec(slot, lambda q, core_ref: (2 * q + core_ref[0], 0))
    else:
        gspec = pl.BlockSpec(slot, lambda q, core_ref: (0, 2 * q + core_ref[0]))
    per_chip = pl.BlockSpec((None,) + slot, lambda q, core_ref: (q, 0, 0))
    return _pcall(
        body, name=name,
        grid_spec=pltpu.PrefetchScalarGridSpec(num_scalar_prefetch=1, grid=(N_CHIP,), in_specs=[gspec, per_chip], out_specs=per_chip),
        out_shape=jax.ShapeDtypeStruct(other.shape, BF16),
        compiler_params=_params("arbitrary"),
    )(core, grad, other)


def _hbm():
    return pl.BlockSpec(memory_space=pltpu.HBM)


def _window(ref, spec, j):
    axis, width = spec
    start = pl.multiple_of(j * width, width)
    return ref.at[(slice(None),) * axis + (pl.ds(start, width),)]


def _here():
    return lax.axis_index("x"), lax.axis_index("y"), lax.axis_index("c")


class _Gather:
    def __init__(self, shards, specs, fulls):
        n = len(shards)
        self.ins, self.specs, self.out_shape = list(shards), list(specs), list(fulls)
        self.sems = [pltpu.SemaphoreType.DMA((7 * n,)), pltpu.SemaphoreType.DMA((7 * n,)), pltpu.SemaphoreType.DMA((n,))]

    def _plan(self, ins, outs, sems):
        send_sems, recv_sems, local_sems = sems
        x, y, c = _here()
        me, sibling = (x, y, c), (x, y, 1 - c)
        chips = [(1 - x, y), (x, 1 - y), (1 - x, 1 - y)]

        def slot(t, dev):
            return _window(outs[t], self.specs[t], 4 * dev[0] + 2 * dev[1] + dev[2])

        def copy(t, k, block, to, src=None):
            return pltpu.make_async_remote_copy(
                src_ref=slot(t, block) if src is None else src, dst_ref=slot(t, block),
                send_sem=send_sems.at[7 * t + k], recv_sem=recv_sems.at[7 * t + k], device_id=to, device_id_type=MESH)

        plan = []
        for t in range(len(ins)):
            plan.append(dict(
                mine=pltpu.make_async_copy(ins[t], slot(t, me), local_sems.at[t]),
                first=[copy(t, 0, me, sibling, src=ins[t])] + [copy(t, 1 + j, me, (*q, c), src=ins[t]) for j, q in enumerate(chips)],
                over_ici=[copy(t, 1 + j, (*q, c), me) for j, q in enumerate(chips)],
                passed=[copy(t, 4 + j, (*q, c), sibling) for j, q in enumerate(chips)],
                from_sibling=[copy(t, 0, sibling, me)] + [copy(t, 4 + j, (*q, 1 - c), me) for j, q in enumerate(chips)]))
        return plan

    def start(self, ins, outs, sems):
        for p in self._plan(ins, outs, sems):
            p["mine"].start()
            for cp in p["first"]:
                cp.start()

    def middle(self, ins, outs, sems):
        for p in self._plan(ins, outs, sems):
            for arrived, onward in zip(p["over_ici"], p["passed"]):
                arrived.wait_recv()
                onward.start()

    def finish(self, ins, outs, sems):
        plan = self._plan(ins, outs, sems)
        for p in plan:
            for cp in p["from_sibling"]:
                cp.wait_recv()
        for p in plan:
            for cp in p["first"] + p["passed"]:
                cp.wait_send()
            p["mine"].wait()


class _PairExchange:
    def __init__(self, grads, specs):
        n = len(grads)
        self.ins, self.specs = list(grads), list(specs)
        self.out_shape = [jax.ShapeDtypeStruct((N_CHIP,) + a.shape[:sp[0]] + (sp[1],) + a.shape[sp[0] + 1:], a.dtype)
                          for a, sp in zip(grads, specs)]
        self.sems = [pltpu.SemaphoreType.DMA((n,)), pltpu.SemaphoreType.DMA((n,))]

    def start(self, ins, outs, sems):
        send_sems, recv_sems = sems
        x, y, c = _here()
        for t in range(len(ins)):
            for q in range(N_CHIP):
                pltpu.make_async_remote_copy(
                    src_ref=_window(ins[t], self.specs[t], 2 * q + (1 - c)), dst_ref=outs[t].at[q],
                    send_sem=send_sems.at[t], recv_sem=recv_sems.at[t], device_id=(x, y, 1 - c), device_id_type=MESH).start()

    def middle(self, ins, outs, sems):
        pass

    def finish(self, ins, outs, sems):
        send_sems, recv_sems = sems
        x, y, c = _here()
        for t in range(len(ins)):
            every = pltpu.make_async_remote_copy(src_ref=outs[t], dst_ref=outs[t], send_sem=send_sems.at[t],
                                                 recv_sem=recv_sems.at[t], device_id=(x, y, 1 - c), device_id_type=MESH)
            every.wait_send()
            every.wait_recv()


class _ChipExchange:
    def __init__(self, pairs, slotted=(), whole=()):
        self.ins = list(pairs) + list(slotted) + list(whole)
        self.npair, self.nslot = len(pairs), len(pairs) + len(slotted)
        n = len(self.ins)
        self.out_shape = ([jax.ShapeDtypeStruct(a.shape, a.dtype) for a in list(pairs) + list(slotted)]
                          + [jax.ShapeDtypeStruct((N_DEV,) + a.shape, a.dtype) for a in whole])
        self.sems = [pltpu.SemaphoreType.DMA((7 * n,)), pltpu.SemaphoreType.DMA((7 * n,)), pltpu.SemaphoreType.DMA((n,))]

    def _plan(self, ins, outs, sems):
        send_sems, recv_sems, local_sems = sems
        npair, nslot = self.npair, self.nslot
        x, y, c = _here()
        me, chip = 4 * x + 2 * y + c, 2 * x + y
        chips = [(1 - x, y), (x, 1 - y), (1 - x, 1 - y)]
        peers = [(x, y, 1 - c)] + [(*q, c) for q in chips] + [(*q, 1 - c) for q in chips]

        def index(dev):
            return 4 * dev[0] + 2 * dev[1] + dev[2]

        def copy(t, k, arriving):
            peer = peers[k]
            if t < npair:
                src, mine, theirs = ins[t].at[2 * peer[0] + peer[1]], chip, 2 * peer[0] + peer[1]
            else:
                src, mine, theirs = (ins[t].at[index(peer)] if t < nslot else ins[t]), me, index(peer)
            return pltpu.make_async_remote_copy(
                src_ref=src, dst_ref=outs[t].at[theirs if arriving else mine],
                send_sem=send_sems.at[7 * t + k], recv_sem=recv_sems.at[7 * t + k], device_id=peer, device_id_type=MESH)

        own, sent, arriving = [], [], []
        for t in range(len(ins)):
            fan = range(1, 4) if t < npair else range(7)
            if t < npair:
                own.append(pltpu.make_async_copy(ins[t].at[chip], outs[t].at[chip], local_sems.at[t]))
            else:
                own.append(pltpu.make_async_copy(ins[t].at[me] if t < nslot else ins[t], outs[t].at[me], local_sems.at[t]))
            sent += [copy(t, k, False) for k in fan]
            arriving += [copy(t, k, True) for k in fan]
        return own, sent, arriving

    def start(self, ins, outs, sems):
        own, sent, _ = self._plan(ins, outs, sems)
        for cp in own + sent:
            cp.start()

    def middle(self, ins, outs, sems):
        pass

    def finish(self, ins, outs, sems):
        own, sent, arriving = self._plan(ins, outs, sems)
        for cp in arriving:
            cp.wait_recv()
        for cp in sent:
            cp.wait_send()
        for cp in own:
            cp.wait()


class _Jobs:
    def __init__(self, *jobs):
        self.jobs = jobs
        self.ins = [a for j in jobs for a in j.ins]
        self.out_shape = [a for j in jobs for a in j.out_shape]
        self.sems = [a for j in jobs for a in j.sems]

    def _split(self, ins, outs, sems):
        i = o = s = 0
        for j in self.jobs:
            yield j, ins[i:i + len(j.ins)], outs[o:o + len(j.out_shape)], sems[s:s + len(j.sems)]
            i, o, s = i + len(j.ins), o + len(j.out_shape), s + len(j.sems)

    def start(self, ins, outs, sems):
        for j, a, b, c in self._split(ins, outs, sems):
            j.start(a, b, c)

    def middle(self, ins, outs, sems):
        for j, a, b, c in self._split(ins, outs, sems):
            j.middle(a, b, c)

    def finish(self, ins, outs, sems):
        for j, a, b, c in self._split(ins, outs, sems):
            j.finish(a, b, c)

    def results(self, outs):
        return [b for _, _, b, _ in self._split((), outs, ())]


def _alone(job, name):
    return _pcall(lambda: None, comm=job, name=name, in_specs=[], out_specs=[], out_shape=[])()[1]


SMALL_ROWS = 24
REP_COLS = 1024


def _pad_to(a, rows, cols):
    return jnp.pad(a, ((0, rows - a.shape[0]), (0, cols - a.shape[1])))


def _pack_small(conv_a, sgu_norm, conv_ffn, cols):
    return jnp.concatenate([_pad_to(conv_a, 8, cols), _pad_to(sgu_norm, 8, cols),
                            _pad_to(conv_ffn.reshape(-1, conv_ffn.shape[-1]), 8, cols)], axis=0)


def _unpack_small(p, ca_w, sg_w, cf_w):
    return p[0:3, 0:ca_w], p[8:9, 0:sg_w], p[16:22, 0:cf_w].reshape(2, 3, cf_w)


def _tile_rows(rows):
    return -(-rows // 8) * 8


def _pack_rows(parts):
    return jnp.concatenate([_pad_to(a, _tile_rows(a.shape[0]), REP_COLS) for a in parts], axis=0)


def _unpack_rows(p, shapes):
    out, r0 = [], 0
    for r, c in shapes:
        out.append(p[r0:r0 + r, 0:c])
        r0 += _tile_rows(r)
    return out


def _rep_late(norm_mix0, norm_ffn0, pool_scale, b_conv0):
    return _pack_rows([norm_mix0, norm_ffn0, pool_scale.reshape(1, -1), _pad_to(b_conv0, 1, 3 * REP_COLS).reshape(3, REP_COLS)])


def _rep_early(norm_mix1, norm_ffn1, final_norm, b_spatial, b_conv1, loss):
    return _pack_rows([norm_mix1, norm_ffn1, final_norm.reshape(1, -1), b_spatial.reshape(1, -1),
                       _pad_to(b_conv1, 1, 3 * REP_COLS).reshape(3, REP_COLS), loss])


def _unpack_rep(late, early, wsp, wp, like):
    f = like["b_conv_ffn"].shape[1]
    nm0, nf0, ps, bc0 = _unpack_rows(late, [(1, REP_COLS), (1, REP_COLS), (1, like["pool_scale"].shape[1]), (3, REP_COLS)])
    nm1, nf1, fin, bs, bc1, _ = _unpack_rows(early, [(1, REP_COLS)] * 4 + [(3, REP_COLS), (1, REP_COLS)])
    return {
        "norm_mix": jnp.concatenate([nm0, nm1]), "norm_ffn": jnp.concatenate([nf0, nf1]), "final_norm": fin[0], "pool_scale": ps,
        "b_spatial": bs.reshape(like["b_spatial"].shape),
        "b_conv_ffn": jnp.concatenate([bc0.reshape(1, -1), bc1.reshape(1, -1)])[:, 0:f],
        "w_pool": wp.reshape(like["w_pool"].shape), "w_spatial": wsp.reshape(like["w_spatial"].shape),
    }


def _pad_slots(a, width, padded):
    a = a.reshape(*a.shape[:-1], N_DEV, width)
    a = jnp.pad(a, ((0, 0),) * (a.ndim - 1) + ((0, padded - width),))
    return a.reshape(*a.shape[:-2], N_DEV * padded)


def _unpad_slots(a, width, padded):
    a = a.reshape(*a.shape[:-1], N_DEV, padded)[..., 0:width]
    return a.reshape(*a.shape[:-2], N_DEV * width)


def kernel(x, norm_mix, norm_ffn, final_norm, w_in_even, conv_a, w_pool, pool_scale, w_out_even, w_in_odd, sgu_norm, w_spatial, b_spatial, w_out_odd, w_ffn_gate, w_ffn_up, conv_ffn, b_conv_ffn, w_ffn_down, loss_target, m_norm_mix, m_norm_ffn, m_final_norm, m_w_in_even, m_conv_a, m_w_pool, m_pool_scale, m_w_out_even, m_w_in_odd, m_sgu_norm, m_w_spatial, m_b_spatial, m_w_out_odd, m_w_ffn_gate, m_w_ffn_up, m_conv_ffn, m_b_conv_ffn, m_w_ffn_down, v_norm_mix, v_norm_ffn, v_final_norm, v_w_in_even, v_conv_a, v_w_pool, v_pool_scale, v_w_out_even, v_w_in_odd, v_sgu_norm, v_w_spatial, v_b_spatial, v_w_out_odd, v_w_ffn_gate, v_w_ffn_up, v_conv_ffn, v_b_conv_ffn, v_w_ffn_down):
    s, d = x.shape[1], x.shape[2]
    x2, target = x[0], loss_target[0]
    tm = min(512, s)
    tm_wide = min(2048, s)
    tk_wide = 1024
    tn_gu = 256
    tn_fwd = 512
    row = lambda a: a.reshape(1, -1)
    ein, ro = w_in_even.shape[2], w_out_even.shape[1]
    fs = w_ffn_gate.shape[2]
    fsp = -(-fs // LANE) * LANE
    fp = N_DEV * fsp
    full = lambda shape, dtype=BF16: jax.ShapeDtypeStruct(shape, dtype)

    wg_s = [jnp.pad(w_ffn_gate[l].T, ((0, fsp - fs), (0, 0))).astype(BF16) for l in range(2)]
    wu_s = [jnp.pad(w_ffn_up[l].T, ((0, fsp - fs), (0, 0))).astype(BF16) for l in range(2)]
    wd_s = [jnp.pad(w_ffn_down[l], ((0, fsp - fs), (0, 0))).astype(BF16) for l in range(2)]
    small_s = _pack_small(conv_a[0], sgu_norm, conv_ffn, fsp)[None]
    in_spec, out_spec, gu_spec, down_spec = (1, ein), (0, ro), (0, fsp), (0, fsp)
    full_in, full_out, full_gu, full_down = full((d, N_DEV * ein)), full((N_DEV * ro, d)), full((fp, d)), full((fp, d))
    win_e, wout_e, gsmall, wg0 = _alone(_Gather(
        [w_in_even[0].astype(BF16), w_out_even[0].astype(BF16), small_s, wg_s[0]], [in_spec, out_spec, (0, 1), gu_spec],
        [full_in, full_out, full((N_DEV, SMALL_ROWS, fsp), F32), full_gu]), "gather_mix0")
    ca_full = jnp.moveaxis(gsmall[:, 0:3, 0:conv_a.shape[2]], 0, 1).reshape(3, -1)
    sgu_full = gsmall[:, 8, 0:sgu_norm.shape[1]].reshape(1, -1)
    cf_full = jnp.moveaxis(gsmall[:, 16:22, :].reshape(N_DEV, 2, 3, fsp), 0, 2).reshape(2, 3, fp)
    cb_full = _pad_slots(b_conv_ffn, fs, fsp)

    tril = jnp.tril(jnp.ones((CHUNK, CHUNK), F32))
    ws_m = w_spatial[0] * tril
    ws_b = ws_m.astype(BF16)
    wst_b = jnp.swapaxes(ws_m, 1, 2).astype(BF16)
    bfull = jnp.repeat(b_spatial[0].T, CHUNK, axis=1)
    wpool_b = w_pool[0].astype(BF16)
    wpoolt_b = jnp.swapaxes(w_pool[0], 1, 2).astype(BF16)

    (xn0, proj0, cq0, pooled0, mix0, h1, hn0), (wu0,) = _even_fwd(
        x2, norm_mix[0:1], norm_ffn[0:1], win_e, ca_full, wpool_b, pool_scale, wout_e, tm,
        comm=_Gather([wu_s[0]], [gu_spec], [full_gu]))
    (g0, a0, sl0, ud0), (wd0, win_o, wout_o) = _ffn_gate_up(
        hn0, wg0, wu0, cf_full[0], cb_full[0:1], tm_wide, tn_gu, "ffn_gate_up_l0",
        comm=_Gather([wd_s[0], w_in_odd[0].astype(BF16), w_out_odd[0].astype(BF16)], [down_spec, in_spec, out_spec], [full_down, full_in, full_out]))
    (h2, xn1), (wg1,) = _ffn_fwd2(a0, wd0, h1, norm_mix[1:2], tm, "ffn_fwd2_l0", comm=_Gather([wg_s[1]], [gu_spec], [full_gu]))
    (pre1, gate1, mixo1, h3, hn1), (wu1,) = _odd_fwd(xn1, h2, win_o, sgu_full, ws_b, bfull, wout_o, norm_ffn[1:2], tm,
                                                    comm=_Gather([wu_s[1]], [gu_spec], [full_gu]))
    (g1, a1, sl1, ud1), (wd1,) = _ffn_gate_up(hn1, wg1, wu1, cf_full[1], cb_full[1:2], tm_wide, tn_gu, "ffn_gate_up_l1",
                                              comm=_Gather([wd_s[1]], [down_spec], [full_down]))
    dh4, dh4b, d_final, lossvec = _ffn_fwd2_loss(a1, wd1, h3, row(final_norm), target, tm)

    core = lax.axis_index("c").astype(jnp.int32).reshape(1)
    ts = min(1024, s)
    gd1 = _wgrad(a1, dh4b, tk_wide, s, "wgrad_down_l1")
    (dg1, dup1, dcw1, dcb1), (o_d1,) = _ffn_bwd1(dh4b, g1, sl1, ud1, wd1, cf_full[1], tm_wide, tn_fwd, "ffn_bwd1_l1",
                                                 comm=_PairExchange([gd1], [down_spec]))
    p_d1 = _pair_add(gd1, o_d1, down_spec, core, "pair_add_down_l1")
    gg1 = _wgrad(dg1, hn1, tk_wide, s, "wgrad_gate_l1")
    gu1, (o_g1,) = _wgrad(dup1, hn1, tk_wide, s, "wgrad_up_l1", comm=_PairExchange([gg1], [gu_spec]))
    p_g1 = _pair_add(gg1, o_g1, gu_spec, core, "pair_add_gate_l1")
    jobs = _Jobs(_ChipExchange([p_d1]), _PairExchange([gu1], [gu_spec]))
    (dh3, dh3b, d_nffn1), res = _ffn_bwd2(dg1, dup1, wg1, wu1, h3, norm_ffn[1:2], dh4, tm, "ffn_bwd2_l1", comm=jobs)
    (s_d1,), (o_u1,) = jobs.results(res)
    p_u1 = _pair_add(gu1, o_u1, gu_spec, core, "pair_add_up_l1")
    (dpre1, dh2, dh2b, d_nmix1, d_sgu, d_ws, d_b), (s_g1, s_u1) = _odd_bwd(
        dh3b, dh3, wout_o, pre1, gate1, ws_b, wst_b, sgu_full, win_o, h2, norm_mix[1:2], tm, comm=_ChipExchange([p_g1, p_u1]))
    gi1 = _wgrad(xn1, dpre1, tk_wide, ts, "wgrad_in_odd")
    go1, (o_i1,) = _wgrad(mixo1, dh3b, tk_wide, ts, "wgrad_out_odd", comm=_PairExchange([gi1], [in_spec]))
    p_i1 = _pair_add(gi1, o_i1, in_spec, core, "pair_add_in_odd")
    d_wsp = (d_ws * tril).reshape(-1, CHUNK).astype(BF16)
    jobs = _Jobs(_PairExchange([go1], [out_spec]), _ChipExchange([], [], [d_wsp]))
    gd0, res = _wgrad(a0, dh2b, tk_wide, s, "wgrad_down_l0", comm=jobs)
    (o_o1,), (r_wsp,) = jobs.results(res)
    p_o1 = _pair_add(go1, o_o1, out_spec, core, "pair_add_out_odd")
    d_early = _rep_early(d_nmix1, d_nffn1, d_final, d_b, _unpad_slots(dcb1, fs, fsp), lossvec)
    jobs = _Jobs(_ChipExchange([p_i1, p_o1], [], [d_early]), _PairExchange([gd0], [down_spec]))
    (dg0, dup0, dcw0, dcb0), res = _ffn_bwd1(dh2b, g0, sl0, ud0, wd0, cf_full[0], tm_wide, tn_fwd, "ffn_bwd1_l0", comm=jobs)
    (s_i1, s_o1, r_early), (o_d0,) = jobs.results(res)
    p_d0 = _pair_add(gd0, o_d0, down_spec, core, "pair_add_down_l0")
    gg0, (s_d0,) = _wgrad(dg0, hn0, tk_wide, s, "wgrad_gate_l0", comm=_ChipExchange([p_d0]))
    gu0, (o_g0,) = _wgrad(dup0, hn0, tk_wide, s, "wgrad_up_l0", comm=_PairExchange([gg0], [gu_spec]))
    p_g0 = _pair_add(gg0, o_g0, gu_spec, core, "pair_add_gate_l0")
    jobs = _Jobs(_ChipExchange([p_g0]), _PairExchange([gu0], [gu_spec]))
    (dh1, dh1b, d_nffn0), res = _ffn_bwd2(dg0, dup0, wg0, wu0, h1, norm_ffn[0:1], dh2, tm, "ffn_bwd2_l0", comm=jobs)
    (s_g0,), (o_u0,) = jobs.results(res)
    p_u0 = _pair_add(gu0, o_u0, gu_spec, core, "pair_add_up_l0")
    go0 = _wgrad(mix0, dh1b, tk_wide, ts, "wgrad_out_even")
    jobs = _Jobs(_ChipExchange([p_u0]), _PairExchange([go0], [out_spec]))
    (dproj0, grad_x, d_nmix0, d_ca, d_wp, d_ps), res = _even_bwd(
        dh1b, dh1, wout_e, proj0, cq0, pooled0, ca_full, wpool_b, wpoolt_b, pool_scale, win_e, x2, norm_mix[0:1], tm, comm=jobs)
    (s_u0,), (o_o0,) = jobs.results(res)
    p_o0 = _pair_add(go0, o_o0, out_spec, core, "pair_add_out_even")
    d_small = jnp.stack([_pack_small(a, b, c, fsp) for a, b, c in zip(
        jnp.moveaxis(d_ca.reshape(3, N_DEV, -1), 1, 0), jnp.moveaxis(d_sgu.reshape(1, N_DEV, -1), 1, 0),
        jnp.moveaxis(jnp.stack([dcw0, dcw1]).reshape(2, 3, N_DEV, fsp), 2, 0))])
    d_late = _rep_late(d_nmix0, d_nffn0, d_ps, _unpad_slots(dcb0, fs, fsp))
    d_wpl = d_wp.reshape(-1, CHUNK).astype(BF16)
    gi0, (s_o0, r_small, r_late, r_wpl) = _wgrad(xn0, dproj0, tk_wide, ts, "wgrad_in_even",
                                                 comm=_ChipExchange([p_o0], [d_small], [d_late, d_wpl]))
    (o_i0,) = _alone(_PairExchange([gi0], [in_spec]), "pair_exchange_in_even")
    p_i0 = _pair_add(gi0, o_i0, in_spec, core, "pair_add_in_even")
    (s_i0,) = _alone(_ChipExchange([p_i0]), "chip_exchange_last")
    loss = jnp.sum(r_early[:, r_early.shape[1] - 8, :])

    out = {}
    out["w_in_even"], (grad_x,) = _adamw(s_i0, w_in_even[0], m_w_in_even[0], v_w_in_even[0], 256, "adamw_in_even", carried=[grad_x])
    out["w_out_even"] = _adamw(s_o0, w_out_even[0], m_w_out_even[0], v_w_out_even[0], ro // 2, "adamw_out_even")
    out["w_in_odd"] = _adamw(s_i1, w_in_odd[0], m_w_in_odd[0], v_w_in_odd[0], 256, "adamw_in_odd")
    out["w_out_odd"] = _adamw(s_o1, w_out_odd[0], m_w_out_odd[0], v_w_out_odd[0], ro // 2, "adamw_out_odd")
    tp = lambda a: jnp.swapaxes(a, 1, 2)
    for nm, s1, s0, w, m, v, back in (("w_ffn_gate", s_g1, s_g0, tp(w_ffn_gate), tp(m_w_ffn_gate), tp(v_w_ffn_gate), tp),
                                      ("w_ffn_up", s_u1, s_u0, tp(w_ffn_up), tp(m_w_ffn_up), tp(v_w_ffn_up), tp),
                                      ("w_ffn_down", s_d1, s_d0, w_ffn_down, m_w_ffn_down, v_w_ffn_down, lambda a: a)):
        l1 = _adamw(s1, w, m, v, fs // 2, "adamw_%s_l1" % nm, layer=1)
        out[nm] = [back(a) for a in _adamw(s0, w, m, v, fs // 2, "adamw_%s_l0" % nm, layer=0, into=l1)]
    small = _adamw(r_small, _pack_small(conv_a[0], sgu_norm, conv_ffn, fsp), _pack_small(m_conv_a[0], m_sgu_norm, m_conv_ffn, fsp),
                   _pack_small(v_conv_a[0], v_sgu_norm, v_conv_ffn, fsp), SMALL_ROWS, "adamw_small")
    no_loss = jnp.zeros((1, REP_COLS), F32)
    early = _adamw(r_early, *[_rep_early(nm[1:2], nf[1:2], fn, bs, bc[1:2], no_loss) for nm, nf, fn, bs, bc in (
        (norm_mix, norm_ffn, final_norm, b_spatial, b_conv_ffn), (m_norm_mix, m_norm_ffn, m_final_norm, m_b_spatial, m_b_conv_ffn),
        (v_norm_mix, v_norm_ffn, v_final_norm, v_b_spatial, v_b_conv_ffn))], r_early.shape[1], "adamw_replicated_early")
    wsp = _adamw(r_wsp, w_spatial.reshape(-1, CHUNK), m_w_spatial.reshape(-1, CHUNK), v_w_spatial.reshape(-1, CHUNK),
                 r_wsp.shape[1], "adamw_w_spatial")
    late = _adamw(r_late, *[_rep_late(nm[0:1], nf[0:1], ps, bc[0:1]) for nm, nf, ps, bc in (
        (norm_mix, norm_ffn, pool_scale, b_conv_ffn), (m_norm_mix, m_norm_ffn, m_pool_scale, m_b_conv_ffn),
        (v_norm_mix, v_norm_ffn, v_pool_scale, v_b_conv_ffn))], r_late.shape[1], "adamw_replicated_late")
    wpl = _adamw(r_wpl, w_pool.reshape(-1, CHUNK), m_w_pool.reshape(-1, CHUNK), v_w_pool.reshape(-1, CHUNK),
                 r_wpl.shape[1], "adamw_w_pool")

    names = ["norm_mix", "norm_ffn", "final_norm", "w_in_even", "conv_a", "w_pool", "pool_scale", "w_out_even", "w_in_odd", "sgu_norm",
             "w_spatial", "b_spatial", "w_out_odd", "w_ffn_gate", "w_ffn_up", "conv_ffn", "b_conv_ffn", "w_ffn_down"]
    like = {"norm_mix": norm_mix, "norm_ffn": norm_ffn, "final_norm": final_norm, "w_in_even": w_in_even, "conv_a": conv_a,
            "w_pool": w_pool, "pool_scale": pool_scale, "w_out_even": w_out_even, "w_in_odd": w_in_odd, "sgu_norm": sgu_norm,
            "w_spatial": w_spatial, "b_spatial": b_spatial, "w_out_odd": w_out_odd, "w_ffn_gate": w_ffn_gate, "w_ffn_up": w_ffn_up,
            "conv_ffn": conv_ffn, "b_conv_ffn": b_conv_ffn, "w_ffn_down": w_ffn_down}
    groups = []
    for k in range(4):
        ca_k, sg_k, cf_k = _unpack_small(small[k], conv_a.shape[2], sgu_norm.shape[1], conv_ffn.shape[2])
        vals = dict(_unpack_rep(late[k], early[k], wsp[k], wpl[k], like))
        vals.update(conv_a=ca_k, sgu_norm=sg_k, conv_ffn=cf_k)
        for nm in ("w_in_even", "w_in_odd", "w_out_even", "w_out_odd", "w_ffn_gate", "w_ffn_up", "w_ffn_down"):
            vals[nm] = out[nm][k]
        groups.append([vals[nm].reshape(like[nm].shape) for nm in names])
    return (loss, grad_x[None], *groups[0], *groups[1], *groups[2], *groups[3])
```

```python
import functools

import jax
import jax.numpy as jnp
from jax import lax
from jax.experimental import pallas as pl
from jax.experimental.pallas import tpu as pltpu

F32, BF16 = jnp.float32, jnp.bfloat16
EPS = 1e-6
WINDOWS = (2, 4, 8, 16)
HALO = 16
CHUNK = 128
N_DEV = 8
N_CHIP = 4
MESH = pl.DeviceIdType.MESH
VMEM_LIMIT = 56 * 2**20
LANE = 128
ADAM_LR, ADAM_B1, ADAM_B2, ADAM_EPS, ADAM_WD, ADAM_STEP = 0.001, 0.9, 0.999, 1e-08, 0.01, 10
SUB_ROWS = 256
LATE_NUM, LATE_DEN = 7, 8
INV_SQRT2 = 0.7071067811865476
INV_SQRT2PI = 0.3989422804014327


def _pcall(body, comm=None, **kw):
    if comm is None:
        return pl.pallas_call(body, **kw)
    in_specs, out_specs, out_shape = list(kw.pop("in_specs")), kw.pop("out_specs"), kw.pop("out_shape")
    single = not isinstance(out_shape, (list, tuple))
    out_specs, out_shape = ([out_specs], [out_shape]) if single else (list(out_specs), list(out_shape))
    scratch = list(kw.pop("scratch_shapes", []))
    grid = kw.get("grid", ())
    n_in, n_out, n_scr, c_in, c_out = len(in_specs), len(out_specs), len(scratch), len(comm.ins), len(comm.out_shape)

    def hosted(*refs):
        cuts = [0, n_in, n_in + c_in, n_in + c_in + n_out, n_in + c_in + n_out + c_out, n_in + c_in + n_out + c_out + n_scr, len(refs)]
        ins, cins, outs, couts, scr, sems = (refs[a:b] for a, b in zip(cuts[:-1], cuts[1:]))
        if grid:
            step, steps = 0, 1
            for axis, size in enumerate(grid):
                step, steps = step * size + pl.program_id(axis), steps * size
            pl.when(step == 0)(lambda: comm.start(cins, couts, sems))
            pl.when(step == (steps * LATE_NUM) // LATE_DEN)(lambda: comm.middle(cins, couts, sems))
            body(*ins, *outs, *scr)
            pl.when(step == steps - 1)(lambda: comm.finish(cins, couts, sems))
        else:
            comm.start(cins, couts, sems)
            comm.middle(cins, couts, sems)
            body(*ins, *outs, *scr)
            comm.finish(cins, couts, sems)

    call = pl.pallas_call(hosted, in_specs=in_specs + [_hbm()] * c_in, out_specs=out_specs + [_hbm()] * c_out,
                          out_shape=out_shape + list(comm.out_shape), scratch_shapes=scratch + list(comm.sems), **kw)

    def run(*args):
        res = call(*args, *comm.ins)
        own = res[0] if single else res[:n_out]
        return own, res[n_out:]

    return run


def _params(*sem):
    return pltpu.CompilerParams(dimension_semantics=sem, vmem_limit_bytes=VMEM_LIMIT)


def _whole(shape):
    return pl.BlockSpec(shape, lambda *_: (0,) * len(shape))


def _resident(shape):
    return pl.BlockSpec(shape, lambda *_: (0,) * len(shape), pipeline_mode=pl.Buffered(1))


def _rows(i, tm):
    return pl.ds(pl.multiple_of(i * tm, tm), tm)


def _nn(a, b):
    return jnp.dot(a, b, preferred_element_type=F32)


def _nt(a, b):
    return lax.dot_general(a, b, (((1,), (1,)), ((), ())), preferred_element_type=F32)


def _tn(a, b):
    return lax.dot_general(a, b, (((0,), (0,)), ((), ())), preferred_element_type=F32)


def _rms(x, gain):
    r = lax.rsqrt(jnp.mean(x * x, axis=-1, keepdims=True) + EPS)
    return x * r * gain


def _rms_bwd(dy, x, gain):
    r = lax.rsqrt(jnp.mean(x * x, axis=-1, keepdims=True) + EPS)
    xh = x * r
    dgain = jnp.sum(dy * xh, axis=0, keepdims=True)
    dxh = dy * gain
    dx = r * (dxh - xh * jnp.mean(dxh * xh, axis=-1, keepdims=True))
    return dx, dgain


def _gelu(x):
    return 0.5 * x * (1.0 + lax.erf(x * INV_SQRT2))


def _gelu_grad(x):
    return 0.5 * (1.0 + lax.erf(x * INV_SQRT2)) + x * jnp.exp(-0.5 * x * x) * INV_SQRT2PI


def _acc(ref, val, first):
    @pl.when(first)
    def _():
        ref[...] = val

    @pl.when(jnp.logical_not(first))
    def _():
        ref[...] += val


def _shift(ext, k, back):
    n = ext.shape[0]
    return pltpu.roll(ext, k if back else n - k, axis=0)


def _window_sum(ext, w, back):
    total, step = ext, 1
    while step < w:
        total = total + _shift(total, step, back)
        step *= 2
    return total


def _counts(row0, tm, w):
    pos1 = (row0 + lax.broadcasted_iota(jnp.int32, (tm, 1), 0) + 1).astype(F32)
    return jnp.minimum(pos1, float(w))


def _even_fwd(x, gmix, gffn, win, conva, wpool, pscale, wout, tm, comm=None):
    s, d = x.shape
    e = win.shape[1]
    aw = e // 4

    def body(x_ref, gmix_ref, gffn_ref, win_ref, ca_ref, wp_ref, ps_ref, wout_ref,
             xn_ref, proj_ref, cq_ref, pooled_ref, mix_ref, h_ref, hn_ref, qbuf, zbuf):
        i = pl.program_id(0)

        @pl.when(i == 0)
        def _():
            qbuf[0:HALO, :] = jnp.zeros((HALO, aw), F32)
            zbuf[0:HALO, :] = jnp.zeros((HALO, aw), F32)

        xv = x_ref[...]
        xn = _rms(xv, gmix_ref[...]).astype(BF16)
        xn_ref[...] = xn
        proj = _nn(xn, win_ref[...])
        proj_ref[...] = proj.astype(BF16)
        a_b, a_c, a_v, z = (proj[:, k * aw:(k + 1) * aw] for k in range(4))
        q = a_c * a_v
        qbuf[HALO:HALO + tm, :] = q
        qext = qbuf[...]
        cur = slice(HALO, HALO + tm)
        cq = ca_ref[2:3, :] * q + ca_ref[1:2, :] * _shift(qext, 1, True)[cur, :] + ca_ref[0:1, :] * _shift(qext, 2, True)[cur, :]
        cq_ref[...] = cq.astype(BF16)
        y_a = a_b * cq
        zbuf[HALO:HALO + tm, :] = z
        zext = zbuf[...]
        ys = []
        for g, w in enumerate(WINDOWS):
            cols = slice(g * LANE, (g + 1) * LANE)
            acc = _window_sum(zext[:, cols], w, True)[cur, :]
            pooled = (acc / _counts(i * tm, tm, w) - z[:, cols]).astype(BF16)
            pooled_ref[:, cols] = pooled
            ys.append(_nn(pooled, wp_ref[g]))
        y_b = jnp.concatenate(ys, axis=1) * ps_ref[...]
        mix = jnp.concatenate([y_a, y_b], axis=1).astype(BF16)
        mix_ref[...] = mix
        h = xv + _nn(mix, wout_ref[...])
        h_ref[...] = h
        hn_ref[...] = _rms(h, gffn_ref[...]).astype(BF16)
        qbuf[0:HALO, :] = qbuf[tm:tm + HALO, :]
        zbuf[0:HALO, :] = zbuf[tm:tm + HALO, :]

    row = lambda c: pl.BlockSpec((tm, c), lambda i: (i, 0))
    return _pcall(
        body, comm=comm, name="even_fwd", grid=(s // tm,),
        in_specs=[row(d), _whole((1, d)), _whole((1, d)), _whole(win.shape), _whole(conva.shape), _whole(wpool.shape),
                  _whole(pscale.shape), _whole(wout.shape)],
        out_specs=[row(d), row(e), row(aw), row(aw), row(d), row(d), row(d)],
        out_shape=[jax.ShapeDtypeStruct((s, d), BF16), jax.ShapeDtypeStruct((s, e), BF16), jax.ShapeDtypeStruct((s, aw), BF16),
                   jax.ShapeDtypeStruct((s, aw), BF16), jax.ShapeDtypeStruct((s, d), BF16), jax.ShapeDtypeStruct((s, d), F32),
                   jax.ShapeDtypeStruct((s, d), BF16)],
        scratch_shapes=[pltpu.VMEM((tm + HALO, aw), F32), pltpu.VMEM((tm + HALO, aw), F32)],
        compiler_params=_params("arbitrary"),
    )(x, gmix, gffn, win, conva, wpool, pscale, wout)


def _ffn_gate(hn, wgt, cw, cb, tm, tn, name, comm=None):
    s, d = hn.shape
    f = wgt.shape[0]
    sub = tm

    def body(hn_ref, wg_ref, cw_ref, cb_ref, g_ref, gc_ref, gbuf):
        i = pl.program_id(1)

        @pl.when(i == 0)
        def _():
            gbuf[0:HALO, :] = jnp.zeros((HALO, tn), F32)

        wg = wg_ref[...]
        for c in range(tm // sub):
            rows = pl.ds(c * sub, sub)
            g = _nt(hn_ref[pl.ds(pl.multiple_of(i * tm + c * sub, sub), sub), :], wg)
            g_ref[rows, :] = g.astype(BF16)
            gbuf[pl.ds(HALO + c * sub, sub), :] = g
            ext = gbuf[pl.ds(c * sub, sub + HALO), :]
            gc = (cw_ref[2:3, :] * g + cw_ref[1:2, :] * _shift(ext, 1, True)[HALO:, :]
                  + cw_ref[0:1, :] * _shift(ext, 2, True)[HALO:, :] + cb_ref[...])
            gc_ref[rows, :] = gc.astype(BF16)
        gbuf[0:HALO, :] = gbuf[tm:tm + HALO, :]

    tile = pl.BlockSpec((tm, tn), lambda j, i: (i, j))
    wcol = lambda r: pl.BlockSpec((r, tn), lambda j, i: (0, j))
    out = jax.ShapeDtypeStruct((s, f), BF16)
    return _pcall(
        body, comm=comm, name=name, grid=(f // tn, s // tm),
        in_specs=[_resident((s, d)), pl.BlockSpec((tn, d), lambda j, i: (j, 0)), wcol(3), wcol(1)],
        out_specs=[tile, tile], out_shape=[out, out],
        scratch_shapes=[pltpu.VMEM((tm + HALO, tn), F32)],
        compiler_params=_params("arbitrary", "arbitrary"),
    )(hn, wgt, cw, cb)


def _ffn_up(hn, wut, gc, tm, tn, name, comm=None):
    s, d = hn.shape
    f = wut.shape[0]
    sub = min(SUB_ROWS, tm)

    def body(hn_ref, wu_ref, gc_ref, a_ref, silu_ref, upds_ref):
        wu = wu_ref[...]
        for c in range(tm // sub):
            rows = pl.ds(c * sub, sub)
            up = _nt(hn_ref[pl.ds(pl.multiple_of(pl.program_id(1) * tm + c * sub, sub), sub), :], wu)
            gc = gc_ref[rows, :].astype(F32)
            sg = jax.nn.sigmoid(gc)
            silu = gc * sg
            silu_ref[rows, :] = silu.astype(BF16)
            upds_ref[rows, :] = (up * (sg * (1.0 - silu) + silu)).astype(BF16)
            a_ref[rows, :] = (silu * up).astype(BF16)

    tile = pl.BlockSpec((tm, tn), lambda j, i: (i, j))
    out = jax.ShapeDtypeStruct((s, f), BF16)
    return _pcall(
        body, comm=comm, name=name, grid=(f // tn, s // tm),
        in_specs=[_resident((s, d)), pl.BlockSpec((tn, d), lambda j, i: (j, 0)), tile],
        out_specs=[tile, tile, tile], out_shape=[out, out, out],
        compiler_params=_params("arbitrary", "arbitrary"),
    )(hn, wut, gc)


def _ffn_fwd2(a, wd, h, gain, tm, name, comm=None):
    s, d = h.shape
    f = a.shape[1]

    def body(a_ref, wd_ref, h_ref, gain_ref, ho_ref, hn_ref):
        wd_v, gain = wd_ref[...], gain_ref[...]
        sub = min(SUB_ROWS, tm)
        for c in range(tm // sub):
            rows = pl.ds(c * sub, sub)
            ho = h_ref[rows, :] + _nn(a_ref[rows, :], wd_v)
            ho_ref[rows, :] = ho
            hn_ref[rows, :] = _rms(ho, gain).astype(BF16)

    row = lambda c: pl.BlockSpec((tm, c), lambda i: (i, 0))
    return _pcall(
        body, comm=comm, name=name, grid=(s // tm,),
        in_specs=[row(f), _resident(wd.shape), row(d), _whole((1, d))],
        out_specs=[row(d), row(d)],
        out_shape=[jax.ShapeDtypeStruct((s, d), F32), jax.ShapeDtypeStruct((s, d), BF16)],
        compiler_params=_params("arbitrary"),
    )(a, wd, h, gain)


def _ffn_fwd2_loss(a, wd, h, gain, target, tm):
    s, d = h.shape
    f = a.shape[1]

    def body(a_ref, wd_ref, h_ref, gain_ref, t_ref, dh_ref, dhb_ref, dgain_ref, loss_ref):
        wd_v, gain = wd_ref[...], gain_ref[...]
        sub = min(SUB_ROWS, tm)
        dgain = loss = None
        for c in range(tm // sub):
            rows = pl.ds(c * sub, sub)
            ho = h_ref[rows, :] + _nn(a_ref[rows, :], wd_v)
            err = _rms(ho, gain) - t_ref[rows, :]
            dx, part = _rms_bwd(err * (1.0 / d), ho, gain)
            dh_ref[rows, :] = dx
            dhb_ref[rows, :] = dx.astype(BF16)
            sq = jnp.sum(err * err, axis=0, keepdims=True) * (0.5 / d)
            dgain, loss = (part, sq) if dgain is None else (dgain + part, loss + sq)
        _acc(dgain_ref, dgain, pl.program_id(0) == 0)
        _acc(loss_ref, loss, pl.program_id(0) == 0)

    row = lambda c: pl.BlockSpec((tm, c), lambda i: (i, 0))
    return _pcall(
        body, name="ffn_fwd2_loss", grid=(s // tm,),
        in_specs=[row(f), _resident(wd.shape), row(d), _whole((1, d)), row(d)],
        out_specs=[row(d), row(d), _whole((1, d)), _whole((1, d))],
        out_shape=[jax.ShapeDtypeStruct((s, d), F32), jax.ShapeDtypeStruct((s, d), BF16), jax.ShapeDtypeStruct((1, d), F32),
                   jax.ShapeDtypeStruct((1, d), F32)],
        compiler_params=_params("arbitrary"),
    )(a, wd, h, gain, target)


def _odd_fwd(xn, h, win, sgu, ws, bfull, wout, gffn, tm, comm=None):
    s, d = h.shape
    e = win.shape[1]
    cw = e // 2
    heads = ws.shape[0]

    def body(xn_ref, h_ref, win_ref, sgu_ref, ws_ref, b_ref, wout_ref, gffn_ref,
             pre_ref, gate_ref, mixo_ref, ho_ref, hn_ref, gbuf):
        pre = _nn(xn_ref[...], win_ref[...])
        pre_ref[...] = pre.astype(BF16)
        p = _gelu(pre)
        u, v = p[:, :cw], p[:, cw:]
        vn = _rms(v, sgu_ref[...]).astype(BF16)
        for n in range(tm // CHUNK):
            rows = slice(n * CHUNK, (n + 1) * CHUNK)
            for hd in range(heads):
                cols = slice(hd * CHUNK, (hd + 1) * CHUNK)
                gbuf[rows, cols] = _nn(ws_ref[hd], vn[rows, cols]) + b_ref[:, cols]
        gate = gbuf[...]
        gate_ref[...] = gate.astype(BF16)
        mixo = (u * gate).astype(BF16)
        mixo_ref[...] = mixo
        ho = h_ref[...] + _nn(mixo, wout_ref[...])
        ho_ref[...] = ho
        hn_ref[...] = _rms(ho, gffn_ref[...]).astype(BF16)

    row = lambda c: pl.BlockSpec((tm, c), lambda i: (i, 0))
    return _pcall(
        body, comm=comm, name="odd_fwd", grid=(s // tm,),
        in_specs=[row(d), row(d), _whole(win.shape), _whole(sgu.shape), _whole(ws.shape), _whole(bfull.shape),
                  _whole(wout.shape), _whole((1, d))],
        out_specs=[row(e), row(cw), row(cw), row(d), row(d)],
        out_shape=[jax.ShapeDtypeStruct((s, e), BF16), jax.ShapeDtypeStruct((s, cw), BF16), jax.ShapeDtypeStruct((s, cw), BF16),
                   jax.ShapeDtypeStruct((s, d), F32), jax.ShapeDtypeStruct((s, d), BF16)],
        scratch_shapes=[pltpu.VMEM((tm, cw), F32)],
        compiler_params=_params("arbitrary"),
    )(xn, h, win, sgu, ws, bfull, wout, gffn)


def _ffn_bwd1(dhb, g, silu, upds, wd, cw, tm, tn, name, comm=None):
    s, d = dhb.shape
    f = g.shape[1]
    ni = s // tm

    def body(dh_ref, g_ref, silu_ref, upds_ref, wd_ref, cw_ref, dg_ref, dup_ref, dcw_ref, dcb_ref, ebuf):
        i = pl.program_id(1)

        @pl.when(i == 0)
        def _():
            ebuf[tm:tm + HALO, :] = jnp.zeros((HALO, tn), F32)

        wd_v = wd_ref[...]
        sub = min(SUB_ROWS, tm)
        sums = [None] * 4
        for c in reversed(range(tm // sub)):
            rows = pl.ds(c * sub, sub)
            da = _nt(dh_ref[pl.ds(pl.multiple_of((ni - 1 - i) * tm + c * sub, sub), sub), :], wd_v)
            dup_ref[rows, :] = (da * silu_ref[rows, :].astype(F32)).astype(BF16)
            dgc = da * upds_ref[rows, :].astype(F32)
            ebuf[rows, :] = dgc
            ext = ebuf[pl.ds(c * sub, sub + HALO), :]
            s1 = _shift(ext, 1, False)[0:sub, :]
            s2 = _shift(ext, 2, False)[0:sub, :]
            dg_ref[rows, :] = (cw_ref[2:3, :] * dgc + cw_ref[1:2, :] * s1 + cw_ref[0:1, :] * s2).astype(BF16)
            gv = g_ref[rows, :].astype(F32)
            for k, term in enumerate((s2 * gv, s1 * gv, dgc * gv, dgc)):
                part = jnp.sum(term, axis=0, keepdims=True)
                sums[k] = part if sums[k] is None else sums[k] + part
        for k in range(3):
            _acc(dcw_ref.at[k:k + 1, :], sums[k], i == 0)
        _acc(dcb_ref, sums[3], i == 0)
        ebuf[tm:tm + HALO, :] = ebuf[0:HALO, :]

    tile = pl.BlockSpec((tm, tn), lambda j, i: (ni - 1 - i, j))
    wcol = lambda r: pl.BlockSpec((r, tn), lambda j, i: (0, j))
    out = jax.ShapeDtypeStruct((s, f), BF16)
    return _pcall(
        body, comm=comm, name=name, grid=(f // tn, ni),
        in_specs=[_resident((s, d)), tile, tile, tile,
                  pl.BlockSpec((tn, d), lambda j, i: (j, 0)), wcol(3)],
        out_specs=[tile, tile, wcol(3), wcol(1)],
        out_shape=[out, out, jax.ShapeDtypeStruct((3, f), F32), jax.ShapeDtypeStruct((1, f), F32)],
        scratch_shapes=[pltpu.VMEM((tm + HALO, tn), F32)],
        compiler_params=_params("arbitrary", "arbitrary"),
    )(dhb, g, silu, upds, wd, cw)


def _ffn_bwd2(dg, dup, wg, wu, h, gain, dh, tm, name, comm=None):
    s, d = h.shape
    f = dg.shape[1]

    def body(dg_ref, dup_ref, wg_ref, wu_ref, h_ref, gain_ref, dh_ref, dho_ref, dhb_ref, dgain_ref):
        wg_v, wu_v, gain = wg_ref[...], wu_ref[...], gain_ref[...]
        sub = min(SUB_ROWS, tm)
        dgain = None
        for c in range(tm // sub):
            rows = pl.ds(c * sub, sub)
            dhn = _nn(dg_ref[rows, :], wg_v) + _nn(dup_ref[rows, :], wu_v)
            dx, part = _rms_bwd(dhn, h_ref[rows, :], gain)
            dgain = part if dgain is None else dgain + part
            dho = dh_ref[rows, :] + dx
            dho_ref[rows, :] = dho
            dhb_ref[rows, :] = dho.astype(BF16)
        _acc(dgain_ref, dgain, pl.program_id(0) == 0)

    row = lambda c: pl.BlockSpec((tm, c), lambda i: (i, 0))
    return _pcall(
        body, comm=comm, name=name, grid=(s // tm,),
        in_specs=[row(f), row(f), _resident(wg.shape), _resident(wu.shape), row(d), _whole((1, d)), row(d)],
        out_specs=[row(d), row(d), _whole((1, d))],
        out_shape=[jax.ShapeDtypeStruct((s, d), F32), jax.ShapeDtypeStruct((s, d), BF16), jax.ShapeDtypeStruct((1, d), F32)],
        compiler_params=_params("arbitrary"),
    )(dg, dup, wg, wu, h, gain, dh)


def _odd_bwd(dhb, dh, wout, pre, gate, ws, wst, sgu, win, h, gmix, tm, comm=None):
    s, d = h.shape
    e = win.shape[1]
    cw = e // 2
    heads = ws.shape[0]
    ni = s // tm

    def body(dhb_ref, dh_ref, wout_ref, pre_ref, gate_ref, ws_ref, wst_ref, sgu_ref, win_ref, h_ref, gmix_ref,
             dpre_ref, dho_ref, dhob_ref, dgain_ref, dsgu_ref, dws_ref, db_ref, vbuf, gacc):
        i = pl.program_id(0)
        first = i == 0
        dmixo = _nt(dhb_ref[...], wout_ref[...])
        pre = pre_ref[...].astype(F32)
        p = _gelu(pre)
        u, v = p[:, :cw], p[:, cw:]
        sgu = sgu_ref[...]
        rv = lax.rsqrt(jnp.mean(v * v, axis=-1, keepdims=True) + EPS)
        vh = v * rv
        vn = (vh * sgu).astype(BF16)
        du = dmixo * gate_ref[...].astype(F32)
        dgate = dmixo * u
        dgate_b = dgate.astype(BF16)
        gsum = dgate[0:CHUNK, :]
        for n in range(1, tm // CHUNK):
            gsum = gsum + dgate[n * CHUNK:(n + 1) * CHUNK, :]
        _acc(gacc, gsum, first)
        for hd in range(heads):
            cols = slice(hd * CHUNK, (hd + 1) * CHUNK)
            dws = None
            for n in range(tm // CHUNK):
                rows = slice(n * CHUNK, (n + 1) * CHUNK)
                vbuf[rows, cols] = _nn(wst_ref[hd], dgate_b[rows, cols])
                part = _nt(dgate_b[rows, cols], vn[rows, cols])
                dws = part if dws is None else dws + part
            _acc(dws_ref.at[hd], dws, first)
        dvn = vbuf[...]
        _acc(dsgu_ref, jnp.sum(dvn * vh, axis=0, keepdims=True), first)
        dvh = dvn * sgu
        dv = rv * (dvh - vh * jnp.mean(dvh * vh, axis=-1, keepdims=True))
        dpre = (jnp.concatenate([du, dv], axis=1) * _gelu_grad(pre)).astype(BF16)
        dpre_ref[...] = dpre
        dx, dgain = _rms_bwd(_nt(dpre, win_ref[...]), h_ref[...], gmix_ref[...])
        dho = dh_ref[...] + dx
        dho_ref[...] = dho
        dhob_ref[...] = dho.astype(BF16)
        _acc(dgain_ref, dgain, first)

        @pl.when(i == ni - 1)
        def _():
            ones = jnp.ones((8, CHUNK), F32)
            for hd in range(heads):
                tot = lax.dot_general(ones, gacc[:, hd * CHUNK:(hd + 1) * CHUNK], (((1,), (1,)), ((), ())),
                                      preferred_element_type=F32, precision=lax.Precision.HIGHEST)
                db_ref[hd:hd + 1, :] = tot[0:1, :]

    row = lambda c: pl.BlockSpec((tm, c), lambda i: (i, 0))
    return _pcall(
        body, comm=comm, name="odd_bwd", grid=(ni,),
        in_specs=[row(d), row(d), _whole(wout.shape), row(e), row(cw), _whole(ws.shape), _whole(wst.shape), _whole(sgu.shape),
                  _whole(win.shape), row(d), _whole((1, d))],
        out_specs=[row(e), row(d), row(d), _whole((1, d)), _whole((1, cw)), _whole(ws.shape), _whole((heads, CHUNK))],
        out_shape=[jax.ShapeDtypeStruct((s, e), BF16), jax.ShapeDtypeStruct((s, d), F32), jax.ShapeDtypeStruct((s, d), BF16),
                   jax.ShapeDtypeStruct((1, d), F32), jax.ShapeDtypeStruct((1, cw), F32), jax.ShapeDtypeStruct(ws.shape, F32),
                   jax.ShapeDtypeStruct((heads, CHUNK), F32)],
        scratch_shapes=[pltpu.VMEM((tm, cw), F32), pltpu.VMEM((CHUNK, cw), F32)],
        compiler_params=_params("arbitrary"),
    )(dhb, dh, wout, pre, gate, ws, wst, sgu, win, h, gmix)


def _even_bwd(dhb, dh, wout, proj, cq, pooled, conva, wpool, wpoolt, pscale, win, x, gmix, tm, comm=None):
    s, d = x.shape
    e = win.shape[1]
    aw = e // 4
    ni = s // tm

    def body(dhb_ref, dh_ref, wout_ref, proj_ref, cq_ref, pooled_ref, ca_ref, wp_ref, wpt_ref, ps_ref, win_ref, x_ref, gmix_ref,
             dproj_ref, dx_ref, dgain_ref, dca_ref, dwp_ref, dps_ref, cbuf, ebuf):
        i = pl.program_id(0)
        first = i == 0

        @pl.when(first)
        def _():
            cbuf[tm:tm + HALO, :] = jnp.zeros((HALO, aw), F32)
            ebuf[tm:tm + HALO, :] = jnp.zeros((HALO, aw), F32)

        dmix = _nt(dhb_ref[...], wout_ref[...])
        dy_a, dy_b = dmix[:, :aw], dmix[:, aw:]
        proj = proj_ref[...].astype(F32)
        a_b, a_c, a_v = (proj[:, k * aw:(k + 1) * aw] for k in range(3))
        da_b = dy_a * cq_ref[...].astype(F32)
        dcq = dy_a * a_b
        cbuf[0:tm, :] = dcq
        cext = cbuf[...]
        s1 = _shift(cext, 1, False)[0:tm, :]
        s2 = _shift(cext, 2, False)[0:tm, :]
        q = a_c * a_v
        for k, shifted in enumerate((s2, s1, dcq)):
            _acc(dca_ref.at[k:k + 1, :], jnp.sum(shifted * q, axis=0, keepdims=True), first)
        dq = ca_ref[2:3, :] * dcq + ca_ref[1:2, :] * s1 + ca_ref[0:1, :] * s2
        da_c = dq * a_v
        da_v = dq * a_c
        dps, dpool = [], []
        for g, w in enumerate(WINDOWS):
            cols = slice(g * LANE, (g + 1) * LANE)
            pooled = pooled_ref[:, cols]
            mixed = _nn(pooled, wp_ref[g])
            dps.append(jnp.sum(dy_b[:, cols] * mixed, axis=0, keepdims=True))
            dmixed = (dy_b[:, cols] * ps_ref[:, cols]).astype(BF16)
            _acc(dwp_ref.at[g], _tn(pooled, dmixed), first)
            dp = _nn(dmixed, wpt_ref[g])
            dpool.append(dp)
            ebuf[0:tm, cols] = dp / _counts((ni - 1 - i) * tm, tm, w)
        _acc(dps_ref, jnp.concatenate(dps, axis=1), first)
        eext = ebuf[...]
        dzs = []
        for g, w in enumerate(WINDOWS):
            cols = slice(g * LANE, (g + 1) * LANE)
            dzs.append(_window_sum(eext[:, cols], w, False)[0:tm, :] - dpool[g])
        dproj = jnp.concatenate([da_b, da_c, da_v] + dzs, axis=1).astype(BF16)
        dproj_ref[...] = dproj
        dx, dgain = _rms_bwd(_nt(dproj, win_ref[...]), x_ref[...], gmix_ref[...])
        dx_ref[...] = dh_ref[...] + dx
        _acc(dgain_ref, dgain, first)
        cbuf[tm:tm + HALO, :] = cbuf[0:HALO, :]
        ebuf[tm:tm + HALO, :] = ebuf[0:HALO, :]

    row = lambda c: pl.BlockSpec((tm, c), lambda i: (ni - 1 - i, 0))
    return _pcall(
        body, comm=comm, name="even_bwd", grid=(ni,),
        in_specs=[row(d), row(d), _whole(wout.shape), row(e), row(aw), row(aw), _whole(conva.shape), _whole(wpool.shape),
                  _whole(wpoolt.shape), _whole(pscale.shape), _whole(win.shape), row(d), _whole((1, d))],
        out_specs=[row(e), row(d), _whole((1, d)), _whole(conva.shape), _whole(wpool.shape), _whole(pscale.shape)],
        out_shape=[jax.ShapeDtypeStruct((s, e), BF16), jax.ShapeDtypeStruct((s, d), F32), jax.ShapeDtypeStruct((1, d), F32),
                   jax.ShapeDtypeStruct(conva.shape, F32), jax.ShapeDtypeStruct(wpool.shape, F32),
                   jax.ShapeDtypeStruct(pscale.shape, F32)],
        scratch_shapes=[pltpu.VMEM((tm + HALO, aw), F32), pltpu.VMEM((tm + HALO, aw), F32)],
        compiler_params=_params("arbitrary"),
    )(dhb, dh, wout, proj, cq, pooled, conva, wpool, wpoolt, pscale, win, x, gmix)


def _wgrad(a, b, tk, ts, name, comm=None):
    s, ka = a.shape
    nb = b.shape[1]
    nt = s // ts

    def body(a_ref, b_ref, o_ref, acc):
        t = pl.program_id(1)
        _acc(acc, _tn(a_ref[...], b_ref[...]), t == 0)

        @pl.when(t == nt - 1)
        def _():
            o_ref[...] = acc[...].astype(BF16)

    return _pcall(
        body, comm=comm, name=name, grid=(ka // tk, nt),
        in_specs=[pl.BlockSpec((ts, tk), lambda k, t: (t, k)),
                  _resident((s, nb)) if nt == 1 else pl.BlockSpec((ts, nb), lambda k, t: (t, 0))],
        out_specs=pl.BlockSpec((tk, nb), lambda k, t: (k, 0)),
        out_shape=jax.ShapeDtypeStruct((ka, nb), BF16),
        scratch_shapes=[pltpu.VMEM((tk, nb), F32)],
        compiler_params=_params("arbitrary", "arbitrary"),
    )(a, b)


def _adamw(parts, w, m, v, tr, name, layer=None, into=None, carried=()):
    r, c = w.shape[-2:]
    n, rp, cp = parts.shape
    assert rp >= r and r % tr == 0
    prev, carried = ([] if into is None else list(into)), list(carried)

    def body(p_ref, w_ref, m_ref, v_ref, *rest):
        g_ref, d_ref, mo_ref, vo_ref = rest[len(prev) + len(carried):len(prev) + len(carried) + 4]
        g = p_ref[0, 0:tr, 0:c].astype(F32)
        for j in range(1, n):
            g = g + p_ref[j, 0:tr, 0:c].astype(F32)
        g_ref[...] = g
        mn = ADAM_B1 * m_ref[...] + (1.0 - ADAM_B1) * g
        vn = ADAM_B2 * v_ref[...] + (1.0 - ADAM_B2) * (g * g)
        mo_ref[...] = mn
        vo_ref[...] = vn
        m_hat = mn / (1.0 - ADAM_B1 ** ADAM_STEP)
        v_hat = vn / (1.0 - ADAM_B2 ** ADAM_STEP)
        d_ref[...] = -ADAM_LR * (m_hat / (jnp.sqrt(v_hat) + ADAM_EPS) + ADAM_WD * w_ref[...])

    if layer is None:
        row = pl.BlockSpec((tr, c), lambda i: (i, 0))
    else:
        row = pl.BlockSpec((None, tr, c), lambda i: (layer, i, 0))
    out = jax.ShapeDtypeStruct(w.shape, F32)
    untouched = pl.BlockSpec(memory_space=pl.ANY)
    aliases = {4 + k: k for k in range(len(prev))}
    aliases.update({4 + len(prev) + k: 4 + k for k in range(len(carried))})
    res = _pcall(
        body, name=name, grid=(r // tr,),
        in_specs=[pl.BlockSpec((n, tr, cp), lambda i: (0, i, 0)), row, row, row] + [untouched] * (len(prev) + len(carried)),
        out_specs=[row, row, row, row] + [untouched] * len(carried),
        out_shape=[out, out, out, out] + [jax.ShapeDtypeStruct(a.shape, a.dtype) for a in carried],
        input_output_aliases=aliases,
        compiler_params=_params("arbitrary"),
    )(parts, w, m, v, *prev, *carried)
    return (res[:4], res[4:]) if carried else res


def _pair_add(grad, other, spec, core, name):
    axis, width = spec
    slot = other.shape[1:]

    def body(core_ref, g_ref, o_ref, out_ref):
        out_ref[...] = (g_ref[...].astype(F32) + o_ref[...].astype(F32)).astype(BF16)

    if axis == 0:
        gspec = pl.BlockSpec(slot, lambda q, core_ref: (2 * q + core_ref[0], 0))
    else:
        gspec = pl.BlockSpec(slot, lambda q, core_ref: (0, 2 * q + core_ref[0]))
    per_chip = pl.BlockSpec((None,) + slot, lambda q, core_ref: (q, 0, 0))
    return _pcall(
        body, name=name,
        grid_spec=pltpu.PrefetchScalarGridSpec(num_scalar_prefetch=1, grid=(N_CHIP,), in_specs=[gspec, per_chip], out_specs=per_chip),
        out_shape=jax.ShapeDtypeStruct(other.shape, BF16),
        compiler_params=_params("arbitrary"),
    )(core, grad, other)


def _hbm():
    return pl.BlockSpec(memory_space=pltpu.HBM)


def _window(ref, spec, j):
    axis, width = spec
    start = pl.multiple_of(j * width, width)
    return ref.at[(slice(None),) * axis + (pl.ds(start, width),)]


def _here():
    return lax.axis_index("x"), lax.axis_index("y"), lax.axis_index("c")


class _Gather:
    def __init__(self, shards, specs, fulls):
        n = len(shards)
        self.ins, self.specs, self.out_shape = list(shards), list(specs), list(fulls)
        self.sems = [pltpu.SemaphoreType.DMA((7 * n,)), pltpu.SemaphoreType.DMA((7 * n,)), pltpu.SemaphoreType.DMA((n,))]

    def _plan(self, ins, outs, sems):
        send_sems, recv_sems, local_sems = sems
        x, y, c = _here()
        me, sibling = (x, y, c), (x, y, 1 - c)
        chips = [(1 - x, y), (x, 1 - y), (1 - x, 1 - y)]

        def slot(t, dev):
            return _window(outs[t], self.specs[t], 4 * dev[0] + 2 * dev[1] + dev[2])

        def copy(t, k, block, to, src=None):
            return pltpu.make_async_remote_copy(
                src_ref=slot(t, block) if src is None else src, dst_ref=slot(t, block),
                send_sem=send_sems.at[7 * t + k], recv_sem=recv_sems.at[7 * t + k], device_id=to, device_id_type=MESH)

        plan = []
        for t in range(len(ins)):
            plan.append(dict(
                mine=pltpu.make_async_copy(ins[t], slot(t, me), local_sems.at[t]),
                first=[copy(t, 0, me, sibling, src=ins[t])] + [copy(t, 1 + j, me, (*q, c), src=ins[t]) for j, q in enumerate(chips)],
                over_ici=[copy(t, 1 + j, (*q, c), me) for j, q in enumerate(chips)],
                passed=[copy(t, 4 + j, (*q, c), sibling) for j, q in enumerate(chips)],
                from_sibling=[copy(t, 0, sibling, me)] + [copy(t, 4 + j, (*q, 1 - c), me) for j, q in enumerate(chips)]))
        return plan

    def start(self, ins, outs, sems):
        for p in self._plan(ins, outs, sems):
            p["mine"].start()
            for cp in p["first"]:
                cp.start()

    def middle(self, ins, outs, sems):
        for p in self._plan(ins, outs, sems):
            for arrived, onward in zip(p["over_ici"], p["passed"]):
                arrived.wait_recv()
                onward.start()

    def finish(self, ins, outs, sems):
        plan = self._plan(ins, outs, sems)
        for p in plan:
            for cp in p["from_sibling"]:
                cp.wait_recv()
        for p in plan:
            for cp in p["first"] + p["passed"]:
                cp.wait_send()
            p["mine"].wait()


class _PairExchange:
    def __init__(self, grads, specs):
        n = len(grads)
        self.ins, self.specs = list(grads), list(specs)
        self.out_shape = [jax.ShapeDtypeStruct((N_CHIP,) + a.shape[:sp[0]] + (sp[1],) + a.shape[sp[0] + 1:], a.dtype)
                          for a, sp in zip(grads, specs)]
        self.sems = [pltpu.SemaphoreType.DMA((n,)), pltpu.SemaphoreType.DMA((n,))]

    def start(self, ins, outs, sems):
        send_sems, recv_sems = sems
        x, y, c = _here()
        for t in range(len(ins)):
            for q in range(N_CHIP):
                pltpu.make_async_remote_copy(
                    src_ref=_window(ins[t], self.specs[t], 2 * q + (1 - c)), dst_ref=outs[t].at[q],
                    send_sem=send_sems.at[t], recv_sem=recv_sems.at[t], device_id=(x, y, 1 - c), device_id_type=MESH).start()

    def middle(self, ins, outs, sems):
        pass

    def finish(self, ins, outs, sems):
        send_sems, recv_sems = sems
        x, y, c = _here()
        for t in range(len(ins)):
            every = pltpu.make_async_remote_copy(src_ref=outs[t], dst_ref=outs[t], send_sem=send_sems.at[t],
                                                 recv_sem=recv_sems.at[t], device_id=(x, y, 1 - c), device_id_type=MESH)
            every.wait_send()
            every.wait_recv()


class _ChipExchange:
    def __init__(self, pairs, slotted=(), whole=()):
        self.ins = list(pairs) + list(slotted) + list(whole)
        self.npair, self.nslot = len(pairs), len(pairs) + len(slotted)
        n = len(self.ins)
        self.out_shape = ([jax.ShapeDtypeStruct(a.shape, a.dtype) for a in list(pairs) + list(slotted)]
                          + [jax.ShapeDtypeStruct((N_DEV,) + a.shape, a.dtype) for a in whole])
        self.sems = [pltpu.SemaphoreType.DMA((7 * n,)), pltpu.SemaphoreType.DMA((7 * n,)), pltpu.SemaphoreType.DMA((n,))]

    def _plan(self, ins, outs, sems):
        send_sems, recv_sems, local_sems = sems
        npair, nslot = self.npair, self.nslot
        x, y, c = _here()
        me, chip = 4 * x + 2 * y + c, 2 * x + y
        chips = [(1 - x, y), (x, 1 - y), (1 - x, 1 - y)]
        peers = [(x, y, 1 - c)] + [(*q, c) for q in chips] + [(*q, 1 - c) for q in chips]

        def index(dev):
            return 4 * dev[0] + 2 * dev[1] + dev[2]

        def copy(t, k, arriving):
            peer = peers[k]
            if t < npair:
                src, mine, theirs = ins[t].at[2 * peer[0] + peer[1]], chip, 2 * peer[0] + peer[1]
            else:
                src, mine, theirs = (ins[t].at[index(peer)] if t < nslot else ins[t]), me, index(peer)
            return pltpu.make_async_remote_copy(
                src_ref=src, dst_ref=outs[t].at[theirs if arriving else mine],
                send_sem=send_sems.at[7 * t + k], recv_sem=recv_sems.at[7 * t + k], device_id=peer, device_id_type=MESH)

        own, sent, arriving = [], [], []
        for t in range(len(ins)):
            fan = range(1, 4) if t < npair else range(7)
            if t < npair:
                own.append(pltpu.make_async_copy(ins[t].at[chip], outs[t].at[chip], local_sems.at[t]))
            else:
                own.append(pltpu.make_async_copy(ins[t].at[me] if t < nslot else ins[t], outs[t].at[me], local_sems.at[t]))
            sent += [copy(t, k, False) for k in fan]
            arriving += [copy(t, k, True) for k in fan]
        return own, sent, arriving

    def start(self, ins, outs, sems):
        own, sent, _ = self._plan(ins, outs, sems)
        for cp in own + sent:
            cp.start()

    def middle(self, ins, outs, sems):
        pass

    def finish(self, ins, outs, sems):
        own, sent, arriving = self._plan(ins, outs, sems)
        for cp in arriving:
            cp.wait_recv()
        for cp in sent:
            cp.wait_send()
        for cp in own:
            cp.wait()


class _Jobs:
    def __init__(self, *jobs):
        self.jobs = jobs
        self.ins = [a for j in jobs for a in j.ins]
        self.out_shape = [a for j in jobs for a in j.out_shape]
        self.sems = [a for j in jobs for a in j.sems]

    def _split(self, ins, outs, sems):
        i = o = s = 0
        for j in self.jobs:
            yield j, ins[i:i + len(j.ins)], outs[o:o + len(j.out_shape)], sems[s:s + len(j.sems)]
            i, o, s = i + len(j.ins), o + len(j.out_shape), s + len(j.sems)

    def start(self, ins, outs, sems):
        for j, a, b, c in self._split(ins, outs, sems):
            j.start(a, b, c)

    def middle(self, ins, outs, sems):
        for j, a, b, c in self._split(ins, outs, sems):
            j.middle(a, b, c)

    def finish(self, ins, outs, sems):
        for j, a, b, c in self._split(ins, outs, sems):
            j.finish(a, b, c)

    def results(self, outs):
        return [b for _, _, b, _ in self._split((), outs, ())]


def _alone(job, name):
    return _pcall(lambda: None, comm=job, name=name, in_specs=[], out_specs=[], out_shape=[])()[1]


SMALL_ROWS = 24
REP_COLS = 1024


def _pad_to(a, rows, cols):
    return jnp.pad(a, ((0, rows - a.shape[0]), (0, cols - a.shape[1])))


def _pack_small(conv_a, sgu_norm, conv_ffn, cols):
    return jnp.concatenate([_pad_to(conv_a, 8, cols), _pad_to(sgu_norm, 8, cols),
                            _pad_to(conv_ffn.reshape(-1, conv_ffn.shape[-1]), 8, cols)], axis=0)


def _unpack_small(p, ca_w, sg_w, cf_w):
    return p[0:3, 0:ca_w], p[8:9, 0:sg_w], p[16:22, 0:cf_w].reshape(2, 3, cf_w)


def _tile_rows(rows):
    return -(-rows // 8) * 8


def _pack_rows(parts):
    return jnp.concatenate([_pad_to(a, _tile_rows(a.shape[0]), REP_COLS) for a in parts], axis=0)


def _unpack_rows(p, shapes):
    out, r0 = [], 0
    for r, c in shapes:
        out.append(p[r0:r0 + r, 0:c])
        r0 += _tile_rows(r)
    return out


def _rep_late(norm_mix0, norm_ffn0, pool_scale, b_conv0):
    return _pack_rows([norm_mix0, norm_ffn0, pool_scale.reshape(1, -1), _pad_to(b_conv0, 1, 3 * REP_COLS).reshape(3, REP_COLS)])


def _rep_early(norm_mix1, norm_ffn1, final_norm, b_spatial, b_conv1, loss):
    return _pack_rows([norm_mix1, norm_ffn1, final_norm.reshape(1, -1), b_spatial.reshape(1, -1),
                       _pad_to(b_conv1, 1, 3 * REP_COLS).reshape(3, REP_COLS), loss])


def _unpack_rep(late, early, wsp, wp, like):
    f = like["b_conv_ffn"].shape[1]
    nm0, nf0, ps, bc0 = _unpack_rows(late, [(1, REP_COLS), (1, REP_COLS), (1, like["pool_scale"].shape[1]), (3, REP_COLS)])
    nm1, nf1, fin, bs, bc1, _ = _unpack_rows(early, [(1, REP_COLS)] * 4 + [(3, REP_COLS), (1, REP_COLS)])
    return {
        "norm_mix": jnp.concatenate([nm0, nm1]), "norm_ffn": jnp.concatenate([nf0, nf1]), "final_norm": fin[0], "pool_scale": ps,
        "b_spatial": bs.reshape(like["b_spatial"].shape),
        "b_conv_ffn": jnp.concatenate([bc0.reshape(1, -1), bc1.reshape(1, -1)])[:, 0:f],
        "w_pool": wp.reshape(like["w_pool"].shape), "w_spatial": wsp.reshape(like["w_spatial"].shape),
    }


def _pad_slots(a, width, padded):
    a = a.reshape(*a.shape[:-1], N_DEV, width)
    a = jnp.pad(a, ((0, 0),) * (a.ndim - 1) + ((0, padded - width),))
    return a.reshape(*a.shape[:-2], N_DEV * padded)


def _unpad_slots(a, width, padded):
    a = a.reshape(*a.shape[:-1], N_DEV, padded)[..., 0:width]
    return a.reshape(*a.shape[:-2], N_DEV * width)


def kernel(x, norm_mix, norm_ffn, final_norm, w_in_even, conv_a, w_pool, pool_scale, w_out_even, w_in_odd, sgu_norm, w_spatial, b_spatial, w_out_odd, w_ffn_gate, w_ffn_up, conv_ffn, b_conv_ffn, w_ffn_down, loss_target, m_norm_mix, m_norm_ffn, m_final_norm, m_w_in_even, m_conv_a, m_w_pool, m_pool_scale, m_w_out_even, m_w_in_odd, m_sgu_norm, m_w_spatial, m_b_spatial, m_w_out_odd, m_w_ffn_gate, m_w_ffn_up, m_conv_ffn, m_b_conv_ffn, m_w_ffn_down, v_norm_mix, v_norm_ffn, v_final_norm, v_w_in_even, v_conv_a, v_w_pool, v_pool_scale, v_w_out_even, v_w_in_odd, v_sgu_norm, v_w_spatial, v_b_spatial, v_w_out_odd, v_w_ffn_gate, v_w_ffn_up, v_conv_ffn, v_b_conv_ffn, v_w_ffn_down):
    s, d = x.shape[1], x.shape[2]
    x2, target = x[0], loss_target[0]
    tm = min(512, s)
    tm_wide = min(2048, s)
    tk_wide = 1024
    tn_fwd = 512
    row = lambda a: a.reshape(1, -1)
    ein, ro = w_in_even.shape[2], w_out_even.shape[1]
    fs = w_ffn_gate.shape[2]
    fsp = -(-fs // LANE) * LANE
    fp = N_DEV * fsp
    full = lambda shape, dtype=BF16: jax.ShapeDtypeStruct(shape, dtype)

    wg_s = [jnp.pad(w_ffn_gate[l].T, ((0, fsp - fs), (0, 0))).astype(BF16) for l in range(2)]
    wu_s = [jnp.pad(w_ffn_up[l].T, ((0, fsp - fs), (0, 0))).astype(BF16) for l in range(2)]
    wd_s = [jnp.pad(w_ffn_down[l], ((0, fsp - fs), (0, 0))).astype(BF16) for l in range(2)]
    small_s = _pack_small(conv_a[0], sgu_norm, conv_ffn, fsp)[None]
    in_spec, out_spec, gu_spec, down_spec = (1, ein), (0, ro), (0, fsp), (0, fsp)
    full_in, full_out, full_gu, full_down = full((d, N_DEV * ein)), full((N_DEV * ro, d)), full((fp, d)), full((fp, d))
    win_e, wout_e, gsmall = _alone(_Gather([w_in_even[0].astype(BF16), w_out_even[0].astype(BF16), small_s], [in_spec, out_spec, (0, 1)],
                                           [full_in, full_out, full((N_DEV, SMALL_ROWS, fsp), F32)]), "gather_mix0")
    ca_full = jnp.moveaxis(gsmall[:, 0:3, 0:conv_a.shape[2]], 0, 1).reshape(3, -1)
    sgu_full = gsmall[:, 8, 0:sgu_norm.shape[1]].reshape(1, -1)
    cf_full = jnp.moveaxis(gsmall[:, 16:22, :].reshape(N_DEV, 2, 3, fsp), 0, 2).reshape(2, 3, fp)
    cb_full = _pad_slots(b_conv_ffn, fs, fsp)

    tril = jnp.tril(jnp.ones((CHUNK, CHUNK), F32))
    ws_m = w_spatial[0] * tril
    ws_b = ws_m.astype(BF16)
    wst_b = jnp.swapaxes(ws_m, 1, 2).astype(BF16)
    bfull = jnp.repeat(b_spatial[0].T, CHUNK, axis=1)
    wpool_b = w_pool[0].astype(BF16)
    wpoolt_b = jnp.swapaxes(w_pool[0], 1, 2).astype(BF16)

    (xn0, proj0, cq0, pooled0, mix0, h1, hn0), (wg0,) = _even_fwd(
        x2, norm_mix[0:1], norm_ffn[0:1], win_e, ca_full, wpool_b, pool_scale, wout_e, tm,
        comm=_Gather([wg_s[0]], [gu_spec], [full_gu]))
    (g0, gc0), (wu0,) = _ffn_gate(hn0, wg0, cf_full[0], cb_full[0:1], tm_wide, tn_fwd, "ffn_gate_l0", comm=_Gather([wu_s[0]], [gu_spec], [full_gu]))
    (a0, sl0, ud0), (wd0,) = _ffn_up(hn0, wu0, gc0, tm_wide, tn_fwd, "ffn_up_l0", comm=_Gather([wd_s[0]], [down_spec], [full_down]))
    (h2, xn1), (win_o, wout_o) = _ffn_fwd2(a0, wd0, h1, norm_mix[1:2], tm, "ffn_fwd2_l0", comm=_Gather(
        [w_in_odd[0].astype(BF16), w_out_odd[0].astype(BF16)], [in_spec, out_spec], [full_in, full_out]))
    (pre1, gate1, mixo1, h3, hn1), (wg1,) = _odd_fwd(xn1, h2, win_o, sgu_full, ws_b, bfull, wout_o, norm_ffn[1:2], tm,
                                                    comm=_Gather([wg_s[1]], [gu_spec], [full_gu]))
    (g1, gc1), (wu1,) = _ffn_gate(hn1, wg1, cf_full[1], cb_full[1:2], tm_wide, tn_fwd, "ffn_gate_l1", comm=_Gather([wu_s[1]], [gu_spec], [full_gu]))
    (a1, sl1, ud1), (wd1,) = _ffn_up(hn1, wu1, gc1, tm_wide, tn_fwd, "ffn_up_l1", comm=_Gather([wd_s[1]], [down_spec], [full_down]))
    dh4, dh4b, d_final, lossvec = _ffn_fwd2_loss(a1, wd1, h3, row(final_norm), target, tm)

    core = lax.axis_index("c").astype(jnp.int32).reshape(1)
    ts = min(1024, s)
    gd1 = _wgrad(a1, dh4b, tk_wide, s, "wgrad_down_l1")
    (dg1, dup1, dcw1, dcb1), (o_d1,) = _ffn_bwd1(dh4b, g1, sl1, ud1, wd1, cf_full[1], tm_wide, tn_fwd, "ffn_bwd1_l1",
                                                 comm=_PairExchange([gd1], [down_spec]))
    p_d1 = _pair_add(gd1, o_d1, down_spec, core, "pair_add_down_l1")
    gg1 = _wgrad(dg1, hn1, tk_wide, s, "wgrad_gate_l1")
    gu1 = _wgrad(dup1, hn1, tk_wide, s, "wgrad_up_l1")
    jobs = _Jobs(_ChipExchange([p_d1]), _PairExchange([gg1, gu1], [gu_spec, gu_spec]))
    (dh3, dh3b, d_nffn1), res = _ffn_bwd2(dg1, dup1, wg1, wu1, h3, norm_ffn[1:2], dh4, tm, "ffn_bwd2_l1", comm=jobs)
    (s_d1,), (o_g1, o_u1) = jobs.results(res)
    p_g1 = _pair_add(gg1, o_g1, gu_spec, core, "pair_add_gate_l1")
    p_u1 = _pair_add(gu1, o_u1, gu_spec, core, "pair_add_up_l1")
    (dpre1, dh2, dh2b, d_nmix1, d_sgu, d_ws, d_b), (s_g1, s_u1) = _odd_bwd(
        dh3b, dh3, wout_o, pre1, gate1, ws_b, wst_b, sgu_full, win_o, h2, norm_mix[1:2], tm, comm=_ChipExchange([p_g1, p_u1]))
    gi1 = _wgrad(xn1, dpre1, tk_wide, ts, "wgrad_in_odd")
    go1 = _wgrad(mixo1, dh3b, tk_wide, ts, "wgrad_out_odd")
    d_wsp = (d_ws * tril).reshape(-1, CHUNK).astype(BF16)
    jobs = _Jobs(_PairExchange([gi1, go1], [in_spec, out_spec]), _ChipExchange([], [], [d_wsp]))
    gd0, res = _wgrad(a0, dh2b, tk_wide, s, "wgrad_down_l0", comm=jobs)
    (o_i1, o_o1), (r_wsp,) = jobs.results(res)
    p_i1 = _pair_add(gi1, o_i1, in_spec, core, "pair_add_in_odd")
    p_o1 = _pair_add(go1, o_o1, out_spec, core, "pair_add_out_odd")
    d_early = _rep_early(d_nmix1, d_nffn1, d_final, d_b, _unpad_slots(dcb1, fs, fsp), lossvec)
    jobs = _Jobs(_ChipExchange([p_i1, p_o1], [], [d_early]), _PairExchange([gd0], [down_spec]))
    (dg0, dup0, dcw0, dcb0), res = _ffn_bwd1(dh2b, g0, sl0, ud0, wd0, cf_full[0], tm_wide, tn_fwd, "ffn_bwd1_l0", comm=jobs)
    (s_i1, s_o1, r_early), (o_d0,) = jobs.results(res)
    p_d0 = _pair_add(gd0, o_d0, down_spec, core, "pair_add_down_l0")
    gg0, (s_d0,) = _wgrad(dg0, hn0, tk_wide, s, "wgrad_gate_l0", comm=_ChipExchange([p_d0]))
    gu0, (o_g0,) = _wgrad(dup0, hn0, tk_wide, s, "wgrad_up_l0", comm=_PairExchange([gg0], [gu_spec]))
    p_g0 = _pair_add(gg0, o_g0, gu_spec, core, "pair_add_gate_l0")
    jobs = _Jobs(_ChipExchange([p_g0]), _PairExchange([gu0], [gu_spec]))
    (dh1, dh1b, d_nffn0), res = _ffn_bwd2(dg0, dup0, wg0, wu0, h1, norm_ffn[0:1], dh2, tm, "ffn_bwd2_l0", comm=jobs)
    (s_g0,), (o_u0,) = jobs.results(res)
    p_u0 = _pair_add(gu0, o_u0, gu_spec, core, "pair_add_up_l0")
    go0 = _wgrad(mix0, dh1b, tk_wide, ts, "wgrad_out_even")
    jobs = _Jobs(_ChipExchange([p_u0]), _PairExchange([go0], [out_spec]))
    (dproj0, grad_x, d_nmix0, d_ca, d_wp, d_ps), res = _even_bwd(
        dh1b, dh1, wout_e, proj0, cq0, pooled0, ca_full, wpool_b, wpoolt_b, pool_scale, win_e, x2, norm_mix[0:1], tm, comm=jobs)
    (s_u0,), (o_o0,) = jobs.results(res)
    p_o0 = _pair_add(go0, o_o0, out_spec, core, "pair_add_out_even")
    d_small = jnp.stack([_pack_small(a, b, c, fsp) for a, b, c in zip(
        jnp.moveaxis(d_ca.reshape(3, N_DEV, -1), 1, 0), jnp.moveaxis(d_sgu.reshape(1, N_DEV, -1), 1, 0),
        jnp.moveaxis(jnp.stack([dcw0, dcw1]).reshape(2, 3, N_DEV, fsp), 2, 0))])
    d_late = _rep_late(d_nmix0, d_nffn0, d_ps, _unpad_slots(dcb0, fs, fsp))
    d_wpl = d_wp.reshape(-1, CHUNK).astype(BF16)
    gi0, (s_o0, r_small, r_late, r_wpl) = _wgrad(xn0, dproj0, tk_wide, ts, "wgrad_in_even",
                                                 comm=_ChipExchange([p_o0], [d_small], [d_late, d_wpl]))
    (o_i0,) = _alone(_PairExchange([gi0], [in_spec]), "pair_exchange_in_even")
    p_i0 = _pair_add(gi0, o_i0, in_spec, core, "pair_add_in_even")
    (s_i0,) = _alone(_ChipExchange([p_i0]), "chip_exchange_last")
    loss = jnp.sum(r_early[:, r_early.shape[1] - 8, :])

    out = {}
    out["w_in_even"], (grad_x,) = _adamw(s_i0, w_in_even[0], m_w_in_even[0], v_w_in_even[0], 256, "adamw_in_even", carried=[grad_x])
    out["w_out_even"] = _adamw(s_o0, w_out_even[0], m_w_out_even[0], v_w_out_even[0], ro // 2, "adamw_out_even")
    out["w_in_odd"] = _adamw(s_i1, w_in_odd[0], m_w_in_odd[0], v_w_in_odd[0], 256, "adamw_in_odd")
    out["w_out_odd"] = _adamw(s_o1, w_out_odd[0], m_w_out_odd[0], v_w_out_odd[0], ro // 2, "adamw_out_odd")
    tp = lambda a: jnp.swapaxes(a, 1, 2)
    for nm, s1, s0, w, m, v, back in (("w_ffn_gate", s_g1, s_g0, tp(w_ffn_gate), tp(m_w_ffn_gate), tp(v_w_ffn_gate), tp),
                                      ("w_ffn_up", s_u1, s_u0, tp(w_ffn_up), tp(m_w_ffn_up), tp(v_w_ffn_up), tp),
                                      ("w_ffn_down", s_d1, s_d0, w_ffn_down, m_w_ffn_down, v_w_ffn_down, lambda a: a)):
        l1 = _adamw(s1, w, m, v, fs // 2, "adamw_%s_l1" % nm, layer=1)
        out[nm] = [back(a) for a in _adamw(s0, w, m, v, fs // 2, "adamw_%s_l0" % nm, layer=0, into=l1)]
    small = _adamw(r_small, _pack_small(conv_a[0], sgu_norm, conv_ffn, fsp), _pack_small(m_conv_a[0], m_sgu_norm, m_conv_ffn, fsp),
                   _pack_small(v_conv_a[0], v_sgu_norm, v_conv_ffn, fsp), SMALL_ROWS, "adamw_small")
    no_loss = jnp.zeros((1, REP_COLS), F32)
    early = _adamw(r_early, *[_rep_early(nm[1:2], nf[1:2], fn, bs, bc[1:2], no_loss) for nm, nf, fn, bs, bc in (
        (norm_mix, norm_ffn, final_norm, b_spatial, b_conv_ffn), (m_norm_mix, m_norm_ffn, m_final_norm, m_b_spatial, m_b_conv_ffn),
        (v_norm_mix, v_norm_ffn, v_final_norm, v_b_spatial, v_b_conv_ffn))], r_early.shape[1], "adamw_replicated_early")
    wsp = _adamw(r_wsp, w_spatial.reshape(-1, CHUNK), m_w_spatial.reshape(-1, CHUNK), v_w_spatial.reshape(-1, CHUNK),
                 r_wsp.shape[1], "adamw_w_spatial")
    late = _adamw(r_late, *[_rep_late(nm[0:1], nf[0:1], ps, bc[0:1]) for nm, nf, ps, bc in (
        (norm_mix, norm_ffn, pool_scale, b_conv_ffn), (m_norm_mix, m_norm_ffn, m_pool_scale, m_b_conv_ffn),
        (v_norm_mix, v_norm_ffn, v_pool_scale, v_b_conv_ffn))], r_late.shape[1], "adamw_replicated_late")
    wpl = _adamw(r_wpl, w_pool.reshape(-1, CHUNK), m_w_pool.reshape(-1, CHUNK), v_w_pool.reshape(-1, CHUNK),
                 r_wpl.shape[1], "adamw_w_pool")

    names = ["norm_mix", "norm_ffn", "final_norm", "w_in_even", "conv_a", "w_pool", "pool_scale", "w_out_even", "w_in_odd", "sgu_norm",
             "w_spatial", "b_spatial", "w_out_odd", "w_ffn_gate", "w_ffn_up", "conv_ffn", "b_conv_ffn", "w_ffn_down"]
    like = {"norm_mix": norm_mix, "norm_ffn": norm_ffn, "final_norm": final_norm, "w_in_even": w_in_even, "conv_a": conv_a,
            "w_pool": w_pool, "pool_scale": pool_scale, "w_out_even": w_out_even, "w_in_odd": w_in_odd, "sgu_norm": sgu_norm,
            "w_spatial": w_spatial, "b_spatial": b_spatial, "w_out_odd": w_out_odd, "w_ffn_gate": w_ffn_gate, "w_ffn_up": w_ffn_up,
            "conv_ffn": conv_ffn, "b_conv_ffn": b_conv_ffn, "w_ffn_down": w_ffn_down}
    groups = []
    for k in range(4):
        ca_k, sg_k, cf_k = _unpack_small(small[k], conv_a.shape[2], sgu_norm.shape[1], conv_ffn.shape[2])
        vals = dict(_unpack_rep(late[k], early[k], wsp[k], wpl[k], like))
        vals.update(conv_a=ca_k, sgu_norm=sg_k, conv_ffn=cf_k)
        for nm in ("w_in_even", "w_in_odd", "w_out_even", "w_out_odd", "w_ffn_gate", "w_ffn_up", "w_ffn_down"):
            vals[nm] = out[nm][k]
        groups.append([vals[nm].reshape(like[nm].shape) for nm in names])
    return (loss, grad_x[None], *groups[0], *groups[1], *groups[2], *groups[3])
```

```python
import functools

import jax
import jax.numpy as jnp
from jax import lax
from jax.experimental import pallas as pl
from jax.experimental.pallas import tpu as pltpu

F32, BF16 = jnp.float32, jnp.bfloat16
EPS = 1e-6
WINDOWS = (2, 4, 8, 16)
HALO = 16
CHUNK = 128
N_DEV = 8
N_CHIP = 4
MESH = pl.DeviceIdType.MESH
VMEM_LIMIT = 56 * 2**20
LANE = 128
ADAM_LR, ADAM_B1, ADAM_B2, ADAM_EPS, ADAM_WD, ADAM_STEP = 0.001, 0.9, 0.999, 1e-08, 0.01, 10
SUB_ROWS = 256
LATE_NUM, LATE_DEN = 7, 8
INV_SQRT2 = 0.7071067811865476
INV_SQRT2PI = 0.3989422804014327


def _pcall(body, comm=None, **kw):
    if comm is None:
        return pl.pallas_call(body, **kw)
    in_specs, out_specs, out_shape = list(kw.pop("in_specs")), kw.pop("out_specs"), kw.pop("out_shape")
    single = not isinstance(out_shape, (list, tuple))
    out_specs, out_shape = ([out_specs], [out_shape]) if single else (list(out_specs), list(out_shape))
    scratch = list(kw.pop("scratch_shapes", []))
    grid = kw.get("grid", ())
    n_in, n_out, n_scr, c_in, c_out = len(in_specs), len(out_specs), len(scratch), len(comm.ins), len(comm.out_shape)

    def hosted(*refs):
        cuts = [0, n_in, n_in + c_in, n_in + c_in + n_out, n_in + c_in + n_out + c_out, n_in + c_in + n_out + c_out + n_scr, len(refs)]
        ins, cins, outs, couts, scr, sems = (refs[a:b] for a, b in zip(cuts[:-1], cuts[1:]))
        if grid:
            step, steps = 0, 1
            for axis, size in enumerate(grid):
                step, steps = step * size + pl.program_id(axis), steps * size
            pl.when(step == 0)(lambda: comm.start(cins, couts, sems))
            pl.when(step == (steps * LATE_NUM) // LATE_DEN)(lambda: comm.middle(cins, couts, sems))
            body(*ins, *outs, *scr)
            pl.when(step == steps - 1)(lambda: comm.finish(cins, couts, sems))
        else:
            comm.start(cins, couts, sems)
            comm.middle(cins, couts, sems)
            body(*ins, *outs, *scr)
            comm.finish(cins, couts, sems)

    call = pl.pallas_call(hosted, in_specs=in_specs + [_hbm()] * c_in, out_specs=out_specs + [_hbm()] * c_out,
                          out_shape=out_shape + list(comm.out_shape), scratch_shapes=scratch + list(comm.sems), **kw)

    def run(*args):
        res = call(*args, *comm.ins)
        own = res[0] if single else res[:n_out]
        return own, res[n_out:]

    return run


def _params(*sem):
    return pltpu.CompilerParams(dimension_semantics=sem, vmem_limit_bytes=VMEM_LIMIT)


def _whole(shape):
    return pl.BlockSpec(shape, lambda *_: (0,) * len(shape))


def _resident(shape):
    return pl.BlockSpec(shape, lambda *_: (0,) * len(shape), pipeline_mode=pl.Buffered(1))


def _rows(i, tm):
    return pl.ds(pl.multiple_of(i * tm, tm), tm)


def _nn(a, b):
    return jnp.dot(a, b, preferred_element_type=F32)


def _nt(a, b):
    return lax.dot_general(a, b, (((1,), (1,)), ((), ())), preferred_element_type=F32)


def _tn(a, b):
    return lax.dot_general(a, b, (((0,), (0,)), ((), ())), preferred_element_type=F32)


def _rms(x, gain):
    r = lax.rsqrt(jnp.mean(x * x, axis=-1, keepdims=True) + EPS)
    return x * r * gain


def _rms_bwd(dy, x, gain):
    r = lax.rsqrt(jnp.mean(x * x, axis=-1, keepdims=True) + EPS)
    xh = x * r
    dgain = jnp.sum(dy * xh, axis=0, keepdims=True)
    dxh = dy * gain
    dx = r * (dxh - xh * jnp.mean(dxh * xh, axis=-1, keepdims=True))
    return dx, dgain


def _gelu(x):
    return 0.5 * x * (1.0 + lax.erf(x * INV_SQRT2))


def _gelu_grad(x):
    return 0.5 * (1.0 + lax.erf(x * INV_SQRT2)) + x * jnp.exp(-0.5 * x * x) * INV_SQRT2PI


def _acc(ref, val, first):
    @pl.when(first)
    def _():
        ref[...] = val

    @pl.when(jnp.logical_not(first))
    def _():
        ref[...] += val


def _shift(ext, k, back):
    n = ext.shape[0]
    return pltpu.roll(ext, k if back else n - k, axis=0)


def _window_sum(ext, w, back):
    total, step = ext, 1
    while step < w:
        total = total + _shift(total, step, back)
        step *= 2
    return total


def _counts(row0, tm, w):
    pos1 = (row0 + lax.broadcasted_iota(jnp.int32, (tm, 1), 0) + 1).astype(F32)
    return jnp.minimum(pos1, float(w))


def _even_fwd(x, gmix, gffn, win, conva, wpool, pscale, wout, tm, comm=None):
    s, d = x.shape
    e = win.shape[1]
    aw = e // 4

    def body(x_ref, gmix_ref, gffn_ref, win_ref, ca_ref, wp_ref, ps_ref, wout_ref,
             xn_ref, proj_ref, cq_ref, pooled_ref, mix_ref, h_ref, hn_ref, qbuf, zbuf):
        i = pl.program_id(0)

        @pl.when(i == 0)
        def _():
            qbuf[0:HALO, :] = jnp.zeros((HALO, aw), F32)
            zbuf[0:HALO, :] = jnp.zeros((HALO, aw), F32)

        xv = x_ref[...]
        xn = _rms(xv, gmix_ref[...]).astype(BF16)
        xn_ref[...] = xn
        proj = _nn(xn, win_ref[...])
        proj_ref[...] = proj.astype(BF16)
        a_b, a_c, a_v, z = (proj[:, k * aw:(k + 1) * aw] for k in range(4))
        q = a_c * a_v
        qbuf[HALO:HALO + tm, :] = q
        qext = qbuf[...]
        cur = slice(HALO, HALO + tm)
        cq = ca_ref[2:3, :] * q + ca_ref[1:2, :] * _shift(qext, 1, True)[cur, :] + ca_ref[0:1, :] * _shift(qext, 2, True)[cur, :]
        cq_ref[...] = cq.astype(BF16)
        y_a = a_b * cq
        zbuf[HALO:HALO + tm, :] = z
        zext = zbuf[...]
        ys = []
        for g, w in enumerate(WINDOWS):
            cols = slice(g * LANE, (g + 1) * LANE)
            acc = _window_sum(zext[:, cols], w, True)[cur, :]
            pooled = (acc / _counts(i * tm, tm, w) - z[:, cols]).astype(BF16)
            pooled_ref[:, cols] = pooled
            ys.append(_nn(pooled, wp_ref[g]))
        y_b = jnp.concatenate(ys, axis=1) * ps_ref[...]
        mix = jnp.concatenate([y_a, y_b], axis=1).astype(BF16)
        mix_ref[...] = mix
        h = xv + _nn(mix, wout_ref[...])
        h_ref[...] = h
        hn_ref[...] = _rms(h, gffn_ref[...]).astype(BF16)
        qbuf[0:HALO, :] = qbuf[tm:tm + HALO, :]
        zbuf[0:HALO, :] = zbuf[tm:tm + HALO, :]

    row = lambda c: pl.BlockSpec((tm, c), lambda i: (i, 0))
    return _pcall(
        body, comm=comm, name="even_fwd", grid=(s // tm,),
        in_specs=[row(d), _whole((1, d)), _whole((1, d)), _whole(win.shape), _whole(conva.shape), _whole(wpool.shape),
                  _whole(pscale.shape), _whole(wout.shape)],
        out_specs=[row(d), row(e), row(aw), row(aw), row(d), row(d), row(d)],
        out_shape=[jax.ShapeDtypeStruct((s, d), BF16), jax.ShapeDtypeStruct((s, e), BF16), jax.ShapeDtypeStruct((s, aw), BF16),
                   jax.ShapeDtypeStruct((s, aw), BF16), jax.ShapeDtypeStruct((s, d), BF16), jax.ShapeDtypeStruct((s, d), F32),
                   jax.ShapeDtypeStruct((s, d), BF16)],
        scratch_shapes=[pltpu.VMEM((tm + HALO, aw), F32), pltpu.VMEM((tm + HALO, aw), F32)],
        compiler_params=_params("arbitrary"),
    )(x, gmix, gffn, win, conva, wpool, pscale, wout)


def _ffn_gate(hn, wgt, cw, cb, tm, tn, name, comm=None):
    s, d = hn.shape
    f = wgt.shape[0]
    sub = tm

    def body(hn_ref, wg_ref, cw_ref, cb_ref, g_ref, gc_ref, gbuf):
        i = pl.program_id(1)

        @pl.when(i == 0)
        def _():
            gbuf[0:HALO, :] = jnp.zeros((HALO, tn), F32)

        wg = wg_ref[...]
        for c in range(tm // sub):
            rows = pl.ds(c * sub, sub)
            g = _nt(hn_ref[pl.ds(pl.multiple_of(i * tm + c * sub, sub), sub), :], wg)
            g_ref[rows, :] = g.astype(BF16)
            gbuf[pl.ds(HALO + c * sub, sub), :] = g
            ext = gbuf[pl.ds(c * sub, sub + HALO), :]
            gc = (cw_ref[2:3, :] * g + cw_ref[1:2, :] * _shift(ext, 1, True)[HALO:, :]
                  + cw_ref[0:1, :] * _shift(ext, 2, True)[HALO:, :] + cb_ref[...])
            gc_ref[rows, :] = gc.astype(BF16)
        gbuf[0:HALO, :] = gbuf[tm:tm + HALO, :]

    tile = pl.BlockSpec((tm, tn), lambda j, i: (i, j))
    wcol = lambda r: pl.BlockSpec((r, tn), lambda j, i: (0, j))
    out = jax.ShapeDtypeStruct((s, f), BF16)
    return _pcall(
        body, comm=comm, name=name, grid=(f // tn, s // tm),
        in_specs=[_resident((s, d)), pl.BlockSpec((tn, d), lambda j, i: (j, 0)), wcol(3), wcol(1)],
        out_specs=[tile, tile], out_shape=[out, out],
        scratch_shapes=[pltpu.VMEM((tm + HALO, tn), F32)],
        compiler_params=_params("arbitrary", "arbitrary"),
    )(hn, wgt, cw, cb)


def _ffn_up(hn, wut, gc, tm, tn, name, comm=None):
    s, d = hn.shape
    f = wut.shape[0]
    sub = min(SUB_ROWS, tm)

    def body(hn_ref, wu_ref, gc_ref, a_ref, silu_ref, upds_ref):
        wu = wu_ref[...]
        for c in range(tm // sub):
            rows = pl.ds(c * sub, sub)
            up = _nt(hn_ref[pl.ds(pl.multiple_of(pl.program_id(1) * tm + c * sub, sub), sub), :], wu)
            gc = gc_ref[rows, :].astype(F32)
            sg = jax.nn.sigmoid(gc)
            silu = gc * sg
            silu_ref[rows, :] = silu.astype(BF16)
            upds_ref[rows, :] = (up * (sg * (1.0 - silu) + silu)).astype(BF16)
            a_ref[rows, :] = (silu * up).astype(BF16)

    tile = pl.BlockSpec((tm, tn), lambda j, i: (i, j))
    out = jax.ShapeDtypeStruct((s, f), BF16)
    return _pcall(
        body, comm=comm, name=name, grid=(f // tn, s // tm),
        in_specs=[_resident((s, d)), pl.BlockSpec((tn, d), lambda j, i: (j, 0)), tile],
        out_specs=[tile, tile, tile], out_shape=[out, out, out],
        compiler_params=_params("arbitrary", "arbitrary"),
    )(hn, wut, gc)


def _ffn_fwd2(a, wd, h, gain, tm, name, comm=None):
    s, d = h.shape
    f = a.shape[1]

    def body(a_ref, wd_ref, h_ref, gain_ref, ho_ref, hn_ref):
        wd_v, gain = wd_ref[...], gain_ref[...]
        sub = min(SUB_ROWS, tm)
        for c in range(tm // sub):
            rows = pl.ds(c * sub, sub)
            ho = h_ref[rows, :] + _nn(a_ref[rows, :], wd_v)
            ho_ref[rows, :] = ho
            hn_ref[rows, :] = _rms(ho, gain).astype(BF16)

    row = lambda c: pl.BlockSpec((tm, c), lambda i: (i, 0))
    return _pcall(
        body, comm=comm, name=name, grid=(s // tm,),
        in_specs=[row(f), _resident(wd.shape), row(d), _whole((1, d))],
        out_specs=[row(d), row(d)],
        out_shape=[jax.ShapeDtypeStruct((s, d), F32), jax.ShapeDtypeStruct((s, d), BF16)],
        compiler_params=_params("arbitrary"),
    )(a, wd, h, gain)


def _ffn_fwd2_loss(a, wd, h, gain, target, tm):
    s, d = h.shape
    f = a.shape[1]

    def body(a_ref, wd_ref, h_ref, gain_ref, t_ref, dh_ref, dhb_ref, dgain_ref, loss_ref):
        wd_v, gain = wd_ref[...], gain_ref[...]
        sub = min(SUB_ROWS, tm)
        dgain = loss = None
        for c in range(tm // sub):
            rows = pl.ds(c * sub, sub)
            ho = h_ref[rows, :] + _nn(a_ref[rows, :], wd_v)
            err = _rms(ho, gain) - t_ref[rows, :]
            dx, part = _rms_bwd(err * (1.0 / d), ho, gain)
            dh_ref[rows, :] = dx
            dhb_ref[rows, :] = dx.astype(BF16)
            sq = jnp.sum(err * err, axis=0, keepdims=True) * (0.5 / d)
            dgain, loss = (part, sq) if dgain is None else (dgain + part, loss + sq)
        _acc(dgain_ref, dgain, pl.program_id(0) == 0)
        _acc(loss_ref, loss, pl.program_id(0) == 0)

    row = lambda c: pl.BlockSpec((tm, c), lambda i: (i, 0))
    return _pcall(
        body, name="ffn_fwd2_loss", grid=(s // tm,),
        in_specs=[row(f), _resident(wd.shape), row(d), _whole((1, d)), row(d)],
        out_specs=[row(d), row(d), _whole((1, d)), _whole((1, d))],
        out_shape=[jax.ShapeDtypeStruct((s, d), F32), jax.ShapeDtypeStruct((s, d), BF16), jax.ShapeDtypeStruct((1, d), F32),
                   jax.ShapeDtypeStruct((1, d), F32)],
        compiler_params=_params("arbitrary"),
    )(a, wd, h, gain, target)


def _odd_fwd(xn, h, win, sgu, ws, bfull, wout, gffn, tm, comm=None):
    s, d = h.shape
    e = win.shape[1]
    cw = e // 2
    heads = ws.shape[0]

    def body(xn_ref, h_ref, win_ref, sgu_ref, ws_ref, b_ref, wout_ref, gffn_ref,
             pre_ref, gate_ref, mixo_ref, ho_ref, hn_ref, gbuf):
        pre = _nn(xn_ref[...], win_ref[...])
        pre_ref[...] = pre.astype(BF16)
        p = _gelu(pre)
        u, v = p[:, :cw], p[:, cw:]
        vn = _rms(v, sgu_ref[...]).astype(BF16)
        for n in range(tm // CHUNK):
            rows = slice(n * CHUNK, (n + 1) * CHUNK)
            for hd in range(heads):
                cols = slice(hd * CHUNK, (hd + 1) * CHUNK)
                gbuf[rows, cols] = _nn(ws_ref[hd], vn[rows, cols]) + b_ref[:, cols]
        gate = gbuf[...]
        gate_ref[...] = gate.astype(BF16)
        mixo = (u * gate).astype(BF16)
        mixo_ref[...] = mixo
        ho = h_ref[...] + _nn(mixo, wout_ref[...])
        ho_ref[...] = ho
        hn_ref[...] = _rms(ho, gffn_ref[...]).astype(BF16)

    row = lambda c: pl.BlockSpec((tm, c), lambda i: (i, 0))
    return _pcall(
        body, comm=comm, name="odd_fwd", grid=(s // tm,),
        in_specs=[row(d), row(d), _whole(win.shape), _whole(sgu.shape), _whole(ws.shape), _whole(bfull.shape),
                  _whole(wout.shape), _whole((1, d))],
        out_specs=[row(e), row(cw), row(cw), row(d), row(d)],
        out_shape=[jax.ShapeDtypeStruct((s, e), BF16), jax.ShapeDtypeStruct((s, cw), BF16), jax.ShapeDtypeStruct((s, cw), BF16),
                   jax.ShapeDtypeStruct((s, d), F32), jax.ShapeDtypeStruct((s, d), BF16)],
        scratch_shapes=[pltpu.VMEM((tm, cw), F32)],
        compiler_params=_params("arbitrary"),
    )(xn, h, win, sgu, ws, bfull, wout, gffn)


def _ffn_bwd1(dhb, g, silu, upds, wd, cw, tm, tn, name, comm=None):
    s, d = dhb.shape
    f = g.shape[1]
    ni = s // tm

    def body(dh_ref, g_ref, silu_ref, upds_ref, wd_ref, cw_ref, dg_ref, dup_ref, dcw_ref, dcb_ref, ebuf):
        i = pl.program_id(1)

        @pl.when(i == 0)
        def _():
            ebuf[tm:tm + HALO, :] = jnp.zeros((HALO, tn), F32)

        wd_v = wd_ref[...]
        sub = min(SUB_ROWS, tm)
        sums = [None] * 4
        for c in reversed(range(tm // sub)):
            rows = pl.ds(c * sub, sub)
            da = _nt(dh_ref[pl.ds(pl.multiple_of((ni - 1 - i) * tm + c * sub, sub), sub), :], wd_v)
            dup_ref[rows, :] = (da * silu_ref[rows, :].astype(F32)).astype(BF16)
            dgc = da * upds_ref[rows, :].astype(F32)
            ebuf[rows, :] = dgc
            ext = ebuf[pl.ds(c * sub, sub + HALO), :]
            s1 = _shift(ext, 1, False)[0:sub, :]
            s2 = _shift(ext, 2, False)[0:sub, :]
            dg_ref[rows, :] = (cw_ref[2:3, :] * dgc + cw_ref[1:2, :] * s1 + cw_ref[0:1, :] * s2).astype(BF16)
            gv = g_ref[rows, :].astype(F32)
            for k, term in enumerate((s2 * gv, s1 * gv, dgc * gv, dgc)):
                part = jnp.sum(term, axis=0, keepdims=True)
                sums[k] = part if sums[k] is None else sums[k] + part
        for k in range(3):
            _acc(dcw_ref.at[k:k + 1, :], sums[k], i == 0)
        _acc(dcb_ref, sums[3], i == 0)
        ebuf[tm:tm + HALO, :] = ebuf[0:HALO, :]

    tile = pl.BlockSpec((tm, tn), lambda j, i: (ni - 1 - i, j))
    wcol = lambda r: pl.BlockSpec((r, tn), lambda j, i: (0, j))
    out = jax.ShapeDtypeStruct((s, f), BF16)
    return _pcall(
        body, comm=comm, name=name, grid=(f // tn, ni),
        in_specs=[_resident((s, d)), tile, tile, tile,
                  pl.BlockSpec((tn, d), lambda j, i: (j, 0)), wcol(3)],
        out_specs=[tile, tile, wcol(3), wcol(1)],
        out_shape=[out, out, jax.ShapeDtypeStruct((3, f), F32), jax.ShapeDtypeStruct((1, f), F32)],
        scratch_shapes=[pltpu.VMEM((tm + HALO, tn), F32)],
        compiler_params=_params("arbitrary", "arbitrary"),
    )(dhb, g, silu, upds, wd, cw)


def _ffn_bwd2(dg, dup, wg, wu, h, gain, dh, tm, name, comm=None):
    s, d = h.shape
    f = dg.shape[1]

    def body(dg_ref, dup_ref, wg_ref, wu_ref, h_ref, gain_ref, dh_ref, dho_ref, dhb_ref, dgain_ref):
        wg_v, wu_v, gain = wg_ref[...], wu_ref[...], gain_ref[...]
        sub = min(SUB_ROWS, tm)
        dgain = None
        for c in range(tm // sub):
            rows = pl.ds(c * sub, sub)
            dhn = _nn(dg_ref[rows, :], wg_v) + _nn(dup_ref[rows, :], wu_v)
            dx, part = _rms_bwd(dhn, h_ref[rows, :], gain)
            dgain = part if dgain is None else dgain + part
            dho = dh_ref[rows, :] + dx
            dho_ref[rows, :] = dho
            dhb_ref[rows, :] = dho.astype(BF16)
        _acc(dgain_ref, dgain, pl.program_id(0) == 0)

    row = lambda c: pl.BlockSpec((tm, c), lambda i: (i, 0))
    return _pcall(
        body, comm=comm, name=name, grid=(s // tm,),
        in_specs=[row(f), row(f), _resident(wg.shape), _resident(wu.shape), row(d), _whole((1, d)), row(d)],
        out_specs=[row(d), row(d), _whole((1, d))],
        out_shape=[jax.ShapeDtypeStruct((s, d), F32), jax.ShapeDtypeStruct((s, d), BF16), jax.ShapeDtypeStruct((1, d), F32)],
        compiler_params=_params("arbitrary"),
    )(dg, dup, wg, wu, h, gain, dh)


def _odd_bwd(dhb, dh, wout, pre, gate, ws, wst, sgu, win, h, gmix, tm, comm=None):
    s, d = h.shape
    e = win.shape[1]
    cw = e // 2
    heads = ws.shape[0]
    ni = s // tm

    def body(dhb_ref, dh_ref, wout_ref, pre_ref, gate_ref, ws_ref, wst_ref, sgu_ref, win_ref, h_ref, gmix_ref,
             dpre_ref, dho_ref, dhob_ref, dgain_ref, dsgu_ref, dws_ref, db_ref, vbuf, gacc):
        i = pl.program_id(0)
        first = i == 0
        dmixo = _nt(dhb_ref[...], wout_ref[...])
        pre = pre_ref[...].astype(F32)
        p = _gelu(pre)
        u, v = p[:, :cw], p[:, cw:]
        sgu = sgu_ref[...]
        rv = lax.rsqrt(jnp.mean(v * v, axis=-1, keepdims=True) + EPS)
        vh = v * rv
        vn = (vh * sgu).astype(BF16)
        du = dmixo * gate_ref[...].astype(F32)
        dgate = dmixo * u
        dgate_b = dgate.astype(BF16)
        gsum = dgate[0:CHUNK, :]
        for n in range(1, tm // CHUNK):
            gsum = gsum + dgate[n * CHUNK:(n + 1) * CHUNK, :]
        _acc(gacc, gsum, first)
        for hd in range(heads):
            cols = slice(hd * CHUNK, (hd + 1) * CHUNK)
            dws = None
            for n in range(tm // CHUNK):
                rows = slice(n * CHUNK, (n + 1) * CHUNK)
                vbuf[rows, cols] = _nn(wst_ref[hd], dgate_b[rows, cols])
                part = _nt(dgate_b[rows, cols], vn[rows, cols])
                dws = part if dws is None else dws + part
            _acc(dws_ref.at[hd], dws, first)
        dvn = vbuf[...]
        _acc(dsgu_ref, jnp.sum(dvn * vh, axis=0, keepdims=True), first)
        dvh = dvn * sgu
        dv = rv * (dvh - vh * jnp.mean(dvh * vh, axis=-1, keepdims=True))
        dpre = (jnp.concatenate([du, dv], axis=1) * _gelu_grad(pre)).astype(BF16)
        dpre_ref[...] = dpre
        dx, dgain = _rms_bwd(_nt(dpre, win_ref[...]), h_ref[...], gmix_ref[...])
        dho = dh_ref[...] + dx
        dho_ref[...] = dho
        dhob_ref[...] = dho.astype(BF16)
        _acc(dgain_ref, dgain, first)

        @pl.when(i == ni - 1)
        def _():
            ones = jnp.ones((8, CHUNK), F32)
            for hd in range(heads):
                tot = lax.dot_general(ones, gacc[:, hd * CHUNK:(hd + 1) * CHUNK], (((1,), (1,)), ((), ())),
                                      preferred_element_type=F32, precision=lax.Precision.HIGHEST)
                db_ref[hd:hd + 1, :] = tot[0:1, :]

    row = lambda c: pl.BlockSpec((tm, c), lambda i: (i, 0))
    return _pcall(
        body, comm=comm, name="odd_bwd", grid=(ni,),
        in_specs=[row(d), row(d), _whole(wout.shape), row(e), row(cw), _whole(ws.shape), _whole(wst.shape), _whole(sgu.shape),
                  _whole(win.shape), row(d), _whole((1, d))],
        out_specs=[row(e), row(d), row(d), _whole((1, d)), _whole((1, cw)), _whole(ws.shape), _whole((heads, CHUNK))],
        out_shape=[jax.ShapeDtypeStruct((s, e), BF16), jax.ShapeDtypeStruct((s, d), F32), jax.ShapeDtypeStruct((s, d), BF16),
                   jax.ShapeDtypeStruct((1, d), F32), jax.ShapeDtypeStruct((1, cw), F32), jax.ShapeDtypeStruct(ws.shape, F32),
                   jax.ShapeDtypeStruct((heads, CHUNK), F32)],
        scratch_shapes=[pltpu.VMEM((tm, cw), F32), pltpu.VMEM((CHUNK, cw), F32)],
        compiler_params=_params("arbitrary"),
    )(dhb, dh, wout, pre, gate, ws, wst, sgu, win, h, gmix)


def _even_bwd(dhb, dh, wout, proj, cq, pooled, conva, wpool, wpoolt, pscale, win, x, gmix, tm, comm=None):
    s, d = x.shape
    e = win.shape[1]
    aw = e // 4
    ni = s // tm

    def body(dhb_ref, dh_ref, wout_ref, proj_ref, cq_ref, pooled_ref, ca_ref, wp_ref, wpt_ref, ps_ref, win_ref, x_ref, gmix_ref,
             dproj_ref, dx_ref, dgain_ref, dca_ref, dwp_ref, dps_ref, cbuf, ebuf):
        i = pl.program_id(0)
        first = i == 0

        @pl.when(first)
        def _():
            cbuf[tm:tm + HALO, :] = jnp.zeros((HALO, aw), F32)
            ebuf[tm:tm + HALO, :] = jnp.zeros((HALO, aw), F32)

        dmix = _nt(dhb_ref[...], wout_ref[...])
        dy_a, dy_b = dmix[:, :aw], dmix[:, aw:]
        proj = proj_ref[...].astype(F32)
        a_b, a_c, a_v = (proj[:, k * aw:(k + 1) * aw] for k in range(3))
        da_b = dy_a * cq_ref[...].astype(F32)
        dcq = dy_a * a_b
        cbuf[0:tm, :] = dcq
        cext = cbuf[...]
        s1 = _shift(cext, 1, False)[0:tm, :]
        s2 = _shift(cext, 2, False)[0:tm, :]
        q = a_c * a_v
        for k, shifted in enumerate((s2, s1, dcq)):
            _acc(dca_ref.at[k:k + 1, :], jnp.sum(shifted * q, axis=0, keepdims=True), first)
        dq = ca_ref[2:3, :] * dcq + ca_ref[1:2, :] * s1 + ca_ref[0:1, :] * s2
        da_c = dq * a_v
        da_v = dq * a_c
        dps, dpool = [], []
        for g, w in enumerate(WINDOWS):
            cols = slice(g * LANE, (g + 1) * LANE)
            pooled = pooled_ref[:, cols]
            mixed = _nn(pooled, wp_ref[g])
            dps.append(jnp.sum(dy_b[:, cols] * mixed, axis=0, keepdims=True))
            dmixed = (dy_b[:, cols] * ps_ref[:, cols]).astype(BF16)
            _acc(dwp_ref.at[g], _tn(pooled, dmixed), first)
            dp = _nn(dmixed, wpt_ref[g])
            dpool.append(dp)
            ebuf[0:tm, cols] = dp / _counts((ni - 1 - i) * tm, tm, w)
        _acc(dps_ref, jnp.concatenate(dps, axis=1), first)
        eext = ebuf[...]
        dzs = []
        for g, w in enumerate(WINDOWS):
            cols = slice(g * LANE, (g + 1) * LANE)
            dzs.append(_window_sum(eext[:, cols], w, False)[0:tm, :] - dpool[g])
        dproj = jnp.concatenate([da_b, da_c, da_v] + dzs, axis=1).astype(BF16)
        dproj_ref[...] = dproj
        dx, dgain = _rms_bwd(_nt(dproj, win_ref[...]), x_ref[...], gmix_ref[...])
        dx_ref[...] = dh_ref[...] + dx
        _acc(dgain_ref, dgain, first)
        cbuf[tm:tm + HALO, :] = cbuf[0:HALO, :]
        ebuf[tm:tm + HALO, :] = ebuf[0:HALO, :]

    row = lambda c: pl.BlockSpec((tm, c), lambda i: (ni - 1 - i, 0))
    return _pcall(
        body, comm=comm, name="even_bwd", grid=(ni,),
        in_specs=[row(d), row(d), _whole(wout.shape), row(e), row(aw), row(aw), _whole(conva.shape), _whole(wpool.shape),
                  _whole(wpoolt.shape), _whole(pscale.shape), _whole(win.shape), row(d), _whole((1, d))],
        out_specs=[row(e), row(d), _whole((1, d)), _whole(conva.shape), _whole(wpool.shape), _whole(pscale.shape)],
        out_shape=[jax.ShapeDtypeStruct((s, e), BF16), jax.ShapeDtypeStruct((s, d), F32), jax.ShapeDtypeStruct((1, d), F32),
                   jax.ShapeDtypeStruct(conva.shape, F32), jax.ShapeDtypeStruct(wpool.shape, F32),
                   jax.ShapeDtypeStruct(pscale.shape, F32)],
        scratch_shapes=[pltpu.VMEM((tm + HALO, aw), F32), pltpu.VMEM((tm + HALO, aw), F32)],
        compiler_params=_params("arbitrary"),
    )(dhb, dh, wout, proj, cq, pooled, conva, wpool, wpoolt, pscale, win, x, gmix)


def _wgrad(a, b, tk, ts, name, comm=None):
    s, ka = a.shape
    nb = b.shape[1]
    nt = s // ts

    def body(a_ref, b_ref, o_ref, acc):
        t = pl.program_id(1)
        _acc(acc, _tn(a_ref[...], b_ref[...]), t == 0)

        @pl.when(t == nt - 1)
        def _():
            o_ref[...] = acc[...].astype(BF16)

    return _pcall(
        body, comm=comm, name=name, grid=(ka // tk, nt),
        in_specs=[pl.BlockSpec((ts, tk), lambda k, t: (t, k)),
                  _resident((s, nb)) if nt == 1 else pl.BlockSpec((ts, nb), lambda k, t: (t, 0))],
        out_specs=pl.BlockSpec((tk, nb), lambda k, t: (k, 0)),
        out_shape=jax.ShapeDtypeStruct((ka, nb), BF16),
        scratch_shapes=[pltpu.VMEM((tk, nb), F32)],
        compiler_params=_params("arbitrary", "arbitrary"),
    )(a, b)


def _adamw(parts, w, m, v, tr, name, layer=None, into=None, carried=(), comm=None):
    r, c = w.shape[-2:]
    n, rp, cp = parts.shape
    assert rp >= r and r % tr == 0
    prev, carried = ([] if into is None else list(into)), list(carried)

    def body(p_ref, w_ref, m_ref, v_ref, *rest):
        g_ref, d_ref, mo_ref, vo_ref = rest[len(prev) + len(carried):len(prev) + len(carried) + 4]
        g = p_ref[0, 0:tr, 0:c].astype(F32)
        for j in range(1, n):
            g = g + p_ref[j, 0:tr, 0:c].astype(F32)
        g_ref[...] = g
        mn = ADAM_B1 * m_ref[...] + (1.0 - ADAM_B1) * g
        vn = ADAM_B2 * v_ref[...] + (1.0 - ADAM_B2) * (g * g)
        mo_ref[...] = mn
        vo_ref[...] = vn
        m_hat = mn / (1.0 - ADAM_B1 ** ADAM_STEP)
        v_hat = vn / (1.0 - ADAM_B2 ** ADAM_STEP)
        d_ref[...] = -ADAM_LR * (m_hat / (jnp.sqrt(v_hat) + ADAM_EPS) + ADAM_WD * w_ref[...])

    if layer is None:
        row = pl.BlockSpec((tr, c), lambda i: (i, 0))
    else:
        row = pl.BlockSpec((None, tr, c), lambda i: (layer, i, 0))
    out = jax.ShapeDtypeStruct(w.shape, F32)
    untouched = pl.BlockSpec(memory_space=pl.ANY)
    aliases = {4 + k: k for k in range(len(prev))}
    aliases.update({4 + len(prev) + k: 4 + k for k in range(len(carried))})
    res = _pcall(
        body, comm=comm, name=name, grid=(r // tr,),
        in_specs=[pl.BlockSpec((n, tr, cp), lambda i: (0, i, 0)), row, row, row] + [untouched] * (len(prev) + len(carried)),
        out_specs=[row, row, row, row] + [untouched] * len(carried),
        out_shape=[out, out, out, out] + [jax.ShapeDtypeStruct(a.shape, a.dtype) for a in carried],
        input_output_aliases=aliases,
        compiler_params=_params("arbitrary"),
    )(parts, w, m, v, *prev, *carried)
    if comm is not None:
        return res
    return (res[:4], res[4:]) if carried else res


def _pair_add(grad, other, spec, core, name):
    axis, width = spec
    slot = other.shape[1:]

    def body(core_ref, g_ref, o_ref, out_ref):
        out_ref[...] = (g_ref[...].astype(F32) + o_ref[...].astype(F32)).astype(BF16)

    if axis == 0:
        gspec = pl.BlockSpec(slot, lambda q, core_ref: (2 * q + core_ref[0], 0))
    else:
        gspec = pl.BlockSpec(slot, lambda q, core_ref: (0, 2 * q + core_ref[0]))
    per_chip = pl.BlockSpec((None,) + slot, lambda q, core_ref: (q, 0, 0))
    return _pcall(
        body, name=name,
        grid_spec=pltpu.PrefetchScalarGridSpec(num_scalar_prefetch=1, grid=(N_CHIP,), in_specs=[gspec, per_chip], out_specs=per_chip),
        out_shape=jax.ShapeDtypeStruct(other.shape, BF16),
        compiler_params=_params("arbitrary"),
    )(core, grad, other)


def _hbm():
    return pl.BlockSpec(memory_space=pltpu.HBM)


def _window(ref, spec, j):
    axis, width = spec
    start = pl.multiple_of(j * width, width)
    return ref.at[(slice(None),) * axis + (pl.ds(start, width),)]


def _here():
    return lax.axis_index("x"), lax.axis_index("y"), lax.axis_index("c")


class _Gather:
    def __init__(self, shards, specs, fulls):
        n = len(shards)
        self.ins, self.specs, self.out_shape = list(shards), list(specs), list(fulls)
        self.sems = [pltpu.SemaphoreType.DMA((7 * n,)), pltpu.SemaphoreType.DMA((7 * n,)), pltpu.SemaphoreType.DMA((n,))]

    def _plan(self, ins, outs, sems):
        send_sems, recv_sems, local_sems = sems
        x, y, c = _here()
        me, sibling = (x, y, c), (x, y, 1 - c)
        chips = [(1 - x, y), (x, 1 - y), (1 - x, 1 - y)]

        def slot(t, dev):
            return _window(outs[t], self.specs[t], 4 * dev[0] + 2 * dev[1] + dev[2])

        def copy(t, k, block, to, src=None):
            return pltpu.make_async_remote_copy(
                src_ref=slot(t, block) if src is None else src, dst_ref=slot(t, block),
                send_sem=send_sems.at[7 * t + k], recv_sem=recv_sems.at[7 * t + k], device_id=to, device_id_type=MESH)

        plan = []
        for t in range(len(ins)):
            plan.append(dict(
                mine=pltpu.make_async_copy(ins[t], slot(t, me), local_sems.at[t]),
                first=[copy(t, 0, me, sibling, src=ins[t])] + [copy(t, 1 + j, me, (*q, c), src=ins[t]) for j, q in enumerate(chips)],
                over_ici=[copy(t, 1 + j, (*q, c), me) for j, q in enumerate(chips)],
                passed=[copy(t, 4 + j, (*q, c), sibling) for j, q in enumerate(chips)],
                from_sibling=[copy(t, 0, sibling, me)] + [copy(t, 4 + j, (*q, 1 - c), me) for j, q in enumerate(chips)]))
        return plan

    def start(self, ins, outs, sems):
        for p in self._plan(ins, outs, sems):
            p["mine"].start()
            for cp in p["first"]:
                cp.start()

    def middle(self, ins, outs, sems):
        for p in self._plan(ins, outs, sems):
            for arrived, onward in zip(p["over_ici"], p["passed"]):
                arrived.wait_recv()
                onward.start()

    def finish(self, ins, outs, sems):
        plan = self._plan(ins, outs, sems)
        for p in plan:
            for cp in p["from_sibling"]:
                cp.wait_recv()
        for p in plan:
            for cp in p["first"] + p["passed"]:
                cp.wait_send()
            p["mine"].wait()


class _PairExchange:
    def __init__(self, grads, specs):
        n = len(grads)
        self.ins, self.specs = list(grads), list(specs)
        self.out_shape = [jax.ShapeDtypeStruct((N_CHIP,) + a.shape[:sp[0]] + (sp[1],) + a.shape[sp[0] + 1:], a.dtype)
                          for a, sp in zip(grads, specs)]
        self.sems = [pltpu.SemaphoreType.DMA((n,)), pltpu.SemaphoreType.DMA((n,))]

    def start(self, ins, outs, sems):
        send_sems, recv_sems = sems
        x, y, c = _here()
        for t in range(len(ins)):
            for q in range(N_CHIP):
                pltpu.make_async_remote_copy(
                    src_ref=_window(ins[t], self.specs[t], 2 * q + (1 - c)), dst_ref=outs[t].at[q],
                    send_sem=send_sems.at[t], recv_sem=recv_sems.at[t], device_id=(x, y, 1 - c), device_id_type=MESH).start()

    def middle(self, ins, outs, sems):
        pass

    def finish(self, ins, outs, sems):
        send_sems, recv_sems = sems
        x, y, c = _here()
        for t in range(len(ins)):
            every = pltpu.make_async_remote_copy(src_ref=outs[t], dst_ref=outs[t], send_sem=send_sems.at[t],
                                                 recv_sem=recv_sems.at[t], device_id=(x, y, 1 - c), device_id_type=MESH)
            every.wait_send()
            every.wait_recv()


class _ChipExchange:
    def __init__(self, pairs, slotted=(), whole=()):
        self.ins = list(pairs) + list(slotted) + list(whole)
        self.npair, self.nslot = len(pairs), len(pairs) + len(slotted)
        n = len(self.ins)
        self.out_shape = ([jax.ShapeDtypeStruct(a.shape, a.dtype) for a in list(pairs) + list(slotted)]
                          + [jax.ShapeDtypeStruct((N_DEV,) + a.shape, a.dtype) for a in whole])
        self.sems = [pltpu.SemaphoreType.DMA((7 * n,)), pltpu.SemaphoreType.DMA((7 * n,)), pltpu.SemaphoreType.DMA((n,))]

    def _plan(self, ins, outs, sems):
        send_sems, recv_sems, local_sems = sems
        npair, nslot = self.npair, self.nslot
        x, y, c = _here()
        me, chip = 4 * x + 2 * y + c, 2 * x + y
        chips = [(1 - x, y), (x, 1 - y), (1 - x, 1 - y)]
        peers = [(x, y, 1 - c)] + [(*q, c) for q in chips] + [(*q, 1 - c) for q in chips]

        def index(dev):
            return 4 * dev[0] + 2 * dev[1] + dev[2]

        def copy(t, k, arriving):
            peer = peers[k]
            if t < npair:
                src, mine, theirs = ins[t].at[2 * peer[0] + peer[1]], chip, 2 * peer[0] + peer[1]
            else:
                src, mine, theirs = (ins[t].at[index(peer)] if t < nslot else ins[t]), me, index(peer)
            return pltpu.make_async_remote_copy(
                src_ref=src, dst_ref=outs[t].at[theirs if arriving else mine],
                send_sem=send_sems.at[7 * t + k], recv_sem=recv_sems.at[7 * t + k], device_id=peer, device_id_type=MESH)

        own, sent, arriving = [], [], []
        for t in range(len(ins)):
            fan = range(1, 4) if t < npair else range(7)
            if t < npair:
                own.append(pltpu.make_async_copy(ins[t].at[chip], outs[t].at[chip], local_sems.at[t]))
            else:
                own.append(pltpu.make_async_copy(ins[t].at[me] if t < nslot else ins[t], outs[t].at[me], local_sems.at[t]))
            sent += [copy(t, k, False) for k in fan]
            arriving += [copy(t, k, True) for k in fan]
        return own, sent, arriving

    def start(self, ins, outs, sems):
        own, sent, _ = self._plan(ins, outs, sems)
        for cp in own + sent:
            cp.start()

    def middle(self, ins, outs, sems):
        pass

    def finish(self, ins, outs, sems):
        own, sent, arriving = self._plan(ins, outs, sems)
        for cp in arriving:
            cp.wait_recv()
        for cp in sent:
            cp.wait_send()
        for cp in own:
            cp.wait()


class _Jobs:
    def __init__(self, *jobs):
        self.jobs = jobs
        self.ins = [a for j in jobs for a in j.ins]
        self.out_shape = [a for j in jobs for a in j.out_shape]
        self.sems = [a for j in jobs for a in j.sems]

    def _split(self, ins, outs, sems):
        i = o = s = 0
        for j in self.jobs:
            yield j, ins[i:i + len(j.ins)], outs[o:o + len(j.out_shape)], sems[s:s + len(j.sems)]
            i, o, s = i + len(j.ins), o + len(j.out_shape), s + len(j.sems)

    def start(self, ins, outs, sems):
        for j, a, b, c in self._split(ins, outs, sems):
            j.start(a, b, c)

    def middle(self, ins, outs, sems):
        for j, a, b, c in self._split(ins, outs, sems):
            j.middle(a, b, c)

    def finish(self, ins, outs, sems):
        for j, a, b, c in self._split(ins, outs, sems):
            j.finish(a, b, c)

    def results(self, outs):
        return [b for _, _, b, _ in self._split((), outs, ())]


def _alone(job, name):
    return _pcall(lambda: None, comm=job, name=name, in_specs=[], out_specs=[], out_shape=[])()[1]


SMALL_ROWS = 24
REP_COLS = 1024


def _pad_to(a, rows, cols):
    return jnp.pad(a, ((0, rows - a.shape[0]), (0, cols - a.shape[1])))


def _pack_small(conv_a, sgu_norm, conv_ffn, cols):
    return jnp.concatenate([_pad_to(conv_a, 8, cols), _pad_to(sgu_norm, 8, cols),
                            _pad_to(conv_ffn.reshape(-1, conv_ffn.shape[-1]), 8, cols)], axis=0)


def _unpack_small(p, ca_w, sg_w, cf_w):
    return p[0:3, 0:ca_w], p[8:9, 0:sg_w], p[16:22, 0:cf_w].reshape(2, 3, cf_w)


def _tile_rows(rows):
    return -(-rows // 8) * 8


def _pack_rows(parts):
    return jnp.concatenate([_pad_to(a, _tile_rows(a.shape[0]), REP_COLS) for a in parts], axis=0)


def _unpack_rows(p, shapes):
    out, r0 = [], 0
    for r, c in shapes:
        out.append(p[r0:r0 + r, 0:c])
        r0 += _tile_rows(r)
    return out


def _rep_late(norm_mix0, norm_ffn0, pool_scale, b_conv0):
    return _pack_rows([norm_mix0, norm_ffn0, pool_scale.reshape(1, -1), _pad_to(b_conv0, 1, 3 * REP_COLS).reshape(3, REP_COLS)])


def _rep_early(norm_mix1, norm_ffn1, final_norm, b_spatial, b_conv1, loss):
    return _pack_rows([norm_mix1, norm_ffn1, final_norm.reshape(1, -1), b_spatial.reshape(1, -1),
                       _pad_to(b_conv1, 1, 3 * REP_COLS).reshape(3, REP_COLS), loss])


def _unpack_rep(late, early, wsp, wp, like):
    f = like["b_conv_ffn"].shape[1]
    nm0, nf0, ps, bc0 = _unpack_rows(late, [(1, REP_COLS), (1, REP_COLS), (1, like["pool_scale"].shape[1]), (3, REP_COLS)])
    nm1, nf1, fin, bs, bc1, _ = _unpack_rows(early, [(1, REP_COLS)] * 4 + [(3, REP_COLS), (1, REP_COLS)])
    return {
        "norm_mix": jnp.concatenate([nm0, nm1]), "norm_ffn": jnp.concatenate([nf0, nf1]), "final_norm": fin[0], "pool_scale": ps,
        "b_spatial": bs.reshape(like["b_spatial"].shape),
        "b_conv_ffn": jnp.concatenate([bc0.reshape(1, -1), bc1.reshape(1, -1)])[:, 0:f],
        "w_pool": wp.reshape(like["w_pool"].shape), "w_spatial": wsp.reshape(like["w_spatial"].shape),
    }


def _pad_slots(a, width, padded):
    a = a.reshape(*a.shape[:-1], N_DEV, width)
    a = jnp.pad(a, ((0, 0),) * (a.ndim - 1) + ((0, padded - width),))
    return a.reshape(*a.shape[:-2], N_DEV * padded)


def _unpad_slots(a, width, padded):
    a = a.reshape(*a.shape[:-1], N_DEV, padded)[..., 0:width]
    return a.reshape(*a.shape[:-2], N_DEV * width)


def kernel(x, norm_mix, norm_ffn, final_norm, w_in_even, conv_a, w_pool, pool_scale, w_out_even, w_in_odd, sgu_norm, w_spatial, b_spatial, w_out_odd, w_ffn_gate, w_ffn_up, conv_ffn, b_conv_ffn, w_ffn_down, loss_target, m_norm_mix, m_norm_ffn, m_final_norm, m_w_in_even, m_conv_a, m_w_pool, m_pool_scale, m_w_out_even, m_w_in_odd, m_sgu_norm, m_w_spatial, m_b_spatial, m_w_out_odd, m_w_ffn_gate, m_w_ffn_up, m_conv_ffn, m_b_conv_ffn, m_w_ffn_down, v_norm_mix, v_norm_ffn, v_final_norm, v_w_in_even, v_conv_a, v_w_pool, v_pool_scale, v_w_out_even, v_w_in_odd, v_sgu_norm, v_w_spatial, v_b_spatial, v_w_out_odd, v_w_ffn_gate, v_w_ffn_up, v_conv_ffn, v_b_conv_ffn, v_w_ffn_down):
    s, d = x.shape[1], x.shape[2]
    x2, target = x[0], loss_target[0]
    tm = min(512, s)
    tm_wide = min(2048, s)
    tk_wide = 1024
    tn_fwd = 512
    row = lambda a: a.reshape(1, -1)
    ein, ro = w_in_even.shape[2], w_out_even.shape[1]
    fs = w_ffn_gate.shape[2]
    fsp = -(-fs // LANE) * LANE
    fp = N_DEV * fsp
    full = lambda shape, dtype=BF16: jax.ShapeDtypeStruct(shape, dtype)

    wg_s = [jnp.pad(w_ffn_gate[l].T, ((0, fsp - fs), (0, 0))).astype(BF16) for l in range(2)]
    wu_s = [jnp.pad(w_ffn_up[l].T, ((0, fsp - fs), (0, 0))).astype(BF16) for l in range(2)]
    wd_s = [jnp.pad(w_ffn_down[l], ((0, fsp - fs), (0, 0))).astype(BF16) for l in range(2)]
    small_s = _pack_small(conv_a[0], sgu_norm, conv_ffn, fsp)[None]
    in_spec, out_spec, gu_spec, down_spec = (1, ein), (0, ro), (0, fsp), (0, fsp)
    full_in, full_out, full_gu, full_down = full((d, N_DEV * ein)), full((N_DEV * ro, d)), full((fp, d)), full((fp, d))
    win_e, wout_e, gsmall = _alone(_Gather([w_in_even[0].astype(BF16), w_out_even[0].astype(BF16), small_s], [in_spec, out_spec, (0, 1)],
                                           [full_in, full_out, full((N_DEV, SMALL_ROWS, fsp), F32)]), "gather_mix0")
    ca_full = jnp.moveaxis(gsmall[:, 0:3, 0:conv_a.shape[2]], 0, 1).reshape(3, -1)
    sgu_full = gsmall[:, 8, 0:sgu_norm.shape[1]].reshape(1, -1)
    cf_full = jnp.moveaxis(gsmall[:, 16:22, :].reshape(N_DEV, 2, 3, fsp), 0, 2).reshape(2, 3, fp)
    cb_full = _pad_slots(b_conv_ffn, fs, fsp)

    tril = jnp.tril(jnp.ones((CHUNK, CHUNK), F32))
    ws_m = w_spatial[0] * tril
    ws_b = ws_m.astype(BF16)
    wst_b = jnp.swapaxes(ws_m, 1, 2).astype(BF16)
    bfull = jnp.repeat(b_spatial[0].T, CHUNK, axis=1)
    wpool_b = w_pool[0].astype(BF16)
    wpoolt_b = jnp.swapaxes(w_pool[0], 1, 2).astype(BF16)

    (xn0, proj0, cq0, pooled0, mix0, h1, hn0), (wg0,) = _even_fwd(
        x2, norm_mix[0:1], norm_ffn[0:1], win_e, ca_full, wpool_b, pool_scale, wout_e, tm,
        comm=_Gather([wg_s[0]], [gu_spec], [full_gu]))
    (g0, gc0), (wu0,) = _ffn_gate(hn0, wg0, cf_full[0], cb_full[0:1], tm_wide, tn_fwd, "ffn_gate_l0", comm=_Gather([wu_s[0]], [gu_spec], [full_gu]))
    (a0, sl0, ud0), (wd0,) = _ffn_up(hn0, wu0, gc0, tm_wide, tn_fwd, "ffn_up_l0", comm=_Gather([wd_s[0]], [down_spec], [full_down]))
    (h2, xn1), (win_o, wout_o) = _ffn_fwd2(a0, wd0, h1, norm_mix[1:2], tm, "ffn_fwd2_l0", comm=_Gather(
        [w_in_odd[0].astype(BF16), w_out_odd[0].astype(BF16)], [in_spec, out_spec], [full_in, full_out]))
    (pre1, gate1, mixo1, h3, hn1), (wg1,) = _odd_fwd(xn1, h2, win_o, sgu_full, ws_b, bfull, wout_o, norm_ffn[1:2], tm,
                                                    comm=_Gather([wg_s[1]], [gu_spec], [full_gu]))
    (g1, gc1), (wu1,) = _ffn_gate(hn1, wg1, cf_full[1], cb_full[1:2], tm_wide, tn_fwd, "ffn_gate_l1", comm=_Gather([wu_s[1]], [gu_spec], [full_gu]))
    (a1, sl1, ud1), (wd1,) = _ffn_up(hn1, wu1, gc1, tm_wide, tn_fwd, "ffn_up_l1", comm=_Gather([wd_s[1]], [down_spec], [full_down]))
    dh4, dh4b, d_final, lossvec = _ffn_fwd2_loss(a1, wd1, h3, row(final_norm), target, tm)

    core = lax.axis_index("c").astype(jnp.int32).reshape(1)
    ts = min(1024, s)
    gd1 = _wgrad(a1, dh4b, tk_wide, s, "wgrad_down_l1")
    (dg1, dup1, dcw1, dcb1), (o_d1,) = _ffn_bwd1(dh4b, g1, sl1, ud1, wd1, cf_full[1], tm_wide, tn_fwd, "ffn_bwd1_l1",
                                                 comm=_PairExchange([gd1], [down_spec]))
    p_d1 = _pair_add(gd1, o_d1, down_spec, core, "pair_add_down_l1")
    gg1 = _wgrad(dg1, hn1, tk_wide, s, "wgrad_gate_l1")
    gu1 = _wgrad(dup1, hn1, tk_wide, s, "wgrad_up_l1")
    jobs = _Jobs(_ChipExchange([p_d1]), _PairExchange([gg1, gu1], [gu_spec, gu_spec]))
    (dh3, dh3b, d_nffn1), res = _ffn_bwd2(dg1, dup1, wg1, wu1, h3, norm_ffn[1:2], dh4, tm, "ffn_bwd2_l1", comm=jobs)
    (s_d1,), (o_g1, o_u1) = jobs.results(res)
    p_g1 = _pair_add(gg1, o_g1, gu_spec, core, "pair_add_gate_l1")
    p_u1 = _pair_add(gu1, o_u1, gu_spec, core, "pair_add_up_l1")
    (dpre1, dh2, dh2b, d_nmix1, d_sgu, d_ws, d_b), (s_g1, s_u1) = _odd_bwd(
        dh3b, dh3, wout_o, pre1, gate1, ws_b, wst_b, sgu_full, win_o, h2, norm_mix[1:2], tm, comm=_ChipExchange([p_g1, p_u1]))
    gi1 = _wgrad(xn1, dpre1, tk_wide, ts, "wgrad_in_odd")
    go1 = _wgrad(mixo1, dh3b, tk_wide, ts, "wgrad_out_odd")
    d_wsp = (d_ws * tril).reshape(-1, CHUNK).astype(BF16)
    jobs = _Jobs(_PairExchange([gi1, go1], [in_spec, out_spec]), _ChipExchange([], [], [d_wsp]))
    gd0, res = _wgrad(a0, dh2b, tk_wide, s, "wgrad_down_l0", comm=jobs)
    (o_i1, o_o1), (r_wsp,) = jobs.results(res)
    p_i1 = _pair_add(gi1, o_i1, in_spec, core, "pair_add_in_odd")
    p_o1 = _pair_add(go1, o_o1, out_spec, core, "pair_add_out_odd")
    d_early = _rep_early(d_nmix1, d_nffn1, d_final, d_b, _unpad_slots(dcb1, fs, fsp), lossvec)
    jobs = _Jobs(_ChipExchange([p_i1, p_o1], [], [d_early]), _PairExchange([gd0], [down_spec]))
    (dg0, dup0, dcw0, dcb0), res = _ffn_bwd1(dh2b, g0, sl0, ud0, wd0, cf_full[0], tm_wide, tn_fwd, "ffn_bwd1_l0", comm=jobs)
    (s_i1, s_o1, r_early), (o_d0,) = jobs.results(res)
    p_d0 = _pair_add(gd0, o_d0, down_spec, core, "pair_add_down_l0")
    gg0, (s_d0,) = _wgrad(dg0, hn0, tk_wide, s, "wgrad_gate_l0", comm=_ChipExchange([p_d0]))
    gu0, (o_g0,) = _wgrad(dup0, hn0, tk_wide, s, "wgrad_up_l0", comm=_PairExchange([gg0], [gu_spec]))
    p_g0 = _pair_add(gg0, o_g0, gu_spec, core, "pair_add_gate_l0")
    jobs = _Jobs(_ChipExchange([p_g0]), _PairExchange([gu0], [gu_spec]))
    (dh1, dh1b, d_nffn0), res = _ffn_bwd2(dg0, dup0, wg0, wu0, h1, norm_ffn[0:1], dh2, tm, "ffn_bwd2_l0", comm=jobs)
    (s_g0,), (o_u0,) = jobs.results(res)
    p_u0 = _pair_add(gu0, o_u0, gu_spec, core, "pair_add_up_l0")
    go0 = _wgrad(mix0, dh1b, tk_wide, ts, "wgrad_out_even")
    jobs = _Jobs(_ChipExchange([p_u0]), _PairExchange([go0], [out_spec]))
    (dproj0, grad_x, d_nmix0, d_ca, d_wp, d_ps), res = _even_bwd(
        dh1b, dh1, wout_e, proj0, cq0, pooled0, ca_full, wpool_b, wpoolt_b, pool_scale, win_e, x2, norm_mix[0:1], tm, comm=jobs)
    (s_u0,), (o_o0,) = jobs.results(res)
    p_o0 = _pair_add(go0, o_o0, out_spec, core, "pair_add_out_even")
    d_small = jnp.stack([_pack_small(a, b, c, fsp) for a, b, c in zip(
        jnp.moveaxis(d_ca.reshape(3, N_DEV, -1), 1, 0), jnp.moveaxis(d_sgu.reshape(1, N_DEV, -1), 1, 0),
        jnp.moveaxis(jnp.stack([dcw0, dcw1]).reshape(2, 3, N_DEV, fsp), 2, 0))])
    d_late = _rep_late(d_nmix0, d_nffn0, d_ps, _unpad_slots(dcb0, fs, fsp))
    d_wpl = d_wp.reshape(-1, CHUNK).astype(BF16)
    gi0, (s_o0, r_small, r_late, r_wpl) = _wgrad(xn0, dproj0, tk_wide, ts, "wgrad_in_even",
                                                 comm=_ChipExchange([p_o0], [d_small], [d_late, d_wpl]))
    loss = jnp.sum(r_early[:, r_early.shape[1] - 8, :])

    tp = lambda a: jnp.swapaxes(a, 1, 2)
    ffn = {"w_ffn_gate": (s_g1, s_g0, tp(w_ffn_gate), tp(m_w_ffn_gate), tp(v_w_ffn_gate), tp),
           "w_ffn_up": (s_u1, s_u0, tp(w_ffn_up), tp(m_w_ffn_up), tp(v_w_ffn_up), tp),
           "w_ffn_down": (s_d1, s_d0, w_ffn_down, m_w_ffn_down, v_w_ffn_down, lambda a: a)}
    first = {}
    first["w_ffn_gate"], (o_i0,) = _adamw(s_g1, *ffn["w_ffn_gate"][2:5], fs // 2, "adamw_w_ffn_gate_l1", layer=1,
                                          comm=_PairExchange([gi0], [in_spec]))
    p_i0 = _pair_add(gi0, o_i0, in_spec, core, "pair_add_in_even")
    first["w_ffn_up"], (s_i0,) = _adamw(s_u1, *ffn["w_ffn_up"][2:5], fs // 2, "adamw_w_ffn_up_l1", layer=1, comm=_ChipExchange([p_i0]))
    first["w_ffn_down"] = _adamw(s_d1, *ffn["w_ffn_down"][2:5], fs // 2, "adamw_w_ffn_down_l1", layer=1)
    out = {}
    out["w_in_even"], (grad_x,) = _adamw(s_i0, w_in_even[0], m_w_in_even[0], v_w_in_even[0], 256, "adamw_in_even", carried=[grad_x])
    out["w_out_even"] = _adamw(s_o0, w_out_even[0], m_w_out_even[0], v_w_out_even[0], ro // 2, "adamw_out_even")
    out["w_in_odd"] = _adamw(s_i1, w_in_odd[0], m_w_in_odd[0], v_w_in_odd[0], 256, "adamw_in_odd")
    out["w_out_odd"] = _adamw(s_o1, w_out_odd[0], m_w_out_odd[0], v_w_out_odd[0], ro // 2, "adamw_out_odd")
    for nm, (_, s0, w, m, v, back) in ffn.items():
        out[nm] = [back(a) for a in _adamw(s0, w, m, v, fs // 2, "adamw_%s_l0" % nm, layer=0, into=first[nm])]
    small = _adamw(r_small, _pack_small(conv_a[0], sgu_norm, conv_ffn, fsp), _pack_small(m_conv_a[0], m_sgu_norm, m_conv_ffn, fsp),
                   _pack_small(v_conv_a[0], v_sgu_norm, v_conv_ffn, fsp), SMALL_ROWS, "adamw_small")
    no_loss = jnp.zeros((1, REP_COLS), F32)
    early = _adamw(r_early, *[_rep_early(nm[1:2], nf[1:2], fn, bs, bc[1:2], no_loss) for nm, nf, fn, bs, bc in (
        (norm_mix, norm_ffn, final_norm, b_spatial, b_conv_ffn), (m_norm_mix, m_norm_ffn, m_final_norm, m_b_spatial, m_b_conv_ffn),
        (v_norm_mix, v_norm_ffn, v_final_norm, v_b_spatial, v_b_conv_ffn))], r_early.shape[1], "adamw_replicated_early")
    wsp = _adamw(r_wsp, w_spatial.reshape(-1, CHUNK), m_w_spatial.reshape(-1, CHUNK), v_w_spatial.reshape(-1, CHUNK),
                 r_wsp.shape[1], "adamw_w_spatial")
    late = _adamw(r_late, *[_rep_late(nm[0:1], nf[0:1], ps, bc[0:1]) for nm, nf, ps, bc in (
        (norm_mix, norm_ffn, pool_scale, b_conv_ffn), (m_norm_mix, m_norm_ffn, m_pool_scale, m_b_conv_ffn),
        (v_norm_mix, v_norm_ffn, v_pool_scale, v_b_conv_ffn))], r_late.shape[1], "adamw_replicated_late")
    wpl = _adamw(r_wpl, w_pool.reshape(-1, CHUNK), m_w_pool.reshape(-1, CHUNK), v_w_pool.reshape(-1, CHUNK),
                 r_wpl.shape[1], "adamw_w_pool")

    names = ["norm_mix", "norm_ffn", "final_norm", "w_in_even", "conv_a", "w_pool", "pool_scale", "w_out_even", "w_in_odd", "sgu_norm",
             "w_spatial", "b_spatial", "w_out_odd", "w_ffn_gate", "w_ffn_up", "conv_ffn", "b_conv_ffn", "w_ffn_down"]
    like = {"norm_mix": norm_mix, "norm_ffn": norm_ffn, "final_norm": final_norm, "w_in_even": w_in_even, "conv_a": conv_a,
            "w_pool": w_pool, "pool_scale": pool_scale, "w_out_even": w_out_even, "w_in_odd": w_in_odd, "sgu_norm": sgu_norm,
            "w_spatial": w_spatial, "b_spatial": b_spatial, "w_out_odd": w_out_odd, "w_ffn_gate": w_ffn_gate, "w_ffn_up": w_ffn_up,
            "conv_ffn": conv_ffn, "b_conv_ffn": b_conv_ffn, "w_ffn_down": w_ffn_down}
    groups = []
    for k in range(4):
        ca_k, sg_k, cf_k = _unpack_small(small[k], conv_a.shape[2], sgu_norm.shape[1], conv_ffn.shape[2])
        vals = dict(_unpack_rep(late[k], early[k], wsp[k], wpl[k], like))
        vals.update(conv_a=ca_k, sgu_norm=sg_k, conv_ffn=cf_k)
        for nm in ("w_in_even", "w_in_odd", "w_out_even", "w_out_odd", "w_ffn_gate", "w_ffn_up", "w_ffn_down"):
            vals[nm] = out[nm][k]
        groups.append([vals[nm].reshape(like[nm].shape) for nm in names])
    return (loss, grad_x[None], *groups[0], *groups[1], *groups[2], *groups[3])
```
